```python
import jax, jax.numpy as jnp
from jax import lax
import numpy as np

D_MODEL = 2048
BATCH = 8
SEQ = 2048
DEPTH = 2

MEM_LEN = 256
GLA_HEADS = 4
GLA_DK = 128
GLA_DV = 256
GLA_GATE_RANK = 16
GLA_TAU = 16.0
GLA_CHUNK = 64
NSA_HEADS = 16
NSA_GROUPS = 4
NSA_HPG = NSA_HEADS // NSA_GROUPS
NSA_DH = 64
CMP_LEN = 32
CMP_STRIDE = 16
CMP_HIDDEN = 256
SEL_LEN = 64
SEL_TOPN = 8
WINDOW = 512
Q_BLOCK = 128
SEL_Q_BLOCK = 64
XA_HEADS = 4
XA_DH = 128
N_EXPERTS = 16
N_GROUPS = 4
EXPERTS_PER_GROUP = N_EXPERTS // N_GROUPS
TOP_K = 2
D_FF = 1536
MOE_BLOCK = 256
DN_ALPHA = float((2 * DEPTH) ** 0.25)
DN_BETA = float((8 * DEPTH) ** -0.25)
LN_EPS = 1e-5
NEG = -1e30
FORCE_BONUS = 1e6

GLA_QK = GLA_HEADS * GLA_DK
GLA_V = GLA_HEADS * GLA_DV
NSA_Q = NSA_HEADS * NSA_DH
NSA_KV = NSA_GROUPS * NSA_DH
IN_SPLITS = (GLA_QK, GLA_QK, GLA_V, GLA_V, GLA_GATE_RANK, NSA_Q, 6 * NSA_KV, 3 * NSA_HEADS, 2 * D_MODEL)
D_IN = sum(IN_SPLITS)
SPLIT_IDX = tuple(int(v) for v in np.cumsum(IN_SPLITS)[:-1])

kernel_name = "hybrid_gla_nsa_moe_deepnorm"


def layer_norm(x, g, b):
    xf = x.astype(jnp.float32)
    mu = jnp.mean(xf, -1, keepdims=True)
    var = jnp.mean(jnp.square(xf - mu), -1, keepdims=True)
    return ((xf - mu) * lax.rsqrt(var + LN_EPS)).astype(x.dtype) * g + b


def alibi_slopes(n):
    return 2.0 ** (-8.0 * jnp.arange(1, n + 1, dtype=jnp.float32) / n)


def gla_mixer(q, k, v, r, a_low, w_a2, b_a, norm_g):
    B, T, _ = q.shape
    H, C = GLA_HEADS, GLA_CHUNK
    N = T // C
    f32 = jnp.float32
    log_a = jax.nn.log_sigmoid((a_low @ w_a2 + b_a).astype(f32)) / GLA_TAU

    def chunks(t, d):
        return t.astype(f32).reshape(B, N, C, H, d).transpose(1, 0, 3, 2, 4)

    qc = chunks(q, GLA_DK) * (GLA_DK ** -0.5)
    kc = chunks(k, GLA_DK)
    vc = chunks(v, GLA_DV)
    gc = chunks(log_a, GLA_DK)
    causal = jnp.tril(jnp.ones((C, C), dtype=bool))[:, :, None]

    def step(S, inp):
        qi, ki, vi, gi = inp
        b = jnp.cumsum(gi, axis=2)
        b_last = b[:, :, -1, :]
        o_inter = jnp.einsum('bhtd,bhdv->bhtv', qi * jnp.exp(b), S)
        rel = jnp.exp(jnp.where(causal, b[:, :, :, None, :] - b[:, :, None, :, :], -jnp.inf))
        att = jnp.einsum('bhtd,bhsd,bhtsd->bhts', qi, ki, rel)
        o_intra = jnp.einsum('bhts,bhsv->bhtv', att, vi)
        S = jnp.exp(b_last)[..., None] * S + jnp.einsum(
            'bhsd,bhsv->bhdv', ki * jnp.exp(b_last[:, :, None, :] - b), vi)
        return S, o_inter + o_intra

    S0 = jnp.zeros((B, H, GLA_DK, GLA_DV), f32)
    _, o = lax.scan(step, S0, (qc, kc, vc, gc))
    o = o.transpose(1, 0, 3, 2, 4).reshape(B, T, H, GLA_DV)
    mu = jnp.mean(o, -1, keepdims=True)
    var = jnp.mean(jnp.square(o - mu), -1, keepdims=True)
    o = (o - mu) * lax.rsqrt(var + LN_EPS) * norm_g.reshape(H, GLA_DV)
    return (o.reshape(B, T, GLA_V) * jax.nn.silu(r.astype(f32))).astype(q.dtype)


def nsa_mixer(q, kv, gate_logits, cmp_pe, cmp_w1, cmp_w2):
    B, T, _ = q.shape
    G, HPG, DH = NSA_GROUPS, NSA_HPG, NSA_DH
    f32 = jnp.float32
    q = q.astype(f32).reshape(B, T, G, HPG, DH).transpose(0, 2, 3, 1, 4) * (DH ** -0.5)
    kv = kv.astype(f32).reshape(B, T, 6, G, DH).transpose(2, 0, 3, 1, 4)
    k_c, v_c, k_s, v_s, k_w, v_w = kv[0], kv[1], kv[2], kv[3], kv[4], kv[5]
    slope = alibi_slopes(NSA_HEADS).reshape(G, HPG)[None, :, :, None, None]
    pos = jnp.arange(T)

    n_cmp = (T - CMP_LEN) // CMP_STRIDE + 1
    blk_start = jnp.arange(n_cmp) * CMP_STRIDE
    blk_idx = blk_start[:, None] + jnp.arange(CMP_LEN)[None, :]

    def compress(t, pe, w1, w2):
        blocks = t[:, :, blk_idx, :] + pe
        flat = blocks.reshape(B, G, n_cmp, CMP_LEN * DH)
        return jax.nn.gelu(flat @ w1) @ w2

    kc = compress(k_c, cmp_pe[0], cmp_w1[0], cmp_w2[0])
    vc = compress(v_c, cmp_pe[1], cmp_w1[1], cmp_w2[1])
    blk_end = blk_start + CMP_LEN - 1
    blk_center = blk_start.astype(f32) + 0.5 * (CMP_LEN - 1)
    s_c = jnp.einsum('bghtd,bgnd->bghtn', q, kc)
    s_c = s_c - slope * jnp.abs(pos[:, None].astype(f32) - blk_center[None, :])
    mask_c = blk_end[None, :] <= pos[:, None]
    p_c = jax.nn.softmax(jnp.where(mask_c, s_c, NEG), axis=-1) * mask_c
    o_cmp = jnp.einsum('bghtn,bgnd->bghtd', p_c, vc)

    n_sel = T // SEL_LEN
    sel_start = jnp.arange(n_sel) * SEL_LEN
    overlap = ((blk_start[:, None] < sel_start[None, :] + SEL_LEN)
               & (blk_start[:, None] + CMP_LEN > sel_start[None, :])).astype(f32)
    imp = jnp.einsum('bghtn,nj->bgtj', p_c, overlap)
    cur = pos // SEL_LEN
    jj = jnp.arange(n_sel)
    forced = (jj[None, :] == 0) | (jj[None, :] == cur[:, None]) | (jj[None, :] == cur[:, None] - 1)
    valid_blk = sel_start[None, :] <= pos[:, None]
    score = jnp.where(valid_blk, imp + jnp.where(forced, FORCE_BONUS, 0.0), NEG)
    top_n = min(SEL_TOPN, n_sel)
    top_val, sel_idx = lax.top_k(score, top_n)
    sel_ok = top_val > 0.5 * NEG
    ks_blk = k_s.reshape(B, G, n_sel, SEL_LEN, DH)
    vs_blk = v_s.reshape(B, G, n_sel, SEL_LEN, DH)
    b_ix = jnp.arange(B)[:, None, None, None]
    g_ix = jnp.arange(G)[None, :, None, None]

    def sel_block(i):
        t0 = i * SEL_Q_BLOCK
        qi = lax.dynamic_slice_in_dim(q, t0, SEL_Q_BLOCK, axis=3)
        idx_i = lax.dynamic_slice_in_dim(sel_idx, t0, SEL_Q_BLOCK, axis=2)
        ok_i = lax.dynamic_slice_in_dim(sel_ok, t0, SEL_Q_BLOCK, axis=2)
        kg = ks_blk[b_ix, g_ix, idx_i]
        vg = vs_blk[b_ix, g_ix, idx_i]
        tq = t0 + jnp.arange(SEL_Q_BLOCK)
        kpos = idx_i[..., None] * SEL_LEN + jnp.arange(SEL_LEN)
        dist = (tq[:, None, None] - kpos).astype(f32)
        s = jnp.einsum('bghqd,bgqnkd->bghqnk', qi, kg) - slope[..., None] * dist[:, :, None]
        mask = (ok_i[..., None] & (kpos <= tq[:, None, None]))[:, :, None]
        s = jnp.where(mask, s, NEG).reshape(B, G, HPG, SEL_Q_BLOCK, top_n * SEL_LEN)
        p = jax.nn.softmax(s, axis=-1)
        return jnp.einsum('bghqm,bgqmd->bghqd', p, vg.reshape(B, G, SEL_Q_BLOCK, top_n * SEL_LEN, DH))

    o_sel = lax.map(sel_block, jnp.arange(T // SEL_Q_BLOCK))
    o_sel = jnp.moveaxis(o_sel, 0, 3).reshape(B, G, HPG, T, DH)

    zpad = jnp.zeros((B, G, WINDOW, DH), f32)
    kw_pad = jnp.concatenate([zpad, k_w], axis=2)
    vw_pad = jnp.concatenate([zpad, v_w], axis=2)
    span = WINDOW + Q_BLOCK

    def win_block(i):
        t0 = i * Q_BLOCK
        qi = lax.dynamic_slice_in_dim(q, t0, Q_BLOCK, axis=3)
        kb = lax.dynamic_slice_in_dim(kw_pad, t0, span, axis=2)
        vb = lax.dynamic_slice_in_dim(vw_pad, t0, span, axis=2)
        tq = t0 + jnp.arange(Q_BLOCK)
        tk = t0 - WINDOW + jnp.arange(span)
        d = tq[:, None] - tk[None, :]
        mask = (d >= 0) & (d < WINDOW) & (tk[None, :] >= 0)
        s = jnp.einsum('bghqd,bgkd->bghqk', qi, kb) - slope * d.astype(f32)
        p = jax.nn.softmax(jnp.where(mask, s, NEG), axis=-1)
        return jnp.einsum('bghqk,bgkd->bghqd', p, vb)

    o_win = lax.map(win_block, jnp.arange(T // Q_BLOCK))
    o_win = jnp.moveaxis(o_win, 0, 3).reshape(B, G, HPG, T, DH)

    gt = jax.nn.sigmoid(gate_logits.astype(f32)).reshape(B, T, G, HPG, 3).transpose(0, 2, 3, 1, 4)
    o = gt[..., 0:1] * o_cmp + gt[..., 1:2] * o_sel + gt[..., 2:3] * o_win
    return o.transpose(0, 3, 1, 2, 4).reshape(B, T, NSA_Q).astype(gate_logits.dtype)


def memory_xattn(x, mem, wq, wkv, wo):
    B, T, _ = x.shape
    M = mem.shape[1]
    q = (x @ wq).reshape(B, T, XA_HEADS, XA_DH) * (XA_DH ** -0.5)
    kv = (mem @ wkv).reshape(B, M, 2, XA_HEADS, XA_DH)
    s = jnp.einsum('bthd,bmhd->bhtm', q, kv[:, :, 0]).astype(jnp.float32)
    p = jax.nn.softmax(s, axis=-1).astype(x.dtype)
    o = jnp.einsum('bhtm,bmhd->bthd', p, kv[:, :, 1]).reshape(B, T, XA_HEADS * XA_DH)
    return o @ wo


def moe_ffn(x, router_w, router_b, w_in, w_down):
    B, T, D = x.shape
    N = B * T
    xt = x.reshape(N, D)
    logits = (xt @ router_w).astype(jnp.float32)
    grp = (logits + router_b.astype(jnp.float32)).reshape(N, N_GROUPS, EXPERTS_PER_GROUP)
    grp_score = lax.top_k(grp, TOP_K)[0].sum(-1)
    g_best = jnp.argmax(grp_score, axis=-1)
    in_grp = grp[jnp.arange(N), g_best]
    _, local = lax.top_k(in_grp, TOP_K)
    expert = g_best[:, None] * EXPERTS_PER_GROUP + local
    gate = jax.nn.softmax(jnp.take_along_axis(logits, expert, axis=1), axis=-1)

    M = N * TOP_K
    e_flat = expert.reshape(M)
    tok = jnp.repeat(jnp.arange(N), TOP_K)
    order = jnp.argsort(e_flat)
    e_sorted = e_flat[order]
    tok_sorted = tok[order]
    w_sorted = gate.reshape(M)[order]
    counts = jnp.bincount(e_flat, length=N_EXPERTS)
    padded = (counts + MOE_BLOCK - 1) // MOE_BLOCK * MOE_BLOCK
    pad_end = jnp.cumsum(padded)
    pad_start = pad_end - padded
    raw_start = jnp.cumsum(counts) - counts
    dest = pad_start[e_sorted] + jnp.arange(M) - raw_start[e_sorted]
    n_blocks = -(-M // MOE_BLOCK) + N_EXPERTS
    P = n_blocks * MOE_BLOCK
    buf = jnp.zeros((P, D), x.dtype).at[dest].set(xt[tok_sorted])
    blk_expert = jnp.clip(jnp.searchsorted(pad_end, jnp.arange(n_blocks) * MOE_BLOCK, side='right'),
                          0, N_EXPERTS - 1)

    def expert_block(args):
        xb, e = args
        a, u = jnp.split(xb @ w_in[e], 2, axis=-1)
        return (jax.nn.silu(a) * u) @ w_down[e]

    y_buf = lax.map(expert_block, (buf.reshape(n_blocks, MOE_BLOCK, D), blk_expert)).reshape(P, D)
    y = y_buf[dest] * w_sorted[:, None].astype(x.dtype)
    return jnp.zeros((N, D), x.dtype).at[tok_sorted].add(y).reshape(B, T, D)


def setup_inputs(seed: int = 0) -> dict:
    key = jax.random.key(seed)
    ks = jax.random.split(key, 25)
    L, D, E = DEPTH, D_MODEL, N_EXPERTS
    f32 = jnp.float32

    def nrm(k, shape, fan_in, scale=1.0):
        return jax.random.normal(k, shape, f32) * (scale * fan_in ** -0.5)

    def gain(k, shape):
        return 1.0 + 0.02 * jax.random.normal(k, shape, f32)

    def small(k, shape, s=0.01):
        return s * jax.random.normal(k, shape, f32)

    return {
        "x": jax.random.normal(ks[0], (BATCH, SEQ, D), f32),
        "mem": jax.random.normal(ks[1], (BATCH, MEM_LEN, D), f32),
        "w_in": nrm(ks[2], (L, D, D_IN), D),
        "gla_w_a2": nrm(ks[3], (L, GLA_GATE_RANK, GLA_QK), GLA_GATE_RANK),
        "gla_b_a": small(ks[4], (L, GLA_QK), 0.1),
        "gla_norm_g": gain(ks[5], (L, GLA_V)),
        "nsa_cmp_pe": small(ks[6], (L, 2, CMP_LEN, NSA_DH), 0.02),
        "nsa_cmp_w1": nrm(ks[7], (L, 2, CMP_LEN * NSA_DH, CMP_HIDDEN), CMP_LEN * NSA_DH),
        "nsa_cmp_w2": nrm(ks[8], (L, 2, CMP_HIDDEN, NSA_DH), CMP_HIDDEN),
        "w_branch_gla": nrm(ks[9], (L, GLA_V, D), GLA_V),
        "w_branch_nsa": nrm(ks[10], (L, NSA_Q, D), NSA_Q),
        "w_out": nrm(ks[11], (L, D, D), D, DN_BETA),
        "ln_mix_g": gain(ks[12], (L, D)),
        "ln_mix_b": small(ks[13], (L, D)),
        "xa_wq": nrm(ks[14], (L, D, XA_HEADS * XA_DH), D),
        "xa_wkv": nrm(ks[15], (L, D, 2 * XA_HEADS * XA_DH), D),
        "xa_wo": nrm(ks[16], (L, XA_HEADS * XA_DH, D), XA_HEADS * XA_DH, DN_BETA),
        "ln_xa_g": gain(ks[17], (L, D)),
        "ln_xa_b": small(ks[18], (L, D)),
        "router_w": nrm(ks[19], (D, E), D),
        "router_b": small(ks[20], (E,)),
        "moe_w_in": nrm(ks[21], (L, E, D, 2 * D_FF), D),
        "moe_w_down": nrm(ks[22], (L, E, D_FF, D), D_FF, DN_BETA),
        "ln_ffn_g": gain(ks[23], (L, D)),
        "ln_ffn_b": small(ks[24], (L, D)),
    }


def reference(x, mem, w_in, gla_w_a2, gla_b_a, gla_norm_g, nsa_cmp_pe, nsa_cmp_w1, nsa_cmp_w2,
              w_branch_gla, w_branch_nsa, w_out, ln_mix_g, ln_mix_b, xa_wq, xa_wkv, xa_wo,
              ln_xa_g, ln_xa_b, router_w, router_b, moe_w_in, moe_w_down, ln_ffn_g, ln_ffn_b):
    D = D_MODEL
    for l in range(DEPTH):
        h = x @ w_in[l]
        g_q, g_k, g_v, g_r, g_a, n_q, n_kv, n_g, m_g = jnp.split(h, SPLIT_IDX, axis=-1)
        o_gla = gla_mixer(g_q, g_k, g_v, g_r, g_a, gla_w_a2[l], gla_b_a[l], gla_norm_g[l])
        o_nsa = nsa_mixer(n_q, n_kv, n_g, nsa_cmp_pe[l], nsa_cmp_w1[l], nsa_cmp_w2[l])
        gates = jax.nn.sigmoid(m_g)
        merged = gates[..., :D] * (o_gla @ w_branch_gla[l]) + gates[..., D:] * (o_nsa @ w_branch_nsa[l])
        x = layer_norm(DN_ALPHA * x + merged @ w_out[l], ln_mix_g[l], ln_mix_b[l])
        x = layer_norm(DN_ALPHA * x + memory_xattn(x, mem, xa_wq[l], xa_wkv[l], xa_wo[l]),
                       ln_xa_g[l], ln_xa_b[l])
        x = layer_norm(DN_ALPHA * x + moe_ffn(x, router_w, router_b, moe_w_in[l], moe_w_down[l]),
                       ln_ffn_g[l], ln_ffn_b[l])
    return x
```

```python
import functools

import jax
import jax.numpy as jnp
import numpy as np
from jax import lax
from jax.experimental import pallas as pl
from jax.experimental.pallas import tpu as pltpu

F32 = jnp.float32
BF16 = jnp.bfloat16
I32 = jnp.int32

DEPTH = 2
MEM_LEN = 256
GLA_HEADS = 4
GLA_DK = 128
GLA_DV = 256
GLA_GATE_RANK = 16
GLA_TAU = 16.0
GLA_CHUNK = 64
NSA_HEADS = 16
NSA_GROUPS = 4
NSA_HPG = NSA_HEADS // NSA_GROUPS
NSA_DH = 64
CMP_LEN = 32
CMP_STRIDE = 16
CMP_HIDDEN = 256
SEL_LEN = 64
SEL_TOPN = 8
WINDOW = 512
XA_HEADS = 4
XA_DH = 128
N_EXPERTS = 16
N_GROUPS = 4
EXPERTS_PER_GROUP = N_EXPERTS // N_GROUPS
TOP_K = 2
D_FF = 1536
DN_ALPHA = float((2 * DEPTH) ** 0.25)
LN_EPS = 1e-5
NEG = -1e30
FORCE_BONUS = 1e6

GLA_QK = GLA_HEADS * GLA_DK
GLA_V = GLA_HEADS * GLA_DV
NSA_Q = NSA_HEADS * NSA_DH
NSA_KV = NSA_GROUPS * NSA_DH

LANES = 128
VMEM_LIMIT = 56 * 1024 * 1024

COL_MG = 0
COL_GQ = 2 * 2048
COL_GK = COL_GQ + GLA_QK
COL_GV = COL_GK + GLA_QK
COL_GR = COL_GV + GLA_V
COL_NQ = COL_GR + GLA_V
COL_NKV = COL_NQ + NSA_Q
COL_END = COL_NKV + 6 * NSA_KV

MOE_BLOCK = 512
FF_TILE = 512


def _cp(sem):
    return pltpu.CompilerParams(dimension_semantics=sem, vmem_limit_bytes=VMEM_LIMIT)


def _dot(a, b):
    return jnp.dot(a, b, preferred_element_type=F32)


def _dot_nt(a, b):
    return lax.dot_general(a, b, (((1,), (1,)), ((), ())), preferred_element_type=F32)


def _dot_tn(a, b):
    return lax.dot_general(a, b, (((0,), (0,)), ((), ())), preferred_element_type=F32)


def _layer_norm(z, g, b):
    mu = jnp.mean(z, axis=-1, keepdims=True)
    zc = z - mu
    var = jnp.mean(zc * zc, axis=-1, keepdims=True)
    return zc * lax.rsqrt(var + LN_EPS) * g + b


def _mm_kernel(a_ref, b_ref, o_ref):
    o_ref[...] = _dot(a_ref[...], b_ref[...]).astype(o_ref.dtype)


def _matmul(a, b, out_dtype, tm, tn):
    m, k = a.shape
    n = b.shape[1]
    return pl.pallas_call(
        _mm_kernel,
        grid=(m // tm, n // tn),
        in_specs=[pl.BlockSpec((tm, k), lambda i, j: (i, 0)),
                  pl.BlockSpec((k, tn), lambda i, j: (0, j))],
        out_specs=pl.BlockSpec((tm, tn), lambda i, j: (i, j)),
        out_shape=jax.ShapeDtypeStruct((m, n), out_dtype),
        compiler_params=_cp(("parallel", "parallel")),
        name="matmul",
    )(a, b)


def _gla_kernel(q_ref, k_ref, v_ref, r_ref, sm_ref, wa_ref, ba_ref, ng_ref, o_ref, st_ref):
    C = GLA_CHUNK
    n_chunks = q_ref.shape[0] // C
    st_ref[...] = jnp.zeros_like(st_ref)
    rowi = lax.broadcasted_iota(I32, (C, GLA_DK), 0)
    tt = lax.broadcasted_iota(I32, (C, C), 0)
    ss = lax.broadcasted_iota(I32, (C, C), 1)
    levels = (1, 2, 4, 8, 16, 32)
    pair_masks = [((tt // (2 * L)) == (ss // (2 * L))) & ((tt & L) != 0) & ((ss & L) == 0) for L in levels]
    diag_mask = tt == ss
    scale = GLA_DK ** -0.5

    def chunk(c, carry):
        rows = pl.ds(pl.multiple_of(c * C, C), C)
        q = q_ref[rows, :].astype(F32) * scale
        k = k_ref[rows, :].astype(F32)
        v = v_ref[rows, :]
        z = _dot(sm_ref[rows, :].astype(BF16), wa_ref[...]) + ba_ref[...]
        g = (jnp.minimum(z, 0.0) - jnp.log1p(jnp.exp(-jnp.abs(z)))) * (1.0 / GLA_TAU)
        incl = g
        tot = g
        att = jnp.where(diag_mask, _dot_nt(q.astype(BF16), k.astype(BF16)), 0.0)
        for L, pm in zip(levels, pair_masks):
            ql = (q * jnp.exp(incl)).astype(BF16)
            kl = (k * jnp.exp(tot - incl)).astype(BF16)
            att = jnp.where(pm, _dot_nt(ql, kl), att)
            upper = (rowi & L) != 0
            from_lower = pltpu.roll(tot, L, 0)
            from_upper = pltpu.roll(tot, C - L, 0)
            incl = incl + jnp.where(upper, from_lower, 0.0)
            tot = tot + jnp.where(upper, from_lower, from_upper)
        qd = (q * jnp.exp(incl)).astype(BF16)
        kd = (k * jnp.exp(tot - incl)).astype(BF16)
        st = st_ref[...]
        o = _dot_nt(qd, st.astype(BF16)) + _dot(att.astype(BF16), v)
        st_ref[...] = st * jnp.exp(tot[0:1, :]) + _dot_tn(v, kd)
        mu = jnp.mean(o, axis=-1, keepdims=True)
        oc = o - mu
        var = jnp.mean(oc * oc, axis=-1, keepdims=True)
        on = oc * lax.rsqrt(var + LN_EPS) * ng_ref[...]
        r = r_ref[rows, :].astype(F32)
        o_ref[rows, :] = (on * (r * jax.nn.sigmoid(r))).astype(o_ref.dtype)
        return carry

    lax.fori_loop(0, n_chunks, chunk, 0)


def _gla(h_big, h_small, wa_pad, b_a, norm_g, B, T):
    n = B * T
    H = GLA_HEADS
    return pl.pallas_call(
        _gla_kernel,
        grid=(B, H),
        in_specs=[
            pl.BlockSpec((T, GLA_DK), lambda b, h: (b, COL_GQ // GLA_DK + h)),
            pl.BlockSpec((T, GLA_DK), lambda b, h: (b, COL_GK // GLA_DK + h)),
            pl.BlockSpec((T, GLA_DV), lambda b, h: (b, COL_GV // GLA_DV + h)),
            pl.BlockSpec((T, GLA_DV), lambda b, h: (b, COL_GR // GLA_DV + h)),
            pl.BlockSpec((T, LANES), lambda b, h: (b, 0)),
            pl.BlockSpec((LANES, GLA_DK), lambda b, h: (0, h)),
            pl.BlockSpec((1, GLA_DK), lambda b, h: (0, h)),
            pl.BlockSpec((1, GLA_DV), lambda b, h: (0, h)),
        ],
        out_specs=pl.BlockSpec((T, GLA_DV), lambda b, h: (b, h)),
        out_shape=jax.ShapeDtypeStruct((n, GLA_V), BF16),
        scratch_shapes=[pltpu.VMEM((GLA_DV, GLA_DK), F32)],
        compiler_params=_cp(("parallel", "parallel")),
        name="gla",
    )(h_big, h_big, h_big, h_big, h_small, wa_pad, b_a, norm_g)


def _compress_kernel(u_ref, w1_ref, w2_ref, w2t_ref, pe_ref, o_ref, ot_ref):
    half = CMP_STRIDE * NSA_DH
    u = u_ref[0, 0, 0]
    nrow = u.shape[0]
    a = _dot(u, w1_ref[0, 0:half, :])
    bm = _dot(u, w1_ref[0, half:2 * half, :])
    c = _dot(pe_ref[0], w1_ref[0])
    hid = a + pltpu.roll(bm, nrow - 1, 0) + c[0:1, :]
    act = jax.nn.gelu(hid).astype(BF16)
    o_ref[0, 0, 0] = _dot(act, w2_ref[0]).astype(o_ref.dtype)
    ot_ref[0, 0, 0] = _dot_nt(w2t_ref[0], act).astype(ot_ref.dtype)


def _compress(kv_t, w1, w2, w2t, pe8, B, T):
    nc = T // CMP_STRIDE
    u = kv_t.reshape(B, 6, NSA_GROUPS, nc, CMP_STRIDE * NSA_DH)
    return pl.pallas_call(
        _compress_kernel,
        grid=(B, 2, NSA_GROUPS),
        in_specs=[
            pl.BlockSpec((1, 1, 1, nc, CMP_STRIDE * NSA_DH), lambda b, s, g: (b, s, g, 0, 0)),
            pl.BlockSpec((1, CMP_LEN * NSA_DH, CMP_HIDDEN), lambda b, s, g: (s, 0, 0)),
            pl.BlockSpec((1, CMP_HIDDEN, NSA_DH), lambda b, s, g: (s, 0, 0)),
            pl.BlockSpec((1, NSA_DH, CMP_HIDDEN), lambda b, s, g: (s, 0, 0)),
            pl.BlockSpec((1, 16, CMP_LEN * NSA_DH), lambda b, s, g: (s, 0, 0)),
        ],
        out_specs=[
            pl.BlockSpec((1, 1, 1, nc, NSA_DH), lambda b, s, g: (b, s, g, 0, 0)),
            pl.BlockSpec((1, 1, 1, NSA_DH, nc), lambda b, s, g: (b, s, g, 0, 0)),
        ],
        out_shape=[jax.ShapeDtypeStruct((B, 2, NSA_GROUPS, nc, NSA_DH), BF16),
                   jax.ShapeDtypeStruct((B, 2, NSA_GROUPS, NSA_DH, nc), BF16)],
        compiler_params=_cp(("parallel", "parallel", "parallel")),
        name="nsa_compress",
    )(u, w1, w2, w2t, pe8)


def _cmp_select_kernel(slopes_ref, qt_ref, kc_ref, vct_ref, ovt_ref, ocmp_ref, mb_ref):
    g = pl.program_id(1)
    i = pl.program_id(2)
    tq = qt_ref.shape[3]
    nc = kc_ref.shape[3]
    ns = mb_ref.shape[2]
    t0 = i * tq
    tpos = (t0 + lax.broadcasted_iota(I32, (nc, tq), 1))
    nidx = lax.broadcasted_iota(I32, (nc, tq), 0)
    mask_c = (nidx * CMP_STRIDE + (CMP_LEN - 1)) <= tpos
    absd = jnp.abs(tpos.astype(F32) - (nidx.astype(F32) * CMP_STRIDE + 0.5 * (CMP_LEN - 1)))
    kc = kc_ref[0, 0, 0]
    vct = vct_ref[0, 0, 0]
    psum = jnp.zeros((nc, tq), F32)
    for hh in range(NSA_HPG):
        slope = slopes_ref[g * NSA_HPG + hh]
        q = qt_ref[0, hh] * jnp.asarray(NSA_DH ** -0.5, BF16)
        s = _dot(kc, q) - slope * absd
        s = jnp.where(mask_c, s, NEG)
        e = jnp.exp(s - jnp.max(s, axis=0, keepdims=True))
        p = jnp.where(mask_c, e / jnp.sum(e, axis=0, keepdims=True), 0.0)
        ocmp_ref[0, hh] = _dot(vct, p.astype(BF16)).astype(ocmp_ref.dtype)
        psum = psum + p
    p_hi = psum.astype(BF16)
    p_lo = (psum - p_hi.astype(F32)).astype(BF16)
    imp = _dot(ovt_ref[...], p_hi) + _dot(ovt_ref[...], p_lo)
    j = lax.broadcasted_iota(I32, (ns, tq), 0)
    tp = t0 + lax.broadcasted_iota(I32, (ns, tq), 1)
    cur = tp // SEL_LEN
    forced = (j == 0) | (j == cur) | (j == cur - 1)
    valid = j * SEL_LEN <= tp
    score = jnp.where(valid, imp + jnp.where(forced, FORCE_BONUS, 0.0), NEG)
    rank = jnp.zeros((ns, tq), F32)
    for jp in range(ns):
        row = score[jp:jp + 1, :]
        beats = (row > score) | ((row == score) & (j > jp))
        rank = rank + jnp.where(beats, 1.0, 0.0)
    keep = valid & (rank < float(min(SEL_TOPN, ns)))
    mb_ref[0, 0] = jnp.where(keep, 0.0, NEG).astype(mb_ref.dtype)


def _cmp_select(slopes, q_t, kcmp, kcmp_t, ovt, B, T, tq):
    nc = T // CMP_STRIDE
    ns = T // SEL_LEN
    grid_spec = pltpu.PrefetchScalarGridSpec(
        num_scalar_prefetch=1,
        grid=(B, NSA_GROUPS, T // tq),
        in_specs=[
            pl.BlockSpec((1, NSA_HPG, NSA_DH, tq), lambda b, g, i, s: (b, g, 0, i)),
            pl.BlockSpec((1, 1, 1, nc, NSA_DH), lambda b, g, i, s: (b, 0, g, 0, 0)),
            pl.BlockSpec((1, 1, 1, NSA_DH, nc), lambda b, g, i, s: (b, 1, g, 0, 0)),
            pl.BlockSpec((ns, nc), lambda b, g, i, s: (0, 0)),
        ],
        out_specs=[
            pl.BlockSpec((1, NSA_HPG, NSA_DH, tq), lambda b, g, i, s: (b, g, 0, i)),
            pl.BlockSpec((1, 1, ns, tq), lambda b, g, i, s: (b, g, 0, i)),
        ],
    )
    return pl.pallas_call(
        _cmp_select_kernel,
        grid_spec=grid_spec,
        out_shape=[jax.ShapeDtypeStruct((B, NSA_HEADS, NSA_DH, T), BF16),
                   jax.ShapeDtypeStruct((B, NSA_GROUPS, ns, T), BF16)],
        compiler_params=_cp(("parallel", "parallel", "parallel")),
        name="nsa_cmp_select",
    )(slopes, q_t, kcmp, kcmp_t, ovt)


def _sel_win_kernel(slopes_ref, qt_ref, ks_ref, vst_ref, kw_ref, vwt_ref, mb_ref, ocmp_ref, gt_ref,
                    o_ref, qaug_ref, m_ref, l_ref, acc_ref):
    g = pl.program_id(1)
    i = pl.program_id(2)
    tq = qt_ref.shape[3]
    tk = tq
    ns = mb_ref.shape[2]
    dh = NSA_DH
    t0 = i * tq
    for hh in range(NSA_HPG):
        cols = slice(hh * tq, (hh + 1) * tq)
        qaug_ref[0:dh, cols] = qt_ref[0, hh] * jnp.asarray(dh ** -0.5, BF16)
        qaug_ref[dh:dh + ns, cols] = mb_ref[0, 0]
        qaug_ref[dh + ns:, cols] = jnp.zeros((qaug_ref.shape[0] - dh - ns, tq), BF16)
    dist0 = (lax.broadcasted_iota(I32, (tk, tq), 1) - lax.broadcasted_iota(I32, (tk, tq), 0)).astype(F32)

    def reset():
        m_ref[...] = jnp.full(m_ref.shape, NEG, F32)
        l_ref[...] = jnp.zeros(l_ref.shape, F32)
        acc_ref[...] = jnp.zeros(acc_ref.shape, F32)

    def tile_step(k_tile, vt_tile, kd, s0, mode):
        dist = dist0 + (t0 - s0).astype(F32)
        for hh in range(NSA_HPG):
            slope = slopes_ref[g * NSA_HPG + hh]
            s = _dot(k_tile, qaug_ref[0:kd, hh * tq:(hh + 1) * tq]) - slope * dist
            if mode == "causal":
                s = jnp.where(dist >= 0.0, s, NEG)
            elif mode == "far":
                s = jnp.where(dist < float(WINDOW), s, NEG)
            m_old = m_ref[hh]
            m_new = jnp.maximum(m_old, jnp.max(s, axis=0, keepdims=True))
            alpha = jnp.exp(m_old - m_new)
            p = jnp.exp(s - m_new)
            l_ref[hh] = alpha * l_ref[hh] + jnp.sum(p, axis=0, keepdims=True)
            acc_ref[hh] = alpha * acc_ref[hh] + _dot(vt_tile, p.astype(BF16))
            m_ref[hh] = m_new

    def finish(gate_row):
        outs = []
        for hh in range(NSA_HPG):
            gate = jax.nn.sigmoid(gt_ref[0, 0, 3 * hh + gate_row:3 * hh + gate_row + 1, :])
            outs.append(gate * (acc_ref[hh] / l_ref[hh]))
        return outs

    reset()

    def sel_body(kb, carry):
        s0 = pl.multiple_of(kb * tk, tk)
        tile_step(ks_ref[0, 0, pl.ds(s0, tk), :], vst_ref[0, 0, :, pl.ds(s0, tk)], qaug_ref.shape[0], s0, "none")
        return carry

    lax.fori_loop(0, i, sel_body, 0)
    d0 = pl.multiple_of(t0, tk)
    tile_step(ks_ref[0, 0, pl.ds(d0, tk), :], vst_ref[0, 0, :, pl.ds(d0, tk)], qaug_ref.shape[0], d0, "causal")
    o_sel = finish(1)

    reset()
    n_back = WINDOW // tk

    @pl.when(i >= n_back)
    def _():
        s0 = pl.multiple_of((i - n_back) * tk, tk)
        tile_step(kw_ref[0, 0, pl.ds(s0, tk), :], vwt_ref[0, 0, :, pl.ds(s0, tk)], dh, s0, "far")

    for back in range(n_back - 1, 0, -1):
        @pl.when(i >= back)
        def _(back=back):
            s0 = pl.multiple_of((i - back) * tk, tk)
            tile_step(kw_ref[0, 0, pl.ds(s0, tk), :], vwt_ref[0, 0, :, pl.ds(s0, tk)], dh, s0, "none")

    tile_step(kw_ref[0, 0, pl.ds(d0, tk), :], vwt_ref[0, 0, :, pl.ds(d0, tk)], dh, d0, "causal")
    o_win = finish(2)

    for hh in range(NSA_HPG):
        g_cmp = jax.nn.sigmoid(gt_ref[0, 0, 3 * hh:3 * hh + 1, :])
        o = g_cmp * ocmp_ref[0, hh].astype(F32) + o_sel[hh] + o_win[hh]
        o_ref[0, hh] = o.astype(o_ref.dtype)


def _sel_win(slopes, q_t, ks_aug, vs_t, kw, vw_t, mb, ocmp_t, gates_t, B, T, tq):
    ns = T // SEL_LEN
    kaug = ks_aug.shape[-1]
    G = NSA_GROUPS
    grid_spec = pltpu.PrefetchScalarGridSpec(
        num_scalar_prefetch=1,
        grid=(B, G, T // tq),
        in_specs=[
            pl.BlockSpec((1, NSA_HPG, NSA_DH, tq), lambda b, g, i, s: (b, g, 0, i)),
            pl.BlockSpec((1, 1, T, kaug), lambda b, g, i, s: (b, g, 0, 0)),
            pl.BlockSpec((1, 1, NSA_DH, T), lambda b, g, i, s: (b, g, 0, 0)),
            pl.BlockSpec((1, 1, T, NSA_DH), lambda b, g, i, s: (b, g, 0, 0)),
            pl.BlockSpec((1, 1, NSA_DH, T), lambda b, g, i, s: (b, g, 0, 0)),
            pl.BlockSpec((1, 1, ns, tq), lambda b, g, i, s: (b, g, 0, i)),
            pl.BlockSpec((1, NSA_HPG, NSA_DH, tq), lambda b, g, i, s: (b, g, 0, i)),
            pl.BlockSpec((1, 1, 16, tq), lambda b, g, i, s: (b, g, 0, i)),
        ],
        out_specs=pl.BlockSpec((1, NSA_HPG, NSA_DH, tq), lambda b, g, i, s: (b, g, 0, i)),
        scratch_shapes=[
            pltpu.VMEM((kaug, NSA_HPG * tq), BF16),
            pltpu.VMEM((NSA_HPG, 1, tq), F32),
            pltpu.VMEM((NSA_HPG, 1, tq), F32),
            pltpu.VMEM((NSA_HPG, NSA_DH, tq), F32),
        ],
    )
    return pl.pallas_call(
        _sel_win_kernel,
        grid_spec=grid_spec,
        out_shape=jax.ShapeDtypeStruct((B, NSA_HEADS, NSA_DH, T), BF16),
        compiler_params=_cp(("parallel", "parallel", "parallel")),
        name="nsa_sel_win",
    )(slopes, q_t, ks_aug, vs_t, kw, vw_t, mb, ocmp_t, gates_t)


def _mix_kernel(og_ref, on_ref, mg1_ref, mg2_ref, x_ref, wg_ref, wn_ref, wo_ref, lg_ref, lb_ref,
                x1_ref, x1b_ref):
    g1 = _dot(og_ref[...], wg_ref[...])
    g2 = _dot(on_ref[...], wn_ref[...])
    merged = (jax.nn.sigmoid(mg1_ref[...].astype(F32)) * g1
              + jax.nn.sigmoid(mg2_ref[...].astype(F32)) * g2)
    y = _dot(merged.astype(BF16), wo_ref[...])
    x1 = _layer_norm(DN_ALPHA * x_ref[...] + y, lg_ref[...], lb_ref[...])
    x1_ref[...] = x1
    x1b_ref[...] = x1.astype(BF16)


def _const_spec(shape):
    nd = len(shape)
    return pl.BlockSpec(shape, lambda *_: (0,) * nd, pipeline_mode=pl.Buffered(1))


def _mix(o_gla, o_nsa, h_big, x, wg, wn, wo, lg, lb, tm):
    n, d = x.shape
    return pl.pallas_call(
        _mix_kernel,
        grid=(n // tm,),
        in_specs=[
            pl.BlockSpec((tm, GLA_V), lambda i: (i, 0)),
            pl.BlockSpec((tm, NSA_Q), lambda i: (i, 0)),
            pl.BlockSpec((tm, d), lambda i: (i, 0)),
            pl.BlockSpec((tm, d), lambda i: (i, 1)),
            pl.BlockSpec((tm, d), lambda i: (i, 0)),
            _const_spec(wg.shape), _const_spec(wn.shape), _const_spec(wo.shape),
            _const_spec(lg.shape), _const_spec(lb.shape),
        ],
        out_specs=[pl.BlockSpec((tm, d), lambda i: (i, 0)), pl.BlockSpec((tm, d), lambda i: (i, 0))],
        out_shape=[jax.ShapeDtypeStruct((n, d), F32), jax.ShapeDtypeStruct((n, d), BF16)],
        compiler_params=_cp(("parallel",)),
        name="mix_ln",
    )(o_gla, o_nsa, h_big, h_big, x, wg, wn, wo, lg, lb)


def _xattn_kernel(x_ref, xb_ref, kv_ref, wq_ref, wo_ref, lg_ref, lb_ref, x2_ref):
    hd = XA_HEADS * XA_DH
    q = (_dot(xb_ref[...], wq_ref[...]) * (XA_DH ** -0.5)).astype(BF16)
    outs = []
    for h in range(XA_HEADS):
        kh = kv_ref[0, :, h * XA_DH:(h + 1) * XA_DH]
        vh = kv_ref[0, :, hd + h * XA_DH:hd + (h + 1) * XA_DH]
        s = _dot_nt(q[:, h * XA_DH:(h + 1) * XA_DH], kh)
        e = jnp.exp(s - jnp.max(s, axis=-1, keepdims=True))
        p = e / jnp.sum(e, axis=-1, keepdims=True)
        outs.append(_dot(p.astype(BF16), vh).astype(BF16))
    o = jnp.concatenate(outs, axis=-1)
    y = _dot(o, wo_ref[...])
    x2_ref[...] = _layer_norm(DN_ALPHA * x_ref[...] + y, lg_ref[...], lb_ref[...])


def _xattn(x1, x1b, kv, wq, wo, lg, lb, B, T, tm):
    n, d = x1.shape
    nt = T // tm
    return pl.pallas_call(
        _xattn_kernel,
        grid=(B, nt),
        in_specs=[
            pl.BlockSpec((tm, d), lambda b, i: (b * nt + i, 0)),
            pl.BlockSpec((tm, d), lambda b, i: (b * nt + i, 0)),
            pl.BlockSpec((1,) + kv.shape[1:], lambda b, i: (b, 0, 0)),
            _const_spec(wq.shape), _const_spec(wo.shape), _const_spec(lg.shape), _const_spec(lb.shape),
        ],
        out_specs=pl.BlockSpec((tm, d), lambda b, i: (b * nt + i, 0)),
        out_shape=jax.ShapeDtypeStruct((n, d), F32),
        compiler_params=_cp(("parallel", "parallel")),
        name="xattn_ln",
    )(x1, x1b, kv, wq, wo, lg, lb)


def _router_kernel(x_ref, wh_ref, wl_ref, rb_ref, e_ref, gate_ref, rank_ref, cnt_ref, carry_ref):
    i = pl.program_id(0)
    tr = x_ref.shape[0]
    E = N_EXPERTS

    @pl.when(i == 0)
    def _():
        carry_ref[...] = jnp.zeros_like(carry_ref)

    x = x_ref[...]
    x_hi = x.astype(BF16)
    x_lo = (x - x_hi.astype(F32)).astype(BF16)
    wh = wh_ref[...]
    logits = _dot_nt(wh, x_hi) + _dot_nt(wh, x_lo) + _dot_nt(wl_ref[...], x_hi)
    biased = logits + rb_ref[...]
    rows = [biased[e:e + 1, :] for e in range(E)]
    raw = [logits[e:e + 1, :] for e in range(E)]
    best_score = None
    best = None
    for gi in range(N_GROUPS):
        v = rows[gi * EXPERTS_PER_GROUP:(gi + 1) * EXPERTS_PER_GROUP]
        sc = None
        for a in range(EXPERTS_PER_GROUP):
            for b in range(a + 1, EXPERTS_PER_GROUP):
                pair = v[a] + v[b]
                sc = pair if sc is None else jnp.maximum(sc, pair)
        if best is None:
            best_score, best = sc, jnp.zeros((1, tr), I32)
        else:
            better = sc > best_score
            best_score = jnp.where(better, sc, best_score)
            best = jnp.where(better, gi, best)

    def pick(vals):
        out = vals[0:EXPERTS_PER_GROUP]
        for gi in range(1, N_GROUPS):
            out = [jnp.where(best == gi, vals[gi * EXPERTS_PER_GROUP + a], out[a]) for a in range(EXPERTS_PER_GROUP)]
        return out

    w = pick(rows)
    lraw = pick(raw)
    i1 = jnp.zeros((1, tr), I32)
    v1 = w[0]
    l1 = lraw[0]
    for a in range(1, EXPERTS_PER_GROUP):
        better = w[a] > v1
        v1 = jnp.where(better, w[a], v1)
        l1 = jnp.where(better, lraw[a], l1)
        i1 = jnp.where(better, a, i1)
    i2 = jnp.full((1, tr), -1, I32)
    v2 = jnp.full((1, tr), -jnp.inf, F32)
    l2 = jnp.zeros((1, tr), F32)
    for a in range(EXPERTS_PER_GROUP):
        better = (i1 != a) & ((w[a] > v2) | (i2 < 0))
        v2 = jnp.where(better, w[a], v2)
        l2 = jnp.where(better, lraw[a], l2)
        i2 = jnp.where(better, a, i2)
    e1 = best * EXPERTS_PER_GROUP + i1
    e2 = best * EXPERTS_PER_GROUP + i2
    mx = jnp.maximum(l1, l2)
    p1 = jnp.exp(l1 - mx)
    p2 = jnp.exp(l2 - mx)
    den = p1 + p2
    e_ref[0:1, :] = e1
    e_ref[1:2, :] = e2
    gate_ref[0:1, :] = p1 / den
    gate_ref[1:2, :] = p2 / den
    eidx = lax.broadcasted_iota(I32, (E, tr), 0)
    is1 = eidx == e1
    is2 = eidx == e2
    member = jnp.where(is1 | is2, 1.0, 0.0)
    uu = lax.broadcasted_iota(I32, (tr, tr), 0)
    tt = lax.broadcasted_iota(I32, (tr, tr), 1)
    tri = jnp.where(uu <= tt, 1.0, 0.0).astype(BF16)
    incl = _dot(member.astype(BF16), tri)
    excl = carry_ref[:, 0:1] + incl - member
    rank_ref[0:1, :] = jnp.sum(jnp.where(is1, excl, 0.0), axis=0, keepdims=True).astype(I32)
    rank_ref[1:2, :] = jnp.sum(jnp.where(is2, excl, 0.0), axis=0, keepdims=True).astype(I32)
    new_carry = carry_ref[...] + jnp.sum(member, axis=1, keepdims=True)
    carry_ref[...] = new_carry
    cnt_ref[...] = new_carry


def _router(x2, rw_hi, rw_lo, rb, tr):
    n, d = x2.shape
    E = N_EXPERTS
    return pl.pallas_call(
        _router_kernel,
        grid=(n // tr,),
        in_specs=[
            pl.BlockSpec((tr, d), lambda i: (i, 0)),
            pl.BlockSpec((E, d), lambda i: (0, 0)),
            pl.BlockSpec((E, d), lambda i: (0, 0)),
            pl.BlockSpec((E, 1), lambda i: (0, 0)),
        ],
        out_specs=[
            pl.BlockSpec((2, tr), lambda i: (0, i)),
            pl.BlockSpec((2, tr), lambda i: (0, i)),
            pl.BlockSpec((2, tr), lambda i: (0, i)),
            pl.BlockSpec((E, LANES), lambda i: (0, 0)),
        ],
        out_shape=[jax.ShapeDtypeStruct((2, n), I32), jax.ShapeDtypeStruct((2, n), F32),
                   jax.ShapeDtypeStruct((2, n), I32), jax.ShapeDtypeStruct((E, LANES), F32)],
        scratch_shapes=[pltpu.VMEM((E, LANES), F32)],
        compiler_params=_cp(("arbitrary",)),
        name="moe_router",
    )(x2, rw_hi, rw_lo, rb)


def _dispatch_kernel(ps_ref, e_ref, rank_ref, x_hbm, buf_in, buf_hbm, sem):
    del buf_in
    i = pl.program_id(0)
    td = e_ref.shape[1]

    def row_copy(t, kk):
        dest = ps_ref[e_ref[kk, t]] + rank_ref[kk, t]
        return pltpu.make_async_copy(x_hbm.at[pl.ds(i * td + t, 1), :], buf_hbm.at[pl.ds(dest, 1), :], sem)

    def issue(t, carry):
        row_copy(t, 0).start()
        row_copy(t, 1).start()
        return carry

    def drain(t, carry):
        row_copy(t, 0).wait()
        row_copy(t, 1).wait()
        return carry

    lax.fori_loop(0, td, issue, 0)
    lax.fori_loop(0, td, drain, 0)


def _dispatch(pad_start, e, rank, x2, buf0, td):
    n, d = x2.shape
    grid_spec = pltpu.PrefetchScalarGridSpec(
        num_scalar_prefetch=1,
        grid=(n // td,),
        in_specs=[
            pl.BlockSpec((2, td), lambda i, s: (0, i), memory_space=pltpu.SMEM),
            pl.BlockSpec((2, td), lambda i, s: (0, i), memory_space=pltpu.SMEM),
            pl.BlockSpec(memory_space=pl.ANY),
            pl.BlockSpec(memory_space=pl.ANY),
        ],
        out_specs=pl.BlockSpec(memory_space=pl.ANY),
        scratch_shapes=[pltpu.SemaphoreType.DMA(())],
    )
    return pl.pallas_call(
        _dispatch_kernel,
        grid_spec=grid_spec,
        out_shape=jax.ShapeDtypeStruct(buf0.shape, buf0.dtype),
        input_output_aliases={4: 0},
        compiler_params=_cp(("arbitrary",)),
        name="moe_dispatch",
    )(pad_start, e, rank, x2, buf0)


def _expert_kernel(be_ref, nb_ref, x_ref, wa_ref, wu_ref, wd_ref, y_ref, xb_ref):
    blk = pl.program_id(0)
    f = pl.program_id(1)

    @pl.when((blk >= nb_ref[0]) & (f == 0))
    def _():
        y_ref[...] = jnp.zeros_like(y_ref)

    @pl.when(blk < nb_ref[0])
    def _():
        @pl.when(f == 0)
        def _():
            xb_ref[...] = x_ref[...].astype(BF16)

        xb = xb_ref[...]
        a = _dot(xb, wa_ref[0])
        u = _dot(xb, wu_ref[0])
        act = (a * jax.nn.sigmoid(a) * u).astype(BF16)
        y = _dot(act, wd_ref[0])

        @pl.when(f == 0)
        def _():
            y_ref[...] = y

        @pl.when(f != 0)
        def _():
            y_ref[...] += y


def _experts(blk_expert, n_used, buf, w_in, w_down):
    p, d = buf.shape
    nb = p // MOE_BLOCK
    nf = D_FF // FF_TILE
    grid_spec = pltpu.PrefetchScalarGridSpec(
        num_scalar_prefetch=2,
        grid=(nb, nf),
        in_specs=[
            pl.BlockSpec((MOE_BLOCK, d), lambda b, f, be, nu: (b, 0)),
            pl.BlockSpec((1, d, FF_TILE), lambda b, f, be, nu: (be[b], 0, f)),
            pl.BlockSpec((1, d, FF_TILE), lambda b, f, be, nu: (be[b], 0, nf + f)),
            pl.BlockSpec((1, FF_TILE, d), lambda b, f, be, nu: (be[b], f, 0)),
        ],
        out_specs=pl.BlockSpec((MOE_BLOCK, d), lambda b, f, be, nu: (b, 0)),
        scratch_shapes=[pltpu.VMEM((MOE_BLOCK, d), BF16)],
    )
    return pl.pallas_call(
        _expert_kernel,
        grid_spec=grid_spec,
        out_shape=jax.ShapeDtypeStruct((p, d), F32),
        compiler_params=_cp(("arbitrary", "arbitrary")),
        name="moe_experts",
    )(blk_expert, n_used, buf, w_in, w_in, w_down)


def _combine_kernel(ps_ref, e_ref, rank_ref, y_hbm, x_ref, gate_ref, lg_ref, lb_ref, x3_ref, x3b_ref,
                    y0_ref, y1_ref, sem):
    tc = x_ref.shape[0]
    bufs = (y0_ref, y1_ref)

    def row_copy(t, kk):
        src = ps_ref[e_ref[kk, t]] + rank_ref[kk, t]
        return pltpu.make_async_copy(y_hbm.at[pl.ds(src, 1), :], bufs[kk].at[pl.ds(t, 1), :], sem)

    def issue(t, carry):
        row_copy(t, 0).start()
        row_copy(t, 1).start()
        return carry

    def drain(t, carry):
        row_copy(t, 0).wait()
        row_copy(t, 1).wait()
        return carry

    lax.fori_loop(0, tc, issue, 0)
    lax.fori_loop(0, tc, drain, 0)
    gate = gate_ref[...]
    z = DN_ALPHA * x_ref[...] + gate[:, 0:1] * y0_ref[...] + gate[:, 1:2] * y1_ref[...]
    x3 = _layer_norm(z, lg_ref[...], lb_ref[...])
    x3_ref[...] = x3
    x3b_ref[...] = x3.astype(BF16)


def _combine(pad_start, e, rank, y, x2, gate_nt, lg, lb, tc):
    n, d = x2.shape
    grid_spec = pltpu.PrefetchScalarGridSpec(
        num_scalar_prefetch=1,
        grid=(n // tc,),
        in_specs=[
            pl.BlockSpec((2, tc), lambda i, s: (0, i), memory_space=pltpu.SMEM),
            pl.BlockSpec((2, tc), lambda i, s: (0, i), memory_space=pltpu.SMEM),
            pl.BlockSpec(memory_space=pl.ANY),
            pl.BlockSpec((tc, d), lambda i, s: (i, 0)),
            pl.BlockSpec((tc, 2), lambda i, s: (i, 0)),
            pl.BlockSpec((1, d), lambda i, s: (0, 0)),
            pl.BlockSpec((1, d), lambda i, s: (0, 0)),
        ],
        out_specs=[pl.BlockSpec((tc, d), lambda i, s: (i, 0)), pl.BlockSpec((tc, d), lambda i, s: (i, 0))],
        scratch_shapes=[pltpu.VMEM((tc, d), F32), pltpu.VMEM((tc, d), F32), pltpu.SemaphoreType.DMA(())],
    )
    return pl.pallas_call(
        _combine_kernel,
        grid_spec=grid_spec,
        out_shape=[jax.ShapeDtypeStruct((n, d), F32), jax.ShapeDtypeStruct((n, d), BF16)],
        compiler_params=_cp(("arbitrary",)),
        name="moe_combine_ln",
    )(pad_start, e, rank, y, x2, gate_nt, lg, lb)


def _layer(x, xb, mem_b, p, consts, B, T):
    n, d = x.shape
    G, HPG, DH = NSA_GROUPS, NSA_HPG, NSA_DH
    slopes, ovt, sel_onehot = consts

    h_big = _matmul(xb, p["w_big"], BF16, 1024, 512)
    h_small = _matmul(xb, p["w_small"], F32, 1024, LANES)

    o_gla = _gla(h_big, h_small, p["wa_pad"], p["b_a"], p["norm_g"], B, T)

    q_t = h_big[:, COL_NQ:COL_NQ + NSA_Q].reshape(B, T, NSA_HEADS, DH).transpose(0, 2, 3, 1)
    kv = h_big[:, COL_NKV:COL_END].reshape(B, T, 6, G, DH)
    kv_rows = kv.transpose(0, 2, 3, 1, 4)
    kv_cols = kv.transpose(0, 2, 3, 4, 1)
    kcmp, kcmp_t = _compress(kv_rows, p["cmp_w1"], p["cmp_w2"], p["cmp_w2t"], p["cmp_pe8"], B, T)
    ocmp_t, mb = _cmp_select(slopes, q_t, kcmp, kcmp_t, ovt, B, T, 512)
    pad = jnp.zeros((B, G, T, LANES - DH - sel_onehot.shape[1]), BF16)
    ks_aug = jnp.concatenate([kv_rows[:, 2], jnp.broadcast_to(sel_onehot, (B, G) + sel_onehot.shape), pad], axis=-1)
    gates_t = h_small[:, GLA_GATE_RANK:GLA_GATE_RANK + 3 * NSA_HEADS].reshape(B, T, G, 3 * HPG)
    gates_t = jnp.pad(gates_t.transpose(0, 2, 3, 1), ((0, 0), (0, 0), (0, 16 - 3 * HPG), (0, 0)))
    o_nsa_t = _sel_win(slopes, q_t, ks_aug, kv_cols[:, 3], kv_rows[:, 4], kv_cols[:, 5], mb, ocmp_t, gates_t,
                       B, T, 256)
    o_nsa = o_nsa_t.transpose(0, 3, 1, 2).reshape(n, NSA_Q)

    x1, x1b = _mix(o_gla, o_nsa, h_big, x, p["w_bg"], p["w_bn"], p["w_out"], p["ln_mix_g"], p["ln_mix_b"], 256)

    kvm = _matmul(mem_b, p["xa_wkv"], BF16, 512, 512).reshape(B, MEM_LEN, 2 * XA_HEADS * XA_DH)
    x2 = _xattn(x1, x1b, kvm, p["xa_wq"], p["xa_wo"], p["ln_xa_g"], p["ln_xa_b"], B, T, 256)

    e, gate, rank, cnt = _router(x2, p["rw_hi"], p["rw_lo"], p["rb"], 512)
    counts = cnt[:, 0].astype(I32)
    padded = (counts + MOE_BLOCK - 1) // MOE_BLOCK * MOE_BLOCK
    pad_end = jnp.cumsum(padded)
    pad_start = (pad_end - padded).astype(I32)
    nb = (n * TOP_K) // MOE_BLOCK + N_EXPERTS
    n_used = (pad_end[-1] // MOE_BLOCK).astype(I32).reshape(1)
    blk_start = jnp.arange(nb, dtype=I32) * MOE_BLOCK
    blk_expert = jnp.minimum(jnp.sum(blk_start[:, None] >= pad_end[None, :], axis=1), N_EXPERTS - 1).astype(I32)
    blk_expert = jnp.where(jnp.arange(nb) < n_used[0], blk_expert, blk_expert[jnp.maximum(n_used[0] - 1, 0)])
    buf = _dispatch(pad_start, e, rank, x2, jnp.zeros((nb * MOE_BLOCK, d), F32), 512)
    y = _experts(blk_expert, n_used, buf, p["moe_w_in"], p["moe_w_down"])
    x3, x3b = _combine(pad_start, e, rank, y, x2, gate.T, p["ln_ffn_g"], p["ln_ffn_b"], 256)
    return x3, x3b


def _prep_layer(l, w_in, gla_w_a2, gla_b_a, gla_norm_g, nsa_cmp_pe, nsa_cmp_w1, nsa_cmp_w2, w_branch_gla,
                w_branch_nsa, w_out, ln_mix_g, ln_mix_b, xa_wq, xa_wkv, xa_wo, ln_xa_g, ln_xa_b, router_w,
                router_b, moe_w_in, moe_w_down, ln_ffn_g, ln_ffn_b):
    d = w_in.shape[1]
    w = w_in[l]
    o_gq, o_gk, o_gv, o_gr = 0, GLA_QK, 2 * GLA_QK, 2 * GLA_QK + GLA_V
    o_ga = o_gr + GLA_V
    o_nq = o_ga + GLA_GATE_RANK
    o_nkv = o_nq + NSA_Q
    o_ng = o_nkv + 6 * NSA_KV
    o_mg = o_ng + 3 * NSA_HEADS
    w_big = jnp.concatenate([w[:, o_mg:o_mg + 2 * d], w[:, o_gq:o_ga], w[:, o_nq:o_ng]], axis=1).astype(BF16)
    w_small = jnp.concatenate([w[:, o_ga:o_nq], w[:, o_ng:o_mg],
                               jnp.zeros((d, LANES - GLA_GATE_RANK - 3 * NSA_HEADS), F32)], axis=1).astype(BF16)
    wa_pad = jnp.concatenate([gla_w_a2[l], jnp.zeros((LANES - GLA_GATE_RANK, GLA_QK), F32)], axis=0).astype(BF16)
    rw_t = router_w.T
    rw_hi = rw_t.astype(BF16)
    rw_lo = (rw_t - rw_hi.astype(F32)).astype(BF16)
    return dict(
        w_big=w_big, w_small=w_small, wa_pad=wa_pad,
        b_a=gla_b_a[l].reshape(1, -1), norm_g=gla_norm_g[l].reshape(1, -1),
        cmp_w1=nsa_cmp_w1[l].astype(BF16), cmp_w2=nsa_cmp_w2[l].astype(BF16),
        cmp_w2t=nsa_cmp_w2[l].transpose(0, 2, 1).astype(BF16),
        cmp_pe8=jnp.broadcast_to(nsa_cmp_pe[l].reshape(2, 1, CMP_LEN * NSA_DH), (2, 16, CMP_LEN * NSA_DH)).astype(BF16),
        w_bg=w_branch_gla[l].astype(BF16), w_bn=w_branch_nsa[l].astype(BF16), w_out=w_out[l].astype(BF16),
        ln_mix_g=ln_mix_g[l].reshape(1, -1), ln_mix_b=ln_mix_b[l].reshape(1, -1),
        xa_wq=xa_wq[l].astype(BF16), xa_wkv=xa_wkv[l].astype(BF16), xa_wo=xa_wo[l].astype(BF16),
        ln_xa_g=ln_xa_g[l].reshape(1, -1), ln_xa_b=ln_xa_b[l].reshape(1, -1),
        rw_hi=rw_hi, rw_lo=rw_lo, rb=router_b.reshape(-1, 1),
        moe_w_in=moe_w_in[l].astype(BF16), moe_w_down=moe_w_down[l].astype(BF16),
        ln_ffn_g=ln_ffn_g[l].reshape(1, -1), ln_ffn_b=ln_ffn_b[l].reshape(1, -1),
    )


def kernel(x, mem, w_in, gla_w_a2, gla_b_a, gla_norm_g, nsa_cmp_pe, nsa_cmp_w1, nsa_cmp_w2, w_branch_gla, w_branch_nsa, w_out, ln_mix_g, ln_mix_b, xa_wq, xa_wkv, xa_wo, ln_xa_g, ln_xa_b, router_w, router_b, moe_w_in, moe_w_down, ln_ffn_g, ln_ffn_b):
    B, T, d = x.shape
    assert T % 512 == 0 and d == 2048 and mem.shape[1] == MEM_LEN
    n = B * T
    params = (w_in, gla_w_a2, gla_b_a, gla_norm_g, nsa_cmp_pe, nsa_cmp_w1, nsa_cmp_w2, w_branch_gla, w_branch_nsa,
              w_out, ln_mix_g, ln_mix_b, xa_wq, xa_wkv, xa_wo, ln_xa_g, ln_xa_b, router_w, router_b, moe_w_in,
              moe_w_down, ln_ffn_g, ln_ffn_b)
    slopes = (2.0 ** (-8.0 * jnp.arange(1, NSA_HEADS + 1, dtype=F32) / NSA_HEADS)).astype(F32)
    nc, ns = T // CMP_STRIDE, T // SEL_LEN
    cs = np.arange(nc) * CMP_STRIDE
    ss = np.arange(ns) * SEL_LEN
    ovt = ((cs[None, :] < ss[:, None] + SEL_LEN) & (cs[None, :] + CMP_LEN > ss[:, None])
           & (cs[None, :] + CMP_LEN <= T)).astype(np.float32)
    sel_onehot = (np.arange(T)[:, None] // SEL_LEN == np.arange(ns)[None, :]).astype(np.float32)
    consts = (slopes, jnp.asarray(ovt, BF16), jnp.asarray(sel_onehot, BF16))

    xf = x.reshape(n, d)
    xb = xf.astype(BF16)
    mem_b = mem.reshape(B * MEM_LEN, d).astype(BF16)
    for l in range(DEPTH):
        p = _prep_layer(l, *params)
        xf, xb = _layer(xf, xb, mem_b, p, consts, B, T)
    return xf.reshape(B, T, d)
```

```python
import functools

import jax
import jax.numpy as jnp
import numpy as np
from jax import lax
from jax.experimental import pallas as pl
from jax.experimental.pallas import tpu as pltpu

F32 = jnp.float32
BF16 = jnp.bfloat16
I32 = jnp.int32

DEPTH = 2
MEM_LEN = 256
GLA_HEADS = 4
GLA_DK = 128
GLA_DV = 256
GLA_GATE_RANK = 16
GLA_TAU = 16.0
GLA_CHUNK = 64
NSA_HEADS = 16
NSA_GROUPS = 4
NSA_HPG = NSA_HEADS // NSA_GROUPS
NSA_DH = 64
CMP_LEN = 32
CMP_STRIDE = 16
CMP_HIDDEN = 256
SEL_LEN = 64
SEL_TOPN = 8
WINDOW = 512
XA_HEADS = 4
XA_DH = 128
N_EXPERTS = 16
N_GROUPS = 4
EXPERTS_PER_GROUP = N_EXPERTS // N_GROUPS
TOP_K = 2
D_FF = 1536
DN_ALPHA = float((2 * DEPTH) ** 0.25)
LN_EPS = 1e-5
NEG = -1e30
FORCE_BONUS = 1e6

GLA_QK = GLA_HEADS * GLA_DK
GLA_V = GLA_HEADS * GLA_DV
NSA_Q = NSA_HEADS * NSA_DH
NSA_KV = NSA_GROUPS * NSA_DH

LANES = 128
VMEM_LIMIT = 56 * 1024 * 1024

COL_MG = 0
COL_GQ = 2 * 2048
COL_GK = COL_GQ + GLA_QK
COL_GV = COL_GK + GLA_QK
COL_GR = COL_GV + GLA_V
COL_NQ = COL_GR + GLA_V
COL_NKV = COL_NQ + NSA_Q
COL_END = COL_NKV + 6 * NSA_KV

MOE_BLOCK = 512
FF_TILE = 512


def _cp(sem):
    return pltpu.CompilerParams(dimension_semantics=sem, vmem_limit_bytes=VMEM_LIMIT)


def _dot(a, b):
    return jnp.dot(a, b, preferred_element_type=F32)


def _dot_nt(a, b):
    return lax.dot_general(a, b, (((1,), (1,)), ((), ())), preferred_element_type=F32)


def _dot_tn(a, b):
    return lax.dot_general(a, b, (((0,), (0,)), ((), ())), preferred_element_type=F32)


def _layer_norm(z, g, b):
    mu = jnp.mean(z, axis=-1, keepdims=True)
    zc = z - mu
    var = jnp.mean(zc * zc, axis=-1, keepdims=True)
    return zc * lax.rsqrt(var + LN_EPS) * g + b


def _mm_kernel(a_ref, b_ref, o_ref):
    o_ref[...] = _dot(a_ref[...], b_ref[...]).astype(o_ref.dtype)


def _matmul(a, b, out_dtype, tm, tn):
    m, k = a.shape
    n = b.shape[1]
    return pl.pallas_call(
        _mm_kernel,
        grid=(m // tm, n // tn),
        in_specs=[pl.BlockSpec((tm, k), lambda i, j: (i, 0)),
                  pl.BlockSpec((k, tn), lambda i, j: (0, j))],
        out_specs=pl.BlockSpec((tm, tn), lambda i, j: (i, j)),
        out_shape=jax.ShapeDtypeStruct((m, n), out_dtype),
        compiler_params=_cp(("parallel", "parallel")),
        name="matmul",
    )(a, b)


def _gla_kernel(q_ref, k_ref, v_ref, r_ref, sm_ref, wa_ref, ba_ref, ng_ref, o_ref, st_ref):
    C = GLA_CHUNK
    n_chunks = q_ref.shape[0] // C
    st_ref[...] = jnp.zeros_like(st_ref)
    rowi = lax.broadcasted_iota(I32, (C, GLA_DK), 0)
    tt = lax.broadcasted_iota(I32, (C, C), 0)
    ss = lax.broadcasted_iota(I32, (C, C), 1)
    levels = (1, 2, 4, 8, 16, 32)
    pair_masks = [((tt // (2 * L)) == (ss // (2 * L))) & ((tt & L) != 0) & ((ss & L) == 0) for L in levels]
    diag_mask = tt == ss
    scale = GLA_DK ** -0.5

    def head_chunk(rows, h, z):
        qk_cols = slice(h * GLA_DK, (h + 1) * GLA_DK)
        v_cols = slice(h * GLA_DV, (h + 1) * GLA_DV)
        q = q_ref[rows, qk_cols].astype(F32) * scale
        k = k_ref[rows, qk_cols].astype(F32)
        v = v_ref[rows, v_cols]
        g = (jnp.minimum(z, 0.0) - jnp.log1p(jnp.exp(-jnp.abs(z)))) * (1.0 / GLA_TAU)
        incl = g
        tot = g
        att = jnp.where(diag_mask, _dot_nt(q.astype(BF16), k.astype(BF16)), 0.0)
        for L, pm in zip(levels, pair_masks):
            ql = (q * jnp.exp(incl)).astype(BF16)
            kl = (k * jnp.exp(tot - incl)).astype(BF16)
            att = jnp.where(pm, _dot_nt(ql, kl), att)
            upper = (rowi & L) != 0
            from_lower = pltpu.roll(tot, L, 0)
            from_upper = pltpu.roll(tot, C - L, 0)
            incl = incl + jnp.where(upper, from_lower, 0.0)
            tot = tot + jnp.where(upper, from_lower, from_upper)
        qd = (q * jnp.exp(incl)).astype(BF16)
        kd = (k * jnp.exp(tot - incl)).astype(BF16)
        st = st_ref[h]
        o = _dot_nt(qd, st.astype(BF16)) + _dot(att.astype(BF16), v)
        st_ref[h] = st * jnp.exp(tot[0:1, :]) + _dot_tn(v, kd)
        mu = jnp.mean(o, axis=-1, keepdims=True)
        oc = o - mu
        var = jnp.mean(oc * oc, axis=-1, keepdims=True)
        on = oc * lax.rsqrt(var + LN_EPS) * ng_ref[:, v_cols]
        r = r_ref[rows, v_cols].astype(F32)
        o_ref[rows, v_cols] = (on * (r * jax.nn.sigmoid(r))).astype(o_ref.dtype)

    def chunk(c, carry):
        rows = pl.ds(pl.multiple_of(c * C, C), C)
        z = _dot(sm_ref[rows, :].astype(BF16), wa_ref[...]) + ba_ref[...]
        for h in range(GLA_HEADS):
            head_chunk(rows, h, z[:, h * GLA_DK:(h + 1) * GLA_DK])
        return carry

    lax.fori_loop(0, n_chunks, chunk, 0)


def _gla(h_big, h_small, wa_pad, b_a, norm_g, B, T):
    n = B * T
    return pl.pallas_call(
        _gla_kernel,
        grid=(B,),
        in_specs=[
            pl.BlockSpec((T, GLA_QK), lambda b: (b, COL_GQ // GLA_QK)),
            pl.BlockSpec((T, GLA_QK), lambda b: (b, COL_GK // GLA_QK)),
            pl.BlockSpec((T, GLA_V), lambda b: (b, COL_GV // GLA_V)),
            pl.BlockSpec((T, GLA_V), lambda b: (b, COL_GR // GLA_V)),
            pl.BlockSpec((T, LANES), lambda b: (b, 0)),
            pl.BlockSpec((LANES, GLA_QK), lambda b: (0, 0)),
            pl.BlockSpec((1, GLA_QK), lambda b: (0, 0)),
            pl.BlockSpec((1, GLA_V), lambda b: (0, 0)),
        ],
        out_specs=pl.BlockSpec((T, GLA_V), lambda b: (b, 0)),
        out_shape=jax.ShapeDtypeStruct((n, GLA_V), BF16),
        scratch_shapes=[pltpu.VMEM((GLA_HEADS, GLA_DV, GLA_DK), F32)],
        compiler_params=_cp(("parallel",)),
        name="gla",
    )(h_big, h_big, h_big, h_big, h_small, wa_pad, b_a, norm_g)


def _compress_kernel(u_ref, w1_ref, w2_ref, w2t_ref, pe_ref, o_ref, ot_ref):
    half = CMP_STRIDE * NSA_DH
    u = u_ref[0, 0, 0]
    nrow = u.shape[0]
    a = _dot(u, w1_ref[0, 0:half, :])
    bm = _dot(u, w1_ref[0, half:2 * half, :])
    c = _dot(pe_ref[0], w1_ref[0])
    hid = a + pltpu.roll(bm, nrow - 1, 0) + c[0:1, :]
    act = jax.nn.gelu(hid).astype(BF16)
    o_ref[0, 0, 0] = _dot(act, w2_ref[0]).astype(o_ref.dtype)
    ot_ref[0, 0, 0] = _dot_nt(w2t_ref[0], act).astype(ot_ref.dtype)


def _compress(kv_t, w1, w2, w2t, pe8, B, T):
    nc = T // CMP_STRIDE
    u = kv_t.reshape(B, 6, NSA_GROUPS, nc, CMP_STRIDE * NSA_DH)
    return pl.pallas_call(
        _compress_kernel,
        grid=(B, 2, NSA_GROUPS),
        in_specs=[
            pl.BlockSpec((1, 1, 1, nc, CMP_STRIDE * NSA_DH), lambda b, s, g: (b, s, g, 0, 0)),
            pl.BlockSpec((1, CMP_LEN * NSA_DH, CMP_HIDDEN), lambda b, s, g: (s, 0, 0)),
            pl.BlockSpec((1, CMP_HIDDEN, NSA_DH), lambda b, s, g: (s, 0, 0)),
            pl.BlockSpec((1, NSA_DH, CMP_HIDDEN), lambda b, s, g: (s, 0, 0)),
            pl.BlockSpec((1, 16, CMP_LEN * NSA_DH), lambda b, s, g: (s, 0, 0)),
        ],
        out_specs=[
            pl.BlockSpec((1, 1, 1, nc, NSA_DH), lambda b, s, g: (b, s, g, 0, 0)),
            pl.BlockSpec((1, 1, 1, NSA_DH, nc), lambda b, s, g: (b, s, g, 0, 0)),
        ],
        out_shape=[jax.ShapeDtypeStruct((B, 2, NSA_GROUPS, nc, NSA_DH), BF16),
                   jax.ShapeDtypeStruct((B, 2, NSA_GROUPS, NSA_DH, nc), BF16)],
        compiler_params=_cp(("parallel", "parallel", "parallel")),
        name="nsa_compress",
    )(u, w1, w2, w2t, pe8)


def _cmp_select_kernel(slopes_ref, qt_ref, kc_ref, vct_ref, ovt_ref, ocmp_ref, mb_ref):
    g = pl.program_id(1)
    i = pl.program_id(2)
    tq = qt_ref.shape[3]
    nc = kc_ref.shape[3]
    ns = mb_ref.shape[2]
    t0 = i * tq
    tpos = (t0 + lax.broadcasted_iota(I32, (nc, tq), 1))
    nidx = lax.broadcasted_iota(I32, (nc, tq), 0)
    mask_c = (nidx * CMP_STRIDE + (CMP_LEN - 1)) <= tpos
    absd = jnp.abs(tpos.astype(F32) - (nidx.astype(F32) * CMP_STRIDE + 0.5 * (CMP_LEN - 1)))
    kc = kc_ref[0, 0, 0]
    vct = vct_ref[0, 0, 0]
    psum = jnp.zeros((nc, tq), F32)
    for hh in range(NSA_HPG):
        slope = slopes_ref[g * NSA_HPG + hh]
        q = qt_ref[0, hh] * jnp.asarray(NSA_DH ** -0.5, BF16)
        s = _dot(kc, q) - slope * absd
        s = jnp.where(mask_c, s, NEG)
        e = jnp.exp(s - jnp.max(s, axis=0, keepdims=True))
        p = jnp.where(mask_c, e / jnp.sum(e, axis=0, keepdims=True), 0.0)
        ocmp_ref[0, hh] = _dot(vct, p.astype(BF16)).astype(ocmp_ref.dtype)
        psum = psum + p
    p_hi = psum.astype(BF16)
    p_lo = (psum - p_hi.astype(F32)).astype(BF16)
    imp = _dot(ovt_ref[...], p_hi) + _dot(ovt_ref[...], p_lo)
    j = lax.broadcasted_iota(I32, (ns, tq), 0)
    tp = t0 + lax.broadcasted_iota(I32, (ns, tq), 1)
    cur = tp // SEL_LEN
    forced = (j == 0) | (j == cur) | (j == cur - 1)
    valid = j * SEL_LEN <= tp
    score = jnp.where(valid, imp + jnp.where(forced, FORCE_BONUS, 0.0), NEG)
    rank = jnp.zeros((ns, tq), F32)
    for jp in range(ns):
        row = score[jp:jp + 1, :]
        beats = (row > score) | ((row == score) & (j > jp))
        rank = rank + jnp.where(beats, 1.0, 0.0)
    keep = valid & (rank < float(min(SEL_TOPN, ns)))
    mb_ref[0, 0] = jnp.where(keep, 0.0, NEG).astype(mb_ref.dtype)


def _cmp_select(slopes, q_t, kcmp, kcmp_t, ovt, B, T, tq):
    nc = T // CMP_STRIDE
    ns = T // SEL_LEN
    grid_spec = pltpu.PrefetchScalarGridSpec(
        num_scalar_prefetch=1,
        grid=(B, NSA_GROUPS, T // tq),
        in_specs=[
            pl.BlockSpec((1, NSA_HPG, NSA_DH, tq), lambda b, g, i, s: (b, g, 0, i)),
            pl.BlockSpec((1, 1, 1, nc, NSA_DH), lambda b, g, i, s: (b, 0, g, 0, 0)),
            pl.BlockSpec((1, 1, 1, NSA_DH, nc), lambda b, g, i, s: (b, 1, g, 0, 0)),
            pl.BlockSpec((ns, nc), lambda b, g, i, s: (0, 0)),
        ],
        out_specs=[
            pl.BlockSpec((1, NSA_HPG, NSA_DH, tq), lambda b, g, i, s: (b, g, 0, i)),
            pl.BlockSpec((1, 1, ns, tq), lambda b, g, i, s: (b, g, 0, i)),
        ],
    )
    return pl.pallas_call(
        _cmp_select_kernel,
        grid_spec=grid_spec,
        out_shape=[jax.ShapeDtypeStruct((B, NSA_HEADS, NSA_DH, T), BF16),
                   jax.ShapeDtypeStruct((B, NSA_GROUPS, ns, T), BF16)],
        compiler_params=_cp(("parallel", "parallel", "parallel")),
        name="nsa_cmp_select",
    )(slopes, q_t, kcmp, kcmp_t, ovt)


def _sel_win_kernel(slopes_ref, qt_ref, ks_ref, vst_ref, kw_ref, vwt_ref, mb_ref, ocmp_ref, gt_ref,
                    o_ref, qaug_ref, m_ref, l_ref, acc_ref, srow_ref, bias_ref):
    g = pl.program_id(1)
    i = pl.program_id(2)
    tq = qt_ref.shape[3]
    tk = tq
    ns = mb_ref.shape[2]
    dh = NSA_DH
    wide = NSA_HPG * tq
    t0 = i * tq
    BIG = -NEG

    @pl.when(i == 0)
    def _():
        srow = jnp.concatenate([jnp.full((1, tq), slopes_ref[g * NSA_HPG + hh], F32) for hh in range(NSA_HPG)],
                               axis=1)
        srow_ref[...] = srow
        lane = lax.broadcasted_iota(I32, (tk, wide), 1) & (tq - 1)
        dist0 = (lane - lax.broadcasted_iota(I32, (tk, wide), 0)).astype(F32)
        sd0 = srow * dist0
        bias_ref[0] = sd0
        bias_ref[1] = sd0 + jnp.where(dist0 >= 0.0, 0.0, BIG)
        bias_ref[2] = sd0 + jnp.where(dist0 < 0.0, 0.0, BIG)

    for hh in range(NSA_HPG):
        cols = slice(hh * tq, (hh + 1) * tq)
        qaug_ref[0:dh, cols] = qt_ref[0, hh] * jnp.asarray(dh ** -0.5, BF16)
        qaug_ref[dh:dh + ns, cols] = mb_ref[0, 0]
        qaug_ref[dh + ns:, cols] = jnp.zeros((qaug_ref.shape[0] - dh - ns, tq), BF16)

    def reset():
        m_ref[...] = jnp.full(m_ref.shape, NEG, F32)
        l_ref[...] = jnp.zeros(l_ref.shape, F32)
        acc_ref[...] = jnp.zeros(acc_ref.shape, F32)

    def tile_step(k_tile, vt_tile, kd, s0, mode):
        s = _dot(k_tile, qaug_ref[0:kd, :]) - bias_ref[mode]
        if s0 is not None:
            s = s - srow_ref[...] * (t0 - s0).astype(F32)
        m_old = m_ref[...]
        m_new = jnp.maximum(m_old, jnp.max(s, axis=0, keepdims=True))
        alpha = jnp.exp(m_old - m_new)
        p = jnp.exp(s - m_new)
        l_ref[...] = alpha * l_ref[...] + jnp.sum(p, axis=0, keepdims=True)
        acc_ref[...] = alpha * acc_ref[...] + _dot(vt_tile, p.astype(BF16))
        m_ref[...] = m_new

    def gate_row(branch):
        rows = [gt_ref[0, 0, 3 * hh + branch:3 * hh + branch + 1, :] for hh in range(NSA_HPG)]
        return jax.nn.sigmoid(jnp.concatenate(rows, axis=1))

    reset()

    def sel_body(kb, carry):
        s0 = pl.multiple_of(kb * tk, tk)
        tile_step(ks_ref[0, 0, pl.ds(s0, tk), :], vst_ref[0, 0, :, pl.ds(s0, tk)], qaug_ref.shape[0], s0, 0)
        return carry

    lax.fori_loop(0, i, sel_body, 0)
    d0 = pl.multiple_of(t0, tk)
    tile_step(ks_ref[0, 0, pl.ds(d0, tk), :], vst_ref[0, 0, :, pl.ds(d0, tk)], qaug_ref.shape[0], None, 1)
    o = gate_row(1) * (acc_ref[...] / l_ref[...])

    reset()
    n_back = WINDOW // tk

    @pl.when(i >= n_back)
    def _():
        s0 = pl.multiple_of((i - n_back) * tk, tk)
        tile_step(kw_ref[0, 0, pl.ds(s0, tk), :], vwt_ref[0, 0, :, pl.ds(s0, tk)], dh, s0, 2)

    for back in range(n_back - 1, 0, -1):
        @pl.when(i >= back)
        def _(back=back):
            s0 = pl.multiple_of((i - back) * tk, tk)
            tile_step(kw_ref[0, 0, pl.ds(s0, tk), :], vwt_ref[0, 0, :, pl.ds(s0, tk)], dh, s0, 0)

    tile_step(kw_ref[0, 0, pl.ds(d0, tk), :], vwt_ref[0, 0, :, pl.ds(d0, tk)], dh, None, 1)
    o = o + gate_row(2) * (acc_ref[...] / l_ref[...])
    ocmp = jnp.concatenate([ocmp_ref[0, hh] for hh in range(NSA_HPG)], axis=1).astype(F32)
    o = o + gate_row(0) * ocmp
    for hh in range(NSA_HPG):
        o_ref[0, hh] = o[:, hh * tq:(hh + 1) * tq].astype(o_ref.dtype)


def _sel_win(slopes, q_t, ks_aug, vs_t, kw, vw_t, mb, ocmp_t, gates_t, B, T, tq):
    ns = T // SEL_LEN
    kaug = ks_aug.shape[-1]
    G = NSA_GROUPS
    grid_spec = pltpu.PrefetchScalarGridSpec(
        num_scalar_prefetch=1,
        grid=(B, G, T // tq),
        in_specs=[
            pl.BlockSpec((1, NSA_HPG, NSA_DH, tq), lambda b, g, i, s: (b, g, 0, i)),
            pl.BlockSpec((1, 1, T, kaug), lambda b, g, i, s: (b, g, 0, 0)),
            pl.BlockSpec((1, 1, NSA_DH, T), lambda b, g, i, s: (b, g, 0, 0)),
            pl.BlockSpec((1, 1, T, NSA_DH), lambda b, g, i, s: (b, g, 0, 0)),
            pl.BlockSpec((1, 1, NSA_DH, T), lambda b, g, i, s: (b, g, 0, 0)),
            pl.BlockSpec((1, 1, ns, tq), lambda b, g, i, s: (b, g, 0, i)),
            pl.BlockSpec((1, NSA_HPG, NSA_DH, tq), lambda b, g, i, s: (b, g, 0, i)),
            pl.BlockSpec((1, 1, 16, tq), lambda b, g, i, s: (b, g, 0, i)),
        ],
        out_specs=pl.BlockSpec((1, NSA_HPG, NSA_DH, tq), lambda b, g, i, s: (b, g, 0, i)),
        scratch_shapes=[
            pltpu.VMEM((kaug, NSA_HPG * tq), BF16),
            pltpu.VMEM((1, NSA_HPG * tq), F32),
            pltpu.VMEM((1, NSA_HPG * tq), F32),
            pltpu.VMEM((NSA_DH, NSA_HPG * tq), F32),
            pltpu.VMEM((1, NSA_HPG * tq), F32),
            pltpu.VMEM((3, tq, NSA_HPG * tq), F32),
        ],
    )
    return pl.pallas_call(
        _sel_win_kernel,
        grid_spec=grid_spec,
        out_shape=jax.ShapeDtypeStruct((B, NSA_HEADS, NSA_DH, T), BF16),
        compiler_params=_cp(("parallel", "parallel", "arbitrary")),
        name="nsa_sel_win",
    )(slopes, q_t, ks_aug, vs_t, kw, vw_t, mb, ocmp_t, gates_t)


def _mix_kernel(og_ref, on_ref, mg1_ref, mg2_ref, x_ref, wg_ref, wn_ref, wo_ref, lg_ref, lb_ref,
                x1_ref, x1b_ref):
    g1 = _dot(og_ref[...], wg_ref[...])
    g2 = _dot(on_ref[...], wn_ref[...])
    merged = (jax.nn.sigmoid(mg1_ref[...].astype(F32)) * g1
              + jax.nn.sigmoid(mg2_ref[...].astype(F32)) * g2)
    y = _dot(merged.astype(BF16), wo_ref[...])
    x1 = _layer_norm(DN_ALPHA * x_ref[...] + y, lg_ref[...], lb_ref[...])
    x1_ref[...] = x1
    x1b_ref[...] = x1.astype(BF16)


def _const_spec(shape):
    nd = len(shape)
    return pl.BlockSpec(shape, lambda *_: (0,) * nd, pipeline_mode=pl.Buffered(1))


def _mix(o_gla, o_nsa, h_big, x, wg, wn, wo, lg, lb, tm):
    n, d = x.shape
    return pl.pallas_call(
        _mix_kernel,
        grid=(n // tm,),
        in_specs=[
            pl.BlockSpec((tm, GLA_V), lambda i: (i, 0)),
            pl.BlockSpec((tm, NSA_Q), lambda i: (i, 0)),
            pl.BlockSpec((tm, d), lambda i: (i, 0)),
            pl.BlockSpec((tm, d), lambda i: (i, 1)),
            pl.BlockSpec((tm, d), lambda i: (i, 0)),
            _const_spec(wg.shape), _const_spec(wn.shape), _const_spec(wo.shape),
            _const_spec(lg.shape), _const_spec(lb.shape),
        ],
        out_specs=[pl.BlockSpec((tm, d), lambda i: (i, 0)), pl.BlockSpec((tm, d), lambda i: (i, 0))],
        out_shape=[jax.ShapeDtypeStruct((n, d), F32), jax.ShapeDtypeStruct((n, d), BF16)],
        compiler_params=_cp(("parallel",)),
        name="mix_ln",
    )(o_gla, o_nsa, h_big, h_big, x, wg, wn, wo, lg, lb)


def _xattn_kernel(x_ref, xb_ref, kv_ref, wq_ref, wo_ref, lg_ref, lb_ref, x2_ref):
    hd = XA_HEADS * XA_DH
    q = (_dot(xb_ref[...], wq_ref[...]) * (XA_DH ** -0.5)).astype(BF16)
    outs = []
    for h in range(XA_HEADS):
        kh = kv_ref[0, :, h * XA_DH:(h + 1) * XA_DH]
        vh = kv_ref[0, :, hd + h * XA_DH:hd + (h + 1) * XA_DH]
        s = _dot_nt(q[:, h * XA_DH:(h + 1) * XA_DH], kh)
        e = jnp.exp(s - jnp.max(s, axis=-1, keepdims=True))
        p = e / jnp.sum(e, axis=-1, keepdims=True)
        outs.append(_dot(p.astype(BF16), vh).astype(BF16))
    o = jnp.concatenate(outs, axis=-1)
    y = _dot(o, wo_ref[...])
    x2_ref[...] = _layer_norm(DN_ALPHA * x_ref[...] + y, lg_ref[...], lb_ref[...])


def _xattn(x1, x1b, kv, wq, wo, lg, lb, B, T, tm):
    n, d = x1.shape
    nt = T // tm
    return pl.pallas_call(
        _xattn_kernel,
        grid=(B, nt),
        in_specs=[
            pl.BlockSpec((tm, d), lambda b, i: (b * nt + i, 0)),
            pl.BlockSpec((tm, d), lambda b, i: (b * nt + i, 0)),
            pl.BlockSpec((1,) + kv.shape[1:], lambda b, i: (b, 0, 0)),
            _const_spec(wq.shape), _const_spec(wo.shape), _const_spec(lg.shape), _const_spec(lb.shape),
        ],
        out_specs=pl.BlockSpec((tm, d), lambda b, i: (b * nt + i, 0)),
        out_shape=jax.ShapeDtypeStruct((n, d), F32),
        compiler_params=_cp(("parallel", "parallel")),
        name="xattn_ln",
    )(x1, x1b, kv, wq, wo, lg, lb)


def _router_kernel(x_ref, wh_ref, wl_ref, rb_ref, e_ref, gate_ref, rank_ref, cnt_ref, carry_ref):
    i = pl.program_id(0)
    tr = x_ref.shape[0]
    E = N_EXPERTS

    @pl.when(i == 0)
    def _():
        carry_ref[...] = jnp.zeros_like(carry_ref)

    x = x_ref[...]
    x_hi = x.astype(BF16)
    x_lo = (x - x_hi.astype(F32)).astype(BF16)
    wh = wh_ref[...]
    logits = _dot_nt(wh, x_hi) + _dot_nt(wh, x_lo) + _dot_nt(wl_ref[...], x_hi)
    biased = logits + rb_ref[...]
    rows = [biased[e:e + 1, :] for e in range(E)]
    raw = [logits[e:e + 1, :] for e in range(E)]
    best_score = None
    best = None
    for gi in range(N_GROUPS):
        v = rows[gi * EXPERTS_PER_GROUP:(gi + 1) * EXPERTS_PER_GROUP]
        sc = None
        for a in range(EXPERTS_PER_GROUP):
            for b in range(a + 1, EXPERTS_PER_GROUP):
                pair = v[a] + v[b]
                sc = pair if sc is None else jnp.maximum(sc, pair)
        if best is None:
            best_score, best = sc, jnp.zeros((1, tr), I32)
        else:
            better = sc > best_score
            best_score = jnp.where(better, sc, best_score)
            best = jnp.where(better, gi, best)

    def pick(vals):
        out = vals[0:EXPERTS_PER_GROUP]
        for gi in range(1, N_GROUPS):
            out = [jnp.where(best == gi, vals[gi * EXPERTS_PER_GROUP + a], out[a]) for a in range(EXPERTS_PER_GROUP)]
        return out

    w = pick(rows)
    lraw = pick(raw)
    i1 = jnp.zeros((1, tr), I32)
    v1 = w[0]
    l1 = lraw[0]
    for a in range(1, EXPERTS_PER_GROUP):
        better = w[a] > v1
        v1 = jnp.where(better, w[a], v1)
        l1 = jnp.where(better, lraw[a], l1)
        i1 = jnp.where(better, a, i1)
    i2 = jnp.full((1, tr), -1, I32)
    v2 = jnp.full((1, tr), -jnp.inf, F32)
    l2 = jnp.zeros((1, tr), F32)
    for a in range(EXPERTS_PER_GROUP):
        better = (i1 != a) & ((w[a] > v2) | (i2 < 0))
        v2 = jnp.where(better, w[a], v2)
        l2 = jnp.where(better, lraw[a], l2)
        i2 = jnp.where(better, a, i2)
    e1 = best * EXPERTS_PER_GROUP + i1
    e2 = best * EXPERTS_PER_GROUP + i2
    mx = jnp.maximum(l1, l2)
    p1 = jnp.exp(l1 - mx)
    p2 = jnp.exp(l2 - mx)
    den = p1 + p2
    e_ref[0:1, :] = e1
    e_ref[1:2, :] = e2
    gate_ref[0:1, :] = p1 / den
    gate_ref[1:2, :] = p2 / den
    eidx = lax.broadcasted_iota(I32, (E, tr), 0)
    is1 = eidx == e1
    is2 = eidx == e2
    member = jnp.where(is1 | is2, 1.0, 0.0)
    uu = lax.broadcasted_iota(I32, (tr, tr), 0)
    tt = lax.broadcasted_iota(I32, (tr, tr), 1)
    tri = jnp.where(uu <= tt, 1.0, 0.0).astype(BF16)
    incl = _dot(member.astype(BF16), tri)
    excl = carry_ref[:, 0:1] + incl - member
    rank_ref[0:1, :] = jnp.sum(jnp.where(is1, excl, 0.0), axis=0, keepdims=True).astype(I32)
    rank_ref[1:2, :] = jnp.sum(jnp.where(is2, excl, 0.0), axis=0, keepdims=True).astype(I32)
    new_carry = carry_ref[...] + jnp.sum(member, axis=1, keepdims=True)
    carry_ref[...] = new_carry
    cnt_ref[...] = new_carry


def _router(x2, rw_hi, rw_lo, rb, tr):
    n, d = x2.shape
    E = N_EXPERTS
    return pl.pallas_call(
        _router_kernel,
        grid=(n // tr,),
        in_specs=[
            pl.BlockSpec((tr, d), lambda i: (i, 0)),
            pl.BlockSpec((E, d), lambda i: (0, 0)),
            pl.BlockSpec((E, d), lambda i: (0, 0)),
            pl.BlockSpec((E, 1), lambda i: (0, 0)),
        ],
        out_specs=[
            pl.BlockSpec((2, tr), lambda i: (0, i)),
            pl.BlockSpec((2, tr), lambda i: (0, i)),
            pl.BlockSpec((2, tr), lambda i: (0, i)),
            pl.BlockSpec((E, LANES), lambda i: (0, 0)),
        ],
        out_shape=[jax.ShapeDtypeStruct((2, n), I32), jax.ShapeDtypeStruct((2, n), F32),
                   jax.ShapeDtypeStruct((2, n), I32), jax.ShapeDtypeStruct((E, LANES), F32)],
        scratch_shapes=[pltpu.VMEM((E, LANES), F32)],
        compiler_params=_cp(("arbitrary",)),
        name="moe_router",
    )(x2, rw_hi, rw_lo, rb)


def _dispatch_kernel(ps_ref, e_ref, rank_ref, x_ref, buf_in, buf_hbm, sem):
    del buf_in
    td = e_ref.shape[1]

    def issue(t, carry):
        for kk in range(TOP_K):
            dest = ps_ref[e_ref[kk, t]] + rank_ref[kk, t]
            pltpu.make_async_copy(x_ref.at[pl.ds(t, 1), :], buf_hbm.at[pl.ds(dest, 1), :], sem).start()
        return carry

    lax.fori_loop(0, td, issue, 0, unroll=8)
    for kk in range(TOP_K):
        pltpu.make_async_copy(x_ref, buf_hbm.at[pl.ds(0, td), :], sem).wait()


def _dispatch(pad_start, e, rank, x2, buf0, td):
    n, d = x2.shape
    grid_spec = pltpu.PrefetchScalarGridSpec(
        num_scalar_prefetch=1,
        grid=(n // td,),
        in_specs=[
            pl.BlockSpec((2, td), lambda i, s: (0, i), memory_space=pltpu.SMEM),
            pl.BlockSpec((2, td), lambda i, s: (0, i), memory_space=pltpu.SMEM),
            pl.BlockSpec((td, d), lambda i, s: (i, 0)),
            pl.BlockSpec(memory_space=pl.ANY),
        ],
        out_specs=pl.BlockSpec(memory_space=pl.ANY),
        scratch_shapes=[pltpu.SemaphoreType.DMA(())],
    )
    return pl.pallas_call(
        _dispatch_kernel,
        grid_spec=grid_spec,
        out_shape=jax.ShapeDtypeStruct(buf0.shape, buf0.dtype),
        input_output_aliases={4: 0},
        compiler_params=_cp(("arbitrary",)),
        name="moe_dispatch",
    )(pad_start, e, rank, x2, buf0)


def _expert_kernel(be_ref, nb_ref, x_ref, wa_ref, wu_ref, wd_ref, y_ref, xb_ref):
    blk = pl.program_id(0)
    f = pl.program_id(1)

    @pl.when((blk >= nb_ref[0]) & (f == 0))
    def _():
        y_ref[...] = jnp.zeros_like(y_ref)

    @pl.when(blk < nb_ref[0])
    def _():
        @pl.when(f == 0)
        def _():
            xb_ref[...] = x_ref[...].astype(BF16)

        xb = xb_ref[...]
        a = _dot(xb, wa_ref[0])
        u = _dot(xb, wu_ref[0])
        act = (a * jax.nn.sigmoid(a) * u).astype(BF16)
        y = _dot(act, wd_ref[0])

        @pl.when(f == 0)
        def _():
            y_ref[...] = y

        @pl.when(f != 0)
        def _():
            y_ref[...] += y


def _experts(blk_expert, n_used, buf, w_in, w_down):
    p, d = buf.shape
    nb = p // MOE_BLOCK
    nf = D_FF // FF_TILE
    grid_spec = pltpu.PrefetchScalarGridSpec(
        num_scalar_prefetch=2,
        grid=(nb, nf),
        in_specs=[
            pl.BlockSpec((MOE_BLOCK, d), lambda b, f, be, nu: (b, 0)),
            pl.BlockSpec((1, d, FF_TILE), lambda b, f, be, nu: (be[b], 0, f)),
            pl.BlockSpec((1, d, FF_TILE), lambda b, f, be, nu: (be[b], 0, nf + f)),
            pl.BlockSpec((1, FF_TILE, d), lambda b, f, be, nu: (be[b], f, 0)),
        ],
        out_specs=pl.BlockSpec((MOE_BLOCK, d), lambda b, f, be, nu: (b, 0)),
        scratch_shapes=[pltpu.VMEM((MOE_BLOCK, d), BF16)],
    )
    return pl.pallas_call(
        _expert_kernel,
        grid_spec=grid_spec,
        out_shape=jax.ShapeDtypeStruct((p, d), F32),
        compiler_params=_cp(("arbitrary", "arbitrary")),
        name="moe_experts",
    )(blk_expert, n_used, buf, w_in, w_in, w_down)


def _combine_kernel(ps_ref, e_ref, rank_ref, y_hbm, x_ref, gate_ref, lg_ref, lb_ref, x3_ref, x3b_ref,
                    y0_ref, y1_ref, sem):
    tc = x_ref.shape[0]
    bufs = (y0_ref, y1_ref)

    def issue(t, carry):
        for kk in range(TOP_K):
            src = ps_ref[e_ref[kk, t]] + rank_ref[kk, t]
            pltpu.make_async_copy(y_hbm.at[pl.ds(src, 1), :], bufs[kk].at[pl.ds(t, 1), :], sem).start()
        return carry

    lax.fori_loop(0, tc, issue, 0, unroll=8)
    for kk in range(TOP_K):
        pltpu.make_async_copy(y_hbm.at[pl.ds(0, tc), :], bufs[kk], sem).wait()
    gate = gate_ref[...]
    z = DN_ALPHA * x_ref[...] + gate[:, 0:1] * y0_ref[...] + gate[:, 1:2] * y1_ref[...]
    x3 = _layer_norm(z, lg_ref[...], lb_ref[...])
    x3_ref[...] = x3
    x3b_ref[...] = x3.astype(BF16)


def _combine(pad_start, e, rank, y, x2, gate_nt, lg, lb, tc):
    n, d = x2.shape
    grid_spec = pltpu.PrefetchScalarGridSpec(
        num_scalar_prefetch=1,
        grid=(n // tc,),
        in_specs=[
            pl.BlockSpec((2, tc), lambda i, s: (0, i), memory_space=pltpu.SMEM),
            pl.BlockSpec((2, tc), lambda i, s: (0, i), memory_space=pltpu.SMEM),
            pl.BlockSpec(memory_space=pl.ANY),
            pl.BlockSpec((tc, d), lambda i, s: (i, 0)),
            pl.BlockSpec((tc, 2), lambda i, s: (i, 0)),
            pl.BlockSpec((1, d), lambda i, s: (0, 0)),
            pl.BlockSpec((1, d), lambda i, s: (0, 0)),
        ],
        out_specs=[pl.BlockSpec((tc, d), lambda i, s: (i, 0)), pl.BlockSpec((tc, d), lambda i, s: (i, 0))],
        scratch_shapes=[pltpu.VMEM((tc, d), F32), pltpu.VMEM((tc, d), F32), pltpu.SemaphoreType.DMA(())],
    )
    return pl.pallas_call(
        _combine_kernel,
        grid_spec=grid_spec,
        out_shape=[jax.ShapeDtypeStruct((n, d), F32), jax.ShapeDtypeStruct((n, d), BF16)],
        compiler_params=_cp(("arbitrary",)),
        name="moe_combine_ln",
    )(pad_start, e, rank, y, x2, gate_nt, lg, lb)


def _layer(x, xb, mem_b, p, consts, B, T):
    n, d = x.shape
    G, HPG, DH = NSA_GROUPS, NSA_HPG, NSA_DH
    slopes, ovt, sel_onehot = consts

    h_big = _matmul(xb, p["w_big"], BF16, 1024, 512)
    h_small = _matmul(xb, p["w_small"], F32, 1024, LANES)

    o_gla = _gla(h_big, h_small, p["wa_pad"], p["b_a"], p["norm_g"], B, T)

    q_t = h_big[:, COL_NQ:COL_NQ + NSA_Q].reshape(B, T, NSA_HEADS, DH).transpose(0, 2, 3, 1)
    kv = h_big[:, COL_NKV:COL_END].reshape(B, T, 6, G, DH)
    kv_rows = kv.transpose(0, 2, 3, 1, 4)
    kv_cols = kv.transpose(0, 2, 3, 4, 1)
    kcmp, kcmp_t = _compress(kv_rows, p["cmp_w1"], p["cmp_w2"], p["cmp_w2t"], p["cmp_pe8"], B, T)
    ocmp_t, mb = _cmp_select(slopes, q_t, kcmp, kcmp_t, ovt, B, T, 512)
    pad = jnp.zeros((B, G, T, LANES - DH - sel_onehot.shape[1]), BF16)
    ks_aug = jnp.concatenate([kv_rows[:, 2], jnp.broadcast_to(sel_onehot, (B, G) + sel_onehot.shape), pad], axis=-1)
    gates_t = h_small[:, GLA_GATE_RANK:GLA_GATE_RANK + 3 * NSA_HEADS].reshape(B, T, G, 3 * HPG)
    gates_t = jnp.pad(gates_t.transpose(0, 2, 3, 1), ((0, 0), (0, 0), (0, 16 - 3 * HPG), (0, 0)))
    o_nsa_t = _sel_win(slopes, q_t, ks_aug, kv_cols[:, 3], kv_rows[:, 4], kv_cols[:, 5], mb, ocmp_t, gates_t,
                       B, T, 256)
    o_nsa = o_nsa_t.transpose(0, 3, 1, 2).reshape(n, NSA_Q)

    x1, x1b = _mix(o_gla, o_nsa, h_big, x, p["w_bg"], p["w_bn"], p["w_out"], p["ln_mix_g"], p["ln_mix_b"], 256)

    kvm = _matmul(mem_b, p["xa_wkv"], BF16, 512, 512).reshape(B, MEM_LEN, 2 * XA_HEADS * XA_DH)
    x2 = _xattn(x1, x1b, kvm, p["xa_wq"], p["xa_wo"], p["ln_xa_g"], p["ln_xa_b"], B, T, 256)

    e, gate, rank, cnt = _router(x2, p["rw_hi"], p["rw_lo"], p["rb"], 512)
    counts = cnt[:, 0].astype(I32)
    padded = (counts + MOE_BLOCK - 1) // MOE_BLOCK * MOE_BLOCK
    pad_end = jnp.cumsum(padded)
    pad_start = (pad_end - padded).astype(I32)
    nb = (n * TOP_K) // MOE_BLOCK + N_EXPERTS
    n_used = (pad_end[-1] // MOE_BLOCK).astype(I32).reshape(1)
    blk_start = jnp.arange(nb, dtype=I32) * MOE_BLOCK
    blk_expert = jnp.minimum(jnp.sum(blk_start[:, None] >= pad_end[None, :], axis=1), N_EXPERTS - 1).astype(I32)
    blk_expert = jnp.where(jnp.arange(nb) < n_used[0], blk_expert, blk_expert[jnp.maximum(n_used[0] - 1, 0)])
    buf = _dispatch(pad_start, e, rank, x2, jnp.zeros((nb * MOE_BLOCK, d), F32), 512)
    y = _experts(blk_expert, n_used, buf, p["moe_w_in"], p["moe_w_down"])
    x3, x3b = _combine(pad_start, e, rank, y, x2, gate.T, p["ln_ffn_g"], p["ln_ffn_b"], 256)
    return x3, x3b


def _prep_layer(l, w_in, gla_w_a2, gla_b_a, gla_norm_g, nsa_cmp_pe, nsa_cmp_w1, nsa_cmp_w2, w_branch_gla,
                w_branch_nsa, w_out, ln_mix_g, ln_mix_b, xa_wq, xa_wkv, xa_wo, ln_xa_g, ln_xa_b, router_w,
                router_b, moe_w_in, moe_w_down, ln_ffn_g, ln_ffn_b):
    d = w_in.shape[1]
    w = w_in[l]
    o_gq, o_gk, o_gv, o_gr = 0, GLA_QK, 2 * GLA_QK, 2 * GLA_QK + GLA_V
    o_ga = o_gr + GLA_V
    o_nq = o_ga + GLA_GATE_RANK
    o_nkv = o_nq + NSA_Q
    o_ng = o_nkv + 6 * NSA_KV
    o_mg = o_ng + 3 * NSA_HEADS
    w_big = jnp.concatenate([w[:, o_mg:o_mg + 2 * d], w[:, o_gq:o_ga], w[:, o_nq:o_ng]], axis=1).astype(BF16)
    w_small = jnp.concatenate([w[:, o_ga:o_nq], w[:, o_ng:o_mg],
                               jnp.zeros((d, LANES - GLA_GATE_RANK - 3 * NSA_HEADS), F32)], axis=1).astype(BF16)
    wa_pad = jnp.concatenate([gla_w_a2[l], jnp.zeros((LANES - GLA_GATE_RANK, GLA_QK), F32)], axis=0).astype(BF16)
    rw_t = router_w.T
    rw_hi = rw_t.astype(BF16)
    rw_lo = (rw_t - rw_hi.astype(F32)).astype(BF16)
    return dict(
        w_big=w_big, w_small=w_small, wa_pad=wa_pad,
        b_a=gla_b_a[l].reshape(1, -1), norm_g=gla_norm_g[l].reshape(1, -1),
        cmp_w1=nsa_cmp_w1[l].astype(BF16), cmp_w2=nsa_cmp_w2[l].astype(BF16),
        cmp_w2t=nsa_cmp_w2[l].transpose(0, 2, 1).astype(BF16),
        cmp_pe8=jnp.broadcast_to(nsa_cmp_pe[l].reshape(2, 1, CMP_LEN * NSA_DH), (2, 16, CMP_LEN * NSA_DH)).astype(BF16),
        w_bg=w_branch_gla[l].astype(BF16), w_bn=w_branch_nsa[l].astype(BF16), w_out=w_out[l].astype(BF16),
        ln_mix_g=ln_mix_g[l].reshape(1, -1), ln_mix_b=ln_mix_b[l].reshape(1, -1),
        xa_wq=xa_wq[l].astype(BF16), xa_wkv=xa_wkv[l].astype(BF16), xa_wo=xa_wo[l].astype(BF16),
        ln_xa_g=ln_xa_g[l].reshape(1, -1), ln_xa_b=ln_xa_b[l].reshape(1, -1),
        rw_hi=rw_hi, rw_lo=rw_lo, rb=router_b.reshape(-1, 1),
        moe_w_in=moe_w_in[l].astype(BF16), moe_w_down=moe_w_down[l].astype(BF16),
        ln_ffn_g=ln_ffn_g[l].reshape(1, -1), ln_ffn_b=ln_ffn_b[l].reshape(1, -1),
    )


def kernel(x, mem, w_in, gla_w_a2, gla_b_a, gla_norm_g, nsa_cmp_pe, nsa_cmp_w1, nsa_cmp_w2, w_branch_gla, w_branch_nsa, w_out, ln_mix_g, ln_mix_b, xa_wq, xa_wkv, xa_wo, ln_xa_g, ln_xa_b, router_w, router_b, moe_w_in, moe_w_down, ln_ffn_g, ln_ffn_b):
    B, T, d = x.shape
    assert T % 512 == 0 and d == 2048 and mem.shape[1] == MEM_LEN
    n = B * T
    params = (w_in, gla_w_a2, gla_b_a, gla_norm_g, nsa_cmp_pe, nsa_cmp_w1, nsa_cmp_w2, w_branch_gla, w_branch_nsa,
              w_out, ln_mix_g, ln_mix_b, xa_wq, xa_wkv, xa_wo, ln_xa_g, ln_xa_b, router_w, router_b, moe_w_in,
              moe_w_down, ln_ffn_g, ln_ffn_b)
    slopes = (2.0 ** (-8.0 * jnp.arange(1, NSA_HEADS + 1, dtype=F32) / NSA_HEADS)).astype(F32)
    nc, ns = T // CMP_STRIDE, T // SEL_LEN
    cs = np.arange(nc) * CMP_STRIDE
    ss = np.arange(ns) * SEL_LEN
    ovt = ((cs[None, :] < ss[:, None] + SEL_LEN) & (cs[None, :] + CMP_LEN > ss[:, None])
           & (cs[None, :] + CMP_LEN <= T)).astype(np.float32)
    sel_onehot = (np.arange(T)[:, None] // SEL_LEN == np.arange(ns)[None, :]).astype(np.float32)
    consts = (slopes, jnp.asarray(ovt, BF16), jnp.asarray(sel_onehot, BF16))

    xf = x.reshape(n, d)
    xb = xf.astype(BF16)
    mem_b = mem.reshape(B * MEM_LEN, d).astype(BF16)
    for l in range(DEPTH):
        p = _prep_layer(l, *params)
        xf, xb = _layer(xf, xb, mem_b, p, consts, B, T)
    return xf.reshape(B, T, d)
```

```python
import functools

import jax
import jax.numpy as jnp
import numpy as np
from jax import lax
from jax.experimental import pallas as pl
from jax.experimental.pallas import tpu as pltpu

F32 = jnp.float32
BF16 = jnp.bfloat16
I32 = jnp.int32

DEPTH = 2
MEM_LEN = 256
GLA_HEADS = 4
GLA_DK = 128
GLA_DV = 256
GLA_GATE_RANK = 16
GLA_TAU = 16.0
GLA_CHUNK = 64
NSA_HEADS = 16
NSA_GROUPS = 4
NSA_HPG = NSA_HEADS // NSA_GROUPS
NSA_DH = 64
CMP_LEN = 32
CMP_STRIDE = 16
CMP_HIDDEN = 256
SEL_LEN = 64
SEL_TOPN = 8
WINDOW = 512
XA_HEADS = 4
XA_DH = 128
N_EXPERTS = 16
N_GROUPS = 4
EXPERTS_PER_GROUP = N_EXPERTS // N_GROUPS
TOP_K = 2
D_FF = 1536
DN_ALPHA = float((2 * DEPTH) ** 0.25)
LN_EPS = 1e-5
NEG = -1e30
FORCE_BONUS = 1e6

GLA_QK = GLA_HEADS * GLA_DK
GLA_V = GLA_HEADS * GLA_DV
NSA_Q = NSA_HEADS * NSA_DH
NSA_KV = NSA_GROUPS * NSA_DH

LANES = 128
VMEM_LIMIT = 56 * 1024 * 1024

COL_MG = 0
COL_GQ = 2 * 2048
COL_GK = COL_GQ + GLA_QK
COL_GV = COL_GK + GLA_QK
COL_GR = COL_GV + GLA_V
COL_NQ = COL_GR + GLA_V
COL_NKV = COL_NQ + NSA_Q
COL_END = COL_NKV + 6 * NSA_KV

MOE_BLOCK = 512
FF_TILE = 512


def _cp(sem):
    return pltpu.CompilerParams(dimension_semantics=sem, vmem_limit_bytes=VMEM_LIMIT)


def _dot(a, b):
    return jnp.dot(a, b, preferred_element_type=F32)


def _dot_nt(a, b):
    return lax.dot_general(a, b, (((1,), (1,)), ((), ())), preferred_element_type=F32)


def _dot_tn(a, b):
    return lax.dot_general(a, b, (((0,), (0,)), ((), ())), preferred_element_type=F32)


def _layer_norm(z, g, b):
    mu = jnp.mean(z, axis=-1, keepdims=True)
    zc = z - mu
    var = jnp.mean(zc * zc, axis=-1, keepdims=True)
    return zc * lax.rsqrt(var + LN_EPS) * g + b


def _mm_kernel(a_ref, b_ref, o_ref):
    o_ref[...] = _dot(a_ref[...], b_ref[...]).astype(o_ref.dtype)


def _matmul(a, b, out_dtype, tm, tn):
    m, k = a.shape
    n = b.shape[1]
    return pl.pallas_call(
        _mm_kernel,
        grid=(m // tm, n // tn),
        in_specs=[pl.BlockSpec((tm, k), lambda i, j: (i, 0)),
                  pl.BlockSpec((k, tn), lambda i, j: (0, j))],
        out_specs=pl.BlockSpec((tm, tn), lambda i, j: (i, j)),
        out_shape=jax.ShapeDtypeStruct((m, n), out_dtype),
        compiler_params=_cp(("parallel", "parallel")),
        name="matmul",
    )(a, b)


def _gla_kernel(q_ref, k_ref, v_ref, r_ref, sm_ref, wa_ref, ba_ref, ng_ref, o_ref, st_ref):
    C = GLA_CHUNK
    n_chunks = q_ref.shape[0] // C
    st_ref[...] = jnp.zeros_like(st_ref)
    rowi = lax.broadcasted_iota(I32, (C, GLA_DK), 0)
    tt = lax.broadcasted_iota(I32, (C, C), 0)
    ss = lax.broadcasted_iota(I32, (C, C), 1)
    levels = (1, 2, 4, 8, 16, 32)
    pair_masks = [((tt // (2 * L)) == (ss // (2 * L))) & ((tt & L) != 0) & ((ss & L) == 0) for L in levels]
    diag_mask = tt == ss
    scale = GLA_DK ** -0.5

    def head_chunk(rows, h, z):
        qk_cols = slice(h * GLA_DK, (h + 1) * GLA_DK)
        v_cols = slice(h * GLA_DV, (h + 1) * GLA_DV)
        q = q_ref[rows, qk_cols].astype(F32) * scale
        k = k_ref[rows, qk_cols].astype(F32)
        v = v_ref[rows, v_cols]
        g = (jnp.minimum(z, 0.0) - jnp.log1p(jnp.exp(-jnp.abs(z)))) * (1.0 / GLA_TAU)
        incl = g
        tot = g
        att = jnp.where(diag_mask, _dot_nt(q.astype(BF16), k.astype(BF16)), 0.0)
        for L, pm in zip(levels, pair_masks):
            ql = (q * jnp.exp(incl)).astype(BF16)
            kl = (k * jnp.exp(tot - incl)).astype(BF16)
            att = jnp.where(pm, _dot_nt(ql, kl), att)
            upper = (rowi & L) != 0
            from_lower = pltpu.roll(tot, L, 0)
            from_upper = pltpu.roll(tot, C - L, 0)
            incl = incl + jnp.where(upper, from_lower, 0.0)
            tot = tot + jnp.where(upper, from_lower, from_upper)
        qd = (q * jnp.exp(incl)).astype(BF16)
        kd = (k * jnp.exp(tot - incl)).astype(BF16)
        st = st_ref[h]
        o = _dot_nt(qd, st.astype(BF16)) + _dot(att.astype(BF16), v)
        st_ref[h] = st * jnp.exp(tot[0:1, :]) + _dot_tn(v, kd)
        mu = jnp.mean(o, axis=-1, keepdims=True)
        oc = o - mu
        var = jnp.mean(oc * oc, axis=-1, keepdims=True)
        on = oc * lax.rsqrt(var + LN_EPS) * ng_ref[:, v_cols]
        r = r_ref[rows, v_cols].astype(F32)
        o_ref[rows, v_cols] = (on * (r * jax.nn.sigmoid(r))).astype(o_ref.dtype)

    def chunk(c, carry):
        rows = pl.ds(pl.multiple_of(c * C, C), C)
        z = _dot(sm_ref[rows, :].astype(BF16), wa_ref[...]) + ba_ref[...]
        for h in range(GLA_HEADS):
            head_chunk(rows, h, z[:, h * GLA_DK:(h + 1) * GLA_DK])
        return carry

    lax.fori_loop(0, n_chunks, chunk, 0)


def _gla(h_big, h_small, wa_pad, b_a, norm_g, B, T):
    n = B * T
    return pl.pallas_call(
        _gla_kernel,
        grid=(B,),
        in_specs=[
            pl.BlockSpec((T, GLA_QK), lambda b: (b, COL_GQ // GLA_QK)),
            pl.BlockSpec((T, GLA_QK), lambda b: (b, COL_GK // GLA_QK)),
            pl.BlockSpec((T, GLA_V), lambda b: (b, COL_GV // GLA_V)),
            pl.BlockSpec((T, GLA_V), lambda b: (b, COL_GR // GLA_V)),
            pl.BlockSpec((T, LANES), lambda b: (b, 0)),
            pl.BlockSpec((LANES, GLA_QK), lambda b: (0, 0)),
            pl.BlockSpec((1, GLA_QK), lambda b: (0, 0)),
            pl.BlockSpec((1, GLA_V), lambda b: (0, 0)),
        ],
        out_specs=pl.BlockSpec((T, GLA_V), lambda b: (b, 0)),
        out_shape=jax.ShapeDtypeStruct((n, GLA_V), BF16),
        scratch_shapes=[pltpu.VMEM((GLA_HEADS, GLA_DV, GLA_DK), F32)],
        compiler_params=_cp(("parallel",)),
        name="gla",
    )(h_big, h_big, h_big, h_big, h_small, wa_pad, b_a, norm_g)


def _compress_kernel(u_ref, w1_ref, w2_ref, w2t_ref, pe_ref, o_ref, ot_ref):
    half = CMP_STRIDE * NSA_DH
    u = u_ref[0, 0, 0]
    nrow = u.shape[0]
    a = _dot(u, w1_ref[0, 0:half, :])
    bm = _dot(u, w1_ref[0, half:2 * half, :])
    c = _dot(pe_ref[0], w1_ref[0])
    hid = a + pltpu.roll(bm, nrow - 1, 0) + c[0:1, :]
    act = jax.nn.gelu(hid).astype(BF16)
    o_ref[0, 0, 0] = _dot(act, w2_ref[0]).astype(o_ref.dtype)
    ot_ref[0, 0, 0] = _dot_nt(w2t_ref[0], act).astype(ot_ref.dtype)


def _compress(kv_t, w1, w2, w2t, pe8, B, T):
    nc = T // CMP_STRIDE
    u = kv_t.reshape(B, 6, NSA_GROUPS, nc, CMP_STRIDE * NSA_DH)
    return pl.pallas_call(
        _compress_kernel,
        grid=(B, 2, NSA_GROUPS),
        in_specs=[
            pl.BlockSpec((1, 1, 1, nc, CMP_STRIDE * NSA_DH), lambda b, s, g: (b, s, g, 0, 0)),
            pl.BlockSpec((1, CMP_LEN * NSA_DH, CMP_HIDDEN), lambda b, s, g: (s, 0, 0)),
            pl.BlockSpec((1, CMP_HIDDEN, NSA_DH), lambda b, s, g: (s, 0, 0)),
            pl.BlockSpec((1, NSA_DH, CMP_HIDDEN), lambda b, s, g: (s, 0, 0)),
            pl.BlockSpec((1, 16, CMP_LEN * NSA_DH), lambda b, s, g: (s, 0, 0)),
        ],
        out_specs=[
            pl.BlockSpec((1, 1, 1, nc, NSA_DH), lambda b, s, g: (b, s, g, 0, 0)),
            pl.BlockSpec((1, 1, 1, NSA_DH, nc), lambda b, s, g: (b, s, g, 0, 0)),
        ],
        out_shape=[jax.ShapeDtypeStruct((B, 2, NSA_GROUPS, nc, NSA_DH), BF16),
                   jax.ShapeDtypeStruct((B, 2, NSA_GROUPS, NSA_DH, nc), BF16)],
        compiler_params=_cp(("parallel", "parallel", "parallel")),
        name="nsa_compress",
    )(u, w1, w2, w2t, pe8)


def _cmp_select_kernel(slopes_ref, qt_ref, kc_ref, vct_ref, ovt_ref, ocmp_ref, mb_ref):
    g = pl.program_id(1)
    i = pl.program_id(2)
    tq = qt_ref.shape[3]
    nc = kc_ref.shape[3]
    ns = mb_ref.shape[2]
    t0 = i * tq
    tpos = (t0 + lax.broadcasted_iota(I32, (nc, tq), 1))
    nidx = lax.broadcasted_iota(I32, (nc, tq), 0)
    mask_c = (nidx * CMP_STRIDE + (CMP_LEN - 1)) <= tpos
    absd = jnp.abs(tpos.astype(F32) - (nidx.astype(F32) * CMP_STRIDE + 0.5 * (CMP_LEN - 1)))
    kc = kc_ref[0, 0, 0]
    vct = vct_ref[0, 0, 0]
    psum = jnp.zeros((nc, tq), F32)
    for hh in range(NSA_HPG):
        slope = slopes_ref[g * NSA_HPG + hh]
        q = qt_ref[0, hh] * jnp.asarray(NSA_DH ** -0.5, BF16)
        s = _dot(kc, q) - slope * absd
        s = jnp.where(mask_c, s, NEG)
        e = jnp.exp(s - jnp.max(s, axis=0, keepdims=True))
        p = jnp.where(mask_c, e / jnp.sum(e, axis=0, keepdims=True), 0.0)
        ocmp_ref[0, hh] = _dot(vct, p.astype(BF16)).astype(ocmp_ref.dtype)
        psum = psum + p
    p_hi = psum.astype(BF16)
    p_lo = (psum - p_hi.astype(F32)).astype(BF16)
    imp = _dot(ovt_ref[...], p_hi) + _dot(ovt_ref[...], p_lo)
    j = lax.broadcasted_iota(I32, (ns, tq), 0)
    tp = t0 + lax.broadcasted_iota(I32, (ns, tq), 1)
    cur = tp // SEL_LEN
    forced = (j == 0) | (j == cur) | (j == cur - 1)
    valid = j * SEL_LEN <= tp
    score = jnp.where(valid, imp + jnp.where(forced, FORCE_BONUS, 0.0), NEG)
    rank = jnp.zeros((ns, tq), F32)
    for jp in range(ns):
        row = score[jp:jp + 1, :]
        beats = (row > score) | ((row == score) & (j > jp))
        rank = rank + jnp.where(beats, 1.0, 0.0)
    keep = valid & (rank < float(min(SEL_TOPN, ns)))
    mb_ref[0, 0] = jnp.where(keep, 0.0, NEG).astype(mb_ref.dtype)


def _cmp_select(slopes, q_t, kcmp, kcmp_t, ovt, B, T, tq):
    nc = T // CMP_STRIDE
    ns = T // SEL_LEN
    grid_spec = pltpu.PrefetchScalarGridSpec(
        num_scalar_prefetch=1,
        grid=(B, NSA_GROUPS, T // tq),
        in_specs=[
            pl.BlockSpec((1, NSA_HPG, NSA_DH, tq), lambda b, g, i, s: (b, g, 0, i)),
            pl.BlockSpec((1, 1, 1, nc, NSA_DH), lambda b, g, i, s: (b, 0, g, 0, 0)),
            pl.BlockSpec((1, 1, 1, NSA_DH, nc), lambda b, g, i, s: (b, 1, g, 0, 0)),
            pl.BlockSpec((ns, nc), lambda b, g, i, s: (0, 0)),
        ],
        out_specs=[
            pl.BlockSpec((1, NSA_HPG, NSA_DH, tq), lambda b, g, i, s: (b, g, 0, i)),
            pl.BlockSpec((1, 1, ns, tq), lambda b, g, i, s: (b, g, 0, i)),
        ],
    )
    return pl.pallas_call(
        _cmp_select_kernel,
        grid_spec=grid_spec,
        out_shape=[jax.ShapeDtypeStruct((B, NSA_HEADS, NSA_DH, T), BF16),
                   jax.ShapeDtypeStruct((B, NSA_GROUPS, ns, T), BF16)],
        compiler_params=_cp(("parallel", "parallel", "parallel")),
        name="nsa_cmp_select",
    )(slopes, q_t, kcmp, kcmp_t, ovt)


def _sel_win_kernel(slopes_ref, qt_ref, k_ref, vt_ref, mb_ref, ocmp_ref, gt_ref,
                    o_ref, qaug_ref, m_ref, l_ref, acc_ref, srow_ref, bias_ref, s_ref, p_ref, alpha_ref):
    g = pl.program_id(1)
    i = pl.program_id(2)
    tq = qt_ref.shape[3]
    tk = tq
    ns = mb_ref.shape[2]
    dh = NSA_DH
    wide = NSA_HPG * tq
    t0 = i * tq
    BIG = -NEG

    @pl.when(i == 0)
    def _():
        srow = jnp.concatenate([jnp.full((1, tq), slopes_ref[g * NSA_HPG + hh], F32) for hh in range(NSA_HPG)],
                               axis=1)
        srow_ref[...] = srow
        lane = lax.broadcasted_iota(I32, (tk, wide), 1) & (tq - 1)
        dist0 = (lane - lax.broadcasted_iota(I32, (tk, wide), 0)).astype(F32)
        sd0 = srow * dist0
        bias_ref[0] = sd0
        bias_ref[1] = sd0 + jnp.where(dist0 >= 0.0, 0.0, BIG)
        bias_ref[2] = sd0 + jnp.where(dist0 < 0.0, 0.0, BIG)
        bias_ref[3] = jnp.full((tk, wide), BIG, F32)

    for hh in range(NSA_HPG):
        cols = slice(hh * tq, (hh + 1) * tq)
        qaug_ref[0:dh, cols] = qt_ref[0, hh] * jnp.asarray(dh ** -0.5, BF16)
        qaug_ref[dh:dh + ns, cols] = mb_ref[0, 0]
        qaug_ref[dh + ns:, cols] = jnp.zeros((qaug_ref.shape[0] - dh - ns, tq), BF16)

    m_ref[...] = jnp.full(m_ref.shape, NEG, F32)
    l_ref[...] = jnp.zeros(l_ref.shape, F32)
    acc_ref[...] = jnp.zeros(acc_ref.shape, F32)

    n_back = WINDOW // tk
    n_sel = i + 1
    n_win = jnp.minimum(i, n_back) + 1
    n_steps = n_sel + n_win

    def describe(n):
        n = jnp.maximum(n, 0)
        is_win = n >= n_sel
        kb = jnp.clip(jnp.where(is_win, i - n_win + 1 + (n - n_sel), n), 0, i)
        mode = jnp.where(kb == i, 1, jnp.where(is_win & (kb == i - n_back), 2, 0))
        mode = jnp.where(n >= n_steps, 3, mode)
        return is_win.astype(I32), kb, mode

    def scores(n, slot):
        br, kb, _ = describe(n)
        s0 = pl.multiple_of(kb * tk, tk)
        s_ref[slot] = _dot(k_ref[0, br, 0, pl.ds(s0, tk), :], qaug_ref[...])

    def softmax(n, slot):
        br, kb, mode = describe(n)
        crow = srow_ref[...] * ((i - kb) * tk).astype(F32)
        s = s_ref[slot] - bias_ref[mode]
        m_old = m_ref[br]
        m_new = jnp.maximum(m_old, jnp.max(s, axis=0, keepdims=True) - crow)
        alpha = jnp.exp(m_old - m_new)
        p = jnp.exp(s - (m_new + crow))
        l_ref[br] = alpha * l_ref[br] + jnp.sum(p, axis=0, keepdims=True)
        m_ref[br] = m_new
        alpha_ref[slot] = alpha
        p_ref[slot] = p.astype(BF16)

    def weighted_values(n, slot):
        br, kb, _ = describe(n)
        s0 = pl.multiple_of(kb * tk, tk)
        acc_ref[br] = alpha_ref[slot] * acc_ref[br] + _dot(vt_ref[0, br, 0, :, pl.ds(s0, tk)], p_ref[slot])

    p_ref[1] = jnp.zeros(p_ref.shape[1:], BF16)
    alpha_ref[1] = jnp.ones(alpha_ref.shape[1:], F32)
    scores(0, 0)

    def pair(j, carry):
        n = 2 * j
        scores(n + 1, 1)
        softmax(n, 0)
        weighted_values(n - 1, 1)
        scores(n + 2, 0)
        softmax(n + 1, 1)
        weighted_values(n, 0)
        return carry

    n_pairs = (n_steps + 1) // 2
    lax.fori_loop(0, n_pairs, pair, 0)
    weighted_values(2 * n_pairs - 1, 1)

    def gate_row(branch):
        rows = [gt_ref[0, 0, 3 * hh + branch:3 * hh + branch + 1, :] for hh in range(NSA_HPG)]
        return jax.nn.sigmoid(jnp.concatenate(rows, axis=1))

    o = gate_row(1) * (acc_ref[0] / l_ref[0]) + gate_row(2) * (acc_ref[1] / l_ref[1])
    ocmp = jnp.concatenate([ocmp_ref[0, hh] for hh in range(NSA_HPG)], axis=1).astype(F32)
    o = o + gate_row(0) * ocmp
    for hh in range(NSA_HPG):
        o_ref[0, hh] = o[:, hh * tq:(hh + 1) * tq].astype(o_ref.dtype)


def _sel_win(slopes, q_t, k_all, vt_all, mb, ocmp_t, gates_t, B, T, tq):
    ns = T // SEL_LEN
    kaug = k_all.shape[-1]
    G = NSA_GROUPS
    grid_spec = pltpu.PrefetchScalarGridSpec(
        num_scalar_prefetch=1,
        grid=(B, G, T // tq),
        in_specs=[
            pl.BlockSpec((1, NSA_HPG, NSA_DH, tq), lambda b, g, i, s: (b, g, 0, i)),
            pl.BlockSpec((1, 2, 1, T, kaug), lambda b, g, i, s: (b, 0, g, 0, 0)),
            pl.BlockSpec((1, 2, 1, NSA_DH, T), lambda b, g, i, s: (b, 0, g, 0, 0)),
            pl.BlockSpec((1, 1, ns, tq), lambda b, g, i, s: (b, g, 0, i)),
            pl.BlockSpec((1, NSA_HPG, NSA_DH, tq), lambda b, g, i, s: (b, g, 0, i)),
            pl.BlockSpec((1, 1, 16, tq), lambda b, g, i, s: (b, g, 0, i)),
        ],
        out_specs=pl.BlockSpec((1, NSA_HPG, NSA_DH, tq), lambda b, g, i, s: (b, g, 0, i)),
        scratch_shapes=[
            pltpu.VMEM((kaug, NSA_HPG * tq), BF16),
            pltpu.VMEM((2, 1, NSA_HPG * tq), F32),
            pltpu.VMEM((2, 1, NSA_HPG * tq), F32),
            pltpu.VMEM((2, NSA_DH, NSA_HPG * tq), F32),
            pltpu.VMEM((1, NSA_HPG * tq), F32),
            pltpu.VMEM((4, tq, NSA_HPG * tq), F32),
            pltpu.VMEM((2, tq, NSA_HPG * tq), F32),
            pltpu.VMEM((2, tq, NSA_HPG * tq), BF16),
            pltpu.VMEM((2, 1, NSA_HPG * tq), F32),
        ],
    )
    return pl.pallas_call(
        _sel_win_kernel,
        grid_spec=grid_spec,
        out_shape=jax.ShapeDtypeStruct((B, NSA_HEADS, NSA_DH, T), BF16),
        compiler_params=_cp(("parallel", "parallel", "arbitrary")),
        name="nsa_sel_win",
    )(slopes, q_t, k_all, vt_all, mb, ocmp_t, gates_t)


def _mix_kernel(og_ref, on_ref, mg1_ref, mg2_ref, x_ref, wg_ref, wn_ref, wo_ref, lg_ref, lb_ref,
                x1_ref, x1b_ref):
    g1 = _dot(og_ref[...], wg_ref[...])
    g2 = _dot(on_ref[...], wn_ref[...])
    merged = (jax.nn.sigmoid(mg1_ref[...].astype(F32)) * g1
              + jax.nn.sigmoid(mg2_ref[...].astype(F32)) * g2)
    y = _dot(merged.astype(BF16), wo_ref[...])
    x1 = _layer_norm(DN_ALPHA * x_ref[...] + y, lg_ref[...], lb_ref[...])
    x1_ref[...] = x1
    x1b_ref[...] = x1.astype(BF16)


def _const_spec(shape):
    nd = len(shape)
    return pl.BlockSpec(shape, lambda *_: (0,) * nd, pipeline_mode=pl.Buffered(1))


def _mix(o_gla, o_nsa, h_big, x, wg, wn, wo, lg, lb, tm):
    n, d = x.shape
    return pl.pallas_call(
        _mix_kernel,
        grid=(n // tm,),
        in_specs=[
            pl.BlockSpec((tm, GLA_V), lambda i: (i, 0)),
            pl.BlockSpec((tm, NSA_Q), lambda i: (i, 0)),
            pl.BlockSpec((tm, d), lambda i: (i, 0)),
            pl.BlockSpec((tm, d), lambda i: (i, 1)),
            pl.BlockSpec((tm, d), lambda i: (i, 0)),
            _const_spec(wg.shape), _const_spec(wn.shape), _const_spec(wo.shape),
            _const_spec(lg.shape), _const_spec(lb.shape),
        ],
        out_specs=[pl.BlockSpec((tm, d), lambda i: (i, 0)), pl.BlockSpec((tm, d), lambda i: (i, 0))],
        out_shape=[jax.ShapeDtypeStruct((n, d), F32), jax.ShapeDtypeStruct((n, d), BF16)],
        compiler_params=_cp(("parallel",)),
        name="mix_ln",
    )(o_gla, o_nsa, h_big, h_big, x, wg, wn, wo, lg, lb)


def _xattn_kernel(x_ref, xb_ref, kv_ref, wq_ref, wo_ref, lg_ref, lb_ref, x2_ref):
    hd = XA_HEADS * XA_DH
    q = (_dot(xb_ref[...], wq_ref[...]) * (XA_DH ** -0.5)).astype(BF16)
    outs = []
    for h in range(XA_HEADS):
        kh = kv_ref[0, :, h * XA_DH:(h + 1) * XA_DH]
        vh = kv_ref[0, :, hd + h * XA_DH:hd + (h + 1) * XA_DH]
        s = _dot_nt(q[:, h * XA_DH:(h + 1) * XA_DH], kh)
        e = jnp.exp(s - jnp.max(s, axis=-1, keepdims=True))
        p = e / jnp.sum(e, axis=-1, keepdims=True)
        outs.append(_dot(p.astype(BF16), vh).astype(BF16))
    o = jnp.concatenate(outs, axis=-1)
    y = _dot(o, wo_ref[...])
    x2_ref[...] = _layer_norm(DN_ALPHA * x_ref[...] + y, lg_ref[...], lb_ref[...])


def _xattn(x1, x1b, kv, wq, wo, lg, lb, B, T, tm):
    n, d = x1.shape
    nt = T // tm
    return pl.pallas_call(
        _xattn_kernel,
        grid=(B, nt),
        in_specs=[
            pl.BlockSpec((tm, d), lambda b, i: (b * nt + i, 0)),
            pl.BlockSpec((tm, d), lambda b, i: (b * nt + i, 0)),
            pl.BlockSpec((1,) + kv.shape[1:], lambda b, i: (b, 0, 0)),
            _const_spec(wq.shape), _const_spec(wo.shape), _const_spec(lg.shape), _const_spec(lb.shape),
        ],
        out_specs=pl.BlockSpec((tm, d), lambda b, i: (b * nt + i, 0)),
        out_shape=jax.ShapeDtypeStruct((n, d), F32),
        compiler_params=_cp(("parallel", "parallel")),
        name="xattn_ln",
    )(x1, x1b, kv, wq, wo, lg, lb)


def _router_kernel(x_ref, wh_ref, wl_ref, rb_ref, e_ref, gate_ref, rank_ref, cnt_ref, carry_ref):
    i = pl.program_id(0)
    tr = x_ref.shape[0]
    E = N_EXPERTS

    @pl.when(i == 0)
    def _():
        carry_ref[...] = jnp.zeros_like(carry_ref)

    x = x_ref[...]
    x_hi = x.astype(BF16)
    x_lo = (x - x_hi.astype(F32)).astype(BF16)
    wh = wh_ref[...]
    logits = _dot_nt(wh, x_hi) + _dot_nt(wh, x_lo) + _dot_nt(wl_ref[...], x_hi)
    biased = logits + rb_ref[...]
    rows = [biased[e:e + 1, :] for e in range(E)]
    raw = [logits[e:e + 1, :] for e in range(E)]
    best_score = None
    best = None
    for gi in range(N_GROUPS):
        v = rows[gi * EXPERTS_PER_GROUP:(gi + 1) * EXPERTS_PER_GROUP]
        sc = None
        for a in range(EXPERTS_PER_GROUP):
            for b in range(a + 1, EXPERTS_PER_GROUP):
                pair = v[a] + v[b]
                sc = pair if sc is None else jnp.maximum(sc, pair)
        if best is None:
            best_score, best = sc, jnp.zeros((1, tr), I32)
        else:
            better = sc > best_score
            best_score = jnp.where(better, sc, best_score)
            best = jnp.where(better, gi, best)

    def pick(vals):
        out = vals[0:EXPERTS_PER_GROUP]
        for gi in range(1, N_GROUPS):
            out = [jnp.where(best == gi, vals[gi * EXPERTS_PER_GROUP + a], out[a]) for a in range(EXPERTS_PER_GROUP)]
        return out

    w = pick(rows)
    lraw = pick(raw)
    i1 = jnp.zeros((1, tr), I32)
    v1 = w[0]
    l1 = lraw[0]
    for a in range(1, EXPERTS_PER_GROUP):
        better = w[a] > v1
        v1 = jnp.where(better, w[a], v1)
        l1 = jnp.where(better, lraw[a], l1)
        i1 = jnp.where(better, a, i1)
    i2 = jnp.full((1, tr), -1, I32)
    v2 = jnp.full((1, tr), -jnp.inf, F32)
    l2 = jnp.zeros((1, tr), F32)
    for a in range(EXPERTS_PER_GROUP):
        better = (i1 != a) & ((w[a] > v2) | (i2 < 0))
        v2 = jnp.where(better, w[a], v2)
        l2 = jnp.where(better, lraw[a], l2)
        i2 = jnp.where(better, a, i2)
    e1 = best * EXPERTS_PER_GROUP + i1
    e2 = best * EXPERTS_PER_GROUP + i2
    mx = jnp.maximum(l1, l2)
    p1 = jnp.exp(l1 - mx)
    p2 = jnp.exp(l2 - mx)
    den = p1 + p2
    e_ref[0:1, :] = e1
    e_ref[1:2, :] = e2
    gate_ref[0:1, :] = p1 / den
    gate_ref[1:2, :] = p2 / den
    eidx = lax.broadcasted_iota(I32, (E, tr), 0)
    is1 = eidx == e1
    is2 = eidx == e2
    member = jnp.where(is1 | is2, 1.0, 0.0)
    uu = lax.broadcasted_iota(I32, (tr, tr), 0)
    tt = lax.broadcasted_iota(I32, (tr, tr), 1)
    tri = jnp.where(uu <= tt, 1.0, 0.0).astype(BF16)
    incl = _dot(member.astype(BF16), tri)
    excl = carry_ref[:, 0:1] + incl - member
    rank_ref[0:1, :] = jnp.sum(jnp.where(is1, excl, 0.0), axis=0, keepdims=True).astype(I32)
    rank_ref[1:2, :] = jnp.sum(jnp.where(is2, excl, 0.0), axis=0, keepdims=True).astype(I32)
    new_carry = carry_ref[...] + jnp.sum(member, axis=1, keepdims=True)
    carry_ref[...] = new_carry
    cnt_ref[...] = new_carry


def _router(x2, rw_hi, rw_lo, rb, tr):
    n, d = x2.shape
    E = N_EXPERTS
    return pl.pallas_call(
        _router_kernel,
        grid=(n // tr,),
        in_specs=[
            pl.BlockSpec((tr, d), lambda i: (i, 0)),
            pl.BlockSpec((E, d), lambda i: (0, 0)),
            pl.BlockSpec((E, d), lambda i: (0, 0)),
            pl.BlockSpec((E, 1), lambda i: (0, 0)),
        ],
        out_specs=[
            pl.BlockSpec((2, tr), lambda i: (0, i)),
            pl.BlockSpec((2, tr), lambda i: (0, i)),
            pl.BlockSpec((2, tr), lambda i: (0, i)),
            pl.BlockSpec((E, LANES), lambda i: (0, 0)),
        ],
        out_shape=[jax.ShapeDtypeStruct((2, n), I32), jax.ShapeDtypeStruct((2, n), F32),
                   jax.ShapeDtypeStruct((2, n), I32), jax.ShapeDtypeStruct((E, LANES), F32)],
        scratch_shapes=[pltpu.VMEM((E, LANES), F32)],
        compiler_params=_cp(("arbitrary",)),
        name="moe_router",
    )(x2, rw_hi, rw_lo, rb)


def _dispatch_kernel(ps_ref, pe_ref, e_ref, rank_ref, x_ref, buf_hbm, zero_ref, sem):
    td = e_ref.shape[1]

    @pl.when(pl.program_id(0) == 0)
    def _():
        zero_ref[...] = jnp.zeros_like(zero_ref)

        def zero_copy(ex):
            last = pl.multiple_of(jnp.maximum(pe_ref[ex] - MOE_BLOCK, 0), MOE_BLOCK)
            return pltpu.make_async_copy(zero_ref, buf_hbm.at[pl.ds(last, MOE_BLOCK), :], sem)

        def nonempty(ex):
            return pe_ref[ex] > (pe_ref[ex - 1] if ex > 0 else 0)

        n_blocks = buf_hbm.shape[0] // MOE_BLOCK
        first_unused = pe_ref[N_EXPERTS - 1] // MOE_BLOCK

        def tail_copy(k):
            row = pl.multiple_of((first_unused + k) * MOE_BLOCK, MOE_BLOCK)
            return pltpu.make_async_copy(zero_ref, buf_hbm.at[pl.ds(row, MOE_BLOCK), :], sem)

        for ex in range(N_EXPERTS):
            pl.when(nonempty(ex))(lambda ex=ex: zero_copy(ex).start())
            pl.when(first_unused + ex < n_blocks)(lambda ex=ex: tail_copy(ex).start())
        for ex in range(N_EXPERTS):
            pl.when(nonempty(ex))(lambda ex=ex: zero_copy(ex).wait())
            pl.when(first_unused + ex < n_blocks)(lambda ex=ex: tail_copy(ex).wait())

    def issue(t, carry):
        for kk in range(TOP_K):
            dest = ps_ref[e_ref[kk, t]] + rank_ref[kk, t]
            pltpu.make_async_copy(x_ref.at[pl.ds(t, 1), :], buf_hbm.at[pl.ds(dest, 1), :], sem).start()
        return carry

    lax.fori_loop(0, td, issue, 0, unroll=8)
    for kk in range(TOP_K):
        pltpu.make_async_copy(x_ref, buf_hbm.at[pl.ds(0, td), :], sem).wait()


def _dispatch(pad_start, pad_end, e, rank, x2, n_rows, td):
    n, d = x2.shape
    grid_spec = pltpu.PrefetchScalarGridSpec(
        num_scalar_prefetch=2,
        grid=(n // td,),
        in_specs=[
            pl.BlockSpec((2, td), lambda i, s, t: (0, i), memory_space=pltpu.SMEM),
            pl.BlockSpec((2, td), lambda i, s, t: (0, i), memory_space=pltpu.SMEM),
            pl.BlockSpec((td, d), lambda i, s, t: (i, 0)),
        ],
        out_specs=pl.BlockSpec(memory_space=pl.ANY),
        scratch_shapes=[pltpu.VMEM((MOE_BLOCK, d), F32), pltpu.SemaphoreType.DMA(())],
    )
    return pl.pallas_call(
        _dispatch_kernel,
        grid_spec=grid_spec,
        out_shape=jax.ShapeDtypeStruct((n_rows, d), F32),
        compiler_params=_cp(("arbitrary",)),
        name="moe_dispatch",
    )(pad_start, pad_end, e, rank, x2)


def _expert_kernel(be_ref, nb_ref, x_ref, wa_ref, wu_ref, wd_ref, y_ref, xb_ref):
    blk = pl.program_id(0)
    f = pl.program_id(1)

    @pl.when((blk >= nb_ref[0]) & (f == 0))
    def _():
        y_ref[...] = jnp.zeros_like(y_ref)

    @pl.when(blk < nb_ref[0])
    def _():
        @pl.when(f == 0)
        def _():
            xb_ref[...] = x_ref[...].astype(BF16)

        xb = xb_ref[...]
        a = _dot(xb, wa_ref[0, 0])
        u = _dot(xb, wu_ref[0, 0])
        act = (a * jax.nn.sigmoid(a) * u).astype(BF16)
        y = _dot(act, wd_ref[0, 0])

        @pl.when(f == 0)
        def _():
            y_ref[...] = y

        @pl.when(f != 0)
        def _():
            y_ref[...] += y


def _experts(blk_expert, n_used, buf, w_in, w_down, layer):
    p, d = buf.shape
    nb = p // MOE_BLOCK
    nf = D_FF // FF_TILE
    grid_spec = pltpu.PrefetchScalarGridSpec(
        num_scalar_prefetch=2,
        grid=(nb, nf),
        in_specs=[
            pl.BlockSpec((MOE_BLOCK, d), lambda b, f, be, nu: (jnp.minimum(b, nu[0] - 1), 0)),
            pl.BlockSpec((1, 1, d, FF_TILE), lambda b, f, be, nu: (layer, be[b], 0, f)),
            pl.BlockSpec((1, 1, d, FF_TILE), lambda b, f, be, nu: (layer, be[b], 0, nf + f)),
            pl.BlockSpec((1, 1, FF_TILE, d), lambda b, f, be, nu: (layer, be[b], f, 0)),
        ],
        out_specs=pl.BlockSpec((MOE_BLOCK, d), lambda b, f, be, nu: (b, 0)),
        scratch_shapes=[pltpu.VMEM((MOE_BLOCK, d), BF16)],
    )
    return pl.pallas_call(
        _expert_kernel,
        grid_spec=grid_spec,
        out_shape=jax.ShapeDtypeStruct((p, d), F32),
        compiler_params=_cp(("arbitrary", "arbitrary")),
        name="moe_experts",
    )(blk_expert, n_used, buf, w_in, w_in, w_down)


def _combine_kernel(ps_ref, e_ref, rank_ref, y_hbm, x_ref, gate_ref, lg_ref, lb_ref, x3_ref, x3b_ref,
                    y0_ref, y1_ref, sem):
    tc = x_ref.shape[0]
    bufs = (y0_ref, y1_ref)

    def issue(t, carry):
        for kk in range(TOP_K):
            src = ps_ref[e_ref[kk, t]] + rank_ref[kk, t]
            pltpu.make_async_copy(y_hbm.at[pl.ds(src, 1), :], bufs[kk].at[pl.ds(t, 1), :], sem).start()
        return carry

    lax.fori_loop(0, tc, issue, 0, unroll=8)
    for kk in range(TOP_K):
        pltpu.make_async_copy(y_hbm.at[pl.ds(0, tc), :], bufs[kk], sem).wait()
    gate = gate_ref[...]
    z = DN_ALPHA * x_ref[...] + gate[:, 0:1] * y0_ref[...] + gate[:, 1:2] * y1_ref[...]
    x3 = _layer_norm(z, lg_ref[...], lb_ref[...])
    x3_ref[...] = x3
    x3b_ref[...] = x3.astype(BF16)


def _combine(pad_start, e, rank, y, x2, gate_nt, lg, lb, tc):
    n, d = x2.shape
    grid_spec = pltpu.PrefetchScalarGridSpec(
        num_scalar_prefetch=1,
        grid=(n // tc,),
        in_specs=[
            pl.BlockSpec((2, tc), lambda i, s: (0, i), memory_space=pltpu.SMEM),
            pl.BlockSpec((2, tc), lambda i, s: (0, i), memory_space=pltpu.SMEM),
            pl.BlockSpec(memory_space=pl.ANY),
            pl.BlockSpec((tc, d), lambda i, s: (i, 0)),
            pl.BlockSpec((tc, 2), lambda i, s: (i, 0)),
            pl.BlockSpec((1, d), lambda i, s: (0, 0)),
            pl.BlockSpec((1, d), lambda i, s: (0, 0)),
        ],
        out_specs=[pl.BlockSpec((tc, d), lambda i, s: (i, 0)), pl.BlockSpec((tc, d), lambda i, s: (i, 0))],
        scratch_shapes=[pltpu.VMEM((tc, d), F32), pltpu.VMEM((tc, d), F32), pltpu.SemaphoreType.DMA(())],
    )
    return pl.pallas_call(
        _combine_kernel,
        grid_spec=grid_spec,
        out_shape=[jax.ShapeDtypeStruct((n, d), F32), jax.ShapeDtypeStruct((n, d), BF16)],
        compiler_params=_cp(("arbitrary",)),
        name="moe_combine_ln",
    )(pad_start, e, rank, y, x2, gate_nt, lg, lb)


def _layer(x, xb, mem_b, p, moe_w, layer, consts, B, T):
    n, d = x.shape
    G, HPG, DH = NSA_GROUPS, NSA_HPG, NSA_DH
    slopes, ovt, sel_onehot = consts

    h_big = _matmul(xb, p["w_big"], BF16, 1024, 512)
    h_small = _matmul(xb, p["w_small"], F32, 1024, LANES)

    o_gla = _gla(h_big, h_small, p["wa_pad"], p["b_a"], p["norm_g"], B, T)

    q_t = h_big[:, COL_NQ:COL_NQ + NSA_Q].reshape(B, T, NSA_HEADS, DH).transpose(0, 2, 3, 1)
    kv = h_big[:, COL_NKV:COL_END].reshape(B, T, 6, G, DH)
    kv_rows = kv.transpose(0, 2, 3, 1, 4)
    kv_cols = kv.transpose(0, 2, 3, 4, 1)
    kcmp, kcmp_t = _compress(kv_rows, p["cmp_w1"], p["cmp_w2"], p["cmp_w2t"], p["cmp_pe8"], B, T)
    ocmp_t, mb = _cmp_select(slopes, q_t, kcmp, kcmp_t, ovt, B, T, 512)
    pad = jnp.zeros((B, G, T, LANES - DH - sel_onehot.shape[1]), BF16)
    ks_aug = jnp.concatenate([kv_rows[:, 2], jnp.broadcast_to(sel_onehot, (B, G) + sel_onehot.shape), pad], axis=-1)
    gates_t = h_small[:, GLA_GATE_RANK:GLA_GATE_RANK + 3 * NSA_HEADS].reshape(B, T, G, 3 * HPG)
    gates_t = jnp.pad(gates_t.transpose(0, 2, 3, 1), ((0, 0), (0, 0), (0, 16 - 3 * HPG), (0, 0)))
    kw_pad = jnp.concatenate([kv_rows[:, 4], jnp.zeros((B, G, T, LANES - DH), BF16)], axis=-1)
    k_all = jnp.stack([ks_aug, kw_pad], axis=1)
    vt_all = jnp.stack([kv_cols[:, 3], kv_cols[:, 5]], axis=1)
    o_nsa_t = _sel_win(slopes, q_t, k_all, vt_all, mb, ocmp_t, gates_t, B, T, 256)
    o_nsa = o_nsa_t.transpose(0, 3, 1, 2).reshape(n, NSA_Q)

    x1, x1b = _mix(o_gla, o_nsa, h_big, x, p["w_bg"], p["w_bn"], p["w_out"], p["ln_mix_g"], p["ln_mix_b"], 256)

    kvm = _matmul(mem_b, p["xa_wkv"], BF16, 512, 512).reshape(B, MEM_LEN, 2 * XA_HEADS * XA_DH)
    x2 = _xattn(x1, x1b, kvm, p["xa_wq"], p["xa_wo"], p["ln_xa_g"], p["ln_xa_b"], B, T, 256)

    e, gate, rank, cnt = _router(x2, p["rw_hi"], p["rw_lo"], p["rb"], 512)
    counts = cnt[:, 0].astype(I32)
    padded = (counts + MOE_BLOCK - 1) // MOE_BLOCK * MOE_BLOCK
    pad_end = jnp.cumsum(padded)
    pad_start = (pad_end - padded).astype(I32)
    nb = (n * TOP_K) // MOE_BLOCK + N_EXPERTS
    n_used = (pad_end[-1] // MOE_BLOCK).astype(I32).reshape(1)
    blk_start = jnp.arange(nb, dtype=I32) * MOE_BLOCK
    blk_expert = jnp.minimum(jnp.sum(blk_start[:, None] >= pad_end[None, :], axis=1), N_EXPERTS - 1).astype(I32)
    blk_expert = jnp.where(jnp.arange(nb) < n_used[0], blk_expert, blk_expert[jnp.maximum(n_used[0] - 1, 0)])
    buf = _dispatch(pad_start, pad_end.astype(I32), e, rank, x2, nb * MOE_BLOCK, 512)
    y = _experts(blk_expert, n_used, buf, moe_w[0], moe_w[1], layer)
    x3, x3b = _combine(pad_start, e, rank, y, x2, gate.T, p["ln_ffn_g"], p["ln_ffn_b"], 256)
    return x3, x3b


def _prep_layer(l, w_in, gla_w_a2, gla_b_a, gla_norm_g, nsa_cmp_pe, nsa_cmp_w1, nsa_cmp_w2, w_branch_gla,
                w_branch_nsa, w_out, ln_mix_g, ln_mix_b, xa_wq, xa_wkv, xa_wo, ln_xa_g, ln_xa_b, router_w,
                router_b, moe_w_in, moe_w_down, ln_ffn_g, ln_ffn_b):
    d = w_in.shape[1]
    w = w_in[l]
    o_gq, o_gk, o_gv, o_gr = 0, GLA_QK, 2 * GLA_QK, 2 * GLA_QK + GLA_V
    o_ga = o_gr + GLA_V
    o_nq = o_ga + GLA_GATE_RANK
    o_nkv = o_nq + NSA_Q
    o_ng = o_nkv + 6 * NSA_KV
    o_mg = o_ng + 3 * NSA_HEADS
    w_big = jnp.concatenate([w[:, o_mg:o_mg + 2 * d], w[:, o_gq:o_ga], w[:, o_nq:o_ng]], axis=1).astype(BF16)
    w_small = jnp.concatenate([w[:, o_ga:o_nq], w[:, o_ng:o_mg],
                               jnp.zeros((d, LANES - GLA_GATE_RANK - 3 * NSA_HEADS), F32)], axis=1).astype(BF16)
    wa_pad = jnp.concatenate([gla_w_a2[l], jnp.zeros((LANES - GLA_GATE_RANK, GLA_QK), F32)], axis=0).astype(BF16)
    rw_t = router_w.T
    rw_hi = rw_t.astype(BF16)
    rw_lo = (rw_t - rw_hi.astype(F32)).astype(BF16)
    return dict(
        w_big=w_big, w_small=w_small, wa_pad=wa_pad,
        b_a=gla_b_a[l].reshape(1, -1), norm_g=gla_norm_g[l].reshape(1, -1),
        cmp_w1=nsa_cmp_w1[l].astype(BF16), cmp_w2=nsa_cmp_w2[l].astype(BF16),
        cmp_w2t=nsa_cmp_w2[l].transpose(0, 2, 1).astype(BF16),
        cmp_pe8=jnp.broadcast_to(nsa_cmp_pe[l].reshape(2, 1, CMP_LEN * NSA_DH), (2, 16, CMP_LEN * NSA_DH)).astype(BF16),
        w_bg=w_branch_gla[l].astype(BF16), w_bn=w_branch_nsa[l].astype(BF16), w_out=w_out[l].astype(BF16),
        ln_mix_g=ln_mix_g[l].reshape(1, -1), ln_mix_b=ln_mix_b[l].reshape(1, -1),
        xa_wq=xa_wq[l].astype(BF16), xa_wkv=xa_wkv[l].astype(BF16), xa_wo=xa_wo[l].astype(BF16),
        ln_xa_g=ln_xa_g[l].reshape(1, -1), ln_xa_b=ln_xa_b[l].reshape(1, -1),
        rw_hi=rw_hi, rw_lo=rw_lo, rb=router_b.reshape(-1, 1),
        ln_ffn_g=ln_ffn_g[l].reshape(1, -1), ln_ffn_b=ln_ffn_b[l].reshape(1, -1),
    )


def kernel(x, mem, w_in, gla_w_a2, gla_b_a, gla_norm_g, nsa_cmp_pe, nsa_cmp_w1, nsa_cmp_w2, w_branch_gla, w_branch_nsa, w_out, ln_mix_g, ln_mix_b, xa_wq, xa_wkv, xa_wo, ln_xa_g, ln_xa_b, router_w, router_b, moe_w_in, moe_w_down, ln_ffn_g, ln_ffn_b):
    B, T, d = x.shape
    assert T % 512 == 0 and d == 2048 and mem.shape[1] == MEM_LEN
    n = B * T
    params = (w_in, gla_w_a2, gla_b_a, gla_norm_g, nsa_cmp_pe, nsa_cmp_w1, nsa_cmp_w2, w_branch_gla, w_branch_nsa,
              w_out, ln_mix_g, ln_mix_b, xa_wq, xa_wkv, xa_wo, ln_xa_g, ln_xa_b, router_w, router_b, moe_w_in,
              moe_w_down, ln_ffn_g, ln_ffn_b)
    slopes = (2.0 ** (-8.0 * jnp.arange(1, NSA_HEADS + 1, dtype=F32) / NSA_HEADS)).astype(F32)
    nc, ns = T // CMP_STRIDE, T // SEL_LEN
    cs = np.arange(nc) * CMP_STRIDE
    ss = np.arange(ns) * SEL_LEN
    ovt = ((cs[None, :] < ss[:, None] + SEL_LEN) & (cs[None, :] + CMP_LEN > ss[:, None])
           & (cs[None, :] + CMP_LEN <= T)).astype(np.float32)
    sel_onehot = (np.arange(T)[:, None] // SEL_LEN == np.arange(ns)[None, :]).astype(np.float32)
    consts = (slopes, jnp.asarray(ovt, BF16), jnp.asarray(sel_onehot, BF16))

    xf = x.reshape(n, d)
    xb = xf.astype(BF16)
    mem_b = mem.reshape(B * MEM_LEN, d).astype(BF16)
    moe_w = (moe_w_in.astype(BF16), moe_w_down.astype(BF16))
    for l in range(DEPTH):
        p = _prep_layer(l, *params)
        xf, xb = _layer(xf, xb, mem_b, p, moe_w, l, consts, B, T)
    return xf.reshape(B, T, d)
```

```python
import functools

import jax
import jax.numpy as jnp
import numpy as np
from jax import lax
from jax.experimental import pallas as pl
from jax.experimental.pallas import tpu as pltpu

F32 = jnp.float32
BF16 = jnp.bfloat16
I32 = jnp.int32

DEPTH = 2
MEM_LEN = 256
GLA_HEADS = 4
GLA_DK = 128
GLA_DV = 256
GLA_GATE_RANK = 16
GLA_TAU = 16.0
GLA_CHUNK = 64
NSA_HEADS = 16
NSA_GROUPS = 4
NSA_HPG = NSA_HEADS // NSA_GROUPS
NSA_DH = 64
CMP_LEN = 32
CMP_STRIDE = 16
CMP_HIDDEN = 256
SEL_LEN = 64
SEL_TOPN = 8
WINDOW = 512
XA_HEADS = 4
XA_DH = 128
N_EXPERTS = 16
N_GROUPS = 4
EXPERTS_PER_GROUP = N_EXPERTS // N_GROUPS
TOP_K = 2
D_FF = 1536
DN_ALPHA = float((2 * DEPTH) ** 0.25)
LN_EPS = 1e-5
NEG = -1e30
FORCE_BONUS = 1e6

GLA_QK = GLA_HEADS * GLA_DK
GLA_V = GLA_HEADS * GLA_DV
NSA_Q = NSA_HEADS * NSA_DH
NSA_KV = NSA_GROUPS * NSA_DH

LANES = 128
VMEM_LIMIT = 56 * 1024 * 1024

COL_MG = 0
COL_GQ = 2 * 2048
COL_GK = COL_GQ + GLA_QK
COL_GV = COL_GK + GLA_QK
COL_GR = COL_GV + GLA_V
COL_NQ = COL_GR + GLA_V
COL_NKV = COL_NQ + NSA_Q
COL_END = COL_NKV + 6 * NSA_KV

MOE_BLOCK = 512
FF_TILE = 512


def _cp(sem):
    return pltpu.CompilerParams(dimension_semantics=sem, vmem_limit_bytes=VMEM_LIMIT)


def _dot(a, b):
    return jnp.dot(a, b, preferred_element_type=F32)


def _dot_nt(a, b):
    return lax.dot_general(a, b, (((1,), (1,)), ((), ())), preferred_element_type=F32)


def _dot_tn(a, b):
    return lax.dot_general(a, b, (((0,), (0,)), ((), ())), preferred_element_type=F32)


def _layer_norm(z, g, b):
    mu = jnp.mean(z, axis=-1, keepdims=True)
    zc = z - mu
    var = jnp.mean(zc * zc, axis=-1, keepdims=True)
    return zc * lax.rsqrt(var + LN_EPS) * g + b


def _mm_kernel(a_ref, b_ref, o_ref):
    o_ref[...] = _dot(a_ref[...], b_ref[...]).astype(o_ref.dtype)


def _matmul(a, b, out_dtype, tm, tn):
    m, k = a.shape
    n = b.shape[1]
    return pl.pallas_call(
        _mm_kernel,
        grid=(m // tm, n // tn),
        in_specs=[pl.BlockSpec((tm, k), lambda i, j: (i, 0)),
                  pl.BlockSpec((k, tn), lambda i, j: (0, j))],
        out_specs=pl.BlockSpec((tm, tn), lambda i, j: (i, j)),
        out_shape=jax.ShapeDtypeStruct((m, n), out_dtype),
        compiler_params=_cp(("parallel", "parallel")),
        name="matmul",
    )(a, b)


def _gla_kernel(q_ref, k_ref, v_ref, r_ref, sm_ref, wa_ref, ba_ref, ng_ref, o_ref, st_ref):
    C = GLA_CHUNK
    n_chunks = q_ref.shape[0] // C
    st_ref[...] = jnp.zeros_like(st_ref)
    rowi = lax.broadcasted_iota(I32, (C, GLA_DK), 0)
    tt = lax.broadcasted_iota(I32, (C, C), 0)
    ss = lax.broadcasted_iota(I32, (C, C), 1)
    levels = (1, 2, 4, 8, 16, 32)
    pair_masks = [((tt // (2 * L)) == (ss // (2 * L))) & ((tt & L) != 0) & ((ss & L) == 0) for L in levels]
    diag_mask = tt == ss
    scale = GLA_DK ** -0.5

    def head_chunk(rows, h, z):
        qk_cols = slice(h * GLA_DK, (h + 1) * GLA_DK)
        v_cols = slice(h * GLA_DV, (h + 1) * GLA_DV)
        q = q_ref[rows, qk_cols].astype(F32) * scale
        k = k_ref[rows, qk_cols].astype(F32)
        v = v_ref[rows, v_cols]
        g = (jnp.minimum(z, 0.0) - jnp.log1p(jnp.exp(-jnp.abs(z)))) * (1.0 / GLA_TAU)
        incl = g
        tot = g
        att = jnp.where(diag_mask, _dot_nt(q.astype(BF16), k.astype(BF16)), 0.0)
        for L, pm in zip(levels, pair_masks):
            ql = (q * jnp.exp(incl)).astype(BF16)
            kl = (k * jnp.exp(tot - incl)).astype(BF16)
            att = jnp.where(pm, _dot_nt(ql, kl), att)
            upper = (rowi & L) != 0
            from_lower = pltpu.roll(tot, L, 0)
            from_upper = pltpu.roll(tot, C - L, 0)
            incl = incl + jnp.where(upper, from_lower, 0.0)
            tot = tot + jnp.where(upper, from_lower, from_upper)
        qd = (q * jnp.exp(incl)).astype(BF16)
        kd = (k * jnp.exp(tot - incl)).astype(BF16)
        st = st_ref[h]
        o = _dot_nt(qd, st.astype(BF16)) + _dot(att.astype(BF16), v)
        st_ref[h] = st * jnp.exp(tot[0:1, :]) + _dot_tn(v, kd)
        mu = jnp.mean(o, axis=-1, keepdims=True)
        oc = o - mu
        var = jnp.mean(oc * oc, axis=-1, keepdims=True)
        on = oc * lax.rsqrt(var + LN_EPS) * ng_ref[:, v_cols]
        r = r_ref[rows, v_cols].astype(F32)
        o_ref[rows, v_cols] = (on * (r * jax.nn.sigmoid(r))).astype(o_ref.dtype)

    def chunk(c, carry):
        rows = pl.ds(pl.multiple_of(c * C, C), C)
        z = _dot(sm_ref[rows, :].astype(BF16), wa_ref[...]) + ba_ref[...]
        for h in range(GLA_HEADS):
            head_chunk(rows, h, z[:, h * GLA_DK:(h + 1) * GLA_DK])
        return carry

    lax.fori_loop(0, n_chunks, chunk, 0)


def _gla(h_big, h_small, wa_pad, b_a, norm_g, B, T):
    n = B * T
    return pl.pallas_call(
        _gla_kernel,
        grid=(B,),
        in_specs=[
            pl.BlockSpec((T, GLA_QK), lambda b: (b, COL_GQ // GLA_QK)),
            pl.BlockSpec((T, GLA_QK), lambda b: (b, COL_GK // GLA_QK)),
            pl.BlockSpec((T, GLA_V), lambda b: (b, COL_GV // GLA_V)),
            pl.BlockSpec((T, GLA_V), lambda b: (b, COL_GR // GLA_V)),
            pl.BlockSpec((T, LANES), lambda b: (b, 0)),
            pl.BlockSpec((LANES, GLA_QK), lambda b: (0, 0)),
            pl.BlockSpec((1, GLA_QK), lambda b: (0, 0)),
            pl.BlockSpec((1, GLA_V), lambda b: (0, 0)),
        ],
        out_specs=pl.BlockSpec((T, GLA_V), lambda b: (b, 0)),
        out_shape=jax.ShapeDtypeStruct((n, GLA_V), BF16),
        scratch_shapes=[pltpu.VMEM((GLA_HEADS, GLA_DV, GLA_DK), F32)],
        compiler_params=_cp(("parallel",)),
        name="gla",
    )(h_big, h_big, h_big, h_big, h_small, wa_pad, b_a, norm_g)


def _compress_kernel(u_ref, w1_ref, w2_ref, w2t_ref, pe_ref, o_ref, ot_ref):
    half = CMP_STRIDE * NSA_DH
    u = u_ref[0, 0, 0]
    nrow = u.shape[0]
    a = _dot(u, w1_ref[0, 0:half, :])
    bm = _dot(u, w1_ref[0, half:2 * half, :])
    c = _dot(pe_ref[0], w1_ref[0])
    hid = a + pltpu.roll(bm, nrow - 1, 0) + c[0:1, :]
    act = jax.nn.gelu(hid).astype(BF16)
    o_ref[0, 0, 0] = _dot(act, w2_ref[0]).astype(o_ref.dtype)
    ot_ref[0, 0, 0] = _dot_nt(w2t_ref[0], act).astype(ot_ref.dtype)


def _compress(kv_t, w1, w2, w2t, pe8, B, T):
    nc = T // CMP_STRIDE
    u = kv_t.reshape(B, 6, NSA_GROUPS, nc, CMP_STRIDE * NSA_DH)
    return pl.pallas_call(
        _compress_kernel,
        grid=(B, 2, NSA_GROUPS),
        in_specs=[
            pl.BlockSpec((1, 1, 1, nc, CMP_STRIDE * NSA_DH), lambda b, s, g: (b, s, g, 0, 0)),
            pl.BlockSpec((1, CMP_LEN * NSA_DH, CMP_HIDDEN), lambda b, s, g: (s, 0, 0)),
            pl.BlockSpec((1, CMP_HIDDEN, NSA_DH), lambda b, s, g: (s, 0, 0)),
            pl.BlockSpec((1, NSA_DH, CMP_HIDDEN), lambda b, s, g: (s, 0, 0)),
            pl.BlockSpec((1, 16, CMP_LEN * NSA_DH), lambda b, s, g: (s, 0, 0)),
        ],
        out_specs=[
            pl.BlockSpec((1, 1, 1, nc, NSA_DH), lambda b, s, g: (b, s, g, 0, 0)),
            pl.BlockSpec((1, 1, 1, NSA_DH, nc), lambda b, s, g: (b, s, g, 0, 0)),
        ],
        out_shape=[jax.ShapeDtypeStruct((B, 2, NSA_GROUPS, nc, NSA_DH), BF16),
                   jax.ShapeDtypeStruct((B, 2, NSA_GROUPS, NSA_DH, nc), BF16)],
        compiler_params=_cp(("parallel", "parallel", "parallel")),
        name="nsa_compress",
    )(u, w1, w2, w2t, pe8)


def _cmp_select_kernel(slopes_ref, qt_ref, kc_ref, vct_ref, ovt_ref, ocmp_ref, mb_ref):
    g = pl.program_id(1)
    i = pl.program_id(2)
    tq = qt_ref.shape[3]
    nc = kc_ref.shape[3]
    ns = mb_ref.shape[2]
    t0 = i * tq
    tpos = (t0 + lax.broadcasted_iota(I32, (nc, tq), 1))
    nidx = lax.broadcasted_iota(I32, (nc, tq), 0)
    mask_c = (nidx * CMP_STRIDE + (CMP_LEN - 1)) <= tpos
    absd = jnp.abs(tpos.astype(F32) - (nidx.astype(F32) * CMP_STRIDE + 0.5 * (CMP_LEN - 1)))
    kc = kc_ref[0, 0, 0]
    vct = vct_ref[0, 0, 0]
    psum = jnp.zeros((nc, tq), F32)
    for hh in range(NSA_HPG):
        slope = slopes_ref[g * NSA_HPG + hh]
        q = qt_ref[0, hh] * jnp.asarray(NSA_DH ** -0.5, BF16)
        s = _dot(kc, q) - slope * absd
        s = jnp.where(mask_c, s, NEG)
        e = jnp.exp(s - jnp.max(s, axis=0, keepdims=True))
        p = jnp.where(mask_c, e / jnp.sum(e, axis=0, keepdims=True), 0.0)
        ocmp_ref[0, hh] = _dot(vct, p.astype(BF16)).astype(ocmp_ref.dtype)
        psum = psum + p
    p_hi = psum.astype(BF16)
    p_lo = (psum - p_hi.astype(F32)).astype(BF16)
    imp = _dot(ovt_ref[...], p_hi) + _dot(ovt_ref[...], p_lo)
    j = lax.broadcasted_iota(I32, (ns, tq), 0)
    tp = t0 + lax.broadcasted_iota(I32, (ns, tq), 1)
    cur = tp // SEL_LEN
    forced = (j == 0) | (j == cur) | (j == cur - 1)
    valid = j * SEL_LEN <= tp
    score = jnp.where(valid, imp + jnp.where(forced, FORCE_BONUS, 0.0), NEG)
    rank = jnp.zeros((ns, tq), F32)
    for jp in range(ns):
        row = score[jp:jp + 1, :]
        beats = (row > score) | ((row == score) & (j > jp))
        rank = rank + jnp.where(beats, 1.0, 0.0)
    keep = valid & (rank < float(min(SEL_TOPN, ns)))
    mb_ref[0, 0] = jnp.where(keep, 0.0, NEG).astype(mb_ref.dtype)


def _cmp_select(slopes, q_t, kcmp, kcmp_t, ovt, B, T, tq):
    nc = T // CMP_STRIDE
    ns = T // SEL_LEN
    grid_spec = pltpu.PrefetchScalarGridSpec(
        num_scalar_prefetch=1,
        grid=(B, NSA_GROUPS, T // tq),
        in_specs=[
            pl.BlockSpec((1, NSA_HPG, NSA_DH, tq), lambda b, g, i, s: (b, g, 0, i)),
            pl.BlockSpec((1, 1, 1, nc, NSA_DH), lambda b, g, i, s: (b, 0, g, 0, 0)),
            pl.BlockSpec((1, 1, 1, NSA_DH, nc), lambda b, g, i, s: (b, 1, g, 0, 0)),
            pl.BlockSpec((ns, nc), lambda b, g, i, s: (0, 0)),
        ],
        out_specs=[
            pl.BlockSpec((1, NSA_HPG, NSA_DH, tq), lambda b, g, i, s: (b, g, 0, i)),
            pl.BlockSpec((1, 1, ns, tq), lambda b, g, i, s: (b, g, 0, i)),
        ],
    )
    return pl.pallas_call(
        _cmp_select_kernel,
        grid_spec=grid_spec,
        out_shape=[jax.ShapeDtypeStruct((B, NSA_HEADS, NSA_DH, T), BF16),
                   jax.ShapeDtypeStruct((B, NSA_GROUPS, ns, T), BF16)],
        compiler_params=_cp(("parallel", "parallel", "parallel")),
        name="nsa_cmp_select",
    )(slopes, q_t, kcmp, kcmp_t, ovt)


def _sel_win_kernel(slopes_ref, qt_ref, k_ref, vt_ref, mb_ref, ocmp_ref, gt_ref,
                    o_ref, qaug_ref, m_ref, l_ref, acc_ref, srow_ref, bias_ref, s_ref, p_ref, alpha_ref):
    g = pl.program_id(1)
    i = pl.program_id(2)
    tq = qt_ref.shape[3]
    tk = tq
    ns = mb_ref.shape[2]
    dh = NSA_DH
    wide = NSA_HPG * tq
    t0 = i * tq
    BIG = -NEG

    @pl.when(i == 0)
    def _():
        srow = jnp.concatenate([jnp.full((1, tq), slopes_ref[g * NSA_HPG + hh], F32) for hh in range(NSA_HPG)],
                               axis=1)
        srow_ref[...] = srow
        lane = lax.broadcasted_iota(I32, (tk, wide), 1) & (tq - 1)
        dist0 = (lane - lax.broadcasted_iota(I32, (tk, wide), 0)).astype(F32)
        sd0 = srow * dist0
        bias_ref[0] = sd0
        bias_ref[1] = sd0 + jnp.where(dist0 >= 0.0, 0.0, BIG)
        bias_ref[2] = sd0 + jnp.where(dist0 < 0.0, 0.0, BIG)
        bias_ref[3] = jnp.full((tk, wide), BIG, F32)

    for hh in range(NSA_HPG):
        cols = slice(hh * tq, (hh + 1) * tq)
        qaug_ref[0:dh, cols] = qt_ref[0, hh] * jnp.asarray(dh ** -0.5, BF16)
        qaug_ref[dh:dh + ns, cols] = mb_ref[0, 0]
        qaug_ref[dh + ns:, cols] = jnp.zeros((qaug_ref.shape[0] - dh - ns, tq), BF16)

    m_ref[...] = jnp.full(m_ref.shape, NEG, F32)
    l_ref[...] = jnp.zeros(l_ref.shape, F32)
    acc_ref[...] = jnp.zeros(acc_ref.shape, F32)

    n_back = WINDOW // tk
    n_sel = i + 1
    n_win = jnp.minimum(i, n_back) + 1
    n_steps = n_sel + n_win

    def describe(n):
        n = jnp.maximum(n, 0)
        is_win = n >= n_sel
        kb = jnp.clip(jnp.where(is_win, i - n_win + 1 + (n - n_sel), n), 0, i)
        mode = jnp.where(kb == i, 1, jnp.where(is_win & (kb == i - n_back), 2, 0))
        mode = jnp.where(n >= n_steps, 3, mode)
        return is_win.astype(I32), kb, mode

    def scores(n, slot):
        br, kb, _ = describe(n)
        s0 = pl.multiple_of(kb * tk, tk)
        s_ref[slot] = _dot(k_ref[0, br, 0, pl.ds(s0, tk), :], qaug_ref[...])

    def softmax(n, slot):
        br, kb, mode = describe(n)
        crow = srow_ref[...] * ((i - kb) * tk).astype(F32)
        s = s_ref[slot] - bias_ref[mode]
        m_old = m_ref[br]
        m_new = jnp.maximum(m_old, jnp.max(s, axis=0, keepdims=True) - crow)
        alpha = jnp.exp(m_old - m_new)
        p = jnp.exp(s - (m_new + crow))
        l_ref[br] = alpha * l_ref[br] + jnp.sum(p, axis=0, keepdims=True)
        m_ref[br] = m_new
        alpha_ref[slot] = alpha
        p_ref[slot] = p.astype(BF16)

    def weighted_values(n, slot):
        br, kb, _ = describe(n)
        s0 = pl.multiple_of(kb * tk, tk)
        acc_ref[br] = alpha_ref[slot] * acc_ref[br] + _dot(vt_ref[0, br, 0, :, pl.ds(s0, tk)], p_ref[slot])

    p_ref[1] = jnp.zeros(p_ref.shape[1:], BF16)
    alpha_ref[1] = jnp.ones(alpha_ref.shape[1:], F32)
    scores(0, 0)

    def pair(j, carry):
        n = 2 * j
        scores(n + 1, 1)
        softmax(n, 0)
        weighted_values(n - 1, 1)
        scores(n + 2, 0)
        softmax(n + 1, 1)
        weighted_values(n, 0)
        return carry

    n_pairs = (n_steps + 1) // 2
    lax.fori_loop(0, n_pairs, pair, 0)
    weighted_values(2 * n_pairs - 1, 1)

    def gate_row(branch):
        rows = [gt_ref[0, 0, 3 * hh + branch:3 * hh + branch + 1, :] for hh in range(NSA_HPG)]
        return jax.nn.sigmoid(jnp.concatenate(rows, axis=1))

    o = gate_row(1) * (acc_ref[0] / l_ref[0]) + gate_row(2) * (acc_ref[1] / l_ref[1])
    ocmp = jnp.concatenate([ocmp_ref[0, hh] for hh in range(NSA_HPG)], axis=1).astype(F32)
    o = o + gate_row(0) * ocmp
    for hh in range(NSA_HPG):
        o_ref[0, hh] = o[:, hh * tq:(hh + 1) * tq].astype(o_ref.dtype)


def _sel_win(slopes, q_t, k_all, vt_all, mb, ocmp_t, gates_t, B, T, tq):
    ns = T // SEL_LEN
    kaug = k_all.shape[-1]
    G = NSA_GROUPS
    grid_spec = pltpu.PrefetchScalarGridSpec(
        num_scalar_prefetch=1,
        grid=(B, G, T // tq),
        in_specs=[
            pl.BlockSpec((1, NSA_HPG, NSA_DH, tq), lambda b, g, i, s: (b, g, 0, i)),
            pl.BlockSpec((1, 2, 1, T, kaug), lambda b, g, i, s: (b, 0, g, 0, 0)),
            pl.BlockSpec((1, 2, 1, NSA_DH, T), lambda b, g, i, s: (b, 0, g, 0, 0)),
            pl.BlockSpec((1, 1, ns, tq), lambda b, g, i, s: (b, g, 0, i)),
            pl.BlockSpec((1, NSA_HPG, NSA_DH, tq), lambda b, g, i, s: (b, g, 0, i)),
            pl.BlockSpec((1, 1, 16, tq), lambda b, g, i, s: (b, g, 0, i)),
        ],
        out_specs=pl.BlockSpec((1, NSA_HPG, NSA_DH, tq), lambda b, g, i, s: (b, g, 0, i)),
        scratch_shapes=[
            pltpu.VMEM((kaug, NSA_HPG * tq), BF16),
            pltpu.VMEM((2, 1, NSA_HPG * tq), F32),
            pltpu.VMEM((2, 1, NSA_HPG * tq), F32),
            pltpu.VMEM((2, NSA_DH, NSA_HPG * tq), F32),
            pltpu.VMEM((1, NSA_HPG * tq), F32),
            pltpu.VMEM((4, tq, NSA_HPG * tq), F32),
            pltpu.VMEM((2, tq, NSA_HPG * tq), F32),
            pltpu.VMEM((2, tq, NSA_HPG * tq), BF16),
            pltpu.VMEM((2, 1, NSA_HPG * tq), F32),
        ],
    )
    return pl.pallas_call(
        _sel_win_kernel,
        grid_spec=grid_spec,
        out_shape=jax.ShapeDtypeStruct((B, NSA_HEADS, NSA_DH, T), BF16),
        compiler_params=_cp(("parallel", "parallel", "arbitrary")),
        name="nsa_sel_win",
    )(slopes, q_t, k_all, vt_all, mb, ocmp_t, gates_t)


def _mix_kernel(og_ref, on_ref, mg1_ref, mg2_ref, x_ref, wg_ref, wn_ref, wo_ref, lg_ref, lb_ref,
                x1_ref, x1b_ref):
    g1 = _dot(og_ref[...], wg_ref[...])
    g2 = _dot(on_ref[...], wn_ref[...])
    merged = (jax.nn.sigmoid(mg1_ref[...].astype(F32)) * g1
              + jax.nn.sigmoid(mg2_ref[...].astype(F32)) * g2)
    y = _dot(merged.astype(BF16), wo_ref[...])
    x1 = _layer_norm(DN_ALPHA * x_ref[...] + y, lg_ref[...], lb_ref[...])
    x1_ref[...] = x1
    x1b_ref[...] = x1.astype(BF16)


def _const_spec(shape):
    nd = len(shape)
    return pl.BlockSpec(shape, lambda *_: (0,) * nd, pipeline_mode=pl.Buffered(1))


def _mix(o_gla, o_nsa, h_big, x, wg, wn, wo, lg, lb, tm):
    n, d = x.shape
    return pl.pallas_call(
        _mix_kernel,
        grid=(n // tm,),
        in_specs=[
            pl.BlockSpec((tm, GLA_V), lambda i: (i, 0)),
            pl.BlockSpec((tm, NSA_Q), lambda i: (i, 0)),
            pl.BlockSpec((tm, d), lambda i: (i, 0)),
            pl.BlockSpec((tm, d), lambda i: (i, 1)),
            pl.BlockSpec((tm, d), lambda i: (i, 0)),
            _const_spec(wg.shape), _const_spec(wn.shape), _const_spec(wo.shape),
            _const_spec(lg.shape), _const_spec(lb.shape),
        ],
        out_specs=[pl.BlockSpec((tm, d), lambda i: (i, 0)), pl.BlockSpec((tm, d), lambda i: (i, 0))],
        out_shape=[jax.ShapeDtypeStruct((n, d), F32), jax.ShapeDtypeStruct((n, d), BF16)],
        compiler_params=_cp(("parallel",)),
        name="mix_ln",
    )(o_gla, o_nsa, h_big, h_big, x, wg, wn, wo, lg, lb)


def _xattn_kernel(x_ref, xb_ref, kv_ref, wq_ref, wo_ref, lg_ref, lb_ref, x2_ref):
    hd = XA_HEADS * XA_DH
    q = (_dot(xb_ref[...], wq_ref[...]) * (XA_DH ** -0.5)).astype(BF16)
    outs = []
    for h in range(XA_HEADS):
        kh = kv_ref[0, :, h * XA_DH:(h + 1) * XA_DH]
        vh = kv_ref[0, :, hd + h * XA_DH:hd + (h + 1) * XA_DH]
        s = _dot_nt(q[:, h * XA_DH:(h + 1) * XA_DH], kh)
        e = jnp.exp(s - jnp.max(s, axis=-1, keepdims=True))
        p = e / jnp.sum(e, axis=-1, keepdims=True)
        outs.append(_dot(p.astype(BF16), vh).astype(BF16))
    o = jnp.concatenate(outs, axis=-1)
    y = _dot(o, wo_ref[...])
    x2_ref[...] = _layer_norm(DN_ALPHA * x_ref[...] + y, lg_ref[...], lb_ref[...])


def _xattn(x1, x1b, kv, wq, wo, lg, lb, B, T, tm):
    n, d = x1.shape
    nt = T // tm
    return pl.pallas_call(
        _xattn_kernel,
        grid=(B, nt),
        in_specs=[
            pl.BlockSpec((tm, d), lambda b, i: (b * nt + i, 0)),
            pl.BlockSpec((tm, d), lambda b, i: (b * nt + i, 0)),
            pl.BlockSpec((1,) + kv.shape[1:], lambda b, i: (b, 0, 0)),
            _const_spec(wq.shape), _const_spec(wo.shape), _const_spec(lg.shape), _const_spec(lb.shape),
        ],
        out_specs=pl.BlockSpec((tm, d), lambda b, i: (b * nt + i, 0)),
        out_shape=jax.ShapeDtypeStruct((n, d), F32),
        compiler_params=_cp(("parallel", "parallel")),
        name="xattn_ln",
    )(x1, x1b, kv, wq, wo, lg, lb)


def _router_kernel(x_ref, wh_ref, wl_ref, rb_ref, e_ref, gate_ref, rank_ref, cnt_ref, carry_ref):
    i = pl.program_id(0)
    tr = x_ref.shape[0]
    E = N_EXPERTS

    @pl.when(i == 0)
    def _():
        carry_ref[...] = jnp.zeros_like(carry_ref)

    x = x_ref[...]
    x_hi = x.astype(BF16)
    x_lo = (x - x_hi.astype(F32)).astype(BF16)
    wh = wh_ref[...]
    logits = _dot_nt(wh, x_hi) + _dot_nt(wh, x_lo) + _dot_nt(wl_ref[...], x_hi)
    biased = logits + rb_ref[...]
    rows = [biased[e:e + 1, :] for e in range(E)]
    raw = [logits[e:e + 1, :] for e in range(E)]
    best_score = None
    best = None
    for gi in range(N_GROUPS):
        v = rows[gi * EXPERTS_PER_GROUP:(gi + 1) * EXPERTS_PER_GROUP]
        sc = None
        for a in range(EXPERTS_PER_GROUP):
            for b in range(a + 1, EXPERTS_PER_GROUP):
                pair = v[a] + v[b]
                sc = pair if sc is None else jnp.maximum(sc, pair)
        if best is None:
            best_score, best = sc, jnp.zeros((1, tr), I32)
        else:
            better = sc > best_score
            best_score = jnp.where(better, sc, best_score)
            best = jnp.where(better, gi, best)

    def pick(vals):
        out = vals[0:EXPERTS_PER_GROUP]
        for gi in range(1, N_GROUPS):
            out = [jnp.where(best == gi, vals[gi * EXPERTS_PER_GROUP + a], out[a]) for a in range(EXPERTS_PER_GROUP)]
        return out

    w = pick(rows)
    lraw = pick(raw)
    i1 = jnp.zeros((1, tr), I32)
    v1 = w[0]
    l1 = lraw[0]
    for a in range(1, EXPERTS_PER_GROUP):
        better = w[a] > v1
        v1 = jnp.where(better, w[a], v1)
        l1 = jnp.where(better, lraw[a], l1)
        i1 = jnp.where(better, a, i1)
    i2 = jnp.full((1, tr), -1, I32)
    v2 = jnp.full((1, tr), -jnp.inf, F32)
    l2 = jnp.zeros((1, tr), F32)
    for a in range(EXPERTS_PER_GROUP):
        better = (i1 != a) & ((w[a] > v2) | (i2 < 0))
        v2 = jnp.where(better, w[a], v2)
        l2 = jnp.where(better, lraw[a], l2)
        i2 = jnp.where(better, a, i2)
    e1 = best * EXPERTS_PER_GROUP + i1
    e2 = best * EXPERTS_PER_GROUP + i2
    mx = jnp.maximum(l1, l2)
    p1 = jnp.exp(l1 - mx)
    p2 = jnp.exp(l2 - mx)
    den = p1 + p2
    e_ref[0:1, :] = e1
    e_ref[1:2, :] = e2
    gate_ref[0:1, :] = p1 / den
    gate_ref[1:2, :] = p2 / den
    eidx = lax.broadcasted_iota(I32, (E, tr), 0)
    is1 = eidx == e1
    is2 = eidx == e2
    member = jnp.where(is1 | is2, 1.0, 0.0)
    uu = lax.broadcasted_iota(I32, (tr, tr), 0)
    tt = lax.broadcasted_iota(I32, (tr, tr), 1)
    tri = jnp.where(uu <= tt, 1.0, 0.0).astype(BF16)
    incl = _dot(member.astype(BF16), tri)
    excl = carry_ref[:, 0:1] + incl - member
    rank_ref[0:1, :] = jnp.sum(jnp.where(is1, excl, 0.0), axis=0, keepdims=True).astype(I32)
    rank_ref[1:2, :] = jnp.sum(jnp.where(is2, excl, 0.0), axis=0, keepdims=True).astype(I32)
    new_carry = carry_ref[...] + jnp.sum(member, axis=1, keepdims=True)
    carry_ref[...] = new_carry
    cnt_ref[...] = new_carry


def _router(x2, rw_hi, rw_lo, rb, tr):
    n, d = x2.shape
    E = N_EXPERTS
    return pl.pallas_call(
        _router_kernel,
        grid=(n // tr,),
        in_specs=[
            pl.BlockSpec((tr, d), lambda i: (i, 0)),
            pl.BlockSpec((E, d), lambda i: (0, 0)),
            pl.BlockSpec((E, d), lambda i: (0, 0)),
            pl.BlockSpec((E, 1), lambda i: (0, 0)),
        ],
        out_specs=[
            pl.BlockSpec((2, tr), lambda i: (0, i)),
            pl.BlockSpec((2, tr), lambda i: (0, i)),
            pl.BlockSpec((2, tr), lambda i: (0, i)),
            pl.BlockSpec((E, LANES), lambda i: (0, 0)),
        ],
        out_shape=[jax.ShapeDtypeStruct((2, n), I32), jax.ShapeDtypeStruct((2, n), F32),
                   jax.ShapeDtypeStruct((2, n), I32), jax.ShapeDtypeStruct((E, LANES), F32)],
        scratch_shapes=[pltpu.VMEM((E, LANES), F32)],
        compiler_params=_cp(("arbitrary",)),
        name="moe_router",
    )(x2, rw_hi, rw_lo, rb)


def _dispatch_kernel(ps_ref, pe_ref, e_ref, rank_ref, x_ref, buf_hbm, zero_ref, sem):
    td = e_ref.shape[1]

    @pl.when(pl.program_id(0) == 0)
    def _():
        zero_ref[...] = jnp.zeros_like(zero_ref)

        def zero_copy(ex):
            last = pl.multiple_of(jnp.maximum(pe_ref[ex] - MOE_BLOCK, 0), MOE_BLOCK)
            return pltpu.make_async_copy(zero_ref, buf_hbm.at[pl.ds(last, MOE_BLOCK), :], sem)

        def nonempty(ex):
            return pe_ref[ex] > (pe_ref[ex - 1] if ex > 0 else 0)

        n_blocks = buf_hbm.shape[0] // MOE_BLOCK
        first_unused = pe_ref[N_EXPERTS - 1] // MOE_BLOCK

        def tail_copy(k):
            row = pl.multiple_of((first_unused + k) * MOE_BLOCK, MOE_BLOCK)
            return pltpu.make_async_copy(zero_ref, buf_hbm.at[pl.ds(row, MOE_BLOCK), :], sem)

        for ex in range(N_EXPERTS):
            pl.when(nonempty(ex))(lambda ex=ex: zero_copy(ex).start())
            pl.when(first_unused + ex < n_blocks)(lambda ex=ex: tail_copy(ex).start())
        for ex in range(N_EXPERTS):
            pl.when(nonempty(ex))(lambda ex=ex: zero_copy(ex).wait())
            pl.when(first_unused + ex < n_blocks)(lambda ex=ex: tail_copy(ex).wait())

    def issue(t, carry):
        for kk in range(TOP_K):
            dest = ps_ref[e_ref[kk, t]] + rank_ref[kk, t]
            pltpu.make_async_copy(x_ref.at[pl.ds(t, 1), :], buf_hbm.at[pl.ds(dest, 1), :], sem).start()
        return carry

    lax.fori_loop(0, td, issue, 0, unroll=8)
    for kk in range(TOP_K):
        pltpu.make_async_copy(x_ref, buf_hbm.at[pl.ds(0, td), :], sem).wait()


def _dispatch(pad_start, pad_end, e, rank, x2, n_rows, td):
    n, d = x2.shape
    grid_spec = pltpu.PrefetchScalarGridSpec(
        num_scalar_prefetch=2,
        grid=(n // td,),
        in_specs=[
            pl.BlockSpec((2, td), lambda i, s, t: (0, i), memory_space=pltpu.SMEM),
            pl.BlockSpec((2, td), lambda i, s, t: (0, i), memory_space=pltpu.SMEM),
            pl.BlockSpec((td, d), lambda i, s, t: (i, 0)),
        ],
        out_specs=pl.BlockSpec(memory_space=pl.ANY),
        scratch_shapes=[pltpu.VMEM((MOE_BLOCK, d), F32), pltpu.SemaphoreType.DMA(())],
    )
    return pl.pallas_call(
        _dispatch_kernel,
        grid_spec=grid_spec,
        out_shape=jax.ShapeDtypeStruct((n_rows, d), F32),
        compiler_params=_cp(("arbitrary",)),
        name="moe_dispatch",
    )(pad_start, pad_end, e, rank, x2)


def _expert_kernel(be_ref, nb_ref, x_ref, win_hbm, wdn_hbm, y_ref, xb_ref, wa_s, wu_s, wd_s, sa, su, sd, sems,
                   *, layer):
    b = pl.program_id(0)
    nf = D_FF // FF_TILE
    n_used = nb_ref[0]
    e = be_ref[b]
    e_prev = be_ref[jnp.maximum(b - 1, 0)]
    e_next = be_ref[jnp.minimum(b + 1, pl.num_programs(0) - 1)]
    active = b < n_used
    is_first = active & ((b == 0) | (e_prev != e))
    feeds_next = active & (b + 1 < n_used) & (e_next != e)

    def tile_copies(ex, f):
        lo = f * FF_TILE
        return (pltpu.make_async_copy(win_hbm.at[layer, ex, :, pl.ds(lo, FF_TILE)], sa, sems.at[0]),
                pltpu.make_async_copy(win_hbm.at[layer, ex, :, pl.ds(D_FF + lo, FF_TILE)], su, sems.at[1]),
                pltpu.make_async_copy(wdn_hbm.at[layer, ex, pl.ds(lo, FF_TILE), :], sd, sems.at[2]))

    def start(ex, f):
        for c in tile_copies(ex, f):
            c.start()

    def finish(ex, f):
        for c in tile_copies(ex, f):
            c.wait()
        wa_s[f] = sa[...].astype(BF16)
        wu_s[f] = su[...].astype(BF16)
        wd_s[f] = sd[...].astype(BF16)

    @pl.when(b == 0)
    def _():
        for f in range(nf - 1):
            start(e, f)
            finish(e, f)
        start(e, nf - 1)

    @pl.when(jnp.logical_not(active))
    def _():
        y_ref[...] = jnp.zeros_like(y_ref)

    @pl.when(active)
    def _():
        xb_ref[...] = x_ref[...].astype(BF16)
        for f in range(nf):
            xb = xb_ref[...]
            a = _dot(xb, wa_s[f])
            u = _dot(xb, wu_s[f])
            act = (a * jax.nn.sigmoid(a) * u).astype(BF16)
            y = _dot(act, wd_s[f])
            if f == 0:
                y_ref[...] = y
                pl.when(is_first)(lambda: finish(e, nf - 1))
            else:
                y_ref[...] += y

            @pl.when(feeds_next)
            def _(f=f):
                if f >= 1:
                    finish(e_next, f - 1)
                start(e_next, f)


def _experts(blk_expert, n_used, buf, w_in, w_down, layer):
    p, d = buf.shape
    nb = p // MOE_BLOCK
    nf = D_FF // FF_TILE
    grid_spec = pltpu.PrefetchScalarGridSpec(
        num_scalar_prefetch=2,
        grid=(nb,),
        in_specs=[
            pl.BlockSpec((MOE_BLOCK, d), lambda b, be, nu: (jnp.minimum(b, nu[0] - 1), 0)),
            pl.BlockSpec(memory_space=pl.ANY),
            pl.BlockSpec(memory_space=pl.ANY),
        ],
        out_specs=pl.BlockSpec((MOE_BLOCK, d), lambda b, be, nu: (b, 0)),
        scratch_shapes=[
            pltpu.VMEM((MOE_BLOCK, d), BF16),
            pltpu.VMEM((nf, d, FF_TILE), BF16),
            pltpu.VMEM((nf, d, FF_TILE), BF16),
            pltpu.VMEM((nf, FF_TILE, d), BF16),
            pltpu.VMEM((d, FF_TILE), F32),
            pltpu.VMEM((d, FF_TILE), F32),
            pltpu.VMEM((FF_TILE, d), F32),
            pltpu.SemaphoreType.DMA((3,)),
        ],
    )
    return pl.pallas_call(
        functools.partial(_expert_kernel, layer=layer),
        grid_spec=grid_spec,
        out_shape=jax.ShapeDtypeStruct((p, d), F32),
        compiler_params=_cp(("arbitrary",)),
        name="moe_experts",
    )(blk_expert, n_used, buf, w_in, w_down)


def _combine_kernel(ps_ref, e_ref, rank_ref, y_hbm, x_ref, gate_ref, lg_ref, lb_ref, x3_ref, x3b_ref,
                    y0_ref, y1_ref, sem):
    tc = x_ref.shape[0]
    bufs = (y0_ref, y1_ref)

    def issue(t, carry):
        for kk in range(TOP_K):
            src = ps_ref[e_ref[kk, t]] + rank_ref[kk, t]
            pltpu.make_async_copy(y_hbm.at[pl.ds(src, 1), :], bufs[kk].at[pl.ds(t, 1), :], sem).start()
        return carry

    lax.fori_loop(0, tc, issue, 0, unroll=8)
    for kk in range(TOP_K):
        pltpu.make_async_copy(y_hbm.at[pl.ds(0, tc), :], bufs[kk], sem).wait()
    gate = gate_ref[...]
    z = DN_ALPHA * x_ref[...] + gate[:, 0:1] * y0_ref[...] + gate[:, 1:2] * y1_ref[...]
    x3 = _layer_norm(z, lg_ref[...], lb_ref[...])
    x3_ref[...] = x3
    x3b_ref[...] = x3.astype(BF16)


def _combine(pad_start, e, rank, y, x2, gate_nt, lg, lb, tc):
    n, d = x2.shape
    grid_spec = pltpu.PrefetchScalarGridSpec(
        num_scalar_prefetch=1,
        grid=(n // tc,),
        in_specs=[
            pl.BlockSpec((2, tc), lambda i, s: (0, i), memory_space=pltpu.SMEM),
            pl.BlockSpec((2, tc), lambda i, s: (0, i), memory_space=pltpu.SMEM),
            pl.BlockSpec(memory_space=pl.ANY),
            pl.BlockSpec((tc, d), lambda i, s: (i, 0)),
            pl.BlockSpec((tc, 2), lambda i, s: (i, 0)),
            pl.BlockSpec((1, d), lambda i, s: (0, 0)),
            pl.BlockSpec((1, d), lambda i, s: (0, 0)),
        ],
        out_specs=[pl.BlockSpec((tc, d), lambda i, s: (i, 0)), pl.BlockSpec((tc, d), lambda i, s: (i, 0))],
        scratch_shapes=[pltpu.VMEM((tc, d), F32), pltpu.VMEM((tc, d), F32), pltpu.SemaphoreType.DMA(())],
    )
    return pl.pallas_call(
        _combine_kernel,
        grid_spec=grid_spec,
        out_shape=[jax.ShapeDtypeStruct((n, d), F32), jax.ShapeDtypeStruct((n, d), BF16)],
        compiler_params=_cp(("arbitrary",)),
        name="moe_combine_ln",
    )(pad_start, e, rank, y, x2, gate_nt, lg, lb)


def _layer(x, xb, mem_b, p, moe_w, layer, consts, B, T):
    n, d = x.shape
    G, HPG, DH = NSA_GROUPS, NSA_HPG, NSA_DH
    slopes, ovt, sel_onehot = consts

    h_big = _matmul(xb, p["w_big"], BF16, 1024, 512)
    h_small = _matmul(xb, p["w_small"], F32, 1024, LANES)

    o_gla = _gla(h_big, h_small, p["wa_pad"], p["b_a"], p["norm_g"], B, T)

    q_t = h_big[:, COL_NQ:COL_NQ + NSA_Q].reshape(B, T, NSA_HEADS, DH).transpose(0, 2, 3, 1)
    kv = h_big[:, COL_NKV:COL_END].reshape(B, T, 6, G, DH)
    kv_rows = kv.transpose(0, 2, 3, 1, 4)
    kv_cols = kv.transpose(0, 2, 3, 4, 1)
    kcmp, kcmp_t = _compress(kv_rows, p["cmp_w1"], p["cmp_w2"], p["cmp_w2t"], p["cmp_pe8"], B, T)
    ocmp_t, mb = _cmp_select(slopes, q_t, kcmp, kcmp_t, ovt, B, T, 512)
    pad = jnp.zeros((B, G, T, LANES - DH - sel_onehot.shape[1]), BF16)
    ks_aug = jnp.concatenate([kv_rows[:, 2], jnp.broadcast_to(sel_onehot, (B, G) + sel_onehot.shape), pad], axis=-1)
    gates_t = h_small[:, GLA_GATE_RANK:GLA_GATE_RANK + 3 * NSA_HEADS].reshape(B, T, G, 3 * HPG)
    gates_t = jnp.pad(gates_t.transpose(0, 2, 3, 1), ((0, 0), (0, 0), (0, 16 - 3 * HPG), (0, 0)))
    kw_pad = jnp.concatenate([kv_rows[:, 4], jnp.zeros((B, G, T, LANES - DH), BF16)], axis=-1)
    k_all = jnp.stack([ks_aug, kw_pad], axis=1)
    vt_all = jnp.stack([kv_cols[:, 3], kv_cols[:, 5]], axis=1)
    o_nsa_t = _sel_win(slopes, q_t, k_all, vt_all, mb, ocmp_t, gates_t, B, T, 256)
    o_nsa = o_nsa_t.transpose(0, 3, 1, 2).reshape(n, NSA_Q)

    x1, x1b = _mix(o_gla, o_nsa, h_big, x, p["w_bg"], p["w_bn"], p["w_out"], p["ln_mix_g"], p["ln_mix_b"], 256)

    kvm = _matmul(mem_b, p["xa_wkv"], BF16, 512, 512).reshape(B, MEM_LEN, 2 * XA_HEADS * XA_DH)
    x2 = _xattn(x1, x1b, kvm, p["xa_wq"], p["xa_wo"], p["ln_xa_g"], p["ln_xa_b"], B, T, 256)

    e, gate, rank, cnt = _router(x2, p["rw_hi"], p["rw_lo"], p["rb"], 512)
    counts = cnt[:, 0].astype(I32)
    padded = (counts + MOE_BLOCK - 1) // MOE_BLOCK * MOE_BLOCK
    pad_end = jnp.cumsum(padded)
    pad_start = (pad_end - padded).astype(I32)
    nb = (n * TOP_K) // MOE_BLOCK + N_EXPERTS
    n_used = (pad_end[-1] // MOE_BLOCK).astype(I32).reshape(1)
    blk_start = jnp.arange(nb, dtype=I32) * MOE_BLOCK
    blk_expert = jnp.minimum(jnp.sum(blk_start[:, None] >= pad_end[None, :], axis=1), N_EXPERTS - 1).astype(I32)
    blk_expert = jnp.where(jnp.arange(nb) < n_used[0], blk_expert, blk_expert[jnp.maximum(n_used[0] - 1, 0)])
    buf = _dispatch(pad_start, pad_end.astype(I32), e, rank, x2, nb * MOE_BLOCK, 512)
    y = _experts(blk_expert, n_used, buf, moe_w[0], moe_w[1], layer)
    x3, x3b = _combine(pad_start, e, rank, y, x2, gate.T, p["ln_ffn_g"], p["ln_ffn_b"], 256)
    return x3, x3b


def _prep_layer(l, w_in, gla_w_a2, gla_b_a, gla_norm_g, nsa_cmp_pe, nsa_cmp_w1, nsa_cmp_w2, w_branch_gla,
                w_branch_nsa, w_out, ln_mix_g, ln_mix_b, xa_wq, xa_wkv, xa_wo, ln_xa_g, ln_xa_b, router_w,
                router_b, moe_w_in, moe_w_down, ln_ffn_g, ln_ffn_b):
    d = w_in.shape[1]
    w = w_in[l]
    o_gq, o_gk, o_gv, o_gr = 0, GLA_QK, 2 * GLA_QK, 2 * GLA_QK + GLA_V
    o_ga = o_gr + GLA_V
    o_nq = o_ga + GLA_GATE_RANK
    o_nkv = o_nq + NSA_Q
    o_ng = o_nkv + 6 * NSA_KV
    o_mg = o_ng + 3 * NSA_HEADS
    w_big = jnp.concatenate([w[:, o_mg:o_mg + 2 * d], w[:, o_gq:o_ga], w[:, o_nq:o_ng]], axis=1).astype(BF16)
    w_small = jnp.concatenate([w[:, o_ga:o_nq], w[:, o_ng:o_mg],
                               jnp.zeros((d, LANES - GLA_GATE_RANK - 3 * NSA_HEADS), F32)], axis=1).astype(BF16)
    wa_pad = jnp.concatenate([gla_w_a2[l], jnp.zeros((LANES - GLA_GATE_RANK, GLA_QK), F32)], axis=0).astype(BF16)
    rw_t = router_w.T
    rw_hi = rw_t.astype(BF16)
    rw_lo = (rw_t - rw_hi.astype(F32)).astype(BF16)
    return dict(
        w_big=w_big, w_small=w_small, wa_pad=wa_pad,
        b_a=gla_b_a[l].reshape(1, -1), norm_g=gla_norm_g[l].reshape(1, -1),
        cmp_w1=nsa_cmp_w1[l].astype(BF16), cmp_w2=nsa_cmp_w2[l].astype(BF16),
        cmp_w2t=nsa_cmp_w2[l].transpose(0, 2, 1).astype(BF16),
        cmp_pe8=jnp.broadcast_to(nsa_cmp_pe[l].reshape(2, 1, CMP_LEN * NSA_DH), (2, 16, CMP_LEN * NSA_DH)).astype(BF16),
        w_bg=w_branch_gla[l].astype(BF16), w_bn=w_branch_nsa[l].astype(BF16), w_out=w_out[l].astype(BF16),
        ln_mix_g=ln_mix_g[l].reshape(1, -1), ln_mix_b=ln_mix_b[l].reshape(1, -1),
        xa_wq=xa_wq[l].astype(BF16), xa_wkv=xa_wkv[l].astype(BF16), xa_wo=xa_wo[l].astype(BF16),
        ln_xa_g=ln_xa_g[l].reshape(1, -1), ln_xa_b=ln_xa_b[l].reshape(1, -1),
        rw_hi=rw_hi, rw_lo=rw_lo, rb=router_b.reshape(-1, 1),
        ln_ffn_g=ln_ffn_g[l].reshape(1, -1), ln_ffn_b=ln_ffn_b[l].reshape(1, -1),
    )


def kernel(x, mem, w_in, gla_w_a2, gla_b_a, gla_norm_g, nsa_cmp_pe, nsa_cmp_w1, nsa_cmp_w2, w_branch_gla, w_branch_nsa, w_out, ln_mix_g, ln_mix_b, xa_wq, xa_wkv, xa_wo, ln_xa_g, ln_xa_b, router_w, router_b, moe_w_in, moe_w_down, ln_ffn_g, ln_ffn_b):
    B, T, d = x.shape
    assert T % 512 == 0 and d == 2048 and mem.shape[1] == MEM_LEN
    n = B * T
    params = (w_in, gla_w_a2, gla_b_a, gla_norm_g, nsa_cmp_pe, nsa_cmp_w1, nsa_cmp_w2, w_branch_gla, w_branch_nsa,
              w_out, ln_mix_g, ln_mix_b, xa_wq, xa_wkv, xa_wo, ln_xa_g, ln_xa_b, router_w, router_b, moe_w_in,
              moe_w_down, ln_ffn_g, ln_ffn_b)
    slopes = (2.0 ** (-8.0 * jnp.arange(1, NSA_HEADS + 1, dtype=F32) / NSA_HEADS)).astype(F32)
    nc, ns = T // CMP_STRIDE, T // SEL_LEN
    cs = np.arange(nc) * CMP_STRIDE
    ss = np.arange(ns) * SEL_LEN
    ovt = ((cs[None, :] < ss[:, None] + SEL_LEN) & (cs[None, :] + CMP_LEN > ss[:, None])
           & (cs[None, :] + CMP_LEN <= T)).astype(np.float32)
    sel_onehot = (np.arange(T)[:, None] // SEL_LEN == np.arange(ns)[None, :]).astype(np.float32)
    consts = (slopes, jnp.asarray(ovt, BF16), jnp.asarray(sel_onehot, BF16))

    xf = x.reshape(n, d)
    xb = xf.astype(BF16)
    mem_b = mem.reshape(B * MEM_LEN, d).astype(BF16)
    moe_w = (moe_w_in, moe_w_down)
    for l in range(DEPTH):
        p = _prep_layer(l, *params)
        xf, xb = _layer(xf, xb, mem_b, p, moe_w, l, consts, B, T)
    return xf.reshape(B, T, d)
```

```python
import functools

import jax
import jax.numpy as jnp
import numpy as np
from jax import lax
from jax.experimental import pallas as pl
from jax.experimental.pallas import tpu as pltpu

F32 = jnp.float32
BF16 = jnp.bfloat16
I32 = jnp.int32

DEPTH = 2
MEM_LEN = 256
GLA_HEADS = 4
GLA_DK = 128
GLA_DV = 256
GLA_GATE_RANK = 16
GLA_TAU = 16.0
GLA_CHUNK = 64
NSA_HEADS = 16
NSA_GROUPS = 4
NSA_HPG = NSA_HEADS // NSA_GROUPS
NSA_DH = 64
CMP_LEN = 32
CMP_STRIDE = 16
CMP_HIDDEN = 256
SEL_LEN = 64
SEL_TOPN = 8
WINDOW = 512
XA_HEADS = 4
XA_DH = 128
N_EXPERTS = 16
N_GROUPS = 4
EXPERTS_PER_GROUP = N_EXPERTS // N_GROUPS
TOP_K = 2
D_FF = 1536
DN_ALPHA = float((2 * DEPTH) ** 0.25)
LN_EPS = 1e-5
NEG = -1e30
FORCE_BONUS = 1e6

GLA_QK = GLA_HEADS * GLA_DK
GLA_V = GLA_HEADS * GLA_DV
NSA_Q = NSA_HEADS * NSA_DH
NSA_KV = NSA_GROUPS * NSA_DH

LANES = 128
VMEM_LIMIT = 56 * 1024 * 1024

COL_MG = 0
COL_GQ = 2 * 2048
COL_GK = COL_GQ + GLA_QK
COL_GV = COL_GK + GLA_QK
COL_GR = COL_GV + GLA_V
COL_KS = COL_GR + GLA_V
COL_KW = COL_KS + NSA_GROUPS * LANES
COL_END = COL_KW + NSA_GROUPS * LANES
SCOL_CK = LANES
SCOL_CV = SCOL_CK + NSA_KV
SCOL_END = SCOL_CV + NSA_KV
TROW_Q = 0
TROW_VS = NSA_Q
TROW_VW = TROW_VS + NSA_KV
TROW_END = TROW_VW + NSA_KV

MOE_BLOCK = 512
FF_TILE = 512


def _cp(sem):
    return pltpu.CompilerParams(dimension_semantics=sem, vmem_limit_bytes=VMEM_LIMIT)


def _dot(a, b):
    return jnp.dot(a, b, preferred_element_type=F32)


def _dot_nt(a, b):
    return lax.dot_general(a, b, (((1,), (1,)), ((), ())), preferred_element_type=F32)


def _dot_tn(a, b):
    return lax.dot_general(a, b, (((0,), (0,)), ((), ())), preferred_element_type=F32)


def _layer_norm(z, g, b):
    mu = jnp.mean(z, axis=-1, keepdims=True)
    zc = z - mu
    var = jnp.mean(zc * zc, axis=-1, keepdims=True)
    return zc * lax.rsqrt(var + LN_EPS) * g + b


def _mm_kernel(a_ref, b_ref, o_ref):
    o_ref[...] = _dot(a_ref[...], b_ref[...]).astype(o_ref.dtype)


def _matmul(a, b, out_dtype, tm, tn):
    m, k = a.shape
    n = b.shape[1]
    return pl.pallas_call(
        _mm_kernel,
        grid=(m // tm, n // tn),
        in_specs=[pl.BlockSpec((tm, k), lambda i, j: (i, 0)),
                  pl.BlockSpec((k, tn), lambda i, j: (0, j))],
        out_specs=pl.BlockSpec((tm, tn), lambda i, j: (i, j)),
        out_shape=jax.ShapeDtypeStruct((m, n), out_dtype),
        compiler_params=_cp(("parallel", "parallel")),
        name="matmul",
    )(a, b)


def _mm_nt_kernel(wt_ref, x_ref, o_ref):
    o_ref[0] = _dot_nt(wt_ref[...], x_ref[...]).astype(o_ref.dtype)


def _matmul_t(x, wt, B, T, tm, tr):
    n, k = x.shape
    r = wt.shape[0]
    nt = T // tm
    return pl.pallas_call(
        _mm_nt_kernel,
        grid=(n // tm, r // tr),
        in_specs=[pl.BlockSpec((tr, k), lambda i, j: (j, 0)),
                  pl.BlockSpec((tm, k), lambda i, j: (i, 0))],
        out_specs=pl.BlockSpec((1, tr, tm), lambda i, j: (i // nt, j, i % nt)),
        out_shape=jax.ShapeDtypeStruct((B, r, T), BF16),
        compiler_params=_cp(("parallel", "parallel")),
        name="matmul_t",
    )(wt, x)


def _gla_kernel(q_ref, k_ref, v_ref, r_ref, sm_ref, wa_ref, ba_ref, ng_ref, o_ref, st_ref):
    C = GLA_CHUNK
    n_chunks = q_ref.shape[0] // C
    st_ref[...] = jnp.zeros_like(st_ref)
    rowi = lax.broadcasted_iota(I32, (C, GLA_DK), 0)
    tt = lax.broadcasted_iota(I32, (C, C), 0)
    ss = lax.broadcasted_iota(I32, (C, C), 1)
    levels = (1, 2, 4, 8, 16, 32)
    pair_masks = [((tt // (2 * L)) == (ss // (2 * L))) & ((tt & L) != 0) & ((ss & L) == 0) for L in levels]
    diag_mask = tt == ss
    scale = GLA_DK ** -0.5

    def head_chunk(rows, h, z):
        qk_cols = slice(h * GLA_DK, (h + 1) * GLA_DK)
        v_cols = slice(h * GLA_DV, (h + 1) * GLA_DV)
        q = q_ref[rows, qk_cols].astype(F32) * scale
        k = k_ref[rows, qk_cols].astype(F32)
        v = v_ref[rows, v_cols]
        g = (jnp.minimum(z, 0.0) - jnp.log1p(jnp.exp(-jnp.abs(z)))) * (1.0 / GLA_TAU)
        incl = g
        tot = g
        att = jnp.where(diag_mask, _dot_nt(q.astype(BF16), k.astype(BF16)), 0.0)
        for L, pm in zip(levels, pair_masks):
            ql = (q * jnp.exp(incl)).astype(BF16)
            kl = (k * jnp.exp(tot - incl)).astype(BF16)
            att = jnp.where(pm, _dot_nt(ql, kl), att)
            upper = (rowi & L) != 0
            from_lower = pltpu.roll(tot, L, 0)
            from_upper = pltpu.roll(tot, C - L, 0)
            incl = incl + jnp.where(upper, from_lower, 0.0)
            tot = tot + jnp.where(upper, from_lower, from_upper)
        qd = (q * jnp.exp(incl)).astype(BF16)
        kd = (k * jnp.exp(tot - incl)).astype(BF16)
        st = st_ref[h]
        o = _dot_nt(qd, st.astype(BF16)) + _dot(att.astype(BF16), v)
        st_ref[h] = st * jnp.exp(tot[0:1, :]) + _dot_tn(v, kd)
        mu = jnp.mean(o, axis=-1, keepdims=True)
        oc = o - mu
        var = jnp.mean(oc * oc, axis=-1, keepdims=True)
        on = oc * lax.rsqrt(var + LN_EPS) * ng_ref[:, v_cols]
        r = r_ref[rows, v_cols].astype(F32)
        o_ref[rows, v_cols] = (on * (r * jax.nn.sigmoid(r))).astype(o_ref.dtype)

    def chunk(c, carry):
        rows = pl.ds(pl.multiple_of(c * C, C), C)
        z = _dot(sm_ref[rows, :].astype(BF16), wa_ref[...]) + ba_ref[...]
        for h in range(GLA_HEADS):
            head_chunk(rows, h, z[:, h * GLA_DK:(h + 1) * GLA_DK])
        return carry

    lax.fori_loop(0, n_chunks, chunk, 0)


def _gla(h_big, h_small, wa_pad, b_a, norm_g, B, T):
    n = B * T
    return pl.pallas_call(
        _gla_kernel,
        grid=(B,),
        in_specs=[
            pl.BlockSpec((T, GLA_QK), lambda b: (b, COL_GQ // GLA_QK)),
            pl.BlockSpec((T, GLA_QK), lambda b: (b, COL_GK // GLA_QK)),
            pl.BlockSpec((T, GLA_V), lambda b: (b, COL_GV // GLA_V)),
            pl.BlockSpec((T, GLA_V), lambda b: (b, COL_GR // GLA_V)),
            pl.BlockSpec((T, LANES), lambda b: (b, 0)),
            pl.BlockSpec((LANES, GLA_QK), lambda b: (0, 0)),
            pl.BlockSpec((1, GLA_QK), lambda b: (0, 0)),
            pl.BlockSpec((1, GLA_V), lambda b: (0, 0)),
        ],
        out_specs=pl.BlockSpec((T, GLA_V), lambda b: (b, 0)),
        out_shape=jax.ShapeDtypeStruct((n, GLA_V), BF16),
        scratch_shapes=[pltpu.VMEM((GLA_HEADS, GLA_DV, GLA_DK), F32)],
        compiler_params=_cp(("parallel",)),
        name="gla",
    )(h_big, h_big, h_big, h_big, h_small, wa_pad, b_a, norm_g)


def _compress_kernel(x_ref, w1_ref, w2_ref, w2t_ref, pe_ref, o_ref, ot_ref):
    nc = x_ref.shape[0] // CMP_STRIDE
    hid_w = w1_ref.shape[3]
    a = jnp.zeros((nc, hid_w), F32)
    bm = jnp.zeros((nc, hid_w), F32)
    c = jnp.zeros((pe_ref.shape[2], hid_w), F32)
    for l in range(CMP_STRIDE):
        xl = x_ref[pl.ds(l, nc, stride=CMP_STRIDE), :].astype(BF16)
        a = a + _dot(xl, w1_ref[0, l])
        bm = bm + _dot(xl, w1_ref[0, CMP_STRIDE + l])
    for l in range(CMP_LEN):
        c = c + _dot(pe_ref[0, l], w1_ref[0, l])
    hid = a + pltpu.roll(bm, nc - 1, 0) + c[0:1, :]
    act = jax.nn.gelu(hid).astype(BF16)
    o_ref[0, 0, 0] = _dot(act, w2_ref[0]).astype(o_ref.dtype)
    ot_ref[0, 0, 0] = _dot_nt(w2t_ref[0], act).astype(ot_ref.dtype)


def _compress(h_small, w1bd, w2bd, w2bdt, pe_pair, B, T):
    nc = T // CMP_STRIDE
    pairs = NSA_GROUPS // 2
    return pl.pallas_call(
        _compress_kernel,
        grid=(B, 2, pairs),
        in_specs=[
            pl.BlockSpec((T, LANES), lambda b, s, j: (b, SCOL_CK // LANES + s * pairs + j)),
            pl.BlockSpec((1,) + w1bd.shape[1:], lambda b, s, j: (s, 0, 0, 0)),
            pl.BlockSpec((1,) + w2bd.shape[1:], lambda b, s, j: (s, 0, 0)),
            pl.BlockSpec((1,) + w2bdt.shape[1:], lambda b, s, j: (s, 0, 0)),
            pl.BlockSpec((1,) + pe_pair.shape[1:], lambda b, s, j: (s, 0, 0, 0)),
        ],
        out_specs=[
            pl.BlockSpec((1, 1, 1, nc, LANES), lambda b, s, j: (b, s, j, 0, 0)),
            pl.BlockSpec((1, 1, 1, LANES, nc), lambda b, s, j: (b, s, j, 0, 0)),
        ],
        out_shape=[jax.ShapeDtypeStruct((B, 2, pairs, nc, LANES), BF16),
                   jax.ShapeDtypeStruct((B, 2, pairs, LANES, nc), BF16)],
        compiler_params=_cp(("parallel", "parallel", "parallel")),
        name="nsa_compress",
    )(h_small, w1bd, w2bd, w2bdt, pe_pair)


def _cmp_select_kernel(slopes_ref, qt_ref, kc_ref, vct_ref, ovt_ref, ocmp_ref, mb_ref, qpad_ref):
    g = pl.program_id(1)
    i = pl.program_id(2)
    tq = qt_ref.shape[2]
    nc = kc_ref.shape[3]
    ns = mb_ref.shape[2]
    dh = NSA_DH
    t0 = i * tq
    tpos = (t0 + lax.broadcasted_iota(I32, (nc, tq), 1))
    nidx = lax.broadcasted_iota(I32, (nc, tq), 0)
    mask_c = (nidx * CMP_STRIDE + (CMP_LEN - 1)) <= tpos
    absd = jnp.abs(tpos.astype(F32) - (nidx.astype(F32) * CMP_STRIDE + 0.5 * (CMP_LEN - 1)))
    lower = g % 2 == 0
    kc = kc_ref[0, 0, 0]
    vct = jnp.where(lower, vct_ref[0, 0, 0, 0:dh, :], vct_ref[0, 0, 0, dh:2 * dh, :])
    psum = jnp.zeros((nc, tq), F32)
    for hh in range(NSA_HPG):
        slope = slopes_ref[g * NSA_HPG + hh]
        q = qt_ref[0, hh * dh:(hh + 1) * dh, :] * jnp.asarray(dh ** -0.5, BF16)
        zero = jnp.zeros_like(q)
        qpad_ref[0:dh, :] = jnp.where(lower, q, zero)
        qpad_ref[dh:2 * dh, :] = jnp.where(lower, zero, q)
        s = _dot(kc, qpad_ref[...]) - slope * absd
        s = jnp.where(mask_c, s, NEG)
        e = jnp.exp(s - jnp.max(s, axis=0, keepdims=True))
        p = jnp.where(mask_c, e / jnp.sum(e, axis=0, keepdims=True), 0.0)
        ocmp_ref[0, hh * dh:(hh + 1) * dh, :] = _dot(vct, p.astype(BF16)).astype(ocmp_ref.dtype)
        psum = psum + p
    p_hi = psum.astype(BF16)
    p_lo = (psum - p_hi.astype(F32)).astype(BF16)
    imp = _dot(ovt_ref[...], p_hi) + _dot(ovt_ref[...], p_lo)
    j = lax.broadcasted_iota(I32, (ns, tq), 0)
    tp = t0 + lax.broadcasted_iota(I32, (ns, tq), 1)
    cur = tp // SEL_LEN
    forced = (j == 0) | (j == cur) | (j == cur - 1)
    valid = j * SEL_LEN <= tp
    score = jnp.where(valid, imp + jnp.where(forced, FORCE_BONUS, 0.0), NEG)
    rank = jnp.zeros((ns, tq), F32)
    for jp in range(ns):
        row = score[jp:jp + 1, :]
        beats = (row > score) | ((row == score) & (j > jp))
        rank = rank + jnp.where(beats, 1.0, 0.0)
    keep = valid & (rank < float(min(SEL_TOPN, ns)))
    mb_ref[0, 0] = jnp.where(keep, 0.0, NEG).astype(mb_ref.dtype)


def _cmp_select(slopes, h_t, kcmp, kcmp_t, ovt, B, T, tq):
    nc = T // CMP_STRIDE
    ns = T // SEL_LEN
    grp_rows = NSA_HPG * NSA_DH
    grid_spec = pltpu.PrefetchScalarGridSpec(
        num_scalar_prefetch=1,
        grid=(B, NSA_GROUPS, T // tq),
        in_specs=[
            pl.BlockSpec((1, grp_rows, tq), lambda b, g, i, s: (b, TROW_Q // grp_rows + g, i)),
            pl.BlockSpec((1, 1, 1, nc, LANES), lambda b, g, i, s: (b, 0, g // 2, 0, 0)),
            pl.BlockSpec((1, 1, 1, LANES, nc), lambda b, g, i, s: (b, 1, g // 2, 0, 0)),
            pl.BlockSpec((ns, nc), lambda b, g, i, s: (0, 0)),
        ],
        out_specs=[
            pl.BlockSpec((1, grp_rows, tq), lambda b, g, i, s: (b, g, i)),
            pl.BlockSpec((1, 1, ns, tq), lambda b, g, i, s: (b, g, 0, i)),
        ],
        scratch_shapes=[pltpu.VMEM((LANES, tq), BF16)],
    )
    return pl.pallas_call(
        _cmp_select_kernel,
        grid_spec=grid_spec,
        out_shape=[jax.ShapeDtypeStruct((B, NSA_Q, T), BF16),
                   jax.ShapeDtypeStruct((B, NSA_GROUPS, ns, T), BF16)],
        compiler_params=_cp(("parallel", "parallel", "parallel")),
        name="nsa_cmp_select",
    )(slopes, h_t, kcmp, kcmp_t, ovt)


def _sel_win_kernel(slopes_ref, qt_ref, ks_ref, kw_ref, vs_ref, vw_ref, epad_ref, mb_ref, ocmp_ref, gt_ref,
                    o_ref, qaug_ref, m_ref, l_ref, acc_ref, srow_ref, bias_ref, s_ref, p_ref, alpha_ref,
                    kall_ref, vall_ref):
    g = pl.program_id(1)
    i = pl.program_id(2)
    tq = qt_ref.shape[2]
    tk = tq
    ns = mb_ref.shape[2]
    dh = NSA_DH
    wide = NSA_HPG * tq
    t0 = i * tq
    BIG = -NEG

    @pl.when(i == 0)
    def _():
        srow = jnp.concatenate([jnp.full((1, tq), slopes_ref[g * NSA_HPG + hh], F32) for hh in range(NSA_HPG)],
                               axis=1)
        srow_ref[...] = srow
        lane = lax.broadcasted_iota(I32, (tk, wide), 1) & (tq - 1)
        dist0 = (lane - lax.broadcasted_iota(I32, (tk, wide), 0)).astype(F32)
        sd0 = srow * dist0
        bias_ref[0] = sd0
        bias_ref[1] = sd0 + jnp.where(dist0 >= 0.0, 0.0, BIG)
        bias_ref[2] = sd0 + jnp.where(dist0 < 0.0, 0.0, BIG)
        bias_ref[3] = jnp.full((tk, wide), BIG, F32)
        kall_ref[0] = ks_ref[...] + epad_ref[...]
        kall_ref[1] = kw_ref[...]
        vall_ref[0] = vs_ref[0]
        vall_ref[1] = vw_ref[0]

    for hh in range(NSA_HPG):
        cols = slice(hh * tq, (hh + 1) * tq)
        qaug_ref[0:dh, cols] = qt_ref[0, hh * dh:(hh + 1) * dh, :] * jnp.asarray(dh ** -0.5, BF16)
        qaug_ref[dh:dh + ns, cols] = mb_ref[0, 0]
        qaug_ref[dh + ns:, cols] = jnp.zeros((qaug_ref.shape[0] - dh - ns, tq), BF16)

    m_ref[...] = jnp.full(m_ref.shape, NEG, F32)
    l_ref[...] = jnp.zeros(l_ref.shape, F32)
    acc_ref[...] = jnp.zeros(acc_ref.shape, F32)

    n_back = WINDOW // tk
    n_sel = i + 1
    n_win = jnp.minimum(i, n_back) + 1
    n_steps = n_sel + n_win

    def describe(n):
        n = jnp.maximum(n, 0)
        is_win = n >= n_sel
        kb = jnp.clip(jnp.where(is_win, i - n_win + 1 + (n - n_sel), n), 0, i)
        mode = jnp.where(kb == i, 1, jnp.where(is_win & (kb == i - n_back), 2, 0))
        mode = jnp.where(n >= n_steps, 3, mode)
        return is_win.astype(I32), kb, mode

    def scores(n, slot):
        br, kb, _ = describe(n)
        s0 = pl.multiple_of(kb * tk, tk)
        s_ref[slot] = _dot(kall_ref[br, pl.ds(s0, tk), :], qaug_ref[...])

    def softmax(n, slot):
        br, kb, mode = describe(n)
        crow = srow_ref[...] * ((i - kb) * tk).astype(F32)
        s = s_ref[slot] - bias_ref[mode]
        m_old = m_ref[br]
        m_new = jnp.maximum(m_old, jnp.max(s, axis=0, keepdims=True) - crow)
        alpha = jnp.exp(m_old - m_new)
        p = jnp.exp(s - (m_new + crow))
        l_ref[br] = alpha * l_ref[br] + jnp.sum(p, axis=0, keepdims=True)
        m_ref[br] = m_new
        alpha_ref[slot] = alpha
        p_ref[slot] = p.astype(BF16)

    def weighted_values(n, slot):
        br, kb, _ = describe(n)
        s0 = pl.multiple_of(kb * tk, tk)
        acc_ref[br] = alpha_ref[slot] * acc_ref[br] + _dot(vall_ref[br, :, pl.ds(s0, tk)], p_ref[slot])

    p_ref[1] = jnp.zeros(p_ref.shape[1:], BF16)
    alpha_ref[1] = jnp.ones(alpha_ref.shape[1:], F32)
    scores(0, 0)

    def pair(j, carry):
        n = 2 * j
        scores(n + 1, 1)
        softmax(n, 0)
        weighted_values(n - 1, 1)
        scores(n + 2, 0)
        softmax(n + 1, 1)
        weighted_values(n, 0)
        return carry

    n_pairs = (n_steps + 1) // 2
    lax.fori_loop(0, n_pairs, pair, 0)
    weighted_values(2 * n_pairs - 1, 1)

    def gate_row(branch):
        rows = [gt_ref[0, 0, 3 * hh + branch:3 * hh + branch + 1, :] for hh in range(NSA_HPG)]
        return jax.nn.sigmoid(jnp.concatenate(rows, axis=1))

    o = gate_row(1) * (acc_ref[0] / l_ref[0]) + gate_row(2) * (acc_ref[1] / l_ref[1])
    ocmp = jnp.concatenate([ocmp_ref[0, hh * dh:(hh + 1) * dh, :] for hh in range(NSA_HPG)], axis=1).astype(F32)
    o = o + gate_row(0) * ocmp
    o_heads = jnp.concatenate([o[:, hh * tq:(hh + 1) * tq] for hh in range(NSA_HPG)], axis=0)
    o_ref[...] = o_heads.T.astype(o_ref.dtype)


def _sel_win(slopes, h_t, h_big, epad, mb, ocmp_t, gates_t, B, T, tq):
    ns = T // SEL_LEN
    kaug = LANES
    G = NSA_GROUPS
    grp_rows = NSA_HPG * NSA_DH
    nq = T // tq
    grid_spec = pltpu.PrefetchScalarGridSpec(
        num_scalar_prefetch=1,
        grid=(B, G, nq),
        in_specs=[
            pl.BlockSpec((1, grp_rows, tq), lambda b, g, i, s: (b, TROW_Q // grp_rows + g, i)),
            pl.BlockSpec((T, LANES), lambda b, g, i, s: (b, COL_KS // LANES + g)),
            pl.BlockSpec((T, LANES), lambda b, g, i, s: (b, COL_KW // LANES + g)),
            pl.BlockSpec((1, NSA_DH, T), lambda b, g, i, s: (b, TROW_VS // NSA_DH + g, 0)),
            pl.BlockSpec((1, NSA_DH, T), lambda b, g, i, s: (b, TROW_VW // NSA_DH + g, 0)),
            pl.BlockSpec((T, LANES), lambda b, g, i, s: (0, 0)),
            pl.BlockSpec((1, 1, ns, tq), lambda b, g, i, s: (b, g, 0, i)),
            pl.BlockSpec((1, grp_rows, tq), lambda b, g, i, s: (b, g, i)),
            pl.BlockSpec((1, 1, 16, tq), lambda b, g, i, s: (b, g, 0, i)),
        ],
        out_specs=pl.BlockSpec((tq, grp_rows), lambda b, g, i, s: (b * nq + i, g)),
        scratch_shapes=[
            pltpu.VMEM((kaug, NSA_HPG * tq), BF16),
            pltpu.VMEM((2, 1, NSA_HPG * tq), F32),
            pltpu.VMEM((2, 1, NSA_HPG * tq), F32),
            pltpu.VMEM((2, NSA_DH, NSA_HPG * tq), F32),
            pltpu.VMEM((1, NSA_HPG * tq), F32),
            pltpu.VMEM((4, tq, NSA_HPG * tq), F32),
            pltpu.VMEM((2, tq, NSA_HPG * tq), F32),
            pltpu.VMEM((2, tq, NSA_HPG * tq), BF16),
            pltpu.VMEM((2, 1, NSA_HPG * tq), F32),
            pltpu.VMEM((2, T, kaug), BF16),
            pltpu.VMEM((2, NSA_DH, T), BF16),
        ],
    )
    return pl.pallas_call(
        _sel_win_kernel,
        grid_spec=grid_spec,
        out_shape=jax.ShapeDtypeStruct((B * T, NSA_Q), BF16),
        compiler_params=_cp(("parallel", "parallel", "arbitrary")),
        name="nsa_sel_win",
    )(slopes, h_t, h_big, h_big, h_t, h_t, epad, mb, ocmp_t, gates_t)


def _mix_kernel(og_ref, on_ref, mg1_ref, mg2_ref, x_ref, wg_ref, wn_ref, wo_ref, lg_ref, lb_ref,
                x1_ref, x1b_ref):
    g1 = _dot(og_ref[...], wg_ref[...])
    g2 = _dot(on_ref[...], wn_ref[...])
    merged = (jax.nn.sigmoid(mg1_ref[...].astype(F32)) * g1
              + jax.nn.sigmoid(mg2_ref[...].astype(F32)) * g2)
    y = _dot(merged.astype(BF16), wo_ref[...])
    x1 = _layer_norm(DN_ALPHA * x_ref[...] + y, lg_ref[...], lb_ref[...])
    x1_ref[...] = x1
    x1b_ref[...] = x1.astype(BF16)


def _const_spec(shape):
    nd = len(shape)
    return pl.BlockSpec(shape, lambda *_: (0,) * nd, pipeline_mode=pl.Buffered(1))


def _mix(o_gla, o_nsa, h_big, x, wg, wn, wo, lg, lb, tm):
    n, d = x.shape
    return pl.pallas_call(
        _mix_kernel,
        grid=(n // tm,),
        in_specs=[
            pl.BlockSpec((tm, GLA_V), lambda i: (i, 0)),
            pl.BlockSpec((tm, NSA_Q), lambda i: (i, 0)),
            pl.BlockSpec((tm, d), lambda i: (i, 0)),
            pl.BlockSpec((tm, d), lambda i: (i, 1)),
            pl.BlockSpec((tm, d), lambda i: (i, 0)),
            _const_spec(wg.shape), _const_spec(wn.shape), _const_spec(wo.shape),
            _const_spec(lg.shape), _const_spec(lb.shape),
        ],
        out_specs=[pl.BlockSpec((tm, d), lambda i: (i, 0)), pl.BlockSpec((tm, d), lambda i: (i, 0))],
        out_shape=[jax.ShapeDtypeStruct((n, d), F32), jax.ShapeDtypeStruct((n, d), BF16)],
        compiler_params=_cp(("parallel",)),
        name="mix_ln",
    )(o_gla, o_nsa, h_big, h_big, x, wg, wn, wo, lg, lb)


def _xattn_kernel(x_ref, xb_ref, kv_ref, wq_ref, wo_ref, lg_ref, lb_ref, x2_ref):
    hd = XA_HEADS * XA_DH
    q = (_dot(xb_ref[...], wq_ref[...]) * (XA_DH ** -0.5)).astype(BF16)
    outs = []
    for h in range(XA_HEADS):
        kh = kv_ref[0, :, h * XA_DH:(h + 1) * XA_DH]
        vh = kv_ref[0, :, hd + h * XA_DH:hd + (h + 1) * XA_DH]
        s = _dot_nt(q[:, h * XA_DH:(h + 1) * XA_DH], kh)
        e = jnp.exp(s - jnp.max(s, axis=-1, keepdims=True))
        p = e / jnp.sum(e, axis=-1, keepdims=True)
        outs.append(_dot(p.astype(BF16), vh).astype(BF16))
    o = jnp.concatenate(outs, axis=-1)
    y = _dot(o, wo_ref[...])
    x2_ref[...] = _layer_norm(DN_ALPHA * x_ref[...] + y, lg_ref[...], lb_ref[...])


def _xattn(x1, x1b, kv, wq, wo, lg, lb, B, T, tm):
    n, d = x1.shape
    nt = T // tm
    return pl.pallas_call(
        _xattn_kernel,
        grid=(B, nt),
        in_specs=[
            pl.BlockSpec((tm, d), lambda b, i: (b * nt + i, 0)),
            pl.BlockSpec((tm, d), lambda b, i: (b * nt + i, 0)),
            pl.BlockSpec((1,) + kv.shape[1:], lambda b, i: (b, 0, 0)),
            _const_spec(wq.shape), _const_spec(wo.shape), _const_spec(lg.shape), _const_spec(lb.shape),
        ],
        out_specs=pl.BlockSpec((tm, d), lambda b, i: (b * nt + i, 0)),
        out_shape=jax.ShapeDtypeStruct((n, d), F32),
        compiler_params=_cp(("parallel", "parallel")),
        name="xattn_ln",
    )(x1, x1b, kv, wq, wo, lg, lb)


def _router_kernel(x_ref, wh_ref, wl_ref, rb_ref, e_ref, gate_ref, rank_ref, cnt_ref, carry_ref):
    i = pl.program_id(0)
    tr = x_ref.shape[0]
    E = N_EXPERTS

    @pl.when(i == 0)
    def _():
        carry_ref[...] = jnp.zeros_like(carry_ref)

    x = x_ref[...]
    x_hi = x.astype(BF16)
    x_lo = (x - x_hi.astype(F32)).astype(BF16)
    wh = wh_ref[...]
    logits = _dot_nt(wh, x_hi) + _dot_nt(wh, x_lo) + _dot_nt(wl_ref[...], x_hi)
    biased = logits + rb_ref[...]
    rows = [biased[e:e + 1, :] for e in range(E)]
    raw = [logits[e:e + 1, :] for e in range(E)]
    best_score = None
    best = None
    for gi in range(N_GROUPS):
        v = rows[gi * EXPERTS_PER_GROUP:(gi + 1) * EXPERTS_PER_GROUP]
        sc = None
        for a in range(EXPERTS_PER_GROUP):
            for b in range(a + 1, EXPERTS_PER_GROUP):
                pair = v[a] + v[b]
                sc = pair if sc is None else jnp.maximum(sc, pair)
        if best is None:
            best_score, best = sc, jnp.zeros((1, tr), I32)
        else:
            better = sc > best_score
            best_score = jnp.where(better, sc, best_score)
            best = jnp.where(better, gi, best)

    def pick(vals):
        out = vals[0:EXPERTS_PER_GROUP]
        for gi in range(1, N_GROUPS):
            out = [jnp.where(best == gi, vals[gi * EXPERTS_PER_GROUP + a], out[a]) for a in range(EXPERTS_PER_GROUP)]
        return out

    w = pick(rows)
    lraw = pick(raw)
    i1 = jnp.zeros((1, tr), I32)
    v1 = w[0]
    l1 = lraw[0]
    for a in range(1, EXPERTS_PER_GROUP):
        better = w[a] > v1
        v1 = jnp.where(better, w[a], v1)
        l1 = jnp.where(better, lraw[a], l1)
        i1 = jnp.where(better, a, i1)
    i2 = jnp.full((1, tr), -1, I32)
    v2 = jnp.full((1, tr), -jnp.inf, F32)
    l2 = jnp.zeros((1, tr), F32)
    for a in range(EXPERTS_PER_GROUP):
        better = (i1 != a) & ((w[a] > v2) | (i2 < 0))
        v2 = jnp.where(better, w[a], v2)
        l2 = jnp.where(better, lraw[a], l2)
        i2 = jnp.where(better, a, i2)
    e1 = best * EXPERTS_PER_GROUP + i1
    e2 = best * EXPERTS_PER_GROUP + i2
    mx = jnp.maximum(l1, l2)
    p1 = jnp.exp(l1 - mx)
    p2 = jnp.exp(l2 - mx)
    den = p1 + p2
    e_ref[0:1, :] = e1
    e_ref[1:2, :] = e2
    gate_ref[0:1, :] = p1 / den
    gate_ref[1:2, :] = p2 / den
    eidx = lax.broadcasted_iota(I32, (E, tr), 0)
    is1 = eidx == e1
    is2 = eidx == e2
    member = jnp.where(is1 | is2, 1.0, 0.0)
    uu = lax.broadcasted_iota(I32, (tr, tr), 0)
    tt = lax.broadcasted_iota(I32, (tr, tr), 1)
    tri = jnp.where(uu <= tt, 1.0, 0.0).astype(BF16)
    incl = _dot(member.astype(BF16), tri)
    excl = carry_ref[:, 0:1] + incl - member
    rank_ref[0:1, :] = jnp.sum(jnp.where(is1, excl, 0.0), axis=0, keepdims=True).astype(I32)
    rank_ref[1:2, :] = jnp.sum(jnp.where(is2, excl, 0.0), axis=0, keepdims=True).astype(I32)
    new_carry = carry_ref[...] + jnp.sum(member, axis=1, keepdims=True)
    carry_ref[...] = new_carry
    cnt_ref[...] = new_carry


def _router(x2, rw_hi, rw_lo, rb, tr):
    n, d = x2.shape
    E = N_EXPERTS
    return pl.pallas_call(
        _router_kernel,
        grid=(n // tr,),
        in_specs=[
            pl.BlockSpec((tr, d), lambda i: (i, 0)),
            pl.BlockSpec((E, d), lambda i: (0, 0)),
            pl.BlockSpec((E, d), lambda i: (0, 0)),
            pl.BlockSpec((E, 1), lambda i: (0, 0)),
        ],
        out_specs=[
            pl.BlockSpec((2, tr), lambda i: (0, i)),
            pl.BlockSpec((2, tr), lambda i: (0, i)),
            pl.BlockSpec((2, tr), lambda i: (0, i)),
            pl.BlockSpec((E, LANES), lambda i: (0, 0)),
        ],
        out_shape=[jax.ShapeDtypeStruct((2, n), I32), jax.ShapeDtypeStruct((2, n), F32),
                   jax.ShapeDtypeStruct((2, n), I32), jax.ShapeDtypeStruct((E, LANES), F32)],
        scratch_shapes=[pltpu.VMEM((E, LANES), F32)],
        compiler_params=_cp(("arbitrary",)),
        name="moe_router",
    )(x2, rw_hi, rw_lo, rb)


def _dispatch_kernel(ps_ref, pe_ref, e_ref, rank_ref, x_ref, buf_hbm, zero_ref, sem):
    td = e_ref.shape[1]

    @pl.when(pl.program_id(0) == 0)
    def _():
        zero_ref[...] = jnp.zeros_like(zero_ref)

        def zero_copy(ex):
            last = pl.multiple_of(jnp.maximum(pe_ref[ex] - MOE_BLOCK, 0), MOE_BLOCK)
            return pltpu.make_async_copy(zero_ref, buf_hbm.at[pl.ds(last, MOE_BLOCK), :], sem)

        def nonempty(ex):
            return pe_ref[ex] > (pe_ref[ex - 1] if ex > 0 else 0)

        n_blocks = buf_hbm.shape[0] // MOE_BLOCK
        first_unused = pe_ref[N_EXPERTS - 1] // MOE_BLOCK

        def tail_copy(k):
            row = pl.multiple_of((first_unused + k) * MOE_BLOCK, MOE_BLOCK)
            return pltpu.make_async_copy(zero_ref, buf_hbm.at[pl.ds(row, MOE_BLOCK), :], sem)

        for ex in range(N_EXPERTS):
            pl.when(nonempty(ex))(lambda ex=ex: zero_copy(ex).start())
            pl.when(first_unused + ex < n_blocks)(lambda ex=ex: tail_copy(ex).start())
        for ex in range(N_EXPERTS):
            pl.when(nonempty(ex))(lambda ex=ex: zero_copy(ex).wait())
            pl.when(first_unused + ex < n_blocks)(lambda ex=ex: tail_copy(ex).wait())

    def issue(t, carry):
        for kk in range(TOP_K):
            dest = ps_ref[e_ref[kk, t]] + rank_ref[kk, t]
            pltpu.make_async_copy(x_ref.at[pl.ds(t, 1), :], buf_hbm.at[pl.ds(dest, 1), :], sem).start()
        return carry

    lax.fori_loop(0, td, issue, 0, unroll=8)
    for kk in range(TOP_K):
        pltpu.make_async_copy(x_ref, buf_hbm.at[pl.ds(0, td), :], sem).wait()


def _dispatch(pad_start, pad_end, e, rank, x2, n_rows, td):
    n, d = x2.shape
    grid_spec = pltpu.PrefetchScalarGridSpec(
        num_scalar_prefetch=2,
        grid=(n // td,),
        in_specs=[
            pl.BlockSpec((2, td), lambda i, s, t: (0, i), memory_space=pltpu.SMEM),
            pl.BlockSpec((2, td), lambda i, s, t: (0, i), memory_space=pltpu.SMEM),
            pl.BlockSpec((td, d), lambda i, s, t: (i, 0)),
        ],
        out_specs=pl.BlockSpec(memory_space=pl.ANY),
        scratch_shapes=[pltpu.VMEM((MOE_BLOCK, d), F32), pltpu.SemaphoreType.DMA(())],
    )
    return pl.pallas_call(
        _dispatch_kernel,
        grid_spec=grid_spec,
        out_shape=jax.ShapeDtypeStruct((n_rows, d), F32),
        compiler_params=_cp(("arbitrary",)),
        name="moe_dispatch",
    )(pad_start, pad_end, e, rank, x2)


def _expert_kernel(be_ref, nb_ref, x_ref, win_hbm, wdn_hbm, y_ref, xb_ref, wa_s, wu_s, wd_s, sa, su, sd, sems,
                   *, layer):
    b = pl.program_id(0)
    nf = D_FF // FF_TILE
    n_used = nb_ref[0]
    e = be_ref[b]
    e_prev = be_ref[jnp.maximum(b - 1, 0)]
    e_next = be_ref[jnp.minimum(b + 1, pl.num_programs(0) - 1)]
    active = b < n_used
    is_first = active & ((b == 0) | (e_prev != e))
    feeds_next = active & (b + 1 < n_used) & (e_next != e)

    def tile_copies(ex, f):
        lo = f * FF_TILE
        return (pltpu.make_async_copy(win_hbm.at[layer, ex, :, pl.ds(lo, FF_TILE)], sa, sems.at[0]),
                pltpu.make_async_copy(win_hbm.at[layer, ex, :, pl.ds(D_FF + lo, FF_TILE)], su, sems.at[1]),
                pltpu.make_async_copy(wdn_hbm.at[layer, ex, pl.ds(lo, FF_TILE), :], sd, sems.at[2]))

    def start(ex, f):
        for c in tile_copies(ex, f):
            c.start()

    def finish(ex, f):
        for c in tile_copies(ex, f):
            c.wait()
        wa_s[f] = sa[...].astype(BF16)
        wu_s[f] = su[...].astype(BF16)
        wd_s[f] = sd[...].astype(BF16)

    @pl.when(b == 0)
    def _():
        for f in range(nf - 1):
            start(e, f)
            finish(e, f)
        start(e, nf - 1)

    @pl.when(jnp.logical_not(active))
    def _():
        y_ref[...] = jnp.zeros_like(y_ref)

    @pl.when(active)
    def _():
        xb_ref[...] = x_ref[...].astype(BF16)
        for f in range(nf):
            xb = xb_ref[...]
            a = _dot(xb, wa_s[f])
            u = _dot(xb, wu_s[f])
            act = (a * jax.nn.sigmoid(a) * u).astype(BF16)
            y = _dot(act, wd_s[f])
            if f == 0:
                y_ref[...] = y
                pl.when(is_first)(lambda: finish(e, nf - 1))
            else:
                y_ref[...] += y

            @pl.when(feeds_next)
            def _(f=f):
                if f >= 1:
                    finish(e_next, f - 1)
                start(e_next, f)


def _experts(blk_expert, n_used, buf, w_in, w_down, layer):
    p, d = buf.shape
    nb = p // MOE_BLOCK
    nf = D_FF // FF_TILE
    grid_spec = pltpu.PrefetchScalarGridSpec(
        num_scalar_prefetch=2,
        grid=(nb,),
        in_specs=[
            pl.BlockSpec((MOE_BLOCK, d), lambda b, be, nu: (jnp.minimum(b, nu[0] - 1), 0)),
            pl.BlockSpec(memory_space=pl.ANY),
            pl.BlockSpec(memory_space=pl.ANY),
        ],
        out_specs=pl.BlockSpec((MOE_BLOCK, d), lambda b, be, nu: (b, 0)),
        scratch_shapes=[
            pltpu.VMEM((MOE_BLOCK, d), BF16),
            pltpu.VMEM((nf, d, FF_TILE), BF16),
            pltpu.VMEM((nf, d, FF_TILE), BF16),
            pltpu.VMEM((nf, FF_TILE, d), BF16),
            pltpu.VMEM((d, FF_TILE), F32),
            pltpu.VMEM((d, FF_TILE), F32),
            pltpu.VMEM((FF_TILE, d), F32),
            pltpu.SemaphoreType.DMA((3,)),
        ],
    )
    return pl.pallas_call(
        functools.partial(_expert_kernel, layer=layer),
        grid_spec=grid_spec,
        out_shape=jax.ShapeDtypeStruct((p, d), F32),
        compiler_params=_cp(("arbitrary",)),
        name="moe_experts",
    )(blk_expert, n_used, buf, w_in, w_down)


def _combine_kernel(ps_ref, e_ref, rank_ref, y_hbm, x_ref, gate_ref, lg_ref, lb_ref, x3_ref, x3b_ref,
                    y0_ref, y1_ref, sem):
    tc = x_ref.shape[0]
    bufs = (y0_ref, y1_ref)

    def issue(t, carry):
        for kk in range(TOP_K):
            src = ps_ref[e_ref[kk, t]] + rank_ref[kk, t]
            pltpu.make_async_copy(y_hbm.at[pl.ds(src, 1), :], bufs[kk].at[pl.ds(t, 1), :], sem).start()
        return carry

    lax.fori_loop(0, tc, issue, 0, unroll=8)
    for kk in range(TOP_K):
        pltpu.make_async_copy(y_hbm.at[pl.ds(0, tc), :], bufs[kk], sem).wait()
    gate = gate_ref[...]
    z = DN_ALPHA * x_ref[...] + gate[:, 0:1] * y0_ref[...] + gate[:, 1:2] * y1_ref[...]
    x3 = _layer_norm(z, lg_ref[...], lb_ref[...])
    x3_ref[...] = x3
    x3b_ref[...] = x3.astype(BF16)


def _combine(pad_start, e, rank, y, x2, gate_nt, lg, lb, tc):
    n, d = x2.shape
    grid_spec = pltpu.PrefetchScalarGridSpec(
        num_scalar_prefetch=1,
        grid=(n // tc,),
        in_specs=[
            pl.BlockSpec((2, tc), lambda i, s: (0, i), memory_space=pltpu.SMEM),
            pl.BlockSpec((2, tc), lambda i, s: (0, i), memory_space=pltpu.SMEM),
            pl.BlockSpec(memory_space=pl.ANY),
            pl.BlockSpec((tc, d), lambda i, s: (i, 0)),
            pl.BlockSpec((tc, 2), lambda i, s: (i, 0)),
            pl.BlockSpec((1, d), lambda i, s: (0, 0)),
            pl.BlockSpec((1, d), lambda i, s: (0, 0)),
        ],
        out_specs=[pl.BlockSpec((tc, d), lambda i, s: (i, 0)), pl.BlockSpec((tc, d), lambda i, s: (i, 0))],
        scratch_shapes=[pltpu.VMEM((tc, d), F32), pltpu.VMEM((tc, d), F32), pltpu.SemaphoreType.DMA(())],
    )
    return pl.pallas_call(
        _combine_kernel,
        grid_spec=grid_spec,
        out_shape=[jax.ShapeDtypeStruct((n, d), F32), jax.ShapeDtypeStruct((n, d), BF16)],
        compiler_params=_cp(("arbitrary",)),
        name="moe_combine_ln",
    )(pad_start, e, rank, y, x2, gate_nt, lg, lb)


def _layer(x, xb, mem_b, p, moe_w, layer, consts, B, T):
    n, d = x.shape
    G, HPG, DH = NSA_GROUPS, NSA_HPG, NSA_DH
    slopes, ovt, epad = consts

    h_big = _matmul(xb, p["w_big"], BF16, 1024, 512)
    h_small = _matmul(xb, p["w_small"], F32, 1024, LANES)
    h_t = _matmul_t(xb, p["w_t"], B, T, 1024, 512)

    o_gla = _gla(h_big, h_small, p["wa_pad"], p["b_a"], p["norm_g"], B, T)

    kcmp, kcmp_t = _compress(h_small, p["cmp_w1bd"], p["cmp_w2bd"], p["cmp_w2bdt"], p["cmp_pe_pair"], B, T)
    ocmp_t, mb = _cmp_select(slopes, h_t, kcmp, kcmp_t, ovt, B, T, 512)
    gates_t = h_small[:, GLA_GATE_RANK:GLA_GATE_RANK + 3 * NSA_HEADS].reshape(B, T, G, 3 * HPG)
    gates_t = jnp.pad(gates_t.transpose(0, 2, 3, 1), ((0, 0), (0, 0), (0, 16 - 3 * HPG), (0, 0)))
    o_nsa = _sel_win(slopes, h_t, h_big, epad, mb, ocmp_t, gates_t, B, T, 256)

    x1, x1b = _mix(o_gla, o_nsa, h_big, x, p["w_bg"], p["w_bn"], p["w_out"], p["ln_mix_g"], p["ln_mix_b"], 256)

    kvm = _matmul(mem_b, p["xa_wkv"], BF16, 512, 512).reshape(B, MEM_LEN, 2 * XA_HEADS * XA_DH)
    x2 = _xattn(x1, x1b, kvm, p["xa_wq"], p["xa_wo"], p["ln_xa_g"], p["ln_xa_b"], B, T, 256)

    e, gate, rank, cnt = _router(x2, p["rw_hi"], p["rw_lo"], p["rb"], 512)
    counts = cnt[:, 0].astype(I32)
    padded = (counts + MOE_BLOCK - 1) // MOE_BLOCK * MOE_BLOCK
    pad_end = jnp.cumsum(padded)
    pad_start = (pad_end - padded).astype(I32)
    nb = (n * TOP_K) // MOE_BLOCK + N_EXPERTS
    n_used = (pad_end[-1] // MOE_BLOCK).astype(I32).reshape(1)
    blk_start = jnp.arange(nb, dtype=I32) * MOE_BLOCK
    blk_expert = jnp.minimum(jnp.sum(blk_start[:, None] >= pad_end[None, :], axis=1), N_EXPERTS - 1).astype(I32)
    blk_expert = jnp.where(jnp.arange(nb) < n_used[0], blk_expert, blk_expert[jnp.maximum(n_used[0] - 1, 0)])
    buf = _dispatch(pad_start, pad_end.astype(I32), e, rank, x2, nb * MOE_BLOCK, 512)
    y = _experts(blk_expert, n_used, buf, moe_w[0], moe_w[1], layer)
    x3, x3b = _combine(pad_start, e, rank, y, x2, gate.T, p["ln_ffn_g"], p["ln_ffn_b"], 256)
    return x3, x3b


def _prep_layer(l, w_in, gla_w_a2, gla_b_a, gla_norm_g, nsa_cmp_pe, nsa_cmp_w1, nsa_cmp_w2, w_branch_gla,
                w_branch_nsa, w_out, ln_mix_g, ln_mix_b, xa_wq, xa_wkv, xa_wo, ln_xa_g, ln_xa_b, router_w,
                router_b, moe_w_in, moe_w_down, ln_ffn_g, ln_ffn_b):
    d = w_in.shape[1]
    w = w_in[l]
    o_gq, o_gk, o_gv, o_gr = 0, GLA_QK, 2 * GLA_QK, 2 * GLA_QK + GLA_V
    o_ga = o_gr + GLA_V
    o_nq = o_ga + GLA_GATE_RANK
    o_nkv = o_nq + NSA_Q
    o_ng = o_nkv + 6 * NSA_KV
    o_mg = o_ng + 3 * NSA_HEADS
    G, DH = NSA_GROUPS, NSA_DH

    def kv_cols(kind):
        return w[:, o_nkv + kind * NSA_KV:o_nkv + (kind + 1) * NSA_KV]

    def slabs(wk):
        return jnp.pad(wk.reshape(d, G, DH), ((0, 0), (0, 0), (0, LANES - DH))).reshape(d, G * LANES)

    w_big = jnp.concatenate([w[:, o_mg:o_mg + 2 * d], w[:, o_gq:o_ga], slabs(kv_cols(2)), slabs(kv_cols(4))],
                            axis=1).astype(BF16)
    w_small = jnp.concatenate([w[:, o_ga:o_nq], w[:, o_ng:o_mg],
                               jnp.zeros((d, LANES - GLA_GATE_RANK - 3 * NSA_HEADS), F32),
                               kv_cols(0), kv_cols(1)], axis=1).astype(BF16)
    w_t = jnp.concatenate([w[:, o_nq:o_nkv], kv_cols(3), kv_cols(5)], axis=1).T.astype(BF16)
    w1 = nsa_cmp_w1[l].reshape(2, CMP_LEN, DH, CMP_HIDDEN)
    z1 = jnp.zeros_like(w1)
    w1bd = jnp.concatenate([jnp.concatenate([w1, z1], axis=3), jnp.concatenate([z1, w1], axis=3)], axis=2)
    w2 = nsa_cmp_w2[l]
    z2 = jnp.zeros_like(w2)
    w2bd = jnp.concatenate([jnp.concatenate([w2, z2], axis=2), jnp.concatenate([z2, w2], axis=2)], axis=1)
    pe = nsa_cmp_pe[l]
    pe_pair = jnp.broadcast_to(jnp.concatenate([pe, pe], axis=-1)[:, :, None, :], (2, CMP_LEN, 16, 2 * DH))
    wa_pad = jnp.concatenate([gla_w_a2[l], jnp.zeros((LANES - GLA_GATE_RANK, GLA_QK), F32)], axis=0).astype(BF16)
    rw_t = router_w.T
    rw_hi = rw_t.astype(BF16)
    rw_lo = (rw_t - rw_hi.astype(F32)).astype(BF16)
    return dict(
        w_big=w_big, w_small=w_small, w_t=w_t, wa_pad=wa_pad,
        b_a=gla_b_a[l].reshape(1, -1), norm_g=gla_norm_g[l].reshape(1, -1),
        cmp_w1bd=w1bd.astype(BF16), cmp_w2bd=w2bd.astype(BF16), cmp_w2bdt=w2bd.transpose(0, 2, 1).astype(BF16),
        cmp_pe_pair=pe_pair.astype(BF16),
        w_bg=w_branch_gla[l].astype(BF16), w_bn=w_branch_nsa[l].astype(BF16), w_out=w_out[l].astype(BF16),
        ln_mix_g=ln_mix_g[l].reshape(1, -1), ln_mix_b=ln_mix_b[l].reshape(1, -1),
        xa_wq=xa_wq[l].astype(BF16), xa_wkv=xa_wkv[l].astype(BF16), xa_wo=xa_wo[l].astype(BF16),
        ln_xa_g=ln_xa_g[l].reshape(1, -1), ln_xa_b=ln_xa_b[l].reshape(1, -1),
        rw_hi=rw_hi, rw_lo=rw_lo, rb=router_b.reshape(-1, 1),
        ln_ffn_g=ln_ffn_g[l].reshape(1, -1), ln_ffn_b=ln_ffn_b[l].reshape(1, -1),
    )


def kernel(x, mem, w_in, gla_w_a2, gla_b_a, gla_norm_g, nsa_cmp_pe, nsa_cmp_w1, nsa_cmp_w2, w_branch_gla, w_branch_nsa, w_out, ln_mix_g, ln_mix_b, xa_wq, xa_wkv, xa_wo, ln_xa_g, ln_xa_b, router_w, router_b, moe_w_in, moe_w_down, ln_ffn_g, ln_ffn_b):
    B, T, d = x.shape
    assert T % 512 == 0 and d == 2048 and mem.shape[1] == MEM_LEN
    n = B * T
    params = (w_in, gla_w_a2, gla_b_a, gla_norm_g, nsa_cmp_pe, nsa_cmp_w1, nsa_cmp_w2, w_branch_gla, w_branch_nsa,
              w_out, ln_mix_g, ln_mix_b, xa_wq, xa_wkv, xa_wo, ln_xa_g, ln_xa_b, router_w, router_b, moe_w_in,
              moe_w_down, ln_ffn_g, ln_ffn_b)
    slopes = (2.0 ** (-8.0 * jnp.arange(1, NSA_HEADS + 1, dtype=F32) / NSA_HEADS)).astype(F32)
    nc, ns = T // CMP_STRIDE, T // SEL_LEN
    cs = np.arange(nc) * CMP_STRIDE
    ss = np.arange(ns) * SEL_LEN
    ovt = ((cs[None, :] < ss[:, None] + SEL_LEN) & (cs[None, :] + CMP_LEN > ss[:, None])
           & (cs[None, :] + CMP_LEN <= T)).astype(np.float32)
    assert NSA_DH + ns <= LANES
    epad = np.zeros((T, LANES), np.float32)
    epad[np.arange(T), NSA_DH + np.arange(T) // SEL_LEN] = 1.0
    consts = (slopes, jnp.asarray(ovt, BF16), jnp.asarray(epad, BF16))

    xf = x.reshape(n, d)
    xb = xf.astype(BF16)
    mem_b = mem.reshape(B * MEM_LEN, d).astype(BF16)
    moe_w = (moe_w_in, moe_w_down)
    for l in range(DEPTH):
        p = _prep_layer(l, *params)
        xf, xb = _layer(xf, xb, mem_b, p, moe_w, l, consts, B, T)
    return xf.reshape(B, T, d)
```

```python
import functools

import jax
import jax.numpy as jnp
import numpy as np
from jax import lax
from jax.experimental import pallas as pl
from jax.experimental.pallas import tpu as pltpu

F32 = jnp.float32
BF16 = jnp.bfloat16
I32 = jnp.int32

DEPTH = 2
MEM_LEN = 256
GLA_HEADS = 4
GLA_DK = 128
GLA_DV = 256
GLA_GATE_RANK = 16
GLA_TAU = 16.0
GLA_CHUNK = 64
NSA_HEADS = 16
NSA_GROUPS = 4
NSA_HPG = NSA_HEADS // NSA_GROUPS
NSA_DH = 64
CMP_LEN = 32
CMP_STRIDE = 16
CMP_HIDDEN = 256
SEL_LEN = 64
SEL_TOPN = 8
WINDOW = 512
XA_HEADS = 4
XA_DH = 128
N_EXPERTS = 16
N_GROUPS = 4
EXPERTS_PER_GROUP = N_EXPERTS // N_GROUPS
TOP_K = 2
D_FF = 1536
DN_ALPHA = float((2 * DEPTH) ** 0.25)
LN_EPS = 1e-5
NEG = -1e30
LOG2E = 1.4426950408889634
FORCE_BONUS = 1e6

GLA_QK = GLA_HEADS * GLA_DK
GLA_V = GLA_HEADS * GLA_DV
NSA_Q = NSA_HEADS * NSA_DH
NSA_KV = NSA_GROUPS * NSA_DH

LANES = 128
VMEM_LIMIT = 56 * 1024 * 1024

COL_MG = 0
COL_GQ = 2 * 2048
COL_GK = COL_GQ + GLA_QK
COL_GV = COL_GK + GLA_QK
COL_GR = COL_GV + GLA_V
COL_KS = COL_GR + GLA_V
COL_KW = COL_KS + NSA_GROUPS * LANES
COL_END = COL_KW + NSA_GROUPS * LANES
SCOL_CK = LANES
SCOL_CV = SCOL_CK + NSA_KV
SCOL_END = SCOL_CV + NSA_KV
TROW_Q = 0
TROW_VS = NSA_Q
TROW_VW = TROW_VS + NSA_KV
TROW_END = TROW_VW + NSA_KV

MOE_BLOCK = 512
FF_TILE = 512


def _cp(sem):
    return pltpu.CompilerParams(dimension_semantics=sem, vmem_limit_bytes=VMEM_LIMIT)


def _dot(a, b):
    return jnp.dot(a, b, preferred_element_type=F32)


def _dot_nt(a, b):
    return lax.dot_general(a, b, (((1,), (1,)), ((), ())), preferred_element_type=F32)


def _dot_tn(a, b):
    return lax.dot_general(a, b, (((0,), (0,)), ((), ())), preferred_element_type=F32)


def _layer_norm(z, g, b):
    mu = jnp.mean(z, axis=-1, keepdims=True)
    zc = z - mu
    var = jnp.mean(zc * zc, axis=-1, keepdims=True)
    return zc * lax.rsqrt(var + LN_EPS) * g + b


def _mm_kernel(a_ref, b_ref, o_ref):
    o_ref[...] = _dot(a_ref[...], b_ref[...]).astype(o_ref.dtype)


def _matmul(a, b, out_dtype, tm, tn):
    m, k = a.shape
    n = b.shape[1]
    return pl.pallas_call(
        _mm_kernel,
        grid=(m // tm, n // tn),
        in_specs=[pl.BlockSpec((tm, k), lambda i, j: (i, 0)),
                  pl.BlockSpec((k, tn), lambda i, j: (0, j))],
        out_specs=pl.BlockSpec((tm, tn), lambda i, j: (i, j)),
        out_shape=jax.ShapeDtypeStruct((m, n), out_dtype),
        compiler_params=_cp(("parallel", "parallel")),
        name="matmul",
    )(a, b)


def _mm_nt_kernel(wt_ref, x_ref, o_ref):
    o_ref[0] = _dot_nt(wt_ref[...], x_ref[...]).astype(o_ref.dtype)


def _matmul_t(x, wt, B, T, tm, tr):
    n, k = x.shape
    r = wt.shape[0]
    nt = T // tm
    return pl.pallas_call(
        _mm_nt_kernel,
        grid=(n // tm, r // tr),
        in_specs=[pl.BlockSpec((tr, k), lambda i, j: (j, 0)),
                  pl.BlockSpec((tm, k), lambda i, j: (i, 0))],
        out_specs=pl.BlockSpec((1, tr, tm), lambda i, j: (i // nt, j, i % nt)),
        out_shape=jax.ShapeDtypeStruct((B, r, T), BF16),
        compiler_params=_cp(("parallel", "parallel")),
        name="matmul_t",
    )(wt, x)


def _gla_kernel(q_ref, k_ref, v_ref, r_ref, sm_ref, wa_ref, ba_ref, ng_ref, o_ref, st_ref):
    C = GLA_CHUNK
    n_chunks = q_ref.shape[0] // C
    st_ref[...] = jnp.zeros_like(st_ref)
    rowi = lax.broadcasted_iota(I32, (C, GLA_DK), 0)
    tt = lax.broadcasted_iota(I32, (C, C), 0)
    ss = lax.broadcasted_iota(I32, (C, C), 1)
    levels = (1, 2, 4, 8, 16, 32)
    pair_masks = [((tt // (2 * L)) == (ss // (2 * L))) & ((tt & L) != 0) & ((ss & L) == 0) for L in levels]
    diag_mask = tt == ss
    scale = GLA_DK ** -0.5

    def head_chunk(rows, h, z):
        qk_cols = slice(h * GLA_DK, (h + 1) * GLA_DK)
        v_cols = slice(h * GLA_DV, (h + 1) * GLA_DV)
        q = q_ref[rows, qk_cols].astype(F32) * scale
        k = k_ref[rows, qk_cols].astype(F32)
        v = v_ref[rows, v_cols]
        g = (jnp.minimum(z, 0.0) - jnp.log1p(jnp.exp(-jnp.abs(z)))) * (1.0 / GLA_TAU)
        incl = g
        tot = g
        att = jnp.where(diag_mask, _dot_nt(q.astype(BF16), k.astype(BF16)), 0.0)
        for L, pm in zip(levels, pair_masks):
            ql = (q * jnp.exp(incl)).astype(BF16)
            kl = (k * jnp.exp(tot - incl)).astype(BF16)
            att = jnp.where(pm, _dot_nt(ql, kl), att)
            upper = (rowi & L) != 0
            from_lower = pltpu.roll(tot, L, 0)
            from_upper = pltpu.roll(tot, C - L, 0)
            incl = incl + jnp.where(upper, from_lower, 0.0)
            tot = tot + jnp.where(upper, from_lower, from_upper)
        qd = (q * jnp.exp(incl)).astype(BF16)
        kd = (k * jnp.exp(tot - incl)).astype(BF16)
        st = st_ref[h]
        o = _dot_nt(qd, st.astype(BF16)) + _dot(att.astype(BF16), v)
        st_ref[h] = st * jnp.exp(tot[0:1, :]) + _dot_tn(v, kd)
        mu = jnp.mean(o, axis=-1, keepdims=True)
        oc = o - mu
        var = jnp.mean(oc * oc, axis=-1, keepdims=True)
        on = oc * lax.rsqrt(var + LN_EPS) * ng_ref[:, v_cols]
        r = r_ref[rows, v_cols].astype(F32)
        o_ref[rows, v_cols] = (on * (r * jax.nn.sigmoid(r))).astype(o_ref.dtype)

    def chunk(c, carry):
        rows = pl.ds(pl.multiple_of(c * C, C), C)
        z = _dot(sm_ref[rows, :].astype(BF16), wa_ref[...]) + ba_ref[...]
        for h in range(GLA_HEADS):
            head_chunk(rows, h, z[:, h * GLA_DK:(h + 1) * GLA_DK])
        return carry

    lax.fori_loop(0, n_chunks, chunk, 0)


def _gla(h_big, h_small, wa_pad, b_a, norm_g, B, T):
    n = B * T
    return pl.pallas_call(
        _gla_kernel,
        grid=(B,),
        in_specs=[
            pl.BlockSpec((T, GLA_QK), lambda b: (b, COL_GQ // GLA_QK)),
            pl.BlockSpec((T, GLA_QK), lambda b: (b, COL_GK // GLA_QK)),
            pl.BlockSpec((T, GLA_V), lambda b: (b, COL_GV // GLA_V)),
            pl.BlockSpec((T, GLA_V), lambda b: (b, COL_GR // GLA_V)),
            pl.BlockSpec((T, LANES), lambda b: (b, 0)),
            pl.BlockSpec((LANES, GLA_QK), lambda b: (0, 0)),
            pl.BlockSpec((1, GLA_QK), lambda b: (0, 0)),
            pl.BlockSpec((1, GLA_V), lambda b: (0, 0)),
        ],
        out_specs=pl.BlockSpec((T, GLA_V), lambda b: (b, 0)),
        out_shape=jax.ShapeDtypeStruct((n, GLA_V), BF16),
        scratch_shapes=[pltpu.VMEM((GLA_HEADS, GLA_DV, GLA_DK), F32)],
        compiler_params=_cp(("parallel",)),
        name="gla",
    )(h_big, h_big, h_big, h_big, h_small, wa_pad, b_a, norm_g)


def _compress_kernel(x_ref, w1_ref, w2_ref, w2t_ref, pe_ref, o_ref, ot_ref):
    nc = x_ref.shape[0] // CMP_STRIDE
    hid_w = w1_ref.shape[3]
    a = jnp.zeros((nc, hid_w), F32)
    bm = jnp.zeros((nc, hid_w), F32)
    c = jnp.zeros((pe_ref.shape[2], hid_w), F32)
    for l in range(CMP_STRIDE):
        xl = x_ref[pl.ds(l, nc, stride=CMP_STRIDE), :].astype(BF16)
        a = a + _dot(xl, w1_ref[0, l])
        bm = bm + _dot(xl, w1_ref[0, CMP_STRIDE + l])
    for l in range(CMP_LEN):
        c = c + _dot(pe_ref[0, l], w1_ref[0, l])
    hid = a + pltpu.roll(bm, nc - 1, 0) + c[0:1, :]
    act = jax.nn.gelu(hid).astype(BF16)
    o_ref[0, 0, 0] = _dot(act, w2_ref[0]).astype(o_ref.dtype)
    ot_ref[0, 0, 0] = _dot_nt(w2t_ref[0], act).astype(ot_ref.dtype)


def _compress(h_small, w1bd, w2bd, w2bdt, pe_pair, B, T):
    nc = T // CMP_STRIDE
    pairs = NSA_GROUPS // 2
    return pl.pallas_call(
        _compress_kernel,
        grid=(B, 2, pairs),
        in_specs=[
            pl.BlockSpec((T, LANES), lambda b, s, j: (b, SCOL_CK // LANES + s * pairs + j)),
            pl.BlockSpec((1,) + w1bd.shape[1:], lambda b, s, j: (s, 0, 0, 0)),
            pl.BlockSpec((1,) + w2bd.shape[1:], lambda b, s, j: (s, 0, 0)),
            pl.BlockSpec((1,) + w2bdt.shape[1:], lambda b, s, j: (s, 0, 0)),
            pl.BlockSpec((1,) + pe_pair.shape[1:], lambda b, s, j: (s, 0, 0, 0)),
        ],
        out_specs=[
            pl.BlockSpec((1, 1, 1, nc, LANES), lambda b, s, j: (b, s, j, 0, 0)),
            pl.BlockSpec((1, 1, 1, LANES, nc), lambda b, s, j: (b, s, j, 0, 0)),
        ],
        out_shape=[jax.ShapeDtypeStruct((B, 2, pairs, nc, LANES), BF16),
                   jax.ShapeDtypeStruct((B, 2, pairs, LANES, nc), BF16)],
        compiler_params=_cp(("parallel", "parallel", "parallel")),
        name="nsa_compress",
    )(h_small, w1bd, w2bd, w2bdt, pe_pair)


def _cmp_select_kernel(slopes_ref, qt_ref, kc_ref, vct_ref, ovt_ref, ocmp_ref, mb_ref, qpad_ref):
    g = pl.program_id(1)
    i = pl.program_id(2)
    tq = qt_ref.shape[2]
    nc = kc_ref.shape[3]
    ns = mb_ref.shape[2]
    dh = NSA_DH
    t0 = i * tq
    tpos = (t0 + lax.broadcasted_iota(I32, (nc, tq), 1))
    nidx = lax.broadcasted_iota(I32, (nc, tq), 0)
    mask_c = (nidx * CMP_STRIDE + (CMP_LEN - 1)) <= tpos
    absd = jnp.abs(tpos.astype(F32) - (nidx.astype(F32) * CMP_STRIDE + 0.5 * (CMP_LEN - 1)))
    lower = g % 2 == 0
    kc = kc_ref[0, 0, 0]
    vct = jnp.where(lower, vct_ref[0, 0, 0, 0:dh, :], vct_ref[0, 0, 0, dh:2 * dh, :])
    psum = jnp.zeros((nc, tq), F32)
    for hh in range(NSA_HPG):
        slope = slopes_ref[g * NSA_HPG + hh]
        q = qt_ref[0, hh * dh:(hh + 1) * dh, :] * jnp.asarray(dh ** -0.5, BF16)
        zero = jnp.zeros_like(q)
        qpad_ref[0:dh, :] = jnp.where(lower, q, zero)
        qpad_ref[dh:2 * dh, :] = jnp.where(lower, zero, q)
        s = _dot(kc, qpad_ref[...]) - slope * absd
        s = jnp.where(mask_c, s, NEG)
        e = jnp.exp(s - jnp.max(s, axis=0, keepdims=True))
        p = jnp.where(mask_c, e / jnp.sum(e, axis=0, keepdims=True), 0.0)
        ocmp_ref[0, hh * dh:(hh + 1) * dh, :] = _dot(vct, p.astype(BF16)).astype(ocmp_ref.dtype)
        psum = psum + p
    p_hi = psum.astype(BF16)
    p_lo = (psum - p_hi.astype(F32)).astype(BF16)
    imp = _dot(ovt_ref[...], p_hi) + _dot(ovt_ref[...], p_lo)
    j = lax.broadcasted_iota(I32, (ns, tq), 0)
    tp = t0 + lax.broadcasted_iota(I32, (ns, tq), 1)
    cur = tp // SEL_LEN
    forced = (j == 0) | (j == cur) | (j == cur - 1)
    valid = j * SEL_LEN <= tp
    score = jnp.where(valid, imp + jnp.where(forced, FORCE_BONUS, 0.0), NEG)
    rank = jnp.zeros((ns, tq), F32)
    for jp in range(ns):
        row = score[jp:jp + 1, :]
        beats = (row > score) | ((row == score) & (j > jp))
        rank = rank + jnp.where(beats, 1.0, 0.0)
    keep = valid & (rank < float(min(SEL_TOPN, ns)))
    mb_ref[0, 0] = jnp.where(keep, 0.0, NEG).astype(mb_ref.dtype)


def _cmp_select(slopes, h_t, kcmp, kcmp_t, ovt, B, T, tq):
    nc = T // CMP_STRIDE
    ns = T // SEL_LEN
    grp_rows = NSA_HPG * NSA_DH
    grid_spec = pltpu.PrefetchScalarGridSpec(
        num_scalar_prefetch=1,
        grid=(B, NSA_GROUPS, T // tq),
        in_specs=[
            pl.BlockSpec((1, grp_rows, tq), lambda b, g, i, s: (b, TROW_Q // grp_rows + g, i)),
            pl.BlockSpec((1, 1, 1, nc, LANES), lambda b, g, i, s: (b, 0, g // 2, 0, 0)),
            pl.BlockSpec((1, 1, 1, LANES, nc), lambda b, g, i, s: (b, 1, g // 2, 0, 0)),
            pl.BlockSpec((ns, nc), lambda b, g, i, s: (0, 0)),
        ],
        out_specs=[
            pl.BlockSpec((1, grp_rows, tq), lambda b, g, i, s: (b, g, i)),
            pl.BlockSpec((1, 1, ns, tq), lambda b, g, i, s: (b, g, 0, i)),
        ],
        scratch_shapes=[pltpu.VMEM((LANES, tq), BF16)],
    )
    return pl.pallas_call(
        _cmp_select_kernel,
        grid_spec=grid_spec,
        out_shape=[jax.ShapeDtypeStruct((B, NSA_Q, T), BF16),
                   jax.ShapeDtypeStruct((B, NSA_GROUPS, ns, T), BF16)],
        compiler_params=_cp(("parallel", "parallel", "parallel")),
        name="nsa_cmp_select",
    )(slopes, h_t, kcmp, kcmp_t, ovt)


def _sel_win_kernel(slopes_ref, qt_ref, ks_ref, kw_ref, vs_ref, vw_ref, epad_ref, mb_ref, ocmp_ref, gt_ref,
                    o_ref, qaug_ref, m_ref, acc_ref, srow_ref, bias_ref, s_ref, p_ref, alpha_ref,
                    kall_ref, vall_ref):
    g = pl.program_id(1)
    i = pl.program_id(2)
    tq = qt_ref.shape[2]
    tk = tq
    ns = mb_ref.shape[2]
    dh = NSA_DH
    wide = NSA_HPG * tq
    t0 = i * tq
    BIG = -NEG

    @pl.when(i == 0)
    def _():
        srow = jnp.concatenate([jnp.full((1, tq), slopes_ref[g * NSA_HPG + hh] * LOG2E, F32)
                                for hh in range(NSA_HPG)], axis=1)
        srow_ref[...] = srow
        lane = lax.broadcasted_iota(I32, (tk, wide), 1) & (tq - 1)
        dist0 = (lane - lax.broadcasted_iota(I32, (tk, wide), 0)).astype(F32)
        sd0 = srow * dist0
        bias_ref[0] = sd0
        bias_ref[1] = sd0 + jnp.where(dist0 >= 0.0, 0.0, BIG)
        bias_ref[2] = sd0 + jnp.where(dist0 < 0.0, 0.0, BIG)
        bias_ref[3] = jnp.full((tk, wide), BIG, F32)
        kall_ref[0] = ks_ref[...] + epad_ref[...]
        kall_ref[1] = kw_ref[...]
        extra = jnp.where(lax.broadcasted_iota(I32, (vall_ref.shape[1] - dh, vall_ref.shape[2]), 0) == 0, 1.0, 0.0)
        vall_ref[0, 0:dh, :] = vs_ref[0]
        vall_ref[1, 0:dh, :] = vw_ref[0]
        vall_ref[0, dh:, :] = extra.astype(BF16)
        vall_ref[1, dh:, :] = extra.astype(BF16)

    for hh in range(NSA_HPG):
        cols = slice(hh * tq, (hh + 1) * tq)
        q = qt_ref[0, hh * dh:(hh + 1) * dh, :].astype(F32) * (dh ** -0.5 * LOG2E)
        qaug_ref[0:dh, cols] = q.astype(BF16)
        qaug_ref[dh:dh + ns, cols] = mb_ref[0, 0]
        qaug_ref[dh + ns:, cols] = jnp.zeros((qaug_ref.shape[0] - dh - ns, tq), BF16)

    m_ref[...] = jnp.full(m_ref.shape, NEG, F32)
    acc_ref[...] = jnp.zeros(acc_ref.shape, F32)

    n_back = WINDOW // tk
    n_sel = i + 1
    n_win = jnp.minimum(i, n_back) + 1
    n_steps = n_sel + n_win

    def describe(n):
        n = jnp.maximum(n, 0)
        is_win = n >= n_sel
        kb = jnp.clip(jnp.where(is_win, i - n_win + 1 + (n - n_sel), n), 0, i)
        mode = jnp.where(kb == i, 1, jnp.where(is_win & (kb == i - n_back), 2, 0))
        mode = jnp.where(n >= n_steps, 3, mode)
        return is_win.astype(I32), kb, mode

    def scores(n, slot):
        br, kb, _ = describe(n)
        s0 = pl.multiple_of(kb * tk, tk)
        s_ref[slot] = _dot(kall_ref[br, pl.ds(s0, tk), :], qaug_ref[...])

    def softmax(n, slot):
        br, kb, mode = describe(n)
        crow = srow_ref[...] * ((i - kb) * tk).astype(F32)
        s = s_ref[slot] - bias_ref[mode]
        m_old = m_ref[br]
        m_new = jnp.maximum(m_old, jnp.max(s, axis=0, keepdims=True) - crow)
        alpha = jnp.exp2(m_old - m_new)
        p = jnp.exp2(s - (m_new + crow))
        m_ref[br] = m_new
        alpha_ref[slot] = alpha
        p_ref[slot] = p.astype(BF16)

    def weighted_values(n, slot):
        br, kb, _ = describe(n)
        s0 = pl.multiple_of(kb * tk, tk)
        acc_ref[br] = alpha_ref[slot] * acc_ref[br] + _dot(vall_ref[br, :, pl.ds(s0, tk)], p_ref[slot])

    p_ref[1] = jnp.zeros(p_ref.shape[1:], BF16)
    alpha_ref[1] = jnp.ones(alpha_ref.shape[1:], F32)
    scores(0, 0)

    def pair(j, carry):
        n = 2 * j
        scores(n + 1, 1)
        softmax(n, 0)
        weighted_values(n - 1, 1)
        scores(n + 2, 0)
        softmax(n + 1, 1)
        weighted_values(n, 0)
        return carry

    n_pairs = (n_steps + 1) // 2
    lax.fori_loop(0, n_pairs, pair, 0)
    weighted_values(2 * n_pairs - 1, 1)

    def gate_row(branch):
        rows = [gt_ref[0, 0, 3 * hh + branch:3 * hh + branch + 1, :] for hh in range(NSA_HPG)]
        return jax.nn.sigmoid(jnp.concatenate(rows, axis=1))

    o = (gate_row(1) * (acc_ref[0, 0:dh, :] / acc_ref[0, dh:dh + 1, :])
         + gate_row(2) * (acc_ref[1, 0:dh, :] / acc_ref[1, dh:dh + 1, :]))
    ocmp = jnp.concatenate([ocmp_ref[0, hh * dh:(hh + 1) * dh, :] for hh in range(NSA_HPG)], axis=1).astype(F32)
    o = o + gate_row(0) * ocmp
    o_heads = jnp.concatenate([o[:, hh * tq:(hh + 1) * tq] for hh in range(NSA_HPG)], axis=0)
    o_ref[...] = o_heads.T.astype(o_ref.dtype)


def _sel_win(slopes, h_t, h_big, epad, mb, ocmp_t, gates_t, B, T, tq):
    ns = T // SEL_LEN
    kaug = LANES
    G = NSA_GROUPS
    grp_rows = NSA_HPG * NSA_DH
    nq = T // tq
    grid_spec = pltpu.PrefetchScalarGridSpec(
        num_scalar_prefetch=1,
        grid=(B, G, nq),
        in_specs=[
            pl.BlockSpec((1, grp_rows, tq), lambda b, g, i, s: (b, TROW_Q // grp_rows + g, i)),
            pl.BlockSpec((T, LANES), lambda b, g, i, s: (b, COL_KS // LANES + g)),
            pl.BlockSpec((T, LANES), lambda b, g, i, s: (b, COL_KW // LANES + g)),
            pl.BlockSpec((1, NSA_DH, T), lambda b, g, i, s: (b, TROW_VS // NSA_DH + g, 0)),
            pl.BlockSpec((1, NSA_DH, T), lambda b, g, i, s: (b, TROW_VW // NSA_DH + g, 0)),
            pl.BlockSpec((T, LANES), lambda b, g, i, s: (0, 0)),
            pl.BlockSpec((1, 1, ns, tq), lambda b, g, i, s: (b, g, 0, i)),
            pl.BlockSpec((1, grp_rows, tq), lambda b, g, i, s: (b, g, i)),
            pl.BlockSpec((1, 1, 16, tq), lambda b, g, i, s: (b, g, 0, i)),
        ],
        out_specs=pl.BlockSpec((tq, grp_rows), lambda b, g, i, s: (b * nq + i, g)),
        scratch_shapes=[
            pltpu.VMEM((kaug, NSA_HPG * tq), BF16),
            pltpu.VMEM((2, 1, NSA_HPG * tq), F32),
            pltpu.VMEM((2, NSA_DH + 16, NSA_HPG * tq), F32),
            pltpu.VMEM((1, NSA_HPG * tq), F32),
            pltpu.VMEM((4, tq, NSA_HPG * tq), F32),
            pltpu.VMEM((2, tq, NSA_HPG * tq), F32),
            pltpu.VMEM((2, tq, NSA_HPG * tq), BF16),
            pltpu.VMEM((2, 1, NSA_HPG * tq), F32),
            pltpu.VMEM((2, T, kaug), BF16),
            pltpu.VMEM((2, NSA_DH + 16, T), BF16),
        ],
    )
    return pl.pallas_call(
        _sel_win_kernel,
        grid_spec=grid_spec,
        out_shape=jax.ShapeDtypeStruct((B * T, NSA_Q), BF16),
        compiler_params=_cp(("parallel", "parallel", "arbitrary")),
        name="nsa_sel_win",
    )(slopes, h_t, h_big, h_big, h_t, h_t, epad, mb, ocmp_t, gates_t)


def _mix_kernel(og_ref, on_ref, mg1_ref, mg2_ref, x_ref, wg_ref, wn_ref, wo_ref, lg_ref, lb_ref,
                x1_ref, x1b_ref):
    g1 = _dot(og_ref[...], wg_ref[...])
    g2 = _dot(on_ref[...], wn_ref[...])
    merged = (jax.nn.sigmoid(mg1_ref[...].astype(F32)) * g1
              + jax.nn.sigmoid(mg2_ref[...].astype(F32)) * g2)
    y = _dot(merged.astype(BF16), wo_ref[...])
    x1 = _layer_norm(DN_ALPHA * x_ref[...] + y, lg_ref[...], lb_ref[...])
    x1_ref[...] = x1
    x1b_ref[...] = x1.astype(BF16)


def _const_spec(shape):
    nd = len(shape)
    return pl.BlockSpec(shape, lambda *_: (0,) * nd, pipeline_mode=pl.Buffered(1))


def _mix(o_gla, o_nsa, h_big, x, wg, wn, wo, lg, lb, tm):
    n, d = x.shape
    return pl.pallas_call(
        _mix_kernel,
        grid=(n // tm,),
        in_specs=[
            pl.BlockSpec((tm, GLA_V), lambda i: (i, 0)),
            pl.BlockSpec((tm, NSA_Q), lambda i: (i, 0)),
            pl.BlockSpec((tm, d), lambda i: (i, 0)),
            pl.BlockSpec((tm, d), lambda i: (i, 1)),
            pl.BlockSpec((tm, d), lambda i: (i, 0)),
            _const_spec(wg.shape), _const_spec(wn.shape), _const_spec(wo.shape),
            _const_spec(lg.shape), _const_spec(lb.shape),
        ],
        out_specs=[pl.BlockSpec((tm, d), lambda i: (i, 0)), pl.BlockSpec((tm, d), lambda i: (i, 0))],
        out_shape=[jax.ShapeDtypeStruct((n, d), F32), jax.ShapeDtypeStruct((n, d), BF16)],
        compiler_params=_cp(("parallel",)),
        name="mix_ln",
    )(o_gla, o_nsa, h_big, h_big, x, wg, wn, wo, lg, lb)


def _xattn_kernel(x_ref, xb_ref, kv_ref, wq_ref, wo_ref, lg_ref, lb_ref, x2_ref):
    hd = XA_HEADS * XA_DH
    q = (_dot(xb_ref[...], wq_ref[...]) * (XA_DH ** -0.5)).astype(BF16)
    outs = []
    for h in range(XA_HEADS):
        kh = kv_ref[0, :, h * XA_DH:(h + 1) * XA_DH]
        vh = kv_ref[0, :, hd + h * XA_DH:hd + (h + 1) * XA_DH]
        s = _dot_nt(q[:, h * XA_DH:(h + 1) * XA_DH], kh)
        e = jnp.exp(s - jnp.max(s, axis=-1, keepdims=True))
        p = e / jnp.sum(e, axis=-1, keepdims=True)
        outs.append(_dot(p.astype(BF16), vh).astype(BF16))
    o = jnp.concatenate(outs, axis=-1)
    y = _dot(o, wo_ref[...])
    x2_ref[...] = _layer_norm(DN_ALPHA * x_ref[...] + y, lg_ref[...], lb_ref[...])


def _xattn(x1, x1b, kv, wq, wo, lg, lb, B, T, tm):
    n, d = x1.shape
    nt = T // tm
    return pl.pallas_call(
        _xattn_kernel,
        grid=(B, nt),
        in_specs=[
            pl.BlockSpec((tm, d), lambda b, i: (b * nt + i, 0)),
            pl.BlockSpec((tm, d), lambda b, i: (b * nt + i, 0)),
            pl.BlockSpec((1,) + kv.shape[1:], lambda b, i: (b, 0, 0)),
            _const_spec(wq.shape), _const_spec(wo.shape), _const_spec(lg.shape), _const_spec(lb.shape),
        ],
        out_specs=pl.BlockSpec((tm, d), lambda b, i: (b * nt + i, 0)),
        out_shape=jax.ShapeDtypeStruct((n, d), F32),
        compiler_params=_cp(("parallel", "parallel")),
        name="xattn_ln",
    )(x1, x1b, kv, wq, wo, lg, lb)


def _router_kernel(x_ref, wh_ref, wl_ref, rb_ref, e_ref, gate_ref, rank_ref, cnt_ref, carry_ref):
    i = pl.program_id(0)
    tr = x_ref.shape[0]
    E = N_EXPERTS

    @pl.when(i == 0)
    def _():
        carry_ref[...] = jnp.zeros_like(carry_ref)

    x = x_ref[...]
    x_hi = x.astype(BF16)
    x_lo = (x - x_hi.astype(F32)).astype(BF16)
    wh = wh_ref[...]
    logits = _dot_nt(wh, x_hi) + _dot_nt(wh, x_lo) + _dot_nt(wl_ref[...], x_hi)
    biased = logits + rb_ref[...]
    rows = [biased[e:e + 1, :] for e in range(E)]
    raw = [logits[e:e + 1, :] for e in range(E)]
    best_score = None
    best = None
    for gi in range(N_GROUPS):
        v = rows[gi * EXPERTS_PER_GROUP:(gi + 1) * EXPERTS_PER_GROUP]
        sc = None
        for a in range(EXPERTS_PER_GROUP):
            for b in range(a + 1, EXPERTS_PER_GROUP):
                pair = v[a] + v[b]
                sc = pair if sc is None else jnp.maximum(sc, pair)
        if best is None:
            best_score, best = sc, jnp.zeros((1, tr), I32)
        else:
            better = sc > best_score
            best_score = jnp.where(better, sc, best_score)
            best = jnp.where(better, gi, best)

    def pick(vals):
        out = vals[0:EXPERTS_PER_GROUP]
        for gi in range(1, N_GROUPS):
            out = [jnp.where(best == gi, vals[gi * EXPERTS_PER_GROUP + a], out[a]) for a in range(EXPERTS_PER_GROUP)]
        return out

    w = pick(rows)
    lraw = pick(raw)
    i1 = jnp.zeros((1, tr), I32)
    v1 = w[0]
    l1 = lraw[0]
    for a in range(1, EXPERTS_PER_GROUP):
        better = w[a] > v1
        v1 = jnp.where(better, w[a], v1)
        l1 = jnp.where(better, lraw[a], l1)
        i1 = jnp.where(better, a, i1)
    i2 = jnp.full((1, tr), -1, I32)
    v2 = jnp.full((1, tr), -jnp.inf, F32)
    l2 = jnp.zeros((1, tr), F32)
    for a in range(EXPERTS_PER_GROUP):
        better = (i1 != a) & ((w[a] > v2) | (i2 < 0))
        v2 = jnp.where(better, w[a], v2)
        l2 = jnp.where(better, lraw[a], l2)
        i2 = jnp.where(better, a, i2)
    e1 = best * EXPERTS_PER_GROUP + i1
    e2 = best * EXPERTS_PER_GROUP + i2
    mx = jnp.maximum(l1, l2)
    p1 = jnp.exp(l1 - mx)
    p2 = jnp.exp(l2 - mx)
    den = p1 + p2
    e_ref[0:1, :] = e1
    e_ref[1:2, :] = e2
    gate_ref[0:1, :] = p1 / den
    gate_ref[1:2, :] = p2 / den
    eidx = lax.broadcasted_iota(I32, (E, tr), 0)
    is1 = eidx == e1
    is2 = eidx == e2
    member = jnp.where(is1 | is2, 1.0, 0.0)
    uu = lax.broadcasted_iota(I32, (tr, tr), 0)
    tt = lax.broadcasted_iota(I32, (tr, tr), 1)
    tri = jnp.where(uu <= tt, 1.0, 0.0).astype(BF16)
    incl = _dot(member.astype(BF16), tri)
    excl = carry_ref[:, 0:1] + incl - member
    rank_ref[0:1, :] = jnp.sum(jnp.where(is1, excl, 0.0), axis=0, keepdims=True).astype(I32)
    rank_ref[1:2, :] = jnp.sum(jnp.where(is2, excl, 0.0), axis=0, keepdims=True).astype(I32)
    new_carry = carry_ref[...] + jnp.sum(member, axis=1, keepdims=True)
    carry_ref[...] = new_carry
    cnt_ref[...] = new_carry


def _router(x2, rw_hi, rw_lo, rb, tr):
    n, d = x2.shape
    E = N_EXPERTS
    return pl.pallas_call(
        _router_kernel,
        grid=(n // tr,),
        in_specs=[
            pl.BlockSpec((tr, d), lambda i: (i, 0)),
            pl.BlockSpec((E, d), lambda i: (0, 0)),
            pl.BlockSpec((E, d), lambda i: (0, 0)),
            pl.BlockSpec((E, 1), lambda i: (0, 0)),
        ],
        out_specs=[
            pl.BlockSpec((2, tr), lambda i: (0, i)),
            pl.BlockSpec((2, tr), lambda i: (0, i)),
            pl.BlockSpec((2, tr), lambda i: (0, i)),
            pl.BlockSpec((E, LANES), lambda i: (0, 0)),
        ],
        out_shape=[jax.ShapeDtypeStruct((2, n), I32), jax.ShapeDtypeStruct((2, n), F32),
                   jax.ShapeDtypeStruct((2, n), I32), jax.ShapeDtypeStruct((E, LANES), F32)],
        scratch_shapes=[pltpu.VMEM((E, LANES), F32)],
        compiler_params=_cp(("arbitrary",)),
        name="moe_router",
    )(x2, rw_hi, rw_lo, rb)


def _slot_kernel(ps_ref, e_ref, rank_ref, slot_ref):
    e = e_ref[...]
    start = jnp.zeros(e.shape, I32)
    for ex in range(N_EXPERTS):
        start = jnp.where(e == ex, ps_ref[ex], start)
    slot_ref[...] = start + rank_ref[...]


def _slots(pad_start, e, rank, ts):
    n = e.shape[1]
    grid_spec = pltpu.PrefetchScalarGridSpec(
        num_scalar_prefetch=1,
        grid=(n // ts,),
        in_specs=[pl.BlockSpec((TOP_K, ts), lambda i, s: (0, i)), pl.BlockSpec((TOP_K, ts), lambda i, s: (0, i))],
        out_specs=pl.BlockSpec((TOP_K, ts), lambda i, s: (0, i)),
    )
    return pl.pallas_call(
        _slot_kernel,
        grid_spec=grid_spec,
        out_shape=jax.ShapeDtypeStruct((TOP_K, n), I32),
        compiler_params=_cp(("parallel",)),
        name="moe_slots",
    )(pad_start, e, rank)


def _dispatch_kernel(pe_ref, s0_ref, s1_ref, x_ref, buf_hbm, zero_ref, sem):
    td = s0_ref.shape[0]
    slots = (s0_ref, s1_ref)

    @pl.when(pl.program_id(0) == 0)
    def _():
        zero_ref[...] = jnp.zeros_like(zero_ref)

        def zero_copy(ex):
            last = pl.multiple_of(jnp.maximum(pe_ref[ex] - MOE_BLOCK, 0), MOE_BLOCK)
            return pltpu.make_async_copy(zero_ref, buf_hbm.at[pl.ds(last, MOE_BLOCK), :], sem)

        def nonempty(ex):
            return pe_ref[ex] > (pe_ref[ex - 1] if ex > 0 else 0)

        n_blocks = buf_hbm.shape[0] // MOE_BLOCK
        first_unused = pe_ref[N_EXPERTS - 1] // MOE_BLOCK

        def tail_copy(k):
            row = pl.multiple_of((first_unused + k) * MOE_BLOCK, MOE_BLOCK)
            return pltpu.make_async_copy(zero_ref, buf_hbm.at[pl.ds(row, MOE_BLOCK), :], sem)

        for ex in range(N_EXPERTS):
            pl.when(nonempty(ex))(lambda ex=ex: zero_copy(ex).start())
            pl.when(first_unused + ex < n_blocks)(lambda ex=ex: tail_copy(ex).start())
        for ex in range(N_EXPERTS):
            pl.when(nonempty(ex))(lambda ex=ex: zero_copy(ex).wait())
            pl.when(first_unused + ex < n_blocks)(lambda ex=ex: tail_copy(ex).wait())

    def issue(t, carry):
        for kk in range(TOP_K):
            dest = slots[kk][t]
            pltpu.make_async_copy(x_ref.at[pl.ds(t, 1), :], buf_hbm.at[pl.ds(dest, 1), :], sem).start()
        return carry

    lax.fori_loop(0, td, issue, 0, unroll=8)
    for kk in range(TOP_K):
        pltpu.make_async_copy(x_ref, buf_hbm.at[pl.ds(0, td), :], sem).wait()


def _dispatch(pad_end, slot0, slot1, x2, n_rows, td):
    n, d = x2.shape
    grid_spec = pltpu.PrefetchScalarGridSpec(
        num_scalar_prefetch=1,
        grid=(n // td,),
        in_specs=[
            pl.BlockSpec((td,), lambda i, s: (i,), memory_space=pltpu.SMEM),
            pl.BlockSpec((td,), lambda i, s: (i,), memory_space=pltpu.SMEM),
            pl.BlockSpec((td, d), lambda i, s: (i, 0)),
        ],
        out_specs=pl.BlockSpec(memory_space=pl.ANY),
        scratch_shapes=[pltpu.VMEM((MOE_BLOCK, d), F32), pltpu.SemaphoreType.DMA(())],
    )
    return pl.pallas_call(
        _dispatch_kernel,
        grid_spec=grid_spec,
        out_shape=jax.ShapeDtypeStruct((n_rows, d), F32),
        compiler_params=_cp(("arbitrary",)),
        name="moe_dispatch",
    )(pad_end, slot0, slot1, x2)


def _expert_kernel(be_ref, nb_ref, x_ref, win_hbm, wdn_hbm, y_ref, xb_ref, wa_s, wu_s, wd_s, sa, su, sd, sems,
                   *, layer):
    b = pl.program_id(0)
    nf = D_FF // FF_TILE
    n_used = nb_ref[0]
    e = be_ref[b]
    e_prev = be_ref[jnp.maximum(b - 1, 0)]
    e_next = be_ref[jnp.minimum(b + 1, pl.num_programs(0) - 1)]
    active = b < n_used
    is_first = active & ((b == 0) | (e_prev != e))
    feeds_next = active & (b + 1 < n_used) & (e_next != e)

    def tile_copies(ex, f):
        lo = f * FF_TILE
        return (pltpu.make_async_copy(win_hbm.at[layer, ex, :, pl.ds(lo, FF_TILE)], sa, sems.at[0]),
                pltpu.make_async_copy(win_hbm.at[layer, ex, :, pl.ds(D_FF + lo, FF_TILE)], su, sems.at[1]),
                pltpu.make_async_copy(wdn_hbm.at[layer, ex, pl.ds(lo, FF_TILE), :], sd, sems.at[2]))

    def start(ex, f):
        for c in tile_copies(ex, f):
            c.start()

    def finish(ex, f):
        for c in tile_copies(ex, f):
            c.wait()
        wa_s[f] = sa[...].astype(BF16)
        wu_s[f] = su[...].astype(BF16)
        wd_s[f] = sd[...].astype(BF16)

    @pl.when(b == 0)
    def _():
        for f in range(nf - 1):
            start(e, f)
            finish(e, f)
        start(e, nf - 1)

    @pl.when(jnp.logical_not(active))
    def _():
        y_ref[...] = jnp.zeros_like(y_ref)

    @pl.when(active)
    def _():
        xb_ref[...] = x_ref[...].astype(BF16)
        for f in range(nf):
            xb = xb_ref[...]
            a = _dot(xb, wa_s[f])
            u = _dot(xb, wu_s[f])
            act = (a * jax.nn.sigmoid(a) * u).astype(BF16)
            y = _dot(act, wd_s[f])
            if f == 0:
                y_ref[...] = y
                pl.when(is_first)(lambda: finish(e, nf - 1))
            else:
                y_ref[...] += y

            @pl.when(feeds_next)
            def _(f=f):
                if f >= 1:
                    finish(e_next, f - 1)
                start(e_next, f)


def _experts(blk_expert, n_used, buf, w_in, w_down, layer):
    p, d = buf.shape
    nb = p // MOE_BLOCK
    nf = D_FF // FF_TILE
    grid_spec = pltpu.PrefetchScalarGridSpec(
        num_scalar_prefetch=2,
        grid=(nb,),
        in_specs=[
            pl.BlockSpec((MOE_BLOCK, d), lambda b, be, nu: (jnp.minimum(b, nu[0] - 1), 0)),
            pl.BlockSpec(memory_space=pl.ANY),
            pl.BlockSpec(memory_space=pl.ANY),
        ],
        out_specs=pl.BlockSpec((MOE_BLOCK, d), lambda b, be, nu: (b, 0)),
        scratch_shapes=[
            pltpu.VMEM((MOE_BLOCK, d), BF16),
            pltpu.VMEM((nf, d, FF_TILE), BF16),
            pltpu.VMEM((nf, d, FF_TILE), BF16),
            pltpu.VMEM((nf, FF_TILE, d), BF16),
            pltpu.VMEM((d, FF_TILE), F32),
            pltpu.VMEM((d, FF_TILE), F32),
            pltpu.VMEM((FF_TILE, d), F32),
            pltpu.SemaphoreType.DMA((3,)),
        ],
    )
    return pl.pallas_call(
        functools.partial(_expert_kernel, layer=layer),
        grid_spec=grid_spec,
        out_shape=jax.ShapeDtypeStruct((p, d), F32),
        compiler_params=_cp(("arbitrary",)),
        name="moe_experts",
    )(blk_expert, n_used, buf, w_in, w_down)


def _combine_kernel(s0_ref, s1_ref, y_hbm, x_ref, gate_ref, lg_ref, lb_ref, x3_ref, x3b_ref,
                    y0_ref, y1_ref, sem):
    tc = x_ref.shape[0]
    bufs = (y0_ref, y1_ref)
    slots = (s0_ref, s1_ref)

    def issue(t, carry):
        for kk in range(TOP_K):
            src = slots[kk][t]
            pltpu.make_async_copy(y_hbm.at[pl.ds(src, 1), :], bufs[kk].at[pl.ds(t, 1), :], sem).start()
        return carry

    lax.fori_loop(0, tc, issue, 0, unroll=8)
    for kk in range(TOP_K):
        pltpu.make_async_copy(y_hbm.at[pl.ds(0, tc), :], bufs[kk], sem).wait()
    gate = gate_ref[...]
    z = DN_ALPHA * x_ref[...] + gate[:, 0:1] * y0_ref[...] + gate[:, 1:2] * y1_ref[...]
    x3 = _layer_norm(z, lg_ref[...], lb_ref[...])
    x3_ref[...] = x3
    x3b_ref[...] = x3.astype(BF16)


def _combine(slot0, slot1, y, x2, gate_nt, lg, lb, tc):
    n, d = x2.shape
    return pl.pallas_call(
        _combine_kernel,
        grid=(n // tc,),
        in_specs=[
            pl.BlockSpec((tc,), lambda i: (i,), memory_space=pltpu.SMEM),
            pl.BlockSpec((tc,), lambda i: (i,), memory_space=pltpu.SMEM),
            pl.BlockSpec(memory_space=pl.ANY),
            pl.BlockSpec((tc, d), lambda i: (i, 0)),
            pl.BlockSpec((tc, 2), lambda i: (i, 0)),
            pl.BlockSpec((1, d), lambda i: (0, 0)),
            pl.BlockSpec((1, d), lambda i: (0, 0)),
        ],
        out_specs=[pl.BlockSpec((tc, d), lambda i: (i, 0)), pl.BlockSpec((tc, d), lambda i: (i, 0))],
        out_shape=[jax.ShapeDtypeStruct((n, d), F32), jax.ShapeDtypeStruct((n, d), BF16)],
        scratch_shapes=[pltpu.VMEM((tc, d), F32), pltpu.VMEM((tc, d), F32), pltpu.SemaphoreType.DMA(())],
        compiler_params=_cp(("arbitrary",)),
        name="moe_combine_ln",
    )(slot0, slot1, y, x2, gate_nt, lg, lb)


def _layer(x, xb, mem_b, p, moe_w, layer, consts, B, T):
    n, d = x.shape
    G, HPG, DH = NSA_GROUPS, NSA_HPG, NSA_DH
    slopes, ovt, epad = consts

    h_big = _matmul(xb, p["w_big"], BF16, 1024, 512)
    h_small = _matmul(xb, p["w_small"], F32, 1024, SCOL_END)
    h_t = _matmul_t(xb, p["w_t"], B, T, 1024, 512)

    o_gla = _gla(h_big, h_small, p["wa_pad"], p["b_a"], p["norm_g"], B, T)

    kcmp, kcmp_t = _compress(h_small, p["cmp_w1bd"], p["cmp_w2bd"], p["cmp_w2bdt"], p["cmp_pe_pair"], B, T)
    ocmp_t, mb = _cmp_select(slopes, h_t, kcmp, kcmp_t, ovt, B, T, 512)
    gates_t = h_small[:, GLA_GATE_RANK:GLA_GATE_RANK + 3 * NSA_HEADS].reshape(B, T, G, 3 * HPG)
    gates_t = jnp.pad(gates_t.transpose(0, 2, 3, 1), ((0, 0), (0, 0), (0, 16 - 3 * HPG), (0, 0)))
    o_nsa = _sel_win(slopes, h_t, h_big, epad, mb, ocmp_t, gates_t, B, T, 256)

    x1, x1b = _mix(o_gla, o_nsa, h_big, x, p["w_bg"], p["w_bn"], p["w_out"], p["ln_mix_g"], p["ln_mix_b"], 256)

    kvm = _matmul(mem_b, p["xa_wkv"], BF16, 512, 512).reshape(B, MEM_LEN, 2 * XA_HEADS * XA_DH)
    x2 = _xattn(x1, x1b, kvm, p["xa_wq"], p["xa_wo"], p["ln_xa_g"], p["ln_xa_b"], B, T, 256)

    e, gate, rank, cnt = _router(x2, p["rw_hi"], p["rw_lo"], p["rb"], 512)
    counts = cnt[:, 0].astype(I32)
    padded = (counts + MOE_BLOCK - 1) // MOE_BLOCK * MOE_BLOCK
    pad_end = jnp.cumsum(padded)
    pad_start = (pad_end - padded).astype(I32)
    nb = (n * TOP_K) // MOE_BLOCK + N_EXPERTS
    n_used = (pad_end[-1] // MOE_BLOCK).astype(I32).reshape(1)
    blk_start = jnp.arange(nb, dtype=I32) * MOE_BLOCK
    blk_expert = jnp.minimum(jnp.sum(blk_start[:, None] >= pad_end[None, :], axis=1), N_EXPERTS - 1).astype(I32)
    blk_expert = jnp.where(jnp.arange(nb) < n_used[0], blk_expert, blk_expert[jnp.maximum(n_used[0] - 1, 0)])
    slot = _slots(pad_start, e, rank, 2048)
    buf = _dispatch(pad_end.astype(I32), slot[0], slot[1], x2, nb * MOE_BLOCK, 512)
    y = _experts(blk_expert, n_used, buf, moe_w[0], moe_w[1], layer)
    x3, x3b = _combine(slot[0], slot[1], y, x2, gate.T, p["ln_ffn_g"], p["ln_ffn_b"], 256)
    return x3, x3b


def _prep_layer(l, w_in, gla_w_a2, gla_b_a, gla_norm_g, nsa_cmp_pe, nsa_cmp_w1, nsa_cmp_w2, w_branch_gla,
                w_branch_nsa, w_out, ln_mix_g, ln_mix_b, xa_wq, xa_wkv, xa_wo, ln_xa_g, ln_xa_b, router_w,
                router_b, moe_w_in, moe_w_down, ln_ffn_g, ln_ffn_b):
    d = w_in.shape[1]
    w = w_in[l]
    o_gq, o_gk, o_gv, o_gr = 0, GLA_QK, 2 * GLA_QK, 2 * GLA_QK + GLA_V
    o_ga = o_gr + GLA_V
    o_nq = o_ga + GLA_GATE_RANK
    o_nkv = o_nq + NSA_Q
    o_ng = o_nkv + 6 * NSA_KV
    o_mg = o_ng + 3 * NSA_HEADS
    G, DH = NSA_GROUPS, NSA_DH

    def kv_cols(kind):
        return w[:, o_nkv + kind * NSA_KV:o_nkv + (kind + 1) * NSA_KV]

    def slabs(wk):
        return jnp.pad(wk.reshape(d, G, DH), ((0, 0), (0, 0), (0, LANES - DH))).reshape(d, G * LANES)

    w_big = jnp.concatenate([w[:, o_mg:o_mg + 2 * d], w[:, o_gq:o_ga], slabs(kv_cols(2)), slabs(kv_cols(4))],
                            axis=1).astype(BF16)
    w_small = jnp.concatenate([w[:, o_ga:o_nq], w[:, o_ng:o_mg],
                               jnp.zeros((d, LANES - GLA_GATE_RANK - 3 * NSA_HEADS), F32),
                               kv_cols(0), kv_cols(1)], axis=1).astype(BF16)
    w_t = jnp.concatenate([w[:, o_nq:o_nkv], kv_cols(3), kv_cols(5)], axis=1).T.astype(BF16)
    w1 = nsa_cmp_w1[l].reshape(2, CMP_LEN, DH, CMP_HIDDEN)
    z1 = jnp.zeros_like(w1)
    w1bd = jnp.concatenate([jnp.concatenate([w1, z1], axis=3), jnp.concatenate([z1, w1], axis=3)], axis=2)
    w2 = nsa_cmp_w2[l]
    z2 = jnp.zeros_like(w2)
    w2bd = jnp.concatenate([jnp.concatenate([w2, z2], axis=2), jnp.concatenate([z2, w2], axis=2)], axis=1)
    pe = nsa_cmp_pe[l]
    pe_pair = jnp.broadcast_to(jnp.concatenate([pe, pe], axis=-1)[:, :, None, :], (2, CMP_LEN, 16, 2 * DH))
    wa_pad = jnp.concatenate([gla_w_a2[l], jnp.zeros((LANES - GLA_GATE_RANK, GLA_QK), F32)], axis=0).astype(BF16)
    rw_t = router_w.T
    rw_hi = rw_t.astype(BF16)
    rw_lo = (rw_t - rw_hi.astype(F32)).astype(BF16)
    return dict(
        w_big=w_big, w_small=w_small, w_t=w_t, wa_pad=wa_pad,
        b_a=gla_b_a[l].reshape(1, -1), norm_g=gla_norm_g[l].reshape(1, -1),
        cmp_w1bd=w1bd.astype(BF16), cmp_w2bd=w2bd.astype(BF16), cmp_w2bdt=w2bd.transpose(0, 2, 1).astype(BF16),
        cmp_pe_pair=pe_pair.astype(BF16),
        w_bg=w_branch_gla[l].astype(BF16), w_bn=w_branch_nsa[l].astype(BF16), w_out=w_out[l].astype(BF16),
        ln_mix_g=ln_mix_g[l].reshape(1, -1), ln_mix_b=ln_mix_b[l].reshape(1, -1),
        xa_wq=xa_wq[l].astype(BF16), xa_wkv=xa_wkv[l].astype(BF16), xa_wo=xa_wo[l].astype(BF16),
        ln_xa_g=ln_xa_g[l].reshape(1, -1), ln_xa_b=ln_xa_b[l].reshape(1, -1),
        rw_hi=rw_hi, rw_lo=rw_lo, rb=router_b.reshape(-1, 1),
        ln_ffn_g=ln_ffn_g[l].reshape(1, -1), ln_ffn_b=ln_ffn_b[l].reshape(1, -1),
    )


def kernel(x, mem, w_in, gla_w_a2, gla_b_a, gla_norm_g, nsa_cmp_pe, nsa_cmp_w1, nsa_cmp_w2, w_branch_gla, w_branch_nsa, w_out, ln_mix_g, ln_mix_b, xa_wq, xa_wkv, xa_wo, ln_xa_g, ln_xa_b, router_w, router_b, moe_w_in, moe_w_down, ln_ffn_g, ln_ffn_b):
    B, T, d = x.shape
    assert T % 512 == 0 and d == 2048 and mem.shape[1] == MEM_LEN
    n = B * T
    params = (w_in, gla_w_a2, gla_b_a, gla_norm_g, nsa_cmp_pe, nsa_cmp_w1, nsa_cmp_w2, w_branch_gla, w_branch_nsa,
              w_out, ln_mix_g, ln_mix_b, xa_wq, xa_wkv, xa_wo, ln_xa_g, ln_xa_b, router_w, router_b, moe_w_in,
              moe_w_down, ln_ffn_g, ln_ffn_b)
    slopes = (2.0 ** (-8.0 * jnp.arange(1, NSA_HEADS + 1, dtype=F32) / NSA_HEADS)).astype(F32)
    nc, ns = T // CMP_STRIDE, T // SEL_LEN
    cs = np.arange(nc) * CMP_STRIDE
    ss = np.arange(ns) * SEL_LEN
    ovt = ((cs[None, :] < ss[:, None] + SEL_LEN) & (cs[None, :] + CMP_LEN > ss[:, None])
           & (cs[None, :] + CMP_LEN <= T)).astype(np.float32)
    assert NSA_DH + ns <= LANES
    epad = np.zeros((T, LANES), np.float32)
    epad[np.arange(T), NSA_DH + np.arange(T) // SEL_LEN] = 1.0
    consts = (slopes, jnp.asarray(ovt, BF16), jnp.asarray(epad, BF16))

    xf = x.reshape(n, d)
    xb = xf.astype(BF16)
    mem_b = mem.reshape(B * MEM_LEN, d).astype(BF16)
    moe_w = (moe_w_in, moe_w_down)
    for l in range(DEPTH):
        p = _prep_layer(l, *params)
        xf, xb = _layer(xf, xb, mem_b, p, moe_w, l, consts, B, T)
    return xf.reshape(B, T, d)
```

```python
import functools

import jax
import jax.numpy as jnp
import numpy as np
from jax import lax
from jax.experimental import pallas as pl
from jax.experimental.pallas import tpu as pltpu

F32 = jnp.float32
BF16 = jnp.bfloat16
I32 = jnp.int32

DEPTH = 2
MEM_LEN = 256
GLA_HEADS = 4
GLA_DK = 128
GLA_DV = 256
GLA_GATE_RANK = 16
GLA_TAU = 16.0
GLA_CHUNK = 64
NSA_HEADS = 16
NSA_GROUPS = 4
NSA_HPG = NSA_HEADS // NSA_GROUPS
NSA_DH = 64
CMP_LEN = 32
CMP_STRIDE = 16
CMP_HIDDEN = 256
SEL_LEN = 64
SEL_TOPN = 8
WINDOW = 512
XA_HEADS = 4
XA_DH = 128
N_EXPERTS = 16
N_GROUPS = 4
EXPERTS_PER_GROUP = N_EXPERTS // N_GROUPS
TOP_K = 2
D_FF = 1536
DN_ALPHA = float((2 * DEPTH) ** 0.25)
LN_EPS = 1e-5
NEG = -1e30
LOG2E = 1.4426950408889634
FORCE_BONUS = 1e6

GLA_QK = GLA_HEADS * GLA_DK
GLA_V = GLA_HEADS * GLA_DV
NSA_Q = NSA_HEADS * NSA_DH
NSA_KV = NSA_GROUPS * NSA_DH

LANES = 128
VMEM_LIMIT = 56 * 1024 * 1024

COL_MG = 0
COL_GQ = 2 * 2048
COL_GK = COL_GQ + GLA_QK
COL_GV = COL_GK + GLA_QK
COL_GR = COL_GV + GLA_V
COL_KS = COL_GR + GLA_V
COL_KW = COL_KS + NSA_GROUPS * LANES
COL_END = COL_KW + NSA_GROUPS * LANES
SCOL_CK = LANES
SCOL_CV = SCOL_CK + NSA_KV
SCOL_END = SCOL_CV + NSA_KV
TROW_Q = 0
TROW_VS = NSA_Q
TROW_VW = TROW_VS + NSA_KV
TROW_END = TROW_VW + NSA_KV

MOE_BLOCK = 512
FF_TILE = 512


def _cp(sem):
    return pltpu.CompilerParams(dimension_semantics=sem, vmem_limit_bytes=VMEM_LIMIT)


def _dot(a, b):
    return jnp.dot(a, b, preferred_element_type=F32)


def _dot_nt(a, b):
    return lax.dot_general(a, b, (((1,), (1,)), ((), ())), preferred_element_type=F32)


def _dot_tn(a, b):
    return lax.dot_general(a, b, (((0,), (0,)), ((), ())), preferred_element_type=F32)


def _store_rows_contiguous(ref, value, accumulate=False):
    rows, d = value.shape
    chunks = d // LANES
    for c in range(chunks):
        idx = pl.ds(c, rows, stride=chunks)
        part = value[:, c * LANES:(c + 1) * LANES]
        ref[idx, :] = ref[idx, :] + part if accumulate else part


def _load_rows_contiguous(ref, rows, dtype=None):
    chunks = ref.shape[0] // rows
    parts = [ref[pl.ds(c, rows, stride=chunks), :] for c in range(chunks)]
    if dtype is not None:
        parts = [p.astype(dtype) for p in parts]
    return jnp.concatenate(parts, axis=1)


def _layer_norm(z, g, b):
    mu = jnp.mean(z, axis=-1, keepdims=True)
    zc = z - mu
    var = jnp.mean(zc * zc, axis=-1, keepdims=True)
    return zc * lax.rsqrt(var + LN_EPS) * g + b


def _mm_kernel(a_ref, b_ref, o_ref):
    o_ref[...] = _dot(a_ref[...], b_ref[...]).astype(o_ref.dtype)


def _matmul(a, b, out_dtype, tm, tn):
    m, k = a.shape
    n = b.shape[1]
    return pl.pallas_call(
        _mm_kernel,
        grid=(m // tm, n // tn),
        in_specs=[pl.BlockSpec((tm, k), lambda i, j: (i, 0)),
                  pl.BlockSpec((k, tn), lambda i, j: (0, j))],
        out_specs=pl.BlockSpec((tm, tn), lambda i, j: (i, j)),
        out_shape=jax.ShapeDtypeStruct((m, n), out_dtype),
        compiler_params=_cp(("parallel", "parallel")),
        name="matmul",
    )(a, b)


def _mm_nt_kernel(wt_ref, x_ref, o_ref):
    o_ref[0] = _dot_nt(wt_ref[...], x_ref[...]).astype(o_ref.dtype)


def _matmul_t(x, wt, B, T, tm, tr):
    n, k = x.shape
    r = wt.shape[0]
    nt = T // tm
    return pl.pallas_call(
        _mm_nt_kernel,
        grid=(n // tm, r // tr),
        in_specs=[pl.BlockSpec((tr, k), lambda i, j: (j, 0)),
                  pl.BlockSpec((tm, k), lambda i, j: (i, 0))],
        out_specs=pl.BlockSpec((1, tr, tm), lambda i, j: (i // nt, j, i % nt)),
        out_shape=jax.ShapeDtypeStruct((B, r, T), BF16),
        compiler_params=_cp(("parallel", "parallel")),
        name="matmul_t",
    )(wt, x)


def _gla_kernel(q_ref, k_ref, v_ref, r_ref, sm_ref, wa_ref, ba_ref, ng_ref, o_ref, st_ref):
    C = GLA_CHUNK
    n_chunks = q_ref.shape[0] // C
    st_ref[...] = jnp.zeros_like(st_ref)
    rowi = lax.broadcasted_iota(I32, (C, GLA_DK), 0)
    tt = lax.broadcasted_iota(I32, (C, C), 0)
    ss = lax.broadcasted_iota(I32, (C, C), 1)
    levels = (1, 2, 4, 8, 16, 32)
    pair_masks = [((tt // (2 * L)) == (ss // (2 * L))) & ((tt & L) != 0) & ((ss & L) == 0) for L in levels]
    diag_mask = tt == ss
    scale = GLA_DK ** -0.5

    def head_chunk(rows, h, z):
        qk_cols = slice(h * GLA_DK, (h + 1) * GLA_DK)
        v_cols = slice(h * GLA_DV, (h + 1) * GLA_DV)
        q = q_ref[rows, qk_cols].astype(F32) * scale
        k = k_ref[rows, qk_cols].astype(F32)
        v = v_ref[rows, v_cols]
        g = (jnp.minimum(z, 0.0) - jnp.log1p(jnp.exp(-jnp.abs(z)))) * (1.0 / GLA_TAU)
        incl = g
        tot = g
        att = jnp.where(diag_mask, _dot_nt(q.astype(BF16), k.astype(BF16)), 0.0)
        for L, pm in zip(levels, pair_masks):
            ql = (q * jnp.exp(incl)).astype(BF16)
            kl = (k * jnp.exp(tot - incl)).astype(BF16)
            att = jnp.where(pm, _dot_nt(ql, kl), att)
            upper = (rowi & L) != 0
            from_lower = pltpu.roll(tot, L, 0)
            from_upper = pltpu.roll(tot, C - L, 0)
            incl = incl + jnp.where(upper, from_lower, 0.0)
            tot = tot + jnp.where(upper, from_lower, from_upper)
        qd = (q * jnp.exp(incl)).astype(BF16)
        kd = (k * jnp.exp(tot - incl)).astype(BF16)
        st = st_ref[h]
        o = _dot_nt(qd, st.astype(BF16)) + _dot(att.astype(BF16), v)
        st_ref[h] = st * jnp.exp(tot[0:1, :]) + _dot_tn(v, kd)
        mu = jnp.mean(o, axis=-1, keepdims=True)
        oc = o - mu
        var = jnp.mean(oc * oc, axis=-1, keepdims=True)
        on = oc * lax.rsqrt(var + LN_EPS) * ng_ref[:, v_cols]
        r = r_ref[rows, v_cols].astype(F32)
        o_ref[rows, v_cols] = (on * (r * jax.nn.sigmoid(r))).astype(o_ref.dtype)

    def chunk(c, carry):
        rows = pl.ds(pl.multiple_of(c * C, C), C)
        z = _dot(sm_ref[rows, :].astype(BF16), wa_ref[...]) + ba_ref[...]
        for h in range(GLA_HEADS):
            head_chunk(rows, h, z[:, h * GLA_DK:(h + 1) * GLA_DK])
        return carry

    lax.fori_loop(0, n_chunks, chunk, 0)


def _gla(h_big, h_small, wa_pad, b_a, norm_g, B, T):
    n = B * T
    return pl.pallas_call(
        _gla_kernel,
        grid=(B,),
        in_specs=[
            pl.BlockSpec((T, GLA_QK), lambda b: (b, COL_GQ // GLA_QK)),
            pl.BlockSpec((T, GLA_QK), lambda b: (b, COL_GK // GLA_QK)),
            pl.BlockSpec((T, GLA_V), lambda b: (b, COL_GV // GLA_V)),
            pl.BlockSpec((T, GLA_V), lambda b: (b, COL_GR // GLA_V)),
            pl.BlockSpec((T, LANES), lambda b: (b, 0)),
            pl.BlockSpec((LANES, GLA_QK), lambda b: (0, 0)),
            pl.BlockSpec((1, GLA_QK), lambda b: (0, 0)),
            pl.BlockSpec((1, GLA_V), lambda b: (0, 0)),
        ],
        out_specs=pl.BlockSpec((T, GLA_V), lambda b: (b, 0)),
        out_shape=jax.ShapeDtypeStruct((n, GLA_V), BF16),
        scratch_shapes=[pltpu.VMEM((GLA_HEADS, GLA_DV, GLA_DK), F32)],
        compiler_params=_cp(("parallel",)),
        name="gla",
    )(h_big, h_big, h_big, h_big, h_small, wa_pad, b_a, norm_g)


def _compress_kernel(x_ref, w1_ref, w2_ref, w2t_ref, pe_ref, o_ref, ot_ref):
    nc = x_ref.shape[0] // CMP_STRIDE
    hid_w = w1_ref.shape[3]
    a = jnp.zeros((nc, hid_w), F32)
    bm = jnp.zeros((nc, hid_w), F32)
    c = jnp.zeros((pe_ref.shape[2], hid_w), F32)
    for l in range(CMP_STRIDE):
        xl = x_ref[pl.ds(l, nc, stride=CMP_STRIDE), :].astype(BF16)
        a = a + _dot(xl, w1_ref[0, l])
        bm = bm + _dot(xl, w1_ref[0, CMP_STRIDE + l])
    for l in range(CMP_LEN):
        c = c + _dot(pe_ref[0, l], w1_ref[0, l])
    hid = a + pltpu.roll(bm, nc - 1, 0) + c[0:1, :]
    act = jax.nn.gelu(hid).astype(BF16)
    o_ref[0, 0, 0] = _dot(act, w2_ref[0]).astype(o_ref.dtype)
    ot_ref[0, 0, 0] = _dot_nt(w2t_ref[0], act).astype(ot_ref.dtype)


def _compress(h_small, w1bd, w2bd, w2bdt, pe_pair, B, T):
    nc = T // CMP_STRIDE
    pairs = NSA_GROUPS // 2
    return pl.pallas_call(
        _compress_kernel,
        grid=(B, 2, pairs),
        in_specs=[
            pl.BlockSpec((T, LANES), lambda b, s, j: (b, SCOL_CK // LANES + s * pairs + j)),
            pl.BlockSpec((1,) + w1bd.shape[1:], lambda b, s, j: (s, 0, 0, 0)),
            pl.BlockSpec((1,) + w2bd.shape[1:], lambda b, s, j: (s, 0, 0)),
            pl.BlockSpec((1,) + w2bdt.shape[1:], lambda b, s, j: (s, 0, 0)),
            pl.BlockSpec((1,) + pe_pair.shape[1:], lambda b, s, j: (s, 0, 0, 0)),
        ],
        out_specs=[
            pl.BlockSpec((1, 1, 1, nc, LANES), lambda b, s, j: (b, s, j, 0, 0)),
            pl.BlockSpec((1, 1, 1, LANES, nc), lambda b, s, j: (b, s, j, 0, 0)),
        ],
        out_shape=[jax.ShapeDtypeStruct((B, 2, pairs, nc, LANES), BF16),
                   jax.ShapeDtypeStruct((B, 2, pairs, LANES, nc), BF16)],
        compiler_params=_cp(("parallel", "parallel", "parallel")),
        name="nsa_compress",
    )(h_small, w1bd, w2bd, w2bdt, pe_pair)


def _cmp_select_kernel(slopes_ref, qt_ref, kc_ref, vct_ref, ovt_ref, ocmp_ref, mb_ref, qpad_ref):
    g = pl.program_id(1)
    i = pl.program_id(2)
    tq = qt_ref.shape[2]
    nc = kc_ref.shape[3]
    ns = mb_ref.shape[2]
    dh = NSA_DH
    t0 = i * tq
    tpos = (t0 + lax.broadcasted_iota(I32, (nc, tq), 1))
    nidx = lax.broadcasted_iota(I32, (nc, tq), 0)
    mask_c = (nidx * CMP_STRIDE + (CMP_LEN - 1)) <= tpos
    absd = jnp.abs(tpos.astype(F32) - (nidx.astype(F32) * CMP_STRIDE + 0.5 * (CMP_LEN - 1)))
    lower = g % 2 == 0
    kc = kc_ref[0, 0, 0]
    vct = jnp.where(lower, vct_ref[0, 0, 0, 0:dh, :], vct_ref[0, 0, 0, dh:2 * dh, :])
    psum = jnp.zeros((nc, tq), F32)
    for hh in range(NSA_HPG):
        slope = slopes_ref[g * NSA_HPG + hh]
        q = qt_ref[0, hh * dh:(hh + 1) * dh, :] * jnp.asarray(dh ** -0.5, BF16)
        zero = jnp.zeros_like(q)
        qpad_ref[0:dh, :] = jnp.where(lower, q, zero)
        qpad_ref[dh:2 * dh, :] = jnp.where(lower, zero, q)
        s = _dot(kc, qpad_ref[...]) - slope * absd
        s = jnp.where(mask_c, s, NEG)
        e = jnp.exp(s - jnp.max(s, axis=0, keepdims=True))
        p = jnp.where(mask_c, e / jnp.sum(e, axis=0, keepdims=True), 0.0)
        ocmp_ref[0, hh * dh:(hh + 1) * dh, :] = _dot(vct, p.astype(BF16)).astype(ocmp_ref.dtype)
        psum = psum + p
    p_hi = psum.astype(BF16)
    p_lo = (psum - p_hi.astype(F32)).astype(BF16)
    imp = _dot(ovt_ref[...], p_hi) + _dot(ovt_ref[...], p_lo)
    j = lax.broadcasted_iota(I32, (ns, tq), 0)
    tp = t0 + lax.broadcasted_iota(I32, (ns, tq), 1)
    cur = tp // SEL_LEN
    forced = (j == 0) | (j == cur) | (j == cur - 1)
    valid = j * SEL_LEN <= tp
    score = jnp.where(valid, imp + jnp.where(forced, FORCE_BONUS, 0.0), NEG)
    rank = jnp.zeros((ns, tq), F32)
    for jp in range(ns):
        row = score[jp:jp + 1, :]
        beats = (row > score) | ((row == score) & (j > jp))
        rank = rank + jnp.where(beats, 1.0, 0.0)
    keep = valid & (rank < float(min(SEL_TOPN, ns)))
    mb_ref[0, 0] = jnp.where(keep, 0.0, NEG).astype(mb_ref.dtype)


def _cmp_select(slopes, h_t, kcmp, kcmp_t, ovt, B, T, tq):
    nc = T // CMP_STRIDE
    ns = T // SEL_LEN
    grp_rows = NSA_HPG * NSA_DH
    grid_spec = pltpu.PrefetchScalarGridSpec(
        num_scalar_prefetch=1,
        grid=(B, NSA_GROUPS, T // tq),
        in_specs=[
            pl.BlockSpec((1, grp_rows, tq), lambda b, g, i, s: (b, TROW_Q // grp_rows + g, i)),
            pl.BlockSpec((1, 1, 1, nc, LANES), lambda b, g, i, s: (b, 0, g // 2, 0, 0)),
            pl.BlockSpec((1, 1, 1, LANES, nc), lambda b, g, i, s: (b, 1, g // 2, 0, 0)),
            pl.BlockSpec((ns, nc), lambda b, g, i, s: (0, 0)),
        ],
        out_specs=[
            pl.BlockSpec((1, grp_rows, tq), lambda b, g, i, s: (b, g, i)),
            pl.BlockSpec((1, 1, ns, tq), lambda b, g, i, s: (b, g, 0, i)),
        ],
        scratch_shapes=[pltpu.VMEM((LANES, tq), BF16)],
    )
    return pl.pallas_call(
        _cmp_select_kernel,
        grid_spec=grid_spec,
        out_shape=[jax.ShapeDtypeStruct((B, NSA_Q, T), BF16),
                   jax.ShapeDtypeStruct((B, NSA_GROUPS, ns, T), BF16)],
        compiler_params=_cp(("parallel", "parallel", "parallel")),
        name="nsa_cmp_select",
    )(slopes, h_t, kcmp, kcmp_t, ovt)


def _sel_win_kernel(slopes_ref, qt_ref, ks_ref, kw_ref, vs_ref, vw_ref, epad_ref, mb_ref, ocmp_ref, gt_ref,
                    o_ref, qaug_ref, m_ref, acc_ref, srow_ref, bias_ref, s_ref, p_ref, alpha_ref,
                    kall_ref, vall_ref):
    g = pl.program_id(1)
    i = pl.program_id(2)
    tq = qt_ref.shape[2]
    tk = tq
    ns = mb_ref.shape[2]
    dh = NSA_DH
    wide = NSA_HPG * tq
    t0 = i * tq
    BIG = -NEG

    @pl.when(i == 0)
    def _():
        srow = jnp.concatenate([jnp.full((1, tq), slopes_ref[g * NSA_HPG + hh] * LOG2E, F32)
                                for hh in range(NSA_HPG)], axis=1)
        srow_ref[...] = srow
        lane = lax.broadcasted_iota(I32, (tk, wide), 1) & (tq - 1)
        dist0 = (lane - lax.broadcasted_iota(I32, (tk, wide), 0)).astype(F32)
        sd0 = srow * dist0
        bias_ref[0] = sd0
        bias_ref[1] = sd0 + jnp.where(dist0 >= 0.0, 0.0, BIG)
        bias_ref[2] = sd0 + jnp.where(dist0 < 0.0, 0.0, BIG)
        bias_ref[3] = jnp.full((tk, wide), BIG, F32)
        kall_ref[0] = ks_ref[...] + epad_ref[...]
        kall_ref[1] = kw_ref[...]
        extra = jnp.where(lax.broadcasted_iota(I32, (vall_ref.shape[1] - dh, vall_ref.shape[2]), 0) == 0, 1.0, 0.0)
        vall_ref[0, 0:dh, :] = vs_ref[0]
        vall_ref[1, 0:dh, :] = vw_ref[0]
        vall_ref[0, dh:, :] = extra.astype(BF16)
        vall_ref[1, dh:, :] = extra.astype(BF16)

    for hh in range(NSA_HPG):
        cols = slice(hh * tq, (hh + 1) * tq)
        q = qt_ref[0, hh * dh:(hh + 1) * dh, :].astype(F32) * (dh ** -0.5 * LOG2E)
        qaug_ref[0:dh, cols] = q.astype(BF16)
        qaug_ref[dh:dh + ns, cols] = mb_ref[0, 0]
        qaug_ref[dh + ns:, cols] = jnp.zeros((qaug_ref.shape[0] - dh - ns, tq), BF16)

    m_ref[...] = jnp.full(m_ref.shape, NEG, F32)
    acc_ref[...] = jnp.zeros(acc_ref.shape, F32)

    n_back = WINDOW // tk
    n_sel = i + 1
    n_win = jnp.minimum(i, n_back) + 1
    n_steps = n_sel + n_win

    def describe(n):
        n = jnp.maximum(n, 0)
        is_win = n >= n_sel
        kb = jnp.clip(jnp.where(is_win, i - n_win + 1 + (n - n_sel), n), 0, i)
        mode = jnp.where(kb == i, 1, jnp.where(is_win & (kb == i - n_back), 2, 0))
        mode = jnp.where(n >= n_steps, 3, mode)
        return is_win.astype(I32), kb, mode

    def scores(n, slot):
        br, kb, _ = describe(n)
        s0 = pl.multiple_of(kb * tk, tk)
        s_ref[slot] = _dot(kall_ref[br, pl.ds(s0, tk), :], qaug_ref[...])

    def softmax(n, slot):
        br, kb, mode = describe(n)
        crow = srow_ref[...] * ((i - kb) * tk).astype(F32)
        s = s_ref[slot] - bias_ref[mode]
        m_old = m_ref[br]
        m_new = jnp.maximum(m_old, jnp.max(s, axis=0, keepdims=True) - crow)
        alpha = jnp.exp2(m_old - m_new)
        p_ref[slot] = jnp.exp2((s - (m_new + crow)).astype(BF16))
        m_ref[br] = m_new
        alpha_ref[slot] = alpha

    def weighted_values(n, slot):
        br, kb, _ = describe(n)
        s0 = pl.multiple_of(kb * tk, tk)
        acc_ref[br] = alpha_ref[slot] * acc_ref[br] + _dot(vall_ref[br, :, pl.ds(s0, tk)], p_ref[slot])

    p_ref[1] = jnp.zeros(p_ref.shape[1:], BF16)
    alpha_ref[1] = jnp.ones(alpha_ref.shape[1:], F32)
    scores(0, 0)

    def pair(j, carry):
        n = 2 * j
        softmax(n, 0)
        scores(n + 1, 1)
        weighted_values(n - 1, 1)
        softmax(n + 1, 1)
        scores(n + 2, 0)
        weighted_values(n, 0)
        return carry

    n_pairs = (n_steps + 1) // 2
    lax.fori_loop(0, n_pairs, pair, 0)
    weighted_values(2 * n_pairs - 1, 1)

    def gate_row(branch):
        rows = [gt_ref[0, 0, 3 * hh + branch:3 * hh + branch + 1, :] for hh in range(NSA_HPG)]
        return jax.nn.sigmoid(jnp.concatenate(rows, axis=1))

    o = (gate_row(1) * (acc_ref[0, 0:dh, :] / acc_ref[0, dh:dh + 1, :])
         + gate_row(2) * (acc_ref[1, 0:dh, :] / acc_ref[1, dh:dh + 1, :]))
    ocmp = jnp.concatenate([ocmp_ref[0, hh * dh:(hh + 1) * dh, :] for hh in range(NSA_HPG)], axis=1).astype(F32)
    o = o + gate_row(0) * ocmp
    o_heads = jnp.concatenate([o[:, hh * tq:(hh + 1) * tq] for hh in range(NSA_HPG)], axis=0)
    o_ref[...] = o_heads.T.astype(o_ref.dtype)


def _sel_win(slopes, h_t, h_big, epad, mb, ocmp_t, gates_t, B, T, tq):
    ns = T // SEL_LEN
    kaug = LANES
    G = NSA_GROUPS
    grp_rows = NSA_HPG * NSA_DH
    nq = T // tq
    grid_spec = pltpu.PrefetchScalarGridSpec(
        num_scalar_prefetch=1,
        grid=(B, G, nq),
        in_specs=[
            pl.BlockSpec((1, grp_rows, tq), lambda b, g, i, s: (b, TROW_Q // grp_rows + g, i)),
            pl.BlockSpec((T, LANES), lambda b, g, i, s: (b, COL_KS // LANES + g)),
            pl.BlockSpec((T, LANES), lambda b, g, i, s: (b, COL_KW // LANES + g)),
            pl.BlockSpec((1, NSA_DH, T), lambda b, g, i, s: (b, TROW_VS // NSA_DH + g, 0)),
            pl.BlockSpec((1, NSA_DH, T), lambda b, g, i, s: (b, TROW_VW // NSA_DH + g, 0)),
            pl.BlockSpec((T, LANES), lambda b, g, i, s: (0, 0)),
            pl.BlockSpec((1, 1, ns, tq), lambda b, g, i, s: (b, g, 0, i)),
            pl.BlockSpec((1, grp_rows, tq), lambda b, g, i, s: (b, g, i)),
            pl.BlockSpec((1, 1, 16, tq), lambda b, g, i, s: (b, g, 0, i)),
        ],
        out_specs=pl.BlockSpec((tq, grp_rows), lambda b, g, i, s: (b * nq + i, g)),
        scratch_shapes=[
            pltpu.VMEM((kaug, NSA_HPG * tq), BF16),
            pltpu.VMEM((2, 1, NSA_HPG * tq), F32),
            pltpu.VMEM((2, NSA_DH + 16, NSA_HPG * tq), F32),
            pltpu.VMEM((1, NSA_HPG * tq), F32),
            pltpu.VMEM((4, tq, NSA_HPG * tq), F32),
            pltpu.VMEM((2, tq, NSA_HPG * tq), F32),
            pltpu.VMEM((2, tq, NSA_HPG * tq), BF16),
            pltpu.VMEM((2, 1, NSA_HPG * tq), F32),
            pltpu.VMEM((2, T, kaug), BF16),
            pltpu.VMEM((2, NSA_DH + 16, T), BF16),
        ],
    )
    return pl.pallas_call(
        _sel_win_kernel,
        grid_spec=grid_spec,
        out_shape=jax.ShapeDtypeStruct((B * T, NSA_Q), BF16),
        compiler_params=_cp(("parallel", "parallel", "arbitrary")),
        name="nsa_sel_win",
    )(slopes, h_t, h_big, h_big, h_t, h_t, epad, mb, ocmp_t, gates_t)


def _mix_kernel(og_ref, on_ref, mg1_ref, mg2_ref, x_ref, wg_ref, wn_ref, wo_ref, lg_ref, lb_ref,
                x1_ref, x1b_ref):
    g1 = _dot(og_ref[...], wg_ref[...])
    g2 = _dot(on_ref[...], wn_ref[...])
    merged = (jax.nn.sigmoid(mg1_ref[...].astype(F32)) * g1
              + jax.nn.sigmoid(mg2_ref[...].astype(F32)) * g2)
    y = _dot(merged.astype(BF16), wo_ref[...])
    x1 = _layer_norm(DN_ALPHA * x_ref[...] + y, lg_ref[...], lb_ref[...])
    x1_ref[...] = x1
    x1b_ref[...] = x1.astype(BF16)


def _const_spec(shape):
    nd = len(shape)
    return pl.BlockSpec(shape, lambda *_: (0,) * nd, pipeline_mode=pl.Buffered(1))


def _mix(o_gla, o_nsa, h_big, x, wg, wn, wo, lg, lb, tm):
    n, d = x.shape
    return pl.pallas_call(
        _mix_kernel,
        grid=(n // tm,),
        in_specs=[
            pl.BlockSpec((tm, GLA_V), lambda i: (i, 0)),
            pl.BlockSpec((tm, NSA_Q), lambda i: (i, 0)),
            pl.BlockSpec((tm, d), lambda i: (i, 0)),
            pl.BlockSpec((tm, d), lambda i: (i, 1)),
            pl.BlockSpec((tm, d), lambda i: (i, 0)),
            _const_spec(wg.shape), _const_spec(wn.shape), _const_spec(wo.shape),
            _const_spec(lg.shape), _const_spec(lb.shape),
        ],
        out_specs=[pl.BlockSpec((tm, d), lambda i: (i, 0)), pl.BlockSpec((tm, d), lambda i: (i, 0))],
        out_shape=[jax.ShapeDtypeStruct((n, d), F32), jax.ShapeDtypeStruct((n, d), BF16)],
        compiler_params=_cp(("parallel",)),
        name="mix_ln",
    )(o_gla, o_nsa, h_big, h_big, x, wg, wn, wo, lg, lb)


def _xattn_kernel(x_ref, xb_ref, kv_ref, wq_ref, wo_ref, lg_ref, lb_ref, x2_ref, x2c_ref):
    hd = XA_HEADS * XA_DH
    q = (_dot(xb_ref[...], wq_ref[...]) * (XA_DH ** -0.5)).astype(BF16)
    outs = []
    for h in range(XA_HEADS):
        kh = kv_ref[0, :, h * XA_DH:(h + 1) * XA_DH]
        vh = kv_ref[0, :, hd + h * XA_DH:hd + (h + 1) * XA_DH]
        s = _dot_nt(q[:, h * XA_DH:(h + 1) * XA_DH], kh)
        e = jnp.exp(s - jnp.max(s, axis=-1, keepdims=True))
        p = e / jnp.sum(e, axis=-1, keepdims=True)
        outs.append(_dot(p.astype(BF16), vh).astype(BF16))
    o = jnp.concatenate(outs, axis=-1)
    y = _dot(o, wo_ref[...])
    x2 = _layer_norm(DN_ALPHA * x_ref[...] + y, lg_ref[...], lb_ref[...])
    x2_ref[...] = x2
    _store_rows_contiguous(x2c_ref, x2)


def _xattn(x1, x1b, kv, wq, wo, lg, lb, B, T, tm):
    n, d = x1.shape
    nt = T // tm
    chunks = d // LANES
    return pl.pallas_call(
        _xattn_kernel,
        grid=(B, nt),
        in_specs=[
            pl.BlockSpec((tm, d), lambda b, i: (b * nt + i, 0)),
            pl.BlockSpec((tm, d), lambda b, i: (b * nt + i, 0)),
            pl.BlockSpec((1,) + kv.shape[1:], lambda b, i: (b, 0, 0)),
            _const_spec(wq.shape), _const_spec(wo.shape), _const_spec(lg.shape), _const_spec(lb.shape),
        ],
        out_specs=[pl.BlockSpec((tm, d), lambda b, i: (b * nt + i, 0)),
                   pl.BlockSpec((tm * chunks, LANES), lambda b, i: (b * nt + i, 0))],
        out_shape=[jax.ShapeDtypeStruct((n, d), F32), jax.ShapeDtypeStruct((n * chunks, LANES), F32)],
        compiler_params=_cp(("parallel", "parallel")),
        name="xattn_ln",
    )(x1, x1b, kv, wq, wo, lg, lb)


def _router_kernel(x_ref, wh_ref, wl_ref, rb_ref, e_ref, gate_ref, rank_ref, cnt_ref, carry_ref):
    i = pl.program_id(0)
    tr = x_ref.shape[0]
    E = N_EXPERTS

    @pl.when(i == 0)
    def _():
        carry_ref[...] = jnp.zeros_like(carry_ref)

    x = x_ref[...]
    x_hi = x.astype(BF16)
    x_lo = (x - x_hi.astype(F32)).astype(BF16)
    wh = wh_ref[...]
    logits = _dot_nt(wh, x_hi) + _dot_nt(wh, x_lo) + _dot_nt(wl_ref[...], x_hi)
    biased = logits + rb_ref[...]
    rows = [biased[e:e + 1, :] for e in range(E)]
    raw = [logits[e:e + 1, :] for e in range(E)]
    best_score = None
    best = None
    for gi in range(N_GROUPS):
        v = rows[gi * EXPERTS_PER_GROUP:(gi + 1) * EXPERTS_PER_GROUP]
        sc = None
        for a in range(EXPERTS_PER_GROUP):
            for b in range(a + 1, EXPERTS_PER_GROUP):
                pair = v[a] + v[b]
                sc = pair if sc is None else jnp.maximum(sc, pair)
        if best is None:
            best_score, best = sc, jnp.zeros((1, tr), I32)
        else:
            better = sc > best_score
            best_score = jnp.where(better, sc, best_score)
            best = jnp.where(better, gi, best)

    def pick(vals):
        out = vals[0:EXPERTS_PER_GROUP]
        for gi in range(1, N_GROUPS):
            out = [jnp.where(best == gi, vals[gi * EXPERTS_PER_GROUP + a], out[a]) for a in range(EXPERTS_PER_GROUP)]
        return out

    w = pick(rows)
    lraw = pick(raw)
    i1 = jnp.zeros((1, tr), I32)
    v1 = w[0]
    l1 = lraw[0]
    for a in range(1, EXPERTS_PER_GROUP):
        better = w[a] > v1
        v1 = jnp.where(better, w[a], v1)
        l1 = jnp.where(better, lraw[a], l1)
        i1 = jnp.where(better, a, i1)
    i2 = jnp.full((1, tr), -1, I32)
    v2 = jnp.full((1, tr), -jnp.inf, F32)
    l2 = jnp.zeros((1, tr), F32)
    for a in range(EXPERTS_PER_GROUP):
        better = (i1 != a) & ((w[a] > v2) | (i2 < 0))
        v2 = jnp.where(better, w[a], v2)
        l2 = jnp.where(better, lraw[a], l2)
        i2 = jnp.where(better, a, i2)
    e1 = best * EXPERTS_PER_GROUP + i1
    e2 = best * EXPERTS_PER_GROUP + i2
    mx = jnp.maximum(l1, l2)
    p1 = jnp.exp(l1 - mx)
    p2 = jnp.exp(l2 - mx)
    den = p1 + p2
    e_ref[0:1, :] = e1
    e_ref[1:2, :] = e2
    gate_ref[0:1, :] = p1 / den
    gate_ref[1:2, :] = p2 / den
    eidx = lax.broadcasted_iota(I32, (E, tr), 0)
    is1 = eidx == e1
    is2 = eidx == e2
    member = jnp.where(is1 | is2, 1.0, 0.0)
    uu = lax.broadcasted_iota(I32, (tr, tr), 0)
    tt = lax.broadcasted_iota(I32, (tr, tr), 1)
    tri = jnp.where(uu <= tt, 1.0, 0.0).astype(BF16)
    incl = _dot(member.astype(BF16), tri)
    excl = carry_ref[:, 0:1] + incl - member
    rank_ref[0:1, :] = jnp.sum(jnp.where(is1, excl, 0.0), axis=0, keepdims=True).astype(I32)
    rank_ref[1:2, :] = jnp.sum(jnp.where(is2, excl, 0.0), axis=0, keepdims=True).astype(I32)
    new_carry = carry_ref[...] + jnp.sum(member, axis=1, keepdims=True)
    carry_ref[...] = new_carry
    cnt_ref[...] = new_carry


def _router(x2, rw_hi, rw_lo, rb, tr):
    n, d = x2.shape
    E = N_EXPERTS
    return pl.pallas_call(
        _router_kernel,
        grid=(n // tr,),
        in_specs=[
            pl.BlockSpec((tr, d), lambda i: (i, 0)),
            pl.BlockSpec((E, d), lambda i: (0, 0)),
            pl.BlockSpec((E, d), lambda i: (0, 0)),
            pl.BlockSpec((E, 1), lambda i: (0, 0)),
        ],
        out_specs=[
            pl.BlockSpec((2, tr), lambda i: (0, i)),
            pl.BlockSpec((2, tr), lambda i: (0, i)),
            pl.BlockSpec((2, tr), lambda i: (0, i)),
            pl.BlockSpec((E, LANES), lambda i: (0, 0)),
        ],
        out_shape=[jax.ShapeDtypeStruct((2, n), I32), jax.ShapeDtypeStruct((2, n), F32),
                   jax.ShapeDtypeStruct((2, n), I32), jax.ShapeDtypeStruct((E, LANES), F32)],
        scratch_shapes=[pltpu.VMEM((E, LANES), F32)],
        compiler_params=_cp(("arbitrary",)),
        name="moe_router",
    )(x2, rw_hi, rw_lo, rb)


def _slot_kernel(ps_ref, e_ref, rank_ref, slot_ref):
    e = e_ref[...]
    start = jnp.zeros(e.shape, I32)
    for ex in range(N_EXPERTS):
        start = jnp.where(e == ex, ps_ref[ex], start)
    slot_ref[...] = start + rank_ref[...]


def _slots(pad_start, e, rank, ts):
    n = e.shape[1]
    grid_spec = pltpu.PrefetchScalarGridSpec(
        num_scalar_prefetch=1,
        grid=(n // ts,),
        in_specs=[pl.BlockSpec((TOP_K, ts), lambda i, s: (0, i)), pl.BlockSpec((TOP_K, ts), lambda i, s: (0, i))],
        out_specs=pl.BlockSpec((TOP_K, ts), lambda i, s: (0, i)),
    )
    return pl.pallas_call(
        _slot_kernel,
        grid_spec=grid_spec,
        out_shape=jax.ShapeDtypeStruct((TOP_K, n), I32),
        compiler_params=_cp(("parallel",)),
        name="moe_slots",
    )(pad_start, e, rank)


def _dispatch_kernel(pe_ref, s0_ref, s1_ref, x_ref, buf_hbm, zero_ref, sem):
    td = s0_ref.shape[0]
    chunks = x_ref.shape[0] // td
    blk_rows = MOE_BLOCK * chunks
    slots = (s0_ref, s1_ref)

    @pl.when(pl.program_id(0) == 0)
    def _():
        zero_ref[...] = jnp.zeros_like(zero_ref)

        def zero_copy(ex):
            last = pl.multiple_of(jnp.maximum(pe_ref[ex] - MOE_BLOCK, 0) * chunks, blk_rows)
            return pltpu.make_async_copy(zero_ref, buf_hbm.at[pl.ds(last, blk_rows), :], sem)

        def nonempty(ex):
            return pe_ref[ex] > (pe_ref[ex - 1] if ex > 0 else 0)

        n_blocks = buf_hbm.shape[0] // blk_rows
        first_unused = pe_ref[N_EXPERTS - 1] // MOE_BLOCK

        def tail_copy(k):
            row = pl.multiple_of((first_unused + k) * blk_rows, blk_rows)
            return pltpu.make_async_copy(zero_ref, buf_hbm.at[pl.ds(row, blk_rows), :], sem)

        for ex in range(N_EXPERTS):
            pl.when(nonempty(ex))(lambda ex=ex: zero_copy(ex).start())
            pl.when(first_unused + ex < n_blocks)(lambda ex=ex: tail_copy(ex).start())
        for ex in range(N_EXPERTS):
            pl.when(nonempty(ex))(lambda ex=ex: zero_copy(ex).wait())
            pl.when(first_unused + ex < n_blocks)(lambda ex=ex: tail_copy(ex).wait())

    def issue(t, carry):
        for kk in range(TOP_K):
            src = pl.multiple_of(t * chunks, chunks)
            dest = pl.multiple_of(slots[kk][t] * chunks, chunks)
            pltpu.make_async_copy(x_ref.at[pl.ds(src, chunks), :], buf_hbm.at[pl.ds(dest, chunks), :], sem).start()
        return carry

    lax.fori_loop(0, td, issue, 0, unroll=8)
    for kk in range(TOP_K):
        pltpu.make_async_copy(x_ref, buf_hbm.at[pl.ds(0, td * chunks), :], sem).wait()


def _dispatch(pad_end, slot0, slot1, x2c, n_rows, td):
    n = slot0.shape[0]
    chunks = x2c.shape[0] // n
    grid_spec = pltpu.PrefetchScalarGridSpec(
        num_scalar_prefetch=1,
        grid=(n // td,),
        in_specs=[
            pl.BlockSpec((td,), lambda i, s: (i,), memory_space=pltpu.SMEM),
            pl.BlockSpec((td,), lambda i, s: (i,), memory_space=pltpu.SMEM),
            pl.BlockSpec((td * chunks, LANES), lambda i, s: (i, 0)),
        ],
        out_specs=pl.BlockSpec(memory_space=pl.ANY),
        scratch_shapes=[pltpu.VMEM((MOE_BLOCK * chunks, LANES), F32), pltpu.SemaphoreType.DMA(())],
    )
    return pl.pallas_call(
        _dispatch_kernel,
        grid_spec=grid_spec,
        out_shape=jax.ShapeDtypeStruct((n_rows * chunks, LANES), F32),
        compiler_params=_cp(("arbitrary",)),
        name="moe_dispatch",
    )(pad_end, slot0, slot1, x2c)


def _expert_kernel(be_ref, nb_ref, x_ref, win_hbm, wdn_hbm, y_ref, xb_ref, wa_s, wu_s, wd_s, sa, su, sd, sems,
                   *, layer):
    b = pl.program_id(0)
    nf = D_FF // FF_TILE
    n_used = nb_ref[0]
    e = be_ref[b]
    e_prev = be_ref[jnp.maximum(b - 1, 0)]
    e_next = be_ref[jnp.minimum(b + 1, pl.num_programs(0) - 1)]
    active = b < n_used
    is_first = active & ((b == 0) | (e_prev != e))
    feeds_next = active & (b + 1 < n_used) & (e_next != e)

    def tile_copies(ex, f):
        lo = f * FF_TILE
        return (pltpu.make_async_copy(win_hbm.at[layer, ex, :, pl.ds(lo, FF_TILE)], sa, sems.at[0]),
                pltpu.make_async_copy(win_hbm.at[layer, ex, :, pl.ds(D_FF + lo, FF_TILE)], su, sems.at[1]),
                pltpu.make_async_copy(wdn_hbm.at[layer, ex, pl.ds(lo, FF_TILE), :], sd, sems.at[2]))

    def start(ex, f):
        for c in tile_copies(ex, f):
            c.start()

    def finish(ex, f):
        for c in tile_copies(ex, f):
            c.wait()
        wa_s[f] = sa[...].astype(BF16)
        wu_s[f] = su[...].astype(BF16)
        wd_s[f] = sd[...].astype(BF16)

    @pl.when(b == 0)
    def _():
        for f in range(nf - 1):
            start(e, f)
            finish(e, f)
        start(e, nf - 1)

    @pl.when(jnp.logical_not(active))
    def _():
        y_ref[...] = jnp.zeros_like(y_ref)

    @pl.when(active)
    def _():
        xb_ref[...] = _load_rows_contiguous(x_ref, MOE_BLOCK, BF16)
        for f in range(nf):
            xb = xb_ref[...]
            a = _dot(xb, wa_s[f])
            u = _dot(xb, wu_s[f])
            act = (a * jax.nn.sigmoid(a) * u).astype(BF16)
            y = _dot(act, wd_s[f])
            _store_rows_contiguous(y_ref, y, accumulate=f > 0)
            if f == 0:
                pl.when(is_first)(lambda: finish(e, nf - 1))

            @pl.when(feeds_next)
            def _(f=f):
                if f >= 1:
                    finish(e_next, f - 1)
                start(e_next, f)


def _experts(blk_expert, n_used, buf, w_in, w_down, layer):
    d = w_down.shape[-1]
    chunks = d // LANES
    blk_rows = MOE_BLOCK * chunks
    nb = buf.shape[0] // blk_rows
    nf = D_FF // FF_TILE
    grid_spec = pltpu.PrefetchScalarGridSpec(
        num_scalar_prefetch=2,
        grid=(nb,),
        in_specs=[
            pl.BlockSpec((blk_rows, LANES), lambda b, be, nu: (jnp.minimum(b, nu[0] - 1), 0)),
            pl.BlockSpec(memory_space=pl.ANY),
            pl.BlockSpec(memory_space=pl.ANY),
        ],
        out_specs=pl.BlockSpec((blk_rows, LANES), lambda b, be, nu: (b, 0)),
        scratch_shapes=[
            pltpu.VMEM((MOE_BLOCK, d), BF16),
            pltpu.VMEM((nf, d, FF_TILE), BF16),
            pltpu.VMEM((nf, d, FF_TILE), BF16),
            pltpu.VMEM((nf, FF_TILE, d), BF16),
            pltpu.VMEM((d, FF_TILE), F32),
            pltpu.VMEM((d, FF_TILE), F32),
            pltpu.VMEM((FF_TILE, d), F32),
            pltpu.SemaphoreType.DMA((3,)),
        ],
    )
    return pl.pallas_call(
        functools.partial(_expert_kernel, layer=layer),
        grid_spec=grid_spec,
        out_shape=jax.ShapeDtypeStruct(buf.shape, F32),
        compiler_params=_cp(("arbitrary",)),
        name="moe_experts",
    )(blk_expert, n_used, buf, w_in, w_down)


def _combine_kernel(s0_ref, s1_ref, y_hbm, x_ref, gate_ref, lg_ref, lb_ref, x3_ref, x3b_ref,
                    y0_ref, y1_ref, sem):
    tc = x_ref.shape[0]
    chunks = y0_ref.shape[0] // tc
    bufs = (y0_ref, y1_ref)
    slots = (s0_ref, s1_ref)

    def issue(t, carry):
        dst = pl.multiple_of(t * chunks, chunks)
        for kk in range(TOP_K):
            src = pl.multiple_of(slots[kk][t] * chunks, chunks)
            pltpu.make_async_copy(y_hbm.at[pl.ds(src, chunks), :], bufs[kk].at[pl.ds(dst, chunks), :], sem).start()
        return carry

    lax.fori_loop(0, tc, issue, 0, unroll=8)
    for kk in range(TOP_K):
        pltpu.make_async_copy(y_hbm.at[pl.ds(0, tc * chunks), :], bufs[kk], sem).wait()
    gate = gate_ref[...]
    y0 = _load_rows_contiguous(y0_ref, tc)
    y1 = _load_rows_contiguous(y1_ref, tc)
    z = DN_ALPHA * x_ref[...] + gate[:, 0:1] * y0 + gate[:, 1:2] * y1
    x3 = _layer_norm(z, lg_ref[...], lb_ref[...])
    x3_ref[...] = x3
    x3b_ref[...] = x3.astype(BF16)


def _combine(slot0, slot1, y, x2, gate_nt, lg, lb, tc):
    n, d = x2.shape
    return pl.pallas_call(
        _combine_kernel,
        grid=(n // tc,),
        in_specs=[
            pl.BlockSpec((tc,), lambda i: (i,), memory_space=pltpu.SMEM),
            pl.BlockSpec((tc,), lambda i: (i,), memory_space=pltpu.SMEM),
            pl.BlockSpec(memory_space=pl.ANY),
            pl.BlockSpec((tc, d), lambda i: (i, 0)),
            pl.BlockSpec((tc, 2), lambda i: (i, 0)),
            pl.BlockSpec((1, d), lambda i: (0, 0)),
            pl.BlockSpec((1, d), lambda i: (0, 0)),
        ],
        out_specs=[pl.BlockSpec((tc, d), lambda i: (i, 0)), pl.BlockSpec((tc, d), lambda i: (i, 0))],
        out_shape=[jax.ShapeDtypeStruct((n, d), F32), jax.ShapeDtypeStruct((n, d), BF16)],
        scratch_shapes=[pltpu.VMEM((tc * d // LANES, LANES), F32), pltpu.VMEM((tc * d // LANES, LANES), F32),
                        pltpu.SemaphoreType.DMA(())],
        compiler_params=_cp(("arbitrary",)),
        name="moe_combine_ln",
    )(slot0, slot1, y, x2, gate_nt, lg, lb)


def _layer(x, xb, mem_b, p, moe_w, layer, consts, B, T):
    n, d = x.shape
    G, HPG, DH = NSA_GROUPS, NSA_HPG, NSA_DH
    slopes, ovt, epad = consts

    h_big = _matmul(xb, p["w_big"], BF16, 1024, 512)
    h_small = _matmul(xb, p["w_small"], F32, 1024, SCOL_END)
    h_t = _matmul_t(xb, p["w_t"], B, T, 1024, 512)

    o_gla = _gla(h_big, h_small, p["wa_pad"], p["b_a"], p["norm_g"], B, T)

    kcmp, kcmp_t = _compress(h_small, p["cmp_w1bd"], p["cmp_w2bd"], p["cmp_w2bdt"], p["cmp_pe_pair"], B, T)
    ocmp_t, mb = _cmp_select(slopes, h_t, kcmp, kcmp_t, ovt, B, T, 512)
    gates_t = h_small[:, GLA_GATE_RANK:GLA_GATE_RANK + 3 * NSA_HEADS].reshape(B, T, G, 3 * HPG)
    gates_t = jnp.pad(gates_t.transpose(0, 2, 3, 1), ((0, 0), (0, 0), (0, 16 - 3 * HPG), (0, 0)))
    o_nsa = _sel_win(slopes, h_t, h_big, epad, mb, ocmp_t, gates_t, B, T, 256)

    x1, x1b = _mix(o_gla, o_nsa, h_big, x, p["w_bg"], p["w_bn"], p["w_out"], p["ln_mix_g"], p["ln_mix_b"], 256)

    kvm = _matmul(mem_b, p["xa_wkv"], BF16, 512, 512).reshape(B, MEM_LEN, 2 * XA_HEADS * XA_DH)
    x2, x2c = _xattn(x1, x1b, kvm, p["xa_wq"], p["xa_wo"], p["ln_xa_g"], p["ln_xa_b"], B, T, 256)

    e, gate, rank, cnt = _router(x2, p["rw_hi"], p["rw_lo"], p["rb"], 512)
    counts = cnt[:, 0].astype(I32)
    padded = (counts + MOE_BLOCK - 1) // MOE_BLOCK * MOE_BLOCK
    pad_end = jnp.cumsum(padded)
    pad_start = (pad_end - padded).astype(I32)
    nb = (n * TOP_K) // MOE_BLOCK + N_EXPERTS
    n_used = (pad_end[-1] // MOE_BLOCK).astype(I32).reshape(1)
    blk_start = jnp.arange(nb, dtype=I32) * MOE_BLOCK
    blk_expert = jnp.minimum(jnp.sum(blk_start[:, None] >= pad_end[None, :], axis=1), N_EXPERTS - 1).astype(I32)
    blk_expert = jnp.where(jnp.arange(nb) < n_used[0], blk_expert, blk_expert[jnp.maximum(n_used[0] - 1, 0)])
    slot = _slots(pad_start, e, rank, 2048)
    buf = _dispatch(pad_end.astype(I32), slot[0], slot[1], x2c, nb * MOE_BLOCK, 512)
    y = _experts(blk_expert, n_used, buf, moe_w[0], moe_w[1], layer)
    x3, x3b = _combine(slot[0], slot[1], y, x2, gate.T, p["ln_ffn_g"], p["ln_ffn_b"], 256)
    return x3, x3b


def _prep_layer(l, w_in, gla_w_a2, gla_b_a, gla_norm_g, nsa_cmp_pe, nsa_cmp_w1, nsa_cmp_w2, w_branch_gla,
                w_branch_nsa, w_out, ln_mix_g, ln_mix_b, xa_wq, xa_wkv, xa_wo, ln_xa_g, ln_xa_b, router_w,
                router_b, moe_w_in, moe_w_down, ln_ffn_g, ln_ffn_b):
    d = w_in.shape[1]
    w = w_in[l]
    o_gq, o_gk, o_gv, o_gr = 0, GLA_QK, 2 * GLA_QK, 2 * GLA_QK + GLA_V
    o_ga = o_gr + GLA_V
    o_nq = o_ga + GLA_GATE_RANK
    o_nkv = o_nq + NSA_Q
    o_ng = o_nkv + 6 * NSA_KV
    o_mg = o_ng + 3 * NSA_HEADS
    G, DH = NSA_GROUPS, NSA_DH

    def kv_cols(kind):
        return w[:, o_nkv + kind * NSA_KV:o_nkv + (kind + 1) * NSA_KV]

    def slabs(wk):
        return jnp.pad(wk.reshape(d, G, DH), ((0, 0), (0, 0), (0, LANES - DH))).reshape(d, G * LANES)

    w_big = jnp.concatenate([w[:, o_mg:o_mg + 2 * d], w[:, o_gq:o_ga], slabs(kv_cols(2)), slabs(kv_cols(4))],
                            axis=1).astype(BF16)
    w_small = jnp.concatenate([w[:, o_ga:o_nq], w[:, o_ng:o_mg],
                               jnp.zeros((d, LANES - GLA_GATE_RANK - 3 * NSA_HEADS), F32),
                               kv_cols(0), kv_cols(1)], axis=1).astype(BF16)
    w_t = jnp.concatenate([w[:, o_nq:o_nkv], kv_cols(3), kv_cols(5)], axis=1).T.astype(BF16)
    w1 = nsa_cmp_w1[l].reshape(2, CMP_LEN, DH, CMP_HIDDEN)
    z1 = jnp.zeros_like(w1)
    w1bd = jnp.concatenate([jnp.concatenate([w1, z1], axis=3), jnp.concatenate([z1, w1], axis=3)], axis=2)
    w2 = nsa_cmp_w2[l]
    z2 = jnp.zeros_like(w2)
    w2bd = jnp.concatenate([jnp.concatenate([w2, z2], axis=2), jnp.concatenate([z2, w2], axis=2)], axis=1)
    pe = nsa_cmp_pe[l]
    pe_pair = jnp.broadcast_to(jnp.concatenate([pe, pe], axis=-1)[:, :, None, :], (2, CMP_LEN, 16, 2 * DH))
    wa_pad = jnp.concatenate([gla_w_a2[l], jnp.zeros((LANES - GLA_GATE_RANK, GLA_QK), F32)], axis=0).astype(BF16)
    rw_t = router_w.T
    rw_hi = rw_t.astype(BF16)
    rw_lo = (rw_t - rw_hi.astype(F32)).astype(BF16)
    return dict(
        w_big=w_big, w_small=w_small, w_t=w_t, wa_pad=wa_pad,
        b_a=gla_b_a[l].reshape(1, -1), norm_g=gla_norm_g[l].reshape(1, -1),
        cmp_w1bd=w1bd.astype(BF16), cmp_w2bd=w2bd.astype(BF16), cmp_w2bdt=w2bd.transpose(0, 2, 1).astype(BF16),
        cmp_pe_pair=pe_pair.astype(BF16),
        w_bg=w_branch_gla[l].astype(BF16), w_bn=w_branch_nsa[l].astype(BF16), w_out=w_out[l].astype(BF16),
        ln_mix_g=ln_mix_g[l].reshape(1, -1), ln_mix_b=ln_mix_b[l].reshape(1, -1),
        xa_wq=xa_wq[l].astype(BF16), xa_wkv=xa_wkv[l].astype(BF16), xa_wo=xa_wo[l].astype(BF16),
        ln_xa_g=ln_xa_g[l].reshape(1, -1), ln_xa_b=ln_xa_b[l].reshape(1, -1),
        rw_hi=rw_hi, rw_lo=rw_lo, rb=router_b.reshape(-1, 1),
        ln_ffn_g=ln_ffn_g[l].reshape(1, -1), ln_ffn_b=ln_ffn_b[l].reshape(1, -1),
    )


def kernel(x, mem, w_in, gla_w_a2, gla_b_a, gla_norm_g, nsa_cmp_pe, nsa_cmp_w1, nsa_cmp_w2, w_branch_gla, w_branch_nsa, w_out, ln_mix_g, ln_mix_b, xa_wq, xa_wkv, xa_wo, ln_xa_g, ln_xa_b, router_w, router_b, moe_w_in, moe_w_down, ln_ffn_g, ln_ffn_b):
    B, T, d = x.shape
    assert T % 512 == 0 and d == 2048 and mem.shape[1] == MEM_LEN
    n = B * T
    params = (w_in, gla_w_a2, gla_b_a, gla_norm_g, nsa_cmp_pe, nsa_cmp_w1, nsa_cmp_w2, w_branch_gla, w_branch_nsa,
              w_out, ln_mix_g, ln_mix_b, xa_wq, xa_wkv, xa_wo, ln_xa_g, ln_xa_b, router_w, router_b, moe_w_in,
              moe_w_down, ln_ffn_g, ln_ffn_b)
    slopes = (2.0 ** (-8.0 * jnp.arange(1, NSA_HEADS + 1, dtype=F32) / NSA_HEADS)).astype(F32)
    nc, ns = T // CMP_STRIDE, T // SEL_LEN
    cs = np.arange(nc) * CMP_STRIDE
    ss = np.arange(ns) * SEL_LEN
    ovt = ((cs[None, :] < ss[:, None] + SEL_LEN) & (cs[None, :] + CMP_LEN > ss[:, None])
           & (cs[None, :] + CMP_LEN <= T)).astype(np.float32)
    assert NSA_DH + ns <= LANES
    epad = np.zeros((T, LANES), np.float32)
    epad[np.arange(T), NSA_DH + np.arange(T) // SEL_LEN] = 1.0
    consts = (slopes, jnp.asarray(ovt, BF16), jnp.asarray(epad, BF16))

    xf = x.reshape(n, d)
    xb = xf.astype(BF16)
    mem_b = mem.reshape(B * MEM_LEN, d).astype(BF16)
    moe_w = (moe_w_in, moe_w_down)
    for l in range(DEPTH):
        p = _prep_layer(l, *params)
        xf, xb = _layer(xf, xb, mem_b, p, moe_w, l, consts, B, T)
    return xf.reshape(B, T, d)
```

```python
import functools

import jax
import jax.numpy as jnp
import numpy as np
from jax import lax
from jax.experimental import pallas as pl
from jax.experimental.pallas import tpu as pltpu

F32 = jnp.float32
BF16 = jnp.bfloat16
I32 = jnp.int32

DEPTH = 2
MEM_LEN = 256
GLA_HEADS = 4
GLA_DK = 128
GLA_DV = 256
GLA_GATE_RANK = 16
GLA_TAU = 16.0
GLA_CHUNK = 64
NSA_HEADS = 16
NSA_GROUPS = 4
NSA_HPG = NSA_HEADS // NSA_GROUPS
NSA_DH = 64
CMP_LEN = 32
CMP_STRIDE = 16
CMP_HIDDEN = 256
SEL_LEN = 64
SEL_TOPN = 8
WINDOW = 512
XA_HEADS = 4
XA_DH = 128
N_EXPERTS = 16
N_GROUPS = 4
EXPERTS_PER_GROUP = N_EXPERTS // N_GROUPS
TOP_K = 2
D_FF = 1536
DN_ALPHA = float((2 * DEPTH) ** 0.25)
LN_EPS = 1e-5
NEG = -1e30
LOG2E = 1.4426950408889634
FORCE_BONUS = 1e6

GLA_QK = GLA_HEADS * GLA_DK
GLA_V = GLA_HEADS * GLA_DV
NSA_Q = NSA_HEADS * NSA_DH
NSA_KV = NSA_GROUPS * NSA_DH

LANES = 128
VMEM_LIMIT = 56 * 1024 * 1024

COL_MG = 0
COL_GQ = 2 * 2048
COL_GK = COL_GQ + GLA_QK
COL_GV = COL_GK + GLA_QK
COL_GR = COL_GV + GLA_V
COL_KS = COL_GR + GLA_V
COL_KW = COL_KS + NSA_GROUPS * LANES
COL_END = COL_KW + NSA_GROUPS * LANES
SCOL_CK = LANES
SCOL_CV = SCOL_CK + NSA_KV
SCOL_END = SCOL_CV + NSA_KV
TROW_Q = 0
TROW_VS = NSA_Q
TROW_VW = TROW_VS + NSA_KV
TROW_END = TROW_VW + NSA_KV

MOE_BLOCK = 512
FF_TILE = 512


def _cp(sem):
    return pltpu.CompilerParams(dimension_semantics=sem, vmem_limit_bytes=VMEM_LIMIT)


def _dot(a, b):
    return jnp.dot(a, b, preferred_element_type=F32)


def _dot_nt(a, b):
    return lax.dot_general(a, b, (((1,), (1,)), ((), ())), preferred_element_type=F32)


def _dot_tn(a, b):
    return lax.dot_general(a, b, (((0,), (0,)), ((), ())), preferred_element_type=F32)


def _layer_norm(z, g, b):
    mu = jnp.mean(z, axis=-1, keepdims=True)
    zc = z - mu
    var = jnp.mean(zc * zc, axis=-1, keepdims=True)
    return zc * lax.rsqrt(var + LN_EPS) * g + b


def _mm_kernel(a_ref, b_ref, o_ref):
    o_ref[...] = _dot(a_ref[...], b_ref[...]).astype(o_ref.dtype)


def _matmul(a, b, out_dtype, tm, tn):
    m, k = a.shape
    n = b.shape[1]
    return pl.pallas_call(
        _mm_kernel,
        grid=(m // tm, n // tn),
        in_specs=[pl.BlockSpec((tm, k), lambda i, j: (i, 0)),
                  pl.BlockSpec((k, tn), lambda i, j: (0, j))],
        out_specs=pl.BlockSpec((tm, tn), lambda i, j: (i, j)),
        out_shape=jax.ShapeDtypeStruct((m, n), out_dtype),
        compiler_params=_cp(("parallel", "parallel")),
        name="matmul",
    )(a, b)


def _mm_nt_kernel(wt_ref, x_ref, o_ref):
    o_ref[0] = _dot_nt(wt_ref[...], x_ref[...]).astype(o_ref.dtype)


def _matmul_t(x, wt, B, T, tm, tr):
    n, k = x.shape
    r = wt.shape[0]
    nt = T // tm
    return pl.pallas_call(
        _mm_nt_kernel,
        grid=(n // tm, r // tr),
        in_specs=[pl.BlockSpec((tr, k), lambda i, j: (j, 0)),
                  pl.BlockSpec((tm, k), lambda i, j: (i, 0))],
        out_specs=pl.BlockSpec((1, tr, tm), lambda i, j: (i // nt, j, i % nt)),
        out_shape=jax.ShapeDtypeStruct((B, r, T), BF16),
        compiler_params=_cp(("parallel", "parallel")),
        name="matmul_t",
    )(wt, x)


def _gla_kernel(q_ref, k_ref, v_ref, r_ref, sm_ref, wa_ref, ba_ref, ng_ref, o_ref, st_ref):
    C = GLA_CHUNK
    n_chunks = q_ref.shape[0] // C
    st_ref[...] = jnp.zeros_like(st_ref)
    rowi = lax.broadcasted_iota(I32, (C, GLA_DK), 0)
    tt = lax.broadcasted_iota(I32, (C, C), 0)
    ss = lax.broadcasted_iota(I32, (C, C), 1)
    levels = (1, 2, 4, 8, 16, 32)
    pair_masks = [((tt // (2 * L)) == (ss // (2 * L))) & ((tt & L) != 0) & ((ss & L) == 0) for L in levels]
    diag_mask = tt == ss
    scale = GLA_DK ** -0.5

    def head_chunk(rows, h, z):
        qk_cols = slice(h * GLA_DK, (h + 1) * GLA_DK)
        v_cols = slice(h * GLA_DV, (h + 1) * GLA_DV)
        q = q_ref[rows, qk_cols].astype(F32) * scale
        k = k_ref[rows, qk_cols].astype(F32)
        v = v_ref[rows, v_cols]
        g = (jnp.minimum(z, 0.0) - jnp.log1p(jnp.exp(-jnp.abs(z)))) * (1.0 / GLA_TAU)
        incl = g
        tot = g
        att = jnp.where(diag_mask, _dot_nt(q.astype(BF16), k.astype(BF16)), 0.0)
        for L, pm in zip(levels, pair_masks):
            ql = (q * jnp.exp(incl)).astype(BF16)
            kl = (k * jnp.exp(tot - incl)).astype(BF16)
            att = jnp.where(pm, _dot_nt(ql, kl), att)
            upper = (rowi & L) != 0
            from_lower = pltpu.roll(tot, L, 0)
            from_upper = pltpu.roll(tot, C - L, 0)
            incl = incl + jnp.where(upper, from_lower, 0.0)
            tot = tot + jnp.where(upper, from_lower, from_upper)
        qd = (q * jnp.exp(incl)).astype(BF16)
        kd = (k * jnp.exp(tot - incl)).astype(BF16)
        st = st_ref[h]
        o = _dot_nt(qd, st.astype(BF16)) + _dot(att.astype(BF16), v)
        st_ref[h] = st * jnp.exp(tot[0:1, :]) + _dot_tn(v, kd)
        mu = jnp.mean(o, axis=-1, keepdims=True)
        oc = o - mu
        var = jnp.mean(oc * oc, axis=-1, keepdims=True)
        on = oc * lax.rsqrt(var + LN_EPS) * ng_ref[:, v_cols]
        r = r_ref[rows, v_cols].astype(F32)
        o_ref[rows, v_cols] = (on * (r * jax.nn.sigmoid(r))).astype(o_ref.dtype)

    def chunk(c, carry):
        rows = pl.ds(pl.multiple_of(c * C, C), C)
        z = _dot(sm_ref[rows, :].astype(BF16), wa_ref[...]) + ba_ref[...]
        for h in range(GLA_HEADS):
            head_chunk(rows, h, z[:, h * GLA_DK:(h + 1) * GLA_DK])
        return carry

    lax.fori_loop(0, n_chunks, chunk, 0)


def _gla(h_big, h_small, wa_pad, b_a, norm_g, B, T):
    n = B * T
    return pl.pallas_call(
        _gla_kernel,
        grid=(B,),
        in_specs=[
            pl.BlockSpec((T, GLA_QK), lambda b: (b, COL_GQ // GLA_QK)),
            pl.BlockSpec((T, GLA_QK), lambda b: (b, COL_GK // GLA_QK)),
            pl.BlockSpec((T, GLA_V), lambda b: (b, COL_GV // GLA_V)),
            pl.BlockSpec((T, GLA_V), lambda b: (b, COL_GR // GLA_V)),
            pl.BlockSpec((T, LANES), lambda b: (b, 0)),
            pl.BlockSpec((LANES, GLA_QK), lambda b: (0, 0)),
            pl.BlockSpec((1, GLA_QK), lambda b: (0, 0)),
            pl.BlockSpec((1, GLA_V), lambda b: (0, 0)),
        ],
        out_specs=pl.BlockSpec((T, GLA_V), lambda b: (b, 0)),
        out_shape=jax.ShapeDtypeStruct((n, GLA_V), BF16),
        scratch_shapes=[pltpu.VMEM((GLA_HEADS, GLA_DV, GLA_DK), F32)],
        compiler_params=_cp(("parallel",)),
        name="gla",
    )(h_big, h_big, h_big, h_big, h_small, wa_pad, b_a, norm_g)


def _compress_kernel(x_ref, w1_ref, w2_ref, w2t_ref, pe_ref, o_ref, ot_ref):
    nc = x_ref.shape[0] // CMP_STRIDE
    hid_w = w1_ref.shape[3]
    a = jnp.zeros((nc, hid_w), F32)
    bm = jnp.zeros((nc, hid_w), F32)
    c = jnp.zeros((pe_ref.shape[2], hid_w), F32)
    for l in range(CMP_STRIDE):
        xl = x_ref[pl.ds(l, nc, stride=CMP_STRIDE), :].astype(BF16)
        a = a + _dot(xl, w1_ref[0, l])
        bm = bm + _dot(xl, w1_ref[0, CMP_STRIDE + l])
    for l in range(CMP_LEN):
        c = c + _dot(pe_ref[0, l], w1_ref[0, l])
    hid = a + pltpu.roll(bm, nc - 1, 0) + c[0:1, :]
    act = jax.nn.gelu(hid).astype(BF16)
    o_ref[0, 0, 0] = _dot(act, w2_ref[0]).astype(o_ref.dtype)
    ot_ref[0, 0, 0] = _dot_nt(w2t_ref[0], act).astype(ot_ref.dtype)


def _compress(h_small, w1bd, w2bd, w2bdt, pe_pair, B, T):
    nc = T // CMP_STRIDE
    pairs = NSA_GROUPS // 2
    return pl.pallas_call(
        _compress_kernel,
        grid=(B, 2, pairs),
        in_specs=[
            pl.BlockSpec((T, LANES), lambda b, s, j: (b, SCOL_CK // LANES + s * pairs + j)),
            pl.BlockSpec((1,) + w1bd.shape[1:], lambda b, s, j: (s, 0, 0, 0)),
            pl.BlockSpec((1,) + w2bd.shape[1:], lambda b, s, j: (s, 0, 0)),
            pl.BlockSpec((1,) + w2bdt.shape[1:], lambda b, s, j: (s, 0, 0)),
            pl.BlockSpec((1,) + pe_pair.shape[1:], lambda b, s, j: (s, 0, 0, 0)),
        ],
        out_specs=[
            pl.BlockSpec((1, 1, 1, nc, LANES), lambda b, s, j: (b, s, j, 0, 0)),
            pl.BlockSpec((1, 1, 1, LANES, nc), lambda b, s, j: (b, s, j, 0, 0)),
        ],
        out_shape=[jax.ShapeDtypeStruct((B, 2, pairs, nc, LANES), BF16),
                   jax.ShapeDtypeStruct((B, 2, pairs, LANES, nc), BF16)],
        compiler_params=_cp(("parallel", "parallel", "parallel")),
        name="nsa_compress",
    )(h_small, w1bd, w2bd, w2bdt, pe_pair)


def _cmp_select_kernel(slopes_ref, qt_ref, kc_ref, vct_ref, ovt_ref, ind_ref, ocmp_ref, mb_ref, kt_ref, qpad_ref):
    g = pl.program_id(1)
    i = pl.program_id(2)
    tq = qt_ref.shape[2]
    nc = kc_ref.shape[3]
    ns = mb_ref.shape[2]
    dh = NSA_DH
    t0 = i * tq
    tpos = (t0 + lax.broadcasted_iota(I32, (nc, tq), 1))
    nidx = lax.broadcasted_iota(I32, (nc, tq), 0)
    mask_c = (nidx * CMP_STRIDE + (CMP_LEN - 1)) <= tpos
    absd = jnp.abs(tpos.astype(F32) - (nidx.astype(F32) * CMP_STRIDE + 0.5 * (CMP_LEN - 1)))
    lower = g % 2 == 0
    kc = kc_ref[0, 0, 0]
    vct = jnp.where(lower, vct_ref[0, 0, 0, 0:dh, :], vct_ref[0, 0, 0, dh:2 * dh, :])
    psum = jnp.zeros((nc, tq), F32)
    for hh in range(NSA_HPG):
        slope = slopes_ref[g * NSA_HPG + hh]
        q = qt_ref[0, hh * dh:(hh + 1) * dh, :] * jnp.asarray(dh ** -0.5, BF16)
        zero = jnp.zeros_like(q)
        qpad_ref[0:dh, :] = jnp.where(lower, q, zero)
        qpad_ref[dh:2 * dh, :] = jnp.where(lower, zero, q)
        s = _dot(kc, qpad_ref[...]) - slope * absd
        s = jnp.where(mask_c, s, NEG)
        e = jnp.exp(s - jnp.max(s, axis=0, keepdims=True))
        p = jnp.where(mask_c, e / jnp.sum(e, axis=0, keepdims=True), 0.0)
        ocmp_ref[0, hh * dh:(hh + 1) * dh, :] = _dot(vct, p.astype(BF16)).astype(ocmp_ref.dtype)
        psum = psum + p
    p_hi = psum.astype(BF16)
    p_lo = (psum - p_hi.astype(F32)).astype(BF16)
    imp = _dot(ovt_ref[...], p_hi) + _dot(ovt_ref[...], p_lo)
    j = lax.broadcasted_iota(I32, (ns, tq), 0)
    tp = t0 + lax.broadcasted_iota(I32, (ns, tq), 1)
    cur = tp // SEL_LEN
    forced = (j == 0) | (j == cur) | (j == cur - 1)
    valid = j * SEL_LEN <= tp
    score = jnp.where(valid, imp + jnp.where(forced, FORCE_BONUS, 0.0), NEG)
    rank = jnp.zeros((ns, tq), F32)
    for jp in range(ns):
        row = score[jp:jp + 1, :]
        beats = (row > score) | ((row == score) & (j > jp))
        rank = rank + jnp.where(beats, 1.0, 0.0)
    keep = valid & (rank < float(min(SEL_TOPN, ns)))
    mb_ref[0, 0] = jnp.where(keep, 0.0, NEG).astype(mb_ref.dtype)
    kt_ref[0, 0] = _dot(ind_ref[...], jnp.where(keep, 1.0, 0.0).astype(BF16))


def _cmp_select(slopes, h_t, kcmp, kcmp_t, ovt, tile_ind, B, T, tq):
    nc = T // CMP_STRIDE
    ns = T // SEL_LEN
    nkt = tile_ind.shape[0]
    grp_rows = NSA_HPG * NSA_DH
    grid_spec = pltpu.PrefetchScalarGridSpec(
        num_scalar_prefetch=1,
        grid=(B, NSA_GROUPS, T // tq),
        in_specs=[
            pl.BlockSpec((1, grp_rows, tq), lambda b, g, i, s: (b, TROW_Q // grp_rows + g, i)),
            pl.BlockSpec((1, 1, 1, nc, LANES), lambda b, g, i, s: (b, 0, g // 2, 0, 0)),
            pl.BlockSpec((1, 1, 1, LANES, nc), lambda b, g, i, s: (b, 1, g // 2, 0, 0)),
            pl.BlockSpec((ns, nc), lambda b, g, i, s: (0, 0)),
            pl.BlockSpec((nkt, ns), lambda b, g, i, s: (0, 0)),
        ],
        out_specs=[
            pl.BlockSpec((1, grp_rows, tq), lambda b, g, i, s: (b, g, i)),
            pl.BlockSpec((1, 1, ns, tq), lambda b, g, i, s: (b, g, 0, i)),
            pl.BlockSpec((1, 1, nkt, tq), lambda b, g, i, s: (b, g, 0, i)),
        ],
        scratch_shapes=[pltpu.VMEM((LANES, tq), BF16)],
    )
    return pl.pallas_call(
        _cmp_select_kernel,
        grid_spec=grid_spec,
        out_shape=[jax.ShapeDtypeStruct((B, NSA_Q, T), BF16),
                   jax.ShapeDtypeStruct((B, NSA_GROUPS, ns, T), BF16),
                   jax.ShapeDtypeStruct((B, NSA_GROUPS, nkt, T), F32)],
        compiler_params=_cp(("parallel", "parallel", "parallel")),
        name="nsa_cmp_select",
    )(slopes, h_t, kcmp, kcmp_t, ovt, tile_ind)


def _sel_win_kernel(slopes_ref, flags_ref, qt_ref, ks_ref, kw_ref, vs_ref, vw_ref, epad_ref, mb_ref, ocmp_ref, gt_ref,
                    o_ref, qaug_ref, m_ref, acc_ref, srow_ref, bias_ref, s_ref, p_ref, alpha_ref,
                    kall_ref, vall_ref, tiles_ref, *, n_tiles):
    g = pl.program_id(1)
    i = pl.program_id(2)
    tq = qt_ref.shape[2]
    tk = tq
    ns = mb_ref.shape[2]
    dh = NSA_DH
    wide = NSA_HPG * tq
    t0 = i * tq
    BIG = -NEG

    @pl.when(i == 0)
    def _():
        srow = jnp.concatenate([jnp.full((1, tq), slopes_ref[g * NSA_HPG + hh] * LOG2E, F32)
                                for hh in range(NSA_HPG)], axis=1)
        srow_ref[...] = srow
        lane = lax.broadcasted_iota(I32, (tk, wide), 1) & (tq - 1)
        dist0 = (lane - lax.broadcasted_iota(I32, (tk, wide), 0)).astype(F32)
        sd0 = srow * dist0
        bias_ref[0] = sd0
        bias_ref[1] = sd0 + jnp.where(dist0 >= 0.0, 0.0, BIG)
        bias_ref[2] = sd0 + jnp.where(dist0 < 0.0, 0.0, BIG)
        bias_ref[3] = jnp.full((tk, wide), BIG, F32)
        kall_ref[0] = ks_ref[...] + epad_ref[...]
        kall_ref[1] = kw_ref[...]
        extra = jnp.where(lax.broadcasted_iota(I32, (vall_ref.shape[1] - dh, vall_ref.shape[2]), 0) == 0, 1.0, 0.0)
        vall_ref[0, 0:dh, :] = vs_ref[0]
        vall_ref[1, 0:dh, :] = vw_ref[0]
        vall_ref[0, dh:, :] = extra.astype(BF16)
        vall_ref[1, dh:, :] = extra.astype(BF16)

    for hh in range(NSA_HPG):
        cols = slice(hh * tq, (hh + 1) * tq)
        q = qt_ref[0, hh * dh:(hh + 1) * dh, :].astype(F32) * (dh ** -0.5 * LOG2E)
        qaug_ref[0:dh, cols] = q.astype(BF16)
        qaug_ref[dh:dh + ns, cols] = mb_ref[0, 0]
        qaug_ref[dh + ns:, cols] = jnp.zeros((qaug_ref.shape[0] - dh - ns, tq), BF16)

    m_ref[...] = jnp.full(m_ref.shape, NEG, F32)
    acc_ref[...] = jnp.zeros(acc_ref.shape, F32)

    n_back = WINDOW // tk
    flag_base = ((pl.program_id(0) * NSA_GROUPS + g) * n_tiles + i) * n_tiles
    n_sel = jnp.int32(0)
    for kb_static in range(n_tiles - 1):
        active = (kb_static < i) & (flags_ref[flag_base + kb_static] != 0)

        @pl.when(active)
        def _(kb_static=kb_static, n_sel=n_sel):
            tiles_ref[n_sel] = kb_static

        n_sel = n_sel + active.astype(I32)
    tiles_ref[n_sel] = i
    n_sel = n_sel + 1
    n_win = jnp.minimum(i, n_back) + 1
    n_steps = n_sel + n_win

    def describe(n):
        n = jnp.maximum(n, 0)
        is_win = n >= n_sel
        kb_sel = tiles_ref[jnp.minimum(n, n_sel - 1)]
        kb = jnp.clip(jnp.where(is_win, i - n_win + 1 + (n - n_sel), kb_sel), 0, i)
        mode = jnp.where(kb == i, 1, jnp.where(is_win & (kb == i - n_back), 2, 0))
        mode = jnp.where(n >= n_steps, 3, mode)
        return is_win.astype(I32), kb, mode

    def scores(n, slot):
        br, kb, _ = describe(n)
        s0 = pl.multiple_of(kb * tk, tk)
        s_ref[slot] = _dot(kall_ref[br, pl.ds(s0, tk), :], qaug_ref[...])

    def softmax(n, slot):
        br, kb, mode = describe(n)
        crow = srow_ref[...] * ((i - kb) * tk).astype(F32)
        s = s_ref[slot] - bias_ref[mode]
        m_old = m_ref[br]
        m_new = jnp.maximum(m_old, jnp.max(s, axis=0, keepdims=True) - crow)
        alpha = jnp.exp2(m_old - m_new)
        p = jnp.exp2(s - (m_new + crow))
        m_ref[br] = m_new
        alpha_ref[slot] = alpha
        p_ref[slot] = p.astype(BF16)

    def weighted_values(n, slot):
        br, kb, _ = describe(n)
        s0 = pl.multiple_of(kb * tk, tk)
        acc_ref[br] = alpha_ref[slot] * acc_ref[br] + _dot(vall_ref[br, :, pl.ds(s0, tk)], p_ref[slot])

    p_ref[1] = jnp.zeros(p_ref.shape[1:], BF16)
    alpha_ref[1] = jnp.ones(alpha_ref.shape[1:], F32)
    scores(0, 0)

    def pair(j, carry):
        n = 2 * j
        scores(n + 1, 1)
        softmax(n, 0)
        weighted_values(n - 1, 1)
        scores(n + 2, 0)
        softmax(n + 1, 1)
        weighted_values(n, 0)
        return carry

    n_pairs = (n_steps + 1) // 2
    lax.fori_loop(0, n_pairs, pair, 0)
    weighted_values(2 * n_pairs - 1, 1)

    def gate_row(branch):
        rows = [gt_ref[0, 0, 3 * hh + branch:3 * hh + branch + 1, :] for hh in range(NSA_HPG)]
        return jax.nn.sigmoid(jnp.concatenate(rows, axis=1))

    o = (gate_row(1) * (acc_ref[0, 0:dh, :] / acc_ref[0, dh:dh + 1, :])
         + gate_row(2) * (acc_ref[1, 0:dh, :] / acc_ref[1, dh:dh + 1, :]))
    ocmp = jnp.concatenate([ocmp_ref[0, hh * dh:(hh + 1) * dh, :] for hh in range(NSA_HPG)], axis=1).astype(F32)
    o = o + gate_row(0) * ocmp
    o_heads = jnp.concatenate([o[:, hh * tq:(hh + 1) * tq] for hh in range(NSA_HPG)], axis=0)
    o_ref[...] = o_heads.T.astype(o_ref.dtype)


def _sel_win(slopes, tile_flags, h_t, h_big, epad, mb, ocmp_t, gates_t, B, T, tq):
    ns = T // SEL_LEN
    kaug = LANES
    G = NSA_GROUPS
    grp_rows = NSA_HPG * NSA_DH
    nq = T // tq
    grid_spec = pltpu.PrefetchScalarGridSpec(
        num_scalar_prefetch=2,
        grid=(B, G, nq),
        in_specs=[
            pl.BlockSpec((1, grp_rows, tq), lambda b, g, i, s, f: (b, TROW_Q // grp_rows + g, i)),
            pl.BlockSpec((T, LANES), lambda b, g, i, s, f: (b, COL_KS // LANES + g)),
            pl.BlockSpec((T, LANES), lambda b, g, i, s, f: (b, COL_KW // LANES + g)),
            pl.BlockSpec((1, NSA_DH, T), lambda b, g, i, s, f: (b, TROW_VS // NSA_DH + g, 0)),
            pl.BlockSpec((1, NSA_DH, T), lambda b, g, i, s, f: (b, TROW_VW // NSA_DH + g, 0)),
            pl.BlockSpec((T, LANES), lambda b, g, i, s, f: (0, 0)),
            pl.BlockSpec((1, 1, ns, tq), lambda b, g, i, s, f: (b, g, 0, i)),
            pl.BlockSpec((1, grp_rows, tq), lambda b, g, i, s, f: (b, g, i)),
            pl.BlockSpec((1, 1, 16, tq), lambda b, g, i, s, f: (b, g, 0, i)),
        ],
        out_specs=pl.BlockSpec((tq, grp_rows), lambda b, g, i, s, f: (b * nq + i, g)),
        scratch_shapes=[
            pltpu.VMEM((kaug, NSA_HPG * tq), BF16),
            pltpu.VMEM((2, 1, NSA_HPG * tq), F32),
            pltpu.VMEM((2, NSA_DH + 16, NSA_HPG * tq), F32),
            pltpu.VMEM((1, NSA_HPG * tq), F32),
            pltpu.VMEM((4, tq, NSA_HPG * tq), F32),
            pltpu.VMEM((2, tq, NSA_HPG * tq), F32),
            pltpu.VMEM((2, tq, NSA_HPG * tq), BF16),
            pltpu.VMEM((2, 1, NSA_HPG * tq), F32),
            pltpu.VMEM((2, T, kaug), BF16),
            pltpu.VMEM((2, NSA_DH + 16, T), BF16),
            pltpu.SMEM((nq,), I32),
        ],
    )
    return pl.pallas_call(
        functools.partial(_sel_win_kernel, n_tiles=nq),
        grid_spec=grid_spec,
        out_shape=jax.ShapeDtypeStruct((B * T, NSA_Q), BF16),
        compiler_params=_cp(("parallel", "parallel", "arbitrary")),
        name="nsa_sel_win",
    )(slopes, tile_flags, h_t, h_big, h_big, h_t, h_t, epad, mb, ocmp_t, gates_t)


def _mix_kernel(og_ref, on_ref, mg1_ref, mg2_ref, x_ref, wg_ref, wn_ref, wo_ref, lg_ref, lb_ref,
                x1_ref, x1b_ref):
    g1 = _dot(og_ref[...], wg_ref[...])
    g2 = _dot(on_ref[...], wn_ref[...])
    merged = (jax.nn.sigmoid(mg1_ref[...].astype(F32)) * g1
              + jax.nn.sigmoid(mg2_ref[...].astype(F32)) * g2)
    y = _dot(merged.astype(BF16), wo_ref[...])
    x1 = _layer_norm(DN_ALPHA * x_ref[...] + y, lg_ref[...], lb_ref[...])
    x1_ref[...] = x1
    x1b_ref[...] = x1.astype(BF16)


def _const_spec(shape):
    nd = len(shape)
    return pl.BlockSpec(shape, lambda *_: (0,) * nd, pipeline_mode=pl.Buffered(1))


def _mix(o_gla, o_nsa, h_big, x, wg, wn, wo, lg, lb, tm):
    n, d = x.shape
    return pl.pallas_call(
        _mix_kernel,
        grid=(n // tm,),
        in_specs=[
            pl.BlockSpec((tm, GLA_V), lambda i: (i, 0)),
            pl.BlockSpec((tm, NSA_Q), lambda i: (i, 0)),
            pl.BlockSpec((tm, d), lambda i: (i, 0)),
            pl.BlockSpec((tm, d), lambda i: (i, 1)),
            pl.BlockSpec((tm, d), lambda i: (i, 0)),
            _const_spec(wg.shape), _const_spec(wn.shape), _const_spec(wo.shape),
            _const_spec(lg.shape), _const_spec(lb.shape),
        ],
        out_specs=[pl.BlockSpec((tm, d), lambda i: (i, 0)), pl.BlockSpec((tm, d), lambda i: (i, 0))],
        out_shape=[jax.ShapeDtypeStruct((n, d), F32), jax.ShapeDtypeStruct((n, d), BF16)],
        compiler_params=_cp(("parallel",)),
        name="mix_ln",
    )(o_gla, o_nsa, h_big, h_big, x, wg, wn, wo, lg, lb)


def _xattn_kernel(x_ref, xb_ref, kv_ref, wq_ref, wo_ref, lg_ref, lb_ref, x2_ref):
    hd = XA_HEADS * XA_DH
    q = (_dot(xb_ref[...], wq_ref[...]) * (XA_DH ** -0.5)).astype(BF16)
    outs = []
    for h in range(XA_HEADS):
        kh = kv_ref[0, :, h * XA_DH:(h + 1) * XA_DH]
        vh = kv_ref[0, :, hd + h * XA_DH:hd + (h + 1) * XA_DH]
        s = _dot_nt(q[:, h * XA_DH:(h + 1) * XA_DH], kh)
        e = jnp.exp(s - jnp.max(s, axis=-1, keepdims=True))
        p = e / jnp.sum(e, axis=-1, keepdims=True)
        outs.append(_dot(p.astype(BF16), vh).astype(BF16))
    o = jnp.concatenate(outs, axis=-1)
    y = _dot(o, wo_ref[...])
    x2_ref[...] = _layer_norm(DN_ALPHA * x_ref[...] + y, lg_ref[...], lb_ref[...])


def _xattn(x1, x1b, kv, wq, wo, lg, lb, B, T, tm):
    n, d = x1.shape
    nt = T // tm
    return pl.pallas_call(
        _xattn_kernel,
        grid=(B, nt),
        in_specs=[
            pl.BlockSpec((tm, d), lambda b, i: (b * nt + i, 0)),
            pl.BlockSpec((tm, d), lambda b, i: (b * nt + i, 0)),
            pl.BlockSpec((1,) + kv.shape[1:], lambda b, i: (b, 0, 0)),
            _const_spec(wq.shape), _const_spec(wo.shape), _const_spec(lg.shape), _const_spec(lb.shape),
        ],
        out_specs=pl.BlockSpec((tm, d), lambda b, i: (b * nt + i, 0)),
        out_shape=jax.ShapeDtypeStruct((n, d), F32),
        compiler_params=_cp(("parallel", "parallel")),
        name="xattn_ln",
    )(x1, x1b, kv, wq, wo, lg, lb)


def _router_kernel(x_ref, wh_ref, wl_ref, rb_ref, e_ref, gate_ref, rank_ref, cnt_ref, carry_ref):
    i = pl.program_id(0)
    tr = x_ref.shape[0]
    E = N_EXPERTS

    @pl.when(i == 0)
    def _():
        carry_ref[...] = jnp.zeros_like(carry_ref)

    x = x_ref[...]
    x_hi = x.astype(BF16)
    x_lo = (x - x_hi.astype(F32)).astype(BF16)
    wh = wh_ref[...]
    logits = _dot_nt(wh, x_hi) + _dot_nt(wh, x_lo) + _dot_nt(wl_ref[...], x_hi)
    biased = logits + rb_ref[...]
    rows = [biased[e:e + 1, :] for e in range(E)]
    raw = [logits[e:e + 1, :] for e in range(E)]
    best_score = None
    best = None
    for gi in range(N_GROUPS):
        v = rows[gi * EXPERTS_PER_GROUP:(gi + 1) * EXPERTS_PER_GROUP]
        sc = None
        for a in range(EXPERTS_PER_GROUP):
            for b in range(a + 1, EXPERTS_PER_GROUP):
                pair = v[a] + v[b]
                sc = pair if sc is None else jnp.maximum(sc, pair)
        if best is None:
            best_score, best = sc, jnp.zeros((1, tr), I32)
        else:
            better = sc > best_score
            best_score = jnp.where(better, sc, best_score)
            best = jnp.where(better, gi, best)

    def pick(vals):
        out = vals[0:EXPERTS_PER_GROUP]
        for gi in range(1, N_GROUPS):
            out = [jnp.where(best == gi, vals[gi * EXPERTS_PER_GROUP + a], out[a]) for a in range(EXPERTS_PER_GROUP)]
        return out

    w = pick(rows)
    lraw = pick(raw)
    i1 = jnp.zeros((1, tr), I32)
    v1 = w[0]
    l1 = lraw[0]
    for a in range(1, EXPERTS_PER_GROUP):
        better = w[a] > v1
        v1 = jnp.where(better, w[a], v1)
        l1 = jnp.where(better, lraw[a], l1)
        i1 = jnp.where(better, a, i1)
    i2 = jnp.full((1, tr), -1, I32)
    v2 = jnp.full((1, tr), -jnp.inf, F32)
    l2 = jnp.zeros((1, tr), F32)
    for a in range(EXPERTS_PER_GROUP):
        better = (i1 != a) & ((w[a] > v2) | (i2 < 0))
        v2 = jnp.where(better, w[a], v2)
        l2 = jnp.where(better, lraw[a], l2)
        i2 = jnp.where(better, a, i2)
    e1 = best * EXPERTS_PER_GROUP + i1
    e2 = best * EXPERTS_PER_GROUP + i2
    mx = jnp.maximum(l1, l2)
    p1 = jnp.exp(l1 - mx)
    p2 = jnp.exp(l2 - mx)
    den = p1 + p2
    e_ref[0:1, :] = e1
    e_ref[1:2, :] = e2
    gate_ref[0:1, :] = p1 / den
    gate_ref[1:2, :] = p2 / den
    eidx = lax.broadcasted_iota(I32, (E, tr), 0)
    is1 = eidx == e1
    is2 = eidx == e2
    member = jnp.where(is1 | is2, 1.0, 0.0)
    uu = lax.broadcasted_iota(I32, (tr, tr), 0)
    tt = lax.broadcasted_iota(I32, (tr, tr), 1)
    tri = jnp.where(uu <= tt, 1.0, 0.0).astype(BF16)
    incl = _dot(member.astype(BF16), tri)
    excl = carry_ref[:, 0:1] + incl - member
    rank_ref[0:1, :] = jnp.sum(jnp.where(is1, excl, 0.0), axis=0, keepdims=True).astype(I32)
    rank_ref[1:2, :] = jnp.sum(jnp.where(is2, excl, 0.0), axis=0, keepdims=True).astype(I32)
    new_carry = carry_ref[...] + jnp.sum(member, axis=1, keepdims=True)
    carry_ref[...] = new_carry
    cnt_ref[...] = new_carry


def _router(x2, rw_hi, rw_lo, rb, tr):
    n, d = x2.shape
    E = N_EXPERTS
    return pl.pallas_call(
        _router_kernel,
        grid=(n // tr,),
        in_specs=[
            pl.BlockSpec((tr, d), lambda i: (i, 0)),
            pl.BlockSpec((E, d), lambda i: (0, 0)),
            pl.BlockSpec((E, d), lambda i: (0, 0)),
            pl.BlockSpec((E, 1), lambda i: (0, 0)),
        ],
        out_specs=[
            pl.BlockSpec((2, tr), lambda i: (0, i)),
            pl.BlockSpec((2, tr), lambda i: (0, i)),
            pl.BlockSpec((2, tr), lambda i: (0, i)),
            pl.BlockSpec((E, LANES), lambda i: (0, 0)),
        ],
        out_shape=[jax.ShapeDtypeStruct((2, n), I32), jax.ShapeDtypeStruct((2, n), F32),
                   jax.ShapeDtypeStruct((2, n), I32), jax.ShapeDtypeStruct((E, LANES), F32)],
        scratch_shapes=[pltpu.VMEM((E, LANES), F32)],
        compiler_params=_cp(("arbitrary",)),
        name="moe_router",
    )(x2, rw_hi, rw_lo, rb)


def _slot_kernel(ps_ref, e_ref, rank_ref, slot_ref):
    e = e_ref[...]
    start = jnp.zeros(e.shape, I32)
    for ex in range(N_EXPERTS):
        start = jnp.where(e == ex, ps_ref[ex], start)
    slot_ref[...] = start + rank_ref[...]


def _slots(pad_start, e, rank, ts):
    n = e.shape[1]
    grid_spec = pltpu.PrefetchScalarGridSpec(
        num_scalar_prefetch=1,
        grid=(n // ts,),
        in_specs=[pl.BlockSpec((TOP_K, ts), lambda i, s: (0, i)), pl.BlockSpec((TOP_K, ts), lambda i, s: (0, i))],
        out_specs=pl.BlockSpec((TOP_K, ts), lambda i, s: (0, i)),
    )
    return pl.pallas_call(
        _slot_kernel,
        grid_spec=grid_spec,
        out_shape=jax.ShapeDtypeStruct((TOP_K, n), I32),
        compiler_params=_cp(("parallel",)),
        name="moe_slots",
    )(pad_start, e, rank)


def _dispatch_kernel(pe_ref, s0_ref, s1_ref, x_ref, buf_hbm, zero_ref, sem):
    td = s0_ref.shape[0]
    slots = (s0_ref, s1_ref)

    @pl.when(pl.program_id(0) == 0)
    def _():
        zero_ref[...] = jnp.zeros_like(zero_ref)

        def zero_copy(ex):
            last = pl.multiple_of(jnp.maximum(pe_ref[ex] - MOE_BLOCK, 0), MOE_BLOCK)
            return pltpu.make_async_copy(zero_ref, buf_hbm.at[pl.ds(last, MOE_BLOCK), :], sem)

        def nonempty(ex):
            return pe_ref[ex] > (pe_ref[ex - 1] if ex > 0 else 0)

        n_blocks = buf_hbm.shape[0] // MOE_BLOCK
        first_unused = pe_ref[N_EXPERTS - 1] // MOE_BLOCK

        def tail_copy(k):
            row = pl.multiple_of((first_unused + k) * MOE_BLOCK, MOE_BLOCK)
            return pltpu.make_async_copy(zero_ref, buf_hbm.at[pl.ds(row, MOE_BLOCK), :], sem)

        for ex in range(N_EXPERTS):
            pl.when(nonempty(ex))(lambda ex=ex: zero_copy(ex).start())
            pl.when(first_unused + ex < n_blocks)(lambda ex=ex: tail_copy(ex).start())
        for ex in range(N_EXPERTS):
            pl.when(nonempty(ex))(lambda ex=ex: zero_copy(ex).wait())
            pl.when(first_unused + ex < n_blocks)(lambda ex=ex: tail_copy(ex).wait())

    def issue(t, carry):
        for kk in range(TOP_K):
            dest = slots[kk][t]
            pltpu.make_async_copy(x_ref.at[pl.ds(t, 1), :], buf_hbm.at[pl.ds(dest, 1), :], sem).start()
        return carry

    lax.fori_loop(0, td, issue, 0, unroll=8)
    for kk in range(TOP_K):
        pltpu.make_async_copy(x_ref, buf_hbm.at[pl.ds(0, td), :], sem).wait()


def _dispatch(pad_end, slot0, slot1, x2, n_rows, td):
    n, d = x2.shape
    grid_spec = pltpu.PrefetchScalarGridSpec(
        num_scalar_prefetch=1,
        grid=(n // td,),
        in_specs=[
            pl.BlockSpec((td,), lambda i, s: (i,), memory_space=pltpu.SMEM),
            pl.BlockSpec((td,), lambda i, s: (i,), memory_space=pltpu.SMEM),
            pl.BlockSpec((td, d), lambda i, s: (i, 0)),
        ],
        out_specs=pl.BlockSpec(memory_space=pl.ANY),
        scratch_shapes=[pltpu.VMEM((MOE_BLOCK, d), F32), pltpu.SemaphoreType.DMA(())],
    )
    return pl.pallas_call(
        _dispatch_kernel,
        grid_spec=grid_spec,
        out_shape=jax.ShapeDtypeStruct((n_rows, d), F32),
        compiler_params=_cp(("arbitrary",)),
        name="moe_dispatch",
    )(pad_end, slot0, slot1, x2)


def _expert_kernel(be_ref, nb_ref, x_ref, win_hbm, wdn_hbm, y_ref, xb_ref, wa_s, wu_s, wd_s, sa, su, sd, sems,
                   *, layer):
    b = pl.program_id(0)
    nf = D_FF // FF_TILE
    n_used = nb_ref[0]
    e = be_ref[b]
    e_prev = be_ref[jnp.maximum(b - 1, 0)]
    e_next = be_ref[jnp.minimum(b + 1, pl.num_programs(0) - 1)]
    active = b < n_used
    is_first = active & ((b == 0) | (e_prev != e))
    feeds_next = active & (b + 1 < n_used) & (e_next != e)

    def tile_copies(ex, f):
        lo = f * FF_TILE
        return (pltpu.make_async_copy(win_hbm.at[layer, ex, :, pl.ds(lo, FF_TILE)], sa, sems.at[0]),
                pltpu.make_async_copy(win_hbm.at[layer, ex, :, pl.ds(D_FF + lo, FF_TILE)], su, sems.at[1]),
                pltpu.make_async_copy(wdn_hbm.at[layer, ex, pl.ds(lo, FF_TILE), :], sd, sems.at[2]))

    def start(ex, f):
        for c in tile_copies(ex, f):
            c.start()

    def finish(ex, f):
        for c in tile_copies(ex, f):
            c.wait()
        wa_s[f] = sa[...].astype(BF16)
        wu_s[f] = su[...].astype(BF16)
        wd_s[f] = sd[...].astype(BF16)

    @pl.when(b == 0)
    def _():
        for f in range(nf - 1):
            start(e, f)
            finish(e, f)
        start(e, nf - 1)

    @pl.when(jnp.logical_not(active))
    def _():
        y_ref[...] = jnp.zeros_like(y_ref)

    @pl.when(active)
    def _():
        xb_ref[...] = x_ref[...].astype(BF16)
        for f in range(nf):
            xb = xb_ref[...]
            a = _dot(xb, wa_s[f])
            u = _dot(xb, wu_s[f])
            act = (a * jax.nn.sigmoid(a) * u).astype(BF16)
            y = _dot(act, wd_s[f])
            if f == 0:
                y_ref[...] = y
                pl.when(is_first)(lambda: finish(e, nf - 1))
            else:
                y_ref[...] += y

            @pl.when(feeds_next)
            def _(f=f):
                if f >= 1:
                    finish(e_next, f - 1)
                start(e_next, f)


def _experts(blk_expert, n_used, buf, w_in, w_down, layer):
    p, d = buf.shape
    nb = p // MOE_BLOCK
    nf = D_FF // FF_TILE
    grid_spec = pltpu.PrefetchScalarGridSpec(
        num_scalar_prefetch=2,
        grid=(nb,),
        in_specs=[
            pl.BlockSpec((MOE_BLOCK, d), lambda b, be, nu: (jnp.minimum(b, nu[0] - 1), 0)),
            pl.BlockSpec(memory_space=pl.ANY),
            pl.BlockSpec(memory_space=pl.ANY),
        ],
        out_specs=pl.BlockSpec((MOE_BLOCK, d), lambda b, be, nu: (b, 0)),
        scratch_shapes=[
            pltpu.VMEM((MOE_BLOCK, d), BF16),
            pltpu.VMEM((nf, d, FF_TILE), BF16),
            pltpu.VMEM((nf, d, FF_TILE), BF16),
            pltpu.VMEM((nf, FF_TILE, d), BF16),
            pltpu.VMEM((d, FF_TILE), F32),
            pltpu.VMEM((d, FF_TILE), F32),
            pltpu.VMEM((FF_TILE, d), F32),
            pltpu.SemaphoreType.DMA((3,)),
        ],
    )
    return pl.pallas_call(
        functools.partial(_expert_kernel, layer=layer),
        grid_spec=grid_spec,
        out_shape=jax.ShapeDtypeStruct((p, d), F32),
        compiler_params=_cp(("arbitrary",)),
        name="moe_experts",
    )(blk_expert, n_used, buf, w_in, w_down)


def _combine_kernel(s0_ref, s1_ref, y_hbm, x_ref, gate_ref, lg_ref, lb_ref, x3_ref, x3b_ref,
                    y0_ref, y1_ref, sem):
    tc = x_ref.shape[0]
    bufs = (y0_ref, y1_ref)
    slots = (s0_ref, s1_ref)

    def issue(t, carry):
        for kk in range(TOP_K):
            src = slots[kk][t]
            pltpu.make_async_copy(y_hbm.at[pl.ds(src, 1), :], bufs[kk].at[pl.ds(t, 1), :], sem).start()
        return carry

    lax.fori_loop(0, tc, issue, 0, unroll=8)
    for kk in range(TOP_K):
        pltpu.make_async_copy(y_hbm.at[pl.ds(0, tc), :], bufs[kk], sem).wait()
    gate = gate_ref[...]
    z = DN_ALPHA * x_ref[...] + gate[:, 0:1] * y0_ref[...] + gate[:, 1:2] * y1_ref[...]
    x3 = _layer_norm(z, lg_ref[...], lb_ref[...])
    x3_ref[...] = x3
    x3b_ref[...] = x3.astype(BF16)


def _combine(slot0, slot1, y, x2, gate_nt, lg, lb, tc):
    n, d = x2.shape
    return pl.pallas_call(
        _combine_kernel,
        grid=(n // tc,),
        in_specs=[
            pl.BlockSpec((tc,), lambda i: (i,), memory_space=pltpu.SMEM),
            pl.BlockSpec((tc,), lambda i: (i,), memory_space=pltpu.SMEM),
            pl.BlockSpec(memory_space=pl.ANY),
            pl.BlockSpec((tc, d), lambda i: (i, 0)),
            pl.BlockSpec((tc, 2), lambda i: (i, 0)),
            pl.BlockSpec((1, d), lambda i: (0, 0)),
            pl.BlockSpec((1, d), lambda i: (0, 0)),
        ],
        out_specs=[pl.BlockSpec((tc, d), lambda i: (i, 0)), pl.BlockSpec((tc, d), lambda i: (i, 0))],
        out_shape=[jax.ShapeDtypeStruct((n, d), F32), jax.ShapeDtypeStruct((n, d), BF16)],
        scratch_shapes=[pltpu.VMEM((tc, d), F32), pltpu.VMEM((tc, d), F32), pltpu.SemaphoreType.DMA(())],
        compiler_params=_cp(("arbitrary",)),
        name="moe_combine_ln",
    )(slot0, slot1, y, x2, gate_nt, lg, lb)


def _layer(x, xb, mem_b, p, moe_w, layer, consts, B, T):
    n, d = x.shape
    G, HPG, DH = NSA_GROUPS, NSA_HPG, NSA_DH
    slopes, ovt, epad, tile_ind = consts

    h_big = _matmul(xb, p["w_big"], BF16, 1024, 1024)
    h_small = _matmul(xb, p["w_small"], F32, 1024, SCOL_END)
    h_t = _matmul_t(xb, p["w_t"], B, T, 1024, TROW_END // 2)

    o_gla = _gla(h_big, h_small, p["wa_pad"], p["b_a"], p["norm_g"], B, T)

    kcmp, kcmp_t = _compress(h_small, p["cmp_w1bd"], p["cmp_w2bd"], p["cmp_w2bdt"], p["cmp_pe_pair"], B, T)
    tq_sel = 256
    ocmp_t, mb, in_tile = _cmp_select(slopes, h_t, kcmp, kcmp_t, ovt, tile_ind, B, T, 512)
    nq = T // tq_sel
    tile_flags = (in_tile.reshape(B, G, nq, nq, tq_sel).max(axis=-1) > 0).astype(I32)
    tile_flags = tile_flags.transpose(0, 1, 3, 2).reshape(-1)
    gates_t = h_small[:, GLA_GATE_RANK:GLA_GATE_RANK + 3 * NSA_HEADS].reshape(B, T, G, 3 * HPG)
    gates_t = jnp.pad(gates_t.transpose(0, 2, 3, 1), ((0, 0), (0, 0), (0, 16 - 3 * HPG), (0, 0)))
    o_nsa = _sel_win(slopes, tile_flags, h_t, h_big, epad, mb, ocmp_t, gates_t, B, T, tq_sel)

    x1, x1b = _mix(o_gla, o_nsa, h_big, x, p["w_bg"], p["w_bn"], p["w_out"], p["ln_mix_g"], p["ln_mix_b"], 512)

    kvm = _matmul(mem_b, p["xa_wkv"], BF16, 512, 512).reshape(B, MEM_LEN, 2 * XA_HEADS * XA_DH)
    x2 = _xattn(x1, x1b, kvm, p["xa_wq"], p["xa_wo"], p["ln_xa_g"], p["ln_xa_b"], B, T, 512)

    e, gate, rank, cnt = _router(x2, p["rw_hi"], p["rw_lo"], p["rb"], 512)
    counts = cnt[:, 0].astype(I32)
    padded = (counts + MOE_BLOCK - 1) // MOE_BLOCK * MOE_BLOCK
    pad_end = jnp.cumsum(padded)
    pad_start = (pad_end - padded).astype(I32)
    nb = (n * TOP_K) // MOE_BLOCK + N_EXPERTS
    n_used = (pad_end[-1] // MOE_BLOCK).astype(I32).reshape(1)
    blk_start = jnp.arange(nb, dtype=I32) * MOE_BLOCK
    blk_expert = jnp.minimum(jnp.sum(blk_start[:, None] >= pad_end[None, :], axis=1), N_EXPERTS - 1).astype(I32)
    blk_expert = jnp.where(jnp.arange(nb) < n_used[0], blk_expert, blk_expert[jnp.maximum(n_used[0] - 1, 0)])
    slot = _slots(pad_start, e, rank, 2048)
    buf = _dispatch(pad_end.astype(I32), slot[0], slot[1], x2, nb * MOE_BLOCK, 512)
    y = _experts(blk_expert, n_used, buf, moe_w[0], moe_w[1], layer)
    x3, x3b = _combine(slot[0], slot[1], y, x2, gate.T, p["ln_ffn_g"], p["ln_ffn_b"], 256)
    return x3, x3b


def _prep_layer(l, w_in, gla_w_a2, gla_b_a, gla_norm_g, nsa_cmp_pe, nsa_cmp_w1, nsa_cmp_w2, w_branch_gla,
                w_branch_nsa, w_out, ln_mix_g, ln_mix_b, xa_wq, xa_wkv, xa_wo, ln_xa_g, ln_xa_b, router_w,
                router_b, moe_w_in, moe_w_down, ln_ffn_g, ln_ffn_b):
    d = w_in.shape[1]
    w = w_in[l]
    o_gq, o_gk, o_gv, o_gr = 0, GLA_QK, 2 * GLA_QK, 2 * GLA_QK + GLA_V
    o_ga = o_gr + GLA_V
    o_nq = o_ga + GLA_GATE_RANK
    o_nkv = o_nq + NSA_Q
    o_ng = o_nkv + 6 * NSA_KV
    o_mg = o_ng + 3 * NSA_HEADS
    G, DH = NSA_GROUPS, NSA_DH

    def kv_cols(kind):
        return w[:, o_nkv + kind * NSA_KV:o_nkv + (kind + 1) * NSA_KV]

    def slabs(wk):
        return jnp.pad(wk.reshape(d, G, DH), ((0, 0), (0, 0), (0, LANES - DH))).reshape(d, G * LANES)

    w_big = jnp.concatenate([w[:, o_mg:o_mg + 2 * d], w[:, o_gq:o_ga], slabs(kv_cols(2)), slabs(kv_cols(4))],
                            axis=1).astype(BF16)
    w_small = jnp.concatenate([w[:, o_ga:o_nq], w[:, o_ng:o_mg],
                               jnp.zeros((d, LANES - GLA_GATE_RANK - 3 * NSA_HEADS), F32),
                               kv_cols(0), kv_cols(1)], axis=1).astype(BF16)
    w_t = jnp.concatenate([w[:, o_nq:o_nkv], kv_cols(3), kv_cols(5)], axis=1).T.astype(BF16)
    w1 = nsa_cmp_w1[l].reshape(2, CMP_LEN, DH, CMP_HIDDEN)
    z1 = jnp.zeros_like(w1)
    w1bd = jnp.concatenate([jnp.concatenate([w1, z1], axis=3), jnp.concatenate([z1, w1], axis=3)], axis=2)
    w2 = nsa_cmp_w2[l]
    z2 = jnp.zeros_like(w2)
    w2bd = jnp.concatenate([jnp.concatenate([w2, z2], axis=2), jnp.concatenate([z2, w2], axis=2)], axis=1)
    pe = nsa_cmp_pe[l]
    pe_pair = jnp.broadcast_to(jnp.concatenate([pe, pe], axis=-1)[:, :, None, :], (2, CMP_LEN, 16, 2 * DH))
    wa_pad = jnp.concatenate([gla_w_a2[l], jnp.zeros((LANES - GLA_GATE_RANK, GLA_QK), F32)], axis=0).astype(BF16)
    rw_t = router_w.T
    rw_hi = rw_t.astype(BF16)
    rw_lo = (rw_t - rw_hi.astype(F32)).astype(BF16)
    return dict(
        w_big=w_big, w_small=w_small, w_t=w_t, wa_pad=wa_pad,
        b_a=gla_b_a[l].reshape(1, -1), norm_g=gla_norm_g[l].reshape(1, -1),
        cmp_w1bd=w1bd.astype(BF16), cmp_w2bd=w2bd.astype(BF16), cmp_w2bdt=w2bd.transpose(0, 2, 1).astype(BF16),
        cmp_pe_pair=pe_pair.astype(BF16),
        w_bg=w_branch_gla[l].astype(BF16), w_bn=w_branch_nsa[l].astype(BF16), w_out=w_out[l].astype(BF16),
        ln_mix_g=ln_mix_g[l].reshape(1, -1), ln_mix_b=ln_mix_b[l].reshape(1, -1),
        xa_wq=xa_wq[l].astype(BF16), xa_wkv=xa_wkv[l].astype(BF16), xa_wo=xa_wo[l].astype(BF16),
        ln_xa_g=ln_xa_g[l].reshape(1, -1), ln_xa_b=ln_xa_b[l].reshape(1, -1),
        rw_hi=rw_hi, rw_lo=rw_lo, rb=router_b.reshape(-1, 1),
        ln_ffn_g=ln_ffn_g[l].reshape(1, -1), ln_ffn_b=ln_ffn_b[l].reshape(1, -1),
    )


def kernel(x, mem, w_in, gla_w_a2, gla_b_a, gla_norm_g, nsa_cmp_pe, nsa_cmp_w1, nsa_cmp_w2, w_branch_gla, w_branch_nsa, w_out, ln_mix_g, ln_mix_b, xa_wq, xa_wkv, xa_wo, ln_xa_g, ln_xa_b, router_w, router_b, moe_w_in, moe_w_down, ln_ffn_g, ln_ffn_b):
    B, T, d = x.shape
    assert T % 512 == 0 and d == 2048 and mem.shape[1] == MEM_LEN
    n = B * T
    params = (w_in, gla_w_a2, gla_b_a, gla_norm_g, nsa_cmp_pe, nsa_cmp_w1, nsa_cmp_w2, w_branch_gla, w_branch_nsa,
              w_out, ln_mix_g, ln_mix_b, xa_wq, xa_wkv, xa_wo, ln_xa_g, ln_xa_b, router_w, router_b, moe_w_in,
              moe_w_down, ln_ffn_g, ln_ffn_b)
    slopes = (2.0 ** (-8.0 * jnp.arange(1, NSA_HEADS + 1, dtype=F32) / NSA_HEADS)).astype(F32)
    nc, ns = T // CMP_STRIDE, T // SEL_LEN
    cs = np.arange(nc) * CMP_STRIDE
    ss = np.arange(ns) * SEL_LEN
    ovt = ((cs[None, :] < ss[:, None] + SEL_LEN) & (cs[None, :] + CMP_LEN > ss[:, None])
           & (cs[None, :] + CMP_LEN <= T)).astype(np.float32)
    assert NSA_DH + ns <= LANES
    epad = np.zeros((T, LANES), np.float32)
    epad[np.arange(T), NSA_DH + np.arange(T) // SEL_LEN] = 1.0
    tile_ind = (np.arange(ns)[None, :] // (256 // SEL_LEN) == np.arange(T // 256)[:, None]).astype(np.float32)
    consts = (slopes, jnp.asarray(ovt, BF16), jnp.asarray(epad, BF16), jnp.asarray(tile_ind, BF16))

    xf = x.reshape(n, d)
    xb = xf.astype(BF16)
    mem_b = mem.reshape(B * MEM_LEN, d).astype(BF16)
    moe_w = (moe_w_in, moe_w_down)
    for l in range(DEPTH):
        p = _prep_layer(l, *params)
        xf, xb = _layer(xf, xb, mem_b, p, moe_w, l, consts, B, T)
    return xf.reshape(B, T, d)
```

```python
import functools

import jax
import jax.numpy as jnp
import numpy as np
from jax import lax
from jax.experimental import pallas as pl
from jax.experimental.pallas import tpu as pltpu

F32 = jnp.float32
BF16 = jnp.bfloat16
I32 = jnp.int32

DEPTH = 2
MEM_LEN = 256
GLA_HEADS = 4
GLA_DK = 128
GLA_DV = 256
GLA_GATE_RANK = 16
GLA_TAU = 16.0
GLA_CHUNK = 64
NSA_HEADS = 16
NSA_GROUPS = 4
NSA_HPG = NSA_HEADS // NSA_GROUPS
NSA_DH = 64
CMP_LEN = 32
CMP_STRIDE = 16
CMP_HIDDEN = 256
SEL_LEN = 64
SEL_TOPN = 8
WINDOW = 512
XA_HEADS = 4
XA_DH = 128
N_EXPERTS = 16
N_GROUPS = 4
EXPERTS_PER_GROUP = N_EXPERTS // N_GROUPS
TOP_K = 2
D_FF = 1536
DN_ALPHA = float((2 * DEPTH) ** 0.25)
LN_EPS = 1e-5
NEG = -1e30
LOG2E = 1.4426950408889634
FORCE_BONUS = 1e6

GLA_QK = GLA_HEADS * GLA_DK
GLA_V = GLA_HEADS * GLA_DV
NSA_Q = NSA_HEADS * NSA_DH
NSA_KV = NSA_GROUPS * NSA_DH

LANES = 128
VMEM_LIMIT = 56 * 1024 * 1024

COL_MG = 0
COL_GQ = 2 * 2048
COL_GK = COL_GQ + GLA_QK
COL_GV = COL_GK + GLA_QK
COL_GR = COL_GV + GLA_V
COL_KS = COL_GR + GLA_V
COL_KW = COL_KS + NSA_GROUPS * LANES
COL_END = COL_KW + NSA_GROUPS * LANES
SCOL_CK = LANES
SCOL_CV = SCOL_CK + NSA_KV
SCOL_END = SCOL_CV + NSA_KV
TROW_Q = 0
TROW_VS = NSA_Q
TROW_VW = TROW_VS + NSA_KV
TROW_END = TROW_VW + NSA_KV

MOE_BLOCK = 512
FF_TILE = 512


def _cp(sem):
    return pltpu.CompilerParams(dimension_semantics=sem, vmem_limit_bytes=VMEM_LIMIT)


def _dot(a, b):
    return jnp.dot(a, b, preferred_element_type=F32)


def _dot_nt(a, b):
    return lax.dot_general(a, b, (((1,), (1,)), ((), ())), preferred_element_type=F32)


def _dot_tn(a, b):
    return lax.dot_general(a, b, (((0,), (0,)), ((), ())), preferred_element_type=F32)


def _layer_norm(z, g, b):
    mu = jnp.mean(z, axis=-1, keepdims=True)
    zc = z - mu
    var = jnp.mean(zc * zc, axis=-1, keepdims=True)
    return zc * lax.rsqrt(var + LN_EPS) * g + b


def _mm_kernel(a_ref, b_ref, o_ref):
    o_ref[...] = _dot(a_ref[...], b_ref[...]).astype(o_ref.dtype)


def _matmul(a, b, out_dtype, tm, tn):
    m, k = a.shape
    n = b.shape[1]
    return pl.pallas_call(
        _mm_kernel,
        grid=(m // tm, n // tn),
        in_specs=[pl.BlockSpec((tm, k), lambda i, j: (i, 0)),
                  pl.BlockSpec((k, tn), lambda i, j: (0, j))],
        out_specs=pl.BlockSpec((tm, tn), lambda i, j: (i, j)),
        out_shape=jax.ShapeDtypeStruct((m, n), out_dtype),
        compiler_params=_cp(("parallel", "parallel")),
        name="matmul",
    )(a, b)


def _mm_nt_kernel(wt_ref, x_ref, o_ref):
    o_ref[0] = _dot_nt(wt_ref[...], x_ref[...]).astype(o_ref.dtype)


def _matmul_t(x, wt, B, T, tm, tr):
    n, k = x.shape
    r = wt.shape[0]
    nt = T // tm
    return pl.pallas_call(
        _mm_nt_kernel,
        grid=(n // tm, r // tr),
        in_specs=[pl.BlockSpec((tr, k), lambda i, j: (j, 0)),
                  pl.BlockSpec((tm, k), lambda i, j: (i, 0))],
        out_specs=pl.BlockSpec((1, tr, tm), lambda i, j: (i // nt, j, i % nt)),
        out_shape=jax.ShapeDtypeStruct((B, r, T), BF16),
        compiler_params=_cp(("parallel", "parallel")),
        name="matmul_t",
    )(wt, x)


def _gla_kernel(q_ref, k_ref, v_ref, r_ref, sm_ref, wa_ref, ba_ref, ng_ref, o_ref, st_ref):
    C = GLA_CHUNK
    n_chunks = q_ref.shape[0] // C
    st_ref[...] = jnp.zeros_like(st_ref)
    rowi = lax.broadcasted_iota(I32, (C, GLA_DK), 0)
    tt = lax.broadcasted_iota(I32, (C, C), 0)
    ss = lax.broadcasted_iota(I32, (C, C), 1)
    levels = (1, 2, 4, 8, 16, 32)
    pair_masks = [((tt // (2 * L)) == (ss // (2 * L))) & ((tt & L) != 0) & ((ss & L) == 0) for L in levels]
    diag_mask = tt == ss
    scale = GLA_DK ** -0.5

    def head_chunk(rows, h, z):
        qk_cols = slice(h * GLA_DK, (h + 1) * GLA_DK)
        v_cols = slice(h * GLA_DV, (h + 1) * GLA_DV)
        q = q_ref[rows, qk_cols].astype(F32) * scale
        k = k_ref[rows, qk_cols].astype(F32)
        v = v_ref[rows, v_cols]
        g = (jnp.minimum(z, 0.0) - jnp.log1p(jnp.exp(-jnp.abs(z)))) * (1.0 / GLA_TAU)
        incl = g
        tot = g
        att = jnp.where(diag_mask, _dot_nt(q.astype(BF16), k.astype(BF16)), 0.0)
        for L, pm in zip(levels, pair_masks):
            ql = (q * jnp.exp(incl)).astype(BF16)
            kl = (k * jnp.exp(tot - incl)).astype(BF16)
            att = jnp.where(pm, _dot_nt(ql, kl), att)
            upper = (rowi & L) != 0
            from_lower = pltpu.roll(tot, L, 0)
            from_upper = pltpu.roll(tot, C - L, 0)
            incl = incl + jnp.where(upper, from_lower, 0.0)
            tot = tot + jnp.where(upper, from_lower, from_upper)
        qd = (q * jnp.exp(incl)).astype(BF16)
        kd = (k * jnp.exp(tot - incl)).astype(BF16)
        st = st_ref[h]
        o = _dot_nt(qd, st.astype(BF16)) + _dot(att.astype(BF16), v)
        st_ref[h] = st * jnp.exp(tot[0:1, :]) + _dot_tn(v, kd)
        mu = jnp.mean(o, axis=-1, keepdims=True)
        oc = o - mu
        var = jnp.mean(oc * oc, axis=-1, keepdims=True)
        on = oc * lax.rsqrt(var + LN_EPS) * ng_ref[:, v_cols]
        r = r_ref[rows, v_cols].astype(F32)
        o_ref[rows, v_cols] = (on * (r * jax.nn.sigmoid(r))).astype(o_ref.dtype)

    def chunk(c, carry):
        rows = pl.ds(pl.multiple_of(c * C, C), C)
        z = _dot(sm_ref[rows, :].astype(BF16), wa_ref[...]) + ba_ref[...]
        for h in range(GLA_HEADS):
            head_chunk(rows, h, z[:, h * GLA_DK:(h + 1) * GLA_DK])
        return carry

    lax.fori_loop(0, n_chunks, chunk, 0)


def _gla(h_big, h_small, wa_pad, b_a, norm_g, B, T):
    n = B * T
    return pl.pallas_call(
        _gla_kernel,
        grid=(B,),
        in_specs=[
            pl.BlockSpec((T, GLA_QK), lambda b: (b, COL_GQ // GLA_QK)),
            pl.BlockSpec((T, GLA_QK), lambda b: (b, COL_GK // GLA_QK)),
            pl.BlockSpec((T, GLA_V), lambda b: (b, COL_GV // GLA_V)),
            pl.BlockSpec((T, GLA_V), lambda b: (b, COL_GR // GLA_V)),
            pl.BlockSpec((T, LANES), lambda b: (b, 0)),
            pl.BlockSpec((LANES, GLA_QK), lambda b: (0, 0)),
            pl.BlockSpec((1, GLA_QK), lambda b: (0, 0)),
            pl.BlockSpec((1, GLA_V), lambda b: (0, 0)),
        ],
        out_specs=pl.BlockSpec((T, GLA_V), lambda b: (b, 0)),
        out_shape=jax.ShapeDtypeStruct((n, GLA_V), BF16),
        scratch_shapes=[pltpu.VMEM((GLA_HEADS, GLA_DV, GLA_DK), F32)],
        compiler_params=_cp(("parallel",)),
        name="gla",
    )(h_big, h_big, h_big, h_big, h_small, wa_pad, b_a, norm_g)


def _compress_kernel(x_ref, w1_ref, w2_ref, w2t_ref, pe_ref, o_ref, ot_ref):
    nc = x_ref.shape[0] // CMP_STRIDE
    hid_w = w1_ref.shape[3]
    a = jnp.zeros((nc, hid_w), F32)
    bm = jnp.zeros((nc, hid_w), F32)
    c = jnp.zeros((pe_ref.shape[2], hid_w), F32)
    for l in range(CMP_STRIDE):
        xl = x_ref[pl.ds(l, nc, stride=CMP_STRIDE), :].astype(BF16)
        a = a + _dot(xl, w1_ref[0, l])
        bm = bm + _dot(xl, w1_ref[0, CMP_STRIDE + l])
    for l in range(CMP_LEN):
        c = c + _dot(pe_ref[0, l], w1_ref[0, l])
    hid = a + pltpu.roll(bm, nc - 1, 0) + c[0:1, :]
    act = jax.nn.gelu(hid).astype(BF16)
    o_ref[0, 0, 0] = _dot(act, w2_ref[0]).astype(o_ref.dtype)
    ot_ref[0, 0, 0] = _dot_nt(w2t_ref[0], act).astype(ot_ref.dtype)


def _compress(h_small, w1bd, w2bd, w2bdt, pe_pair, B, T):
    nc = T // CMP_STRIDE
    pairs = NSA_GROUPS // 2
    return pl.pallas_call(
        _compress_kernel,
        grid=(B, 2, pairs),
        in_specs=[
            pl.BlockSpec((T, LANES), lambda b, s, j: (b, SCOL_CK // LANES + s * pairs + j)),
            pl.BlockSpec((1,) + w1bd.shape[1:], lambda b, s, j: (s, 0, 0, 0)),
            pl.BlockSpec((1,) + w2bd.shape[1:], lambda b, s, j: (s, 0, 0)),
            pl.BlockSpec((1,) + w2bdt.shape[1:], lambda b, s, j: (s, 0, 0)),
            pl.BlockSpec((1,) + pe_pair.shape[1:], lambda b, s, j: (s, 0, 0, 0)),
        ],
        out_specs=[
            pl.BlockSpec((1, 1, 1, nc, LANES), lambda b, s, j: (b, s, j, 0, 0)),
            pl.BlockSpec((1, 1, 1, LANES, nc), lambda b, s, j: (b, s, j, 0, 0)),
        ],
        out_shape=[jax.ShapeDtypeStruct((B, 2, pairs, nc, LANES), BF16),
                   jax.ShapeDtypeStruct((B, 2, pairs, LANES, nc), BF16)],
        compiler_params=_cp(("parallel", "parallel", "parallel")),
        name="nsa_compress",
    )(h_small, w1bd, w2bd, w2bdt, pe_pair)


def _cmp_select_kernel(slopes_ref, qt_ref, kc_ref, vct_ref, ovt_ref, ind_ref, ocmp_ref, mb_ref, kt_ref, qpad_ref):
    g = pl.program_id(1)
    i = pl.program_id(2)
    tq = qt_ref.shape[2]
    nc = kc_ref.shape[3]
    ns = mb_ref.shape[2]
    dh = NSA_DH
    t0 = i * tq
    tpos = (t0 + lax.broadcasted_iota(I32, (nc, tq), 1))
    nidx = lax.broadcasted_iota(I32, (nc, tq), 0)
    mask_c = (nidx * CMP_STRIDE + (CMP_LEN - 1)) <= tpos
    absd = jnp.abs(tpos.astype(F32) - (nidx.astype(F32) * CMP_STRIDE + 0.5 * (CMP_LEN - 1)))
    lower = g % 2 == 0
    kc = kc_ref[0, 0, 0]
    vct = jnp.where(lower, vct_ref[0, 0, 0, 0:dh, :], vct_ref[0, 0, 0, dh:2 * dh, :])
    psum = jnp.zeros((nc, tq), F32)
    for hh in range(NSA_HPG):
        slope = slopes_ref[g * NSA_HPG + hh]
        q = qt_ref[0, hh * dh:(hh + 1) * dh, :] * jnp.asarray(dh ** -0.5, BF16)
        zero = jnp.zeros_like(q)
        qpad_ref[0:dh, :] = jnp.where(lower, q, zero)
        qpad_ref[dh:2 * dh, :] = jnp.where(lower, zero, q)
        s = _dot(kc, qpad_ref[...]) - slope * absd
        s = jnp.where(mask_c, s, NEG)
        e = jnp.exp(s - jnp.max(s, axis=0, keepdims=True))
        p = jnp.where(mask_c, e / jnp.sum(e, axis=0, keepdims=True), 0.0)
        ocmp_ref[0, hh * dh:(hh + 1) * dh, :] = _dot(vct, p.astype(BF16)).astype(ocmp_ref.dtype)
        psum = psum + p
    p_hi = psum.astype(BF16)
    p_lo = (psum - p_hi.astype(F32)).astype(BF16)
    imp = _dot(ovt_ref[...], p_hi) + _dot(ovt_ref[...], p_lo)
    j = lax.broadcasted_iota(I32, (ns, tq), 0)
    tp = t0 + lax.broadcasted_iota(I32, (ns, tq), 1)
    cur = tp // SEL_LEN
    forced = (j == 0) | (j == cur) | (j == cur - 1)
    valid = j * SEL_LEN <= tp
    score = jnp.where(valid, imp + jnp.where(forced, FORCE_BONUS, 0.0), NEG)
    rank = jnp.zeros((ns, tq), F32)
    for jp in range(ns):
        row = score[jp:jp + 1, :]
        beats = (row > score) | ((row == score) & (j > jp))
        rank = rank + jnp.where(beats, 1.0, 0.0)
    keep = valid & (rank < float(min(SEL_TOPN, ns)))
    mb_ref[0, 0] = jnp.where(keep, 0.0, NEG).astype(mb_ref.dtype)
    kt_ref[0, 0] = _dot(ind_ref[...], jnp.where(keep, 1.0, 0.0).astype(BF16))


def _cmp_select(slopes, h_t, kcmp, kcmp_t, ovt, tile_ind, B, T, tq):
    nc = T // CMP_STRIDE
    ns = T // SEL_LEN
    nkt = tile_ind.shape[0]
    grp_rows = NSA_HPG * NSA_DH
    grid_spec = pltpu.PrefetchScalarGridSpec(
        num_scalar_prefetch=1,
        grid=(B, NSA_GROUPS, T // tq),
        in_specs=[
            pl.BlockSpec((1, grp_rows, tq), lambda b, g, i, s: (b, TROW_Q // grp_rows + g, i)),
            pl.BlockSpec((1, 1, 1, nc, LANES), lambda b, g, i, s: (b, 0, g // 2, 0, 0)),
            pl.BlockSpec((1, 1, 1, LANES, nc), lambda b, g, i, s: (b, 1, g // 2, 0, 0)),
            pl.BlockSpec((ns, nc), lambda b, g, i, s: (0, 0)),
            pl.BlockSpec((nkt, ns), lambda b, g, i, s: (0, 0)),
        ],
        out_specs=[
            pl.BlockSpec((1, grp_rows, tq), lambda b, g, i, s: (b, g, i)),
            pl.BlockSpec((1, 1, ns, tq), lambda b, g, i, s: (b, g, 0, i)),
            pl.BlockSpec((1, 1, nkt, tq), lambda b, g, i, s: (b, g, 0, i)),
        ],
        scratch_shapes=[pltpu.VMEM((LANES, tq), BF16)],
    )
    return pl.pallas_call(
        _cmp_select_kernel,
        grid_spec=grid_spec,
        out_shape=[jax.ShapeDtypeStruct((B, NSA_Q, T), BF16),
                   jax.ShapeDtypeStruct((B, NSA_GROUPS, ns, T), BF16),
                   jax.ShapeDtypeStruct((B, NSA_GROUPS, nkt, T), F32)],
        compiler_params=_cp(("parallel", "parallel", "parallel")),
        name="nsa_cmp_select",
    )(slopes, h_t, kcmp, kcmp_t, ovt, tile_ind)


def _sel_win_kernel(slopes_ref, flags_ref, qt_ref, ks_ref, kw_ref, vs_ref, vw_ref, epad_ref, mb_ref, ocmp_ref, gt_ref,
                    o_ref, qaug_ref, m_ref, acc_ref, srow_ref, bias_ref, s_ref, p_ref, alpha_ref,
                    kall_ref, vall_ref, tiles_ref, *, n_tiles):
    g = pl.program_id(1)
    i = pl.program_id(2)
    tq = qt_ref.shape[2]
    tk = tq
    ns = mb_ref.shape[2]
    dh = NSA_DH
    wide = NSA_HPG * tq
    t0 = i * tq
    BIG = -NEG

    @pl.when(i == 0)
    def _():
        srow = jnp.concatenate([jnp.full((1, tq), slopes_ref[g * NSA_HPG + hh] * LOG2E, F32)
                                for hh in range(NSA_HPG)], axis=1)
        srow_ref[...] = srow
        lane = lax.broadcasted_iota(I32, (tk, wide), 1) & (tq - 1)
        dist0 = (lane - lax.broadcasted_iota(I32, (tk, wide), 0)).astype(F32)
        sd0 = srow * dist0
        bias_ref[0] = sd0
        bias_ref[1] = sd0 + jnp.where(dist0 >= 0.0, 0.0, BIG)
        bias_ref[2] = sd0 + jnp.where(dist0 < 0.0, 0.0, BIG)
        bias_ref[3] = jnp.full((tk, wide), BIG, F32)
        kall_ref[0] = ks_ref[...] + epad_ref[...]
        kall_ref[1] = kw_ref[...]
        extra = jnp.where(lax.broadcasted_iota(I32, (vall_ref.shape[1] - dh, vall_ref.shape[2]), 0) == 0, 1.0, 0.0)
        vall_ref[0, 0:dh, :] = vs_ref[0]
        vall_ref[1, 0:dh, :] = vw_ref[0]
        vall_ref[0, dh:, :] = extra.astype(BF16)
        vall_ref[1, dh:, :] = extra.astype(BF16)

    for hh in range(NSA_HPG):
        cols = slice(hh * tq, (hh + 1) * tq)
        q = qt_ref[0, hh * dh:(hh + 1) * dh, :].astype(F32) * (dh ** -0.5 * LOG2E)
        qaug_ref[0:dh, cols] = q.astype(BF16)
        qaug_ref[dh:dh + ns, cols] = mb_ref[0, 0]
        qaug_ref[dh + ns:, cols] = jnp.zeros((qaug_ref.shape[0] - dh - ns, tq), BF16)

    m_ref[...] = jnp.full(m_ref.shape, NEG, F32)
    acc_ref[...] = jnp.zeros(acc_ref.shape, F32)

    n_back = WINDOW // tk
    flag_base = ((pl.program_id(0) * NSA_GROUPS + g) * n_tiles + i) * n_tiles
    n_sel = jnp.int32(0)
    for kb_static in range(n_tiles - 1):
        active = (kb_static < i) & (flags_ref[flag_base + kb_static] != 0)

        @pl.when(active)
        def _(kb_static=kb_static, n_sel=n_sel):
            tiles_ref[n_sel] = kb_static

        n_sel = n_sel + active.astype(I32)
    tiles_ref[n_sel] = i
    n_sel = n_sel + 1
    n_win = jnp.minimum(i, n_back) + 1
    n_steps = n_sel + n_win

    def describe(n):
        n = jnp.maximum(n, 0)
        is_win = n >= n_sel
        kb_sel = tiles_ref[jnp.minimum(n, n_sel - 1)]
        kb = jnp.clip(jnp.where(is_win, i - n_win + 1 + (n - n_sel), kb_sel), 0, i)
        mode = jnp.where(kb == i, 1, jnp.where(is_win & (kb == i - n_back), 2, 0))
        mode = jnp.where(n >= n_steps, 3, mode)
        return is_win.astype(I32), kb, mode

    def scores(n, slot):
        br, kb, _ = describe(n)
        s0 = pl.multiple_of(kb * tk, tk)
        s_ref[slot] = _dot(kall_ref[br, pl.ds(s0, tk), :], qaug_ref[...])

    def softmax(n, slot):
        br, kb, mode = describe(n)
        crow = srow_ref[...] * ((i - kb) * tk).astype(F32)
        s = s_ref[slot] - bias_ref[mode]
        m_old = m_ref[br]
        m_new = jnp.maximum(m_old, jnp.max(s, axis=0, keepdims=True) - crow)
        alpha = jnp.exp2(m_old - m_new)
        p = jnp.exp2(s - (m_new + crow))
        m_ref[br] = m_new
        alpha_ref[slot] = alpha
        p_ref[slot] = p.astype(BF16)

    def weighted_values(n, slot):
        br, kb, _ = describe(n)
        s0 = pl.multiple_of(kb * tk, tk)
        acc_ref[br] = alpha_ref[slot] * acc_ref[br] + _dot(vall_ref[br, :, pl.ds(s0, tk)], p_ref[slot])

    p_ref[1] = jnp.zeros(p_ref.shape[1:], BF16)
    alpha_ref[1] = jnp.ones(alpha_ref.shape[1:], F32)
    scores(0, 0)

    def pair(j, carry):
        n = 2 * j
        scores(n + 1, 1)
        softmax(n, 0)
        weighted_values(n - 1, 1)
        scores(n + 2, 0)
        softmax(n + 1, 1)
        weighted_values(n, 0)
        return carry

    n_pairs = (n_steps + 1) // 2
    lax.fori_loop(0, n_pairs, pair, 0)
    weighted_values(2 * n_pairs - 1, 1)

    def gate_row(branch):
        rows = [gt_ref[0, 0, 3 * hh + branch:3 * hh + branch + 1, :] for hh in range(NSA_HPG)]
        return jax.nn.sigmoid(jnp.concatenate(rows, axis=1))

    o = (gate_row(1) * (acc_ref[0, 0:dh, :] / acc_ref[0, dh:dh + 1, :])
         + gate_row(2) * (acc_ref[1, 0:dh, :] / acc_ref[1, dh:dh + 1, :]))
    ocmp = jnp.concatenate([ocmp_ref[0, hh * dh:(hh + 1) * dh, :] for hh in range(NSA_HPG)], axis=1).astype(F32)
    o = o + gate_row(0) * ocmp
    o_heads = jnp.concatenate([o[:, hh * tq:(hh + 1) * tq] for hh in range(NSA_HPG)], axis=0)
    o_ref[...] = o_heads.T.astype(o_ref.dtype)


def _sel_win(slopes, tile_flags, h_t, h_big, epad, mb, ocmp_t, gates_t, B, T, tq):
    ns = T // SEL_LEN
    kaug = LANES
    G = NSA_GROUPS
    grp_rows = NSA_HPG * NSA_DH
    nq = T // tq
    grid_spec = pltpu.PrefetchScalarGridSpec(
        num_scalar_prefetch=2,
        grid=(B, G, nq),
        in_specs=[
            pl.BlockSpec((1, grp_rows, tq), lambda b, g, i, s, f: (b, TROW_Q // grp_rows + g, i)),
            pl.BlockSpec((T, LANES), lambda b, g, i, s, f: (b, COL_KS // LANES + g)),
            pl.BlockSpec((T, LANES), lambda b, g, i, s, f: (b, COL_KW // LANES + g)),
            pl.BlockSpec((1, NSA_DH, T), lambda b, g, i, s, f: (b, TROW_VS // NSA_DH + g, 0)),
            pl.BlockSpec((1, NSA_DH, T), lambda b, g, i, s, f: (b, TROW_VW // NSA_DH + g, 0)),
            pl.BlockSpec((T, LANES), lambda b, g, i, s, f: (0, 0)),
            pl.BlockSpec((1, 1, ns, tq), lambda b, g, i, s, f: (b, g, 0, i)),
            pl.BlockSpec((1, grp_rows, tq), lambda b, g, i, s, f: (b, g, i)),
            pl.BlockSpec((1, 1, 16, tq), lambda b, g, i, s, f: (b, g, 0, i)),
        ],
        out_specs=pl.BlockSpec((tq, grp_rows), lambda b, g, i, s, f: (b * nq + i, g)),
        scratch_shapes=[
            pltpu.VMEM((kaug, NSA_HPG * tq), BF16),
            pltpu.VMEM((2, 1, NSA_HPG * tq), F32),
            pltpu.VMEM((2, NSA_DH + 16, NSA_HPG * tq), F32),
            pltpu.VMEM((1, NSA_HPG * tq), F32),
            pltpu.VMEM((4, tq, NSA_HPG * tq), F32),
            pltpu.VMEM((2, tq, NSA_HPG * tq), F32),
            pltpu.VMEM((2, tq, NSA_HPG * tq), BF16),
            pltpu.VMEM((2, 1, NSA_HPG * tq), F32),
            pltpu.VMEM((2, T, kaug), BF16),
            pltpu.VMEM((2, NSA_DH + 16, T), BF16),
            pltpu.SMEM((nq,), I32),
        ],
    )
    return pl.pallas_call(
        functools.partial(_sel_win_kernel, n_tiles=nq),
        grid_spec=grid_spec,
        out_shape=jax.ShapeDtypeStruct((B * T, NSA_Q), BF16),
        compiler_params=_cp(("parallel", "parallel", "arbitrary")),
        name="nsa_sel_win",
    )(slopes, tile_flags, h_t, h_big, h_big, h_t, h_t, epad, mb, ocmp_t, gates_t)


def _mix_kernel(og_ref, on_ref, mg1_ref, mg2_ref, x_ref, wg_ref, wn_ref, wo_ref, lg_ref, lb_ref,
                x1_ref, x1b_ref):
    g1 = _dot(og_ref[...], wg_ref[...])
    g2 = _dot(on_ref[...], wn_ref[...])
    merged = (jax.nn.sigmoid(mg1_ref[...].astype(F32)) * g1
              + jax.nn.sigmoid(mg2_ref[...].astype(F32)) * g2)
    y = _dot(merged.astype(BF16), wo_ref[...])
    x1 = _layer_norm(DN_ALPHA * x_ref[...] + y, lg_ref[...], lb_ref[...])
    x1_ref[...] = x1
    x1b_ref[...] = x1.astype(BF16)


def _const_spec(shape):
    nd = len(shape)
    return pl.BlockSpec(shape, lambda *_: (0,) * nd, pipeline_mode=pl.Buffered(1))


def _mix(o_gla, o_nsa, h_big, x, wg, wn, wo, lg, lb, tm):
    n, d = x.shape
    return pl.pallas_call(
        _mix_kernel,
        grid=(n // tm,),
        in_specs=[
            pl.BlockSpec((tm, GLA_V), lambda i: (i, 0)),
            pl.BlockSpec((tm, NSA_Q), lambda i: (i, 0)),
            pl.BlockSpec((tm, d), lambda i: (i, 0)),
            pl.BlockSpec((tm, d), lambda i: (i, 1)),
            pl.BlockSpec((tm, d), lambda i: (i, 0)),
            _const_spec(wg.shape), _const_spec(wn.shape), _const_spec(wo.shape),
            _const_spec(lg.shape), _const_spec(lb.shape),
        ],
        out_specs=[pl.BlockSpec((tm, d), lambda i: (i, 0)), pl.BlockSpec((tm, d), lambda i: (i, 0))],
        out_shape=[jax.ShapeDtypeStruct((n, d), F32), jax.ShapeDtypeStruct((n, d), BF16)],
        compiler_params=_cp(("parallel",)),
        name="mix_ln",
    )(o_gla, o_nsa, h_big, h_big, x, wg, wn, wo, lg, lb)


def _xattn_kernel(x_ref, xb_ref, kv_ref, wq_ref, wo_ref, lg_ref, lb_ref, x2_ref):
    hd = XA_HEADS * XA_DH
    q = (_dot(xb_ref[...], wq_ref[...]) * (XA_DH ** -0.5)).astype(BF16)
    outs = []
    for h in range(XA_HEADS):
        kh = kv_ref[0, :, h * XA_DH:(h + 1) * XA_DH]
        vh = kv_ref[0, :, hd + h * XA_DH:hd + (h + 1) * XA_DH]
        s = _dot_nt(q[:, h * XA_DH:(h + 1) * XA_DH], kh)
        e = jnp.exp(s - jnp.max(s, axis=-1, keepdims=True))
        p = e / jnp.sum(e, axis=-1, keepdims=True)
        outs.append(_dot(p.astype(BF16), vh).astype(BF16))
    o = jnp.concatenate(outs, axis=-1)
    y = _dot(o, wo_ref[...])
    x2_ref[...] = _layer_norm(DN_ALPHA * x_ref[...] + y, lg_ref[...], lb_ref[...])


def _xattn(x1, x1b, kv, wq, wo, lg, lb, B, T, tm):
    n, d = x1.shape
    nt = T // tm
    return pl.pallas_call(
        _xattn_kernel,
        grid=(B, nt),
        in_specs=[
            pl.BlockSpec((tm, d), lambda b, i: (b * nt + i, 0)),
            pl.BlockSpec((tm, d), lambda b, i: (b * nt + i, 0)),
            pl.BlockSpec((1,) + kv.shape[1:], lambda b, i: (b, 0, 0)),
            _const_spec(wq.shape), _const_spec(wo.shape), _const_spec(lg.shape), _const_spec(lb.shape),
        ],
        out_specs=pl.BlockSpec((tm, d), lambda b, i: (b * nt + i, 0)),
        out_shape=jax.ShapeDtypeStruct((n, d), F32),
        compiler_params=_cp(("parallel", "parallel")),
        name="xattn_ln",
    )(x1, x1b, kv, wq, wo, lg, lb)


def _router_kernel(x_ref, wh_ref, wl_ref, rb_ref, e_ref, gate_ref, rank_ref, cnt_ref, carry_ref):
    i = pl.program_id(0)
    tr = x_ref.shape[0]
    E = N_EXPERTS

    @pl.when(i == 0)
    def _():
        carry_ref[...] = jnp.zeros_like(carry_ref)

    x = x_ref[...]
    x_hi = x.astype(BF16)
    x_lo = (x - x_hi.astype(F32)).astype(BF16)
    wh = wh_ref[...]
    logits = _dot_nt(wh, x_hi) + _dot_nt(wh, x_lo) + _dot_nt(wl_ref[...], x_hi)
    biased = logits + rb_ref[...]
    rows = [biased[e:e + 1, :] for e in range(E)]
    raw = [logits[e:e + 1, :] for e in range(E)]
    best_score = None
    best = None
    for gi in range(N_GROUPS):
        v = rows[gi * EXPERTS_PER_GROUP:(gi + 1) * EXPERTS_PER_GROUP]
        sc = None
        for a in range(EXPERTS_PER_GROUP):
            for b in range(a + 1, EXPERTS_PER_GROUP):
                pair = v[a] + v[b]
                sc = pair if sc is None else jnp.maximum(sc, pair)
        if best is None:
            best_score, best = sc, jnp.zeros((1, tr), I32)
        else:
            better = sc > best_score
            best_score = jnp.where(better, sc, best_score)
            best = jnp.where(better, gi, best)

    def pick(vals):
        out = vals[0:EXPERTS_PER_GROUP]
        for gi in range(1, N_GROUPS):
            out = [jnp.where(best == gi, vals[gi * EXPERTS_PER_GROUP + a], out[a]) for a in range(EXPERTS_PER_GROUP)]
        return out

    w = pick(rows)
    lraw = pick(raw)
    i1 = jnp.zeros((1, tr), I32)
    v1 = w[0]
    l1 = lraw[0]
    for a in range(1, EXPERTS_PER_GROUP):
        better = w[a] > v1
        v1 = jnp.where(better, w[a], v1)
        l1 = jnp.where(better, lraw[a], l1)
        i1 = jnp.where(better, a, i1)
    i2 = jnp.full((1, tr), -1, I32)
    v2 = jnp.full((1, tr), -jnp.inf, F32)
    l2 = jnp.zeros((1, tr), F32)
    for a in range(EXPERTS_PER_GROUP):
        better = (i1 != a) & ((w[a] > v2) | (i2 < 0))
        v2 = jnp.where(better, w[a], v2)
        l2 = jnp.where(better, lraw[a], l2)
        i2 = jnp.where(better, a, i2)
    e1 = best * EXPERTS_PER_GROUP + i1
    e2 = best * EXPERTS_PER_GROUP + i2
    mx = jnp.maximum(l1, l2)
    p1 = jnp.exp(l1 - mx)
    p2 = jnp.exp(l2 - mx)
    den = p1 + p2
    e_ref[0:1, :] = e1
    e_ref[1:2, :] = e2
    gate_ref[0:1, :] = p1 / den
    gate_ref[1:2, :] = p2 / den
    eidx = lax.broadcasted_iota(I32, (E, tr), 0)
    is1 = eidx == e1
    is2 = eidx == e2
    member = jnp.where(is1 | is2, 1.0, 0.0)
    uu = lax.broadcasted_iota(I32, (tr, tr), 0)
    tt = lax.broadcasted_iota(I32, (tr, tr), 1)
    tri = jnp.where(uu <= tt, 1.0, 0.0).astype(BF16)
    incl = _dot(member.astype(BF16), tri)
    excl = carry_ref[:, 0:1] + incl - member
    rank_ref[0:1, :] = jnp.sum(jnp.where(is1, excl, 0.0), axis=0, keepdims=True).astype(I32)
    rank_ref[1:2, :] = jnp.sum(jnp.where(is2, excl, 0.0), axis=0, keepdims=True).astype(I32)
    new_carry = carry_ref[...] + jnp.sum(member, axis=1, keepdims=True)
    carry_ref[...] = new_carry
    cnt_ref[...] = new_carry


def _router(x2, rw_hi, rw_lo, rb, tr):
    n, d = x2.shape
    E = N_EXPERTS
    return pl.pallas_call(
        _router_kernel,
        grid=(n // tr,),
        in_specs=[
            pl.BlockSpec((tr, d), lambda i: (i, 0)),
            pl.BlockSpec((E, d), lambda i: (0, 0)),
            pl.BlockSpec((E, d), lambda i: (0, 0)),
            pl.BlockSpec((E, 1), lambda i: (0, 0)),
        ],
        out_specs=[
            pl.BlockSpec((2, tr), lambda i: (0, i)),
            pl.BlockSpec((2, tr), lambda i: (0, i)),
            pl.BlockSpec((2, tr), lambda i: (0, i)),
            pl.BlockSpec((E, LANES), lambda i: (0, 0)),
        ],
        out_shape=[jax.ShapeDtypeStruct((2, n), I32), jax.ShapeDtypeStruct((2, n), F32),
                   jax.ShapeDtypeStruct((2, n), I32), jax.ShapeDtypeStruct((E, LANES), F32)],
        scratch_shapes=[pltpu.VMEM((E, LANES), F32)],
        compiler_params=_cp(("arbitrary",)),
        name="moe_router",
    )(x2, rw_hi, rw_lo, rb)


def _slot_kernel(ps_ref, e_ref, rank_ref, slot_ref):
    e = e_ref[...]
    start = jnp.zeros(e.shape, I32)
    for ex in range(N_EXPERTS):
        start = jnp.where(e == ex, ps_ref[ex], start)
    slot_ref[...] = start + rank_ref[...]


def _slots(pad_start, e, rank, ts):
    n = e.shape[1]
    grid_spec = pltpu.PrefetchScalarGridSpec(
        num_scalar_prefetch=1,
        grid=(n // ts,),
        in_specs=[pl.BlockSpec((TOP_K, ts), lambda i, s: (0, i)), pl.BlockSpec((TOP_K, ts), lambda i, s: (0, i))],
        out_specs=pl.BlockSpec((TOP_K, ts), lambda i, s: (0, i)),
    )
    return pl.pallas_call(
        _slot_kernel,
        grid_spec=grid_spec,
        out_shape=jax.ShapeDtypeStruct((TOP_K, n), I32),
        compiler_params=_cp(("parallel",)),
        name="moe_slots",
    )(pad_start, e, rank)


def _dispatch_kernel(pe_ref, s0_ref, s1_ref, x_ref, buf_hbm, zero_ref, sem):
    td = s0_ref.shape[0]
    slots = (s0_ref, s1_ref)

    @pl.when(pl.program_id(0) == 0)
    def _():
        zero_ref[...] = jnp.zeros_like(zero_ref)

        def zero_copy(ex):
            last = pl.multiple_of(jnp.maximum(pe_ref[ex] - MOE_BLOCK, 0), MOE_BLOCK)
            return pltpu.make_async_copy(zero_ref, buf_hbm.at[pl.ds(last, MOE_BLOCK), :], sem)

        def nonempty(ex):
            return pe_ref[ex] > (pe_ref[ex - 1] if ex > 0 else 0)

        n_blocks = buf_hbm.shape[0] // MOE_BLOCK
        first_unused = pe_ref[N_EXPERTS - 1] // MOE_BLOCK

        def tail_copy(k):
            row = pl.multiple_of((first_unused + k) * MOE_BLOCK, MOE_BLOCK)
            return pltpu.make_async_copy(zero_ref, buf_hbm.at[pl.ds(row, MOE_BLOCK), :], sem)

        for ex in range(N_EXPERTS):
            pl.when(nonempty(ex))(lambda ex=ex: zero_copy(ex).start())
            pl.when(first_unused + ex < n_blocks)(lambda ex=ex: tail_copy(ex).start())
        for ex in range(N_EXPERTS):
            pl.when(nonempty(ex))(lambda ex=ex: zero_copy(ex).wait())
            pl.when(first_unused + ex < n_blocks)(lambda ex=ex: tail_copy(ex).wait())

    def issue(t, carry):
        for kk in range(TOP_K):
            dest = slots[kk][t]
            pltpu.make_async_copy(x_ref.at[pl.ds(t, 1), :], buf_hbm.at[pl.ds(dest, 1), :], sem).start()
        return carry

    lax.fori_loop(0, td, issue, 0, unroll=8)
    for kk in range(TOP_K):
        pltpu.make_async_copy(x_ref, buf_hbm.at[pl.ds(0, td), :], sem).wait()


def _dispatch(pad_end, slot0, slot1, x2, n_rows, td):
    n, d = x2.shape
    grid_spec = pltpu.PrefetchScalarGridSpec(
        num_scalar_prefetch=1,
        grid=(n // td,),
        in_specs=[
            pl.BlockSpec((td,), lambda i, s: (i,), memory_space=pltpu.SMEM),
            pl.BlockSpec((td,), lambda i, s: (i,), memory_space=pltpu.SMEM),
            pl.BlockSpec((td, d), lambda i, s: (i, 0)),
        ],
        out_specs=pl.BlockSpec(memory_space=pl.ANY),
        scratch_shapes=[pltpu.VMEM((MOE_BLOCK, d), F32), pltpu.SemaphoreType.DMA(())],
    )
    return pl.pallas_call(
        _dispatch_kernel,
        grid_spec=grid_spec,
        out_shape=jax.ShapeDtypeStruct((n_rows, d), F32),
        compiler_params=_cp(("arbitrary",)),
        name="moe_dispatch",
    )(pad_end, slot0, slot1, x2)


def _expert_kernel(be_ref, nb_ref, x_ref, win_hbm, wdn_hbm, y_ref, xb_ref, wa_s, wu_s, wd_s, sa, su, sd, sems,
                   *, layer):
    b = pl.program_id(0)
    nf = D_FF // FF_TILE
    n_used = nb_ref[0]
    e = be_ref[b]
    e_prev = be_ref[jnp.maximum(b - 1, 0)]
    e_next = be_ref[jnp.minimum(b + 1, pl.num_programs(0) - 1)]
    active = b < n_used
    is_first = active & ((b == 0) | (e_prev != e))
    feeds_next = active & (b + 1 < n_used) & (e_next != e)

    def tile_copies(ex, f):
        lo = f * FF_TILE
        return (pltpu.make_async_copy(win_hbm.at[layer, ex, :, pl.ds(lo, FF_TILE)], sa, sems.at[0]),
                pltpu.make_async_copy(win_hbm.at[layer, ex, :, pl.ds(D_FF + lo, FF_TILE)], su, sems.at[1]),
                pltpu.make_async_copy(wdn_hbm.at[layer, ex, pl.ds(lo, FF_TILE), :], sd, sems.at[2]))

    def start(ex, f):
        for c in tile_copies(ex, f):
            c.start()

    def finish(ex, f):
        for c in tile_copies(ex, f):
            c.wait()
        wa_s[f] = sa[...].astype(BF16)
        wu_s[f] = su[...].astype(BF16)
        wd_s[f] = sd[...].astype(BF16)

    @pl.when(b == 0)
    def _():
        for f in range(nf - 1):
            start(e, f)
            finish(e, f)
        start(e, nf - 1)

    @pl.when(jnp.logical_not(active))
    def _():
        y_ref[...] = jnp.zeros_like(y_ref)

    @pl.when(active)
    def _():
        xb_ref[...] = x_ref[...].astype(BF16)
        for f in range(nf):
            xb = xb_ref[...]
            a = _dot(xb, wa_s[f])
            u = _dot(xb, wu_s[f])
            act = (a * jax.nn.sigmoid(a) * u).astype(BF16)
            y = _dot(act, wd_s[f])
            if f == 0:
                y_ref[...] = y
                pl.when(is_first)(lambda: finish(e, nf - 1))
            else:
                y_ref[...] += y

            @pl.when(feeds_next)
            def _(f=f):
                if f >= 1:
                    finish(e_next, f - 1)
                start(e_next, f)


def _experts(blk_expert, n_used, buf, w_in, w_down, layer):
    p, d = buf.shape
    nb = p // MOE_BLOCK
    nf = D_FF // FF_TILE
    grid_spec = pltpu.PrefetchScalarGridSpec(
        num_scalar_prefetch=2,
        grid=(nb,),
        in_specs=[
            pl.BlockSpec((MOE_BLOCK, d), lambda b, be, nu: (jnp.minimum(b, nu[0] - 1), 0)),
            pl.BlockSpec(memory_space=pl.ANY),
            pl.BlockSpec(memory_space=pl.ANY),
        ],
        out_specs=pl.BlockSpec((MOE_BLOCK, d), lambda b, be, nu: (b, 0)),
        scratch_shapes=[
            pltpu.VMEM((MOE_BLOCK, d), BF16),
            pltpu.VMEM((nf, d, FF_TILE), BF16),
            pltpu.VMEM((nf, d, FF_TILE), BF16),
            pltpu.VMEM((nf, FF_TILE, d), BF16),
            pltpu.VMEM((d, FF_TILE), F32),
            pltpu.VMEM((d, FF_TILE), F32),
            pltpu.VMEM((FF_TILE, d), F32),
            pltpu.SemaphoreType.DMA((3,)),
        ],
    )
    return pl.pallas_call(
        functools.partial(_expert_kernel, layer=layer),
        grid_spec=grid_spec,
        out_shape=jax.ShapeDtypeStruct((p, d), F32),
        compiler_params=_cp(("arbitrary",)),
        name="moe_experts",
    )(blk_expert, n_used, buf, w_in, w_down)


def _combine_kernel(s0_ref, s1_ref, n0_ref, n1_ref, y_hbm, x_ref, gate_ref, lg_ref, lb_ref, x3_ref, x3b_ref,
                    y0_ref, y1_ref, sems):
    i = pl.program_id(0)
    tc = x_ref.shape[0]
    bufs = (y0_ref, y1_ref)
    cur = i % 2

    def issue_tile(slot_refs, half):
        def issue(t, carry):
            for kk in range(TOP_K):
                src = slot_refs[kk][t]
                pltpu.make_async_copy(y_hbm.at[pl.ds(src, 1), :], bufs[kk].at[half, pl.ds(t, 1), :],
                                      sems.at[half]).start()
            return carry

        lax.fori_loop(0, tc, issue, 0, unroll=8)

    @pl.when(i == 0)
    def _():
        issue_tile((s0_ref, s1_ref), 0)

    @pl.when(i + 1 < pl.num_programs(0))
    def _():
        issue_tile((n0_ref, n1_ref), 1 - cur)

    for kk in range(TOP_K):
        pltpu.make_async_copy(y_hbm.at[pl.ds(0, tc), :], bufs[kk].at[cur], sems.at[cur]).wait()
    gate = gate_ref[...]
    z = DN_ALPHA * x_ref[...] + gate[:, 0:1] * y0_ref[cur] + gate[:, 1:2] * y1_ref[cur]
    x3 = _layer_norm(z, lg_ref[...], lb_ref[...])
    x3_ref[...] = x3
    x3b_ref[...] = x3.astype(BF16)


def _combine(slot0, slot1, y, x2, gate_nt, lg, lb, tc):
    n, d = x2.shape
    last = n // tc - 1
    return pl.pallas_call(
        _combine_kernel,
        grid=(n // tc,),
        in_specs=[
            pl.BlockSpec((tc,), lambda i: (i,), memory_space=pltpu.SMEM),
            pl.BlockSpec((tc,), lambda i: (i,), memory_space=pltpu.SMEM),
            pl.BlockSpec((tc,), lambda i: (jnp.minimum(i + 1, last),), memory_space=pltpu.SMEM),
            pl.BlockSpec((tc,), lambda i: (jnp.minimum(i + 1, last),), memory_space=pltpu.SMEM),
            pl.BlockSpec(memory_space=pl.ANY),
            pl.BlockSpec((tc, d), lambda i: (i, 0)),
            pl.BlockSpec((tc, 2), lambda i: (i, 0)),
            pl.BlockSpec((1, d), lambda i: (0, 0)),
            pl.BlockSpec((1, d), lambda i: (0, 0)),
        ],
        out_specs=[pl.BlockSpec((tc, d), lambda i: (i, 0)), pl.BlockSpec((tc, d), lambda i: (i, 0))],
        out_shape=[jax.ShapeDtypeStruct((n, d), F32), jax.ShapeDtypeStruct((n, d), BF16)],
        scratch_shapes=[pltpu.VMEM((2, tc, d), F32), pltpu.VMEM((2, tc, d), F32), pltpu.SemaphoreType.DMA((2,))],
        compiler_params=_cp(("arbitrary",)),
        name="moe_combine_ln",
    )(slot0, slot1, slot0, slot1, y, x2, gate_nt, lg, lb)


def _layer(x, xb, mem_b, p, moe_w, layer, consts, B, T):
    n, d = x.shape
    G, HPG, DH = NSA_GROUPS, NSA_HPG, NSA_DH
    slopes, ovt, epad, tile_ind = consts

    h_big = _matmul(xb, p["w_big"], BF16, 1024, 1024)
    h_small = _matmul(xb, p["w_small"], F32, 1024, SCOL_END)
    h_t = _matmul_t(xb, p["w_t"], B, T, 1024, TROW_END // 2)

    o_gla = _gla(h_big, h_small, p["wa_pad"], p["b_a"], p["norm_g"], B, T)

    kcmp, kcmp_t = _compress(h_small, p["cmp_w1bd"], p["cmp_w2bd"], p["cmp_w2bdt"], p["cmp_pe_pair"], B, T)
    tq_sel = 256
    ocmp_t, mb, in_tile = _cmp_select(slopes, h_t, kcmp, kcmp_t, ovt, tile_ind, B, T, 512)
    nq = T // tq_sel
    tile_flags = (in_tile.reshape(B, G, nq, nq, tq_sel).max(axis=-1) > 0).astype(I32)
    tile_flags = tile_flags.transpose(0, 1, 3, 2).reshape(-1)
    gates_t = h_small[:, GLA_GATE_RANK:GLA_GATE_RANK + 3 * NSA_HEADS].reshape(B, T, G, 3 * HPG)
    gates_t = jnp.pad(gates_t.transpose(0, 2, 3, 1), ((0, 0), (0, 0), (0, 16 - 3 * HPG), (0, 0)))
    o_nsa = _sel_win(slopes, tile_flags, h_t, h_big, epad, mb, ocmp_t, gates_t, B, T, tq_sel)

    x1, x1b = _mix(o_gla, o_nsa, h_big, x, p["w_bg"], p["w_bn"], p["w_out"], p["ln_mix_g"], p["ln_mix_b"], 512)

    kvm = _matmul(mem_b, p["xa_wkv"], BF16, 512, 512).reshape(B, MEM_LEN, 2 * XA_HEADS * XA_DH)
    x2 = _xattn(x1, x1b, kvm, p["xa_wq"], p["xa_wo"], p["ln_xa_g"], p["ln_xa_b"], B, T, 512)

    e, gate, rank, cnt = _router(x2, p["rw_hi"], p["rw_lo"], p["rb"], 512)
    counts = cnt[:, 0].astype(I32)
    padded = (counts + MOE_BLOCK - 1) // MOE_BLOCK * MOE_BLOCK
    pad_end = jnp.cumsum(padded)
    pad_start = (pad_end - padded).astype(I32)
    nb = (n * TOP_K) // MOE_BLOCK + N_EXPERTS
    n_used = (pad_end[-1] // MOE_BLOCK).astype(I32).reshape(1)
    blk_start = jnp.arange(nb, dtype=I32) * MOE_BLOCK
    blk_expert = jnp.minimum(jnp.sum(blk_start[:, None] >= pad_end[None, :], axis=1), N_EXPERTS - 1).astype(I32)
    blk_expert = jnp.where(jnp.arange(nb) < n_used[0], blk_expert, blk_expert[jnp.maximum(n_used[0] - 1, 0)])
    slot = _slots(pad_start, e, rank, 2048)
    buf = _dispatch(pad_end.astype(I32), slot[0], slot[1], x2, nb * MOE_BLOCK, 512)
    y = _experts(blk_expert, n_used, buf, moe_w[0], moe_w[1], layer)
    x3, x3b = _combine(slot[0], slot[1], y, x2, gate.T, p["ln_ffn_g"], p["ln_ffn_b"], 256)
    return x3, x3b


def _prep_layer(l, w_in, gla_w_a2, gla_b_a, gla_norm_g, nsa_cmp_pe, nsa_cmp_w1, nsa_cmp_w2, w_branch_gla,
                w_branch_nsa, w_out, ln_mix_g, ln_mix_b, xa_wq, xa_wkv, xa_wo, ln_xa_g, ln_xa_b, router_w,
                router_b, moe_w_in, moe_w_down, ln_ffn_g, ln_ffn_b):
    d = w_in.shape[1]
    w = w_in[l]
    o_gq, o_gk, o_gv, o_gr = 0, GLA_QK, 2 * GLA_QK, 2 * GLA_QK + GLA_V
    o_ga = o_gr + GLA_V
    o_nq = o_ga + GLA_GATE_RANK
    o_nkv = o_nq + NSA_Q
    o_ng = o_nkv + 6 * NSA_KV
    o_mg = o_ng + 3 * NSA_HEADS
    G, DH = NSA_GROUPS, NSA_DH

    def kv_cols(kind):
        return w[:, o_nkv + kind * NSA_KV:o_nkv + (kind + 1) * NSA_KV]

    def slabs(wk):
        return jnp.pad(wk.reshape(d, G, DH), ((0, 0), (0, 0), (0, LANES - DH))).reshape(d, G * LANES)

    w_big = jnp.concatenate([w[:, o_mg:o_mg + 2 * d], w[:, o_gq:o_ga], slabs(kv_cols(2)), slabs(kv_cols(4))],
                            axis=1).astype(BF16)
    w_small = jnp.concatenate([w[:, o_ga:o_nq], w[:, o_ng:o_mg],
                               jnp.zeros((d, LANES - GLA_GATE_RANK - 3 * NSA_HEADS), F32),
                               kv_cols(0), kv_cols(1)], axis=1).astype(BF16)
    w_t = jnp.concatenate([w[:, o_nq:o_nkv], kv_cols(3), kv_cols(5)], axis=1).T.astype(BF16)
    w1 = nsa_cmp_w1[l].reshape(2, CMP_LEN, DH, CMP_HIDDEN)
    z1 = jnp.zeros_like(w1)
    w1bd = jnp.concatenate([jnp.concatenate([w1, z1], axis=3), jnp.concatenate([z1, w1], axis=3)], axis=2)
    w2 = nsa_cmp_w2[l]
    z2 = jnp.zeros_like(w2)
    w2bd = jnp.concatenate([jnp.concatenate([w2, z2], axis=2), jnp.concatenate([z2, w2], axis=2)], axis=1)
    pe = nsa_cmp_pe[l]
    pe_pair = jnp.broadcast_to(jnp.concatenate([pe, pe], axis=-1)[:, :, None, :], (2, CMP_LEN, 16, 2 * DH))
    wa_pad = jnp.concatenate([gla_w_a2[l], jnp.zeros((LANES - GLA_GATE_RANK, GLA_QK), F32)], axis=0).astype(BF16)
    rw_t = router_w.T
    rw_hi = rw_t.astype(BF16)
    rw_lo = (rw_t - rw_hi.astype(F32)).astype(BF16)
    return dict(
        w_big=w_big, w_small=w_small, w_t=w_t, wa_pad=wa_pad,
        b_a=gla_b_a[l].reshape(1, -1), norm_g=gla_norm_g[l].reshape(1, -1),
        cmp_w1bd=w1bd.astype(BF16), cmp_w2bd=w2bd.astype(BF16), cmp_w2bdt=w2bd.transpose(0, 2, 1).astype(BF16),
        cmp_pe_pair=pe_pair.astype(BF16),
        w_bg=w_branch_gla[l].astype(BF16), w_bn=w_branch_nsa[l].astype(BF16), w_out=w_out[l].astype(BF16),
        ln_mix_g=ln_mix_g[l].reshape(1, -1), ln_mix_b=ln_mix_b[l].reshape(1, -1),
        xa_wq=xa_wq[l].astype(BF16), xa_wkv=xa_wkv[l].astype(BF16), xa_wo=xa_wo[l].astype(BF16),
        ln_xa_g=ln_xa_g[l].reshape(1, -1), ln_xa_b=ln_xa_b[l].reshape(1, -1),
        rw_hi=rw_hi, rw_lo=rw_lo, rb=router_b.reshape(-1, 1),
        ln_ffn_g=ln_ffn_g[l].reshape(1, -1), ln_ffn_b=ln_ffn_b[l].reshape(1, -1),
    )


def kernel(x, mem, w_in, gla_w_a2, gla_b_a, gla_norm_g, nsa_cmp_pe, nsa_cmp_w1, nsa_cmp_w2, w_branch_gla, w_branch_nsa, w_out, ln_mix_g, ln_mix_b, xa_wq, xa_wkv, xa_wo, ln_xa_g, ln_xa_b, router_w, router_b, moe_w_in, moe_w_down, ln_ffn_g, ln_ffn_b):
    B, T, d = x.shape
    assert T % 512 == 0 and d == 2048 and mem.shape[1] == MEM_LEN
    n = B * T
    params = (w_in, gla_w_a2, gla_b_a, gla_norm_g, nsa_cmp_pe, nsa_cmp_w1, nsa_cmp_w2, w_branch_gla, w_branch_nsa,
              w_out, ln_mix_g, ln_mix_b, xa_wq, xa_wkv, xa_wo, ln_xa_g, ln_xa_b, router_w, router_b, moe_w_in,
              moe_w_down, ln_ffn_g, ln_ffn_b)
    slopes = (2.0 ** (-8.0 * jnp.arange(1, NSA_HEADS + 1, dtype=F32) / NSA_HEADS)).astype(F32)
    nc, ns = T // CMP_STRIDE, T // SEL_LEN
    cs = np.arange(nc) * CMP_STRIDE
    ss = np.arange(ns) * SEL_LEN
    ovt = ((cs[None, :] < ss[:, None] + SEL_LEN) & (cs[None, :] + CMP_LEN > ss[:, None])
           & (cs[None, :] + CMP_LEN <= T)).astype(np.float32)
    assert NSA_DH + ns <= LANES
    epad = np.zeros((T, LANES), np.float32)
    epad[np.arange(T), NSA_DH + np.arange(T) // SEL_LEN] = 1.0
    tile_ind = (np.arange(ns)[None, :] // (256 // SEL_LEN) == np.arange(T // 256)[:, None]).astype(np.float32)
    consts = (slopes, jnp.asarray(ovt, BF16), jnp.asarray(epad, BF16), jnp.asarray(tile_ind, BF16))

    xf = x.reshape(n, d)
    xb = xf.astype(BF16)
    mem_b = mem.reshape(B * MEM_LEN, d).astype(BF16)
    moe_w = (moe_w_in, moe_w_down)
    for l in range(DEPTH):
        p = _prep_layer(l, *params)
        xf, xb = _layer(xf, xb, mem_b, p, moe_w, l, consts, B, T)
    return xf.reshape(B, T, d)
```

```python
import functools

import jax
import jax.numpy as jnp
import numpy as np
from jax import lax
from jax.experimental import pallas as pl
from jax.experimental.pallas import tpu as pltpu

F32 = jnp.float32
BF16 = jnp.bfloat16
I32 = jnp.int32

DEPTH = 2
MEM_LEN = 256
GLA_HEADS = 4
GLA_DK = 128
GLA_DV = 256
GLA_GATE_RANK = 16
GLA_TAU = 16.0
GLA_CHUNK = 64
NSA_HEADS = 16
NSA_GROUPS = 4
NSA_HPG = NSA_HEADS // NSA_GROUPS
NSA_DH = 64
CMP_LEN = 32
CMP_STRIDE = 16
CMP_HIDDEN = 256
SEL_LEN = 64
SEL_TOPN = 8
WINDOW = 512
XA_HEADS = 4
XA_DH = 128
N_EXPERTS = 16
N_GROUPS = 4
EXPERTS_PER_GROUP = N_EXPERTS // N_GROUPS
TOP_K = 2
D_FF = 1536
DN_ALPHA = float((2 * DEPTH) ** 0.25)
LN_EPS = 1e-5
NEG = -1e30
LOG2E = 1.4426950408889634
FORCE_BONUS = 1e6

GLA_QK = GLA_HEADS * GLA_DK
GLA_V = GLA_HEADS * GLA_DV
NSA_Q = NSA_HEADS * NSA_DH
NSA_KV = NSA_GROUPS * NSA_DH

LANES = 128
VMEM_LIMIT = 56 * 1024 * 1024

COL_MG = 0
COL_GQ = 2 * 2048
COL_GK = COL_GQ + GLA_QK
COL_GV = COL_GK + GLA_QK
COL_GR = COL_GV + GLA_V
COL_KS = COL_GR + GLA_V
COL_KW = COL_KS + NSA_GROUPS * LANES
COL_END = COL_KW + NSA_GROUPS * LANES
SCOL_CK = LANES
SCOL_CV = SCOL_CK + NSA_KV
SCOL_END = SCOL_CV + NSA_KV
TROW_Q = 0
TROW_VS = NSA_Q
TROW_VW = TROW_VS + NSA_KV
TROW_END = TROW_VW + NSA_KV

MOE_BLOCK = 512
FF_TILE = 512


def _cp(sem):
    return pltpu.CompilerParams(dimension_semantics=sem, vmem_limit_bytes=VMEM_LIMIT)


def _dot(a, b):
    return jnp.dot(a, b, preferred_element_type=F32)


def _dot_nt(a, b):
    return lax.dot_general(a, b, (((1,), (1,)), ((), ())), preferred_element_type=F32)


def _dot_tn(a, b):
    return lax.dot_general(a, b, (((0,), (0,)), ((), ())), preferred_element_type=F32)


def _layer_norm(z, g, b):
    mu = jnp.mean(z, axis=-1, keepdims=True)
    zc = z - mu
    var = jnp.mean(zc * zc, axis=-1, keepdims=True)
    return zc * lax.rsqrt(var + LN_EPS) * g + b


def _mm_kernel(a_ref, b_ref, o_ref):
    o_ref[...] = _dot(a_ref[...], b_ref[...]).astype(o_ref.dtype)


def _matmul(a, b, out_dtype, tm, tn):
    m, k = a.shape
    n = b.shape[1]
    return pl.pallas_call(
        _mm_kernel,
        grid=(m // tm, n // tn),
        in_specs=[pl.BlockSpec((tm, k), lambda i, j: (i, 0)),
                  pl.BlockSpec((k, tn), lambda i, j: (0, j))],
        out_specs=pl.BlockSpec((tm, tn), lambda i, j: (i, j)),
        out_shape=jax.ShapeDtypeStruct((m, n), out_dtype),
        compiler_params=_cp(("parallel", "parallel")),
        name="matmul",
    )(a, b)


def _mm_nt_kernel(wt_ref, x_ref, o_ref):
    o_ref[0] = _dot_nt(wt_ref[...], x_ref[...]).astype(o_ref.dtype)


def _matmul_t(x, wt, B, T, tm, tr):
    n, k = x.shape
    r = wt.shape[0]
    nt = T // tm
    return pl.pallas_call(
        _mm_nt_kernel,
        grid=(n // tm, r // tr),
        in_specs=[pl.BlockSpec((tr, k), lambda i, j: (j, 0)),
                  pl.BlockSpec((tm, k), lambda i, j: (i, 0))],
        out_specs=pl.BlockSpec((1, tr, tm), lambda i, j: (i // nt, j, i % nt)),
        out_shape=jax.ShapeDtypeStruct((B, r, T), BF16),
        compiler_params=_cp(("parallel", "parallel")),
        name="matmul_t",
    )(wt, x)


def _gla_kernel(q_ref, k_ref, v_ref, r_ref, sm_ref, wa_ref, ba_ref, ng_ref, o_ref, st_ref):
    C = GLA_CHUNK
    n_chunks = q_ref.shape[0] // C
    st_ref[...] = jnp.zeros_like(st_ref)
    rowi = lax.broadcasted_iota(I32, (C, GLA_DK), 0)
    tt = lax.broadcasted_iota(I32, (C, C), 0)
    ss = lax.broadcasted_iota(I32, (C, C), 1)
    levels = (1, 2, 4, 8, 16, 32)
    pair_masks = [((tt // (2 * L)) == (ss // (2 * L))) & ((tt & L) != 0) & ((ss & L) == 0) for L in levels]
    diag_mask = tt == ss
    scale = GLA_DK ** -0.5

    def head_chunk(rows, h, z):
        qk_cols = slice(h * GLA_DK, (h + 1) * GLA_DK)
        v_cols = slice(h * GLA_DV, (h + 1) * GLA_DV)
        q = q_ref[rows, qk_cols].astype(F32) * scale
        k = k_ref[rows, qk_cols].astype(F32)
        v = v_ref[rows, v_cols]
        g = (jnp.minimum(z, 0.0) - jnp.log1p(jnp.exp(-jnp.abs(z)))) * (1.0 / GLA_TAU)
        incl = g
        tot = g
        att = jnp.where(diag_mask, _dot_nt(q.astype(BF16), k.astype(BF16)), 0.0)
        for L, pm in zip(levels, pair_masks):
            ql = (q * jnp.exp(incl)).astype(BF16)
            kl = (k * jnp.exp(tot - incl)).astype(BF16)
            att = jnp.where(pm, _dot_nt(ql, kl), att)
            upper = (rowi & L) != 0
            from_lower = pltpu.roll(tot, L, 0)
            from_upper = pltpu.roll(tot, C - L, 0)
            incl = incl + jnp.where(upper, from_lower, 0.0)
            tot = tot + jnp.where(upper, from_lower, from_upper)
        qd = (q * jnp.exp(incl)).astype(BF16)
        kd = (k * jnp.exp(tot - incl)).astype(BF16)
        st = st_ref[h]
        o = _dot_nt(qd, st.astype(BF16)) + _dot(att.astype(BF16), v)
        st_ref[h] = st * jnp.exp(tot[0:1, :]) + _dot_tn(v, kd)
        mu = jnp.mean(o, axis=-1, keepdims=True)
        oc = o - mu
        var = jnp.mean(oc * oc, axis=-1, keepdims=True)
        on = oc * lax.rsqrt(var + LN_EPS) * ng_ref[:, v_cols]
        r = r_ref[rows, v_cols].astype(F32)
        o_ref[rows, v_cols] = (on * (r * jax.nn.sigmoid(r))).astype(o_ref.dtype)

    def chunk(c, carry):
        rows = pl.ds(pl.multiple_of(c * C, C), C)
        z = _dot(sm_ref[rows, :].astype(BF16), wa_ref[...]) + ba_ref[...]
        for h in range(GLA_HEADS):
            head_chunk(rows, h, z[:, h * GLA_DK:(h + 1) * GLA_DK])
        return carry

    lax.fori_loop(0, n_chunks, chunk, 0)


def _gla(h_big, h_small, wa_pad, b_a, norm_g, B, T):
    n = B * T
    return pl.pallas_call(
        _gla_kernel,
        grid=(B,),
        in_specs=[
            pl.BlockSpec((T, GLA_QK), lambda b: (b, COL_GQ // GLA_QK)),
            pl.BlockSpec((T, GLA_QK), lambda b: (b, COL_GK // GLA_QK)),
            pl.BlockSpec((T, GLA_V), lambda b: (b, COL_GV // GLA_V)),
            pl.BlockSpec((T, GLA_V), lambda b: (b, COL_GR // GLA_V)),
            pl.BlockSpec((T, LANES), lambda b: (b, 0)),
            pl.BlockSpec((LANES, GLA_QK), lambda b: (0, 0)),
            pl.BlockSpec((1, GLA_QK), lambda b: (0, 0)),
            pl.BlockSpec((1, GLA_V), lambda b: (0, 0)),
        ],
        out_specs=pl.BlockSpec((T, GLA_V), lambda b: (b, 0)),
        out_shape=jax.ShapeDtypeStruct((n, GLA_V), BF16),
        scratch_shapes=[pltpu.VMEM((GLA_HEADS, GLA_DV, GLA_DK), F32)],
        compiler_params=_cp(("parallel",)),
        name="gla",
    )(h_big, h_big, h_big, h_big, h_small, wa_pad, b_a, norm_g)


def _compress_kernel(x_ref, w1_ref, w2_ref, w2t_ref, pe_ref, o_ref, ot_ref):
    nc = x_ref.shape[0] // CMP_STRIDE
    hid_w = w1_ref.shape[3]
    a = jnp.zeros((nc, hid_w), F32)
    bm = jnp.zeros((nc, hid_w), F32)
    c = jnp.zeros((pe_ref.shape[2], hid_w), F32)
    for l in range(CMP_STRIDE):
        xl = x_ref[pl.ds(l, nc, stride=CMP_STRIDE), :].astype(BF16)
        a = a + _dot(xl, w1_ref[0, l])
        bm = bm + _dot(xl, w1_ref[0, CMP_STRIDE + l])
    for l in range(CMP_LEN):
        c = c + _dot(pe_ref[0, l], w1_ref[0, l])
    hid = a + pltpu.roll(bm, nc - 1, 0) + c[0:1, :]
    act = jax.nn.gelu(hid).astype(BF16)
    o_ref[0, 0, 0] = _dot(act, w2_ref[0]).astype(o_ref.dtype)
    ot_ref[0, 0, 0] = _dot_nt(w2t_ref[0], act).astype(ot_ref.dtype)


def _compress(h_small, w1bd, w2bd, w2bdt, pe_pair, B, T):
    nc = T // CMP_STRIDE
    pairs = NSA_GROUPS // 2
    return pl.pallas_call(
        _compress_kernel,
        grid=(B, 2, pairs),
        in_specs=[
            pl.BlockSpec((T, LANES), lambda b, s, j: (b, SCOL_CK // LANES + s * pairs + j)),
            pl.BlockSpec((1,) + w1bd.shape[1:], lambda b, s, j: (s, 0, 0, 0)),
            pl.BlockSpec((1,) + w2bd.shape[1:], lambda b, s, j: (s, 0, 0)),
            pl.BlockSpec((1,) + w2bdt.shape[1:], lambda b, s, j: (s, 0, 0)),
            pl.BlockSpec((1,) + pe_pair.shape[1:], lambda b, s, j: (s, 0, 0, 0)),
        ],
        out_specs=[
            pl.BlockSpec((1, 1, 1, nc, LANES), lambda b, s, j: (b, s, j, 0, 0)),
            pl.BlockSpec((1, 1, 1, LANES, nc), lambda b, s, j: (b, s, j, 0, 0)),
        ],
        out_shape=[jax.ShapeDtypeStruct((B, 2, pairs, nc, LANES), BF16),
                   jax.ShapeDtypeStruct((B, 2, pairs, LANES, nc), BF16)],
        compiler_params=_cp(("parallel", "parallel", "parallel")),
        name="nsa_compress",
    )(h_small, w1bd, w2bd, w2bdt, pe_pair)


def _cmp_select_kernel(slopes_ref, qt_ref, kc_ref, vct_ref, ovt_ref, ind_ref, ocmp_ref, mb_ref, kt_ref, qpad_ref):
    g = pl.program_id(1)
    i = pl.program_id(2)
    tq = qt_ref.shape[2]
    nc = kc_ref.shape[3]
    ns = mb_ref.shape[2]
    dh = NSA_DH
    t0 = i * tq
    tpos = (t0 + lax.broadcasted_iota(I32, (nc, tq), 1))
    nidx = lax.broadcasted_iota(I32, (nc, tq), 0)
    mask_c = (nidx * CMP_STRIDE + (CMP_LEN - 1)) <= tpos
    absd = jnp.abs(tpos.astype(F32) - (nidx.astype(F32) * CMP_STRIDE + 0.5 * (CMP_LEN - 1)))
    lower = g % 2 == 0
    kc = kc_ref[0, 0, 0]
    vct = jnp.where(lower, vct_ref[0, 0, 0, 0:dh, :], vct_ref[0, 0, 0, dh:2 * dh, :])
    psum = jnp.zeros((nc, tq), F32)
    for hh in range(NSA_HPG):
        slope = slopes_ref[g * NSA_HPG + hh]
        q = qt_ref[0, hh * dh:(hh + 1) * dh, :] * jnp.asarray(dh ** -0.5, BF16)
        zero = jnp.zeros_like(q)
        qpad_ref[0:dh, :] = jnp.where(lower, q, zero)
        qpad_ref[dh:2 * dh, :] = jnp.where(lower, zero, q)
        s = _dot(kc, qpad_ref[...]) - slope * absd
        s = jnp.where(mask_c, s, NEG)
        e = jnp.exp(s - jnp.max(s, axis=0, keepdims=True))
        p = jnp.where(mask_c, e / jnp.sum(e, axis=0, keepdims=True), 0.0)
        ocmp_ref[0, hh * dh:(hh + 1) * dh, :] = _dot(vct, p.astype(BF16)).astype(ocmp_ref.dtype)
        psum = psum + p
    p_hi = psum.astype(BF16)
    p_lo = (psum - p_hi.astype(F32)).astype(BF16)
    imp = _dot(ovt_ref[...], p_hi) + _dot(ovt_ref[...], p_lo)
    j = lax.broadcasted_iota(I32, (ns, tq), 0)
    tp = t0 + lax.broadcasted_iota(I32, (ns, tq), 1)
    cur = tp // SEL_LEN
    forced = (j == 0) | (j == cur) | (j == cur - 1)
    valid = j * SEL_LEN <= tp
    score = jnp.where(valid, imp + jnp.where(forced, FORCE_BONUS, 0.0), NEG)
    rank = jnp.zeros((ns, tq), F32)
    for jp in range(ns):
        row = score[jp:jp + 1, :]
        beats = (row > score) | ((row == score) & (j > jp))
        rank = rank + jnp.where(beats, 1.0, 0.0)
    keep = valid & (rank < float(min(SEL_TOPN, ns)))
    mb_ref[0, 0] = jnp.where(keep, 0.0, NEG).astype(mb_ref.dtype)
    kt_ref[0, 0] = _dot(ind_ref[...], jnp.where(keep, 1.0, 0.0).astype(BF16))


def _cmp_select(slopes, h_t, kcmp, kcmp_t, ovt, tile_ind, B, T, tq):
    nc = T // CMP_STRIDE
    ns = T // SEL_LEN
    nkt = tile_ind.shape[0]
    grp_rows = NSA_HPG * NSA_DH
    grid_spec = pltpu.PrefetchScalarGridSpec(
        num_scalar_prefetch=1,
        grid=(B, NSA_GROUPS, T // tq),
        in_specs=[
            pl.BlockSpec((1, grp_rows, tq), lambda b, g, i, s: (b, TROW_Q // grp_rows + g, i)),
            pl.BlockSpec((1, 1, 1, nc, LANES), lambda b, g, i, s: (b, 0, g // 2, 0, 0)),
            pl.BlockSpec((1, 1, 1, LANES, nc), lambda b, g, i, s: (b, 1, g // 2, 0, 0)),
            pl.BlockSpec((ns, nc), lambda b, g, i, s: (0, 0)),
            pl.BlockSpec((nkt, ns), lambda b, g, i, s: (0, 0)),
        ],
        out_specs=[
            pl.BlockSpec((1, grp_rows, tq), lambda b, g, i, s: (b, g, i)),
            pl.BlockSpec((1, 1, ns, tq), lambda b, g, i, s: (b, g, 0, i)),
            pl.BlockSpec((1, 1, nkt, tq), lambda b, g, i, s: (b, g, 0, i)),
        ],
        scratch_shapes=[pltpu.VMEM((LANES, tq), BF16)],
    )
    return pl.pallas_call(
        _cmp_select_kernel,
        grid_spec=grid_spec,
        out_shape=[jax.ShapeDtypeStruct((B, NSA_Q, T), BF16),
                   jax.ShapeDtypeStruct((B, NSA_GROUPS, ns, T), BF16),
                   jax.ShapeDtypeStruct((B, NSA_GROUPS, nkt, T), F32)],
        compiler_params=_cp(("parallel", "parallel", "parallel")),
        name="nsa_cmp_select",
    )(slopes, h_t, kcmp, kcmp_t, ovt, tile_ind)


def _sel_win_kernel(slopes_ref, flags_ref, qt_ref, ks_ref, kw_ref, vs_ref, vw_ref, epad_ref, mb_ref, ocmp_ref, gt_ref,
                    o_ref, qaug_ref, m_ref, acc_ref, srow_ref, bias_ref, s_ref, p_ref, alpha_ref,
                    kall_ref, vall_ref, tiles_ref, *, n_tiles):
    g = pl.program_id(1)
    i = pl.program_id(2)
    tq = qt_ref.shape[2]
    tk = tq
    ns = mb_ref.shape[2]
    dh = NSA_DH
    wide = NSA_HPG * tq
    t0 = i * tq
    BIG = -NEG

    @pl.when(i == 0)
    def _():
        srow = jnp.concatenate([jnp.full((1, tq), slopes_ref[g * NSA_HPG + hh] * LOG2E, F32)
                                for hh in range(NSA_HPG)], axis=1)
        srow_ref[...] = srow
        lane = lax.broadcasted_iota(I32, (tk, wide), 1) & (tq - 1)
        dist0 = (lane - lax.broadcasted_iota(I32, (tk, wide), 0)).astype(F32)
        sd0 = srow * dist0
        bias_ref[0] = sd0
        bias_ref[1] = sd0 + jnp.where(dist0 >= 0.0, 0.0, BIG)
        bias_ref[2] = sd0 + jnp.where(dist0 < 0.0, 0.0, BIG)
        bias_ref[3] = jnp.full((tk, wide), BIG, F32)
        kall_ref[0] = ks_ref[...] + epad_ref[...]
        kall_ref[1] = kw_ref[...]
        extra = jnp.where(lax.broadcasted_iota(I32, (vall_ref.shape[1] - dh, vall_ref.shape[2]), 0) == 0, 1.0, 0.0)
        vall_ref[0, 0:dh, :] = vs_ref[0]
        vall_ref[1, 0:dh, :] = vw_ref[0]
        vall_ref[0, dh:, :] = extra.astype(BF16)
        vall_ref[1, dh:, :] = extra.astype(BF16)

    for hh in range(NSA_HPG):
        cols = slice(hh * tq, (hh + 1) * tq)
        q = qt_ref[0, hh * dh:(hh + 1) * dh, :].astype(F32) * (dh ** -0.5 * LOG2E)
        qaug_ref[0:dh, cols] = q.astype(BF16)
        qaug_ref[dh:dh + ns, cols] = mb_ref[0, 0]
        qaug_ref[dh + ns:, cols] = jnp.zeros((qaug_ref.shape[0] - dh - ns, tq), BF16)

    m_ref[...] = jnp.full(m_ref.shape, NEG, F32)
    acc_ref[...] = jnp.zeros(acc_ref.shape, F32)

    n_back = WINDOW // tk
    flag_base = ((pl.program_id(0) * NSA_GROUPS + g) * n_tiles + i) * n_tiles
    n_sel = jnp.int32(0)
    for kb_static in range(n_tiles - 1):
        active = (kb_static < i) & (flags_ref[flag_base + kb_static] != 0)

        @pl.when(active)
        def _(kb_static=kb_static, n_sel=n_sel):
            tiles_ref[n_sel] = kb_static

        n_sel = n_sel + active.astype(I32)
    tiles_ref[n_sel] = i
    n_sel = n_sel + 1
    n_win = jnp.minimum(i, n_back) + 1
    n_steps = n_sel + n_win

    def describe(n):
        n = jnp.maximum(n, 0)
        is_win = n >= n_sel
        kb_sel = tiles_ref[jnp.minimum(n, n_sel - 1)]
        kb = jnp.clip(jnp.where(is_win, i - n_win + 1 + (n - n_sel), kb_sel), 0, i)
        mode = jnp.where(kb == i, 1, jnp.where(is_win & (kb == i - n_back), 2, 0))
        mode = jnp.where(n >= n_steps, 3, mode)
        return is_win.astype(I32), kb, mode

    def scores(n, slot):
        br, kb, _ = describe(n)
        s0 = pl.multiple_of(kb * tk, tk)
        s_ref[slot] = _dot(kall_ref[br, pl.ds(s0, tk), :], qaug_ref[...])

    def softmax(n, slot):
        br, kb, mode = describe(n)
        crow = srow_ref[...] * ((i - kb) * tk).astype(F32)
        s = s_ref[slot] - bias_ref[mode]
        m_old = m_ref[br]
        m_new = jnp.maximum(m_old, jnp.max(s, axis=0, keepdims=True) - crow)
        alpha = jnp.exp2(m_old - m_new)
        p = jnp.exp2(s - (m_new + crow))
        m_ref[br] = m_new
        alpha_ref[slot] = alpha
        p_ref[slot] = p.astype(BF16)

    def weighted_values(n, slot):
        br, kb, _ = describe(n)
        s0 = pl.multiple_of(kb * tk, tk)
        acc_ref[br] = alpha_ref[slot] * acc_ref[br] + _dot(vall_ref[br, :, pl.ds(s0, tk)], p_ref[slot])

    p_ref[1] = jnp.zeros(p_ref.shape[1:], BF16)
    alpha_ref[1] = jnp.ones(alpha_ref.shape[1:], F32)
    scores(0, 0)

    def pair(j, carry):
        n = 2 * j
        scores(n + 1, 1)
        softmax(n, 0)
        weighted_values(n - 1, 1)
        scores(n + 2, 0)
        softmax(n + 1, 1)
        weighted_values(n, 0)
        return carry

    n_pairs = n_steps // 2
    lax.fori_loop(0, n_pairs, pair, 0)
    weighted_values(2 * n_pairs - 1, 1)

    @pl.when(n_steps % 2 == 1)
    def _():
        softmax(n_steps - 1, 0)
        weighted_values(n_steps - 1, 0)

    def gate_row(branch):
        rows = [gt_ref[0, 0, 3 * hh + branch:3 * hh + branch + 1, :] for hh in range(NSA_HPG)]
        return jax.nn.sigmoid(jnp.concatenate(rows, axis=1))

    o = (acc_ref[0, 0:dh, :] * (gate_row(1) / acc_ref[0, dh:dh + 1, :])
         + acc_ref[1, 0:dh, :] * (gate_row(2) / acc_ref[1, dh:dh + 1, :]))
    ocmp = jnp.concatenate([ocmp_ref[0, hh * dh:(hh + 1) * dh, :] for hh in range(NSA_HPG)], axis=1).astype(F32)
    o = o + gate_row(0) * ocmp
    o_heads = jnp.concatenate([o[:, hh * tq:(hh + 1) * tq] for hh in range(NSA_HPG)], axis=0)
    o_ref[...] = o_heads.T.astype(o_ref.dtype)


def _sel_win(slopes, tile_flags, h_t, h_big, epad, mb, ocmp_t, gates_t, B, T, tq):
    ns = T // SEL_LEN
    kaug = LANES
    G = NSA_GROUPS
    grp_rows = NSA_HPG * NSA_DH
    nq = T // tq
    grid_spec = pltpu.PrefetchScalarGridSpec(
        num_scalar_prefetch=2,
        grid=(B, G, nq),
        in_specs=[
            pl.BlockSpec((1, grp_rows, tq), lambda b, g, i, s, f: (b, TROW_Q // grp_rows + g, i)),
            pl.BlockSpec((T, LANES), lambda b, g, i, s, f: (b, COL_KS // LANES + g)),
            pl.BlockSpec((T, LANES), lambda b, g, i, s, f: (b, COL_KW // LANES + g)),
            pl.BlockSpec((1, NSA_DH, T), lambda b, g, i, s, f: (b, TROW_VS // NSA_DH + g, 0)),
            pl.BlockSpec((1, NSA_DH, T), lambda b, g, i, s, f: (b, TROW_VW // NSA_DH + g, 0)),
            pl.BlockSpec((T, LANES), lambda b, g, i, s, f: (0, 0)),
            pl.BlockSpec((1, 1, ns, tq), lambda b, g, i, s, f: (b, g, 0, i)),
            pl.BlockSpec((1, grp_rows, tq), lambda b, g, i, s, f: (b, g, i)),
            pl.BlockSpec((1, 1, 16, tq), lambda b, g, i, s, f: (b, g, 0, i)),
        ],
        out_specs=pl.BlockSpec((tq, grp_rows), lambda b, g, i, s, f: (b * nq + i, g)),
        scratch_shapes=[
            pltpu.VMEM((kaug, NSA_HPG * tq), BF16),
            pltpu.VMEM((2, 1, NSA_HPG * tq), F32),
            pltpu.VMEM((2, NSA_DH + 16, NSA_HPG * tq), F32),
            pltpu.VMEM((1, NSA_HPG * tq), F32),
            pltpu.VMEM((4, tq, NSA_HPG * tq), F32),
            pltpu.VMEM((2, tq, NSA_HPG * tq), F32),
            pltpu.VMEM((2, tq, NSA_HPG * tq), BF16),
            pltpu.VMEM((2, 1, NSA_HPG * tq), F32),
            pltpu.VMEM((2, T, kaug), BF16),
            pltpu.VMEM((2, NSA_DH + 16, T), BF16),
            pltpu.SMEM((nq,), I32),
        ],
    )
    return pl.pallas_call(
        functools.partial(_sel_win_kernel, n_tiles=nq),
        grid_spec=grid_spec,
        out_shape=jax.ShapeDtypeStruct((B * T, NSA_Q), BF16),
        compiler_params=_cp(("parallel", "parallel", "arbitrary")),
        name="nsa_sel_win",
    )(slopes, tile_flags, h_t, h_big, h_big, h_t, h_t, epad, mb, ocmp_t, gates_t)


def _mix_kernel(og_ref, on_ref, mg1_ref, mg2_ref, x_ref, wg_ref, wn_ref, wo_ref, lg_ref, lb_ref,
                x1_ref, x1b_ref):
    g1 = _dot(og_ref[...], wg_ref[...])
    g2 = _dot(on_ref[...], wn_ref[...])
    merged = (jax.nn.sigmoid(mg1_ref[...].astype(F32)) * g1
              + jax.nn.sigmoid(mg2_ref[...].astype(F32)) * g2)
    y = _dot(merged.astype(BF16), wo_ref[...])
    x1 = _layer_norm(DN_ALPHA * x_ref[...] + y, lg_ref[...], lb_ref[...])
    x1_ref[...] = x1
    x1b_ref[...] = x1.astype(BF16)


def _const_spec(shape):
    nd = len(shape)
    return pl.BlockSpec(shape, lambda *_: (0,) * nd, pipeline_mode=pl.Buffered(1))


def _mix(o_gla, o_nsa, h_big, x, wg, wn, wo, lg, lb, tm):
    n, d = x.shape
    return pl.pallas_call(
        _mix_kernel,
        grid=(n // tm,),
        in_specs=[
            pl.BlockSpec((tm, GLA_V), lambda i: (i, 0)),
            pl.BlockSpec((tm, NSA_Q), lambda i: (i, 0)),
            pl.BlockSpec((tm, d), lambda i: (i, 0)),
            pl.BlockSpec((tm, d), lambda i: (i, 1)),
            pl.BlockSpec((tm, d), lambda i: (i, 0)),
            _const_spec(wg.shape), _const_spec(wn.shape), _const_spec(wo.shape),
            _const_spec(lg.shape), _const_spec(lb.shape),
        ],
        out_specs=[pl.BlockSpec((tm, d), lambda i: (i, 0)), pl.BlockSpec((tm, d), lambda i: (i, 0))],
        out_shape=[jax.ShapeDtypeStruct((n, d), F32), jax.ShapeDtypeStruct((n, d), BF16)],
        compiler_params=_cp(("parallel",)),
        name="mix_ln",
    )(o_gla, o_nsa, h_big, h_big, x, wg, wn, wo, lg, lb)


def _xattn_kernel(x_ref, xb_ref, kv_ref, wq_ref, wo_ref, lg_ref, lb_ref, x2_ref):
    hd = XA_HEADS * XA_DH
    q = (_dot(xb_ref[...], wq_ref[...]) * (XA_DH ** -0.5)).astype(BF16)
    outs = []
    for h in range(XA_HEADS):
        kh = kv_ref[0, :, h * XA_DH:(h + 1) * XA_DH]
        vh = kv_ref[0, :, hd + h * XA_DH:hd + (h + 1) * XA_DH]
        s = _dot_nt(q[:, h * XA_DH:(h + 1) * XA_DH], kh)
        e = jnp.exp(s - jnp.max(s, axis=-1, keepdims=True))
        p = e / jnp.sum(e, axis=-1, keepdims=True)
        outs.append(_dot(p.astype(BF16), vh).astype(BF16))
    o = jnp.concatenate(outs, axis=-1)
    y = _dot(o, wo_ref[...])
    x2_ref[...] = _layer_norm(DN_ALPHA * x_ref[...] + y, lg_ref[...], lb_ref[...])


def _xattn(x1, x1b, kv, wq, wo, lg, lb, B, T, tm):
    n, d = x1.shape
    nt = T // tm
    return pl.pallas_call(
        _xattn_kernel,
        grid=(B, nt),
        in_specs=[
            pl.BlockSpec((tm, d), lambda b, i: (b * nt + i, 0)),
            pl.BlockSpec((tm, d), lambda b, i: (b * nt + i, 0)),
            pl.BlockSpec((1,) + kv.shape[1:], lambda b, i: (b, 0, 0)),
            _const_spec(wq.shape), _const_spec(wo.shape), _const_spec(lg.shape), _const_spec(lb.shape),
        ],
        out_specs=pl.BlockSpec((tm, d), lambda b, i: (b * nt + i, 0)),
        out_shape=jax.ShapeDtypeStruct((n, d), F32),
        compiler_params=_cp(("parallel", "parallel")),
        name="xattn_ln",
    )(x1, x1b, kv, wq, wo, lg, lb)


def _router_kernel(x_ref, wh_ref, wl_ref, rb_ref, e_ref, gate_ref, rank_ref, cnt_ref, carry_ref):
    i = pl.program_id(0)
    tr = x_ref.shape[0]
    E = N_EXPERTS

    @pl.when(i == 0)
    def _():
        carry_ref[...] = jnp.zeros_like(carry_ref)

    x = x_ref[...]
    x_hi = x.astype(BF16)
    x_lo = (x - x_hi.astype(F32)).astype(BF16)
    wh = wh_ref[...]
    logits = _dot_nt(wh, x_hi) + _dot_nt(wh, x_lo) + _dot_nt(wl_ref[...], x_hi)
    biased = logits + rb_ref[...]
    rows = [biased[e:e + 1, :] for e in range(E)]
    raw = [logits[e:e + 1, :] for e in range(E)]
    best_score = None
    best = None
    for gi in range(N_GROUPS):
        v = rows[gi * EXPERTS_PER_GROUP:(gi + 1) * EXPERTS_PER_GROUP]
        sc = None
        for a in range(EXPERTS_PER_GROUP):
            for b in range(a + 1, EXPERTS_PER_GROUP):
                pair = v[a] + v[b]
                sc = pair if sc is None else jnp.maximum(sc, pair)
        if best is None:
            best_score, best = sc, jnp.zeros((1, tr), I32)
        else:
            better = sc > best_score
            best_score = jnp.where(better, sc, best_score)
            best = jnp.where(better, gi, best)

    def pick(vals):
        out = vals[0:EXPERTS_PER_GROUP]
        for gi in range(1, N_GROUPS):
            out = [jnp.where(best == gi, vals[gi * EXPERTS_PER_GROUP + a], out[a]) for a in range(EXPERTS_PER_GROUP)]
        return out

    w = pick(rows)
    lraw = pick(raw)
    i1 = jnp.zeros((1, tr), I32)
    v1 = w[0]
    l1 = lraw[0]
    for a in range(1, EXPERTS_PER_GROUP):
        better = w[a] > v1
        v1 = jnp.where(better, w[a], v1)
        l1 = jnp.where(better, lraw[a], l1)
        i1 = jnp.where(better, a, i1)
    i2 = jnp.full((1, tr), -1, I32)
    v2 = jnp.full((1, tr), -jnp.inf, F32)
    l2 = jnp.zeros((1, tr), F32)
    for a in range(EXPERTS_PER_GROUP):
        better = (i1 != a) & ((w[a] > v2) | (i2 < 0))
        v2 = jnp.where(better, w[a], v2)
        l2 = jnp.where(better, lraw[a], l2)
        i2 = jnp.where(better, a, i2)
    e1 = best * EXPERTS_PER_GROUP + i1
    e2 = best * EXPERTS_PER_GROUP + i2
    mx = jnp.maximum(l1, l2)
    p1 = jnp.exp(l1 - mx)
    p2 = jnp.exp(l2 - mx)
    den = p1 + p2
    e_ref[0:1, :] = e1
    e_ref[1:2, :] = e2
    gate_ref[0:1, :] = p1 / den
    gate_ref[1:2, :] = p2 / den
    eidx = lax.broadcasted_iota(I32, (E, tr), 0)
    is1 = eidx == e1
    is2 = eidx == e2
    member = jnp.where(is1 | is2, 1.0, 0.0)
    uu = lax.broadcasted_iota(I32, (tr, tr), 0)
    tt = lax.broadcasted_iota(I32, (tr, tr), 1)
    tri = jnp.where(uu <= tt, 1.0, 0.0).astype(BF16)
    incl = _dot(member.astype(BF16), tri)
    excl = carry_ref[:, 0:1] + incl - member
    rank_ref[0:1, :] = jnp.sum(jnp.where(is1, excl, 0.0), axis=0, keepdims=True).astype(I32)
    rank_ref[1:2, :] = jnp.sum(jnp.where(is2, excl, 0.0), axis=0, keepdims=True).astype(I32)
    new_carry = carry_ref[...] + jnp.sum(member, axis=1, keepdims=True)
    carry_ref[...] = new_carry
    cnt_ref[...] = new_carry


def _router(x2, rw_hi, rw_lo, rb, tr):
    n, d = x2.shape
    E = N_EXPERTS
    return pl.pallas_call(
        _router_kernel,
        grid=(n // tr,),
        in_specs=[
            pl.BlockSpec((tr, d), lambda i: (i, 0)),
            pl.BlockSpec((E, d), lambda i: (0, 0)),
            pl.BlockSpec((E, d), lambda i: (0, 0)),
            pl.BlockSpec((E, 1), lambda i: (0, 0)),
        ],
        out_specs=[
            pl.BlockSpec((2, tr), lambda i: (0, i)),
            pl.BlockSpec((2, tr), lambda i: (0, i)),
            pl.BlockSpec((2, tr), lambda i: (0, i)),
            pl.BlockSpec((E, LANES), lambda i: (0, 0)),
        ],
        out_shape=[jax.ShapeDtypeStruct((2, n), I32), jax.ShapeDtypeStruct((2, n), F32),
                   jax.ShapeDtypeStruct((2, n), I32), jax.ShapeDtypeStruct((E, LANES), F32)],
        scratch_shapes=[pltpu.VMEM((E, LANES), F32)],
        compiler_params=_cp(("arbitrary",)),
        name="moe_router",
    )(x2, rw_hi, rw_lo, rb)


def _slot_kernel(ps_ref, e_ref, rank_ref, slot_ref):
    e = e_ref[...]
    start = jnp.zeros(e.shape, I32)
    for ex in range(N_EXPERTS):
        start = jnp.where(e == ex, ps_ref[ex], start)
    slot_ref[...] = start + rank_ref[...]


def _slots(pad_start, e, rank, ts):
    n = e.shape[1]
    grid_spec = pltpu.PrefetchScalarGridSpec(
        num_scalar_prefetch=1,
        grid=(n // ts,),
        in_specs=[pl.BlockSpec((TOP_K, ts), lambda i, s: (0, i)), pl.BlockSpec((TOP_K, ts), lambda i, s: (0, i))],
        out_specs=pl.BlockSpec((TOP_K, ts), lambda i, s: (0, i)),
    )
    return pl.pallas_call(
        _slot_kernel,
        grid_spec=grid_spec,
        out_shape=jax.ShapeDtypeStruct((TOP_K, n), I32),
        compiler_params=_cp(("parallel",)),
        name="moe_slots",
    )(pad_start, e, rank)


def _dispatch_kernel(pe_ref, s0_ref, s1_ref, x_ref, buf_hbm, zero_ref, sem):
    td = s0_ref.shape[0]
    slots = (s0_ref, s1_ref)

    @pl.when(pl.program_id(0) == 0)
    def _():
        zero_ref[...] = jnp.zeros_like(zero_ref)

        def zero_copy(ex):
            last = pl.multiple_of(jnp.maximum(pe_ref[ex] - MOE_BLOCK, 0), MOE_BLOCK)
            return pltpu.make_async_copy(zero_ref, buf_hbm.at[pl.ds(last, MOE_BLOCK), :], sem)

        def nonempty(ex):
            return pe_ref[ex] > (pe_ref[ex - 1] if ex > 0 else 0)

        n_blocks = buf_hbm.shape[0] // MOE_BLOCK
        first_unused = pe_ref[N_EXPERTS - 1] // MOE_BLOCK

        def tail_copy(k):
            row = pl.multiple_of((first_unused + k) * MOE_BLOCK, MOE_BLOCK)
            return pltpu.make_async_copy(zero_ref, buf_hbm.at[pl.ds(row, MOE_BLOCK), :], sem)

        for ex in range(N_EXPERTS):
            pl.when(nonempty(ex))(lambda ex=ex: zero_copy(ex).start())
            pl.when(first_unused + ex < n_blocks)(lambda ex=ex: tail_copy(ex).start())
        for ex in range(N_EXPERTS):
            pl.when(nonempty(ex))(lambda ex=ex: zero_copy(ex).wait())
            pl.when(first_unused + ex < n_blocks)(lambda ex=ex: tail_copy(ex).wait())

    def issue(t, carry):
        for kk in range(TOP_K):
            dest = slots[kk][t]
            pltpu.make_async_copy(x_ref.at[pl.ds(t, 1), :], buf_hbm.at[pl.ds(dest, 1), :], sem).start()
        return carry

    lax.fori_loop(0, td, issue, 0, unroll=8)
    for kk in range(TOP_K):
        pltpu.make_async_copy(x_ref, buf_hbm.at[pl.ds(0, td), :], sem).wait()


def _dispatch(pad_end, slot0, slot1, x2, n_rows, td):
    n, d = x2.shape
    grid_spec = pltpu.PrefetchScalarGridSpec(
        num_scalar_prefetch=1,
        grid=(n // td,),
        in_specs=[
            pl.BlockSpec((td,), lambda i, s: (i,), memory_space=pltpu.SMEM),
            pl.BlockSpec((td,), lambda i, s: (i,), memory_space=pltpu.SMEM),
            pl.BlockSpec((td, d), lambda i, s: (i, 0)),
        ],
        out_specs=pl.BlockSpec(memory_space=pl.ANY),
        scratch_shapes=[pltpu.VMEM((MOE_BLOCK, d), F32), pltpu.SemaphoreType.DMA(())],
    )
    return pl.pallas_call(
        _dispatch_kernel,
        grid_spec=grid_spec,
        out_shape=jax.ShapeDtypeStruct((n_rows, d), F32),
        compiler_params=_cp(("arbitrary",)),
        name="moe_dispatch",
    )(pad_end, slot0, slot1, x2)


def _expert_kernel(be_ref, nb_ref, x_ref, win_hbm, wdn_hbm, y_ref, xb_ref, wa_s, wu_s, wd_s, sa, su, sd, sems,
                   *, layer):
    b = pl.program_id(0)
    nf = D_FF // FF_TILE
    n_used = nb_ref[0]
    e = be_ref[b]
    e_prev = be_ref[jnp.maximum(b - 1, 0)]
    e_next = be_ref[jnp.minimum(b + 1, pl.num_programs(0) - 1)]
    active = b < n_used
    is_first = active & ((b == 0) | (e_prev != e))
    feeds_next = active & (b + 1 < n_used) & (e_next != e)

    def tile_copies(ex, f):
        lo = f * FF_TILE
        return (pltpu.make_async_copy(win_hbm.at[layer, ex, :, pl.ds(lo, FF_TILE)], sa, sems.at[0]),
                pltpu.make_async_copy(win_hbm.at[layer, ex, :, pl.ds(D_FF + lo, FF_TILE)], su, sems.at[1]),
                pltpu.make_async_copy(wdn_hbm.at[layer, ex, pl.ds(lo, FF_TILE), :], sd, sems.at[2]))

    def start(ex, f):
        for c in tile_copies(ex, f):
            c.start()

    def finish(ex, f):
        for c in tile_copies(ex, f):
            c.wait()
        wa_s[f] = sa[...].astype(BF16)
        wu_s[f] = su[...].astype(BF16)
        wd_s[f] = sd[...].astype(BF16)

    @pl.when(b == 0)
    def _():
        for f in range(nf - 1):
            start(e, f)
            finish(e, f)
        start(e, nf - 1)

    @pl.when(jnp.logical_not(active))
    def _():
        y_ref[...] = jnp.zeros_like(y_ref)

    @pl.when(active)
    def _():
        xb_ref[...] = x_ref[...].astype(BF16)
        for f in range(nf):
            xb = xb_ref[...]
            a = _dot(xb, wa_s[f])
            u = _dot(xb, wu_s[f])
            act = (a * jax.nn.sigmoid(a) * u).astype(BF16)
            y = _dot(act, wd_s[f])
            if f == 0:
                y_ref[...] = y
                pl.when(is_first)(lambda: finish(e, nf - 1))
            else:
                y_ref[...] += y

            @pl.when(feeds_next)
            def _(f=f):
                if f >= 1:
                    finish(e_next, f - 1)
                start(e_next, f)


def _experts(blk_expert, n_used, buf, w_in, w_down, layer):
    p, d = buf.shape
    nb = p // MOE_BLOCK
    nf = D_FF // FF_TILE
    grid_spec = pltpu.PrefetchScalarGridSpec(
        num_scalar_prefetch=2,
        grid=(nb,),
        in_specs=[
            pl.BlockSpec((MOE_BLOCK, d), lambda b, be, nu: (jnp.minimum(b, nu[0] - 1), 0)),
            pl.BlockSpec(memory_space=pl.ANY),
            pl.BlockSpec(memory_space=pl.ANY),
        ],
        out_specs=pl.BlockSpec((MOE_BLOCK, d), lambda b, be, nu: (b, 0)),
        scratch_shapes=[
            pltpu.VMEM((MOE_BLOCK, d), BF16),
            pltpu.VMEM((nf, d, FF_TILE), BF16),
            pltpu.VMEM((nf, d, FF_TILE), BF16),
            pltpu.VMEM((nf, FF_TILE, d), BF16),
            pltpu.VMEM((d, FF_TILE), F32),
            pltpu.VMEM((d, FF_TILE), F32),
            pltpu.VMEM((FF_TILE, d), F32),
            pltpu.SemaphoreType.DMA((3,)),
        ],
    )
    return pl.pallas_call(
        functools.partial(_expert_kernel, layer=layer),
        grid_spec=grid_spec,
        out_shape=jax.ShapeDtypeStruct((p, d), F32),
        compiler_params=_cp(("arbitrary",)),
        name="moe_experts",
    )(blk_expert, n_used, buf, w_in, w_down)


def _combine_kernel(s0_ref, s1_ref, n0_ref, n1_ref, y_hbm, x_ref, gate_ref, lg_ref, lb_ref, x3_ref, x3b_ref,
                    y0_ref, y1_ref, sems):
    i = pl.program_id(0)
    tc = x_ref.shape[0]
    bufs = (y0_ref, y1_ref)
    cur = i % 2

    def issue_tile(slot_refs, half):
        def issue(t, carry):
            for kk in range(TOP_K):
                src = slot_refs[kk][t]
                pltpu.make_async_copy(y_hbm.at[pl.ds(src, 1), :], bufs[kk].at[half, pl.ds(t, 1), :],
                                      sems.at[half]).start()
            return carry

        lax.fori_loop(0, tc, issue, 0, unroll=8)

    @pl.when(i == 0)
    def _():
        issue_tile((s0_ref, s1_ref), 0)

    @pl.when(i + 1 < pl.num_programs(0))
    def _():
        issue_tile((n0_ref, n1_ref), 1 - cur)

    for kk in range(TOP_K):
        pltpu.make_async_copy(y_hbm.at[pl.ds(0, tc), :], bufs[kk].at[cur], sems.at[cur]).wait()
    gate = gate_ref[...]
    z = DN_ALPHA * x_ref[...] + gate[:, 0:1] * y0_ref[cur] + gate[:, 1:2] * y1_ref[cur]
    x3 = _layer_norm(z, lg_ref[...], lb_ref[...])
    x3_ref[...] = x3
    x3b_ref[...] = x3.astype(BF16)


def _combine(slot0, slot1, y, x2, gate_nt, lg, lb, tc):
    n, d = x2.shape
    last = n // tc - 1
    return pl.pallas_call(
        _combine_kernel,
        grid=(n // tc,),
        in_specs=[
            pl.BlockSpec((tc,), lambda i: (i,), memory_space=pltpu.SMEM),
            pl.BlockSpec((tc,), lambda i: (i,), memory_space=pltpu.SMEM),
            pl.BlockSpec((tc,), lambda i: (jnp.minimum(i + 1, last),), memory_space=pltpu.SMEM),
            pl.BlockSpec((tc,), lambda i: (jnp.minimum(i + 1, last),), memory_space=pltpu.SMEM),
            pl.BlockSpec(memory_space=pl.ANY),
            pl.BlockSpec((tc, d), lambda i: (i, 0)),
            pl.BlockSpec((tc, 2), lambda i: (i, 0)),
            pl.BlockSpec((1, d), lambda i: (0, 0)),
            pl.BlockSpec((1, d), lambda i: (0, 0)),
        ],
        out_specs=[pl.BlockSpec((tc, d), lambda i: (i, 0)), pl.BlockSpec((tc, d), lambda i: (i, 0))],
        out_shape=[jax.ShapeDtypeStruct((n, d), F32), jax.ShapeDtypeStruct((n, d), BF16)],
        scratch_shapes=[pltpu.VMEM((2, tc, d), F32), pltpu.VMEM((2, tc, d), F32), pltpu.SemaphoreType.DMA((2,))],
        compiler_params=_cp(("arbitrary",)),
        name="moe_combine_ln",
    )(slot0, slot1, slot0, slot1, y, x2, gate_nt, lg, lb)


def _layer(x, xb, mem_b, p, moe_w, layer, consts, B, T):
    n, d = x.shape
    G, HPG, DH = NSA_GROUPS, NSA_HPG, NSA_DH
    slopes, ovt, epad, tile_ind = consts

    h_big = _matmul(xb, p["w_big"], BF16, 1024, 1024)
    h_small = _matmul(xb, p["w_small"], F32, 1024, SCOL_END)
    h_t = _matmul_t(xb, p["w_t"], B, T, 1024, TROW_END // 2)

    o_gla = _gla(h_big, h_small, p["wa_pad"], p["b_a"], p["norm_g"], B, T)

    kcmp, kcmp_t = _compress(h_small, p["cmp_w1bd"], p["cmp_w2bd"], p["cmp_w2bdt"], p["cmp_pe_pair"], B, T)
    tq_sel = 256
    ocmp_t, mb, in_tile = _cmp_select(slopes, h_t, kcmp, kcmp_t, ovt, tile_ind, B, T, 512)
    nq = T // tq_sel
    tile_flags = (in_tile.reshape(B, G, nq, nq, tq_sel).max(axis=-1) > 0).astype(I32)
    tile_flags = tile_flags.transpose(0, 1, 3, 2).reshape(-1)
    gates_t = h_small[:, GLA_GATE_RANK:GLA_GATE_RANK + 3 * NSA_HEADS].reshape(B, T, G, 3 * HPG)
    gates_t = jnp.pad(gates_t.transpose(0, 2, 3, 1), ((0, 0), (0, 0), (0, 16 - 3 * HPG), (0, 0)))
    o_nsa = _sel_win(slopes, tile_flags, h_t, h_big, epad, mb, ocmp_t, gates_t, B, T, tq_sel)

    x1, x1b = _mix(o_gla, o_nsa, h_big, x, p["w_bg"], p["w_bn"], p["w_out"], p["ln_mix_g"], p["ln_mix_b"], 512)

    kvm = _matmul(mem_b, p["xa_wkv"], BF16, 512, 512).reshape(B, MEM_LEN, 2 * XA_HEADS * XA_DH)
    x2 = _xattn(x1, x1b, kvm, p["xa_wq"], p["xa_wo"], p["ln_xa_g"], p["ln_xa_b"], B, T, 512)

    e, gate, rank, cnt = _router(x2, p["rw_hi"], p["rw_lo"], p["rb"], 512)
    counts = cnt[:, 0].astype(I32)
    padded = (counts + MOE_BLOCK - 1) // MOE_BLOCK * MOE_BLOCK
    pad_end = jnp.cumsum(padded)
    pad_start = (pad_end - padded).astype(I32)
    nb = (n * TOP_K) // MOE_BLOCK + N_EXPERTS
    n_used = (pad_end[-1] // MOE_BLOCK).astype(I32).reshape(1)
    blk_start = jnp.arange(nb, dtype=I32) * MOE_BLOCK
    blk_expert = jnp.minimum(jnp.sum(blk_start[:, None] >= pad_end[None, :], axis=1), N_EXPERTS - 1).astype(I32)
    blk_expert = jnp.where(jnp.arange(nb) < n_used[0], blk_expert, blk_expert[jnp.maximum(n_used[0] - 1, 0)])
    slot = _slots(pad_start, e, rank, 2048)
    buf = _dispatch(pad_end.astype(I32), slot[0], slot[1], x2, nb * MOE_BLOCK, 512)
    y = _experts(blk_expert, n_used, buf, moe_w[0], moe_w[1], layer)
    x3, x3b = _combine(slot[0], slot[1], y, x2, gate.T, p["ln_ffn_g"], p["ln_ffn_b"], 256)
    return x3, x3b


def _prep_layer(l, w_in, gla_w_a2, gla_b_a, gla_norm_g, nsa_cmp_pe, nsa_cmp_w1, nsa_cmp_w2, w_branch_gla,
                w_branch_nsa, w_out, ln_mix_g, ln_mix_b, xa_wq, xa_wkv, xa_wo, ln_xa_g, ln_xa_b, router_w,
                router_b, moe_w_in, moe_w_down, ln_ffn_g, ln_ffn_b):
    d = w_in.shape[1]
    w = w_in[l]
    o_gq, o_gk, o_gv, o_gr = 0, GLA_QK, 2 * GLA_QK, 2 * GLA_QK + GLA_V
    o_ga = o_gr + GLA_V
    o_nq = o_ga + GLA_GATE_RANK
    o_nkv = o_nq + NSA_Q
    o_ng = o_nkv + 6 * NSA_KV
    o_mg = o_ng + 3 * NSA_HEADS
    G, DH = NSA_GROUPS, NSA_DH

    def kv_cols(kind):
        return w[:, o_nkv + kind * NSA_KV:o_nkv + (kind + 1) * NSA_KV]

    def slabs(wk):
        return jnp.pad(wk.reshape(d, G, DH), ((0, 0), (0, 0), (0, LANES - DH))).reshape(d, G * LANES)

    w_big = jnp.concatenate([w[:, o_mg:o_mg + 2 * d], w[:, o_gq:o_ga], slabs(kv_cols(2)), slabs(kv_cols(4))],
                            axis=1).astype(BF16)
    w_small = jnp.concatenate([w[:, o_ga:o_nq], w[:, o_ng:o_mg],
                               jnp.zeros((d, LANES - GLA_GATE_RANK - 3 * NSA_HEADS), F32),
                               kv_cols(0), kv_cols(1)], axis=1).astype(BF16)
    w_t = jnp.concatenate([w[:, o_nq:o_nkv], kv_cols(3), kv_cols(5)], axis=1).T.astype(BF16)
    w1 = nsa_cmp_w1[l].reshape(2, CMP_LEN, DH, CMP_HIDDEN)
    z1 = jnp.zeros_like(w1)
    w1bd = jnp.concatenate([jnp.concatenate([w1, z1], axis=3), jnp.concatenate([z1, w1], axis=3)], axis=2)
    w2 = nsa_cmp_w2[l]
    z2 = jnp.zeros_like(w2)
    w2bd = jnp.concatenate([jnp.concatenate([w2, z2], axis=2), jnp.concatenate([z2, w2], axis=2)], axis=1)
    pe = nsa_cmp_pe[l]
    pe_pair = jnp.broadcast_to(jnp.concatenate([pe, pe], axis=-1)[:, :, None, :], (2, CMP_LEN, 16, 2 * DH))
    wa_pad = jnp.concatenate([gla_w_a2[l], jnp.zeros((LANES - GLA_GATE_RANK, GLA_QK), F32)], axis=0).astype(BF16)
    rw_t = router_w.T
    rw_hi = rw_t.astype(BF16)
    rw_lo = (rw_t - rw_hi.astype(F32)).astype(BF16)
    return dict(
        w_big=w_big, w_small=w_small, w_t=w_t, wa_pad=wa_pad,
        b_a=gla_b_a[l].reshape(1, -1), norm_g=gla_norm_g[l].reshape(1, -1),
        cmp_w1bd=w1bd.astype(BF16), cmp_w2bd=w2bd.astype(BF16), cmp_w2bdt=w2bd.transpose(0, 2, 1).astype(BF16),
        cmp_pe_pair=pe_pair.astype(BF16),
        w_bg=w_branch_gla[l].astype(BF16), w_bn=w_branch_nsa[l].astype(BF16), w_out=w_out[l].astype(BF16),
        ln_mix_g=ln_mix_g[l].reshape(1, -1), ln_mix_b=ln_mix_b[l].reshape(1, -1),
        xa_wq=xa_wq[l].astype(BF16), xa_wkv=xa_wkv[l].astype(BF16), xa_wo=xa_wo[l].astype(BF16),
        ln_xa_g=ln_xa_g[l].reshape(1, -1), ln_xa_b=ln_xa_b[l].reshape(1, -1),
        rw_hi=rw_hi, rw_lo=rw_lo, rb=router_b.reshape(-1, 1),
        ln_ffn_g=ln_ffn_g[l].reshape(1, -1), ln_ffn_b=ln_ffn_b[l].reshape(1, -1),
    )


def kernel(x, mem, w_in, gla_w_a2, gla_b_a, gla_norm_g, nsa_cmp_pe, nsa_cmp_w1, nsa_cmp_w2, w_branch_gla, w_branch_nsa, w_out, ln_mix_g, ln_mix_b, xa_wq, xa_wkv, xa_wo, ln_xa_g, ln_xa_b, router_w, router_b, moe_w_in, moe_w_down, ln_ffn_g, ln_ffn_b):
    B, T, d = x.shape
    assert T % 512 == 0 and d == 2048 and mem.shape[1] == MEM_LEN
    n = B * T
    params = (w_in, gla_w_a2, gla_b_a, gla_norm_g, nsa_cmp_pe, nsa_cmp_w1, nsa_cmp_w2, w_branch_gla, w_branch_nsa,
              w_out, ln_mix_g, ln_mix_b, xa_wq, xa_wkv, xa_wo, ln_xa_g, ln_xa_b, router_w, router_b, moe_w_in,
              moe_w_down, ln_ffn_g, ln_ffn_b)
    slopes = (2.0 ** (-8.0 * jnp.arange(1, NSA_HEADS + 1, dtype=F32) / NSA_HEADS)).astype(F32)
    nc, ns = T // CMP_STRIDE, T // SEL_LEN
    cs = np.arange(nc) * CMP_STRIDE
    ss = np.arange(ns) * SEL_LEN
    ovt = ((cs[None, :] < ss[:, None] + SEL_LEN) & (cs[None, :] + CMP_LEN > ss[:, None])
           & (cs[None, :] + CMP_LEN <= T)).astype(np.float32)
    assert NSA_DH + ns <= LANES
    epad = np.zeros((T, LANES), np.float32)
    epad[np.arange(T), NSA_DH + np.arange(T) // SEL_LEN] = 1.0
    tile_ind = (np.arange(ns)[None, :] // (256 // SEL_LEN) == np.arange(T // 256)[:, None]).astype(np.float32)
    consts = (slopes, jnp.asarray(ovt, BF16), jnp.asarray(epad, BF16), jnp.asarray(tile_ind, BF16))

    xf = x.reshape(n, d)
    xb = xf.astype(BF16)
    mem_b = mem.reshape(B * MEM_LEN, d).astype(BF16)
    moe_w = (moe_w_in, moe_w_down)
    for l in range(DEPTH):
        p = _prep_layer(l, *params)
        xf, xb = _layer(xf, xb, mem_b, p, moe_w, l, consts, B, T)
    return xf.reshape(B, T, d)
```

```python
import functools

import jax
import jax.numpy as jnp
import numpy as np
from jax import lax
from jax.experimental import pallas as pl
from jax.experimental.pallas import tpu as pltpu

F32 = jnp.float32
BF16 = jnp.bfloat16
I32 = jnp.int32

DEPTH = 2
MEM_LEN = 256
GLA_HEADS = 4
GLA_DK = 128
GLA_DV = 256
GLA_GATE_RANK = 16
GLA_TAU = 16.0
GLA_CHUNK = 64
NSA_HEADS = 16
NSA_GROUPS = 4
NSA_HPG = NSA_HEADS // NSA_GROUPS
NSA_DH = 64
CMP_LEN = 32
CMP_STRIDE = 16
CMP_HIDDEN = 256
SEL_LEN = 64
SEL_TOPN = 8
WINDOW = 512
XA_HEADS = 4
XA_DH = 128
N_EXPERTS = 16
N_GROUPS = 4
EXPERTS_PER_GROUP = N_EXPERTS // N_GROUPS
TOP_K = 2
D_FF = 1536
DN_ALPHA = float((2 * DEPTH) ** 0.25)
LN_EPS = 1e-5
NEG = -1e30
LOG2E = 1.4426950408889634
FORCE_BONUS = 1e6

GLA_QK = GLA_HEADS * GLA_DK
GLA_V = GLA_HEADS * GLA_DV
NSA_Q = NSA_HEADS * NSA_DH
NSA_KV = NSA_GROUPS * NSA_DH

LANES = 128
VMEM_LIMIT = 56 * 1024 * 1024

COL_MG = 0
COL_GQ = 2 * 2048
COL_GK = COL_GQ + GLA_QK
COL_GV = COL_GK + GLA_QK
COL_GR = COL_GV + GLA_V
COL_KS = COL_GR + GLA_V
COL_KW = COL_KS + NSA_GROUPS * LANES
COL_END = COL_KW + NSA_GROUPS * LANES
SCOL_CK = LANES
SCOL_CV = SCOL_CK + NSA_KV
SCOL_END = SCOL_CV + NSA_KV
TROW_Q = 0
TROW_VS = NSA_Q
TROW_VW = TROW_VS + NSA_KV
TROW_END = TROW_VW + NSA_KV

MOE_BLOCK = 512
FF_TILE = 512


def _cp(sem):
    return pltpu.CompilerParams(dimension_semantics=sem, vmem_limit_bytes=VMEM_LIMIT)


def _dot(a, b):
    return jnp.dot(a, b, preferred_element_type=F32)


def _dot_nt(a, b):
    return lax.dot_general(a, b, (((1,), (1,)), ((), ())), preferred_element_type=F32)


def _dot_tn(a, b):
    return lax.dot_general(a, b, (((0,), (0,)), ((), ())), preferred_element_type=F32)


def _layer_norm(z, g, b):
    mu = jnp.mean(z, axis=-1, keepdims=True)
    zc = z - mu
    var = jnp.mean(zc * zc, axis=-1, keepdims=True)
    return zc * lax.rsqrt(var + LN_EPS) * g + b


def _mm_kernel(a_ref, b_ref, o_ref):
    o_ref[...] = _dot(a_ref[...], b_ref[...]).astype(o_ref.dtype)


def _matmul(a, b, out_dtype, tm, tn):
    m, k = a.shape
    n = b.shape[1]
    return pl.pallas_call(
        _mm_kernel,
        grid=(m // tm, n // tn),
        in_specs=[pl.BlockSpec((tm, k), lambda i, j: (i, 0)),
                  pl.BlockSpec((k, tn), lambda i, j: (0, j))],
        out_specs=pl.BlockSpec((tm, tn), lambda i, j: (i, j)),
        out_shape=jax.ShapeDtypeStruct((m, n), out_dtype),
        compiler_params=_cp(("parallel", "parallel")),
        name="matmul",
    )(a, b)


def _mm_nt_kernel(wt_ref, x_ref, o_ref):
    o_ref[0] = _dot_nt(wt_ref[...], x_ref[...]).astype(o_ref.dtype)


def _matmul_t(x, wt, B, T, tm, tr):
    n, k = x.shape
    r = wt.shape[0]
    nt = T // tm
    return pl.pallas_call(
        _mm_nt_kernel,
        grid=(n // tm, r // tr),
        in_specs=[pl.BlockSpec((tr, k), lambda i, j: (j, 0)),
                  pl.BlockSpec((tm, k), lambda i, j: (i, 0))],
        out_specs=pl.BlockSpec((1, tr, tm), lambda i, j: (i // nt, j, i % nt)),
        out_shape=jax.ShapeDtypeStruct((B, r, T), BF16),
        compiler_params=_cp(("parallel", "parallel")),
        name="matmul_t",
    )(wt, x)


def _gla_kernel(q_ref, k_ref, v_ref, r_ref, sm_ref, wa_ref, ba_ref, ng_ref, o_ref, st_ref):
    C = GLA_CHUNK
    n_chunks = q_ref.shape[0] // C
    st_ref[...] = jnp.zeros_like(st_ref)
    rowi = lax.broadcasted_iota(I32, (C, GLA_DK), 0)
    tt = lax.broadcasted_iota(I32, (C, C), 0)
    ss = lax.broadcasted_iota(I32, (C, C), 1)
    levels = (1, 2, 4, 8, 16, 32)
    pair_masks = [((tt // (2 * L)) == (ss // (2 * L))) & ((tt & L) != 0) & ((ss & L) == 0) for L in levels]
    diag_mask = tt == ss
    scale = GLA_DK ** -0.5

    def head_chunk(rows, h, z):
        qk_cols = slice(h * GLA_DK, (h + 1) * GLA_DK)
        v_cols = slice(h * GLA_DV, (h + 1) * GLA_DV)
        q = q_ref[rows, qk_cols].astype(F32) * scale
        k = k_ref[rows, qk_cols].astype(F32)
        v = v_ref[rows, v_cols]
        g = (jnp.minimum(z, 0.0) - jnp.log1p(jnp.exp(-jnp.abs(z)))) * (1.0 / GLA_TAU)
        incl = g
        tot = g
        att = jnp.where(diag_mask, _dot_nt(q.astype(BF16), k.astype(BF16)), 0.0)
        for L, pm in zip(levels, pair_masks):
            ql = (q * jnp.exp(incl)).astype(BF16)
            kl = (k * jnp.exp(tot - incl)).astype(BF16)
            att = jnp.where(pm, _dot_nt(ql, kl), att)
            upper = (rowi & L) != 0
            from_lower = pltpu.roll(tot, L, 0)
            from_upper = pltpu.roll(tot, C - L, 0)
            incl = incl + jnp.where(upper, from_lower, 0.0)
            tot = tot + jnp.where(upper, from_lower, from_upper)
        qd = (q * jnp.exp(incl)).astype(BF16)
        kd = (k * jnp.exp(tot - incl)).astype(BF16)
        st = st_ref[h]
        o = _dot_nt(qd, st.astype(BF16)) + _dot(att.astype(BF16), v)
        st_ref[h] = st * jnp.exp(tot[0:1, :]) + _dot_tn(v, kd)
        mu = jnp.mean(o, axis=-1, keepdims=True)
        oc = o - mu
        var = jnp.mean(oc * oc, axis=-1, keepdims=True)
        on = oc * lax.rsqrt(var + LN_EPS) * ng_ref[:, v_cols]
        r = r_ref[rows, v_cols].astype(F32)
        o_ref[rows, v_cols] = (on * (r * jax.nn.sigmoid(r))).astype(o_ref.dtype)

    def chunk(c, carry):
        rows = pl.ds(pl.multiple_of(c * C, C), C)
        z = _dot(sm_ref[rows, :].astype(BF16), wa_ref[...]) + ba_ref[...]
        for h in range(GLA_HEADS):
            head_chunk(rows, h, z[:, h * GLA_DK:(h + 1) * GLA_DK])
        return carry

    lax.fori_loop(0, n_chunks, chunk, 0)


def _gla(h_big, h_small, wa_pad, b_a, norm_g, B, T):
    n = B * T
    return pl.pallas_call(
        _gla_kernel,
        grid=(B,),
        in_specs=[
            pl.BlockSpec((T, GLA_QK), lambda b: (b, COL_GQ // GLA_QK)),
            pl.BlockSpec((T, GLA_QK), lambda b: (b, COL_GK // GLA_QK)),
            pl.BlockSpec((T, GLA_V), lambda b: (b, COL_GV // GLA_V)),
            pl.BlockSpec((T, GLA_V), lambda b: (b, COL_GR // GLA_V)),
            pl.BlockSpec((T, LANES), lambda b: (b, 0)),
            pl.BlockSpec((LANES, GLA_QK), lambda b: (0, 0)),
            pl.BlockSpec((1, GLA_QK), lambda b: (0, 0)),
            pl.BlockSpec((1, GLA_V), lambda b: (0, 0)),
        ],
        out_specs=pl.BlockSpec((T, GLA_V), lambda b: (b, 0)),
        out_shape=jax.ShapeDtypeStruct((n, GLA_V), BF16),
        scratch_shapes=[pltpu.VMEM((GLA_HEADS, GLA_DV, GLA_DK), F32)],
        compiler_params=_cp(("parallel",)),
        name="gla",
    )(h_big, h_big, h_big, h_big, h_small, wa_pad, b_a, norm_g)


def _compress_kernel(x_ref, w1_ref, w2_ref, w2t_ref, pe_ref, o_ref, ot_ref):
    nc = x_ref.shape[0] // CMP_STRIDE
    hid_w = w1_ref.shape[3]
    a = jnp.zeros((nc, hid_w), F32)
    bm = jnp.zeros((nc, hid_w), F32)
    c = jnp.zeros((pe_ref.shape[2], hid_w), F32)
    for l in range(CMP_STRIDE):
        xl = x_ref[pl.ds(l, nc, stride=CMP_STRIDE), :].astype(BF16)
        a = a + _dot(xl, w1_ref[0, l])
        bm = bm + _dot(xl, w1_ref[0, CMP_STRIDE + l])
    for l in range(CMP_LEN):
        c = c + _dot(pe_ref[0, l], w1_ref[0, l])
    hid = a + pltpu.roll(bm, nc - 1, 0) + c[0:1, :]
    act = jax.nn.gelu(hid).astype(BF16)
    o_ref[0, 0, 0] = _dot(act, w2_ref[0]).astype(o_ref.dtype)
    ot_ref[0, 0, 0] = _dot_nt(w2t_ref[0], act).astype(ot_ref.dtype)


def _compress(h_small, w1bd, w2bd, w2bdt, pe_pair, B, T):
    nc = T // CMP_STRIDE
    pairs = NSA_GROUPS // 2
    return pl.pallas_call(
        _compress_kernel,
        grid=(B, 2, pairs),
        in_specs=[
            pl.BlockSpec((T, LANES), lambda b, s, j: (b, SCOL_CK // LANES + s * pairs + j)),
            pl.BlockSpec((1,) + w1bd.shape[1:], lambda b, s, j: (s, 0, 0, 0)),
            pl.BlockSpec((1,) + w2bd.shape[1:], lambda b, s, j: (s, 0, 0)),
            pl.BlockSpec((1,) + w2bdt.shape[1:], lambda b, s, j: (s, 0, 0)),
            pl.BlockSpec((1,) + pe_pair.shape[1:], lambda b, s, j: (s, 0, 0, 0)),
        ],
        out_specs=[
            pl.BlockSpec((1, 1, 1, nc, LANES), lambda b, s, j: (b, s, j, 0, 0)),
            pl.BlockSpec((1, 1, 1, LANES, nc), lambda b, s, j: (b, s, j, 0, 0)),
        ],
        out_shape=[jax.ShapeDtypeStruct((B, 2, pairs, nc, LANES), BF16),
                   jax.ShapeDtypeStruct((B, 2, pairs, LANES, nc), BF16)],
        compiler_params=_cp(("parallel", "parallel", "parallel")),
        name="nsa_compress",
    )(h_small, w1bd, w2bd, w2bdt, pe_pair)


def _cmp_select_kernel(slopes_ref, qt_ref, kc_ref, vct_ref, ovt_ref, ind_ref, ocmp_ref, mb_ref, kt_ref, qpad_ref):
    g = pl.program_id(1)
    i = pl.program_id(2)
    tq = qt_ref.shape[2]
    nc = kc_ref.shape[3]
    ns = mb_ref.shape[2]
    dh = NSA_DH
    t0 = i * tq
    wide = NSA_HPG * tq
    tpos = t0 + (lax.broadcasted_iota(I32, (nc, wide), 1) & (tq - 1))
    nidx = lax.broadcasted_iota(I32, (nc, wide), 0)
    mask_c = (nidx * CMP_STRIDE + (CMP_LEN - 1)) <= tpos
    absd = jnp.abs(tpos.astype(F32) - (nidx.astype(F32) * CMP_STRIDE + 0.5 * (CMP_LEN - 1)))
    srow = jnp.concatenate([jnp.full((1, tq), slopes_ref[g * NSA_HPG + hh], F32) for hh in range(NSA_HPG)], axis=1)
    lower = g % 2 == 0
    kc = kc_ref[0, 0, 0]
    vct = jnp.where(lower, vct_ref[0, 0, 0, 0:dh, :], vct_ref[0, 0, 0, dh:2 * dh, :])
    for hh in range(NSA_HPG):
        q = qt_ref[0, hh * dh:(hh + 1) * dh, :] * jnp.asarray(dh ** -0.5, BF16)
        zero = jnp.zeros_like(q)
        qpad_ref[0:dh, hh * tq:(hh + 1) * tq] = jnp.where(lower, q, zero)
        qpad_ref[dh:2 * dh, hh * tq:(hh + 1) * tq] = jnp.where(lower, zero, q)
    s = _dot(kc, qpad_ref[...]) - srow * absd
    s = jnp.where(mask_c, s, NEG)
    e = jnp.exp(s - jnp.max(s, axis=0, keepdims=True))
    p = jnp.where(mask_c, e * (1.0 / jnp.sum(e, axis=0, keepdims=True)), 0.0)
    o = _dot(vct, p.astype(BF16))
    psum = jnp.zeros((nc, tq), F32)
    for hh in range(NSA_HPG):
        ocmp_ref[0, hh * dh:(hh + 1) * dh, :] = o[:, hh * tq:(hh + 1) * tq].astype(ocmp_ref.dtype)
        psum = psum + p[:, hh * tq:(hh + 1) * tq]
    p_hi = psum.astype(BF16)
    p_lo = (psum - p_hi.astype(F32)).astype(BF16)
    imp = _dot(ovt_ref[...], p_hi) + _dot(ovt_ref[...], p_lo)
    j = lax.broadcasted_iota(I32, (ns, tq), 0)
    tp = t0 + lax.broadcasted_iota(I32, (ns, tq), 1)
    cur = tp // SEL_LEN
    forced = (j == 0) | (j == cur) | (j == cur - 1)
    valid = j * SEL_LEN <= tp
    score = jnp.where(valid, imp + jnp.where(forced, FORCE_BONUS, 0.0), NEG)
    rank = jnp.zeros((ns, tq), F32)
    for jp in range(ns):
        row = score[jp:jp + 1, :]
        beats = (row > score) | ((row == score) & (j > jp))
        rank = rank + jnp.where(beats, 1.0, 0.0)
    keep = valid & (rank < float(min(SEL_TOPN, ns)))
    mb_ref[0, 0] = jnp.where(keep, 0.0, NEG).astype(mb_ref.dtype)
    kt_ref[0, 0] = _dot(ind_ref[...], jnp.where(keep, 1.0, 0.0).astype(BF16))


def _cmp_select(slopes, h_t, kcmp, kcmp_t, ovt, tile_ind, B, T, tq):
    nc = T // CMP_STRIDE
    ns = T // SEL_LEN
    nkt = tile_ind.shape[0]
    grp_rows = NSA_HPG * NSA_DH
    grid_spec = pltpu.PrefetchScalarGridSpec(
        num_scalar_prefetch=1,
        grid=(B, NSA_GROUPS, T // tq),
        in_specs=[
            pl.BlockSpec((1, grp_rows, tq), lambda b, g, i, s: (b, TROW_Q // grp_rows + g, i)),
            pl.BlockSpec((1, 1, 1, nc, LANES), lambda b, g, i, s: (b, 0, g // 2, 0, 0)),
            pl.BlockSpec((1, 1, 1, LANES, nc), lambda b, g, i, s: (b, 1, g // 2, 0, 0)),
            pl.BlockSpec((ns, nc), lambda b, g, i, s: (0, 0)),
            pl.BlockSpec((nkt, ns), lambda b, g, i, s: (0, 0)),
        ],
        out_specs=[
            pl.BlockSpec((1, grp_rows, tq), lambda b, g, i, s: (b, g, i)),
            pl.BlockSpec((1, 1, ns, tq), lambda b, g, i, s: (b, g, 0, i)),
            pl.BlockSpec((1, 1, nkt, tq), lambda b, g, i, s: (b, g, 0, i)),
        ],
        scratch_shapes=[pltpu.VMEM((LANES, NSA_HPG * tq), BF16)],
    )
    return pl.pallas_call(
        _cmp_select_kernel,
        grid_spec=grid_spec,
        out_shape=[jax.ShapeDtypeStruct((B, NSA_Q, T), BF16),
                   jax.ShapeDtypeStruct((B, NSA_GROUPS, ns, T), BF16),
                   jax.ShapeDtypeStruct((B, NSA_GROUPS, nkt, T), F32)],
        compiler_params=_cp(("parallel", "parallel", "parallel")),
        name="nsa_cmp_select",
    )(slopes, h_t, kcmp, kcmp_t, ovt, tile_ind)


def _sel_win_kernel(slopes_ref, flags_ref, qt_ref, ks_ref, kw_ref, vs_ref, vw_ref, epad_ref, mb_ref, ocmp_ref, gt_ref,
                    o_ref, qaug_ref, m_ref, acc_ref, srow_ref, bias_ref, s_ref, p_ref, alpha_ref,
                    kall_ref, vall_ref, tiles_ref, *, n_tiles):
    g = pl.program_id(1)
    i = pl.program_id(2)
    tq = qt_ref.shape[2]
    tk = tq
    ns = mb_ref.shape[2]
    dh = NSA_DH
    wide = NSA_HPG * tq
    t0 = i * tq
    BIG = -NEG

    @pl.when(i == 0)
    def _():
        srow = jnp.concatenate([jnp.full((1, tq), slopes_ref[g * NSA_HPG + hh] * LOG2E, F32)
                                for hh in range(NSA_HPG)], axis=1)
        srow_ref[...] = srow
        lane = lax.broadcasted_iota(I32, (tk, wide), 1) & (tq - 1)
        dist0 = (lane - lax.broadcasted_iota(I32, (tk, wide), 0)).astype(F32)
        sd0 = srow * dist0
        bias_ref[0] = sd0
        bias_ref[1] = sd0 + jnp.where(dist0 >= 0.0, 0.0, BIG)
        bias_ref[2] = sd0 + jnp.where(dist0 < 0.0, 0.0, BIG)
        bias_ref[3] = jnp.full((tk, wide), BIG, F32)
        kall_ref[0] = ks_ref[...] + epad_ref[...]
        kall_ref[1] = kw_ref[...]
        extra = jnp.where(lax.broadcasted_iota(I32, (vall_ref.shape[1] - dh, vall_ref.shape[2]), 0) == 0, 1.0, 0.0)
        vall_ref[0, 0:dh, :] = vs_ref[0]
        vall_ref[1, 0:dh, :] = vw_ref[0]
        vall_ref[0, dh:, :] = extra.astype(BF16)
        vall_ref[1, dh:, :] = extra.astype(BF16)

    for hh in range(NSA_HPG):
        cols = slice(hh * tq, (hh + 1) * tq)
        q = qt_ref[0, hh * dh:(hh + 1) * dh, :].astype(F32) * (dh ** -0.5 * LOG2E)
        qaug_ref[0:dh, cols] = q.astype(BF16)
        qaug_ref[dh:dh + ns, cols] = mb_ref[0, 0]
        qaug_ref[dh + ns:, cols] = jnp.zeros((qaug_ref.shape[0] - dh - ns, tq), BF16)

    m_ref[...] = jnp.full(m_ref.shape, NEG, F32)
    acc_ref[...] = jnp.zeros(acc_ref.shape, F32)

    n_back = WINDOW // tk
    flag_base = ((pl.program_id(0) * NSA_GROUPS + g) * n_tiles + i) * n_tiles
    n_sel = jnp.int32(0)
    for kb_static in range(n_tiles - 1):
        active = (kb_static < i) & (flags_ref[flag_base + kb_static] != 0)
        tiles_ref[n_sel] = kb_static
        n_sel = n_sel + active.astype(I32)
    tiles_ref[n_sel] = i
    n_sel = n_sel + 1
    n_win = jnp.minimum(i, n_back) + 1
    n_steps = n_sel + n_win

    def describe(n):
        n = jnp.maximum(n, 0)
        is_win = n >= n_sel
        kb_sel = tiles_ref[jnp.minimum(n, n_sel - 1)]
        kb = jnp.clip(jnp.where(is_win, i - n_win + 1 + (n - n_sel), kb_sel), 0, i)
        mode = jnp.where(kb == i, 1, jnp.where(is_win & (kb == i - n_back), 2, 0))
        mode = jnp.where(n >= n_steps, 3, mode)
        return is_win.astype(I32), kb, mode

    def scores(n, slot):
        br, kb, _ = describe(n)
        s0 = pl.multiple_of(kb * tk, tk)
        s_ref[slot] = _dot(kall_ref[br, pl.ds(s0, tk), :], qaug_ref[...])

    def softmax(n, slot):
        br, kb, mode = describe(n)
        crow = srow_ref[...] * ((i - kb) * tk).astype(F32)
        s = s_ref[slot] - bias_ref[mode]
        m_old = m_ref[br]
        m_new = jnp.maximum(m_old, jnp.max(s, axis=0, keepdims=True) - crow)
        alpha = jnp.exp2(m_old - m_new)
        p = jnp.exp2(s - (m_new + crow))
        m_ref[br] = m_new
        alpha_ref[slot] = alpha
        p_ref[slot] = p.astype(BF16)

    def weighted_values(n, slot):
        br, kb, _ = describe(n)
        s0 = pl.multiple_of(kb * tk, tk)
        acc_ref[br] = alpha_ref[slot] * acc_ref[br] + _dot(vall_ref[br, :, pl.ds(s0, tk)], p_ref[slot])

    p_ref[1] = jnp.zeros(p_ref.shape[1:], BF16)
    alpha_ref[1] = jnp.ones(alpha_ref.shape[1:], F32)
    scores(0, 0)

    def pair(j, carry):
        n = 2 * j
        scores(n + 1, 1)
        softmax(n, 0)
        weighted_values(n - 1, 1)
        scores(n + 2, 0)
        softmax(n + 1, 1)
        weighted_values(n, 0)
        return carry

    n_pairs = n_steps // 2
    lax.fori_loop(0, n_pairs, pair, 0)
    weighted_values(2 * n_pairs - 1, 1)

    @pl.when(n_steps % 2 == 1)
    def _():
        softmax(n_steps - 1, 0)
        weighted_values(n_steps - 1, 0)

    def gate_row(branch):
        rows = [gt_ref[0, 0, 3 * hh + branch:3 * hh + branch + 1, :] for hh in range(NSA_HPG)]
        return jax.nn.sigmoid(jnp.concatenate(rows, axis=1))

    o = (acc_ref[0, 0:dh, :] * (gate_row(1) / acc_ref[0, dh:dh + 1, :])
         + acc_ref[1, 0:dh, :] * (gate_row(2) / acc_ref[1, dh:dh + 1, :]))
    ocmp = jnp.concatenate([ocmp_ref[0, hh * dh:(hh + 1) * dh, :] for hh in range(NSA_HPG)], axis=1).astype(F32)
    o = o + gate_row(0) * ocmp
    o_heads = jnp.concatenate([o[:, hh * tq:(hh + 1) * tq] for hh in range(NSA_HPG)], axis=0)
    o_ref[...] = o_heads.T.astype(o_ref.dtype)


def _sel_win(slopes, tile_flags, h_t, h_big, epad, mb, ocmp_t, gates_t, B, T, tq):
    ns = T // SEL_LEN
    kaug = LANES
    G = NSA_GROUPS
    grp_rows = NSA_HPG * NSA_DH
    nq = T // tq
    grid_spec = pltpu.PrefetchScalarGridSpec(
        num_scalar_prefetch=2,
        grid=(B, G, nq),
        in_specs=[
            pl.BlockSpec((1, grp_rows, tq), lambda b, g, i, s, f: (b, TROW_Q // grp_rows + g, i)),
            pl.BlockSpec((T, LANES), lambda b, g, i, s, f: (b, COL_KS // LANES + g)),
            pl.BlockSpec((T, LANES), lambda b, g, i, s, f: (b, COL_KW // LANES + g)),
            pl.BlockSpec((1, NSA_DH, T), lambda b, g, i, s, f: (b, TROW_VS // NSA_DH + g, 0)),
            pl.BlockSpec((1, NSA_DH, T), lambda b, g, i, s, f: (b, TROW_VW // NSA_DH + g, 0)),
            pl.BlockSpec((T, LANES), lambda b, g, i, s, f: (0, 0)),
            pl.BlockSpec((1, 1, ns, tq), lambda b, g, i, s, f: (b, g, 0, i)),
            pl.BlockSpec((1, grp_rows, tq), lambda b, g, i, s, f: (b, g, i)),
            pl.BlockSpec((1, 1, 16, tq), lambda b, g, i, s, f: (b, g, 0, i)),
        ],
        out_specs=pl.BlockSpec((tq, grp_rows), lambda b, g, i, s, f: (b * nq + i, g)),
        scratch_shapes=[
            pltpu.VMEM((kaug, NSA_HPG * tq), BF16),
            pltpu.VMEM((2, 1, NSA_HPG * tq), F32),
            pltpu.VMEM((2, NSA_DH + 16, NSA_HPG * tq), F32),
            pltpu.VMEM((1, NSA_HPG * tq), F32),
            pltpu.VMEM((4, tq, NSA_HPG * tq), F32),
            pltpu.VMEM((2, tq, NSA_HPG * tq), F32),
            pltpu.VMEM((2, tq, NSA_HPG * tq), BF16),
            pltpu.VMEM((2, 1, NSA_HPG * tq), F32),
            pltpu.VMEM((2, T, kaug), BF16),
            pltpu.VMEM((2, NSA_DH + 16, T), BF16),
            pltpu.SMEM((nq,), I32),
        ],
    )
    return pl.pallas_call(
        functools.partial(_sel_win_kernel, n_tiles=nq),
        grid_spec=grid_spec,
        out_shape=jax.ShapeDtypeStruct((B * T, NSA_Q), BF16),
        compiler_params=_cp(("parallel", "parallel", "arbitrary")),
        name="nsa_sel_win",
    )(slopes, tile_flags, h_t, h_big, h_big, h_t, h_t, epad, mb, ocmp_t, gates_t)


def _mix_kernel(og_ref, on_ref, mg1_ref, mg2_ref, x_ref, wg_ref, wn_ref, wo_ref, lg_ref, lb_ref,
                x1_ref, x1b_ref):
    g1 = _dot(og_ref[...], wg_ref[...])
    g2 = _dot(on_ref[...], wn_ref[...])
    merged = (jax.nn.sigmoid(mg1_ref[...].astype(F32)) * g1
              + jax.nn.sigmoid(mg2_ref[...].astype(F32)) * g2)
    y = _dot(merged.astype(BF16), wo_ref[...])
    x1 = _layer_norm(DN_ALPHA * x_ref[...] + y, lg_ref[...], lb_ref[...])
    x1_ref[...] = x1
    x1b_ref[...] = x1.astype(BF16)


def _const_spec(shape):
    nd = len(shape)
    return pl.BlockSpec(shape, lambda *_: (0,) * nd, pipeline_mode=pl.Buffered(1))


def _mix(o_gla, o_nsa, h_big, x, wg, wn, wo, lg, lb, tm):
    n, d = x.shape
    return pl.pallas_call(
        _mix_kernel,
        grid=(n // tm,),
        in_specs=[
            pl.BlockSpec((tm, GLA_V), lambda i: (i, 0)),
            pl.BlockSpec((tm, NSA_Q), lambda i: (i, 0)),
            pl.BlockSpec((tm, d), lambda i: (i, 0)),
            pl.BlockSpec((tm, d), lambda i: (i, 1)),
            pl.BlockSpec((tm, d), lambda i: (i, 0)),
            _const_spec(wg.shape), _const_spec(wn.shape), _const_spec(wo.shape),
            _const_spec(lg.shape), _const_spec(lb.shape),
        ],
        out_specs=[pl.BlockSpec((tm, d), lambda i: (i, 0)), pl.BlockSpec((tm, d), lambda i: (i, 0))],
        out_shape=[jax.ShapeDtypeStruct((n, d), F32), jax.ShapeDtypeStruct((n, d), BF16)],
        compiler_params=_cp(("parallel",)),
        name="mix_ln",
    )(o_gla, o_nsa, h_big, h_big, x, wg, wn, wo, lg, lb)


def _xattn_kernel(x_ref, xb_ref, kv_ref, wq_ref, wo_ref, lg_ref, lb_ref, x2_ref):
    hd = XA_HEADS * XA_DH
    q = (_dot(xb_ref[...], wq_ref[...]) * (XA_DH ** -0.5)).astype(BF16)
    outs = []
    for h in range(XA_HEADS):
        kh = kv_ref[0, :, h * XA_DH:(h + 1) * XA_DH]
        vh = kv_ref[0, :, hd + h * XA_DH:hd + (h + 1) * XA_DH]
        s = _dot_nt(q[:, h * XA_DH:(h + 1) * XA_DH], kh)
        e = jnp.exp(s - jnp.max(s, axis=-1, keepdims=True))
        p = e / jnp.sum(e, axis=-1, keepdims=True)
        outs.append(_dot(p.astype(BF16), vh).astype(BF16))
    o = jnp.concatenate(outs, axis=-1)
    y = _dot(o, wo_ref[...])
    x2_ref[...] = _layer_norm(DN_ALPHA * x_ref[...] + y, lg_ref[...], lb_ref[...])


def _xattn(x1, x1b, kv, wq, wo, lg, lb, B, T, tm):
    n, d = x1.shape
    nt = T // tm
    return pl.pallas_call(
        _xattn_kernel,
        grid=(B, nt),
        in_specs=[
            pl.BlockSpec((tm, d), lambda b, i: (b * nt + i, 0)),
            pl.BlockSpec((tm, d), lambda b, i: (b * nt + i, 0)),
            pl.BlockSpec((1,) + kv.shape[1:], lambda b, i: (b, 0, 0)),
            _const_spec(wq.shape), _const_spec(wo.shape), _const_spec(lg.shape), _const_spec(lb.shape),
        ],
        out_specs=pl.BlockSpec((tm, d), lambda b, i: (b * nt + i, 0)),
        out_shape=jax.ShapeDtypeStruct((n, d), F32),
        compiler_params=_cp(("parallel", "parallel")),
        name="xattn_ln",
    )(x1, x1b, kv, wq, wo, lg, lb)


def _router_kernel(x_ref, wh_ref, wl_ref, rb_ref, e_ref, gate_ref, rank_ref, cnt_ref, carry_ref):
    i = pl.program_id(0)
    tr = x_ref.shape[0]
    E = N_EXPERTS

    @pl.when(i == 0)
    def _():
        carry_ref[...] = jnp.zeros_like(carry_ref)

    x = x_ref[...]
    x_hi = x.astype(BF16)
    x_lo = (x - x_hi.astype(F32)).astype(BF16)
    wh = wh_ref[...]
    logits = _dot_nt(wh, x_hi) + _dot_nt(wh, x_lo) + _dot_nt(wl_ref[...], x_hi)
    biased = logits + rb_ref[...]
    rows = [biased[e:e + 1, :] for e in range(E)]
    raw = [logits[e:e + 1, :] for e in range(E)]
    best_score = None
    best = None
    for gi in range(N_GROUPS):
        v = rows[gi * EXPERTS_PER_GROUP:(gi + 1) * EXPERTS_PER_GROUP]
        sc = None
        for a in range(EXPERTS_PER_GROUP):
            for b in range(a + 1, EXPERTS_PER_GROUP):
                pair = v[a] + v[b]
                sc = pair if sc is None else jnp.maximum(sc, pair)
        if best is None:
            best_score, best = sc, jnp.zeros((1, tr), I32)
        else:
            better = sc > best_score
            best_score = jnp.where(better, sc, best_score)
            best = jnp.where(better, gi, best)

    def pick(vals):
        out = vals[0:EXPERTS_PER_GROUP]
        for gi in range(1, N_GROUPS):
            out = [jnp.where(best == gi, vals[gi * EXPERTS_PER_GROUP + a], out[a]) for a in range(EXPERTS_PER_GROUP)]
        return out

    w = pick(rows)
    lraw = pick(raw)
    i1 = jnp.zeros((1, tr), I32)
    v1 = w[0]
    l1 = lraw[0]
    for a in range(1, EXPERTS_PER_GROUP):
        better = w[a] > v1
        v1 = jnp.where(better, w[a], v1)
        l1 = jnp.where(better, lraw[a], l1)
        i1 = jnp.where(better, a, i1)
    i2 = jnp.full((1, tr), -1, I32)
    v2 = jnp.full((1, tr), -jnp.inf, F32)
    l2 = jnp.zeros((1, tr), F32)
    for a in range(EXPERTS_PER_GROUP):
        better = (i1 != a) & ((w[a] > v2) | (i2 < 0))
        v2 = jnp.where(better, w[a], v2)
        l2 = jnp.where(better, lraw[a], l2)
        i2 = jnp.where(better, a, i2)
    e1 = best * EXPERTS_PER_GROUP + i1
    e2 = best * EXPERTS_PER_GROUP + i2
    mx = jnp.maximum(l1, l2)
    p1 = jnp.exp(l1 - mx)
    p2 = jnp.exp(l2 - mx)
    den = p1 + p2
    e_ref[0:1, :] = e1
    e_ref[1:2, :] = e2
    gate_ref[0:1, :] = p1 / den
    gate_ref[1:2, :] = p2 / den
    eidx = lax.broadcasted_iota(I32, (E, tr), 0)
    is1 = eidx == e1
    is2 = eidx == e2
    member = jnp.where(is1 | is2, 1.0, 0.0)
    uu = lax.broadcasted_iota(I32, (tr, tr), 0)
    tt = lax.broadcasted_iota(I32, (tr, tr), 1)
    tri = jnp.where(uu <= tt, 1.0, 0.0).astype(BF16)
    incl = _dot(member.astype(BF16), tri)
    excl = carry_ref[:, 0:1] + incl - member
    rank_ref[0:1, :] = jnp.sum(jnp.where(is1, excl, 0.0), axis=0, keepdims=True).astype(I32)
    rank_ref[1:2, :] = jnp.sum(jnp.where(is2, excl, 0.0), axis=0, keepdims=True).astype(I32)
    new_carry = carry_ref[...] + jnp.sum(member, axis=1, keepdims=True)
    carry_ref[...] = new_carry
    cnt_ref[...] = new_carry


def _router(x2, rw_hi, rw_lo, rb, tr):
    n, d = x2.shape
    E = N_EXPERTS
    return pl.pallas_call(
        _router_kernel,
        grid=(n // tr,),
        in_specs=[
            pl.BlockSpec((tr, d), lambda i: (i, 0)),
            pl.BlockSpec((E, d), lambda i: (0, 0)),
            pl.BlockSpec((E, d), lambda i: (0, 0)),
            pl.BlockSpec((E, 1), lambda i: (0, 0)),
        ],
        out_specs=[
            pl.BlockSpec((2, tr), lambda i: (0, i)),
            pl.BlockSpec((2, tr), lambda i: (0, i)),
            pl.BlockSpec((2, tr), lambda i: (0, i)),
            pl.BlockSpec((E, LANES), lambda i: (0, 0)),
        ],
        out_shape=[jax.ShapeDtypeStruct((2, n), I32), jax.ShapeDtypeStruct((2, n), F32),
                   jax.ShapeDtypeStruct((2, n), I32), jax.ShapeDtypeStruct((E, LANES), F32)],
        scratch_shapes=[pltpu.VMEM((E, LANES), F32)],
        compiler_params=_cp(("arbitrary",)),
        name="moe_router",
    )(x2, rw_hi, rw_lo, rb)


def _slot_kernel(ps_ref, e_ref, rank_ref, slot_ref):
    e = e_ref[...]
    start = jnp.zeros(e.shape, I32)
    for ex in range(N_EXPERTS):
        start = jnp.where(e == ex, ps_ref[ex], start)
    slot_ref[...] = start + rank_ref[...]


def _slots(pad_start, e, rank, ts):
    n = e.shape[1]
    grid_spec = pltpu.PrefetchScalarGridSpec(
        num_scalar_prefetch=1,
        grid=(n // ts,),
        in_specs=[pl.BlockSpec((TOP_K, ts), lambda i, s: (0, i)), pl.BlockSpec((TOP_K, ts), lambda i, s: (0, i))],
        out_specs=pl.BlockSpec((TOP_K, ts), lambda i, s: (0, i)),
    )
    return pl.pallas_call(
        _slot_kernel,
        grid_spec=grid_spec,
        out_shape=jax.ShapeDtypeStruct((TOP_K, n), I32),
        compiler_params=_cp(("parallel",)),
        name="moe_slots",
    )(pad_start, e, rank)


def _dispatch_kernel(pe_ref, s0_ref, s1_ref, x_ref, buf_hbm, zero_ref, sem):
    td = s0_ref.shape[0]
    slots = (s0_ref, s1_ref)

    @pl.when(pl.program_id(0) == 0)
    def _():
        zero_ref[...] = jnp.zeros_like(zero_ref)

        def zero_copy(ex):
            last = pl.multiple_of(jnp.maximum(pe_ref[ex] - MOE_BLOCK, 0), MOE_BLOCK)
            return pltpu.make_async_copy(zero_ref, buf_hbm.at[pl.ds(last, MOE_BLOCK), :], sem)

        def nonempty(ex):
            return pe_ref[ex] > (pe_ref[ex - 1] if ex > 0 else 0)

        n_blocks = buf_hbm.shape[0] // MOE_BLOCK
        first_unused = pe_ref[N_EXPERTS - 1] // MOE_BLOCK

        def tail_copy(k):
            row = pl.multiple_of((first_unused + k) * MOE_BLOCK, MOE_BLOCK)
            return pltpu.make_async_copy(zero_ref, buf_hbm.at[pl.ds(row, MOE_BLOCK), :], sem)

        for ex in range(N_EXPERTS):
            pl.when(nonempty(ex))(lambda ex=ex: zero_copy(ex).start())
            pl.when(first_unused + ex < n_blocks)(lambda ex=ex: tail_copy(ex).start())
        for ex in range(N_EXPERTS):
            pl.when(nonempty(ex))(lambda ex=ex: zero_copy(ex).wait())
            pl.when(first_unused + ex < n_blocks)(lambda ex=ex: tail_copy(ex).wait())

    def issue(t, carry):
        for kk in range(TOP_K):
            dest = slots[kk][t]
            pltpu.make_async_copy(x_ref.at[pl.ds(t, 1), :], buf_hbm.at[pl.ds(dest, 1), :], sem).start()
        return carry

    lax.fori_loop(0, td, issue, 0, unroll=8)
    for kk in range(TOP_K):
        pltpu.make_async_copy(x_ref, buf_hbm.at[pl.ds(0, td), :], sem).wait()


def _dispatch(pad_end, slot0, slot1, x2, n_rows, td):
    n, d = x2.shape
    grid_spec = pltpu.PrefetchScalarGridSpec(
        num_scalar_prefetch=1,
        grid=(n // td,),
        in_specs=[
            pl.BlockSpec((td,), lambda i, s: (i,), memory_space=pltpu.SMEM),
            pl.BlockSpec((td,), lambda i, s: (i,), memory_space=pltpu.SMEM),
            pl.BlockSpec((td, d), lambda i, s: (i, 0)),
        ],
        out_specs=pl.BlockSpec(memory_space=pl.ANY),
        scratch_shapes=[pltpu.VMEM((MOE_BLOCK, d), F32), pltpu.SemaphoreType.DMA(())],
    )
    return pl.pallas_call(
        _dispatch_kernel,
        grid_spec=grid_spec,
        out_shape=jax.ShapeDtypeStruct((n_rows, d), F32),
        compiler_params=_cp(("arbitrary",)),
        name="moe_dispatch",
    )(pad_end, slot0, slot1, x2)


def _expert_kernel(be_ref, nb_ref, x_ref, win_hbm, wdn_hbm, y_ref, xb_ref, wa_s, wu_s, wd_s, sa, su, sd, sems,
                   *, layer):
    b = pl.program_id(0)
    nf = D_FF // FF_TILE
    n_used = nb_ref[0]
    e = be_ref[b]
    e_prev = be_ref[jnp.maximum(b - 1, 0)]
    e_next = be_ref[jnp.minimum(b + 1, pl.num_programs(0) - 1)]
    active = b < n_used
    is_first = active & ((b == 0) | (e_prev != e))
    feeds_next = active & (b + 1 < n_used) & (e_next != e)

    def tile_copies(ex, f):
        lo = f * FF_TILE
        return (pltpu.make_async_copy(win_hbm.at[layer, ex, :, pl.ds(lo, FF_TILE)], sa, sems.at[0]),
                pltpu.make_async_copy(win_hbm.at[layer, ex, :, pl.ds(D_FF + lo, FF_TILE)], su, sems.at[1]),
                pltpu.make_async_copy(wdn_hbm.at[layer, ex, pl.ds(lo, FF_TILE), :], sd, sems.at[2]))

    def start(ex, f):
        for c in tile_copies(ex, f):
            c.start()

    def finish(ex, f):
        for c in tile_copies(ex, f):
            c.wait()
        wa_s[f] = sa[...].astype(BF16)
        wu_s[f] = su[...].astype(BF16)
        wd_s[f] = sd[...].astype(BF16)

    @pl.when(b == 0)
    def _():
        for f in range(nf - 1):
            start(e, f)
            finish(e, f)
        start(e, nf - 1)

    @pl.when(jnp.logical_not(active))
    def _():
        y_ref[...] = jnp.zeros_like(y_ref)

    @pl.when(active)
    def _():
        xb_ref[...] = x_ref[...].astype(BF16)
        for f in range(nf):
            xb = xb_ref[...]
            a = _dot(xb, wa_s[f])
            u = _dot(xb, wu_s[f])
            act = (a * jax.nn.sigmoid(a) * u).astype(BF16)
            y = _dot(act, wd_s[f])
            if f == 0:
                y_ref[...] = y
                pl.when(is_first)(lambda: finish(e, nf - 1))
            else:
                y_ref[...] += y

            @pl.when(feeds_next)
            def _(f=f):
                if f >= 1:
                    finish(e_next, f - 1)
                start(e_next, f)


def _experts(blk_expert, n_used, buf, w_in, w_down, layer):
    p, d = buf.shape
    nb = p // MOE_BLOCK
    nf = D_FF // FF_TILE
    grid_spec = pltpu.PrefetchScalarGridSpec(
        num_scalar_prefetch=2,
        grid=(nb,),
        in_specs=[
            pl.BlockSpec((MOE_BLOCK, d), lambda b, be, nu: (jnp.minimum(b, nu[0] - 1), 0)),
            pl.BlockSpec(memory_space=pl.ANY),
            pl.BlockSpec(memory_space=pl.ANY),
        ],
        out_specs=pl.BlockSpec((MOE_BLOCK, d), lambda b, be, nu: (b, 0)),
        scratch_shapes=[
            pltpu.VMEM((MOE_BLOCK, d), BF16),
            pltpu.VMEM((nf, d, FF_TILE), BF16),
            pltpu.VMEM((nf, d, FF_TILE), BF16),
            pltpu.VMEM((nf, FF_TILE, d), BF16),
            pltpu.VMEM((d, FF_TILE), F32),
            pltpu.VMEM((d, FF_TILE), F32),
            pltpu.VMEM((FF_TILE, d), F32),
            pltpu.SemaphoreType.DMA((3,)),
        ],
    )
    return pl.pallas_call(
        functools.partial(_expert_kernel, layer=layer),
        grid_spec=grid_spec,
        out_shape=jax.ShapeDtypeStruct((p, d), F32),
        compiler_params=_cp(("arbitrary",)),
        name="moe_experts",
    )(blk_expert, n_used, buf, w_in, w_down)


def _combine_kernel(s0_ref, s1_ref, n0_ref, n1_ref, y_hbm, x_ref, gate_ref, lg_ref, lb_ref, x3_ref, x3b_ref,
                    y0_ref, y1_ref, sems):
    i = pl.program_id(0)
    tc = x_ref.shape[0]
    bufs = (y0_ref, y1_ref)
    cur = i % 2

    def issue_tile(slot_refs, half):
        def issue(t, carry):
            for kk in range(TOP_K):
                src = slot_refs[kk][t]
                pltpu.make_async_copy(y_hbm.at[pl.ds(src, 1), :], bufs[kk].at[half, pl.ds(t, 1), :],
                                      sems.at[half]).start()
            return carry

        lax.fori_loop(0, tc, issue, 0, unroll=8)

    @pl.when(i == 0)
    def _():
        issue_tile((s0_ref, s1_ref), 0)

    @pl.when(i + 1 < pl.num_programs(0))
    def _():
        issue_tile((n0_ref, n1_ref), 1 - cur)

    for kk in range(TOP_K):
        pltpu.make_async_copy(y_hbm.at[pl.ds(0, tc), :], bufs[kk].at[cur], sems.at[cur]).wait()
    gate = gate_ref[...]
    z = DN_ALPHA * x_ref[...] + gate[:, 0:1] * y0_ref[cur] + gate[:, 1:2] * y1_ref[cur]
    x3 = _layer_norm(z, lg_ref[...], lb_ref[...])
    x3_ref[...] = x3
    x3b_ref[...] = x3.astype(BF16)


def _combine(slot0, slot1, y, x2, gate_nt, lg, lb, tc):
    n, d = x2.shape
    last = n // tc - 1
    return pl.pallas_call(
        _combine_kernel,
        grid=(n // tc,),
        in_specs=[
            pl.BlockSpec((tc,), lambda i: (i,), memory_space=pltpu.SMEM),
            pl.BlockSpec((tc,), lambda i: (i,), memory_space=pltpu.SMEM),
            pl.BlockSpec((tc,), lambda i: (jnp.minimum(i + 1, last),), memory_space=pltpu.SMEM),
            pl.BlockSpec((tc,), lambda i: (jnp.minimum(i + 1, last),), memory_space=pltpu.SMEM),
            pl.BlockSpec(memory_space=pl.ANY),
            pl.BlockSpec((tc, d), lambda i: (i, 0)),
            pl.BlockSpec((tc, 2), lambda i: (i, 0)),
            pl.BlockSpec((1, d), lambda i: (0, 0)),
            pl.BlockSpec((1, d), lambda i: (0, 0)),
        ],
        out_specs=[pl.BlockSpec((tc, d), lambda i: (i, 0)), pl.BlockSpec((tc, d), lambda i: (i, 0))],
        out_shape=[jax.ShapeDtypeStruct((n, d), F32), jax.ShapeDtypeStruct((n, d), BF16)],
        scratch_shapes=[pltpu.VMEM((2, tc, d), F32), pltpu.VMEM((2, tc, d), F32), pltpu.SemaphoreType.DMA((2,))],
        compiler_params=_cp(("arbitrary",)),
        name="moe_combine_ln",
    )(slot0, slot1, slot0, slot1, y, x2, gate_nt, lg, lb)


def _layer(x, xb, mem_b, p, moe_w, layer, consts, B, T):
    n, d = x.shape
    G, HPG, DH = NSA_GROUPS, NSA_HPG, NSA_DH
    slopes, ovt, epad, tile_ind = consts

    h_big = _matmul(xb, p["w_big"], BF16, 1024, 1024)
    h_small = _matmul(xb, p["w_small"], F32, 1024, SCOL_END)
    h_t = _matmul_t(xb, p["w_t"], B, T, 1024, TROW_END // 2)

    o_gla = _gla(h_big, h_small, p["wa_pad"], p["b_a"], p["norm_g"], B, T)

    kcmp, kcmp_t = _compress(h_small, p["cmp_w1bd"], p["cmp_w2bd"], p["cmp_w2bdt"], p["cmp_pe_pair"], B, T)
    tq_sel = 256
    ocmp_t, mb, in_tile = _cmp_select(slopes, h_t, kcmp, kcmp_t, ovt, tile_ind, B, T, 512)
    nq = T // tq_sel
    tile_flags = (in_tile.reshape(B, G, nq, nq, tq_sel).max(axis=-1) > 0).astype(I32)
    tile_flags = tile_flags.transpose(0, 1, 3, 2).reshape(-1)
    gates_t = h_small[:, GLA_GATE_RANK:GLA_GATE_RANK + 3 * NSA_HEADS].reshape(B, T, G, 3 * HPG)
    gates_t = jnp.pad(gates_t.transpose(0, 2, 3, 1), ((0, 0), (0, 0), (0, 16 - 3 * HPG), (0, 0)))
    o_nsa = _sel_win(slopes, tile_flags, h_t, h_big, epad, mb, ocmp_t, gates_t, B, T, tq_sel)

    x1, x1b = _mix(o_gla, o_nsa, h_big, x, p["w_bg"], p["w_bn"], p["w_out"], p["ln_mix_g"], p["ln_mix_b"], 512)

    kvm = _matmul(mem_b, p["xa_wkv"], BF16, 512, 512).reshape(B, MEM_LEN, 2 * XA_HEADS * XA_DH)
    x2 = _xattn(x1, x1b, kvm, p["xa_wq"], p["xa_wo"], p["ln_xa_g"], p["ln_xa_b"], B, T, 512)

    e, gate, rank, cnt = _router(x2, p["rw_hi"], p["rw_lo"], p["rb"], 512)
    counts = cnt[:, 0].astype(I32)
    padded = (counts + MOE_BLOCK - 1) // MOE_BLOCK * MOE_BLOCK
    pad_end = jnp.cumsum(padded)
    pad_start = (pad_end - padded).astype(I32)
    nb = (n * TOP_K) // MOE_BLOCK + N_EXPERTS
    n_used = (pad_end[-1] // MOE_BLOCK).astype(I32).reshape(1)
    blk_start = jnp.arange(nb, dtype=I32) * MOE_BLOCK
    blk_expert = jnp.minimum(jnp.sum(blk_start[:, None] >= pad_end[None, :], axis=1), N_EXPERTS - 1).astype(I32)
    blk_expert = jnp.where(jnp.arange(nb) < n_used[0], blk_expert, blk_expert[jnp.maximum(n_used[0] - 1, 0)])
    slot = _slots(pad_start, e, rank, 2048)
    buf = _dispatch(pad_end.astype(I32), slot[0], slot[1], x2, nb * MOE_BLOCK, 512)
    y = _experts(blk_expert, n_used, buf, moe_w[0], moe_w[1], layer)
    x3, x3b = _combine(slot[0], slot[1], y, x2, gate.T, p["ln_ffn_g"], p["ln_ffn_b"], 256)
    return x3, x3b


def _prep_layer(l, w_in, gla_w_a2, gla_b_a, gla_norm_g, nsa_cmp_pe, nsa_cmp_w1, nsa_cmp_w2, w_branch_gla,
                w_branch_nsa, w_out, ln_mix_g, ln_mix_b, xa_wq, xa_wkv, xa_wo, ln_xa_g, ln_xa_b, router_w,
                router_b, moe_w_in, moe_w_down, ln_ffn_g, ln_ffn_b):
    d = w_in.shape[1]
    w = w_in[l]
    o_gq, o_gk, o_gv, o_gr = 0, GLA_QK, 2 * GLA_QK, 2 * GLA_QK + GLA_V
    o_ga = o_gr + GLA_V
    o_nq = o_ga + GLA_GATE_RANK
    o_nkv = o_nq + NSA_Q
    o_ng = o_nkv + 6 * NSA_KV
    o_mg = o_ng + 3 * NSA_HEADS
    G, DH = NSA_GROUPS, NSA_DH

    def kv_cols(kind):
        return w[:, o_nkv + kind * NSA_KV:o_nkv + (kind + 1) * NSA_KV]

    def slabs(wk):
        return jnp.pad(wk.reshape(d, G, DH), ((0, 0), (0, 0), (0, LANES - DH))).reshape(d, G * LANES)

    w_big = jnp.concatenate([w[:, o_mg:o_mg + 2 * d], w[:, o_gq:o_ga], slabs(kv_cols(2)), slabs(kv_cols(4))],
                            axis=1).astype(BF16)
    w_small = jnp.concatenate([w[:, o_ga:o_nq], w[:, o_ng:o_mg],
                               jnp.zeros((d, LANES - GLA_GATE_RANK - 3 * NSA_HEADS), F32),
                               kv_cols(0), kv_cols(1)], axis=1).astype(BF16)
    w_t = jnp.concatenate([w[:, o_nq:o_nkv], kv_cols(3), kv_cols(5)], axis=1).T.astype(BF16)
    w1 = nsa_cmp_w1[l].reshape(2, CMP_LEN, DH, CMP_HIDDEN)
    z1 = jnp.zeros_like(w1)
    w1bd = jnp.concatenate([jnp.concatenate([w1, z1], axis=3), jnp.concatenate([z1, w1], axis=3)], axis=2)
    w2 = nsa_cmp_w2[l]
    z2 = jnp.zeros_like(w2)
    w2bd = jnp.concatenate([jnp.concatenate([w2, z2], axis=2), jnp.concatenate([z2, w2], axis=2)], axis=1)
    pe = nsa_cmp_pe[l]
    pe_pair = jnp.broadcast_to(jnp.concatenate([pe, pe], axis=-1)[:, :, None, :], (2, CMP_LEN, 16, 2 * DH))
    wa_pad = jnp.concatenate([gla_w_a2[l], jnp.zeros((LANES - GLA_GATE_RANK, GLA_QK), F32)], axis=0).astype(BF16)
    rw_t = router_w.T
    rw_hi = rw_t.astype(BF16)
    rw_lo = (rw_t - rw_hi.astype(F32)).astype(BF16)
    return dict(
        w_big=w_big, w_small=w_small, w_t=w_t, wa_pad=wa_pad,
        b_a=gla_b_a[l].reshape(1, -1), norm_g=gla_norm_g[l].reshape(1, -1),
        cmp_w1bd=w1bd.astype(BF16), cmp_w2bd=w2bd.astype(BF16), cmp_w2bdt=w2bd.transpose(0, 2, 1).astype(BF16),
        cmp_pe_pair=pe_pair.astype(BF16),
        w_bg=w_branch_gla[l].astype(BF16), w_bn=w_branch_nsa[l].astype(BF16), w_out=w_out[l].astype(BF16),
        ln_mix_g=ln_mix_g[l].reshape(1, -1), ln_mix_b=ln_mix_b[l].reshape(1, -1),
        xa_wq=xa_wq[l].astype(BF16), xa_wkv=xa_wkv[l].astype(BF16), xa_wo=xa_wo[l].astype(BF16),
        ln_xa_g=ln_xa_g[l].reshape(1, -1), ln_xa_b=ln_xa_b[l].reshape(1, -1),
        rw_hi=rw_hi, rw_lo=rw_lo, rb=router_b.reshape(-1, 1),
        ln_ffn_g=ln_ffn_g[l].reshape(1, -1), ln_ffn_b=ln_ffn_b[l].reshape(1, -1),
    )


def kernel(x, mem, w_in, gla_w_a2, gla_b_a, gla_norm_g, nsa_cmp_pe, nsa_cmp_w1, nsa_cmp_w2, w_branch_gla, w_branch_nsa, w_out, ln_mix_g, ln_mix_b, xa_wq, xa_wkv, xa_wo, ln_xa_g, ln_xa_b, router_w, router_b, moe_w_in, moe_w_down, ln_ffn_g, ln_ffn_b):
    B, T, d = x.shape
    assert T % 512 == 0 and d == 2048 and mem.shape[1] == MEM_LEN
    n = B * T
    params = (w_in, gla_w_a2, gla_b_a, gla_norm_g, nsa_cmp_pe, nsa_cmp_w1, nsa_cmp_w2, w_branch_gla, w_branch_nsa,
              w_out, ln_mix_g, ln_mix_b, xa_wq, xa_wkv, xa_wo, ln_xa_g, ln_xa_b, router_w, router_b, moe_w_in,
              moe_w_down, ln_ffn_g, ln_ffn_b)
    slopes = (2.0 ** (-8.0 * jnp.arange(1, NSA_HEADS + 1, dtype=F32) / NSA_HEADS)).astype(F32)
    nc, ns = T // CMP_STRIDE, T // SEL_LEN
    cs = np.arange(nc) * CMP_STRIDE
    ss = np.arange(ns) * SEL_LEN
    ovt = ((cs[None, :] < ss[:, None] + SEL_LEN) & (cs[None, :] + CMP_LEN > ss[:, None])
           & (cs[None, :] + CMP_LEN <= T)).astype(np.float32)
    assert NSA_DH + ns <= LANES
    epad = np.zeros((T, LANES), np.float32)
    epad[np.arange(T), NSA_DH + np.arange(T) // SEL_LEN] = 1.0
    tile_ind = (np.arange(ns)[None, :] // (256 // SEL_LEN) == np.arange(T // 256)[:, None]).astype(np.float32)
    consts = (slopes, jnp.asarray(ovt, BF16), jnp.asarray(epad, BF16), jnp.asarray(tile_ind, BF16))

    xf = x.reshape(n, d)
    xb = xf.astype(BF16)
    mem_b = mem.reshape(B * MEM_LEN, d).astype(BF16)
    moe_w = (moe_w_in, moe_w_down)
    for l in range(DEPTH):
        p = _prep_layer(l, *params)
        xf, xb = _layer(xf, xb, mem_b, p, moe_w, l, consts, B, T)
    return xf.reshape(B, T, d)
```

```python
import functools

import jax
import jax.numpy as jnp
import numpy as np
from jax import lax
from jax.experimental import pallas as pl
from jax.experimental.pallas import tpu as pltpu

F32 = jnp.float32
BF16 = jnp.bfloat16
I32 = jnp.int32

DEPTH = 2
MEM_LEN = 256
GLA_HEADS = 4
GLA_DK = 128
GLA_DV = 256
GLA_GATE_RANK = 16
GLA_TAU = 16.0
GLA_CHUNK = 64
NSA_HEADS = 16
NSA_GROUPS = 4
NSA_HPG = NSA_HEADS // NSA_GROUPS
NSA_DH = 64
CMP_LEN = 32
CMP_STRIDE = 16
CMP_HIDDEN = 256
SEL_LEN = 64
SEL_TOPN = 8
WINDOW = 512
XA_HEADS = 4
XA_DH = 128
N_EXPERTS = 16
N_GROUPS = 4
EXPERTS_PER_GROUP = N_EXPERTS // N_GROUPS
TOP_K = 2
D_FF = 1536
DN_ALPHA = float((2 * DEPTH) ** 0.25)
LN_EPS = 1e-5
NEG = -1e30
LOG2E = 1.4426950408889634
FORCE_BONUS = 1e6

GLA_QK = GLA_HEADS * GLA_DK
GLA_V = GLA_HEADS * GLA_DV
NSA_Q = NSA_HEADS * NSA_DH
NSA_KV = NSA_GROUPS * NSA_DH

LANES = 128
VMEM_LIMIT = 56 * 1024 * 1024

COL_MG = 0
COL_GQ = 2 * 2048
COL_GK = COL_GQ + GLA_QK
COL_GV = COL_GK + GLA_QK
COL_GR = COL_GV + GLA_V
COL_KS = COL_GR + GLA_V
COL_KW = COL_KS + NSA_GROUPS * LANES
COL_END = COL_KW + NSA_GROUPS * LANES
SCOL_CK = LANES
SCOL_CV = SCOL_CK + NSA_KV
SCOL_END = SCOL_CV + NSA_KV
TROW_Q = 0
TROW_VS = NSA_Q
TROW_VW = TROW_VS + NSA_KV
TROW_END = TROW_VW + NSA_KV

MOE_BLOCK = 512
FF_TILE = 512


def _cp(sem):
    return pltpu.CompilerParams(dimension_semantics=sem, vmem_limit_bytes=VMEM_LIMIT)


def _dot(a, b):
    return jnp.dot(a, b, preferred_element_type=F32)


def _dot_nt(a, b):
    return lax.dot_general(a, b, (((1,), (1,)), ((), ())), preferred_element_type=F32)


def _dot_tn(a, b):
    return lax.dot_general(a, b, (((0,), (0,)), ((), ())), preferred_element_type=F32)


def _layer_norm(z, g, b):
    mu = jnp.mean(z, axis=-1, keepdims=True)
    zc = z - mu
    var = jnp.mean(zc * zc, axis=-1, keepdims=True)
    return zc * lax.rsqrt(var + LN_EPS) * g + b


def _mm_kernel(a_ref, b_ref, o_ref):
    o_ref[...] = _dot(a_ref[...], b_ref[...]).astype(o_ref.dtype)


def _matmul(a, b, out_dtype, tm, tn):
    m, k = a.shape
    n = b.shape[1]
    return pl.pallas_call(
        _mm_kernel,
        grid=(m // tm, n // tn),
        in_specs=[pl.BlockSpec((tm, k), lambda i, j: (i, 0)),
                  pl.BlockSpec((k, tn), lambda i, j: (0, j))],
        out_specs=pl.BlockSpec((tm, tn), lambda i, j: (i, j)),
        out_shape=jax.ShapeDtypeStruct((m, n), out_dtype),
        compiler_params=_cp(("parallel", "parallel")),
        name="matmul",
    )(a, b)


def _mm_nt_kernel(wt_ref, x_ref, o_ref):
    o_ref[0] = _dot_nt(wt_ref[...], x_ref[...]).astype(o_ref.dtype)


def _matmul_t(x, wt, B, T, tm, tr):
    n, k = x.shape
    r = wt.shape[0]
    nt = T // tm
    return pl.pallas_call(
        _mm_nt_kernel,
        grid=(n // tm, r // tr),
        in_specs=[pl.BlockSpec((tr, k), lambda i, j: (j, 0)),
                  pl.BlockSpec((tm, k), lambda i, j: (i, 0))],
        out_specs=pl.BlockSpec((1, tr, tm), lambda i, j: (i // nt, j, i % nt)),
        out_shape=jax.ShapeDtypeStruct((B, r, T), BF16),
        compiler_params=_cp(("parallel", "parallel")),
        name="matmul_t",
    )(wt, x)


def _gla_kernel(q_ref, k_ref, v_ref, r_ref, sm_ref, wa_ref, ba_ref, ng_ref, o_ref, st_ref):
    C = GLA_CHUNK
    seqs = q_ref.shape[1]
    n_chunks = q_ref.shape[2] // C

    @pl.when(pl.program_id(1) == 0)
    def _():
        st_ref[...] = jnp.zeros_like(st_ref)

    rowi = lax.broadcasted_iota(I32, (C, GLA_DK), 0)
    tt = lax.broadcasted_iota(I32, (C, C), 0)
    ss = lax.broadcasted_iota(I32, (C, C), 1)
    levels = (1, 2, 4, 8, 16, 32)
    pair_masks = [((tt // (2 * L)) == (ss // (2 * L))) & ((tt & L) != 0) & ((ss & L) == 0) for L in levels]
    diag_mask = tt == ss
    scale = GLA_DK ** -0.5

    def head_chunk(sq, rows, h, z):
        qk_cols = slice(h * GLA_DK, (h + 1) * GLA_DK)
        v_cols = slice(h * GLA_DV, (h + 1) * GLA_DV)
        state = sq * GLA_HEADS + h
        q = q_ref[0, sq, rows, qk_cols].astype(F32) * scale
        k = k_ref[0, sq, rows, qk_cols].astype(F32)
        v = v_ref[0, sq, rows, v_cols]
        g = (jnp.minimum(z, 0.0) - jnp.log1p(jnp.exp(-jnp.abs(z)))) * (1.0 / GLA_TAU)
        incl = g
        tot = g
        att = jnp.where(diag_mask, _dot_nt(q.astype(BF16), k.astype(BF16)), 0.0)
        for L, pm in zip(levels, pair_masks):
            ql = (q * jnp.exp(incl)).astype(BF16)
            kl = (k * jnp.exp(tot - incl)).astype(BF16)
            att = jnp.where(pm, _dot_nt(ql, kl), att)
            upper = (rowi & L) != 0
            from_lower = pltpu.roll(tot, L, 0)
            from_upper = pltpu.roll(tot, C - L, 0)
            incl = incl + jnp.where(upper, from_lower, 0.0)
            tot = tot + jnp.where(upper, from_lower, from_upper)
        qd = (q * jnp.exp(incl)).astype(BF16)
        kd = (k * jnp.exp(tot - incl)).astype(BF16)
        st = st_ref[state]
        o = _dot_nt(qd, st.astype(BF16)) + _dot(att.astype(BF16), v)
        st_ref[state] = st * jnp.exp(tot[0:1, :]) + _dot_tn(v, kd)
        mu = jnp.mean(o, axis=-1, keepdims=True)
        oc = o - mu
        var = jnp.mean(oc * oc, axis=-1, keepdims=True)
        on = oc * lax.rsqrt(var + LN_EPS) * ng_ref[:, v_cols]
        r = r_ref[0, sq, rows, v_cols].astype(F32)
        o_ref[0, sq, rows, v_cols] = (on * (r * jax.nn.sigmoid(r))).astype(o_ref.dtype)

    def chunk(c, carry):
        rows = pl.ds(pl.multiple_of(c * C, C), C)
        for sq in range(seqs):
            z = _dot(sm_ref[0, sq, rows, :].astype(BF16), wa_ref[...]) + ba_ref[...]
            for h in range(GLA_HEADS):
                head_chunk(sq, rows, h, z[:, h * GLA_DK:(h + 1) * GLA_DK])
        return carry

    lax.fori_loop(0, n_chunks, chunk, 0)


def _gla(h_big, h_small, wa_pad, b_a, norm_g, B, T, seqs, tb):
    groups = B // seqs
    hb = h_big.reshape(groups, seqs, T, h_big.shape[1])
    hs = h_small.reshape(groups, seqs, T, h_small.shape[1])
    out = pl.pallas_call(
        _gla_kernel,
        grid=(groups, T // tb),
        in_specs=[
            pl.BlockSpec((1, seqs, tb, GLA_QK), lambda b, j: (b, 0, j, COL_GQ // GLA_QK)),
            pl.BlockSpec((1, seqs, tb, GLA_QK), lambda b, j: (b, 0, j, COL_GK // GLA_QK)),
            pl.BlockSpec((1, seqs, tb, GLA_V), lambda b, j: (b, 0, j, COL_GV // GLA_V)),
            pl.BlockSpec((1, seqs, tb, GLA_V), lambda b, j: (b, 0, j, COL_GR // GLA_V)),
            pl.BlockSpec((1, seqs, tb, LANES), lambda b, j: (b, 0, j, 0)),
            pl.BlockSpec((LANES, GLA_QK), lambda b, j: (0, 0)),
            pl.BlockSpec((1, GLA_QK), lambda b, j: (0, 0)),
            pl.BlockSpec((1, GLA_V), lambda b, j: (0, 0)),
        ],
        out_specs=pl.BlockSpec((1, seqs, tb, GLA_V), lambda b, j: (b, 0, j, 0)),
        out_shape=jax.ShapeDtypeStruct((groups, seqs, T, GLA_V), BF16),
        scratch_shapes=[pltpu.VMEM((seqs * GLA_HEADS, GLA_DV, GLA_DK), F32)],
        compiler_params=_cp(("parallel", "arbitrary")),
        name="gla",
    )(hb, hb, hb, hb, hs, wa_pad, b_a, norm_g)
    return out.reshape(B * T, GLA_V)


def _compress_kernel(x_ref, w1_ref, w2_ref, w2t_ref, pe_ref, o_ref, ot_ref):
    nc = x_ref.shape[0] // CMP_STRIDE
    hid_w = w1_ref.shape[3]
    a = jnp.zeros((nc, hid_w), F32)
    bm = jnp.zeros((nc, hid_w), F32)
    c = jnp.zeros((pe_ref.shape[2], hid_w), F32)
    for l in range(CMP_STRIDE):
        xl = x_ref[pl.ds(l, nc, stride=CMP_STRIDE), :].astype(BF16)
        a = a + _dot(xl, w1_ref[0, l])
        bm = bm + _dot(xl, w1_ref[0, CMP_STRIDE + l])
    for l in range(CMP_LEN):
        c = c + _dot(pe_ref[0, l], w1_ref[0, l])
    hid = a + pltpu.roll(bm, nc - 1, 0) + c[0:1, :]
    act = jax.nn.gelu(hid).astype(BF16)
    o_ref[0, 0, 0] = _dot(act, w2_ref[0]).astype(o_ref.dtype)
    ot_ref[0, 0, 0] = _dot_nt(w2t_ref[0], act).astype(ot_ref.dtype)


def _compress(h_small, w1bd, w2bd, w2bdt, pe_pair, B, T):
    nc = T // CMP_STRIDE
    pairs = NSA_GROUPS // 2
    return pl.pallas_call(
        _compress_kernel,
        grid=(B, 2, pairs),
        in_specs=[
            pl.BlockSpec((T, LANES), lambda b, s, j: (b, SCOL_CK // LANES + s * pairs + j)),
            pl.BlockSpec((1,) + w1bd.shape[1:], lambda b, s, j: (s, 0, 0, 0)),
            pl.BlockSpec((1,) + w2bd.shape[1:], lambda b, s, j: (s, 0, 0)),
            pl.BlockSpec((1,) + w2bdt.shape[1:], lambda b, s, j: (s, 0, 0)),
            pl.BlockSpec((1,) + pe_pair.shape[1:], lambda b, s, j: (s, 0, 0, 0)),
        ],
        out_specs=[
            pl.BlockSpec((1, 1, 1, nc, LANES), lambda b, s, j: (b, s, j, 0, 0)),
            pl.BlockSpec((1, 1, 1, LANES, nc), lambda b, s, j: (b, s, j, 0, 0)),
        ],
        out_shape=[jax.ShapeDtypeStruct((B, 2, pairs, nc, LANES), BF16),
                   jax.ShapeDtypeStruct((B, 2, pairs, LANES, nc), BF16)],
        compiler_params=_cp(("parallel", "parallel", "parallel")),
        name="nsa_compress",
    )(h_small, w1bd, w2bd, w2bdt, pe_pair)


def _cmp_select_kernel(slopes_ref, qt_ref, kc_ref, vct_ref, ovt_ref, ind_ref, ocmp_ref, mb_ref, kt_ref, qpad_ref):
    g = pl.program_id(1)
    i = pl.program_id(2)
    tq = qt_ref.shape[2]
    nc = kc_ref.shape[3]
    ns = mb_ref.shape[2]
    dh = NSA_DH
    t0 = i * tq
    wide = NSA_HPG * tq
    tpos = t0 + (lax.broadcasted_iota(I32, (nc, wide), 1) & (tq - 1))
    nidx = lax.broadcasted_iota(I32, (nc, wide), 0)
    mask_c = (nidx * CMP_STRIDE + (CMP_LEN - 1)) <= tpos
    absd = jnp.abs(tpos.astype(F32) - (nidx.astype(F32) * CMP_STRIDE + 0.5 * (CMP_LEN - 1)))
    srow = jnp.concatenate([jnp.full((1, tq), slopes_ref[g * NSA_HPG + hh], F32) for hh in range(NSA_HPG)], axis=1)
    lower = g % 2 == 0
    kc = kc_ref[0, 0, 0]
    vct = jnp.where(lower, vct_ref[0, 0, 0, 0:dh, :], vct_ref[0, 0, 0, dh:2 * dh, :])
    for hh in range(NSA_HPG):
        q = qt_ref[0, hh * dh:(hh + 1) * dh, :] * jnp.asarray(dh ** -0.5, BF16)
        zero = jnp.zeros_like(q)
        qpad_ref[0:dh, hh * tq:(hh + 1) * tq] = jnp.where(lower, q, zero)
        qpad_ref[dh:2 * dh, hh * tq:(hh + 1) * tq] = jnp.where(lower, zero, q)
    s = _dot(kc, qpad_ref[...]) - srow * absd
    s = jnp.where(mask_c, s, NEG)
    e = jnp.exp(s - jnp.max(s, axis=0, keepdims=True))
    p = jnp.where(mask_c, e * (1.0 / jnp.sum(e, axis=0, keepdims=True)), 0.0)
    o = _dot(vct, p.astype(BF16))
    psum = jnp.zeros((nc, tq), F32)
    for hh in range(NSA_HPG):
        ocmp_ref[0, hh * dh:(hh + 1) * dh, :] = o[:, hh * tq:(hh + 1) * tq].astype(ocmp_ref.dtype)
        psum = psum + p[:, hh * tq:(hh + 1) * tq]
    p_hi = psum.astype(BF16)
    p_lo = (psum - p_hi.astype(F32)).astype(BF16)
    imp = _dot(ovt_ref[...], p_hi) + _dot(ovt_ref[...], p_lo)
    j = lax.broadcasted_iota(I32, (ns, tq), 0)
    tp = t0 + lax.broadcasted_iota(I32, (ns, tq), 1)
    cur = tp // SEL_LEN
    forced = (j == 0) | (j == cur) | (j == cur - 1)
    valid = j * SEL_LEN <= tp
    score = jnp.where(valid, imp + jnp.where(forced, FORCE_BONUS, 0.0), NEG)
    rank = jnp.zeros((ns, tq), F32)
    for jp in range(ns):
        row = score[jp:jp + 1, :]
        beats = (row > score) | ((row == score) & (j > jp))
        rank = rank + jnp.where(beats, 1.0, 0.0)
    keep = valid & (rank < float(min(SEL_TOPN, ns)))
    mb_ref[0, 0] = jnp.where(keep, 0.0, NEG).astype(mb_ref.dtype)
    kt_ref[0, 0] = _dot(ind_ref[...], jnp.where(keep, 1.0, 0.0).astype(BF16))


def _cmp_select(slopes, h_t, kcmp, kcmp_t, ovt, tile_ind, B, T, tq):
    nc = T // CMP_STRIDE
    ns = T // SEL_LEN
    nkt = tile_ind.shape[0]
    grp_rows = NSA_HPG * NSA_DH
    grid_spec = pltpu.PrefetchScalarGridSpec(
        num_scalar_prefetch=1,
        grid=(B, NSA_GROUPS, T // tq),
        in_specs=[
            pl.BlockSpec((1, grp_rows, tq), lambda b, g, i, s: (b, TROW_Q // grp_rows + g, i)),
            pl.BlockSpec((1, 1, 1, nc, LANES), lambda b, g, i, s: (b, 0, g // 2, 0, 0)),
            pl.BlockSpec((1, 1, 1, LANES, nc), lambda b, g, i, s: (b, 1, g // 2, 0, 0)),
            pl.BlockSpec((ns, nc), lambda b, g, i, s: (0, 0)),
            pl.BlockSpec((nkt, ns), lambda b, g, i, s: (0, 0)),
        ],
        out_specs=[
            pl.BlockSpec((1, grp_rows, tq), lambda b, g, i, s: (b, g, i)),
            pl.BlockSpec((1, 1, ns, tq), lambda b, g, i, s: (b, g, 0, i)),
            pl.BlockSpec((1, 1, nkt, tq), lambda b, g, i, s: (b, g, 0, i)),
        ],
        scratch_shapes=[pltpu.VMEM((LANES, NSA_HPG * tq), BF16)],
    )
    return pl.pallas_call(
        _cmp_select_kernel,
        grid_spec=grid_spec,
        out_shape=[jax.ShapeDtypeStruct((B, NSA_Q, T), BF16),
                   jax.ShapeDtypeStruct((B, NSA_GROUPS, ns, T), BF16),
                   jax.ShapeDtypeStruct((B, NSA_GROUPS, nkt, T), F32)],
        compiler_params=_cp(("parallel", "parallel", "parallel")),
        name="nsa_cmp_select",
    )(slopes, h_t, kcmp, kcmp_t, ovt, tile_ind)


def _sel_win_kernel(slopes_ref, flags_ref, qt_ref, ks_ref, kw_ref, vs_ref, vw_ref, epad_ref, mb_ref, ocmp_ref, gt_ref,
                    o_ref, qaug_ref, m_ref, acc_ref, srow_ref, bias_ref, s_ref, p_ref, alpha_ref,
                    kall_ref, vall_ref, tiles_ref, *, n_tiles):
    g = pl.program_id(1)
    i = pl.program_id(2)
    tq = qt_ref.shape[2]
    tk = tq
    ns = mb_ref.shape[2]
    dh = NSA_DH
    wide = NSA_HPG * tq
    t0 = i * tq
    BIG = -NEG

    @pl.when(i == 0)
    def _():
        srow = jnp.concatenate([jnp.full((1, tq), slopes_ref[g * NSA_HPG + hh] * LOG2E, F32)
                                for hh in range(NSA_HPG)], axis=1)
        srow_ref[...] = srow
        lane = lax.broadcasted_iota(I32, (tk, wide), 1) & (tq - 1)
        dist0 = (lane - lax.broadcasted_iota(I32, (tk, wide), 0)).astype(F32)
        sd0 = srow * dist0
        bias_ref[0] = sd0
        bias_ref[1] = sd0 + jnp.where(dist0 >= 0.0, 0.0, BIG)
        bias_ref[2] = sd0 + jnp.where(dist0 < 0.0, 0.0, BIG)
        bias_ref[3] = jnp.full((tk, wide), BIG, F32)
        kall_ref[0] = ks_ref[...] + epad_ref[...]
        kall_ref[1] = kw_ref[...]
        extra = jnp.where(lax.broadcasted_iota(I32, (vall_ref.shape[1] - dh, vall_ref.shape[2]), 0) == 0, 1.0, 0.0)
        vall_ref[0, 0:dh, :] = vs_ref[0]
        vall_ref[1, 0:dh, :] = vw_ref[0]
        vall_ref[0, dh:, :] = extra.astype(BF16)
        vall_ref[1, dh:, :] = extra.astype(BF16)

    for hh in range(NSA_HPG):
        cols = slice(hh * tq, (hh + 1) * tq)
        q = qt_ref[0, hh * dh:(hh + 1) * dh, :].astype(F32) * (dh ** -0.5 * LOG2E)
        qaug_ref[0:dh, cols] = q.astype(BF16)
        qaug_ref[dh:dh + ns, cols] = mb_ref[0, 0]
        qaug_ref[dh + ns:, cols] = jnp.zeros((qaug_ref.shape[0] - dh - ns, tq), BF16)

    m_ref[...] = jnp.full(m_ref.shape, NEG, F32)
    acc_ref[...] = jnp.zeros(acc_ref.shape, F32)

    n_back = WINDOW // tk
    flag_base = ((pl.program_id(0) * NSA_GROUPS + g) * n_tiles + i) * n_tiles
    n_sel = jnp.int32(0)
    for kb_static in range(n_tiles - 1):
        active = (kb_static < i) & (flags_ref[flag_base + kb_static] != 0)
        tiles_ref[n_sel] = kb_static
        n_sel = n_sel + active.astype(I32)
    tiles_ref[n_sel] = i
    n_sel = n_sel + 1
    n_win = jnp.minimum(i, n_back) + 1
    n_steps = n_sel + n_win

    def describe(n):
        n = jnp.maximum(n, 0)
        is_win = n >= n_sel
        kb_sel = tiles_ref[jnp.minimum(n, n_sel - 1)]
        kb = jnp.clip(jnp.where(is_win, i - n_win + 1 + (n - n_sel), kb_sel), 0, i)
        mode = jnp.where(kb == i, 1, jnp.where(is_win & (kb == i - n_back), 2, 0))
        mode = jnp.where(n >= n_steps, 3, mode)
        return is_win.astype(I32), kb, mode

    def scores(n, slot):
        br, kb, _ = describe(n)
        s0 = pl.multiple_of(kb * tk, tk)
        s_ref[slot] = _dot(kall_ref[br, pl.ds(s0, tk), :], qaug_ref[...])

    def softmax(n, slot):
        br, kb, mode = describe(n)
        crow = srow_ref[...] * ((i - kb) * tk).astype(F32)
        s = s_ref[slot] - bias_ref[mode]
        m_old = m_ref[br]
        m_new = jnp.maximum(m_old, jnp.max(s, axis=0, keepdims=True) - crow)
        alpha = jnp.exp2(m_old - m_new)
        p = jnp.exp2(s - (m_new + crow))
        m_ref[br] = m_new
        alpha_ref[slot] = alpha
        p_ref[slot] = p.astype(BF16)

    def weighted_values(n, slot):
        br, kb, _ = describe(n)
        s0 = pl.multiple_of(kb * tk, tk)
        acc_ref[br] = alpha_ref[slot] * acc_ref[br] + _dot(vall_ref[br, :, pl.ds(s0, tk)], p_ref[slot])

    p_ref[1] = jnp.zeros(p_ref.shape[1:], BF16)
    alpha_ref[1] = jnp.ones(alpha_ref.shape[1:], F32)
    scores(0, 0)

    def pair(j, carry):
        n = 2 * j
        scores(n + 1, 1)
        softmax(n, 0)
        weighted_values(n - 1, 1)
        scores(n + 2, 0)
        softmax(n + 1, 1)
        weighted_values(n, 0)
        return carry

    n_pairs = n_steps // 2
    lax.fori_loop(0, n_pairs, pair, 0)
    weighted_values(2 * n_pairs - 1, 1)

    @pl.when(n_steps % 2 == 1)
    def _():
        softmax(n_steps - 1, 0)
        weighted_values(n_steps - 1, 0)

    def gate_row(branch):
        rows = [gt_ref[0, 0, 3 * hh + branch:3 * hh + branch + 1, :] for hh in range(NSA_HPG)]
        return jax.nn.sigmoid(jnp.concatenate(rows, axis=1))

    o = (acc_ref[0, 0:dh, :] * (gate_row(1) / acc_ref[0, dh:dh + 1, :])
         + acc_ref[1, 0:dh, :] * (gate_row(2) / acc_ref[1, dh:dh + 1, :]))
    ocmp = jnp.concatenate([ocmp_ref[0, hh * dh:(hh + 1) * dh, :] for hh in range(NSA_HPG)], axis=1).astype(F32)
    o = o + gate_row(0) * ocmp
    o_heads = jnp.concatenate([o[:, hh * tq:(hh + 1) * tq] for hh in range(NSA_HPG)], axis=0)
    o_ref[...] = o_heads.T.astype(o_ref.dtype)


def _sel_win(slopes, tile_flags, h_t, h_big, epad, mb, ocmp_t, gates_t, B, T, tq):
    ns = T // SEL_LEN
    kaug = LANES
    G = NSA_GROUPS
    grp_rows = NSA_HPG * NSA_DH
    nq = T // tq
    grid_spec = pltpu.PrefetchScalarGridSpec(
        num_scalar_prefetch=2,
        grid=(B, G, nq),
        in_specs=[
            pl.BlockSpec((1, grp_rows, tq), lambda b, g, i, s, f: (b, TROW_Q // grp_rows + g, i)),
            pl.BlockSpec((T, LANES), lambda b, g, i, s, f: (b, COL_KS // LANES + g)),
            pl.BlockSpec((T, LANES), lambda b, g, i, s, f: (b, COL_KW // LANES + g)),
            pl.BlockSpec((1, NSA_DH, T), lambda b, g, i, s, f: (b, TROW_VS // NSA_DH + g, 0)),
            pl.BlockSpec((1, NSA_DH, T), lambda b, g, i, s, f: (b, TROW_VW // NSA_DH + g, 0)),
            pl.BlockSpec((T, LANES), lambda b, g, i, s, f: (0, 0)),
            pl.BlockSpec((1, 1, ns, tq), lambda b, g, i, s, f: (b, g, 0, i)),
            pl.BlockSpec((1, grp_rows, tq), lambda b, g, i, s, f: (b, g, i)),
            pl.BlockSpec((1, 1, 16, tq), lambda b, g, i, s, f: (b, g, 0, i)),
        ],
        out_specs=pl.BlockSpec((tq, grp_rows), lambda b, g, i, s, f: (b * nq + i, g)),
        scratch_shapes=[
            pltpu.VMEM((kaug, NSA_HPG * tq), BF16),
            pltpu.VMEM((2, 1, NSA_HPG * tq), F32),
            pltpu.VMEM((2, NSA_DH + 16, NSA_HPG * tq), F32),
            pltpu.VMEM((1, NSA_HPG * tq), F32),
            pltpu.VMEM((4, tq, NSA_HPG * tq), F32),
            pltpu.VMEM((2, tq, NSA_HPG * tq), F32),
            pltpu.VMEM((2, tq, NSA_HPG * tq), BF16),
            pltpu.VMEM((2, 1, NSA_HPG * tq), F32),
            pltpu.VMEM((2, T, kaug), BF16),
            pltpu.VMEM((2, NSA_DH + 16, T), BF16),
            pltpu.SMEM((nq,), I32),
        ],
    )
    return pl.pallas_call(
        functools.partial(_sel_win_kernel, n_tiles=nq),
        grid_spec=grid_spec,
        out_shape=jax.ShapeDtypeStruct((B * T, NSA_Q), BF16),
        compiler_params=_cp(("parallel", "parallel", "arbitrary")),
        name="nsa_sel_win",
    )(slopes, tile_flags, h_t, h_big, h_big, h_t, h_t, epad, mb, ocmp_t, gates_t)


def _mix_kernel(og_ref, on_ref, mg1_ref, mg2_ref, x_ref, wg_ref, wn_ref, wo_ref, lg_ref, lb_ref,
                x1_ref, x1b_ref):
    g1 = _dot(og_ref[...], wg_ref[...])
    g2 = _dot(on_ref[...], wn_ref[...])
    merged = (jax.nn.sigmoid(mg1_ref[...].astype(F32)) * g1
              + jax.nn.sigmoid(mg2_ref[...].astype(F32)) * g2)
    y = _dot(merged.astype(BF16), wo_ref[...])
    x1 = _layer_norm(DN_ALPHA * x_ref[...] + y, lg_ref[...], lb_ref[...])
    x1_ref[...] = x1
    x1b_ref[...] = x1.astype(BF16)


def _const_spec(shape):
    nd = len(shape)
    return pl.BlockSpec(shape, lambda *_: (0,) * nd, pipeline_mode=pl.Buffered(1))


def _mix(o_gla, o_nsa, h_big, x, wg, wn, wo, lg, lb, tm):
    n, d = x.shape
    return pl.pallas_call(
        _mix_kernel,
        grid=(n // tm,),
        in_specs=[
            pl.BlockSpec((tm, GLA_V), lambda i: (i, 0)),
            pl.BlockSpec((tm, NSA_Q), lambda i: (i, 0)),
            pl.BlockSpec((tm, d), lambda i: (i, 0)),
            pl.BlockSpec((tm, d), lambda i: (i, 1)),
            pl.BlockSpec((tm, d), lambda i: (i, 0)),
            _const_spec(wg.shape), _const_spec(wn.shape), _const_spec(wo.shape),
            _const_spec(lg.shape), _const_spec(lb.shape),
        ],
        out_specs=[pl.BlockSpec((tm, d), lambda i: (i, 0)), pl.BlockSpec((tm, d), lambda i: (i, 0))],
        out_shape=[jax.ShapeDtypeStruct((n, d), F32), jax.ShapeDtypeStruct((n, d), BF16)],
        compiler_params=_cp(("parallel",)),
        name="mix_ln",
    )(o_gla, o_nsa, h_big, h_big, x, wg, wn, wo, lg, lb)


def _xattn_kernel(x_ref, xb_ref, kv_ref, wq_ref, wo_ref, lg_ref, lb_ref, x2_ref):
    hd = XA_HEADS * XA_DH
    q = (_dot(xb_ref[...], wq_ref[...]) * (XA_DH ** -0.5)).astype(BF16)
    outs = []
    for h in range(XA_HEADS):
        kh = kv_ref[0, :, h * XA_DH:(h + 1) * XA_DH]
        vh = kv_ref[0, :, hd + h * XA_DH:hd + (h + 1) * XA_DH]
        s = _dot_nt(q[:, h * XA_DH:(h + 1) * XA_DH], kh)
        e = jnp.exp(s - jnp.max(s, axis=-1, keepdims=True))
        p = e / jnp.sum(e, axis=-1, keepdims=True)
        outs.append(_dot(p.astype(BF16), vh).astype(BF16))
    o = jnp.concatenate(outs, axis=-1)
    y = _dot(o, wo_ref[...])
    x2_ref[...] = _layer_norm(DN_ALPHA * x_ref[...] + y, lg_ref[...], lb_ref[...])


def _xattn(x1, x1b, kv, wq, wo, lg, lb, B, T, tm):
    n, d = x1.shape
    nt = T // tm
    return pl.pallas_call(
        _xattn_kernel,
        grid=(B, nt),
        in_specs=[
            pl.BlockSpec((tm, d), lambda b, i: (b * nt + i, 0)),
            pl.BlockSpec((tm, d), lambda b, i: (b * nt + i, 0)),
            pl.BlockSpec((1,) + kv.shape[1:], lambda b, i: (b, 0, 0)),
            _const_spec(wq.shape), _const_spec(wo.shape), _const_spec(lg.shape), _const_spec(lb.shape),
        ],
        out_specs=pl.BlockSpec((tm, d), lambda b, i: (b * nt + i, 0)),
        out_shape=jax.ShapeDtypeStruct((n, d), F32),
        compiler_params=_cp(("parallel", "parallel")),
        name="xattn_ln",
    )(x1, x1b, kv, wq, wo, lg, lb)


def _router_kernel(x_ref, wh_ref, wl_ref, rb_ref, e_ref, gate_ref, rank_ref, cnt_ref, carry_ref):
    i = pl.program_id(0)
    tr = x_ref.shape[0]
    E = N_EXPERTS

    @pl.when(i == 0)
    def _():
        carry_ref[...] = jnp.zeros_like(carry_ref)

    x = x_ref[...]
    x_hi = x.astype(BF16)
    x_lo = (x - x_hi.astype(F32)).astype(BF16)
    wh = wh_ref[...]
    logits = _dot_nt(wh, x_hi) + _dot_nt(wh, x_lo) + _dot_nt(wl_ref[...], x_hi)
    biased = logits + rb_ref[...]
    rows = [biased[e:e + 1, :] for e in range(E)]
    raw = [logits[e:e + 1, :] for e in range(E)]
    best_score = None
    best = None
    for gi in range(N_GROUPS):
        v = rows[gi * EXPERTS_PER_GROUP:(gi + 1) * EXPERTS_PER_GROUP]
        sc = None
        for a in range(EXPERTS_PER_GROUP):
            for b in range(a + 1, EXPERTS_PER_GROUP):
                pair = v[a] + v[b]
                sc = pair if sc is None else jnp.maximum(sc, pair)
        if best is None:
            best_score, best = sc, jnp.zeros((1, tr), I32)
        else:
            better = sc > best_score
            best_score = jnp.where(better, sc, best_score)
            best = jnp.where(better, gi, best)

    def pick(vals):
        out = vals[0:EXPERTS_PER_GROUP]
        for gi in range(1, N_GROUPS):
            out = [jnp.where(best == gi, vals[gi * EXPERTS_PER_GROUP + a], out[a]) for a in range(EXPERTS_PER_GROUP)]
        return out

    w = pick(rows)
    lraw = pick(raw)
    i1 = jnp.zeros((1, tr), I32)
    v1 = w[0]
    l1 = lraw[0]
    for a in range(1, EXPERTS_PER_GROUP):
        better = w[a] > v1
        v1 = jnp.where(better, w[a], v1)
        l1 = jnp.where(better, lraw[a], l1)
        i1 = jnp.where(better, a, i1)
    i2 = jnp.full((1, tr), -1, I32)
    v2 = jnp.full((1, tr), -jnp.inf, F32)
    l2 = jnp.zeros((1, tr), F32)
    for a in range(EXPERTS_PER_GROUP):
        better = (i1 != a) & ((w[a] > v2) | (i2 < 0))
        v2 = jnp.where(better, w[a], v2)
        l2 = jnp.where(better, lraw[a], l2)
        i2 = jnp.where(better, a, i2)
    e1 = best * EXPERTS_PER_GROUP + i1
    e2 = best * EXPERTS_PER_GROUP + i2
    mx = jnp.maximum(l1, l2)
    p1 = jnp.exp(l1 - mx)
    p2 = jnp.exp(l2 - mx)
    den = p1 + p2
    e_ref[0:1, :] = e1
    e_ref[1:2, :] = e2
    gate_ref[0:1, :] = p1 / den
    gate_ref[1:2, :] = p2 / den
    eidx = lax.broadcasted_iota(I32, (E, tr), 0)
    is1 = eidx == e1
    is2 = eidx == e2
    member = jnp.where(is1 | is2, 1.0, 0.0)
    uu = lax.broadcasted_iota(I32, (tr, tr), 0)
    tt = lax.broadcasted_iota(I32, (tr, tr), 1)
    tri = jnp.where(uu <= tt, 1.0, 0.0).astype(BF16)
    incl = _dot(member.astype(BF16), tri)
    excl = carry_ref[:, 0:1] + incl - member
    rank_ref[0:1, :] = jnp.sum(jnp.where(is1, excl, 0.0), axis=0, keepdims=True).astype(I32)
    rank_ref[1:2, :] = jnp.sum(jnp.where(is2, excl, 0.0), axis=0, keepdims=True).astype(I32)
    new_carry = carry_ref[...] + jnp.sum(member, axis=1, keepdims=True)
    carry_ref[...] = new_carry
    cnt_ref[...] = new_carry


def _router(x2, rw_hi, rw_lo, rb, tr):
    n, d = x2.shape
    E = N_EXPERTS
    return pl.pallas_call(
        _router_kernel,
        grid=(n // tr,),
        in_specs=[
            pl.BlockSpec((tr, d), lambda i: (i, 0)),
            pl.BlockSpec((E, d), lambda i: (0, 0)),
            pl.BlockSpec((E, d), lambda i: (0, 0)),
            pl.BlockSpec((E, 1), lambda i: (0, 0)),
        ],
        out_specs=[
            pl.BlockSpec((2, tr), lambda i: (0, i)),
            pl.BlockSpec((2, tr), lambda i: (0, i)),
            pl.BlockSpec((2, tr), lambda i: (0, i)),
            pl.BlockSpec((E, LANES), lambda i: (0, 0)),
        ],
        out_shape=[jax.ShapeDtypeStruct((2, n), I32), jax.ShapeDtypeStruct((2, n), F32),
                   jax.ShapeDtypeStruct((2, n), I32), jax.ShapeDtypeStruct((E, LANES), F32)],
        scratch_shapes=[pltpu.VMEM((E, LANES), F32)],
        compiler_params=_cp(("arbitrary",)),
        name="moe_router",
    )(x2, rw_hi, rw_lo, rb)


def _slot_kernel(ps_ref, e_ref, rank_ref, slot_ref):
    e = e_ref[...]
    start = jnp.zeros(e.shape, I32)
    for ex in range(N_EXPERTS):
        start = jnp.where(e == ex, ps_ref[ex], start)
    slot_ref[...] = start + rank_ref[...]


def _slots(pad_start, e, rank, ts):
    n = e.shape[1]
    grid_spec = pltpu.PrefetchScalarGridSpec(
        num_scalar_prefetch=1,
        grid=(n // ts,),
        in_specs=[pl.BlockSpec((TOP_K, ts), lambda i, s: (0, i)), pl.BlockSpec((TOP_K, ts), lambda i, s: (0, i))],
        out_specs=pl.BlockSpec((TOP_K, ts), lambda i, s: (0, i)),
    )
    return pl.pallas_call(
        _slot_kernel,
        grid_spec=grid_spec,
        out_shape=jax.ShapeDtypeStruct((TOP_K, n), I32),
        compiler_params=_cp(("parallel",)),
        name="moe_slots",
    )(pad_start, e, rank)


def _dispatch_kernel(pe_ref, s0_ref, s1_ref, x_ref, buf_hbm, zero_ref, sem):
    td = s0_ref.shape[0]
    slots = (s0_ref, s1_ref)

    @pl.when(pl.program_id(0) == 0)
    def _():
        zero_ref[...] = jnp.zeros_like(zero_ref)

        def zero_copy(ex):
            last = pl.multiple_of(jnp.maximum(pe_ref[ex] - MOE_BLOCK, 0), MOE_BLOCK)
            return pltpu.make_async_copy(zero_ref, buf_hbm.at[pl.ds(last, MOE_BLOCK), :], sem)

        def nonempty(ex):
            return pe_ref[ex] > (pe_ref[ex - 1] if ex > 0 else 0)

        n_blocks = buf_hbm.shape[0] // MOE_BLOCK
        first_unused = pe_ref[N_EXPERTS - 1] // MOE_BLOCK

        def tail_copy(k):
            row = pl.multiple_of((first_unused + k) * MOE_BLOCK, MOE_BLOCK)
            return pltpu.make_async_copy(zero_ref, buf_hbm.at[pl.ds(row, MOE_BLOCK), :], sem)

        for ex in range(N_EXPERTS):
            pl.when(nonempty(ex))(lambda ex=ex: zero_copy(ex).start())
            pl.when(first_unused + ex < n_blocks)(lambda ex=ex: tail_copy(ex).start())
        for ex in range(N_EXPERTS):
            pl.when(nonempty(ex))(lambda ex=ex: zero_copy(ex).wait())
            pl.when(first_unused + ex < n_blocks)(lambda ex=ex: tail_copy(ex).wait())

    def issue(t, carry):
        for kk in range(TOP_K):
            dest = slots[kk][t]
            pltpu.make_async_copy(x_ref.at[pl.ds(t, 1), :], buf_hbm.at[pl.ds(dest, 1), :], sem).start()
        return carry

    lax.fori_loop(0, td, issue, 0, unroll=8)
    for kk in range(TOP_K):
        pltpu.make_async_copy(x_ref, buf_hbm.at[pl.ds(0, td), :], sem).wait()


def _dispatch(pad_end, slot0, slot1, x2, n_rows, td):
    n, d = x2.shape
    grid_spec = pltpu.PrefetchScalarGridSpec(
        num_scalar_prefetch=1,
        grid=(n // td,),
        in_specs=[
            pl.BlockSpec((td,), lambda i, s: (i,), memory_space=pltpu.SMEM),
            pl.BlockSpec((td,), lambda i, s: (i,), memory_space=pltpu.SMEM),
            pl.BlockSpec((td, d), lambda i, s: (i, 0)),
        ],
        out_specs=pl.BlockSpec(memory_space=pl.ANY),
        scratch_shapes=[pltpu.VMEM((MOE_BLOCK, d), F32), pltpu.SemaphoreType.DMA(())],
    )
    return pl.pallas_call(
        _dispatch_kernel,
        grid_spec=grid_spec,
        out_shape=jax.ShapeDtypeStruct((n_rows, d), F32),
        compiler_params=_cp(("arbitrary",)),
        name="moe_dispatch",
    )(pad_end, slot0, slot1, x2)


def _expert_kernel(be_ref, nb_ref, x_ref, win_hbm, wdn_hbm, y_ref, xb_ref, wa_s, wu_s, wd_s, sa, su, sd, sems,
                   *, layer):
    b = pl.program_id(0)
    nf = D_FF // FF_TILE
    n_used = nb_ref[0]
    e = be_ref[b]
    e_prev = be_ref[jnp.maximum(b - 1, 0)]
    e_next = be_ref[jnp.minimum(b + 1, pl.num_programs(0) - 1)]
    active = b < n_used
    is_first = active & ((b == 0) | (e_prev != e))
    feeds_next = active & (b + 1 < n_used) & (e_next != e)

    def tile_copies(ex, f):
        lo = f * FF_TILE
        return (pltpu.make_async_copy(win_hbm.at[layer, ex, :, pl.ds(lo, FF_TILE)], sa, sems.at[0]),
                pltpu.make_async_copy(win_hbm.at[layer, ex, :, pl.ds(D_FF + lo, FF_TILE)], su, sems.at[1]),
                pltpu.make_async_copy(wdn_hbm.at[layer, ex, pl.ds(lo, FF_TILE), :], sd, sems.at[2]))

    def start(ex, f):
        for c in tile_copies(ex, f):
            c.start()

    def finish(ex, f):
        for c in tile_copies(ex, f):
            c.wait()
        wa_s[f] = sa[...].astype(BF16)
        wu_s[f] = su[...].astype(BF16)
        wd_s[f] = sd[...].astype(BF16)

    @pl.when(b == 0)
    def _():
        for f in range(nf - 1):
            start(e, f)
            finish(e, f)
        start(e, nf - 1)

    @pl.when(jnp.logical_not(active))
    def _():
        y_ref[...] = jnp.zeros_like(y_ref)

    @pl.when(active)
    def _():
        xb_ref[...] = x_ref[...].astype(BF16)
        for f in range(nf):
            xb = xb_ref[...]
            a = _dot(xb, wa_s[f])
            u = _dot(xb, wu_s[f])
            act = (a * jax.nn.sigmoid(a) * u).astype(BF16)
            y = _dot(act, wd_s[f])
            if f == 0:
                y_ref[...] = y
                pl.when(is_first)(lambda: finish(e, nf - 1))
            else:
                y_ref[...] += y

            @pl.when(feeds_next)
            def _(f=f):
                if f >= 1:
                    finish(e_next, f - 1)
                start(e_next, f)


def _experts(blk_expert, n_used, buf, w_in, w_down, layer):
    p, d = buf.shape
    nb = p // MOE_BLOCK
    nf = D_FF // FF_TILE
    grid_spec = pltpu.PrefetchScalarGridSpec(
        num_scalar_prefetch=2,
        grid=(nb,),
        in_specs=[
            pl.BlockSpec((MOE_BLOCK, d), lambda b, be, nu: (jnp.minimum(b, nu[0] - 1), 0)),
            pl.BlockSpec(memory_space=pl.ANY),
            pl.BlockSpec(memory_space=pl.ANY),
        ],
        out_specs=pl.BlockSpec((MOE_BLOCK, d), lambda b, be, nu: (b, 0)),
        scratch_shapes=[
            pltpu.VMEM((MOE_BLOCK, d), BF16),
            pltpu.VMEM((nf, d, FF_TILE), BF16),
            pltpu.VMEM((nf, d, FF_TILE), BF16),
            pltpu.VMEM((nf, FF_TILE, d), BF16),
            pltpu.VMEM((d, FF_TILE), F32),
            pltpu.VMEM((d, FF_TILE), F32),
            pltpu.VMEM((FF_TILE, d), F32),
            pltpu.SemaphoreType.DMA((3,)),
        ],
    )
    return pl.pallas_call(
        functools.partial(_expert_kernel, layer=layer),
        grid_spec=grid_spec,
        out_shape=jax.ShapeDtypeStruct((p, d), F32),
        compiler_params=_cp(("arbitrary",)),
        name="moe_experts",
    )(blk_expert, n_used, buf, w_in, w_down)


def _combine_kernel(s0_ref, s1_ref, n0_ref, n1_ref, y_hbm, x_ref, gate_ref, lg_ref, lb_ref, x3_ref, x3b_ref,
                    y0_ref, y1_ref, sems):
    i = pl.program_id(0)
    tc = x_ref.shape[0]
    bufs = (y0_ref, y1_ref)
    cur = i % 2

    def issue_tile(slot_refs, half):
        def issue(t, carry):
            for kk in range(TOP_K):
                src = slot_refs[kk][t]
                pltpu.make_async_copy(y_hbm.at[pl.ds(src, 1), :], bufs[kk].at[half, pl.ds(t, 1), :],
                                      sems.at[half]).start()
            return carry

        lax.fori_loop(0, tc, issue, 0, unroll=8)

    @pl.when(i == 0)
    def _():
        issue_tile((s0_ref, s1_ref), 0)

    @pl.when(i + 1 < pl.num_programs(0))
    def _():
        issue_tile((n0_ref, n1_ref), 1 - cur)

    for kk in range(TOP_K):
        pltpu.make_async_copy(y_hbm.at[pl.ds(0, tc), :], bufs[kk].at[cur], sems.at[cur]).wait()
    gate = gate_ref[...]
    z = DN_ALPHA * x_ref[...] + gate[:, 0:1] * y0_ref[cur] + gate[:, 1:2] * y1_ref[cur]
    x3 = _layer_norm(z, lg_ref[...], lb_ref[...])
    x3_ref[...] = x3
    x3b_ref[...] = x3.astype(BF16)


def _combine(slot0, slot1, y, x2, gate_nt, lg, lb, tc):
    n, d = x2.shape
    last = n // tc - 1
    return pl.pallas_call(
        _combine_kernel,
        grid=(n // tc,),
        in_specs=[
            pl.BlockSpec((tc,), lambda i: (i,), memory_space=pltpu.SMEM),
            pl.BlockSpec((tc,), lambda i: (i,), memory_space=pltpu.SMEM),
            pl.BlockSpec((tc,), lambda i: (jnp.minimum(i + 1, last),), memory_space=pltpu.SMEM),
            pl.BlockSpec((tc,), lambda i: (jnp.minimum(i + 1, last),), memory_space=pltpu.SMEM),
            pl.BlockSpec(memory_space=pl.ANY),
            pl.BlockSpec((tc, d), lambda i: (i, 0)),
            pl.BlockSpec((tc, 2), lambda i: (i, 0)),
            pl.BlockSpec((1, d), lambda i: (0, 0)),
            pl.BlockSpec((1, d), lambda i: (0, 0)),
        ],
        out_specs=[pl.BlockSpec((tc, d), lambda i: (i, 0)), pl.BlockSpec((tc, d), lambda i: (i, 0))],
        out_shape=[jax.ShapeDtypeStruct((n, d), F32), jax.ShapeDtypeStruct((n, d), BF16)],
        scratch_shapes=[pltpu.VMEM((2, tc, d), F32), pltpu.VMEM((2, tc, d), F32), pltpu.SemaphoreType.DMA((2,))],
        compiler_params=_cp(("arbitrary",)),
        name="moe_combine_ln",
    )(slot0, slot1, slot0, slot1, y, x2, gate_nt, lg, lb)


def _layer(x, xb, mem_b, p, moe_w, layer, consts, B, T):
    n, d = x.shape
    G, HPG, DH = NSA_GROUPS, NSA_HPG, NSA_DH
    slopes, ovt, epad, tile_ind = consts

    h_big = _matmul(xb, p["w_big"], BF16, 1024, 1024)
    h_small = _matmul(xb, p["w_small"], F32, 1024, SCOL_END)
    h_t = _matmul_t(xb, p["w_t"], B, T, 1024, TROW_END // 2)

    o_gla = _gla(h_big, h_small, p["wa_pad"], p["b_a"], p["norm_g"], B, T, 4 if B % 4 == 0 else 1, 512)

    kcmp, kcmp_t = _compress(h_small, p["cmp_w1bd"], p["cmp_w2bd"], p["cmp_w2bdt"], p["cmp_pe_pair"], B, T)
    tq_sel = 256
    ocmp_t, mb, in_tile = _cmp_select(slopes, h_t, kcmp, kcmp_t, ovt, tile_ind, B, T, 512)
    nq = T // tq_sel
    tile_flags = (in_tile.reshape(B, G, nq, nq, tq_sel).max(axis=-1) > 0).astype(I32)
    tile_flags = tile_flags.transpose(0, 1, 3, 2).reshape(-1)
    gates_t = h_small[:, GLA_GATE_RANK:GLA_GATE_RANK + 3 * NSA_HEADS].reshape(B, T, G, 3 * HPG)
    gates_t = jnp.pad(gates_t.transpose(0, 2, 3, 1), ((0, 0), (0, 0), (0, 16 - 3 * HPG), (0, 0)))
    o_nsa = _sel_win(slopes, tile_flags, h_t, h_big, epad, mb, ocmp_t, gates_t, B, T, tq_sel)

    x1, x1b = _mix(o_gla, o_nsa, h_big, x, p["w_bg"], p["w_bn"], p["w_out"], p["ln_mix_g"], p["ln_mix_b"], 512)

    kvm = _matmul(mem_b, p["xa_wkv"], BF16, 512, 512).reshape(B, MEM_LEN, 2 * XA_HEADS * XA_DH)
    x2 = _xattn(x1, x1b, kvm, p["xa_wq"], p["xa_wo"], p["ln_xa_g"], p["ln_xa_b"], B, T, 512)

    e, gate, rank, cnt = _router(x2, p["rw_hi"], p["rw_lo"], p["rb"], 512)
    counts = cnt[:, 0].astype(I32)
    padded = (counts + MOE_BLOCK - 1) // MOE_BLOCK * MOE_BLOCK
    pad_end = jnp.cumsum(padded)
    pad_start = (pad_end - padded).astype(I32)
    nb = (n * TOP_K) // MOE_BLOCK + N_EXPERTS
    n_used = (pad_end[-1] // MOE_BLOCK).astype(I32).reshape(1)
    blk_start = jnp.arange(nb, dtype=I32) * MOE_BLOCK
    blk_expert = jnp.minimum(jnp.sum(blk_start[:, None] >= pad_end[None, :], axis=1), N_EXPERTS - 1).astype(I32)
    blk_expert = jnp.where(jnp.arange(nb) < n_used[0], blk_expert, blk_expert[jnp.maximum(n_used[0] - 1, 0)])
    slot = _slots(pad_start, e, rank, 2048)
    buf = _dispatch(pad_end.astype(I32), slot[0], slot[1], x2, nb * MOE_BLOCK, 512)
    y = _experts(blk_expert, n_used, buf, moe_w[0], moe_w[1], layer)
    x3, x3b = _combine(slot[0], slot[1], y, x2, gate.T, p["ln_ffn_g"], p["ln_ffn_b"], 256)
    return x3, x3b


def _prep_layer(l, w_in, gla_w_a2, gla_b_a, gla_norm_g, nsa_cmp_pe, nsa_cmp_w1, nsa_cmp_w2, w_branch_gla,
                w_branch_nsa, w_out, ln_mix_g, ln_mix_b, xa_wq, xa_wkv, xa_wo, ln_xa_g, ln_xa_b, router_w,
                router_b, moe_w_in, moe_w_down, ln_ffn_g, ln_ffn_b):
    d = w_in.shape[1]
    w = w_in[l]
    o_gq, o_gk, o_gv, o_gr = 0, GLA_QK, 2 * GLA_QK, 2 * GLA_QK + GLA_V
    o_ga = o_gr + GLA_V
    o_nq = o_ga + GLA_GATE_RANK
    o_nkv = o_nq + NSA_Q
    o_ng = o_nkv + 6 * NSA_KV
    o_mg = o_ng + 3 * NSA_HEADS
    G, DH = NSA_GROUPS, NSA_DH

    def kv_cols(kind):
        return w[:, o_nkv + kind * NSA_KV:o_nkv + (kind + 1) * NSA_KV]

    def slabs(wk):
        return jnp.pad(wk.reshape(d, G, DH), ((0, 0), (0, 0), (0, LANES - DH))).reshape(d, G * LANES)

    w_big = jnp.concatenate([w[:, o_mg:o_mg + 2 * d], w[:, o_gq:o_ga], slabs(kv_cols(2)), slabs(kv_cols(4))],
                            axis=1).astype(BF16)
    w_small = jnp.concatenate([w[:, o_ga:o_nq], w[:, o_ng:o_mg],
                               jnp.zeros((d, LANES - GLA_GATE_RANK - 3 * NSA_HEADS), F32),
                               kv_cols(0), kv_cols(1)], axis=1).astype(BF16)
    w_t = jnp.concatenate([w[:, o_nq:o_nkv], kv_cols(3), kv_cols(5)], axis=1).T.astype(BF16)
    w1 = nsa_cmp_w1[l].reshape(2, CMP_LEN, DH, CMP_HIDDEN)
    z1 = jnp.zeros_like(w1)
    w1bd = jnp.concatenate([jnp.concatenate([w1, z1], axis=3), jnp.concatenate([z1, w1], axis=3)], axis=2)
    w2 = nsa_cmp_w2[l]
    z2 = jnp.zeros_like(w2)
    w2bd = jnp.concatenate([jnp.concatenate([w2, z2], axis=2), jnp.concatenate([z2, w2], axis=2)], axis=1)
    pe = nsa_cmp_pe[l]
    pe_pair = jnp.broadcast_to(jnp.concatenate([pe, pe], axis=-1)[:, :, None, :], (2, CMP_LEN, 16, 2 * DH))
    wa_pad = jnp.concatenate([gla_w_a2[l], jnp.zeros((LANES - GLA_GATE_RANK, GLA_QK), F32)], axis=0).astype(BF16)
    rw_t = router_w.T
    rw_hi = rw_t.astype(BF16)
    rw_lo = (rw_t - rw_hi.astype(F32)).astype(BF16)
    return dict(
        w_big=w_big, w_small=w_small, w_t=w_t, wa_pad=wa_pad,
        b_a=gla_b_a[l].reshape(1, -1), norm_g=gla_norm_g[l].reshape(1, -1),
        cmp_w1bd=w1bd.astype(BF16), cmp_w2bd=w2bd.astype(BF16), cmp_w2bdt=w2bd.transpose(0, 2, 1).astype(BF16),
        cmp_pe_pair=pe_pair.astype(BF16),
        w_bg=w_branch_gla[l].astype(BF16), w_bn=w_branch_nsa[l].astype(BF16), w_out=w_out[l].astype(BF16),
        ln_mix_g=ln_mix_g[l].reshape(1, -1), ln_mix_b=ln_mix_b[l].reshape(1, -1),
        xa_wq=xa_wq[l].astype(BF16), xa_wkv=xa_wkv[l].astype(BF16), xa_wo=xa_wo[l].astype(BF16),
        ln_xa_g=ln_xa_g[l].reshape(1, -1), ln_xa_b=ln_xa_b[l].reshape(1, -1),
        rw_hi=rw_hi, rw_lo=rw_lo, rb=router_b.reshape(-1, 1),
        ln_ffn_g=ln_ffn_g[l].reshape(1, -1), ln_ffn_b=ln_ffn_b[l].reshape(1, -1),
    )


def kernel(x, mem, w_in, gla_w_a2, gla_b_a, gla_norm_g, nsa_cmp_pe, nsa_cmp_w1, nsa_cmp_w2, w_branch_gla, w_branch_nsa, w_out, ln_mix_g, ln_mix_b, xa_wq, xa_wkv, xa_wo, ln_xa_g, ln_xa_b, router_w, router_b, moe_w_in, moe_w_down, ln_ffn_g, ln_ffn_b):
    B, T, d = x.shape
    assert T % 512 == 0 and d == 2048 and mem.shape[1] == MEM_LEN
    n = B * T
    params = (w_in, gla_w_a2, gla_b_a, gla_norm_g, nsa_cmp_pe, nsa_cmp_w1, nsa_cmp_w2, w_branch_gla, w_branch_nsa,
              w_out, ln_mix_g, ln_mix_b, xa_wq, xa_wkv, xa_wo, ln_xa_g, ln_xa_b, router_w, router_b, moe_w_in,
              moe_w_down, ln_ffn_g, ln_ffn_b)
    slopes = (2.0 ** (-8.0 * jnp.arange(1, NSA_HEADS + 1, dtype=F32) / NSA_HEADS)).astype(F32)
    nc, ns = T // CMP_STRIDE, T // SEL_LEN
    cs = np.arange(nc) * CMP_STRIDE
    ss = np.arange(ns) * SEL_LEN
    ovt = ((cs[None, :] < ss[:, None] + SEL_LEN) & (cs[None, :] + CMP_LEN > ss[:, None])
           & (cs[None, :] + CMP_LEN <= T)).astype(np.float32)
    assert NSA_DH + ns <= LANES
    epad = np.zeros((T, LANES), np.float32)
    epad[np.arange(T), NSA_DH + np.arange(T) // SEL_LEN] = 1.0
    tile_ind = (np.arange(ns)[None, :] // (256 // SEL_LEN) == np.arange(T // 256)[:, None]).astype(np.float32)
    consts = (slopes, jnp.asarray(ovt, BF16), jnp.asarray(epad, BF16), jnp.asarray(tile_ind, BF16))

    xf = x.reshape(n, d)
    xb = xf.astype(BF16)
    mem_b = mem.reshape(B * MEM_LEN, d).astype(BF16)
    moe_w = (moe_w_in, moe_w_down)
    for l in range(DEPTH):
        p = _prep_layer(l, *params)
        xf, xb = _layer(xf, xb, mem_b, p, moe_w, l, consts, B, T)
    return xf.reshape(B, T, d)
```

```python
import functools

import jax
import jax.numpy as jnp
import numpy as np
from jax import lax
from jax.experimental import pallas as pl
from jax.experimental.pallas import tpu as pltpu

F32 = jnp.float32
BF16 = jnp.bfloat16
I32 = jnp.int32

DEPTH = 2
MEM_LEN = 256
GLA_HEADS = 4
GLA_DK = 128
GLA_DV = 256
GLA_GATE_RANK = 16
GLA_TAU = 16.0
GLA_CHUNK = 64
NSA_HEADS = 16
NSA_GROUPS = 4
NSA_HPG = NSA_HEADS // NSA_GROUPS
NSA_DH = 64
CMP_LEN = 32
CMP_STRIDE = 16
CMP_HIDDEN = 256
SEL_LEN = 64
SEL_TOPN = 8
WINDOW = 512
XA_HEADS = 4
XA_DH = 128
N_EXPERTS = 16
N_GROUPS = 4
EXPERTS_PER_GROUP = N_EXPERTS // N_GROUPS
TOP_K = 2
D_FF = 1536
DN_ALPHA = float((2 * DEPTH) ** 0.25)
LN_EPS = 1e-5
NEG = -1e30
LOG2E = 1.4426950408889634
FORCE_BONUS = 1e6

GLA_QK = GLA_HEADS * GLA_DK
GLA_V = GLA_HEADS * GLA_DV
NSA_Q = NSA_HEADS * NSA_DH
NSA_KV = NSA_GROUPS * NSA_DH

LANES = 128
VMEM_LIMIT = 56 * 1024 * 1024

COL_MG = 0
COL_GQ = 2 * 2048
COL_GK = COL_GQ + GLA_QK
COL_GV = COL_GK + GLA_QK
COL_GR = COL_GV + GLA_V
COL_KS = COL_GR + GLA_V
COL_KW = COL_KS + NSA_GROUPS * LANES
COL_END = COL_KW + NSA_GROUPS * LANES
SCOL_CK = LANES
SCOL_CV = SCOL_CK + NSA_KV
SCOL_END = SCOL_CV + NSA_KV
TROW_Q = 0
TROW_VS = NSA_Q
TROW_VW = TROW_VS + NSA_KV
TROW_END = TROW_VW + NSA_KV

MOE_BLOCK = 512
FF_TILE = 512


def _cp(sem):
    return pltpu.CompilerParams(dimension_semantics=sem, vmem_limit_bytes=VMEM_LIMIT)


def _dot(a, b):
    return jnp.dot(a, b, preferred_element_type=F32)


def _dot_nt(a, b):
    return lax.dot_general(a, b, (((1,), (1,)), ((), ())), preferred_element_type=F32)


def _dot_tn(a, b):
    return lax.dot_general(a, b, (((0,), (0,)), ((), ())), preferred_element_type=F32)


def _layer_norm(z, g, b):
    mu = jnp.mean(z, axis=-1, keepdims=True)
    zc = z - mu
    var = jnp.mean(zc * zc, axis=-1, keepdims=True)
    return zc * lax.rsqrt(var + LN_EPS) * g + b


def _mm_kernel(a_ref, b_ref, o_ref):
    o_ref[...] = _dot(a_ref[...], b_ref[...]).astype(o_ref.dtype)


def _matmul(a, b, out_dtype, tm, tn):
    m, k = a.shape
    n = b.shape[1]
    return pl.pallas_call(
        _mm_kernel,
        grid=(m // tm, n // tn),
        in_specs=[pl.BlockSpec((tm, k), lambda i, j: (i, 0)),
                  pl.BlockSpec((k, tn), lambda i, j: (0, j))],
        out_specs=pl.BlockSpec((tm, tn), lambda i, j: (i, j)),
        out_shape=jax.ShapeDtypeStruct((m, n), out_dtype),
        compiler_params=_cp(("parallel", "parallel")),
        name="matmul",
    )(a, b)


def _mm_nt_kernel(wt_ref, x_ref, o_ref):
    o_ref[0] = _dot_nt(wt_ref[...], x_ref[...]).astype(o_ref.dtype)


def _matmul_t(x, wt, B, T, tm, tr):
    n, k = x.shape
    r = wt.shape[0]
    nt = T // tm
    return pl.pallas_call(
        _mm_nt_kernel,
        grid=(n // tm, r // tr),
        in_specs=[pl.BlockSpec((tr, k), lambda i, j: (j, 0)),
                  pl.BlockSpec((tm, k), lambda i, j: (i, 0))],
        out_specs=pl.BlockSpec((1, tr, tm), lambda i, j: (i // nt, j, i % nt)),
        out_shape=jax.ShapeDtypeStruct((B, r, T), BF16),
        compiler_params=_cp(("parallel", "parallel")),
        name="matmul_t",
    )(wt, x)


def _gla_kernel(q_ref, k_ref, v_ref, r_ref, sm_ref, wa_ref, ba_ref, ng_ref, o_ref, st_ref):
    C = GLA_CHUNK
    seqs = q_ref.shape[1]
    n_chunks = q_ref.shape[2] // C

    @pl.when(pl.program_id(1) == 0)
    def _():
        st_ref[...] = jnp.zeros_like(st_ref)

    rowi = lax.broadcasted_iota(I32, (C, GLA_DK), 0)
    tt = lax.broadcasted_iota(I32, (C, C), 0)
    ss = lax.broadcasted_iota(I32, (C, C), 1)
    levels = (1, 2, 4, 8, 16, 32)
    pair_masks = [((tt // (2 * L)) == (ss // (2 * L))) & ((tt & L) != 0) & ((ss & L) == 0) for L in levels]
    diag_mask = tt == ss
    scale = GLA_DK ** -0.5

    def head_chunk(sq, rows, h, z):
        qk_cols = slice(h * GLA_DK, (h + 1) * GLA_DK)
        v_cols = slice(h * GLA_DV, (h + 1) * GLA_DV)
        state = sq * GLA_HEADS + h
        q = q_ref[0, sq, rows, qk_cols].astype(F32) * scale
        k = k_ref[0, sq, rows, qk_cols].astype(F32)
        v = v_ref[0, sq, rows, v_cols]
        g = (jnp.minimum(z, 0.0) - jnp.log1p(jnp.exp(-jnp.abs(z)))) * (1.0 / GLA_TAU)
        incl = g
        tot = g
        att = jnp.where(diag_mask, _dot_nt(q.astype(BF16), k.astype(BF16)), 0.0)
        for L, pm in zip(levels, pair_masks):
            ql = (q * jnp.exp(incl)).astype(BF16)
            kl = (k * jnp.exp(tot - incl)).astype(BF16)
            att = jnp.where(pm, _dot_nt(ql, kl), att)
            upper = (rowi & L) != 0
            from_lower = pltpu.roll(tot, L, 0)
            from_upper = pltpu.roll(tot, C - L, 0)
            incl = incl + jnp.where(upper, from_lower, 0.0)
            tot = tot + jnp.where(upper, from_lower, from_upper)
        qd = (q * jnp.exp(incl)).astype(BF16)
        kd = (k * jnp.exp(tot - incl)).astype(BF16)
        st = st_ref[state]
        o = _dot_nt(qd, st.astype(BF16)) + _dot(att.astype(BF16), v)
        st_ref[state] = st * jnp.exp(tot[0:1, :]) + _dot_tn(v, kd)
        mu = jnp.mean(o, axis=-1, keepdims=True)
        oc = o - mu
        var = jnp.mean(oc * oc, axis=-1, keepdims=True)
        on = oc * lax.rsqrt(var + LN_EPS) * ng_ref[:, v_cols]
        r = r_ref[0, sq, rows, v_cols].astype(F32)
        o_ref[0, sq, rows, v_cols] = (on * (r * jax.nn.sigmoid(r))).astype(o_ref.dtype)

    def chunk(c, carry):
        rows = pl.ds(pl.multiple_of(c * C, C), C)
        for sq in range(seqs):
            z = _dot(sm_ref[0, sq, rows, :].astype(BF16), wa_ref[...]) + ba_ref[...]
            for h in range(GLA_HEADS):
                head_chunk(sq, rows, h, z[:, h * GLA_DK:(h + 1) * GLA_DK])
        return carry

    lax.fori_loop(0, n_chunks, chunk, 0)


def _gla(h_big, h_small, wa_pad, b_a, norm_g, B, T, seqs, tb):
    groups = B // seqs
    hb = h_big.reshape(groups, seqs, T, h_big.shape[1])
    hs = h_small.reshape(groups, seqs, T, h_small.shape[1])
    out = pl.pallas_call(
        _gla_kernel,
        grid=(groups, T // tb),
        in_specs=[
            pl.BlockSpec((1, seqs, tb, GLA_QK), lambda b, j: (b, 0, j, COL_GQ // GLA_QK)),
            pl.BlockSpec((1, seqs, tb, GLA_QK), lambda b, j: (b, 0, j, COL_GK // GLA_QK)),
            pl.BlockSpec((1, seqs, tb, GLA_V), lambda b, j: (b, 0, j, COL_GV // GLA_V)),
            pl.BlockSpec((1, seqs, tb, GLA_V), lambda b, j: (b, 0, j, COL_GR // GLA_V)),
            pl.BlockSpec((1, seqs, tb, LANES), lambda b, j: (b, 0, j, 0)),
            pl.BlockSpec((LANES, GLA_QK), lambda b, j: (0, 0)),
            pl.BlockSpec((1, GLA_QK), lambda b, j: (0, 0)),
            pl.BlockSpec((1, GLA_V), lambda b, j: (0, 0)),
        ],
        out_specs=pl.BlockSpec((1, seqs, tb, GLA_V), lambda b, j: (b, 0, j, 0)),
        out_shape=jax.ShapeDtypeStruct((groups, seqs, T, GLA_V), BF16),
        scratch_shapes=[pltpu.VMEM((seqs * GLA_HEADS, GLA_DV, GLA_DK), F32)],
        compiler_params=_cp(("parallel", "arbitrary")),
        name="gla",
    )(hb, hb, hb, hb, hs, wa_pad, b_a, norm_g)
    return out.reshape(B * T, GLA_V)


def _compress_kernel(x_ref, w1_ref, w2_ref, w2t_ref, pe_ref, o_ref, ot_ref):
    nc = x_ref.shape[0] // CMP_STRIDE
    hid_w = w1_ref.shape[3]
    a = jnp.zeros((nc, hid_w), F32)
    bm = jnp.zeros((nc, hid_w), F32)
    c = jnp.zeros((pe_ref.shape[2], hid_w), F32)
    for l in range(CMP_STRIDE):
        xl = x_ref[pl.ds(l, nc, stride=CMP_STRIDE), :].astype(BF16)
        a = a + _dot(xl, w1_ref[0, l])
        bm = bm + _dot(xl, w1_ref[0, CMP_STRIDE + l])
    for l in range(CMP_LEN):
        c = c + _dot(pe_ref[0, l], w1_ref[0, l])
    hid = a + pltpu.roll(bm, nc - 1, 0) + c[0:1, :]
    act = jax.nn.gelu(hid).astype(BF16)
    o_ref[0, 0, 0] = _dot(act, w2_ref[0]).astype(o_ref.dtype)
    ot_ref[0, 0, 0] = _dot_nt(w2t_ref[0], act).astype(ot_ref.dtype)


def _compress(h_small, w1bd, w2bd, w2bdt, pe_pair, B, T):
    nc = T // CMP_STRIDE
    pairs = NSA_GROUPS // 2
    return pl.pallas_call(
        _compress_kernel,
        grid=(B, 2, pairs),
        in_specs=[
            pl.BlockSpec((T, LANES), lambda b, s, j: (b, SCOL_CK // LANES + s * pairs + j)),
            pl.BlockSpec((1,) + w1bd.shape[1:], lambda b, s, j: (s, 0, 0, 0)),
            pl.BlockSpec((1,) + w2bd.shape[1:], lambda b, s, j: (s, 0, 0)),
            pl.BlockSpec((1,) + w2bdt.shape[1:], lambda b, s, j: (s, 0, 0)),
            pl.BlockSpec((1,) + pe_pair.shape[1:], lambda b, s, j: (s, 0, 0, 0)),
        ],
        out_specs=[
            pl.BlockSpec((1, 1, 1, nc, LANES), lambda b, s, j: (b, s, j, 0, 0)),
            pl.BlockSpec((1, 1, 1, LANES, nc), lambda b, s, j: (b, s, j, 0, 0)),
        ],
        out_shape=[jax.ShapeDtypeStruct((B, 2, pairs, nc, LANES), BF16),
                   jax.ShapeDtypeStruct((B, 2, pairs, LANES, nc), BF16)],
        compiler_params=_cp(("parallel", "parallel", "parallel")),
        name="nsa_compress",
    )(h_small, w1bd, w2bd, w2bdt, pe_pair)


def _cmp_select_kernel(slopes_ref, qt_ref, kc_ref, vct_ref, ovt_ref, ind_ref, ocmp_ref, mb_ref, kt_ref, qpad_ref):
    g = pl.program_id(1)
    i = pl.program_id(2)
    tq = qt_ref.shape[2]
    nc = kc_ref.shape[3]
    ns = mb_ref.shape[2]
    dh = NSA_DH
    t0 = i * tq
    wide = NSA_HPG * tq
    tpos = t0 + (lax.broadcasted_iota(I32, (nc, wide), 1) & (tq - 1))
    nidx = lax.broadcasted_iota(I32, (nc, wide), 0)
    mask_c = (nidx * CMP_STRIDE + (CMP_LEN - 1)) <= tpos
    absd = jnp.abs(tpos.astype(F32) - (nidx.astype(F32) * CMP_STRIDE + 0.5 * (CMP_LEN - 1)))
    srow = jnp.concatenate([jnp.full((1, tq), slopes_ref[g * NSA_HPG + hh], F32) for hh in range(NSA_HPG)], axis=1)
    lower = g % 2 == 0
    kc = kc_ref[0, 0, 0]
    vct = jnp.where(lower, vct_ref[0, 0, 0, 0:dh, :], vct_ref[0, 0, 0, dh:2 * dh, :])
    for hh in range(NSA_HPG):
        q = qt_ref[0, hh * dh:(hh + 1) * dh, :] * jnp.asarray(dh ** -0.5, BF16)
        zero = jnp.zeros_like(q)
        qpad_ref[0:dh, hh * tq:(hh + 1) * tq] = jnp.where(lower, q, zero)
        qpad_ref[dh:2 * dh, hh * tq:(hh + 1) * tq] = jnp.where(lower, zero, q)
    s = _dot(kc, qpad_ref[...]) - srow * absd
    s = jnp.where(mask_c, s, NEG)
    e = jnp.exp(s - jnp.max(s, axis=0, keepdims=True))
    p = jnp.where(mask_c, e * (1.0 / jnp.sum(e, axis=0, keepdims=True)), 0.0)
    o = _dot(vct, p.astype(BF16))
    psum = jnp.zeros((nc, tq), F32)
    for hh in range(NSA_HPG):
        ocmp_ref[0, hh * dh:(hh + 1) * dh, :] = o[:, hh * tq:(hh + 1) * tq].astype(ocmp_ref.dtype)
        psum = psum + p[:, hh * tq:(hh + 1) * tq]
    p_hi = psum.astype(BF16)
    p_lo = (psum - p_hi.astype(F32)).astype(BF16)
    imp = _dot(ovt_ref[...], p_hi) + _dot(ovt_ref[...], p_lo)
    j = lax.broadcasted_iota(I32, (ns, tq), 0)
    tp = t0 + lax.broadcasted_iota(I32, (ns, tq), 1)
    cur = tp // SEL_LEN
    forced = (j == 0) | (j == cur) | (j == cur - 1)
    valid = j * SEL_LEN <= tp
    score = jnp.where(valid, imp + jnp.where(forced, FORCE_BONUS, 0.0), NEG)
    rank = jnp.zeros((ns, tq), F32)
    for jp in range(ns):
        row = score[jp:jp + 1, :]
        beats = (row > score) | ((row == score) & (j > jp))
        rank = rank + jnp.where(beats, 1.0, 0.0)
    keep = valid & (rank < float(min(SEL_TOPN, ns)))
    mb_ref[0, 0] = jnp.where(keep, 0.0, NEG).astype(mb_ref.dtype)
    kt_ref[0, 0] = _dot(ind_ref[...], jnp.where(keep, 1.0, 0.0).astype(BF16))


def _cmp_select(slopes, h_t, kcmp, kcmp_t, ovt, tile_ind, B, T, tq):
    nc = T // CMP_STRIDE
    ns = T // SEL_LEN
    nkt = tile_ind.shape[0]
    grp_rows = NSA_HPG * NSA_DH
    grid_spec = pltpu.PrefetchScalarGridSpec(
        num_scalar_prefetch=1,
        grid=(B, NSA_GROUPS, T // tq),
        in_specs=[
            pl.BlockSpec((1, grp_rows, tq), lambda b, g, i, s: (b, TROW_Q // grp_rows + g, i)),
            pl.BlockSpec((1, 1, 1, nc, LANES), lambda b, g, i, s: (b, 0, g // 2, 0, 0)),
            pl.BlockSpec((1, 1, 1, LANES, nc), lambda b, g, i, s: (b, 1, g // 2, 0, 0)),
            pl.BlockSpec((ns, nc), lambda b, g, i, s: (0, 0)),
            pl.BlockSpec((nkt, ns), lambda b, g, i, s: (0, 0)),
        ],
        out_specs=[
            pl.BlockSpec((1, grp_rows, tq), lambda b, g, i, s: (b, g, i)),
            pl.BlockSpec((1, 1, ns, tq), lambda b, g, i, s: (b, g, 0, i)),
            pl.BlockSpec((1, 1, nkt, tq), lambda b, g, i, s: (b, g, 0, i)),
        ],
        scratch_shapes=[pltpu.VMEM((LANES, NSA_HPG * tq), BF16)],
    )
    return pl.pallas_call(
        _cmp_select_kernel,
        grid_spec=grid_spec,
        out_shape=[jax.ShapeDtypeStruct((B, NSA_Q, T), BF16),
                   jax.ShapeDtypeStruct((B, NSA_GROUPS, ns, T), BF16),
                   jax.ShapeDtypeStruct((B, NSA_GROUPS, nkt, T), F32)],
        compiler_params=_cp(("parallel", "parallel", "parallel")),
        name="nsa_cmp_select",
    )(slopes, h_t, kcmp, kcmp_t, ovt, tile_ind)


def _sel_win_kernel(slopes_ref, flags_ref, qt_ref, ks_ref, kw_ref, vs_ref, vw_ref, epad_ref, mb_ref, ocmp_ref, gt_ref,
                    o_ref, qaug_ref, m_ref, acc_ref, srow_ref, bias_ref, s_ref, p_ref, alpha_ref,
                    kall_ref, vall_ref, tiles_ref, *, n_tiles):
    g = pl.program_id(1)
    i = pl.program_id(2)
    streams = qt_ref.shape[1]
    tq = qt_ref.shape[3]
    tk = tq
    ns = mb_ref.shape[3]
    dh = NSA_DH
    wide = NSA_HPG * tq
    BIG = -NEG

    @pl.when(i == 0)
    def _():
        srow = jnp.concatenate([jnp.full((1, tq), slopes_ref[g * NSA_HPG + hh] * LOG2E, F32)
                                for hh in range(NSA_HPG)], axis=1)
        srow_ref[...] = srow
        lane = lax.broadcasted_iota(I32, (tk, wide), 1) & (tq - 1)
        dist0 = (lane - lax.broadcasted_iota(I32, (tk, wide), 0)).astype(F32)
        sd0 = srow * dist0
        bias_ref[0] = sd0
        bias_ref[1] = sd0 + jnp.where(dist0 >= 0.0, 0.0, BIG)
        bias_ref[2] = sd0 + jnp.where(dist0 < 0.0, 0.0, BIG)
        bias_ref[3] = jnp.full((tk, wide), BIG, F32)
        extra = jnp.where(lax.broadcasted_iota(I32, (vall_ref.shape[2] - dh, vall_ref.shape[3]), 0) == 0, 1.0, 0.0)
        for u in range(streams):
            kall_ref[u, 0] = ks_ref[0, u] + epad_ref[...]
            kall_ref[u, 1] = kw_ref[0, u]
            vall_ref[u, 0, 0:dh, :] = vs_ref[0, u]
            vall_ref[u, 1, 0:dh, :] = vw_ref[0, u]
            vall_ref[u, 0, dh:, :] = extra.astype(BF16)
            vall_ref[u, 1, dh:, :] = extra.astype(BF16)

    for u in range(streams):
        for hh in range(NSA_HPG):
            cols = slice(hh * tq, (hh + 1) * tq)
            q = qt_ref[0, u, hh * dh:(hh + 1) * dh, :].astype(F32) * (dh ** -0.5 * LOG2E)
            qaug_ref[u, 0:dh, cols] = q.astype(BF16)
            qaug_ref[u, dh:dh + ns, cols] = mb_ref[0, u, 0]
            qaug_ref[u, dh + ns:, cols] = jnp.zeros((qaug_ref.shape[1] - dh - ns, tq), BF16)

    m_ref[...] = jnp.full(m_ref.shape, NEG, F32)
    acc_ref[...] = jnp.zeros(acc_ref.shape, F32)

    n_back = WINDOW // tk
    n_win = jnp.minimum(i, n_back) + 1
    n_sel = []
    for u in range(streams):
        flag_base = (((pl.program_id(0) * streams + u) * NSA_GROUPS + g) * n_tiles + i) * n_tiles
        cnt = jnp.int32(0)
        for kb_static in range(n_tiles - 1):
            active = (kb_static < i) & (flags_ref[flag_base + kb_static] != 0)
            tiles_ref[u, cnt] = kb_static
            cnt = cnt + active.astype(I32)
        tiles_ref[u, cnt] = i
        n_sel.append(cnt + 1)
    n_steps = n_sel[0] + n_win
    for u in range(1, streams):
        n_steps = jnp.maximum(n_steps, n_sel[u] + n_win)

    def describe(u, n):
        n = jnp.maximum(n, 0)
        is_win = n >= n_sel[u]
        kb_sel = tiles_ref[u, jnp.minimum(n, n_sel[u] - 1)]
        kb = jnp.clip(jnp.where(is_win, i - n_win + 1 + (n - n_sel[u]), kb_sel), 0, i)
        mode = jnp.where(kb == i, 1, jnp.where(is_win & (kb == i - n_back), 2, 0))
        mode = jnp.where(n >= n_sel[u] + n_win, 3, mode)
        return is_win.astype(I32), kb, mode

    def scores(u, n, slot):
        br, kb, _ = describe(u, n)
        s0 = pl.multiple_of(kb * tk, tk)
        s_ref[u, slot] = _dot(kall_ref[u, br, pl.ds(s0, tk), :], qaug_ref[u])

    def softmax(u, n, slot):
        br, kb, mode = describe(u, n)
        crow = srow_ref[...] * ((i - kb) * tk).astype(F32)
        s = s_ref[u, slot] - bias_ref[mode]
        m_old = m_ref[u, br]
        m_new = jnp.maximum(m_old, jnp.max(s, axis=0, keepdims=True) - crow)
        alpha = jnp.exp2(m_old - m_new)
        p = jnp.exp2(s - (m_new + crow))
        m_ref[u, br] = m_new
        alpha_ref[u, slot] = alpha
        p_ref[u, slot] = p.astype(BF16)

    def weighted_values(u, n, slot):
        br, kb, _ = describe(u, n)
        s0 = pl.multiple_of(kb * tk, tk)
        acc_ref[u, br] = (alpha_ref[u, slot] * acc_ref[u, br]
                          + _dot(vall_ref[u, br, :, pl.ds(s0, tk)], p_ref[u, slot]))

    def each(fn, n, slot):
        for u in range(streams):
            fn(u, n, slot)

    for u in range(streams):
        p_ref[u, 1] = jnp.zeros(p_ref.shape[2:], BF16)
        alpha_ref[u, 1] = jnp.ones(alpha_ref.shape[2:], F32)
    each(scores, 0, 0)

    def pair(j, carry):
        n = 2 * j
        each(scores, n + 1, 1)
        each(softmax, n, 0)
        each(weighted_values, n - 1, 1)
        each(scores, n + 2, 0)
        each(softmax, n + 1, 1)
        each(weighted_values, n, 0)
        return carry

    n_pairs = n_steps // 2
    lax.fori_loop(0, n_pairs, pair, 0)
    each(weighted_values, 2 * n_pairs - 1, 1)

    @pl.when(n_steps % 2 == 1)
    def _():
        each(softmax, n_steps - 1, 0)
        each(weighted_values, n_steps - 1, 0)

    for u in range(streams):
        def gate_row(branch, u=u):
            rows = [gt_ref[0, u, 0, 3 * hh + branch:3 * hh + branch + 1, :] for hh in range(NSA_HPG)]
            return jax.nn.sigmoid(jnp.concatenate(rows, axis=1))

        o = (acc_ref[u, 0, 0:dh, :] * (gate_row(1) / acc_ref[u, 0, dh:dh + 1, :])
             + acc_ref[u, 1, 0:dh, :] * (gate_row(2) / acc_ref[u, 1, dh:dh + 1, :]))
        ocmp = jnp.concatenate([ocmp_ref[0, u, hh * dh:(hh + 1) * dh, :] for hh in range(NSA_HPG)],
                               axis=1).astype(F32)
        o = o + gate_row(0) * ocmp
        o_heads = jnp.concatenate([o[:, hh * tq:(hh + 1) * tq] for hh in range(NSA_HPG)], axis=0)
        o_ref[0, u] = o_heads.T.astype(o_ref.dtype)


def _sel_win(slopes, tile_flags, h_t, h_big, epad, mb, ocmp_t, gates_t, B, T, tq, streams):
    ns = T // SEL_LEN
    kaug = LANES
    G = NSA_GROUPS
    grp_rows = NSA_HPG * NSA_DH
    nq = T // tq
    nbg = B // streams
    wide = NSA_HPG * tq
    h_t5 = h_t.reshape(nbg, streams, h_t.shape[1], T)
    h_big5 = h_big.reshape(nbg, streams, T, h_big.shape[1])
    mb5 = mb.reshape(nbg, streams, G, ns, T)
    ocmp5 = ocmp_t.reshape(nbg, streams, NSA_Q, T)
    gates5 = gates_t.reshape(nbg, streams, G, 16, T)
    grid_spec = pltpu.PrefetchScalarGridSpec(
        num_scalar_prefetch=2,
        grid=(nbg, G, nq),
        in_specs=[
            pl.BlockSpec((1, streams, grp_rows, tq), lambda b, g, i, s, f: (b, 0, TROW_Q // grp_rows + g, i)),
            pl.BlockSpec((1, streams, T, LANES), lambda b, g, i, s, f: (b, 0, 0, COL_KS // LANES + g)),
            pl.BlockSpec((1, streams, T, LANES), lambda b, g, i, s, f: (b, 0, 0, COL_KW // LANES + g)),
            pl.BlockSpec((1, streams, NSA_DH, T), lambda b, g, i, s, f: (b, 0, TROW_VS // NSA_DH + g, 0)),
            pl.BlockSpec((1, streams, NSA_DH, T), lambda b, g, i, s, f: (b, 0, TROW_VW // NSA_DH + g, 0)),
            pl.BlockSpec((T, LANES), lambda b, g, i, s, f: (0, 0)),
            pl.BlockSpec((1, streams, 1, ns, tq), lambda b, g, i, s, f: (b, 0, g, 0, i)),
            pl.BlockSpec((1, streams, grp_rows, tq), lambda b, g, i, s, f: (b, 0, g, i)),
            pl.BlockSpec((1, streams, 1, 16, tq), lambda b, g, i, s, f: (b, 0, g, 0, i)),
        ],
        out_specs=pl.BlockSpec((1, streams, tq, grp_rows), lambda b, g, i, s, f: (b, 0, i, g)),
        scratch_shapes=[
            pltpu.VMEM((streams, kaug, wide), BF16),
            pltpu.VMEM((streams, 2, 1, wide), F32),
            pltpu.VMEM((streams, 2, NSA_DH + 16, wide), F32),
            pltpu.VMEM((1, wide), F32),
            pltpu.VMEM((4, tq, wide), F32),
            pltpu.VMEM((streams, 2, tq, wide), F32),
            pltpu.VMEM((streams, 2, tq, wide), BF16),
            pltpu.VMEM((streams, 2, 1, wide), F32),
            pltpu.VMEM((streams, 2, T, kaug), BF16),
            pltpu.VMEM((streams, 2, NSA_DH + 16, T), BF16),
            pltpu.SMEM((streams, nq), I32),
        ],
    )
    out = pl.pallas_call(
        functools.partial(_sel_win_kernel, n_tiles=nq),
        grid_spec=grid_spec,
        out_shape=jax.ShapeDtypeStruct((nbg, streams, T, NSA_Q), BF16),
        compiler_params=_cp(("parallel", "parallel", "arbitrary")),
        name="nsa_sel_win",
    )(slopes, tile_flags, h_t5, h_big5, h_big5, h_t5, h_t5, epad, mb5, ocmp5, gates5)
    return out.reshape(B * T, NSA_Q)


def _mix_kernel(og_ref, on_ref, mg1_ref, mg2_ref, x_ref, wg_ref, wn_ref, wo_ref, lg_ref, lb_ref,
                x1_ref, x1b_ref):
    g1 = _dot(og_ref[...], wg_ref[...])
    g2 = _dot(on_ref[...], wn_ref[...])
    merged = (jax.nn.sigmoid(mg1_ref[...].astype(F32)) * g1
              + jax.nn.sigmoid(mg2_ref[...].astype(F32)) * g2)
    y = _dot(merged.astype(BF16), wo_ref[...])
    x1 = _layer_norm(DN_ALPHA * x_ref[...] + y, lg_ref[...], lb_ref[...])
    x1_ref[...] = x1
    x1b_ref[...] = x1.astype(BF16)


def _const_spec(shape):
    nd = len(shape)
    return pl.BlockSpec(shape, lambda *_: (0,) * nd, pipeline_mode=pl.Buffered(1))


def _mix(o_gla, o_nsa, h_big, x, wg, wn, wo, lg, lb, tm):
    n, d = x.shape
    return pl.pallas_call(
        _mix_kernel,
        grid=(n // tm,),
        in_specs=[
            pl.BlockSpec((tm, GLA_V), lambda i: (i, 0)),
            pl.BlockSpec((tm, NSA_Q), lambda i: (i, 0)),
            pl.BlockSpec((tm, d), lambda i: (i, 0)),
            pl.BlockSpec((tm, d), lambda i: (i, 1)),
            pl.BlockSpec((tm, d), lambda i: (i, 0)),
            _const_spec(wg.shape), _const_spec(wn.shape), _const_spec(wo.shape),
            _const_spec(lg.shape), _const_spec(lb.shape),
        ],
        out_specs=[pl.BlockSpec((tm, d), lambda i: (i, 0)), pl.BlockSpec((tm, d), lambda i: (i, 0))],
        out_shape=[jax.ShapeDtypeStruct((n, d), F32), jax.ShapeDtypeStruct((n, d), BF16)],
        compiler_params=_cp(("parallel",)),
        name="mix_ln",
    )(o_gla, o_nsa, h_big, h_big, x, wg, wn, wo, lg, lb)


def _xattn_kernel(x_ref, xb_ref, kv_ref, wq_ref, wo_ref, lg_ref, lb_ref, x2_ref):
    hd = XA_HEADS * XA_DH
    q = (_dot(xb_ref[...], wq_ref[...]) * (XA_DH ** -0.5)).astype(BF16)
    outs = []
    for h in range(XA_HEADS):
        kh = kv_ref[0, :, h * XA_DH:(h + 1) * XA_DH]
        vh = kv_ref[0, :, hd + h * XA_DH:hd + (h + 1) * XA_DH]
        s = _dot_nt(q[:, h * XA_DH:(h + 1) * XA_DH], kh)
        e = jnp.exp(s - jnp.max(s, axis=-1, keepdims=True))
        p = e / jnp.sum(e, axis=-1, keepdims=True)
        outs.append(_dot(p.astype(BF16), vh).astype(BF16))
    o = jnp.concatenate(outs, axis=-1)
    y = _dot(o, wo_ref[...])
    x2_ref[...] = _layer_norm(DN_ALPHA * x_ref[...] + y, lg_ref[...], lb_ref[...])


def _xattn(x1, x1b, kv, wq, wo, lg, lb, B, T, tm):
    n, d = x1.shape
    nt = T // tm
    return pl.pallas_call(
        _xattn_kernel,
        grid=(B, nt),
        in_specs=[
            pl.BlockSpec((tm, d), lambda b, i: (b * nt + i, 0)),
            pl.BlockSpec((tm, d), lambda b, i: (b * nt + i, 0)),
            pl.BlockSpec((1,) + kv.shape[1:], lambda b, i: (b, 0, 0)),
            _const_spec(wq.shape), _const_spec(wo.shape), _const_spec(lg.shape), _const_spec(lb.shape),
        ],
        out_specs=pl.BlockSpec((tm, d), lambda b, i: (b * nt + i, 0)),
        out_shape=jax.ShapeDtypeStruct((n, d), F32),
        compiler_params=_cp(("parallel", "parallel")),
        name="xattn_ln",
    )(x1, x1b, kv, wq, wo, lg, lb)


def _router_kernel(x_ref, wh_ref, wl_ref, rb_ref, e_ref, gate_ref, rank_ref, cnt_ref, carry_ref):
    i = pl.program_id(0)
    tr = x_ref.shape[0]
    E = N_EXPERTS

    @pl.when(i == 0)
    def _():
        carry_ref[...] = jnp.zeros_like(carry_ref)

    x = x_ref[...]
    x_hi = x.astype(BF16)
    x_lo = (x - x_hi.astype(F32)).astype(BF16)
    wh = wh_ref[...]
    logits = _dot_nt(wh, x_hi) + _dot_nt(wh, x_lo) + _dot_nt(wl_ref[...], x_hi)
    biased = logits + rb_ref[...]
    rows = [biased[e:e + 1, :] for e in range(E)]
    raw = [logits[e:e + 1, :] for e in range(E)]
    best_score = None
    best = None
    for gi in range(N_GROUPS):
        v = rows[gi * EXPERTS_PER_GROUP:(gi + 1) * EXPERTS_PER_GROUP]
        sc = None
        for a in range(EXPERTS_PER_GROUP):
            for b in range(a + 1, EXPERTS_PER_GROUP):
                pair = v[a] + v[b]
                sc = pair if sc is None else jnp.maximum(sc, pair)
        if best is None:
            best_score, best = sc, jnp.zeros((1, tr), I32)
        else:
            better = sc > best_score
            best_score = jnp.where(better, sc, best_score)
            best = jnp.where(better, gi, best)

    def pick(vals):
        out = vals[0:EXPERTS_PER_GROUP]
        for gi in range(1, N_GROUPS):
            out = [jnp.where(best == gi, vals[gi * EXPERTS_PER_GROUP + a], out[a]) for a in range(EXPERTS_PER_GROUP)]
        return out

    w = pick(rows)
    lraw = pick(raw)
    i1 = jnp.zeros((1, tr), I32)
    v1 = w[0]
    l1 = lraw[0]
    for a in range(1, EXPERTS_PER_GROUP):
        better = w[a] > v1
        v1 = jnp.where(better, w[a], v1)
        l1 = jnp.where(better, lraw[a], l1)
        i1 = jnp.where(better, a, i1)
    i2 = jnp.full((1, tr), -1, I32)
    v2 = jnp.full((1, tr), -jnp.inf, F32)
    l2 = jnp.zeros((1, tr), F32)
    for a in range(EXPERTS_PER_GROUP):
        better = (i1 != a) & ((w[a] > v2) | (i2 < 0))
        v2 = jnp.where(better, w[a], v2)
        l2 = jnp.where(better, lraw[a], l2)
        i2 = jnp.where(better, a, i2)
    e1 = best * EXPERTS_PER_GROUP + i1
    e2 = best * EXPERTS_PER_GROUP + i2
    mx = jnp.maximum(l1, l2)
    p1 = jnp.exp(l1 - mx)
    p2 = jnp.exp(l2 - mx)
    den = p1 + p2
    e_ref[0:1, :] = e1
    e_ref[1:2, :] = e2
    gate_ref[0:1, :] = p1 / den
    gate_ref[1:2, :] = p2 / den
    eidx = lax.broadcasted_iota(I32, (E, tr), 0)
    is1 = eidx == e1
    is2 = eidx == e2
    member = jnp.where(is1 | is2, 1.0, 0.0)
    uu = lax.broadcasted_iota(I32, (tr, tr), 0)
    tt = lax.broadcasted_iota(I32, (tr, tr), 1)
    tri = jnp.where(uu <= tt, 1.0, 0.0).astype(BF16)
    incl = _dot(member.astype(BF16), tri)
    excl = carry_ref[:, 0:1] + incl - member
    rank_ref[0:1, :] = jnp.sum(jnp.where(is1, excl, 0.0), axis=0, keepdims=True).astype(I32)
    rank_ref[1:2, :] = jnp.sum(jnp.where(is2, excl, 0.0), axis=0, keepdims=True).astype(I32)
    new_carry = carry_ref[...] + jnp.sum(member, axis=1, keepdims=True)
    carry_ref[...] = new_carry
    cnt_ref[...] = new_carry


def _router(x2, rw_hi, rw_lo, rb, tr):
    n, d = x2.shape
    E = N_EXPERTS
    return pl.pallas_call(
        _router_kernel,
        grid=(n // tr,),
        in_specs=[
            pl.BlockSpec((tr, d), lambda i: (i, 0)),
            pl.BlockSpec((E, d), lambda i: (0, 0)),
            pl.BlockSpec((E, d), lambda i: (0, 0)),
            pl.BlockSpec((E, 1), lambda i: (0, 0)),
        ],
        out_specs=[
            pl.BlockSpec((2, tr), lambda i: (0, i)),
            pl.BlockSpec((2, tr), lambda i: (0, i)),
            pl.BlockSpec((2, tr), lambda i: (0, i)),
            pl.BlockSpec((E, LANES), lambda i: (0, 0)),
        ],
        out_shape=[jax.ShapeDtypeStruct((2, n), I32), jax.ShapeDtypeStruct((2, n), F32),
                   jax.ShapeDtypeStruct((2, n), I32), jax.ShapeDtypeStruct((E, LANES), F32)],
        scratch_shapes=[pltpu.VMEM((E, LANES), F32)],
        compiler_params=_cp(("arbitrary",)),
        name="moe_router",
    )(x2, rw_hi, rw_lo, rb)


def _slot_kernel(ps_ref, e_ref, rank_ref, slot_ref):
    e = e_ref[...]
    start = jnp.zeros(e.shape, I32)
    for ex in range(N_EXPERTS):
        start = jnp.where(e == ex, ps_ref[ex], start)
    slot_ref[...] = start + rank_ref[...]


def _slots(pad_start, e, rank, ts):
    n = e.shape[1]
    grid_spec = pltpu.PrefetchScalarGridSpec(
        num_scalar_prefetch=1,
        grid=(n // ts,),
        in_specs=[pl.BlockSpec((TOP_K, ts), lambda i, s: (0, i)), pl.BlockSpec((TOP_K, ts), lambda i, s: (0, i))],
        out_specs=pl.BlockSpec((TOP_K, ts), lambda i, s: (0, i)),
    )
    return pl.pallas_call(
        _slot_kernel,
        grid_spec=grid_spec,
        out_shape=jax.ShapeDtypeStruct((TOP_K, n), I32),
        compiler_params=_cp(("parallel",)),
        name="moe_slots",
    )(pad_start, e, rank)


def _dispatch_kernel(pe_ref, s0_ref, s1_ref, x_ref, buf_hbm, zero_ref, sem):
    td = s0_ref.shape[0]
    slots = (s0_ref, s1_ref)

    @pl.when(pl.program_id(0) == 0)
    def _():
        zero_ref[...] = jnp.zeros_like(zero_ref)

        def zero_copy(ex):
            last = pl.multiple_of(jnp.maximum(pe_ref[ex] - MOE_BLOCK, 0), MOE_BLOCK)
            return pltpu.make_async_copy(zero_ref, buf_hbm.at[pl.ds(last, MOE_BLOCK), :], sem)

        def nonempty(ex):
            return pe_ref[ex] > (pe_ref[ex - 1] if ex > 0 else 0)

        n_blocks = buf_hbm.shape[0] // MOE_BLOCK
        first_unused = pe_ref[N_EXPERTS - 1] // MOE_BLOCK

        def tail_copy(k):
            row = pl.multiple_of((first_unused + k) * MOE_BLOCK, MOE_BLOCK)
            return pltpu.make_async_copy(zero_ref, buf_hbm.at[pl.ds(row, MOE_BLOCK), :], sem)

        for ex in range(N_EXPERTS):
            pl.when(nonempty(ex))(lambda ex=ex: zero_copy(ex).start())
            pl.when(first_unused + ex < n_blocks)(lambda ex=ex: tail_copy(ex).start())
        for ex in range(N_EXPERTS):
            pl.when(nonempty(ex))(lambda ex=ex: zero_copy(ex).wait())
            pl.when(first_unused + ex < n_blocks)(lambda ex=ex: tail_copy(ex).wait())

    def issue(t, carry):
        for kk in range(TOP_K):
            dest = slots[kk][t]
            pltpu.make_async_copy(x_ref.at[pl.ds(t, 1), :], buf_hbm.at[pl.ds(dest, 1), :], sem).start()
        return carry

    lax.fori_loop(0, td, issue, 0, unroll=8)
    for kk in range(TOP_K):
        pltpu.make_async_copy(x_ref, buf_hbm.at[pl.ds(0, td), :], sem).wait()


def _dispatch(pad_end, slot0, slot1, x2, n_rows, td):
    n, d = x2.shape
    grid_spec = pltpu.PrefetchScalarGridSpec(
        num_scalar_prefetch=1,
        grid=(n // td,),
        in_specs=[
            pl.BlockSpec((td,), lambda i, s: (i,), memory_space=pltpu.SMEM),
            pl.BlockSpec((td,), lambda i, s: (i,), memory_space=pltpu.SMEM),
            pl.BlockSpec((td, d), lambda i, s: (i, 0)),
        ],
        out_specs=pl.BlockSpec(memory_space=pl.ANY),
        scratch_shapes=[pltpu.VMEM((MOE_BLOCK, d), F32), pltpu.SemaphoreType.DMA(())],
    )
    return pl.pallas_call(
        _dispatch_kernel,
        grid_spec=grid_spec,
        out_shape=jax.ShapeDtypeStruct((n_rows, d), F32),
        compiler_params=_cp(("arbitrary",)),
        name="moe_dispatch",
    )(pad_end, slot0, slot1, x2)


def _expert_kernel(be_ref, nb_ref, x_ref, win_hbm, wdn_hbm, y_ref, xb_ref, wa_s, wu_s, wd_s, sa, su, sd, sems,
                   *, layer):
    b = pl.program_id(0)
    nf = D_FF // FF_TILE
    n_used = nb_ref[0]
    e = be_ref[b]
    e_prev = be_ref[jnp.maximum(b - 1, 0)]
    e_next = be_ref[jnp.minimum(b + 1, pl.num_programs(0) - 1)]
    active = b < n_used
    is_first = active & ((b == 0) | (e_prev != e))
    feeds_next = active & (b + 1 < n_used) & (e_next != e)

    def tile_copies(ex, f):
        lo = f * FF_TILE
        return (pltpu.make_async_copy(win_hbm.at[layer, ex, :, pl.ds(lo, FF_TILE)], sa, sems.at[0]),
                pltpu.make_async_copy(win_hbm.at[layer, ex, :, pl.ds(D_FF + lo, FF_TILE)], su, sems.at[1]),
                pltpu.make_async_copy(wdn_hbm.at[layer, ex, pl.ds(lo, FF_TILE), :], sd, sems.at[2]))

    def start(ex, f):
        for c in tile_copies(ex, f):
            c.start()

    def finish(ex, f):
        for c in tile_copies(ex, f):
            c.wait()
        wa_s[f] = sa[...].astype(BF16)
        wu_s[f] = su[...].astype(BF16)
        wd_s[f] = sd[...].astype(BF16)

    @pl.when(b == 0)
    def _():
        for f in range(nf - 1):
            start(e, f)
            finish(e, f)
        start(e, nf - 1)

    @pl.when(jnp.logical_not(active))
    def _():
        y_ref[...] = jnp.zeros_like(y_ref)

    @pl.when(active)
    def _():
        xb_ref[...] = x_ref[...].astype(BF16)
        for f in range(nf):
            xb = xb_ref[...]
            a = _dot(xb, wa_s[f])
            u = _dot(xb, wu_s[f])
            act = (a * jax.nn.sigmoid(a) * u).astype(BF16)
            y = _dot(act, wd_s[f])
            if f == 0:
                y_ref[...] = y
                pl.when(is_first)(lambda: finish(e, nf - 1))
            else:
                y_ref[...] += y

            @pl.when(feeds_next)
            def _(f=f):
                if f >= 1:
                    finish(e_next, f - 1)
                start(e_next, f)


def _experts(blk_expert, n_used, buf, w_in, w_down, layer):
    p, d = buf.shape
    nb = p // MOE_BLOCK
    nf = D_FF // FF_TILE
    grid_spec = pltpu.PrefetchScalarGridSpec(
        num_scalar_prefetch=2,
        grid=(nb,),
        in_specs=[
            pl.BlockSpec((MOE_BLOCK, d), lambda b, be, nu: (jnp.minimum(b, nu[0] - 1), 0)),
            pl.BlockSpec(memory_space=pl.ANY),
            pl.BlockSpec(memory_space=pl.ANY),
        ],
        out_specs=pl.BlockSpec((MOE_BLOCK, d), lambda b, be, nu: (b, 0)),
        scratch_shapes=[
            pltpu.VMEM((MOE_BLOCK, d), BF16),
            pltpu.VMEM((nf, d, FF_TILE), BF16),
            pltpu.VMEM((nf, d, FF_TILE), BF16),
            pltpu.VMEM((nf, FF_TILE, d), BF16),
            pltpu.VMEM((d, FF_TILE), F32),
            pltpu.VMEM((d, FF_TILE), F32),
            pltpu.VMEM((FF_TILE, d), F32),
            pltpu.SemaphoreType.DMA((3,)),
        ],
    )
    return pl.pallas_call(
        functools.partial(_expert_kernel, layer=layer),
        grid_spec=grid_spec,
        out_shape=jax.ShapeDtypeStruct((p, d), F32),
        compiler_params=_cp(("arbitrary",)),
        name="moe_experts",
    )(blk_expert, n_used, buf, w_in, w_down)


def _combine_kernel(s0_ref, s1_ref, n0_ref, n1_ref, y_hbm, x_ref, gate_ref, lg_ref, lb_ref, x3_ref, x3b_ref,
                    y0_ref, y1_ref, sems):
    i = pl.program_id(0)
    tc = x_ref.shape[0]
    bufs = (y0_ref, y1_ref)
    cur = i % 2

    def issue_tile(slot_refs, half):
        def issue(t, carry):
            for kk in range(TOP_K):
                src = slot_refs[kk][t]
                pltpu.make_async_copy(y_hbm.at[pl.ds(src, 1), :], bufs[kk].at[half, pl.ds(t, 1), :],
                                      sems.at[half]).start()
            return carry

        lax.fori_loop(0, tc, issue, 0, unroll=8)

    @pl.when(i == 0)
    def _():
        issue_tile((s0_ref, s1_ref), 0)

    @pl.when(i + 1 < pl.num_programs(0))
    def _():
        issue_tile((n0_ref, n1_ref), 1 - cur)

    for kk in range(TOP_K):
        pltpu.make_async_copy(y_hbm.at[pl.ds(0, tc), :], bufs[kk].at[cur], sems.at[cur]).wait()
    gate = gate_ref[...]
    z = DN_ALPHA * x_ref[...] + gate[:, 0:1] * y0_ref[cur] + gate[:, 1:2] * y1_ref[cur]
    x3 = _layer_norm(z, lg_ref[...], lb_ref[...])
    x3_ref[...] = x3
    x3b_ref[...] = x3.astype(BF16)


def _combine(slot0, slot1, y, x2, gate_nt, lg, lb, tc):
    n, d = x2.shape
    last = n // tc - 1
    return pl.pallas_call(
        _combine_kernel,
        grid=(n // tc,),
        in_specs=[
            pl.BlockSpec((tc,), lambda i: (i,), memory_space=pltpu.SMEM),
            pl.BlockSpec((tc,), lambda i: (i,), memory_space=pltpu.SMEM),
            pl.BlockSpec((tc,), lambda i: (jnp.minimum(i + 1, last),), memory_space=pltpu.SMEM),
            pl.BlockSpec((tc,), lambda i: (jnp.minimum(i + 1, last),), memory_space=pltpu.SMEM),
            pl.BlockSpec(memory_space=pl.ANY),
            pl.BlockSpec((tc, d), lambda i: (i, 0)),
            pl.BlockSpec((tc, 2), lambda i: (i, 0)),
            pl.BlockSpec((1, d), lambda i: (0, 0)),
            pl.BlockSpec((1, d), lambda i: (0, 0)),
        ],
        out_specs=[pl.BlockSpec((tc, d), lambda i: (i, 0)), pl.BlockSpec((tc, d), lambda i: (i, 0))],
        out_shape=[jax.ShapeDtypeStruct((n, d), F32), jax.ShapeDtypeStruct((n, d), BF16)],
        scratch_shapes=[pltpu.VMEM((2, tc, d), F32), pltpu.VMEM((2, tc, d), F32), pltpu.SemaphoreType.DMA((2,))],
        compiler_params=_cp(("arbitrary",)),
        name="moe_combine_ln",
    )(slot0, slot1, slot0, slot1, y, x2, gate_nt, lg, lb)


def _layer(x, xb, mem_b, p, moe_w, layer, consts, B, T):
    n, d = x.shape
    G, HPG, DH = NSA_GROUPS, NSA_HPG, NSA_DH
    slopes, ovt, epad, tile_ind = consts

    h_big = _matmul(xb, p["w_big"], BF16, 1024, 1024)
    h_small = _matmul(xb, p["w_small"], F32, 1024, SCOL_END)
    h_t = _matmul_t(xb, p["w_t"], B, T, 1024, TROW_END // 2)

    o_gla = _gla(h_big, h_small, p["wa_pad"], p["b_a"], p["norm_g"], B, T, 4 if B % 4 == 0 else 1, 512)

    kcmp, kcmp_t = _compress(h_small, p["cmp_w1bd"], p["cmp_w2bd"], p["cmp_w2bdt"], p["cmp_pe_pair"], B, T)
    tq_sel = 256
    ocmp_t, mb, in_tile = _cmp_select(slopes, h_t, kcmp, kcmp_t, ovt, tile_ind, B, T, 512)
    nq = T // tq_sel
    tile_flags = (in_tile.reshape(B, G, nq, nq, tq_sel).max(axis=-1) > 0).astype(I32)
    tile_flags = tile_flags.transpose(0, 1, 3, 2).reshape(-1)
    gates_t = h_small[:, GLA_GATE_RANK:GLA_GATE_RANK + 3 * NSA_HEADS].reshape(B, T, G, 3 * HPG)
    gates_t = jnp.pad(gates_t.transpose(0, 2, 3, 1), ((0, 0), (0, 0), (0, 16 - 3 * HPG), (0, 0)))
    o_nsa = _sel_win(slopes, tile_flags, h_t, h_big, epad, mb, ocmp_t, gates_t, B, T, tq_sel,
                     2 if B % 2 == 0 else 1)

    x1, x1b = _mix(o_gla, o_nsa, h_big, x, p["w_bg"], p["w_bn"], p["w_out"], p["ln_mix_g"], p["ln_mix_b"], 512)

    kvm = _matmul(mem_b, p["xa_wkv"], BF16, 512, 512).reshape(B, MEM_LEN, 2 * XA_HEADS * XA_DH)
    x2 = _xattn(x1, x1b, kvm, p["xa_wq"], p["xa_wo"], p["ln_xa_g"], p["ln_xa_b"], B, T, 512)

    e, gate, rank, cnt = _router(x2, p["rw_hi"], p["rw_lo"], p["rb"], 512)
    counts = cnt[:, 0].astype(I32)
    padded = (counts + MOE_BLOCK - 1) // MOE_BLOCK * MOE_BLOCK
    pad_end = jnp.cumsum(padded)
    pad_start = (pad_end - padded).astype(I32)
    nb = (n * TOP_K) // MOE_BLOCK + N_EXPERTS
    n_used = (pad_end[-1] // MOE_BLOCK).astype(I32).reshape(1)
    blk_start = jnp.arange(nb, dtype=I32) * MOE_BLOCK
    blk_expert = jnp.minimum(jnp.sum(blk_start[:, None] >= pad_end[None, :], axis=1), N_EXPERTS - 1).astype(I32)
    blk_expert = jnp.where(jnp.arange(nb) < n_used[0], blk_expert, blk_expert[jnp.maximum(n_used[0] - 1, 0)])
    slot = _slots(pad_start, e, rank, 2048)
    buf = _dispatch(pad_end.astype(I32), slot[0], slot[1], x2, nb * MOE_BLOCK, 512)
    y = _experts(blk_expert, n_used, buf, moe_w[0], moe_w[1], layer)
    x3, x3b = _combine(slot[0], slot[1], y, x2, gate.T, p["ln_ffn_g"], p["ln_ffn_b"], 256)
    return x3, x3b


def _prep_layer(l, w_in, gla_w_a2, gla_b_a, gla_norm_g, nsa_cmp_pe, nsa_cmp_w1, nsa_cmp_w2, w_branch_gla,
                w_branch_nsa, w_out, ln_mix_g, ln_mix_b, xa_wq, xa_wkv, xa_wo, ln_xa_g, ln_xa_b, router_w,
                router_b, moe_w_in, moe_w_down, ln_ffn_g, ln_ffn_b):
    d = w_in.shape[1]
    w = w_in[l]
    o_gq, o_gk, o_gv, o_gr = 0, GLA_QK, 2 * GLA_QK, 2 * GLA_QK + GLA_V
    o_ga = o_gr + GLA_V
    o_nq = o_ga + GLA_GATE_RANK
    o_nkv = o_nq + NSA_Q
    o_ng = o_nkv + 6 * NSA_KV
    o_mg = o_ng + 3 * NSA_HEADS
    G, DH = NSA_GROUPS, NSA_DH

    def kv_cols(kind):
        return w[:, o_nkv + kind * NSA_KV:o_nkv + (kind + 1) * NSA_KV]

    def slabs(wk):
        return jnp.pad(wk.reshape(d, G, DH), ((0, 0), (0, 0), (0, LANES - DH))).reshape(d, G * LANES)

    w_big = jnp.concatenate([w[:, o_mg:o_mg + 2 * d], w[:, o_gq:o_ga], slabs(kv_cols(2)), slabs(kv_cols(4))],
                            axis=1).astype(BF16)
    w_small = jnp.concatenate([w[:, o_ga:o_nq], w[:, o_ng:o_mg],
                               jnp.zeros((d, LANES - GLA_GATE_RANK - 3 * NSA_HEADS), F32),
                               kv_cols(0), kv_cols(1)], axis=1).astype(BF16)
    w_t = jnp.concatenate([w[:, o_nq:o_nkv], kv_cols(3), kv_cols(5)], axis=1).T.astype(BF16)
    w1 = nsa_cmp_w1[l].reshape(2, CMP_LEN, DH, CMP_HIDDEN)
    z1 = jnp.zeros_like(w1)
    w1bd = jnp.concatenate([jnp.concatenate([w1, z1], axis=3), jnp.concatenate([z1, w1], axis=3)], axis=2)
    w2 = nsa_cmp_w2[l]
    z2 = jnp.zeros_like(w2)
    w2bd = jnp.concatenate([jnp.concatenate([w2, z2], axis=2), jnp.concatenate([z2, w2], axis=2)], axis=1)
    pe = nsa_cmp_pe[l]
    pe_pair = jnp.broadcast_to(jnp.concatenate([pe, pe], axis=-1)[:, :, None, :], (2, CMP_LEN, 16, 2 * DH))
    wa_pad = jnp.concatenate([gla_w_a2[l], jnp.zeros((LANES - GLA_GATE_RANK, GLA_QK), F32)], axis=0).astype(BF16)
    rw_t = router_w.T
    rw_hi = rw_t.astype(BF16)
    rw_lo = (rw_t - rw_hi.astype(F32)).astype(BF16)
    return dict(
        w_big=w_big, w_small=w_small, w_t=w_t, wa_pad=wa_pad,
        b_a=gla_b_a[l].reshape(1, -1), norm_g=gla_norm_g[l].reshape(1, -1),
        cmp_w1bd=w1bd.astype(BF16), cmp_w2bd=w2bd.astype(BF16), cmp_w2bdt=w2bd.transpose(0, 2, 1).astype(BF16),
        cmp_pe_pair=pe_pair.astype(BF16),
        w_bg=w_branch_gla[l].astype(BF16), w_bn=w_branch_nsa[l].astype(BF16), w_out=w_out[l].astype(BF16),
        ln_mix_g=ln_mix_g[l].reshape(1, -1), ln_mix_b=ln_mix_b[l].reshape(1, -1),
        xa_wq=xa_wq[l].astype(BF16), xa_wkv=xa_wkv[l].astype(BF16), xa_wo=xa_wo[l].astype(BF16),
        ln_xa_g=ln_xa_g[l].reshape(1, -1), ln_xa_b=ln_xa_b[l].reshape(1, -1),
        rw_hi=rw_hi, rw_lo=rw_lo, rb=router_b.reshape(-1, 1),
        ln_ffn_g=ln_ffn_g[l].reshape(1, -1), ln_ffn_b=ln_ffn_b[l].reshape(1, -1),
    )


def kernel(x, mem, w_in, gla_w_a2, gla_b_a, gla_norm_g, nsa_cmp_pe, nsa_cmp_w1, nsa_cmp_w2, w_branch_gla, w_branch_nsa, w_out, ln_mix_g, ln_mix_b, xa_wq, xa_wkv, xa_wo, ln_xa_g, ln_xa_b, router_w, router_b, moe_w_in, moe_w_down, ln_ffn_g, ln_ffn_b):
    B, T, d = x.shape
    assert T % 512 == 0 and d == 2048 and mem.shape[1] == MEM_LEN
    n = B * T
    params = (w_in, gla_w_a2, gla_b_a, gla_norm_g, nsa_cmp_pe, nsa_cmp_w1, nsa_cmp_w2, w_branch_gla, w_branch_nsa,
              w_out, ln_mix_g, ln_mix_b, xa_wq, xa_wkv, xa_wo, ln_xa_g, ln_xa_b, router_w, router_b, moe_w_in,
              moe_w_down, ln_ffn_g, ln_ffn_b)
    slopes = (2.0 ** (-8.0 * jnp.arange(1, NSA_HEADS + 1, dtype=F32) / NSA_HEADS)).astype(F32)
    nc, ns = T // CMP_STRIDE, T // SEL_LEN
    cs = np.arange(nc) * CMP_STRIDE
    ss = np.arange(ns) * SEL_LEN
    ovt = ((cs[None, :] < ss[:, None] + SEL_LEN) & (cs[None, :] + CMP_LEN > ss[:, None])
           & (cs[None, :] + CMP_LEN <= T)).astype(np.float32)
    assert NSA_DH + ns <= LANES
    epad = np.zeros((T, LANES), np.float32)
    epad[np.arange(T), NSA_DH + np.arange(T) // SEL_LEN] = 1.0
    tile_ind = (np.arange(ns)[None, :] // (256 // SEL_LEN) == np.arange(T // 256)[:, None]).astype(np.float32)
    consts = (slopes, jnp.asarray(ovt, BF16), jnp.asarray(epad, BF16), jnp.asarray(tile_ind, BF16))

    xf = x.reshape(n, d)
    xb = xf.astype(BF16)
    mem_b = mem.reshape(B * MEM_LEN, d).astype(BF16)
    moe_w = (moe_w_in, moe_w_down)
    for l in range(DEPTH):
        p = _prep_layer(l, *params)
        xf, xb = _layer(xf, xb, mem_b, p, moe_w, l, consts, B, T)
    return xf.reshape(B, T, d)
```

```python
import functools

import jax
import jax.numpy as jnp
import numpy as np
from jax import lax
from jax.experimental import pallas as pl
from jax.experimental.pallas import tpu as pltpu

F32 = jnp.float32
BF16 = jnp.bfloat16
I32 = jnp.int32

DEPTH = 2
MEM_LEN = 256
GLA_HEADS = 4
GLA_DK = 128
GLA_DV = 256
GLA_GATE_RANK = 16
GLA_TAU = 16.0
GLA_CHUNK = 64
NSA_HEADS = 16
NSA_GROUPS = 4
NSA_HPG = NSA_HEADS // NSA_GROUPS
NSA_DH = 64
CMP_LEN = 32
CMP_STRIDE = 16
CMP_HIDDEN = 256
SEL_LEN = 64
SEL_TOPN = 8
WINDOW = 512
XA_HEADS = 4
XA_DH = 128
N_EXPERTS = 16
N_GROUPS = 4
EXPERTS_PER_GROUP = N_EXPERTS // N_GROUPS
TOP_K = 2
D_FF = 1536
DN_ALPHA = float((2 * DEPTH) ** 0.25)
LN_EPS = 1e-5
NEG = -1e30
LOG2E = 1.4426950408889634
FORCE_BONUS = 1e6

GLA_QK = GLA_HEADS * GLA_DK
GLA_V = GLA_HEADS * GLA_DV
NSA_Q = NSA_HEADS * NSA_DH
NSA_KV = NSA_GROUPS * NSA_DH

LANES = 128
VMEM_LIMIT = 56 * 1024 * 1024

COL_MG = 0
COL_GQ = 2 * 2048
COL_GK = COL_GQ + GLA_QK
COL_GV = COL_GK + GLA_QK
COL_GR = COL_GV + GLA_V
COL_KS = COL_GR + GLA_V
COL_KW = COL_KS + NSA_GROUPS * LANES
COL_END = COL_KW + NSA_GROUPS * LANES
SCOL_CK = LANES
SCOL_CV = SCOL_CK + NSA_KV
SCOL_END = SCOL_CV + NSA_KV
TROW_Q = 0
TROW_VS = NSA_Q
TROW_VW = TROW_VS + NSA_KV
TROW_END = TROW_VW + NSA_KV

MOE_BLOCK = 512
FF_TILE = 512


def _cp(sem):
    return pltpu.CompilerParams(dimension_semantics=sem, vmem_limit_bytes=VMEM_LIMIT)


def _dot(a, b):
    return jnp.dot(a, b, preferred_element_type=F32)


def _dot_nt(a, b):
    return lax.dot_general(a, b, (((1,), (1,)), ((), ())), preferred_element_type=F32)


def _dot_tn(a, b):
    return lax.dot_general(a, b, (((0,), (0,)), ((), ())), preferred_element_type=F32)


def _layer_norm(z, g, b):
    mu = jnp.mean(z, axis=-1, keepdims=True)
    zc = z - mu
    var = jnp.mean(zc * zc, axis=-1, keepdims=True)
    return zc * lax.rsqrt(var + LN_EPS) * g + b


def _mm_kernel(a_ref, b_ref, o_ref):
    o_ref[...] = _dot(a_ref[...], b_ref[...]).astype(o_ref.dtype)


def _matmul(a, b, out_dtype, tm, tn):
    m, k = a.shape
    n = b.shape[1]
    return pl.pallas_call(
        _mm_kernel,
        grid=(m // tm, n // tn),
        in_specs=[pl.BlockSpec((tm, k), lambda i, j: (i, 0)),
                  pl.BlockSpec((k, tn), lambda i, j: (0, j))],
        out_specs=pl.BlockSpec((tm, tn), lambda i, j: (i, j)),
        out_shape=jax.ShapeDtypeStruct((m, n), out_dtype),
        compiler_params=_cp(("parallel", "parallel")),
        name="matmul",
    )(a, b)


def _mm_nt_kernel(wt_ref, x_ref, o_ref):
    o_ref[0] = _dot_nt(wt_ref[...], x_ref[...]).astype(o_ref.dtype)


def _matmul_t(x, wt, B, T, tm, tr):
    n, k = x.shape
    r = wt.shape[0]
    nt = T // tm
    return pl.pallas_call(
        _mm_nt_kernel,
        grid=(n // tm, r // tr),
        in_specs=[pl.BlockSpec((tr, k), lambda i, j: (j, 0)),
                  pl.BlockSpec((tm, k), lambda i, j: (i, 0))],
        out_specs=pl.BlockSpec((1, tr, tm), lambda i, j: (i // nt, j, i % nt)),
        out_shape=jax.ShapeDtypeStruct((B, r, T), BF16),
        compiler_params=_cp(("parallel", "parallel")),
        name="matmul_t",
    )(wt, x)


def _gla_kernel(q_ref, k_ref, v_ref, r_ref, sm_ref, wa_ref, ba_ref, ng_ref, o_ref, st_ref):
    C = GLA_CHUNK
    seqs = q_ref.shape[1]
    n_chunks = q_ref.shape[2] // C

    @pl.when(pl.program_id(1) == 0)
    def _():
        st_ref[...] = jnp.zeros_like(st_ref)

    rowi = lax.broadcasted_iota(I32, (C, GLA_DK), 0)
    tt = lax.broadcasted_iota(I32, (C, C), 0)
    ss = lax.broadcasted_iota(I32, (C, C), 1)
    levels = (1, 2, 4, 8, 16, 32)
    pair_masks = [((tt // (2 * L)) == (ss // (2 * L))) & ((tt & L) != 0) & ((ss & L) == 0) for L in levels]
    diag_mask = tt == ss
    scale = GLA_DK ** -0.5

    def head_chunk(sq, rows, h, z):
        qk_cols = slice(h * GLA_DK, (h + 1) * GLA_DK)
        v_cols = slice(h * GLA_DV, (h + 1) * GLA_DV)
        state = sq * GLA_HEADS + h
        q = q_ref[0, sq, rows, qk_cols].astype(F32) * scale
        k = k_ref[0, sq, rows, qk_cols].astype(F32)
        v = v_ref[0, sq, rows, v_cols]
        g = (jnp.minimum(z, 0.0) - jnp.log1p(jnp.exp(-jnp.abs(z)))) * (1.0 / GLA_TAU)
        incl = g
        tot = g
        att = jnp.where(diag_mask, _dot_nt(q.astype(BF16), k.astype(BF16)), 0.0)
        for L, pm in zip(levels, pair_masks):
            ql = (q * jnp.exp(incl)).astype(BF16)
            kl = (k * jnp.exp(tot - incl)).astype(BF16)
            att = jnp.where(pm, _dot_nt(ql, kl), att)
            upper = (rowi & L) != 0
            from_lower = pltpu.roll(tot, L, 0)
            from_upper = pltpu.roll(tot, C - L, 0)
            incl = incl + jnp.where(upper, from_lower, 0.0)
            tot = tot + jnp.where(upper, from_lower, from_upper)
        qd = (q * jnp.exp(incl)).astype(BF16)
        kd = (k * jnp.exp(tot - incl)).astype(BF16)
        st = st_ref[state]
        o = _dot_nt(qd, st.astype(BF16)) + _dot(att.astype(BF16), v)
        st_ref[state] = st * jnp.exp(tot[0:1, :]) + _dot_tn(v, kd)
        mu = jnp.mean(o, axis=-1, keepdims=True)
        oc = o - mu
        var = jnp.mean(oc * oc, axis=-1, keepdims=True)
        on = oc * lax.rsqrt(var + LN_EPS) * ng_ref[:, v_cols]
        r = r_ref[0, sq, rows, v_cols].astype(F32)
        o_ref[0, sq, rows, v_cols] = (on * (r * jax.nn.sigmoid(r))).astype(o_ref.dtype)

    def chunk(c, carry):
        rows = pl.ds(pl.multiple_of(c * C, C), C)
        for sq in range(seqs):
            z = _dot(sm_ref[0, sq, rows, :].astype(BF16), wa_ref[...]) + ba_ref[...]
            for h in range(GLA_HEADS):
                head_chunk(sq, rows, h, z[:, h * GLA_DK:(h + 1) * GLA_DK])
        return carry

    lax.fori_loop(0, n_chunks, chunk, 0)


def _gla(h_big, h_small, wa_pad, b_a, norm_g, B, T, seqs, tb):
    groups = B // seqs
    hb = h_big.reshape(groups, seqs, T, h_big.shape[1])
    hs = h_small.reshape(groups, seqs, T, h_small.shape[1])
    out = pl.pallas_call(
        _gla_kernel,
        grid=(groups, T // tb),
        in_specs=[
            pl.BlockSpec((1, seqs, tb, GLA_QK), lambda b, j: (b, 0, j, COL_GQ // GLA_QK)),
            pl.BlockSpec((1, seqs, tb, GLA_QK), lambda b, j: (b, 0, j, COL_GK // GLA_QK)),
            pl.BlockSpec((1, seqs, tb, GLA_V), lambda b, j: (b, 0, j, COL_GV // GLA_V)),
            pl.BlockSpec((1, seqs, tb, GLA_V), lambda b, j: (b, 0, j, COL_GR // GLA_V)),
            pl.BlockSpec((1, seqs, tb, LANES), lambda b, j: (b, 0, j, 0)),
            pl.BlockSpec((LANES, GLA_QK), lambda b, j: (0, 0)),
            pl.BlockSpec((1, GLA_QK), lambda b, j: (0, 0)),
            pl.BlockSpec((1, GLA_V), lambda b, j: (0, 0)),
        ],
        out_specs=pl.BlockSpec((1, seqs, tb, GLA_V), lambda b, j: (b, 0, j, 0)),
        out_shape=jax.ShapeDtypeStruct((groups, seqs, T, GLA_V), BF16),
        scratch_shapes=[pltpu.VMEM((seqs * GLA_HEADS, GLA_DV, GLA_DK), F32)],
        compiler_params=_cp(("parallel", "arbitrary")),
        name="gla",
    )(hb, hb, hb, hb, hs, wa_pad, b_a, norm_g)
    return out.reshape(B * T, GLA_V)


def _compress_kernel(x_ref, w1_ref, w2_ref, w2t_ref, pe_ref, o_ref, ot_ref):
    nc = x_ref.shape[0] // CMP_STRIDE
    hid_w = w1_ref.shape[3]
    a = jnp.zeros((nc, hid_w), F32)
    bm = jnp.zeros((nc, hid_w), F32)
    c = jnp.zeros((pe_ref.shape[2], hid_w), F32)
    for l in range(CMP_STRIDE):
        xl = x_ref[pl.ds(l, nc, stride=CMP_STRIDE), :].astype(BF16)
        a = a + _dot(xl, w1_ref[0, l])
        bm = bm + _dot(xl, w1_ref[0, CMP_STRIDE + l])
    for l in range(CMP_LEN):
        c = c + _dot(pe_ref[0, l], w1_ref[0, l])
    hid = a + pltpu.roll(bm, nc - 1, 0) + c[0:1, :]
    act = jax.nn.gelu(hid).astype(BF16)
    o_ref[0, 0, 0] = _dot(act, w2_ref[0]).astype(o_ref.dtype)
    ot_ref[0, 0, 0] = _dot_nt(w2t_ref[0], act).astype(ot_ref.dtype)


def _compress(h_small, w1bd, w2bd, w2bdt, pe_pair, B, T):
    nc = T // CMP_STRIDE
    pairs = NSA_GROUPS // 2
    return pl.pallas_call(
        _compress_kernel,
        grid=(B, 2, pairs),
        in_specs=[
            pl.BlockSpec((T, LANES), lambda b, s, j: (b, SCOL_CK // LANES + s * pairs + j)),
            pl.BlockSpec((1,) + w1bd.shape[1:], lambda b, s, j: (s, 0, 0, 0)),
            pl.BlockSpec((1,) + w2bd.shape[1:], lambda b, s, j: (s, 0, 0)),
            pl.BlockSpec((1,) + w2bdt.shape[1:], lambda b, s, j: (s, 0, 0)),
            pl.BlockSpec((1,) + pe_pair.shape[1:], lambda b, s, j: (s, 0, 0, 0)),
        ],
        out_specs=[
            pl.BlockSpec((1, 1, 1, nc, LANES), lambda b, s, j: (b, s, j, 0, 0)),
            pl.BlockSpec((1, 1, 1, LANES, nc), lambda b, s, j: (b, s, j, 0, 0)),
        ],
        out_shape=[jax.ShapeDtypeStruct((B, 2, pairs, nc, LANES), BF16),
                   jax.ShapeDtypeStruct((B, 2, pairs, LANES, nc), BF16)],
        compiler_params=_cp(("parallel", "parallel", "parallel")),
        name="nsa_compress",
    )(h_small, w1bd, w2bd, w2bdt, pe_pair)


def _cmp_select_kernel(slopes_ref, qt_ref, kc_ref, vct_ref, ovt_ref, ind_ref, ocmp_ref, mb_ref, kt_ref, qpad_ref):
    g = pl.program_id(1)
    i = pl.program_id(2)
    streams = qt_ref.shape[1]
    tq = qt_ref.shape[3]
    nc = kc_ref.shape[4]
    ns = mb_ref.shape[3]
    dh = NSA_DH
    t0 = i * tq
    wide = NSA_HPG * tq
    tpos = t0 + (lax.broadcasted_iota(I32, (nc, wide), 1) & (tq - 1))
    nidx = lax.broadcasted_iota(I32, (nc, wide), 0)
    mask_c = (nidx * CMP_STRIDE + (CMP_LEN - 1)) <= tpos
    absd = jnp.abs(tpos.astype(F32) - (nidx.astype(F32) * CMP_STRIDE + 0.5 * (CMP_LEN - 1)))
    srow = jnp.concatenate([jnp.full((1, tq), slopes_ref[g * NSA_HPG + hh], F32) for hh in range(NSA_HPG)], axis=1)
    lower = g % 2 == 0
    j = lax.broadcasted_iota(I32, (ns, tq), 0)
    tp = t0 + lax.broadcasted_iota(I32, (ns, tq), 1)
    cur = tp // SEL_LEN
    forced = (j == 0) | (j == cur) | (j == cur - 1)
    valid = j * SEL_LEN <= tp
    for u in range(streams):
        kc = kc_ref[0, u, 0, 0]
        vct = jnp.where(lower, vct_ref[0, u, 0, 0, 0:dh, :], vct_ref[0, u, 0, 0, dh:2 * dh, :])
        for hh in range(NSA_HPG):
            q = qt_ref[0, u, hh * dh:(hh + 1) * dh, :] * jnp.asarray(dh ** -0.5, BF16)
            zero = jnp.zeros_like(q)
            qpad_ref[u, 0:dh, hh * tq:(hh + 1) * tq] = jnp.where(lower, q, zero)
            qpad_ref[u, dh:2 * dh, hh * tq:(hh + 1) * tq] = jnp.where(lower, zero, q)
        s = _dot(kc, qpad_ref[u]) - srow * absd
        s = jnp.where(mask_c, s, NEG)
        e = jnp.exp(s - jnp.max(s, axis=0, keepdims=True))
        p = jnp.where(mask_c, e * (1.0 / jnp.sum(e, axis=0, keepdims=True)), 0.0)
        o = _dot(vct, p.astype(BF16))
        psum = jnp.zeros((nc, tq), F32)
        for hh in range(NSA_HPG):
            ocmp_ref[0, u, hh * dh:(hh + 1) * dh, :] = o[:, hh * tq:(hh + 1) * tq].astype(ocmp_ref.dtype)
            psum = psum + p[:, hh * tq:(hh + 1) * tq]
        p_hi = psum.astype(BF16)
        p_lo = (psum - p_hi.astype(F32)).astype(BF16)
        imp = _dot(ovt_ref[...], p_hi) + _dot(ovt_ref[...], p_lo)
        score = jnp.where(valid, imp + jnp.where(forced, FORCE_BONUS, 0.0), NEG)
        rank = jnp.zeros((ns, tq), F32)
        for jp in range(ns):
            row = score[jp:jp + 1, :]
            beats = (row > score) | ((row == score) & (j > jp))
            rank = rank + jnp.where(beats, 1.0, 0.0)
        keep = valid & (rank < float(min(SEL_TOPN, ns)))
        mb_ref[0, u, 0] = jnp.where(keep, 0.0, NEG).astype(mb_ref.dtype)
        kt_ref[0, u, 0] = _dot(ind_ref[...], jnp.where(keep, 1.0, 0.0).astype(BF16))


def _cmp_select(slopes, h_t, kcmp, kcmp_t, ovt, tile_ind, B, T, tq, streams):
    nc = T // CMP_STRIDE
    ns = T // SEL_LEN
    nkt = tile_ind.shape[0]
    G = NSA_GROUPS
    grp_rows = NSA_HPG * NSA_DH
    nbg = B // streams
    h_t4 = h_t.reshape(nbg, streams, h_t.shape[1], T)
    kcmp6 = kcmp.reshape((nbg, streams) + kcmp.shape[1:])
    kcmp_t6 = kcmp_t.reshape((nbg, streams) + kcmp_t.shape[1:])
    grid_spec = pltpu.PrefetchScalarGridSpec(
        num_scalar_prefetch=1,
        grid=(nbg, G, T // tq),
        in_specs=[
            pl.BlockSpec((1, streams, grp_rows, tq), lambda b, g, i, s: (b, 0, TROW_Q // grp_rows + g, i)),
            pl.BlockSpec((1, streams, 1, 1, nc, LANES), lambda b, g, i, s: (b, 0, 0, g // 2, 0, 0)),
            pl.BlockSpec((1, streams, 1, 1, LANES, nc), lambda b, g, i, s: (b, 0, 1, g // 2, 0, 0)),
            pl.BlockSpec((ns, nc), lambda b, g, i, s: (0, 0)),
            pl.BlockSpec((nkt, ns), lambda b, g, i, s: (0, 0)),
        ],
        out_specs=[
            pl.BlockSpec((1, streams, grp_rows, tq), lambda b, g, i, s: (b, 0, g, i)),
            pl.BlockSpec((1, streams, 1, ns, tq), lambda b, g, i, s: (b, 0, g, 0, i)),
            pl.BlockSpec((1, streams, 1, nkt, tq), lambda b, g, i, s: (b, 0, g, 0, i)),
        ],
        scratch_shapes=[pltpu.VMEM((streams, LANES, NSA_HPG * tq), BF16)],
    )
    ocmp, mb, in_tile = pl.pallas_call(
        _cmp_select_kernel,
        grid_spec=grid_spec,
        out_shape=[jax.ShapeDtypeStruct((nbg, streams, NSA_Q, T), BF16),
                   jax.ShapeDtypeStruct((nbg, streams, G, ns, T), BF16),
                   jax.ShapeDtypeStruct((nbg, streams, G, nkt, T), F32)],
        compiler_params=_cp(("parallel", "parallel", "parallel")),
        name="nsa_cmp_select",
    )(slopes, h_t4, kcmp6, kcmp_t6, ovt, tile_ind)
    return ocmp.reshape(B, NSA_Q, T), mb.reshape(B, G, ns, T), in_tile.reshape(B, G, nkt, T)


def _sel_win_kernel(slopes_ref, flags_ref, qt_ref, ks_ref, kw_ref, vs_ref, vw_ref, epad_ref, mb_ref, ocmp_ref, gt_ref,
                    o_ref, qaug_ref, m_ref, acc_ref, srow_ref, bias_ref, s_ref, p_ref, alpha_ref,
                    kall_ref, vall_ref, tiles_ref, *, n_tiles):
    g = pl.program_id(1)
    i = pl.program_id(2)
    streams = qt_ref.shape[1]
    tq = qt_ref.shape[3]
    tk = tq
    ns = mb_ref.shape[3]
    dh = NSA_DH
    wide = NSA_HPG * tq
    BIG = -NEG

    @pl.when(i == 0)
    def _():
        srow = jnp.concatenate([jnp.full((1, tq), slopes_ref[g * NSA_HPG + hh] * LOG2E, F32)
                                for hh in range(NSA_HPG)], axis=1)
        srow_ref[...] = srow
        lane = lax.broadcasted_iota(I32, (tk, wide), 1) & (tq - 1)
        dist0 = (lane - lax.broadcasted_iota(I32, (tk, wide), 0)).astype(F32)
        sd0 = srow * dist0
        bias_ref[0] = sd0
        bias_ref[1] = sd0 + jnp.where(dist0 >= 0.0, 0.0, BIG)
        bias_ref[2] = sd0 + jnp.where(dist0 < 0.0, 0.0, BIG)
        bias_ref[3] = jnp.full((tk, wide), BIG, F32)
        extra = jnp.where(lax.broadcasted_iota(I32, (vall_ref.shape[2] - dh, vall_ref.shape[3]), 0) == 0, 1.0, 0.0)
        for u in range(streams):
            kall_ref[u, 0] = ks_ref[0, u] + epad_ref[...]
            kall_ref[u, 1] = kw_ref[0, u]
            vall_ref[u, 0, 0:dh, :] = vs_ref[0, u]
            vall_ref[u, 1, 0:dh, :] = vw_ref[0, u]
            vall_ref[u, 0, dh:, :] = extra.astype(BF16)
            vall_ref[u, 1, dh:, :] = extra.astype(BF16)

    for u in range(streams):
        for hh in range(NSA_HPG):
            cols = slice(hh * tq, (hh + 1) * tq)
            q = qt_ref[0, u, hh * dh:(hh + 1) * dh, :].astype(F32) * (dh ** -0.5 * LOG2E)
            qaug_ref[u, 0:dh, cols] = q.astype(BF16)
            qaug_ref[u, dh:dh + ns, cols] = mb_ref[0, u, 0]
            qaug_ref[u, dh + ns:, cols] = jnp.zeros((qaug_ref.shape[1] - dh - ns, tq), BF16)

    m_ref[...] = jnp.full(m_ref.shape, NEG, F32)
    acc_ref[...] = jnp.zeros(acc_ref.shape, F32)

    n_back = WINDOW // tk
    n_win = jnp.minimum(i, n_back) + 1
    n_sel = []
    for u in range(streams):
        flag_base = (((pl.program_id(0) * streams + u) * NSA_GROUPS + g) * n_tiles + i) * n_tiles
        cnt = jnp.int32(0)
        for kb_static in range(n_tiles - 1):
            active = (kb_static < i) & (flags_ref[flag_base + kb_static] != 0)
            tiles_ref[u, cnt] = kb_static
            cnt = cnt + active.astype(I32)
        tiles_ref[u, cnt] = i
        n_sel.append(cnt + 1)
    n_steps = n_sel[0] + n_win
    for u in range(1, streams):
        n_steps = jnp.maximum(n_steps, n_sel[u] + n_win)

    def describe(u, n):
        n = jnp.maximum(n, 0)
        is_win = n >= n_sel[u]
        kb_sel = tiles_ref[u, jnp.minimum(n, n_sel[u] - 1)]
        kb = jnp.clip(jnp.where(is_win, i - n_win + 1 + (n - n_sel[u]), kb_sel), 0, i)
        mode = jnp.where(kb == i, 1, jnp.where(is_win & (kb == i - n_back), 2, 0))
        mode = jnp.where(n >= n_sel[u] + n_win, 3, mode)
        return is_win.astype(I32), kb, mode

    def scores(u, n, slot):
        br, kb, _ = describe(u, n)
        s0 = pl.multiple_of(kb * tk, tk)
        s_ref[u, slot] = _dot(kall_ref[u, br, pl.ds(s0, tk), :], qaug_ref[u])

    def softmax(u, n, slot):
        br, kb, mode = describe(u, n)
        crow = srow_ref[...] * ((i - kb) * tk).astype(F32)
        s = s_ref[u, slot] - bias_ref[mode]
        m_old = m_ref[u, br]
        m_new = jnp.maximum(m_old, jnp.max(s, axis=0, keepdims=True) - crow)
        alpha = jnp.exp2(m_old - m_new)
        p = jnp.exp2(s - (m_new + crow))
        m_ref[u, br] = m_new
        alpha_ref[u, slot] = alpha
        p_ref[u, slot] = p.astype(BF16)

    def weighted_values(u, n, slot):
        br, kb, _ = describe(u, n)
        s0 = pl.multiple_of(kb * tk, tk)
        acc_ref[u, br] = (alpha_ref[u, slot] * acc_ref[u, br]
                          + _dot(vall_ref[u, br, :, pl.ds(s0, tk)], p_ref[u, slot]))

    def each(fn, n, slot):
        for u in range(streams):
            fn(u, n, slot)

    for u in range(streams):
        p_ref[u, 1] = jnp.zeros(p_ref.shape[2:], BF16)
        alpha_ref[u, 1] = jnp.ones(alpha_ref.shape[2:], F32)
    each(scores, 0, 0)

    def pair(j, carry):
        n = 2 * j
        each(scores, n + 1, 1)
        each(softmax, n, 0)
        each(weighted_values, n - 1, 1)
        each(scores, n + 2, 0)
        each(softmax, n + 1, 1)
        each(weighted_values, n, 0)
        return carry

    n_pairs = n_steps // 2
    lax.fori_loop(0, n_pairs, pair, 0)
    each(weighted_values, 2 * n_pairs - 1, 1)

    @pl.when(n_steps % 2 == 1)
    def _():
        each(softmax, n_steps - 1, 0)
        each(weighted_values, n_steps - 1, 0)

    for u in range(streams):
        def gate_row(branch, u=u):
            rows = [gt_ref[0, u, 0, 3 * hh + branch:3 * hh + branch + 1, :] for hh in range(NSA_HPG)]
            return jax.nn.sigmoid(jnp.concatenate(rows, axis=1))

        o = (acc_ref[u, 0, 0:dh, :] * (gate_row(1) / acc_ref[u, 0, dh:dh + 1, :])
             + acc_ref[u, 1, 0:dh, :] * (gate_row(2) / acc_ref[u, 1, dh:dh + 1, :]))
        ocmp = jnp.concatenate([ocmp_ref[0, u, hh * dh:(hh + 1) * dh, :] for hh in range(NSA_HPG)],
                               axis=1).astype(F32)
        o = o + gate_row(0) * ocmp
        o_heads = jnp.concatenate([o[:, hh * tq:(hh + 1) * tq] for hh in range(NSA_HPG)], axis=0)
        o_ref[0, u] = o_heads.T.astype(o_ref.dtype)


def _sel_win(slopes, tile_flags, h_t, h_big, epad, mb, ocmp_t, gates_t, B, T, tq, streams):
    ns = T // SEL_LEN
    kaug = LANES
    G = NSA_GROUPS
    grp_rows = NSA_HPG * NSA_DH
    nq = T // tq
    nbg = B // streams
    wide = NSA_HPG * tq
    h_t5 = h_t.reshape(nbg, streams, h_t.shape[1], T)
    h_big5 = h_big.reshape(nbg, streams, T, h_big.shape[1])
    mb5 = mb.reshape(nbg, streams, G, ns, T)
    ocmp5 = ocmp_t.reshape(nbg, streams, NSA_Q, T)
    gates5 = gates_t.reshape(nbg, streams, G, 16, T)
    grid_spec = pltpu.PrefetchScalarGridSpec(
        num_scalar_prefetch=2,
        grid=(nbg, G, nq),
        in_specs=[
            pl.BlockSpec((1, streams, grp_rows, tq), lambda b, g, i, s, f: (b, 0, TROW_Q // grp_rows + g, i)),
            pl.BlockSpec((1, streams, T, LANES), lambda b, g, i, s, f: (b, 0, 0, COL_KS // LANES + g)),
            pl.BlockSpec((1, streams, T, LANES), lambda b, g, i, s, f: (b, 0, 0, COL_KW // LANES + g)),
            pl.BlockSpec((1, streams, NSA_DH, T), lambda b, g, i, s, f: (b, 0, TROW_VS // NSA_DH + g, 0)),
            pl.BlockSpec((1, streams, NSA_DH, T), lambda b, g, i, s, f: (b, 0, TROW_VW // NSA_DH + g, 0)),
            pl.BlockSpec((T, LANES), lambda b, g, i, s, f: (0, 0)),
            pl.BlockSpec((1, streams, 1, ns, tq), lambda b, g, i, s, f: (b, 0, g, 0, i)),
            pl.BlockSpec((1, streams, grp_rows, tq), lambda b, g, i, s, f: (b, 0, g, i)),
            pl.BlockSpec((1, streams, 1, 16, tq), lambda b, g, i, s, f: (b, 0, g, 0, i)),
        ],
        out_specs=pl.BlockSpec((1, streams, tq, grp_rows), lambda b, g, i, s, f: (b, 0, i, g)),
        scratch_shapes=[
            pltpu.VMEM((streams, kaug, wide), BF16),
            pltpu.VMEM((streams, 2, 1, wide), F32),
            pltpu.VMEM((streams, 2, NSA_DH + 16, wide), F32),
            pltpu.VMEM((1, wide), F32),
            pltpu.VMEM((4, tq, wide), F32),
            pltpu.VMEM((streams, 2, tq, wide), F32),
            pltpu.VMEM((streams, 2, tq, wide), BF16),
            pltpu.VMEM((streams, 2, 1, wide), F32),
            pltpu.VMEM((streams, 2, T, kaug), BF16),
            pltpu.VMEM((streams, 2, NSA_DH + 16, T), BF16),
            pltpu.SMEM((streams, nq), I32),
        ],
    )
    out = pl.pallas_call(
        functools.partial(_sel_win_kernel, n_tiles=nq),
        grid_spec=grid_spec,
        out_shape=jax.ShapeDtypeStruct((nbg, streams, T, NSA_Q), BF16),
        compiler_params=_cp(("parallel", "parallel", "arbitrary")),
        name="nsa_sel_win",
    )(slopes, tile_flags, h_t5, h_big5, h_big5, h_t5, h_t5, epad, mb5, ocmp5, gates5)
    return out.reshape(B * T, NSA_Q)


def _mix_kernel(og_ref, on_ref, mg1_ref, mg2_ref, x_ref, wg_ref, wn_ref, wo_ref, lg_ref, lb_ref,
                x1_ref, x1b_ref):
    g1 = _dot(og_ref[...], wg_ref[...])
    g2 = _dot(on_ref[...], wn_ref[...])
    merged = (jax.nn.sigmoid(mg1_ref[...].astype(F32)) * g1
              + jax.nn.sigmoid(mg2_ref[...].astype(F32)) * g2)
    y = _dot(merged.astype(BF16), wo_ref[...])
    x1 = _layer_norm(DN_ALPHA * x_ref[...] + y, lg_ref[...], lb_ref[...])
    x1_ref[...] = x1
    x1b_ref[...] = x1.astype(BF16)


def _const_spec(shape):
    nd = len(shape)
    return pl.BlockSpec(shape, lambda *_: (0,) * nd, pipeline_mode=pl.Buffered(1))


def _mix(o_gla, o_nsa, h_big, x, wg, wn, wo, lg, lb, tm):
    n, d = x.shape
    return pl.pallas_call(
        _mix_kernel,
        grid=(n // tm,),
        in_specs=[
            pl.BlockSpec((tm, GLA_V), lambda i: (i, 0)),
            pl.BlockSpec((tm, NSA_Q), lambda i: (i, 0)),
            pl.BlockSpec((tm, d), lambda i: (i, 0)),
            pl.BlockSpec((tm, d), lambda i: (i, 1)),
            pl.BlockSpec((tm, d), lambda i: (i, 0)),
            _const_spec(wg.shape), _const_spec(wn.shape), _const_spec(wo.shape),
            _const_spec(lg.shape), _const_spec(lb.shape),
        ],
        out_specs=[pl.BlockSpec((tm, d), lambda i: (i, 0)), pl.BlockSpec((tm, d), lambda i: (i, 0))],
        out_shape=[jax.ShapeDtypeStruct((n, d), F32), jax.ShapeDtypeStruct((n, d), BF16)],
        compiler_params=_cp(("parallel",)),
        name="mix_ln",
    )(o_gla, o_nsa, h_big, h_big, x, wg, wn, wo, lg, lb)


def _xattn_kernel(x_ref, xb_ref, kv_ref, wq_ref, wo_ref, lg_ref, lb_ref, x2_ref):
    hd = XA_HEADS * XA_DH
    q = (_dot(xb_ref[...], wq_ref[...]) * (XA_DH ** -0.5)).astype(BF16)
    outs = []
    for h in range(XA_HEADS):
        kh = kv_ref[0, :, h * XA_DH:(h + 1) * XA_DH]
        vh = kv_ref[0, :, hd + h * XA_DH:hd + (h + 1) * XA_DH]
        s = _dot_nt(q[:, h * XA_DH:(h + 1) * XA_DH], kh)
        e = jnp.exp(s - jnp.max(s, axis=-1, keepdims=True))
        p = e / jnp.sum(e, axis=-1, keepdims=True)
        outs.append(_dot(p.astype(BF16), vh).astype(BF16))
    o = jnp.concatenate(outs, axis=-1)
    y = _dot(o, wo_ref[...])
    x2_ref[...] = _layer_norm(DN_ALPHA * x_ref[...] + y, lg_ref[...], lb_ref[...])


def _xattn(x1, x1b, kv, wq, wo, lg, lb, B, T, tm):
    n, d = x1.shape
    nt = T // tm
    return pl.pallas_call(
        _xattn_kernel,
        grid=(B, nt),
        in_specs=[
            pl.BlockSpec((tm, d), lambda b, i: (b * nt + i, 0)),
            pl.BlockSpec((tm, d), lambda b, i: (b * nt + i, 0)),
            pl.BlockSpec((1,) + kv.shape[1:], lambda b, i: (b, 0, 0)),
            _const_spec(wq.shape), _const_spec(wo.shape), _const_spec(lg.shape), _const_spec(lb.shape),
        ],
        out_specs=pl.BlockSpec((tm, d), lambda b, i: (b * nt + i, 0)),
        out_shape=jax.ShapeDtypeStruct((n, d), F32),
        compiler_params=_cp(("parallel", "parallel")),
        name="xattn_ln",
    )(x1, x1b, kv, wq, wo, lg, lb)


def _router_kernel(x_ref, wh_ref, wl_ref, rb_ref, e_ref, gate_ref, rank_ref, cnt_ref, carry_ref):
    i = pl.program_id(0)
    tr = x_ref.shape[0]
    E = N_EXPERTS

    @pl.when(i == 0)
    def _():
        carry_ref[...] = jnp.zeros_like(carry_ref)

    x = x_ref[...]
    x_hi = x.astype(BF16)
    x_lo = (x - x_hi.astype(F32)).astype(BF16)
    wh = wh_ref[...]
    logits = _dot_nt(wh, x_hi) + _dot_nt(wh, x_lo) + _dot_nt(wl_ref[...], x_hi)
    biased = logits + rb_ref[...]
    rows = [biased[e:e + 1, :] for e in range(E)]
    raw = [logits[e:e + 1, :] for e in range(E)]
    best_score = None
    best = None
    for gi in range(N_GROUPS):
        v = rows[gi * EXPERTS_PER_GROUP:(gi + 1) * EXPERTS_PER_GROUP]
        sc = None
        for a in range(EXPERTS_PER_GROUP):
            for b in range(a + 1, EXPERTS_PER_GROUP):
                pair = v[a] + v[b]
                sc = pair if sc is None else jnp.maximum(sc, pair)
        if best is None:
            best_score, best = sc, jnp.zeros((1, tr), I32)
        else:
            better = sc > best_score
            best_score = jnp.where(better, sc, best_score)
            best = jnp.where(better, gi, best)

    def pick(vals):
        out = vals[0:EXPERTS_PER_GROUP]
        for gi in range(1, N_GROUPS):
            out = [jnp.where(best == gi, vals[gi * EXPERTS_PER_GROUP + a], out[a]) for a in range(EXPERTS_PER_GROUP)]
        return out

    w = pick(rows)
    lraw = pick(raw)
    i1 = jnp.zeros((1, tr), I32)
    v1 = w[0]
    l1 = lraw[0]
    for a in range(1, EXPERTS_PER_GROUP):
        better = w[a] > v1
        v1 = jnp.where(better, w[a], v1)
        l1 = jnp.where(better, lraw[a], l1)
        i1 = jnp.where(better, a, i1)
    i2 = jnp.full((1, tr), -1, I32)
    v2 = jnp.full((1, tr), -jnp.inf, F32)
    l2 = jnp.zeros((1, tr), F32)
    for a in range(EXPERTS_PER_GROUP):
        better = (i1 != a) & ((w[a] > v2) | (i2 < 0))
        v2 = jnp.where(better, w[a], v2)
        l2 = jnp.where(better, lraw[a], l2)
        i2 = jnp.where(better, a, i2)
    e1 = best * EXPERTS_PER_GROUP + i1
    e2 = best * EXPERTS_PER_GROUP + i2
    mx = jnp.maximum(l1, l2)
    p1 = jnp.exp(l1 - mx)
    p2 = jnp.exp(l2 - mx)
    den = p1 + p2
    e_ref[0:1, :] = e1
    e_ref[1:2, :] = e2
    gate_ref[0:1, :] = p1 / den
    gate_ref[1:2, :] = p2 / den
    eidx = lax.broadcasted_iota(I32, (E, tr), 0)
    is1 = eidx == e1
    is2 = eidx == e2
    member = jnp.where(is1 | is2, 1.0, 0.0)
    uu = lax.broadcasted_iota(I32, (tr, tr), 0)
    tt = lax.broadcasted_iota(I32, (tr, tr), 1)
    tri = jnp.where(uu <= tt, 1.0, 0.0).astype(BF16)
    incl = _dot(member.astype(BF16), tri)
    excl = carry_ref[:, 0:1] + incl - member
    rank_ref[0:1, :] = jnp.sum(jnp.where(is1, excl, 0.0), axis=0, keepdims=True).astype(I32)
    rank_ref[1:2, :] = jnp.sum(jnp.where(is2, excl, 0.0), axis=0, keepdims=True).astype(I32)
    new_carry = carry_ref[...] + jnp.sum(member, axis=1, keepdims=True)
    carry_ref[...] = new_carry
    cnt_ref[...] = new_carry


def _router(x2, rw_hi, rw_lo, rb, tr):
    n, d = x2.shape
    E = N_EXPERTS
    return pl.pallas_call(
        _router_kernel,
        grid=(n // tr,),
        in_specs=[
            pl.BlockSpec((tr, d), lambda i: (i, 0)),
            pl.BlockSpec((E, d), lambda i: (0, 0)),
            pl.BlockSpec((E, d), lambda i: (0, 0)),
            pl.BlockSpec((E, 1), lambda i: (0, 0)),
        ],
        out_specs=[
            pl.BlockSpec((2, tr), lambda i: (0, i)),
            pl.BlockSpec((2, tr), lambda i: (0, i)),
            pl.BlockSpec((2, tr), lambda i: (0, i)),
            pl.BlockSpec((E, LANES), lambda i: (0, 0)),
        ],
        out_shape=[jax.ShapeDtypeStruct((2, n), I32), jax.ShapeDtypeStruct((2, n), F32),
                   jax.ShapeDtypeStruct((2, n), I32), jax.ShapeDtypeStruct((E, LANES), F32)],
        scratch_shapes=[pltpu.VMEM((E, LANES), F32)],
        compiler_params=_cp(("arbitrary",)),
        name="moe_router",
    )(x2, rw_hi, rw_lo, rb)


def _slot_kernel(ps_ref, e_ref, rank_ref, slot_ref):
    e = e_ref[...]
    start = jnp.zeros(e.shape, I32)
    for ex in range(N_EXPERTS):
        start = jnp.where(e == ex, ps_ref[ex], start)
    slot_ref[...] = start + rank_ref[...]


def _slots(pad_start, e, rank, ts):
    n = e.shape[1]
    grid_spec = pltpu.PrefetchScalarGridSpec(
        num_scalar_prefetch=1,
        grid=(n // ts,),
        in_specs=[pl.BlockSpec((TOP_K, ts), lambda i, s: (0, i)), pl.BlockSpec((TOP_K, ts), lambda i, s: (0, i))],
        out_specs=pl.BlockSpec((TOP_K, ts), lambda i, s: (0, i)),
    )
    return pl.pallas_call(
        _slot_kernel,
        grid_spec=grid_spec,
        out_shape=jax.ShapeDtypeStruct((TOP_K, n), I32),
        compiler_params=_cp(("parallel",)),
        name="moe_slots",
    )(pad_start, e, rank)


def _dispatch_kernel(pe_ref, s0_ref, s1_ref, x_ref, buf_hbm, zero_ref, sem):
    td = s0_ref.shape[0]
    slots = (s0_ref, s1_ref)

    @pl.when(pl.program_id(0) == 0)
    def _():
        zero_ref[...] = jnp.zeros_like(zero_ref)

        def zero_copy(ex):
            last = pl.multiple_of(jnp.maximum(pe_ref[ex] - MOE_BLOCK, 0), MOE_BLOCK)
            return pltpu.make_async_copy(zero_ref, buf_hbm.at[pl.ds(last, MOE_BLOCK), :], sem)

        def nonempty(ex):
            return pe_ref[ex] > (pe_ref[ex - 1] if ex > 0 else 0)

        n_blocks = buf_hbm.shape[0] // MOE_BLOCK
        first_unused = pe_ref[N_EXPERTS - 1] // MOE_BLOCK

        def tail_copy(k):
            row = pl.multiple_of((first_unused + k) * MOE_BLOCK, MOE_BLOCK)
            return pltpu.make_async_copy(zero_ref, buf_hbm.at[pl.ds(row, MOE_BLOCK), :], sem)

        for ex in range(N_EXPERTS):
            pl.when(nonempty(ex))(lambda ex=ex: zero_copy(ex).start())
            pl.when(first_unused + ex < n_blocks)(lambda ex=ex: tail_copy(ex).start())
        for ex in range(N_EXPERTS):
            pl.when(nonempty(ex))(lambda ex=ex: zero_copy(ex).wait())
            pl.when(first_unused + ex < n_blocks)(lambda ex=ex: tail_copy(ex).wait())

    def issue(t, carry):
        for kk in range(TOP_K):
            dest = slots[kk][t]
            pltpu.make_async_copy(x_ref.at[pl.ds(t, 1), :], buf_hbm.at[pl.ds(dest, 1), :], sem).start()
        return carry

    lax.fori_loop(0, td, issue, 0, unroll=8)
    for kk in range(TOP_K):
        pltpu.make_async_copy(x_ref, buf_hbm.at[pl.ds(0, td), :], sem).wait()


def _dispatch(pad_end, slot0, slot1, x2, n_rows, td):
    n, d = x2.shape
    grid_spec = pltpu.PrefetchScalarGridSpec(
        num_scalar_prefetch=1,
        grid=(n // td,),
        in_specs=[
            pl.BlockSpec((td,), lambda i, s: (i,), memory_space=pltpu.SMEM),
            pl.BlockSpec((td,), lambda i, s: (i,), memory_space=pltpu.SMEM),
            pl.BlockSpec((td, d), lambda i, s: (i, 0)),
        ],
        out_specs=pl.BlockSpec(memory_space=pl.ANY),
        scratch_shapes=[pltpu.VMEM((MOE_BLOCK, d), F32), pltpu.SemaphoreType.DMA(())],
    )
    return pl.pallas_call(
        _dispatch_kernel,
        grid_spec=grid_spec,
        out_shape=jax.ShapeDtypeStruct((n_rows, d), F32),
        compiler_params=_cp(("arbitrary",)),
        name="moe_dispatch",
    )(pad_end, slot0, slot1, x2)


def _expert_kernel(be_ref, nb_ref, x_ref, win_hbm, wdn_hbm, y_ref, xb_ref, wa_s, wu_s, wd_s, sa, su, sd, sems,
                   *, layer):
    b = pl.program_id(0)
    nf = D_FF // FF_TILE
    n_used = nb_ref[0]
    e = be_ref[b]
    e_prev = be_ref[jnp.maximum(b - 1, 0)]
    e_next = be_ref[jnp.minimum(b + 1, pl.num_programs(0) - 1)]
    active = b < n_used
    is_first = active & ((b == 0) | (e_prev != e))
    feeds_next = active & (b + 1 < n_used) & (e_next != e)

    def tile_copies(ex, f):
        lo = f * FF_TILE
        return (pltpu.make_async_copy(win_hbm.at[layer, ex, :, pl.ds(lo, FF_TILE)], sa, sems.at[0]),
                pltpu.make_async_copy(win_hbm.at[layer, ex, :, pl.ds(D_FF + lo, FF_TILE)], su, sems.at[1]),
                pltpu.make_async_copy(wdn_hbm.at[layer, ex, pl.ds(lo, FF_TILE), :], sd, sems.at[2]))

    def start(ex, f):
        for c in tile_copies(ex, f):
            c.start()

    def finish(ex, f):
        for c in tile_copies(ex, f):
            c.wait()
        wa_s[f] = sa[...].astype(BF16)
        wu_s[f] = su[...].astype(BF16)
        wd_s[f] = sd[...].astype(BF16)

    @pl.when(b == 0)
    def _():
        for f in range(nf - 1):
            start(e, f)
            finish(e, f)
        start(e, nf - 1)

    @pl.when(jnp.logical_not(active))
    def _():
        y_ref[...] = jnp.zeros_like(y_ref)

    @pl.when(active)
    def _():
        xb_ref[...] = x_ref[...].astype(BF16)
        for f in range(nf):
            xb = xb_ref[...]
            a = _dot(xb, wa_s[f])
            u = _dot(xb, wu_s[f])
            act = (a * jax.nn.sigmoid(a) * u).astype(BF16)
            y = _dot(act, wd_s[f])
            if f == 0:
                y_ref[...] = y
                pl.when(is_first)(lambda: finish(e, nf - 1))
            else:
                y_ref[...] += y

            @pl.when(feeds_next)
            def _(f=f):
                if f >= 1:
                    finish(e_next, f - 1)
                start(e_next, f)


def _experts(blk_expert, n_used, buf, w_in, w_down, layer):
    p, d = buf.shape
    nb = p // MOE_BLOCK
    nf = D_FF // FF_TILE
    grid_spec = pltpu.PrefetchScalarGridSpec(
        num_scalar_prefetch=2,
        grid=(nb,),
        in_specs=[
            pl.BlockSpec((MOE_BLOCK, d), lambda b, be, nu: (jnp.minimum(b, nu[0] - 1), 0)),
            pl.BlockSpec(memory_space=pl.ANY),
            pl.BlockSpec(memory_space=pl.ANY),
        ],
        out_specs=pl.BlockSpec((MOE_BLOCK, d), lambda b, be, nu: (b, 0)),
        scratch_shapes=[
            pltpu.VMEM((MOE_BLOCK, d), BF16),
            pltpu.VMEM((nf, d, FF_TILE), BF16),
            pltpu.VMEM((nf, d, FF_TILE), BF16),
            pltpu.VMEM((nf, FF_TILE, d), BF16),
            pltpu.VMEM((d, FF_TILE), F32),
            pltpu.VMEM((d, FF_TILE), F32),
            pltpu.VMEM((FF_TILE, d), F32),
            pltpu.SemaphoreType.DMA((3,)),
        ],
    )
    return pl.pallas_call(
        functools.partial(_expert_kernel, layer=layer),
        grid_spec=grid_spec,
        out_shape=jax.ShapeDtypeStruct((p, d), F32),
        compiler_params=_cp(("arbitrary",)),
        name="moe_experts",
    )(blk_expert, n_used, buf, w_in, w_down)


def _combine_kernel(s0_ref, s1_ref, n0_ref, n1_ref, y_hbm, x_ref, gate_ref, lg_ref, lb_ref, x3_ref, x3b_ref,
                    y0_ref, y1_ref, sems):
    i = pl.program_id(0)
    tc = x_ref.shape[0]
    bufs = (y0_ref, y1_ref)
    cur = i % 2

    def issue_tile(slot_refs, half):
        def issue(t, carry):
            for kk in range(TOP_K):
                src = slot_refs[kk][t]
                pltpu.make_async_copy(y_hbm.at[pl.ds(src, 1), :], bufs[kk].at[half, pl.ds(t, 1), :],
                                      sems.at[half]).start()
            return carry

        lax.fori_loop(0, tc, issue, 0, unroll=8)

    @pl.when(i == 0)
    def _():
        issue_tile((s0_ref, s1_ref), 0)

    @pl.when(i + 1 < pl.num_programs(0))
    def _():
        issue_tile((n0_ref, n1_ref), 1 - cur)

    for kk in range(TOP_K):
        pltpu.make_async_copy(y_hbm.at[pl.ds(0, tc), :], bufs[kk].at[cur], sems.at[cur]).wait()
    gate = gate_ref[...]
    z = DN_ALPHA * x_ref[...] + gate[:, 0:1] * y0_ref[cur] + gate[:, 1:2] * y1_ref[cur]
    x3 = _layer_norm(z, lg_ref[...], lb_ref[...])
    x3_ref[...] = x3
    x3b_ref[...] = x3.astype(BF16)


def _combine(slot0, slot1, y, x2, gate_nt, lg, lb, tc):
    n, d = x2.shape
    last = n // tc - 1
    return pl.pallas_call(
        _combine_kernel,
        grid=(n // tc,),
        in_specs=[
            pl.BlockSpec((tc,), lambda i: (i,), memory_space=pltpu.SMEM),
            pl.BlockSpec((tc,), lambda i: (i,), memory_space=pltpu.SMEM),
            pl.BlockSpec((tc,), lambda i: (jnp.minimum(i + 1, last),), memory_space=pltpu.SMEM),
            pl.BlockSpec((tc,), lambda i: (jnp.minimum(i + 1, last),), memory_space=pltpu.SMEM),
            pl.BlockSpec(memory_space=pl.ANY),
            pl.BlockSpec((tc, d), lambda i: (i, 0)),
            pl.BlockSpec((tc, 2), lambda i: (i, 0)),
            pl.BlockSpec((1, d), lambda i: (0, 0)),
            pl.BlockSpec((1, d), lambda i: (0, 0)),
        ],
        out_specs=[pl.BlockSpec((tc, d), lambda i: (i, 0)), pl.BlockSpec((tc, d), lambda i: (i, 0))],
        out_shape=[jax.ShapeDtypeStruct((n, d), F32), jax.ShapeDtypeStruct((n, d), BF16)],
        scratch_shapes=[pltpu.VMEM((2, tc, d), F32), pltpu.VMEM((2, tc, d), F32), pltpu.SemaphoreType.DMA((2,))],
        compiler_params=_cp(("arbitrary",)),
        name="moe_combine_ln",
    )(slot0, slot1, slot0, slot1, y, x2, gate_nt, lg, lb)


def _layer(x, xb, mem_b, p, moe_w, layer, consts, B, T):
    n, d = x.shape
    G, HPG, DH = NSA_GROUPS, NSA_HPG, NSA_DH
    slopes, ovt, epad, tile_ind = consts

    h_big = _matmul(xb, p["w_big"], BF16, 1024, 1024)
    h_small = _matmul(xb, p["w_small"], F32, 1024, SCOL_END)
    h_t = _matmul_t(xb, p["w_t"], B, T, 1024, TROW_END // 2)

    o_gla = _gla(h_big, h_small, p["wa_pad"], p["b_a"], p["norm_g"], B, T, 4 if B % 4 == 0 else 1, 512)

    kcmp, kcmp_t = _compress(h_small, p["cmp_w1bd"], p["cmp_w2bd"], p["cmp_w2bdt"], p["cmp_pe_pair"], B, T)
    tq_sel = 256
    ocmp_t, mb, in_tile = _cmp_select(slopes, h_t, kcmp, kcmp_t, ovt, tile_ind, B, T, 512, 4 if B % 4 == 0 else 1)
    nq = T // tq_sel
    tile_flags = (in_tile.reshape(B, G, nq, nq, tq_sel).max(axis=-1) > 0).astype(I32)
    tile_flags = tile_flags.transpose(0, 1, 3, 2).reshape(-1)
    gates_t = h_small[:, GLA_GATE_RANK:GLA_GATE_RANK + 3 * NSA_HEADS].reshape(B, T, G, 3 * HPG)
    gates_t = jnp.pad(gates_t.transpose(0, 2, 3, 1), ((0, 0), (0, 0), (0, 16 - 3 * HPG), (0, 0)))
    o_nsa = _sel_win(slopes, tile_flags, h_t, h_big, epad, mb, ocmp_t, gates_t, B, T, tq_sel,
                     4 if B % 4 == 0 else 1)

    x1, x1b = _mix(o_gla, o_nsa, h_big, x, p["w_bg"], p["w_bn"], p["w_out"], p["ln_mix_g"], p["ln_mix_b"], 512)

    kvm = _matmul(mem_b, p["xa_wkv"], BF16, 512, 512).reshape(B, MEM_LEN, 2 * XA_HEADS * XA_DH)
    x2 = _xattn(x1, x1b, kvm, p["xa_wq"], p["xa_wo"], p["ln_xa_g"], p["ln_xa_b"], B, T, 512)

    e, gate, rank, cnt = _router(x2, p["rw_hi"], p["rw_lo"], p["rb"], 512)
    counts = cnt[:, 0].astype(I32)
    padded = (counts + MOE_BLOCK - 1) // MOE_BLOCK * MOE_BLOCK
    pad_end = jnp.cumsum(padded)
    pad_start = (pad_end - padded).astype(I32)
    nb = (n * TOP_K) // MOE_BLOCK + N_EXPERTS
    n_used = (pad_end[-1] // MOE_BLOCK).astype(I32).reshape(1)
    blk_start = jnp.arange(nb, dtype=I32) * MOE_BLOCK
    blk_expert = jnp.minimum(jnp.sum(blk_start[:, None] >= pad_end[None, :], axis=1), N_EXPERTS - 1).astype(I32)
    blk_expert = jnp.where(jnp.arange(nb) < n_used[0], blk_expert, blk_expert[jnp.maximum(n_used[0] - 1, 0)])
    slot = _slots(pad_start, e, rank, 2048)
    buf = _dispatch(pad_end.astype(I32), slot[0], slot[1], x2, nb * MOE_BLOCK, 512)
    y = _experts(blk_expert, n_used, buf, moe_w[0], moe_w[1], layer)
    x3, x3b = _combine(slot[0], slot[1], y, x2, gate.T, p["ln_ffn_g"], p["ln_ffn_b"], 256)
    return x3, x3b


def _prep_layer(l, w_in, gla_w_a2, gla_b_a, gla_norm_g, nsa_cmp_pe, nsa_cmp_w1, nsa_cmp_w2, w_branch_gla,
                w_branch_nsa, w_out, ln_mix_g, ln_mix_b, xa_wq, xa_wkv, xa_wo, ln_xa_g, ln_xa_b, router_w,
                router_b, moe_w_in, moe_w_down, ln_ffn_g, ln_ffn_b):
    d = w_in.shape[1]
    w = w_in[l]
    o_gq, o_gk, o_gv, o_gr = 0, GLA_QK, 2 * GLA_QK, 2 * GLA_QK + GLA_V
    o_ga = o_gr + GLA_V
    o_nq = o_ga + GLA_GATE_RANK
    o_nkv = o_nq + NSA_Q
    o_ng = o_nkv + 6 * NSA_KV
    o_mg = o_ng + 3 * NSA_HEADS
    G, DH = NSA_GROUPS, NSA_DH

    def kv_cols(kind):
        return w[:, o_nkv + kind * NSA_KV:o_nkv + (kind + 1) * NSA_KV]

    def slabs(wk):
        return jnp.pad(wk.reshape(d, G, DH), ((0, 0), (0, 0), (0, LANES - DH))).reshape(d, G * LANES)

    w_big = jnp.concatenate([w[:, o_mg:o_mg + 2 * d], w[:, o_gq:o_ga], slabs(kv_cols(2)), slabs(kv_cols(4))],
                            axis=1).astype(BF16)
    w_small = jnp.concatenate([w[:, o_ga:o_nq], w[:, o_ng:o_mg],
                               jnp.zeros((d, LANES - GLA_GATE_RANK - 3 * NSA_HEADS), F32),
                               kv_cols(0), kv_cols(1)], axis=1).astype(BF16)
    w_t = jnp.concatenate([w[:, o_nq:o_nkv], kv_cols(3), kv_cols(5)], axis=1).T.astype(BF16)
    w1 = nsa_cmp_w1[l].reshape(2, CMP_LEN, DH, CMP_HIDDEN)
    z1 = jnp.zeros_like(w1)
    w1bd = jnp.concatenate([jnp.concatenate([w1, z1], axis=3), jnp.concatenate([z1, w1], axis=3)], axis=2)
    w2 = nsa_cmp_w2[l]
    z2 = jnp.zeros_like(w2)
    w2bd = jnp.concatenate([jnp.concatenate([w2, z2], axis=2), jnp.concatenate([z2, w2], axis=2)], axis=1)
    pe = nsa_cmp_pe[l]
    pe_pair = jnp.broadcast_to(jnp.concatenate([pe, pe], axis=-1)[:, :, None, :], (2, CMP_LEN, 16, 2 * DH))
    wa_pad = jnp.concatenate([gla_w_a2[l], jnp.zeros((LANES - GLA_GATE_RANK, GLA_QK), F32)], axis=0).astype(BF16)
    rw_t = router_w.T
    rw_hi = rw_t.astype(BF16)
    rw_lo = (rw_t - rw_hi.astype(F32)).astype(BF16)
    return dict(
        w_big=w_big, w_small=w_small, w_t=w_t, wa_pad=wa_pad,
        b_a=gla_b_a[l].reshape(1, -1), norm_g=gla_norm_g[l].reshape(1, -1),
        cmp_w1bd=w1bd.astype(BF16), cmp_w2bd=w2bd.astype(BF16), cmp_w2bdt=w2bd.transpose(0, 2, 1).astype(BF16),
        cmp_pe_pair=pe_pair.astype(BF16),
        w_bg=w_branch_gla[l].astype(BF16), w_bn=w_branch_nsa[l].astype(BF16), w_out=w_out[l].astype(BF16),
        ln_mix_g=ln_mix_g[l].reshape(1, -1), ln_mix_b=ln_mix_b[l].reshape(1, -1),
        xa_wq=xa_wq[l].astype(BF16), xa_wkv=xa_wkv[l].astype(BF16), xa_wo=xa_wo[l].astype(BF16),
        ln_xa_g=ln_xa_g[l].reshape(1, -1), ln_xa_b=ln_xa_b[l].reshape(1, -1),
        rw_hi=rw_hi, rw_lo=rw_lo, rb=router_b.reshape(-1, 1),
        ln_ffn_g=ln_ffn_g[l].reshape(1, -1), ln_ffn_b=ln_ffn_b[l].reshape(1, -1),
    )


def kernel(x, mem, w_in, gla_w_a2, gla_b_a, gla_norm_g, nsa_cmp_pe, nsa_cmp_w1, nsa_cmp_w2, w_branch_gla, w_branch_nsa, w_out, ln_mix_g, ln_mix_b, xa_wq, xa_wkv, xa_wo, ln_xa_g, ln_xa_b, router_w, router_b, moe_w_in, moe_w_down, ln_ffn_g, ln_ffn_b):
    B, T, d = x.shape
    assert T % 512 == 0 and d == 2048 and mem.shape[1] == MEM_LEN
    n = B * T
    params = (w_in, gla_w_a2, gla_b_a, gla_norm_g, nsa_cmp_pe, nsa_cmp_w1, nsa_cmp_w2, w_branch_gla, w_branch_nsa,
              w_out, ln_mix_g, ln_mix_b, xa_wq, xa_wkv, xa_wo, ln_xa_g, ln_xa_b, router_w, router_b, moe_w_in,
              moe_w_down, ln_ffn_g, ln_ffn_b)
    slopes = (2.0 ** (-8.0 * jnp.arange(1, NSA_HEADS + 1, dtype=F32) / NSA_HEADS)).astype(F32)
    nc, ns = T // CMP_STRIDE, T // SEL_LEN
    cs = np.arange(nc) * CMP_STRIDE
    ss = np.arange(ns) * SEL_LEN
    ovt = ((cs[None, :] < ss[:, None] + SEL_LEN) & (cs[None, :] + CMP_LEN > ss[:, None])
           & (cs[None, :] + CMP_LEN <= T)).astype(np.float32)
    assert NSA_DH + ns <= LANES
    epad = np.zeros((T, LANES), np.float32)
    epad[np.arange(T), NSA_DH + np.arange(T) // SEL_LEN] = 1.0
    tile_ind = (np.arange(ns)[None, :] // (256 // SEL_LEN) == np.arange(T // 256)[:, None]).astype(np.float32)
    consts = (slopes, jnp.asarray(ovt, BF16), jnp.asarray(epad, BF16), jnp.asarray(tile_ind, BF16))

    xf = x.reshape(n, d)
    xb = xf.astype(BF16)
    mem_b = mem.reshape(B * MEM_LEN, d).astype(BF16)
    moe_w = (moe_w_in, moe_w_down)
    for l in range(DEPTH):
        p = _prep_layer(l, *params)
        xf, xb = _layer(xf, xb, mem_b, p, moe_w, l, consts, B, T)
    return xf.reshape(B, T, d)
```

```python
import functools
from typing import NamedTuple

import jax
import jax.numpy as jnp
import numpy as np
from jax import lax
from jax.experimental import pallas as pl
from jax.experimental.pallas import tpu as pltpu

F32 = jnp.float32
BF16 = jnp.bfloat16
I32 = jnp.int32

DEPTH = 2
MEM_LEN = 256
GLA_HEADS = 4
GLA_DK = 128
GLA_DV = 256
GLA_GATE_RANK = 16
GLA_TAU = 16.0
GLA_CHUNK = 64
NSA_HEADS = 16
NSA_GROUPS = 4
NSA_HPG = NSA_HEADS // NSA_GROUPS
NSA_DH = 64
CMP_LEN = 32
CMP_STRIDE = 16
CMP_HIDDEN = 256
SEL_LEN = 64
SEL_TOPN = 8
WINDOW = 512
XA_HEADS = 4
XA_DH = 128
N_EXPERTS = 16
N_GROUPS = 4
EXPERTS_PER_GROUP = N_EXPERTS // N_GROUPS
TOP_K = 2
D_FF = 1536
DN_ALPHA = float((2 * DEPTH) ** 0.25)
LN_EPS = 1e-5
NEG = -1e30
LOG2E = 1.4426950408889634
FORCE_BONUS = 1e6

GLA_QK = GLA_HEADS * GLA_DK
GLA_V = GLA_HEADS * GLA_DV
NSA_Q = NSA_HEADS * NSA_DH
NSA_KV = NSA_GROUPS * NSA_DH

LANES = 128
VMEM_LIMIT = 56 * 1024 * 1024

COL_MG = 0
COL_GQ = 2 * 2048
COL_GK = COL_GQ + GLA_QK
COL_GV = COL_GK + GLA_QK
COL_GR = COL_GV + GLA_V
COL_KS = COL_GR + GLA_V
COL_KW = COL_KS + NSA_GROUPS * LANES
COL_END = COL_KW + NSA_GROUPS * LANES
SCOL_CK = LANES
SCOL_CV = SCOL_CK + NSA_KV
SCOL_END = SCOL_CV + NSA_KV
TROW_Q = 0
TROW_VS = NSA_Q
TROW_VW = TROW_VS + NSA_KV
TROW_END = TROW_VW + NSA_KV

MOE_BLOCK = 512
FF_TILE = 512
GATE_ROWS = 16


class _Tiles(NamedTuple):
    proj_rows: int = 1024
    proj_cols: int = 1024
    seqs: int = 4
    gla_rows: int = 512
    cmp_q: int = 512
    sel_q: int = 256
    mix_rows: int = 512
    xattn_rows: int = 512
    kv_rows: int = 512
    kv_cols: int = 512
    router_rows: int = 512
    slot_cols: int = 2048
    dispatch_rows: int = 512
    combine_rows: int = 256


TILES = _Tiles()


def _cp(sem):
    return pltpu.CompilerParams(dimension_semantics=sem, vmem_limit_bytes=VMEM_LIMIT)


def _dot(a, b):
    return jnp.dot(a, b, preferred_element_type=F32)


def _dot_nt(a, b):
    return lax.dot_general(a, b, (((1,), (1,)), ((), ())), preferred_element_type=F32)


def _dot_tn(a, b):
    return lax.dot_general(a, b, (((0,), (0,)), ((), ())), preferred_element_type=F32)


def _layer_norm(z, g, b):
    mu = jnp.mean(z, axis=-1, keepdims=True)
    zc = z - mu
    var = jnp.mean(zc * zc, axis=-1, keepdims=True)
    return zc * lax.rsqrt(var + LN_EPS) * g + b


def _mm_kernel(a_ref, b_ref, o_ref):
    o_ref[...] = _dot(a_ref[...], b_ref[...]).astype(o_ref.dtype)


def _matmul(a, b, out_dtype, tm, tn):
    m, k = a.shape
    n = b.shape[1]
    return pl.pallas_call(
        _mm_kernel,
        grid=(m // tm, n // tn),
        in_specs=[pl.BlockSpec((tm, k), lambda i, j: (i, 0)),
                  pl.BlockSpec((k, tn), lambda i, j: (0, j))],
        out_specs=pl.BlockSpec((tm, tn), lambda i, j: (i, j)),
        out_shape=jax.ShapeDtypeStruct((m, n), out_dtype),
        compiler_params=_cp(("parallel", "parallel")),
        name="matmul",
    )(a, b)


def _mm_nt_kernel(wt_ref, x_ref, o_ref):
    o_ref[0] = _dot_nt(wt_ref[...], x_ref[...]).astype(o_ref.dtype)


def _matmul_t(x, wt, B, T, tm, tr):
    n, k = x.shape
    r = wt.shape[0]
    nt = T // tm
    return pl.pallas_call(
        _mm_nt_kernel,
        grid=(n // tm, r // tr),
        in_specs=[pl.BlockSpec((tr, k), lambda i, j: (j, 0)),
                  pl.BlockSpec((tm, k), lambda i, j: (i, 0))],
        out_specs=pl.BlockSpec((1, tr, tm), lambda i, j: (i // nt, j, i % nt)),
        out_shape=jax.ShapeDtypeStruct((B, r, T), BF16),
        compiler_params=_cp(("parallel", "parallel")),
        name="matmul_t",
    )(wt, x)


def _gla_kernel(q_ref, k_ref, v_ref, r_ref, sm_ref, wa_ref, ba_ref, ng_ref, o_ref, st_ref):
    C = GLA_CHUNK
    seqs = q_ref.shape[1]
    n_chunks = q_ref.shape[2] // C

    @pl.when(pl.program_id(1) == 0)
    def _():
        st_ref[...] = jnp.zeros_like(st_ref)

    rowi = lax.broadcasted_iota(I32, (C, GLA_DK), 0)
    tt = lax.broadcasted_iota(I32, (C, C), 0)
    ss = lax.broadcasted_iota(I32, (C, C), 1)
    levels = (1, 2, 4, 8, 16, 32)
    pair_masks = [((tt // (2 * L)) == (ss // (2 * L))) & ((tt & L) != 0) & ((ss & L) == 0) for L in levels]
    diag_mask = tt == ss
    scale = GLA_DK ** -0.5

    def head_chunk(sq, rows, h, z):
        qk_cols = slice(h * GLA_DK, (h + 1) * GLA_DK)
        v_cols = slice(h * GLA_DV, (h + 1) * GLA_DV)
        state = sq * GLA_HEADS + h
        q = q_ref[0, sq, rows, qk_cols].astype(F32) * scale
        k = k_ref[0, sq, rows, qk_cols].astype(F32)
        v = v_ref[0, sq, rows, v_cols]
        g = (jnp.minimum(z, 0.0) - jnp.log1p(jnp.exp(-jnp.abs(z)))) * (1.0 / GLA_TAU)
        incl = g
        tot = g
        att = jnp.where(diag_mask, _dot_nt(q.astype(BF16), k.astype(BF16)), 0.0)
        for L, pm in zip(levels, pair_masks):
            ql = (q * jnp.exp(incl)).astype(BF16)
            kl = (k * jnp.exp(tot - incl)).astype(BF16)
            att = jnp.where(pm, _dot_nt(ql, kl), att)
            upper = (rowi & L) != 0
            from_lower = pltpu.roll(tot, L, 0)
            from_upper = pltpu.roll(tot, C - L, 0)
            incl = incl + jnp.where(upper, from_lower, 0.0)
            tot = tot + jnp.where(upper, from_lower, from_upper)
        qd = (q * jnp.exp(incl)).astype(BF16)
        kd = (k * jnp.exp(tot - incl)).astype(BF16)
        st = st_ref[state]
        o = _dot_nt(qd, st.astype(BF16)) + _dot(att.astype(BF16), v)
        st_ref[state] = st * jnp.exp(tot[0:1, :]) + _dot_tn(v, kd)
        mu = jnp.mean(o, axis=-1, keepdims=True)
        oc = o - mu
        var = jnp.mean(oc * oc, axis=-1, keepdims=True)
        on = oc * lax.rsqrt(var + LN_EPS) * ng_ref[:, v_cols]
        r = r_ref[0, sq, rows, v_cols].astype(F32)
        o_ref[0, sq, rows, v_cols] = (on * (r * jax.nn.sigmoid(r))).astype(o_ref.dtype)

    def chunk(c, carry):
        rows = pl.ds(pl.multiple_of(c * C, C), C)
        for sq in range(seqs):
            z = _dot(sm_ref[0, sq, rows, :].astype(BF16), wa_ref[...]) + ba_ref[...]
            for h in range(GLA_HEADS):
                head_chunk(sq, rows, h, z[:, h * GLA_DK:(h + 1) * GLA_DK])
        return carry

    lax.fori_loop(0, n_chunks, chunk, 0)


def _gla(h_big, h_small, wa_pad, b_a, norm_g, B, T, seqs, tb):
    groups = B // seqs
    hb = h_big.reshape(groups, seqs, T, h_big.shape[1])
    hs = h_small.reshape(groups, seqs, T, h_small.shape[1])
    out = pl.pallas_call(
        _gla_kernel,
        grid=(groups, T // tb),
        in_specs=[
            pl.BlockSpec((1, seqs, tb, GLA_QK), lambda b, j: (b, 0, j, COL_GQ // GLA_QK)),
            pl.BlockSpec((1, seqs, tb, GLA_QK), lambda b, j: (b, 0, j, COL_GK // GLA_QK)),
            pl.BlockSpec((1, seqs, tb, GLA_V), lambda b, j: (b, 0, j, COL_GV // GLA_V)),
            pl.BlockSpec((1, seqs, tb, GLA_V), lambda b, j: (b, 0, j, COL_GR // GLA_V)),
            pl.BlockSpec((1, seqs, tb, LANES), lambda b, j: (b, 0, j, 0)),
            pl.BlockSpec((LANES, GLA_QK), lambda b, j: (0, 0)),
            pl.BlockSpec((1, GLA_QK), lambda b, j: (0, 0)),
            pl.BlockSpec((1, GLA_V), lambda b, j: (0, 0)),
        ],
        out_specs=pl.BlockSpec((1, seqs, tb, GLA_V), lambda b, j: (b, 0, j, 0)),
        out_shape=jax.ShapeDtypeStruct((groups, seqs, T, GLA_V), BF16),
        scratch_shapes=[pltpu.VMEM((seqs * GLA_HEADS, GLA_DV, GLA_DK), F32)],
        compiler_params=_cp(("parallel", "arbitrary")),
        name="gla",
    )(hb, hb, hb, hb, hs, wa_pad, b_a, norm_g)
    return out.reshape(B * T, GLA_V)


def _compress_kernel(x_ref, w1_ref, w2_ref, w2t_ref, pe_ref, o_ref, ot_ref):
    nc = x_ref.shape[0] // CMP_STRIDE
    hid_w = w1_ref.shape[3]
    a = jnp.zeros((nc, hid_w), F32)
    bm = jnp.zeros((nc, hid_w), F32)
    c = jnp.zeros((pe_ref.shape[2], hid_w), F32)
    for l in range(CMP_STRIDE):
        xl = x_ref[pl.ds(l, nc, stride=CMP_STRIDE), :].astype(BF16)
        a = a + _dot(xl, w1_ref[0, l])
        bm = bm + _dot(xl, w1_ref[0, CMP_STRIDE + l])
    for l in range(CMP_LEN):
        c = c + _dot(pe_ref[0, l], w1_ref[0, l])
    hid = a + pltpu.roll(bm, nc - 1, 0) + c[0:1, :]
    act = jax.nn.gelu(hid).astype(BF16)
    o_ref[0, 0, 0] = _dot(act, w2_ref[0]).astype(o_ref.dtype)
    ot_ref[0, 0, 0] = _dot_nt(w2t_ref[0], act).astype(ot_ref.dtype)


def _compress(h_small, w1bd, w2bd, w2bdt, pe_pair, B, T):
    nc = T // CMP_STRIDE
    pairs = NSA_GROUPS // 2
    return pl.pallas_call(
        _compress_kernel,
        grid=(B, 2, pairs),
        in_specs=[
            pl.BlockSpec((T, LANES), lambda b, s, j: (b, SCOL_CK // LANES + s * pairs + j)),
            pl.BlockSpec((1,) + w1bd.shape[1:], lambda b, s, j: (s, 0, 0, 0)),
            pl.BlockSpec((1,) + w2bd.shape[1:], lambda b, s, j: (s, 0, 0)),
            pl.BlockSpec((1,) + w2bdt.shape[1:], lambda b, s, j: (s, 0, 0)),
            pl.BlockSpec((1,) + pe_pair.shape[1:], lambda b, s, j: (s, 0, 0, 0)),
        ],
        out_specs=[
            pl.BlockSpec((1, 1, 1, nc, LANES), lambda b, s, j: (b, s, j, 0, 0)),
            pl.BlockSpec((1, 1, 1, LANES, nc), lambda b, s, j: (b, s, j, 0, 0)),
        ],
        out_shape=[jax.ShapeDtypeStruct((B, 2, pairs, nc, LANES), BF16),
                   jax.ShapeDtypeStruct((B, 2, pairs, LANES, nc), BF16)],
        compiler_params=_cp(("parallel", "parallel", "parallel")),
        name="nsa_compress",
    )(h_small, w1bd, w2bd, w2bdt, pe_pair)


def _cmp_select_kernel(slopes_ref, qt_ref, kc_ref, vct_ref, ovt_ref, ind_ref, ocmp_ref, mb_ref, kt_ref, qpad_ref):
    g = pl.program_id(1)
    i = pl.program_id(2)
    streams = qt_ref.shape[1]
    tq = qt_ref.shape[3]
    nc = kc_ref.shape[4]
    ns = mb_ref.shape[3]
    dh = NSA_DH
    t0 = i * tq
    wide = NSA_HPG * tq
    tpos = t0 + (lax.broadcasted_iota(I32, (nc, wide), 1) & (tq - 1))
    nidx = lax.broadcasted_iota(I32, (nc, wide), 0)
    mask_c = (nidx * CMP_STRIDE + (CMP_LEN - 1)) <= tpos
    absd = jnp.abs(tpos.astype(F32) - (nidx.astype(F32) * CMP_STRIDE + 0.5 * (CMP_LEN - 1)))
    srow = jnp.concatenate([jnp.full((1, tq), slopes_ref[g * NSA_HPG + hh], F32) for hh in range(NSA_HPG)], axis=1)
    lower = g % 2 == 0
    j = lax.broadcasted_iota(I32, (ns, tq), 0)
    tp = t0 + lax.broadcasted_iota(I32, (ns, tq), 1)
    cur = tp // SEL_LEN
    forced = (j == 0) | (j == cur) | (j == cur - 1)
    valid = j * SEL_LEN <= tp
    for u in range(streams):
        kc = kc_ref[0, u, 0, 0]
        vct = jnp.where(lower, vct_ref[0, u, 0, 0, 0:dh, :], vct_ref[0, u, 0, 0, dh:2 * dh, :])
        for hh in range(NSA_HPG):
            q = qt_ref[0, u, hh * dh:(hh + 1) * dh, :] * jnp.asarray(dh ** -0.5, BF16)
            zero = jnp.zeros_like(q)
            qpad_ref[u, 0:dh, hh * tq:(hh + 1) * tq] = jnp.where(lower, q, zero)
            qpad_ref[u, dh:2 * dh, hh * tq:(hh + 1) * tq] = jnp.where(lower, zero, q)
        s = _dot(kc, qpad_ref[u]) - srow * absd
        s = jnp.where(mask_c, s, NEG)
        e = jnp.exp(s - jnp.max(s, axis=0, keepdims=True))
        p = jnp.where(mask_c, e * (1.0 / jnp.sum(e, axis=0, keepdims=True)), 0.0)
        o = _dot(vct, p.astype(BF16))
        psum = jnp.zeros((nc, tq), F32)
        for hh in range(NSA_HPG):
            ocmp_ref[0, u, hh * dh:(hh + 1) * dh, :] = o[:, hh * tq:(hh + 1) * tq].astype(ocmp_ref.dtype)
            psum = psum + p[:, hh * tq:(hh + 1) * tq]
        p_hi = psum.astype(BF16)
        p_lo = (psum - p_hi.astype(F32)).astype(BF16)
        imp = _dot(ovt_ref[...], p_hi) + _dot(ovt_ref[...], p_lo)
        score = jnp.where(valid, imp + jnp.where(forced, FORCE_BONUS, 0.0), NEG)
        rank = jnp.zeros((ns, tq), F32)
        for jp in range(ns):
            row = score[jp:jp + 1, :]
            beats = (row > score) | ((row == score) & (j > jp))
            rank = rank + jnp.where(beats, 1.0, 0.0)
        keep = valid & (rank < float(min(SEL_TOPN, ns)))
        mb_ref[0, u, 0] = jnp.where(keep, 0.0, NEG).astype(mb_ref.dtype)
        kt_ref[0, u, 0] = _dot(ind_ref[...], jnp.where(keep, 1.0, 0.0).astype(BF16))


def _cmp_select(slopes, h_t, kcmp, kcmp_t, ovt, tile_ind, B, T, tq, streams):
    nc = T // CMP_STRIDE
    ns = T // SEL_LEN
    nkt = tile_ind.shape[0]
    G = NSA_GROUPS
    grp_rows = NSA_HPG * NSA_DH
    nbg = B // streams
    h_t4 = h_t.reshape(nbg, streams, h_t.shape[1], T)
    kcmp6 = kcmp.reshape((nbg, streams) + kcmp.shape[1:])
    kcmp_t6 = kcmp_t.reshape((nbg, streams) + kcmp_t.shape[1:])
    grid_spec = pltpu.PrefetchScalarGridSpec(
        num_scalar_prefetch=1,
        grid=(nbg, G, T // tq),
        in_specs=[
            pl.BlockSpec((1, streams, grp_rows, tq), lambda b, g, i, s: (b, 0, TROW_Q // grp_rows + g, i)),
            pl.BlockSpec((1, streams, 1, 1, nc, LANES), lambda b, g, i, s: (b, 0, 0, g // 2, 0, 0)),
            pl.BlockSpec((1, streams, 1, 1, LANES, nc), lambda b, g, i, s: (b, 0, 1, g // 2, 0, 0)),
            pl.BlockSpec((ns, nc), lambda b, g, i, s: (0, 0)),
            pl.BlockSpec((nkt, ns), lambda b, g, i, s: (0, 0)),
        ],
        out_specs=[
            pl.BlockSpec((1, streams, grp_rows, tq), lambda b, g, i, s: (b, 0, g, i)),
            pl.BlockSpec((1, streams, 1, ns, tq), lambda b, g, i, s: (b, 0, g, 0, i)),
            pl.BlockSpec((1, streams, 1, nkt, tq), lambda b, g, i, s: (b, 0, g, 0, i)),
        ],
        scratch_shapes=[pltpu.VMEM((streams, LANES, NSA_HPG * tq), BF16)],
    )
    ocmp, mb, in_tile = pl.pallas_call(
        _cmp_select_kernel,
        grid_spec=grid_spec,
        out_shape=[jax.ShapeDtypeStruct((nbg, streams, NSA_Q, T), BF16),
                   jax.ShapeDtypeStruct((nbg, streams, G, ns, T), BF16),
                   jax.ShapeDtypeStruct((nbg, streams, G, nkt, T), F32)],
        compiler_params=_cp(("parallel", "parallel", "parallel")),
        name="nsa_cmp_select",
    )(slopes, h_t4, kcmp6, kcmp_t6, ovt, tile_ind)
    return ocmp.reshape(B, NSA_Q, T), mb.reshape(B, G, ns, T), in_tile.reshape(B, G, nkt, T)


def _sel_win_kernel(slopes_ref, flags_ref, qt_ref, ks_ref, kw_ref, vs_ref, vw_ref, epad_ref, mb_ref, ocmp_ref, gt_ref,
                    o_ref, qaug_ref, m_ref, acc_ref, srow_ref, bias_ref, s_ref, p_ref, alpha_ref,
                    kall_ref, vall_ref, tiles_ref, *, n_tiles):
    g = pl.program_id(1)
    i = pl.program_id(2)
    streams = qt_ref.shape[1]
    tq = qt_ref.shape[3]
    tk = tq
    ns = mb_ref.shape[3]
    dh = NSA_DH
    wide = NSA_HPG * tq
    BIG = -NEG

    @pl.when(i == 0)
    def _():
        srow = jnp.concatenate([jnp.full((1, tq), slopes_ref[g * NSA_HPG + hh] * LOG2E, F32)
                                for hh in range(NSA_HPG)], axis=1)
        srow_ref[...] = srow
        lane = lax.broadcasted_iota(I32, (tk, wide), 1) & (tq - 1)
        dist0 = (lane - lax.broadcasted_iota(I32, (tk, wide), 0)).astype(F32)
        sd0 = srow * dist0
        bias_ref[0] = sd0
        bias_ref[1] = sd0 + jnp.where(dist0 >= 0.0, 0.0, BIG)
        bias_ref[2] = sd0 + jnp.where(dist0 < 0.0, 0.0, BIG)
        bias_ref[3] = jnp.full((tk, wide), BIG, F32)
        extra = jnp.where(lax.broadcasted_iota(I32, (vall_ref.shape[2] - dh, vall_ref.shape[3]), 0) == 0, 1.0, 0.0)
        for u in range(streams):
            kall_ref[u, 0] = ks_ref[0, u] + epad_ref[...]
            kall_ref[u, 1] = kw_ref[0, u]
            vall_ref[u, 0, 0:dh, :] = vs_ref[0, u]
            vall_ref[u, 1, 0:dh, :] = vw_ref[0, u]
            vall_ref[u, 0, dh:, :] = extra.astype(BF16)
            vall_ref[u, 1, dh:, :] = extra.astype(BF16)

    for u in range(streams):
        for hh in range(NSA_HPG):
            cols = slice(hh * tq, (hh + 1) * tq)
            q = qt_ref[0, u, hh * dh:(hh + 1) * dh, :].astype(F32) * (dh ** -0.5 * LOG2E)
            qaug_ref[u, 0:dh, cols] = q.astype(BF16)
            qaug_ref[u, dh:dh + ns, cols] = mb_ref[0, u, 0]
            qaug_ref[u, dh + ns:, cols] = jnp.zeros((qaug_ref.shape[1] - dh - ns, tq), BF16)

    m_ref[...] = jnp.full(m_ref.shape, NEG, F32)
    acc_ref[...] = jnp.zeros(acc_ref.shape, F32)

    n_back = WINDOW // tk
    n_win = jnp.minimum(i, n_back) + 1
    n_sel = []
    for u in range(streams):
        flag_base = (((pl.program_id(0) * streams + u) * NSA_GROUPS + g) * n_tiles + i) * n_tiles
        cnt = jnp.int32(0)
        for kb_static in range(n_tiles - 1):
            active = (kb_static < i) & (flags_ref[flag_base + kb_static] != 0)
            tiles_ref[u, cnt] = kb_static
            cnt = cnt + active.astype(I32)
        tiles_ref[u, cnt] = i
        n_sel.append(cnt + 1)
    n_steps = n_sel[0] + n_win
    for u in range(1, streams):
        n_steps = jnp.maximum(n_steps, n_sel[u] + n_win)

    def describe(u, n):
        n = jnp.maximum(n, 0)
        is_win = n >= n_sel[u]
        kb_sel = tiles_ref[u, jnp.minimum(n, n_sel[u] - 1)]
        kb = jnp.clip(jnp.where(is_win, i - n_win + 1 + (n - n_sel[u]), kb_sel), 0, i)
        mode = jnp.where(kb == i, 1, jnp.where(is_win & (kb == i - n_back), 2, 0))
        mode = jnp.where(n >= n_sel[u] + n_win, 3, mode)
        return is_win.astype(I32), kb, mode

    def scores(u, n, slot):
        br, kb, _ = describe(u, n)
        s0 = pl.multiple_of(kb * tk, tk)
        s_ref[u, slot] = _dot(kall_ref[u, br, pl.ds(s0, tk), :], qaug_ref[u])

    def softmax(u, n, slot):
        br, kb, mode = describe(u, n)
        crow = srow_ref[...] * ((i - kb) * tk).astype(F32)
        s = s_ref[u, slot] - bias_ref[mode]
        m_old = m_ref[u, br]
        m_new = jnp.maximum(m_old, jnp.max(s, axis=0, keepdims=True) - crow)
        alpha = jnp.exp2(m_old - m_new)
        p = jnp.exp2(s - (m_new + crow))
        m_ref[u, br] = m_new
        alpha_ref[u, slot] = alpha
        p_ref[u, slot] = p.astype(BF16)

    def weighted_values(u, n, slot):
        br, kb, _ = describe(u, n)
        s0 = pl.multiple_of(kb * tk, tk)
        acc_ref[u, br] = (alpha_ref[u, slot] * acc_ref[u, br]
                          + _dot(vall_ref[u, br, :, pl.ds(s0, tk)], p_ref[u, slot]))

    def each(fn, n, slot):
        for u in range(streams):
            fn(u, n, slot)

    for u in range(streams):
        p_ref[u, 1] = jnp.zeros(p_ref.shape[2:], BF16)
        alpha_ref[u, 1] = jnp.ones(alpha_ref.shape[2:], F32)
    each(scores, 0, 0)

    def pair(j, carry):
        n = 2 * j
        each(scores, n + 1, 1)
        each(softmax, n, 0)
        each(weighted_values, n - 1, 1)
        each(scores, n + 2, 0)
        each(softmax, n + 1, 1)
        each(weighted_values, n, 0)
        return carry

    n_pairs = n_steps // 2
    lax.fori_loop(0, n_pairs, pair, 0)
    each(weighted_values, 2 * n_pairs - 1, 1)

    @pl.when(n_steps % 2 == 1)
    def _():
        each(softmax, n_steps - 1, 0)
        each(weighted_values, n_steps - 1, 0)

    for u in range(streams):
        def gate_row(branch, u=u):
            rows = [gt_ref[0, u, 0, 3 * hh + branch:3 * hh + branch + 1, :] for hh in range(NSA_HPG)]
            return jax.nn.sigmoid(jnp.concatenate(rows, axis=1))

        o = (acc_ref[u, 0, 0:dh, :] * (gate_row(1) / acc_ref[u, 0, dh:dh + 1, :])
             + acc_ref[u, 1, 0:dh, :] * (gate_row(2) / acc_ref[u, 1, dh:dh + 1, :]))
        ocmp = jnp.concatenate([ocmp_ref[0, u, hh * dh:(hh + 1) * dh, :] for hh in range(NSA_HPG)],
                               axis=1).astype(F32)
        o = o + gate_row(0) * ocmp
        o_heads = jnp.concatenate([o[:, hh * tq:(hh + 1) * tq] for hh in range(NSA_HPG)], axis=0)
        o_ref[0, u] = o_heads.T.astype(o_ref.dtype)


def _sel_win(slopes, tile_flags, h_t, h_big, epad, mb, ocmp_t, gates_t, B, T, tq, streams):
    ns = T // SEL_LEN
    kaug = LANES
    G = NSA_GROUPS
    grp_rows = NSA_HPG * NSA_DH
    nq = T // tq
    nbg = B // streams
    wide = NSA_HPG * tq
    h_t5 = h_t.reshape(nbg, streams, h_t.shape[1], T)
    h_big5 = h_big.reshape(nbg, streams, T, h_big.shape[1])
    mb5 = mb.reshape(nbg, streams, G, ns, T)
    ocmp5 = ocmp_t.reshape(nbg, streams, NSA_Q, T)
    gates5 = gates_t.reshape(nbg, streams, G, GATE_ROWS, T)
    grid_spec = pltpu.PrefetchScalarGridSpec(
        num_scalar_prefetch=2,
        grid=(nbg, G, nq),
        in_specs=[
            pl.BlockSpec((1, streams, grp_rows, tq), lambda b, g, i, s, f: (b, 0, TROW_Q // grp_rows + g, i)),
            pl.BlockSpec((1, streams, T, LANES), lambda b, g, i, s, f: (b, 0, 0, COL_KS // LANES + g)),
            pl.BlockSpec((1, streams, T, LANES), lambda b, g, i, s, f: (b, 0, 0, COL_KW // LANES + g)),
            pl.BlockSpec((1, streams, NSA_DH, T), lambda b, g, i, s, f: (b, 0, TROW_VS // NSA_DH + g, 0)),
            pl.BlockSpec((1, streams, NSA_DH, T), lambda b, g, i, s, f: (b, 0, TROW_VW // NSA_DH + g, 0)),
            pl.BlockSpec((T, LANES), lambda b, g, i, s, f: (0, 0)),
            pl.BlockSpec((1, streams, 1, ns, tq), lambda b, g, i, s, f: (b, 0, g, 0, i)),
            pl.BlockSpec((1, streams, grp_rows, tq), lambda b, g, i, s, f: (b, 0, g, i)),
            pl.BlockSpec((1, streams, 1, GATE_ROWS, tq), lambda b, g, i, s, f: (b, 0, g, 0, i)),
        ],
        out_specs=pl.BlockSpec((1, streams, tq, grp_rows), lambda b, g, i, s, f: (b, 0, i, g)),
        scratch_shapes=[
            pltpu.VMEM((streams, kaug, wide), BF16),
            pltpu.VMEM((streams, 2, 1, wide), F32),
            pltpu.VMEM((streams, 2, NSA_DH + 16, wide), F32),
            pltpu.VMEM((1, wide), F32),
            pltpu.VMEM((4, tq, wide), F32),
            pltpu.VMEM((streams, 2, tq, wide), F32),
            pltpu.VMEM((streams, 2, tq, wide), BF16),
            pltpu.VMEM((streams, 2, 1, wide), F32),
            pltpu.VMEM((streams, 2, T, kaug), BF16),
            pltpu.VMEM((streams, 2, NSA_DH + 16, T), BF16),
            pltpu.SMEM((streams, nq), I32),
        ],
    )
    out = pl.pallas_call(
        functools.partial(_sel_win_kernel, n_tiles=nq),
        grid_spec=grid_spec,
        out_shape=jax.ShapeDtypeStruct((nbg, streams, T, NSA_Q), BF16),
        compiler_params=_cp(("parallel", "parallel", "arbitrary")),
        name="nsa_sel_win",
    )(slopes, tile_flags, h_t5, h_big5, h_big5, h_t5, h_t5, epad, mb5, ocmp5, gates5)
    return out.reshape(B * T, NSA_Q)


def _mix_kernel(og_ref, on_ref, mg1_ref, mg2_ref, x_ref, wg_ref, wn_ref, wo_ref, lg_ref, lb_ref,
                x1_ref, x1b_ref):
    g1 = _dot(og_ref[...], wg_ref[...])
    g2 = _dot(on_ref[...], wn_ref[...])
    merged = (jax.nn.sigmoid(mg1_ref[...].astype(F32)) * g1
              + jax.nn.sigmoid(mg2_ref[...].astype(F32)) * g2)
    y = _dot(merged.astype(BF16), wo_ref[...])
    x1 = _layer_norm(DN_ALPHA * x_ref[...] + y, lg_ref[...], lb_ref[...])
    x1_ref[...] = x1
    x1b_ref[...] = x1.astype(BF16)


def _const_spec(shape):
    nd = len(shape)
    return pl.BlockSpec(shape, lambda *_: (0,) * nd, pipeline_mode=pl.Buffered(1))


def _mix(o_gla, o_nsa, h_big, x, wg, wn, wo, lg, lb, tm):
    n, d = x.shape
    return pl.pallas_call(
        _mix_kernel,
        grid=(n // tm,),
        in_specs=[
            pl.BlockSpec((tm, GLA_V), lambda i: (i, 0)),
            pl.BlockSpec((tm, NSA_Q), lambda i: (i, 0)),
            pl.BlockSpec((tm, d), lambda i: (i, 0)),
            pl.BlockSpec((tm, d), lambda i: (i, 1)),
            pl.BlockSpec((tm, d), lambda i: (i, 0)),
            _const_spec(wg.shape), _const_spec(wn.shape), _const_spec(wo.shape),
            _const_spec(lg.shape), _const_spec(lb.shape),
        ],
        out_specs=[pl.BlockSpec((tm, d), lambda i: (i, 0)), pl.BlockSpec((tm, d), lambda i: (i, 0))],
        out_shape=[jax.ShapeDtypeStruct((n, d), F32), jax.ShapeDtypeStruct((n, d), BF16)],
        compiler_params=_cp(("parallel",)),
        name="mix_ln",
    )(o_gla, o_nsa, h_big, h_big, x, wg, wn, wo, lg, lb)


def _xattn_kernel(x_ref, xb_ref, kv_ref, wq_ref, wo_ref, lg_ref, lb_ref, x2_ref):
    hd = XA_HEADS * XA_DH
    q = (_dot(xb_ref[...], wq_ref[...]) * (XA_DH ** -0.5)).astype(BF16)
    outs = []
    for h in range(XA_HEADS):
        kh = kv_ref[0, :, h * XA_DH:(h + 1) * XA_DH]
        vh = kv_ref[0, :, hd + h * XA_DH:hd + (h + 1) * XA_DH]
        s = _dot_nt(q[:, h * XA_DH:(h + 1) * XA_DH], kh)
        e = jnp.exp(s - jnp.max(s, axis=-1, keepdims=True))
        p = e / jnp.sum(e, axis=-1, keepdims=True)
        outs.append(_dot(p.astype(BF16), vh).astype(BF16))
    o = jnp.concatenate(outs, axis=-1)
    y = _dot(o, wo_ref[...])
    x2_ref[...] = _layer_norm(DN_ALPHA * x_ref[...] + y, lg_ref[...], lb_ref[...])


def _xattn(x1, x1b, kv, wq, wo, lg, lb, B, T, tm):
    n, d = x1.shape
    nt = T // tm
    return pl.pallas_call(
        _xattn_kernel,
        grid=(B, nt),
        in_specs=[
            pl.BlockSpec((tm, d), lambda b, i: (b * nt + i, 0)),
            pl.BlockSpec((tm, d), lambda b, i: (b * nt + i, 0)),
            pl.BlockSpec((1,) + kv.shape[1:], lambda b, i: (b, 0, 0)),
            _const_spec(wq.shape), _const_spec(wo.shape), _const_spec(lg.shape), _const_spec(lb.shape),
        ],
        out_specs=pl.BlockSpec((tm, d), lambda b, i: (b * nt + i, 0)),
        out_shape=jax.ShapeDtypeStruct((n, d), F32),
        compiler_params=_cp(("parallel", "parallel")),
        name="xattn_ln",
    )(x1, x1b, kv, wq, wo, lg, lb)


def _router_kernel(x_ref, wh_ref, wl_ref, rb_ref, e_ref, gate_ref, rank_ref, cnt_ref, carry_ref):
    i = pl.program_id(0)
    tr = x_ref.shape[0]
    E = N_EXPERTS

    @pl.when(i == 0)
    def _():
        carry_ref[...] = jnp.zeros_like(carry_ref)

    x = x_ref[...]
    x_hi = x.astype(BF16)
    x_lo = (x - x_hi.astype(F32)).astype(BF16)
    wh = wh_ref[...]
    logits = _dot_nt(wh, x_hi) + _dot_nt(wh, x_lo) + _dot_nt(wl_ref[...], x_hi)
    biased = logits + rb_ref[...]
    rows = [biased[e:e + 1, :] for e in range(E)]
    raw = [logits[e:e + 1, :] for e in range(E)]
    best_score = None
    best = None
    for gi in range(N_GROUPS):
        v = rows[gi * EXPERTS_PER_GROUP:(gi + 1) * EXPERTS_PER_GROUP]
        sc = None
        for a in range(EXPERTS_PER_GROUP):
            for b in range(a + 1, EXPERTS_PER_GROUP):
                pair = v[a] + v[b]
                sc = pair if sc is None else jnp.maximum(sc, pair)
        if best is None:
            best_score, best = sc, jnp.zeros((1, tr), I32)
        else:
            better = sc > best_score
            best_score = jnp.where(better, sc, best_score)
            best = jnp.where(better, gi, best)

    def pick(vals):
        out = vals[0:EXPERTS_PER_GROUP]
        for gi in range(1, N_GROUPS):
            out = [jnp.where(best == gi, vals[gi * EXPERTS_PER_GROUP + a], out[a]) for a in range(EXPERTS_PER_GROUP)]
        return out

    w = pick(rows)
    lraw = pick(raw)
    i1 = jnp.zeros((1, tr), I32)
    v1 = w[0]
    l1 = lraw[0]
    for a in range(1, EXPERTS_PER_GROUP):
        better = w[a] > v1
        v1 = jnp.where(better, w[a], v1)
        l1 = jnp.where(better, lraw[a], l1)
        i1 = jnp.where(better, a, i1)
    i2 = jnp.full((1, tr), -1, I32)
    v2 = jnp.full((1, tr), -jnp.inf, F32)
    l2 = jnp.zeros((1, tr), F32)
    for a in range(EXPERTS_PER_GROUP):
        better = (i1 != a) & ((w[a] > v2) | (i2 < 0))
        v2 = jnp.where(better, w[a], v2)
        l2 = jnp.where(better, lraw[a], l2)
        i2 = jnp.where(better, a, i2)
    e1 = best * EXPERTS_PER_GROUP + i1
    e2 = best * EXPERTS_PER_GROUP + i2
    mx = jnp.maximum(l1, l2)
    p1 = jnp.exp(l1 - mx)
    p2 = jnp.exp(l2 - mx)
    den = p1 + p2
    e_ref[0:1, :] = e1
    e_ref[1:2, :] = e2
    gate_ref[0:1, :] = p1 / den
    gate_ref[1:2, :] = p2 / den
    eidx = lax.broadcasted_iota(I32, (E, tr), 0)
    is1 = eidx == e1
    is2 = eidx == e2
    member = jnp.where(is1 | is2, 1.0, 0.0)
    uu = lax.broadcasted_iota(I32, (tr, tr), 0)
    tt = lax.broadcasted_iota(I32, (tr, tr), 1)
    tri = jnp.where(uu <= tt, 1.0, 0.0).astype(BF16)
    incl = _dot(member.astype(BF16), tri)
    excl = carry_ref[:, 0:1] + incl - member
    rank_ref[0:1, :] = jnp.sum(jnp.where(is1, excl, 0.0), axis=0, keepdims=True).astype(I32)
    rank_ref[1:2, :] = jnp.sum(jnp.where(is2, excl, 0.0), axis=0, keepdims=True).astype(I32)
    new_carry = carry_ref[...] + jnp.sum(member, axis=1, keepdims=True)
    carry_ref[...] = new_carry
    cnt_ref[...] = new_carry


def _router(x2, rw_hi, rw_lo, rb, tr):
    n, d = x2.shape
    E = N_EXPERTS
    return pl.pallas_call(
        _router_kernel,
        grid=(n // tr,),
        in_specs=[
            pl.BlockSpec((tr, d), lambda i: (i, 0)),
            pl.BlockSpec((E, d), lambda i: (0, 0)),
            pl.BlockSpec((E, d), lambda i: (0, 0)),
            pl.BlockSpec((E, 1), lambda i: (0, 0)),
        ],
        out_specs=[
            pl.BlockSpec((2, tr), lambda i: (0, i)),
            pl.BlockSpec((2, tr), lambda i: (0, i)),
            pl.BlockSpec((2, tr), lambda i: (0, i)),
            pl.BlockSpec((E, LANES), lambda i: (0, 0)),
        ],
        out_shape=[jax.ShapeDtypeStruct((2, n), I32), jax.ShapeDtypeStruct((2, n), F32),
                   jax.ShapeDtypeStruct((2, n), I32), jax.ShapeDtypeStruct((E, LANES), F32)],
        scratch_shapes=[pltpu.VMEM((E, LANES), F32)],
        compiler_params=_cp(("arbitrary",)),
        name="moe_router",
    )(x2, rw_hi, rw_lo, rb)


def _slot_kernel(ps_ref, e_ref, rank_ref, slot_ref):
    e = e_ref[...]
    start = jnp.zeros(e.shape, I32)
    for ex in range(N_EXPERTS):
        start = jnp.where(e == ex, ps_ref[ex], start)
    slot_ref[...] = start + rank_ref[...]


def _slots(pad_start, e, rank, ts):
    n = e.shape[1]
    grid_spec = pltpu.PrefetchScalarGridSpec(
        num_scalar_prefetch=1,
        grid=(n // ts,),
        in_specs=[pl.BlockSpec((TOP_K, ts), lambda i, s: (0, i)), pl.BlockSpec((TOP_K, ts), lambda i, s: (0, i))],
        out_specs=pl.BlockSpec((TOP_K, ts), lambda i, s: (0, i)),
    )
    return pl.pallas_call(
        _slot_kernel,
        grid_spec=grid_spec,
        out_shape=jax.ShapeDtypeStruct((TOP_K, n), I32),
        compiler_params=_cp(("parallel",)),
        name="moe_slots",
    )(pad_start, e, rank)


def _dispatch_kernel(pe_ref, s0_ref, s1_ref, x_ref, buf_hbm, zero_ref, sem):
    td = s0_ref.shape[0]
    slots = (s0_ref, s1_ref)

    @pl.when(pl.program_id(0) == 0)
    def _():
        zero_ref[...] = jnp.zeros_like(zero_ref)

        def zero_copy(ex):
            last = pl.multiple_of(jnp.maximum(pe_ref[ex] - MOE_BLOCK, 0), MOE_BLOCK)
            return pltpu.make_async_copy(zero_ref, buf_hbm.at[pl.ds(last, MOE_BLOCK), :], sem)

        def nonempty(ex):
            return pe_ref[ex] > (pe_ref[ex - 1] if ex > 0 else 0)

        n_blocks = buf_hbm.shape[0] // MOE_BLOCK
        first_unused = pe_ref[N_EXPERTS - 1] // MOE_BLOCK

        def tail_copy(k):
            row = pl.multiple_of((first_unused + k) * MOE_BLOCK, MOE_BLOCK)
            return pltpu.make_async_copy(zero_ref, buf_hbm.at[pl.ds(row, MOE_BLOCK), :], sem)

        for ex in range(N_EXPERTS):
            pl.when(nonempty(ex))(lambda ex=ex: zero_copy(ex).start())
            pl.when(first_unused + ex < n_blocks)(lambda ex=ex: tail_copy(ex).start())
        for ex in range(N_EXPERTS):
            pl.when(nonempty(ex))(lambda ex=ex: zero_copy(ex).wait())
            pl.when(first_unused + ex < n_blocks)(lambda ex=ex: tail_copy(ex).wait())

    def issue(t, carry):
        for kk in range(TOP_K):
            dest = slots[kk][t]
            pltpu.make_async_copy(x_ref.at[pl.ds(t, 1), :], buf_hbm.at[pl.ds(dest, 1), :], sem).start()
        return carry

    lax.fori_loop(0, td, issue, 0, unroll=8)
    for kk in range(TOP_K):
        pltpu.make_async_copy(x_ref, buf_hbm.at[pl.ds(0, td), :], sem).wait()


def _dispatch(pad_end, slot0, slot1, x2, n_rows, td):
    n, d = x2.shape
    grid_spec = pltpu.PrefetchScalarGridSpec(
        num_scalar_prefetch=1,
        grid=(n // td,),
        in_specs=[
            pl.BlockSpec((td,), lambda i, s: (i,), memory_space=pltpu.SMEM),
            pl.BlockSpec((td,), lambda i, s: (i,), memory_space=pltpu.SMEM),
            pl.BlockSpec((td, d), lambda i, s: (i, 0)),
        ],
        out_specs=pl.BlockSpec(memory_space=pl.ANY),
        scratch_shapes=[pltpu.VMEM((MOE_BLOCK, d), F32), pltpu.SemaphoreType.DMA(())],
    )
    return pl.pallas_call(
        _dispatch_kernel,
        grid_spec=grid_spec,
        out_shape=jax.ShapeDtypeStruct((n_rows, d), F32),
        compiler_params=_cp(("arbitrary",)),
        name="moe_dispatch",
    )(pad_end, slot0, slot1, x2)


def _expert_kernel(be_ref, nb_ref, x_ref, win_hbm, wdn_hbm, y_ref, xb_ref, wa_s, wu_s, wd_s, sa, su, sd, sems,
                   *, layer):
    b = pl.program_id(0)
    nf = D_FF // FF_TILE
    n_used = nb_ref[0]
    e = be_ref[b]
    e_prev = be_ref[jnp.maximum(b - 1, 0)]
    e_next = be_ref[jnp.minimum(b + 1, pl.num_programs(0) - 1)]
    active = b < n_used
    is_first = active & ((b == 0) | (e_prev != e))
    feeds_next = active & (b + 1 < n_used) & (e_next != e)

    def tile_copies(ex, f):
        lo = f * FF_TILE
        return (pltpu.make_async_copy(win_hbm.at[layer, ex, :, pl.ds(lo, FF_TILE)], sa, sems.at[0]),
                pltpu.make_async_copy(win_hbm.at[layer, ex, :, pl.ds(D_FF + lo, FF_TILE)], su, sems.at[1]),
                pltpu.make_async_copy(wdn_hbm.at[layer, ex, pl.ds(lo, FF_TILE), :], sd, sems.at[2]))

    def start(ex, f):
        for c in tile_copies(ex, f):
            c.start()

    def finish(ex, f):
        for c in tile_copies(ex, f):
            c.wait()
        wa_s[f] = sa[...].astype(BF16)
        wu_s[f] = su[...].astype(BF16)
        wd_s[f] = sd[...].astype(BF16)

    @pl.when(b == 0)
    def _():
        for f in range(nf - 1):
            start(e, f)
            finish(e, f)
        start(e, nf - 1)

    @pl.when(jnp.logical_not(active))
    def _():
        y_ref[...] = jnp.zeros_like(y_ref)

    @pl.when(active)
    def _():
        xb_ref[...] = x_ref[...].astype(BF16)
        for f in range(nf):
            xb = xb_ref[...]
            a = _dot(xb, wa_s[f])
            u = _dot(xb, wu_s[f])
            act = (a * jax.nn.sigmoid(a) * u).astype(BF16)
            y = _dot(act, wd_s[f])
            if f == 0:
                y_ref[...] = y
                pl.when(is_first)(lambda: finish(e, nf - 1))
            else:
                y_ref[...] += y

            @pl.when(feeds_next)
            def _(f=f):
                if f >= 1:
                    finish(e_next, f - 1)
                start(e_next, f)


def _experts(blk_expert, n_used, buf, w_in, w_down, layer):
    p, d = buf.shape
    nb = p // MOE_BLOCK
    nf = D_FF // FF_TILE
    grid_spec = pltpu.PrefetchScalarGridSpec(
        num_scalar_prefetch=2,
        grid=(nb,),
        in_specs=[
            pl.BlockSpec((MOE_BLOCK, d), lambda b, be, nu: (jnp.minimum(b, nu[0] - 1), 0)),
            pl.BlockSpec(memory_space=pl.ANY),
            pl.BlockSpec(memory_space=pl.ANY),
        ],
        out_specs=pl.BlockSpec((MOE_BLOCK, d), lambda b, be, nu: (b, 0)),
        scratch_shapes=[
            pltpu.VMEM((MOE_BLOCK, d), BF16),
            pltpu.VMEM((nf, d, FF_TILE), BF16),
            pltpu.VMEM((nf, d, FF_TILE), BF16),
            pltpu.VMEM((nf, FF_TILE, d), BF16),
            pltpu.VMEM((d, FF_TILE), F32),
            pltpu.VMEM((d, FF_TILE), F32),
            pltpu.VMEM((FF_TILE, d), F32),
            pltpu.SemaphoreType.DMA((3,)),
        ],
    )
    return pl.pallas_call(
        functools.partial(_expert_kernel, layer=layer),
        grid_spec=grid_spec,
        out_shape=jax.ShapeDtypeStruct((p, d), F32),
        compiler_params=_cp(("arbitrary",)),
        name="moe_experts",
    )(blk_expert, n_used, buf, w_in, w_down)


def _combine_kernel(s0_ref, s1_ref, n0_ref, n1_ref, y_hbm, x_ref, gate_ref, lg_ref, lb_ref, x3_ref, x3b_ref,
                    y0_ref, y1_ref, sems):
    i = pl.program_id(0)
    tc = x_ref.shape[0]
    bufs = (y0_ref, y1_ref)
    cur = i % 2

    def issue_tile(slot_refs, half):
        def issue(t, carry):
            for kk in range(TOP_K):
                src = slot_refs[kk][t]
                pltpu.make_async_copy(y_hbm.at[pl.ds(src, 1), :], bufs[kk].at[half, pl.ds(t, 1), :],
                                      sems.at[half]).start()
            return carry

        lax.fori_loop(0, tc, issue, 0, unroll=8)

    @pl.when(i == 0)
    def _():
        issue_tile((s0_ref, s1_ref), 0)

    @pl.when(i + 1 < pl.num_programs(0))
    def _():
        issue_tile((n0_ref, n1_ref), 1 - cur)

    for kk in range(TOP_K):
        pltpu.make_async_copy(y_hbm.at[pl.ds(0, tc), :], bufs[kk].at[cur], sems.at[cur]).wait()
    gate = gate_ref[...]
    z = DN_ALPHA * x_ref[...] + gate[:, 0:1] * y0_ref[cur] + gate[:, 1:2] * y1_ref[cur]
    x3 = _layer_norm(z, lg_ref[...], lb_ref[...])
    x3_ref[...] = x3
    x3b_ref[...] = x3.astype(BF16)


def _combine(slot0, slot1, y, x2, gate_nt, lg, lb, tc):
    n, d = x2.shape
    last = n // tc - 1
    return pl.pallas_call(
        _combine_kernel,
        grid=(n // tc,),
        in_specs=[
            pl.BlockSpec((tc,), lambda i: (i,), memory_space=pltpu.SMEM),
            pl.BlockSpec((tc,), lambda i: (i,), memory_space=pltpu.SMEM),
            pl.BlockSpec((tc,), lambda i: (jnp.minimum(i + 1, last),), memory_space=pltpu.SMEM),
            pl.BlockSpec((tc,), lambda i: (jnp.minimum(i + 1, last),), memory_space=pltpu.SMEM),
            pl.BlockSpec(memory_space=pl.ANY),
            pl.BlockSpec((tc, d), lambda i: (i, 0)),
            pl.BlockSpec((tc, 2), lambda i: (i, 0)),
            pl.BlockSpec((1, d), lambda i: (0, 0)),
            pl.BlockSpec((1, d), lambda i: (0, 0)),
        ],
        out_specs=[pl.BlockSpec((tc, d), lambda i: (i, 0)), pl.BlockSpec((tc, d), lambda i: (i, 0))],
        out_shape=[jax.ShapeDtypeStruct((n, d), F32), jax.ShapeDtypeStruct((n, d), BF16)],
        scratch_shapes=[pltpu.VMEM((2, tc, d), F32), pltpu.VMEM((2, tc, d), F32), pltpu.SemaphoreType.DMA((2,))],
        compiler_params=_cp(("arbitrary",)),
        name="moe_combine_ln",
    )(slot0, slot1, slot0, slot1, y, x2, gate_nt, lg, lb)


def _layer(x, xb, mem_b, p, moe_w, layer, consts, B, T):
    n, d = x.shape
    G, HPG, DH = NSA_GROUPS, NSA_HPG, NSA_DH
    slopes, ovt, epad, tile_ind = consts

    seqs = TILES.seqs if B % TILES.seqs == 0 else 1
    h_big = _matmul(xb, p["w_big"], BF16, TILES.proj_rows, TILES.proj_cols)
    h_small = _matmul(xb, p["w_small"], F32, TILES.proj_rows, SCOL_END)
    h_t = _matmul_t(xb, p["w_t"], B, T, TILES.proj_rows, TROW_END // 2)

    o_gla = _gla(h_big, h_small, p["wa_pad"], p["b_a"], p["norm_g"], B, T, seqs, TILES.gla_rows)

    kcmp, kcmp_t = _compress(h_small, p["cmp_w1bd"], p["cmp_w2bd"], p["cmp_w2bdt"], p["cmp_pe_pair"], B, T)
    tq_sel = TILES.sel_q
    ocmp_t, mb, in_tile = _cmp_select(slopes, h_t, kcmp, kcmp_t, ovt, tile_ind, B, T, TILES.cmp_q, seqs)
    nq = T // tq_sel
    tile_flags = (in_tile.reshape(B, G, nq, nq, tq_sel).max(axis=-1) > 0).astype(I32)
    tile_flags = tile_flags.transpose(0, 1, 3, 2).reshape(-1)
    gates_t = h_small[:, GLA_GATE_RANK:GLA_GATE_RANK + 3 * NSA_HEADS].reshape(B, T, G, 3 * HPG)
    gates_t = jnp.pad(gates_t.transpose(0, 2, 3, 1), ((0, 0), (0, 0), (0, GATE_ROWS - 3 * HPG), (0, 0)))
    o_nsa = _sel_win(slopes, tile_flags, h_t, h_big, epad, mb, ocmp_t, gates_t, B, T, tq_sel, seqs)

    x1, x1b = _mix(o_gla, o_nsa, h_big, x, p["w_bg"], p["w_bn"], p["w_out"], p["ln_mix_g"], p["ln_mix_b"],
                   TILES.mix_rows)

    kvm = _matmul(mem_b, p["xa_wkv"], BF16, TILES.kv_rows, TILES.kv_cols).reshape(B, MEM_LEN, 2 * XA_HEADS * XA_DH)
    x2 = _xattn(x1, x1b, kvm, p["xa_wq"], p["xa_wo"], p["ln_xa_g"], p["ln_xa_b"], B, T, TILES.xattn_rows)

    e, gate, rank, cnt = _router(x2, p["rw_hi"], p["rw_lo"], p["rb"], TILES.router_rows)
    counts = cnt[:, 0].astype(I32)
    padded = (counts + MOE_BLOCK - 1) // MOE_BLOCK * MOE_BLOCK
    pad_end = jnp.cumsum(padded)
    pad_start = (pad_end - padded).astype(I32)
    nb = (n * TOP_K) // MOE_BLOCK + N_EXPERTS
    n_used = (pad_end[-1] // MOE_BLOCK).astype(I32).reshape(1)
    blk_start = jnp.arange(nb, dtype=I32) * MOE_BLOCK
    blk_expert = jnp.minimum(jnp.sum(blk_start[:, None] >= pad_end[None, :], axis=1), N_EXPERTS - 1).astype(I32)
    blk_expert = jnp.where(jnp.arange(nb) < n_used[0], blk_expert, blk_expert[jnp.maximum(n_used[0] - 1, 0)])
    slot = _slots(pad_start, e, rank, TILES.slot_cols)
    buf = _dispatch(pad_end.astype(I32), slot[0], slot[1], x2, nb * MOE_BLOCK, TILES.dispatch_rows)
    y = _experts(blk_expert, n_used, buf, moe_w[0], moe_w[1], layer)
    x3, x3b = _combine(slot[0], slot[1], y, x2, gate.T, p["ln_ffn_g"], p["ln_ffn_b"],
                       TILES.combine_rows)
    return x3, x3b


def _prep_layer(l, w_in, gla_w_a2, gla_b_a, gla_norm_g, nsa_cmp_pe, nsa_cmp_w1, nsa_cmp_w2, w_branch_gla,
                w_branch_nsa, w_out, ln_mix_g, ln_mix_b, xa_wq, xa_wkv, xa_wo, ln_xa_g, ln_xa_b, router_w,
                router_b, moe_w_in, moe_w_down, ln_ffn_g, ln_ffn_b):
    d = w_in.shape[1]
    w = w_in[l]
    o_gq, o_gk, o_gv, o_gr = 0, GLA_QK, 2 * GLA_QK, 2 * GLA_QK + GLA_V
    o_ga = o_gr + GLA_V
    o_nq = o_ga + GLA_GATE_RANK
    o_nkv = o_nq + NSA_Q
    o_ng = o_nkv + 6 * NSA_KV
    o_mg = o_ng + 3 * NSA_HEADS
    G, DH = NSA_GROUPS, NSA_DH

    def kv_cols(kind):
        return w[:, o_nkv + kind * NSA_KV:o_nkv + (kind + 1) * NSA_KV]

    def slabs(wk):
        return jnp.pad(wk.reshape(d, G, DH), ((0, 0), (0, 0), (0, LANES - DH))).reshape(d, G * LANES)

    w_big = jnp.concatenate([w[:, o_mg:o_mg + 2 * d], w[:, o_gq:o_ga], slabs(kv_cols(2)), slabs(kv_cols(4))],
                            axis=1).astype(BF16)
    w_small = jnp.concatenate([w[:, o_ga:o_nq], w[:, o_ng:o_mg],
                               jnp.zeros((d, LANES - GLA_GATE_RANK - 3 * NSA_HEADS), F32),
                               kv_cols(0), kv_cols(1)], axis=1).astype(BF16)
    w_t = jnp.concatenate([w[:, o_nq:o_nkv], kv_cols(3), kv_cols(5)], axis=1).T.astype(BF16)
    w1 = nsa_cmp_w1[l].reshape(2, CMP_LEN, DH, CMP_HIDDEN)
    z1 = jnp.zeros_like(w1)
    w1bd = jnp.concatenate([jnp.concatenate([w1, z1], axis=3), jnp.concatenate([z1, w1], axis=3)], axis=2)
    w2 = nsa_cmp_w2[l]
    z2 = jnp.zeros_like(w2)
    w2bd = jnp.concatenate([jnp.concatenate([w2, z2], axis=2), jnp.concatenate([z2, w2], axis=2)], axis=1)
    pe = nsa_cmp_pe[l]
    pe_pair = jnp.broadcast_to(jnp.concatenate([pe, pe], axis=-1)[:, :, None, :], (2, CMP_LEN, 16, 2 * DH))
    wa_pad = jnp.concatenate([gla_w_a2[l], jnp.zeros((LANES - GLA_GATE_RANK, GLA_QK), F32)], axis=0).astype(BF16)
    rw_t = router_w.T
    rw_hi = rw_t.astype(BF16)
    rw_lo = (rw_t - rw_hi.astype(F32)).astype(BF16)
    return dict(
        w_big=w_big, w_small=w_small, w_t=w_t, wa_pad=wa_pad,
        b_a=gla_b_a[l].reshape(1, -1), norm_g=gla_norm_g[l].reshape(1, -1),
        cmp_w1bd=w1bd.astype(BF16), cmp_w2bd=w2bd.astype(BF16), cmp_w2bdt=w2bd.transpose(0, 2, 1).astype(BF16),
        cmp_pe_pair=pe_pair.astype(BF16),
        w_bg=w_branch_gla[l].astype(BF16), w_bn=w_branch_nsa[l].astype(BF16), w_out=w_out[l].astype(BF16),
        ln_mix_g=ln_mix_g[l].reshape(1, -1), ln_mix_b=ln_mix_b[l].reshape(1, -1),
        xa_wq=xa_wq[l].astype(BF16), xa_wkv=xa_wkv[l].astype(BF16), xa_wo=xa_wo[l].astype(BF16),
        ln_xa_g=ln_xa_g[l].reshape(1, -1), ln_xa_b=ln_xa_b[l].reshape(1, -1),
        rw_hi=rw_hi, rw_lo=rw_lo, rb=router_b.reshape(-1, 1),
        ln_ffn_g=ln_ffn_g[l].reshape(1, -1), ln_ffn_b=ln_ffn_b[l].reshape(1, -1),
    )


def kernel(x, mem, w_in, gla_w_a2, gla_b_a, gla_norm_g, nsa_cmp_pe, nsa_cmp_w1, nsa_cmp_w2, w_branch_gla, w_branch_nsa, w_out, ln_mix_g, ln_mix_b, xa_wq, xa_wkv, xa_wo, ln_xa_g, ln_xa_b, router_w, router_b, moe_w_in, moe_w_down, ln_ffn_g, ln_ffn_b):
    B, T, d = x.shape
    n = B * T
    assert d == 2048 and mem.shape[1] == MEM_LEN
    assert T % max(TILES.cmp_q, TILES.gla_rows, TILES.sel_q, TILES.xattn_rows) == 0 and WINDOW % TILES.sel_q == 0
    assert n % max(TILES.proj_rows, TILES.slot_cols) == 0
    params = (w_in, gla_w_a2, gla_b_a, gla_norm_g, nsa_cmp_pe, nsa_cmp_w1, nsa_cmp_w2, w_branch_gla, w_branch_nsa,
              w_out, ln_mix_g, ln_mix_b, xa_wq, xa_wkv, xa_wo, ln_xa_g, ln_xa_b, router_w, router_b, moe_w_in,
              moe_w_down, ln_ffn_g, ln_ffn_b)
    slopes = (2.0 ** (-8.0 * jnp.arange(1, NSA_HEADS + 1, dtype=F32) / NSA_HEADS)).astype(F32)
    nc, ns = T // CMP_STRIDE, T // SEL_LEN
    cs = np.arange(nc) * CMP_STRIDE
    ss = np.arange(ns) * SEL_LEN
    ovt = ((cs[None, :] < ss[:, None] + SEL_LEN) & (cs[None, :] + CMP_LEN > ss[:, None])
           & (cs[None, :] + CMP_LEN <= T)).astype(np.float32)
    assert NSA_DH + ns <= LANES
    epad = np.zeros((T, LANES), np.float32)
    epad[np.arange(T), NSA_DH + np.arange(T) // SEL_LEN] = 1.0
    tile_ind = (np.arange(ns)[None, :] // (TILES.sel_q // SEL_LEN) == np.arange(T // TILES.sel_q)[:, None]).astype(np.float32)
    consts = (slopes, jnp.asarray(ovt, BF16), jnp.asarray(epad, BF16), jnp.asarray(tile_ind, BF16))

    xf = x.reshape(n, d)
    xb = xf.astype(BF16)
    mem_b = mem.reshape(B * MEM_LEN, d).astype(BF16)
    moe_w = (moe_w_in, moe_w_down)
    for l in range(DEPTH):
        p = _prep_layer(l, *params)
        xf, xb = _layer(xf, xb, mem_b, p, moe_w, l, consts, B, T)
    return xf.reshape(B, T, d)
```

```python
import functools
from typing import NamedTuple

import jax
import jax.numpy as jnp
import numpy as np
from jax import lax
from jax.experimental import pallas as pl
from jax.experimental.pallas import tpu as pltpu

F32 = jnp.float32
BF16 = jnp.bfloat16
I32 = jnp.int32

DEPTH = 2
MEM_LEN = 256
GLA_HEADS = 4
GLA_DK = 128
GLA_DV = 256
GLA_GATE_RANK = 16
GLA_TAU = 16.0
GLA_CHUNK = 64
NSA_HEADS = 16
NSA_GROUPS = 4
NSA_HPG = NSA_HEADS // NSA_GROUPS
NSA_DH = 64
CMP_LEN = 32
CMP_STRIDE = 16
CMP_HIDDEN = 256
SEL_LEN = 64
SEL_TOPN = 8
WINDOW = 512
XA_HEADS = 4
XA_DH = 128
N_EXPERTS = 16
N_GROUPS = 4
EXPERTS_PER_GROUP = N_EXPERTS // N_GROUPS
TOP_K = 2
D_FF = 1536
DN_ALPHA = float((2 * DEPTH) ** 0.25)
LN_EPS = 1e-5
NEG = -1e30
LOG2E = 1.4426950408889634
FORCE_BONUS = 1e6

GLA_QK = GLA_HEADS * GLA_DK
GLA_V = GLA_HEADS * GLA_DV
NSA_Q = NSA_HEADS * NSA_DH
NSA_KV = NSA_GROUPS * NSA_DH

LANES = 128
VMEM_LIMIT = 56 * 1024 * 1024

COL_MG = 0
COL_GQ = 2 * 2048
COL_GK = COL_GQ + GLA_QK
COL_GV = COL_GK + GLA_QK
COL_GR = COL_GV + GLA_V
COL_KS = COL_GR + GLA_V
COL_KW = COL_KS + NSA_GROUPS * LANES
COL_END = COL_KW + NSA_GROUPS * LANES
SCOL_CK = LANES
SCOL_CV = SCOL_CK + NSA_KV
SCOL_END = SCOL_CV + NSA_KV
TROW_Q = 0
TROW_VS = NSA_Q
TROW_VW = TROW_VS + NSA_KV
TROW_END = TROW_VW + NSA_KV

MOE_BLOCK = 512
FF_TILE = 512
GATE_ROWS = 16


class _Tiles(NamedTuple):
    proj_rows: int = 1024
    proj_cols: int = 1024
    seqs: int = 4
    gla_rows: int = 512
    cmp_q: int = 512
    sel_q: int = 256
    mix_rows: int = 512
    xattn_rows: int = 512
    kv_rows: int = 512
    kv_cols: int = 512
    router_rows: int = 512
    slot_cols: int = 2048
    dispatch_rows: int = 512
    combine_rows: int = 256


TILES = _Tiles()


def _cp(sem):
    return pltpu.CompilerParams(dimension_semantics=sem, vmem_limit_bytes=VMEM_LIMIT)


def _dot(a, b):
    return jnp.dot(a, b, preferred_element_type=F32)


def _dot_nt(a, b):
    return lax.dot_general(a, b, (((1,), (1,)), ((), ())), preferred_element_type=F32)


def _dot_tn(a, b):
    return lax.dot_general(a, b, (((0,), (0,)), ((), ())), preferred_element_type=F32)


def _layer_norm(z, g, b):
    mu = jnp.mean(z, axis=-1, keepdims=True)
    zc = z - mu
    var = jnp.mean(zc * zc, axis=-1, keepdims=True)
    return zc * lax.rsqrt(var + LN_EPS) * g + b


def _mm_kernel(a_ref, b_ref, o_ref):
    o_ref[...] = _dot(a_ref[...], b_ref[...]).astype(o_ref.dtype)


def _matmul(a, b, out_dtype, tm, tn):
    m, k = a.shape
    n = b.shape[1]
    return pl.pallas_call(
        _mm_kernel,
        grid=(m // tm, n // tn),
        in_specs=[pl.BlockSpec((tm, k), lambda i, j: (i, 0)),
                  pl.BlockSpec((k, tn), lambda i, j: (0, j))],
        out_specs=pl.BlockSpec((tm, tn), lambda i, j: (i, j)),
        out_shape=jax.ShapeDtypeStruct((m, n), out_dtype),
        compiler_params=_cp(("parallel", "parallel")),
        name="matmul",
    )(a, b)


def _mm_nt_kernel(wt_ref, x_ref, o_ref):
    o_ref[0] = _dot_nt(wt_ref[...], x_ref[...]).astype(o_ref.dtype)


def _matmul_t(x, wt, B, T, tm, tr):
    n, k = x.shape
    r = wt.shape[0]
    nt = T // tm
    return pl.pallas_call(
        _mm_nt_kernel,
        grid=(n // tm, r // tr),
        in_specs=[pl.BlockSpec((tr, k), lambda i, j: (j, 0)),
                  pl.BlockSpec((tm, k), lambda i, j: (i, 0))],
        out_specs=pl.BlockSpec((1, tr, tm), lambda i, j: (i // nt, j, i % nt)),
        out_shape=jax.ShapeDtypeStruct((B, r, T), BF16),
        compiler_params=_cp(("parallel", "parallel")),
        name="matmul_t",
    )(wt, x)


def _gla_kernel(q_ref, k_ref, v_ref, r_ref, sm_ref, wa_ref, ba_ref, ng_ref, o_ref, st_ref):
    C = GLA_CHUNK
    seqs = q_ref.shape[1]
    n_chunks = q_ref.shape[2] // C

    @pl.when(pl.program_id(1) == 0)
    def _():
        st_ref[...] = jnp.zeros_like(st_ref)

    rowi = lax.broadcasted_iota(I32, (C, GLA_DK), 0)
    tt = lax.broadcasted_iota(I32, (C, C), 0)
    ss = lax.broadcasted_iota(I32, (C, C), 1)
    levels = (1, 2, 4, 8, 16, 32)
    pair_masks = [((tt // (2 * L)) == (ss // (2 * L))) & ((tt & L) != 0) & ((ss & L) == 0) for L in levels]
    diag_mask = tt == ss
    scale = GLA_DK ** -0.5

    def head_chunk(sq, rows, h, z):
        qk_cols = slice(h * GLA_DK, (h + 1) * GLA_DK)
        v_cols = slice(h * GLA_DV, (h + 1) * GLA_DV)
        state = sq * GLA_HEADS + h
        q = q_ref[0, sq, rows, qk_cols].astype(F32) * scale
        k = k_ref[0, sq, rows, qk_cols].astype(F32)
        v = v_ref[0, sq, rows, v_cols]
        g = (jnp.minimum(z, 0.0) - jnp.log1p(jnp.exp(-jnp.abs(z)))) * (1.0 / GLA_TAU)
        incl = g
        tot = g
        att = jnp.where(diag_mask, _dot_nt(q.astype(BF16), k.astype(BF16)), 0.0)
        for L, pm in zip(levels, pair_masks):
            ql = (q * jnp.exp(incl)).astype(BF16)
            kl = (k * jnp.exp(tot - incl)).astype(BF16)
            att = jnp.where(pm, _dot_nt(ql, kl), att)
            upper = (rowi & L) != 0
            from_lower = pltpu.roll(tot, L, 0)
            from_upper = pltpu.roll(tot, C - L, 0)
            incl = incl + jnp.where(upper, from_lower, 0.0)
            tot = tot + jnp.where(upper, from_lower, from_upper)
        qd = (q * jnp.exp(incl)).astype(BF16)
        kd = (k * jnp.exp(tot - incl)).astype(BF16)
        st = st_ref[state]
        o = _dot_nt(qd, st.astype(BF16)) + _dot(att.astype(BF16), v)
        st_ref[state] = st * jnp.exp(tot[0:1, :]) + _dot_tn(v, kd)
        mu = jnp.mean(o, axis=-1, keepdims=True)
        oc = o - mu
        var = jnp.mean(oc * oc, axis=-1, keepdims=True)
        on = oc * lax.rsqrt(var + LN_EPS) * ng_ref[:, v_cols]
        r = r_ref[0, sq, rows, v_cols].astype(F32)
        o_ref[0, sq, rows, v_cols] = (on * (r * jax.nn.sigmoid(r))).astype(o_ref.dtype)

    def chunk(c, carry):
        rows = pl.ds(pl.multiple_of(c * C, C), C)
        for sq in range(seqs):
            z = _dot(sm_ref[0, sq, rows, :].astype(BF16), wa_ref[...]) + ba_ref[...]
            for h in range(GLA_HEADS):
                head_chunk(sq, rows, h, z[:, h * GLA_DK:(h + 1) * GLA_DK])
        return carry

    lax.fori_loop(0, n_chunks, chunk, 0)


def _gla(h_big, h_small, wa_pad, b_a, norm_g, B, T, seqs, tb):
    groups = B // seqs
    hb = h_big.reshape(groups, seqs, T, h_big.shape[1])
    hs = h_small.reshape(groups, seqs, T, h_small.shape[1])
    out = pl.pallas_call(
        _gla_kernel,
        grid=(groups, T // tb),
        in_specs=[
            pl.BlockSpec((1, seqs, tb, GLA_QK), lambda b, j: (b, 0, j, COL_GQ // GLA_QK)),
            pl.BlockSpec((1, seqs, tb, GLA_QK), lambda b, j: (b, 0, j, COL_GK // GLA_QK)),
            pl.BlockSpec((1, seqs, tb, GLA_V), lambda b, j: (b, 0, j, COL_GV // GLA_V)),
            pl.BlockSpec((1, seqs, tb, GLA_V), lambda b, j: (b, 0, j, COL_GR // GLA_V)),
            pl.BlockSpec((1, seqs, tb, LANES), lambda b, j: (b, 0, j, 0)),
            pl.BlockSpec((LANES, GLA_QK), lambda b, j: (0, 0)),
            pl.BlockSpec((1, GLA_QK), lambda b, j: (0, 0)),
            pl.BlockSpec((1, GLA_V), lambda b, j: (0, 0)),
        ],
        out_specs=pl.BlockSpec((1, seqs, tb, GLA_V), lambda b, j: (b, 0, j, 0)),
        out_shape=jax.ShapeDtypeStruct((groups, seqs, T, GLA_V), BF16),
        scratch_shapes=[pltpu.VMEM((seqs * GLA_HEADS, GLA_DV, GLA_DK), F32)],
        compiler_params=_cp(("parallel", "arbitrary")),
        name="gla",
    )(hb, hb, hb, hb, hs, wa_pad, b_a, norm_g)
    return out.reshape(B * T, GLA_V)


def _compress_kernel(x_ref, w1_ref, w2_ref, w2t_ref, pe_ref, o_ref, ot_ref):
    nc = x_ref.shape[0] // CMP_STRIDE
    hid_w = w1_ref.shape[3]
    a = jnp.zeros((nc, hid_w), F32)
    bm = jnp.zeros((nc, hid_w), F32)
    c = jnp.zeros((pe_ref.shape[2], hid_w), F32)
    for l in range(CMP_STRIDE):
        xl = x_ref[pl.ds(l, nc, stride=CMP_STRIDE), :].astype(BF16)
        a = a + _dot(xl, w1_ref[0, l])
        bm = bm + _dot(xl, w1_ref[0, CMP_STRIDE + l])
    for l in range(CMP_LEN):
        c = c + _dot(pe_ref[0, l], w1_ref[0, l])
    hid = a + pltpu.roll(bm, nc - 1, 0) + c[0:1, :]
    act = jax.nn.gelu(hid).astype(BF16)
    o_ref[0, 0, 0] = _dot(act, w2_ref[0]).astype(o_ref.dtype)
    ot_ref[0, 0, 0] = _dot_nt(w2t_ref[0], act).astype(ot_ref.dtype)


def _compress(h_small, w1bd, w2bd, w2bdt, pe_pair, B, T):
    nc = T // CMP_STRIDE
    pairs = NSA_GROUPS // 2
    return pl.pallas_call(
        _compress_kernel,
        grid=(B, 2, pairs),
        in_specs=[
            pl.BlockSpec((T, LANES), lambda b, s, j: (b, SCOL_CK // LANES + s * pairs + j)),
            pl.BlockSpec((1,) + w1bd.shape[1:], lambda b, s, j: (s, 0, 0, 0)),
            pl.BlockSpec((1,) + w2bd.shape[1:], lambda b, s, j: (s, 0, 0)),
            pl.BlockSpec((1,) + w2bdt.shape[1:], lambda b, s, j: (s, 0, 0)),
            pl.BlockSpec((1,) + pe_pair.shape[1:], lambda b, s, j: (s, 0, 0, 0)),
        ],
        out_specs=[
            pl.BlockSpec((1, 1, 1, nc, LANES), lambda b, s, j: (b, s, j, 0, 0)),
            pl.BlockSpec((1, 1, 1, LANES, nc), lambda b, s, j: (b, s, j, 0, 0)),
        ],
        out_shape=[jax.ShapeDtypeStruct((B, 2, pairs, nc, LANES), BF16),
                   jax.ShapeDtypeStruct((B, 2, pairs, LANES, nc), BF16)],
        compiler_params=_cp(("parallel", "parallel", "parallel")),
        name="nsa_compress",
    )(h_small, w1bd, w2bd, w2bdt, pe_pair)


def _cmp_select_kernel(slopes_ref, qt_ref, kc_ref, vct_ref, ovt_ref, ind_ref, ocmp_ref, mb_ref, kt_ref, qpad_ref):
    g = pl.program_id(1)
    i = pl.program_id(2)
    streams = qt_ref.shape[1]
    tq = qt_ref.shape[3]
    nc = kc_ref.shape[4]
    ns = mb_ref.shape[3]
    dh = NSA_DH
    t0 = i * tq
    wide = NSA_HPG * tq
    tpos = t0 + (lax.broadcasted_iota(I32, (nc, wide), 1) & (tq - 1))
    nidx = lax.broadcasted_iota(I32, (nc, wide), 0)
    mask_c = (nidx * CMP_STRIDE + (CMP_LEN - 1)) <= tpos
    absd = jnp.abs(tpos.astype(F32) - (nidx.astype(F32) * CMP_STRIDE + 0.5 * (CMP_LEN - 1)))
    srow = jnp.concatenate([jnp.full((1, tq), slopes_ref[g * NSA_HPG + hh], F32) for hh in range(NSA_HPG)], axis=1)
    lower = g % 2 == 0
    j = lax.broadcasted_iota(I32, (ns, tq), 0)
    tp = t0 + lax.broadcasted_iota(I32, (ns, tq), 1)
    cur = tp // SEL_LEN
    forced = (j == 0) | (j == cur) | (j == cur - 1)
    valid = j * SEL_LEN <= tp
    for u in range(streams):
        kc = kc_ref[0, u, 0, 0]
        vct = jnp.where(lower, vct_ref[0, u, 0, 0, 0:dh, :], vct_ref[0, u, 0, 0, dh:2 * dh, :])
        for hh in range(NSA_HPG):
            q = qt_ref[0, u, hh * dh:(hh + 1) * dh, :] * jnp.asarray(dh ** -0.5, BF16)
            zero = jnp.zeros_like(q)
            qpad_ref[u, 0:dh, hh * tq:(hh + 1) * tq] = jnp.where(lower, q, zero)
            qpad_ref[u, dh:2 * dh, hh * tq:(hh + 1) * tq] = jnp.where(lower, zero, q)
        s = _dot(kc, qpad_ref[u]) - srow * absd
        s = jnp.where(mask_c, s, NEG)
        e = jnp.exp(s - jnp.max(s, axis=0, keepdims=True))
        p = jnp.where(mask_c, e * (1.0 / jnp.sum(e, axis=0, keepdims=True)), 0.0)
        o = _dot(vct, p.astype(BF16))
        psum = jnp.zeros((nc, tq), F32)
        for hh in range(NSA_HPG):
            ocmp_ref[0, u, hh * dh:(hh + 1) * dh, :] = o[:, hh * tq:(hh + 1) * tq].astype(ocmp_ref.dtype)
            psum = psum + p[:, hh * tq:(hh + 1) * tq]
        p_hi = psum.astype(BF16)
        p_lo = (psum - p_hi.astype(F32)).astype(BF16)
        imp = _dot(ovt_ref[...], p_hi) + _dot(ovt_ref[...], p_lo)
        score = jnp.where(valid, imp + jnp.where(forced, FORCE_BONUS, 0.0), NEG)
        rank = jnp.zeros((ns, tq), F32)
        for jp in range(ns):
            row = score[jp:jp + 1, :]
            beats = (row > score) | ((row == score) & (j > jp))
            rank = rank + jnp.where(beats, 1.0, 0.0)
        keep = valid & (rank < float(min(SEL_TOPN, ns)))
        mb_ref[0, u, 0] = jnp.where(keep, 0.0, NEG).astype(mb_ref.dtype)
        kt_ref[0, u, 0] = _dot(ind_ref[...], jnp.where(keep, 1.0, 0.0).astype(BF16))


def _cmp_select(slopes, h_t, kcmp, kcmp_t, ovt, tile_ind, B, T, tq, streams):
    nc = T // CMP_STRIDE
    ns = T // SEL_LEN
    nkt = tile_ind.shape[0]
    G = NSA_GROUPS
    grp_rows = NSA_HPG * NSA_DH
    nbg = B // streams
    h_t4 = h_t.reshape(nbg, streams, h_t.shape[1], T)
    kcmp6 = kcmp.reshape((nbg, streams) + kcmp.shape[1:])
    kcmp_t6 = kcmp_t.reshape((nbg, streams) + kcmp_t.shape[1:])
    grid_spec = pltpu.PrefetchScalarGridSpec(
        num_scalar_prefetch=1,
        grid=(nbg, G, T // tq),
        in_specs=[
            pl.BlockSpec((1, streams, grp_rows, tq), lambda b, g, i, s: (b, 0, TROW_Q // grp_rows + g, i)),
            pl.BlockSpec((1, streams, 1, 1, nc, LANES), lambda b, g, i, s: (b, 0, 0, g // 2, 0, 0)),
            pl.BlockSpec((1, streams, 1, 1, LANES, nc), lambda b, g, i, s: (b, 0, 1, g // 2, 0, 0)),
            pl.BlockSpec((ns, nc), lambda b, g, i, s: (0, 0)),
            pl.BlockSpec((nkt, ns), lambda b, g, i, s: (0, 0)),
        ],
        out_specs=[
            pl.BlockSpec((1, streams, grp_rows, tq), lambda b, g, i, s: (b, 0, g, i)),
            pl.BlockSpec((1, streams, 1, ns, tq), lambda b, g, i, s: (b, 0, g, 0, i)),
            pl.BlockSpec((1, streams, 1, nkt, tq), lambda b, g, i, s: (b, 0, g, 0, i)),
        ],
        scratch_shapes=[pltpu.VMEM((streams, LANES, NSA_HPG * tq), BF16)],
    )
    ocmp, mb, in_tile = pl.pallas_call(
        _cmp_select_kernel,
        grid_spec=grid_spec,
        out_shape=[jax.ShapeDtypeStruct((nbg, streams, NSA_Q, T), BF16),
                   jax.ShapeDtypeStruct((nbg, streams, G, ns, T), BF16),
                   jax.ShapeDtypeStruct((nbg, streams, G, nkt, T), F32)],
        compiler_params=_cp(("parallel", "parallel", "parallel")),
        name="nsa_cmp_select",
    )(slopes, h_t4, kcmp6, kcmp_t6, ovt, tile_ind)
    return ocmp.reshape(B, NSA_Q, T), mb.reshape(B, G, ns, T), in_tile.reshape(B, G, nkt, T)


def _sel_win_kernel(slopes_ref, flags_ref, qt_ref, ks_ref, kw_ref, vs_ref, vw_ref, epad_ref, mb_ref, ocmp_ref, gt_ref,
                    o_ref, qaug_ref, m_ref, acc_ref, srow_ref, bias_ref, s_ref, p_ref, alpha_ref,
                    kall_ref, vall_ref, tiles_ref, *, n_tiles):
    g = pl.program_id(1)
    i = pl.program_id(2)
    streams = qt_ref.shape[1]
    tq = qt_ref.shape[3]
    tk = tq
    ns = mb_ref.shape[3]
    dh = NSA_DH
    wide = NSA_HPG * tq
    BIG = -NEG

    @pl.when(i == 0)
    def _():
        srow = jnp.concatenate([jnp.full((1, tq), slopes_ref[g * NSA_HPG + hh] * LOG2E, F32)
                                for hh in range(NSA_HPG)], axis=1)
        srow_ref[...] = srow
        lane = lax.broadcasted_iota(I32, (tk, wide), 1) & (tq - 1)
        dist0 = (lane - lax.broadcasted_iota(I32, (tk, wide), 0)).astype(F32)
        sd0 = srow * dist0
        bias_ref[0] = sd0
        bias_ref[1] = sd0 + jnp.where(dist0 >= 0.0, 0.0, BIG)
        bias_ref[2] = sd0 + jnp.where(dist0 < 0.0, 0.0, BIG)
        bias_ref[3] = jnp.full((tk, wide), BIG, F32)
        extra = jnp.where(lax.broadcasted_iota(I32, (vall_ref.shape[2] - dh, vall_ref.shape[3]), 0) == 0, 1.0, 0.0)
        for u in range(streams):
            kall_ref[u, 0] = ks_ref[0, u] + epad_ref[...]
            kall_ref[u, 1] = kw_ref[0, u]
            vall_ref[u, 0, 0:dh, :] = vs_ref[0, u]
            vall_ref[u, 1, 0:dh, :] = vw_ref[0, u]
            vall_ref[u, 0, dh:, :] = extra.astype(BF16)
            vall_ref[u, 1, dh:, :] = extra.astype(BF16)

    for u in range(streams):
        for hh in range(NSA_HPG):
            cols = slice(hh * tq, (hh + 1) * tq)
            q = qt_ref[0, u, hh * dh:(hh + 1) * dh, :].astype(F32) * (dh ** -0.5 * LOG2E)
            qaug_ref[u, 0:dh, cols] = q.astype(BF16)
            qaug_ref[u, dh:dh + ns, cols] = mb_ref[0, u, 0]
            qaug_ref[u, dh + ns:, cols] = jnp.zeros((qaug_ref.shape[1] - dh - ns, tq), BF16)

    m_ref[...] = jnp.full(m_ref.shape, NEG, F32)
    acc_ref[...] = jnp.zeros(acc_ref.shape, F32)

    n_back = WINDOW // tk
    n_win = jnp.minimum(i, n_back) + 1
    n_sel = []
    for u in range(streams):
        flag_base = (((pl.program_id(0) * streams + u) * NSA_GROUPS + g) * n_tiles + i) * n_tiles
        cnt = jnp.int32(0)
        for kb_static in range(n_tiles - 1):
            active = (kb_static < i) & (flags_ref[flag_base + kb_static] != 0)
            tiles_ref[u, cnt] = kb_static
            cnt = cnt + active.astype(I32)
        tiles_ref[u, cnt] = i
        n_sel.append(cnt + 1)
    n_steps = n_sel[0] + n_win
    for u in range(1, streams):
        n_steps = jnp.maximum(n_steps, n_sel[u] + n_win)

    def describe(u, n):
        n = jnp.maximum(n, 0)
        is_win = n >= n_sel[u]
        kb_sel = tiles_ref[u, jnp.minimum(n, n_sel[u] - 1)]
        kb = jnp.clip(jnp.where(is_win, i - n_win + 1 + (n - n_sel[u]), kb_sel), 0, i)
        mode = jnp.where(kb == i, 1, jnp.where(is_win & (kb == i - n_back), 2, 0))
        mode = jnp.where(n >= n_sel[u] + n_win, 3, mode)
        return is_win.astype(I32), kb, mode

    def scores(u, n, slot):
        br, kb, _ = describe(u, n)
        s0 = pl.multiple_of(kb * tk, tk)
        s_ref[u, slot] = _dot(kall_ref[u, br, pl.ds(s0, tk), :], qaug_ref[u])

    def softmax(u, n, slot):
        br, kb, mode = describe(u, n)
        crow = srow_ref[...] * ((i - kb) * tk).astype(F32)
        s = s_ref[u, slot] - bias_ref[mode]
        m_old = m_ref[u, br]
        m_new = jnp.maximum(m_old, jnp.max(s, axis=0, keepdims=True) - crow)
        alpha = jnp.exp2(m_old - m_new)
        p = jnp.exp2(s - (m_new + crow))
        m_ref[u, br] = m_new
        alpha_ref[u, slot] = alpha
        p_ref[u, slot] = p.astype(BF16)

    def weighted_values(u, n, slot):
        br, kb, _ = describe(u, n)
        s0 = pl.multiple_of(kb * tk, tk)
        acc_ref[u, br] = (alpha_ref[u, slot] * acc_ref[u, br]
                          + _dot(vall_ref[u, br, :, pl.ds(s0, tk)], p_ref[u, slot]))

    def each(fn, n, slot):
        for u in range(streams):
            fn(u, n, slot)

    for u in range(streams):
        p_ref[u, 1] = jnp.zeros(p_ref.shape[2:], BF16)
        alpha_ref[u, 1] = jnp.ones(alpha_ref.shape[2:], F32)
    each(scores, 0, 0)

    def pair(j, carry):
        n = 2 * j
        each(scores, n + 1, 1)
        each(softmax, n, 0)
        each(weighted_values, n - 1, 1)
        each(scores, n + 2, 0)
        each(softmax, n + 1, 1)
        each(weighted_values, n, 0)
        return carry

    n_pairs = n_steps // 2
    lax.fori_loop(0, n_pairs, pair, 0)
    each(weighted_values, 2 * n_pairs - 1, 1)

    @pl.when(n_steps % 2 == 1)
    def _():
        each(softmax, n_steps - 1, 0)
        each(weighted_values, n_steps - 1, 0)

    for u in range(streams):
        def gate_row(branch, u=u):
            rows = [gt_ref[0, u, 0, 3 * hh + branch:3 * hh + branch + 1, :] for hh in range(NSA_HPG)]
            return jax.nn.sigmoid(jnp.concatenate(rows, axis=1))

        o = (acc_ref[u, 0, 0:dh, :] * (gate_row(1) / acc_ref[u, 0, dh:dh + 1, :])
             + acc_ref[u, 1, 0:dh, :] * (gate_row(2) / acc_ref[u, 1, dh:dh + 1, :]))
        ocmp = jnp.concatenate([ocmp_ref[0, u, hh * dh:(hh + 1) * dh, :] for hh in range(NSA_HPG)],
                               axis=1).astype(F32)
        o = o + gate_row(0) * ocmp
        o_heads = jnp.concatenate([o[:, hh * tq:(hh + 1) * tq] for hh in range(NSA_HPG)], axis=0)
        o_ref[0, u] = o_heads.T.astype(o_ref.dtype)


def _sel_win(slopes, tile_flags, h_t, h_big, epad, mb, ocmp_t, gates_t, B, T, tq, streams):
    ns = T // SEL_LEN
    kaug = LANES
    G = NSA_GROUPS
    grp_rows = NSA_HPG * NSA_DH
    nq = T // tq
    nbg = B // streams
    wide = NSA_HPG * tq
    h_t5 = h_t.reshape(nbg, streams, h_t.shape[1], T)
    h_big5 = h_big.reshape(nbg, streams, T, h_big.shape[1])
    mb5 = mb.reshape(nbg, streams, G, ns, T)
    ocmp5 = ocmp_t.reshape(nbg, streams, NSA_Q, T)
    gates5 = gates_t.reshape(nbg, streams, G, GATE_ROWS, T)
    grid_spec = pltpu.PrefetchScalarGridSpec(
        num_scalar_prefetch=2,
        grid=(nbg, G, nq),
        in_specs=[
            pl.BlockSpec((1, streams, grp_rows, tq), lambda b, g, i, s, f: (b, 0, TROW_Q // grp_rows + g, i)),
            pl.BlockSpec((1, streams, T, LANES), lambda b, g, i, s, f: (b, 0, 0, COL_KS // LANES + g)),
            pl.BlockSpec((1, streams, T, LANES), lambda b, g, i, s, f: (b, 0, 0, COL_KW // LANES + g)),
            pl.BlockSpec((1, streams, NSA_DH, T), lambda b, g, i, s, f: (b, 0, TROW_VS // NSA_DH + g, 0)),
            pl.BlockSpec((1, streams, NSA_DH, T), lambda b, g, i, s, f: (b, 0, TROW_VW // NSA_DH + g, 0)),
            pl.BlockSpec((T, LANES), lambda b, g, i, s, f: (0, 0)),
            pl.BlockSpec((1, streams, 1, ns, tq), lambda b, g, i, s, f: (b, 0, g, 0, i)),
            pl.BlockSpec((1, streams, grp_rows, tq), lambda b, g, i, s, f: (b, 0, g, i)),
            pl.BlockSpec((1, streams, 1, GATE_ROWS, tq), lambda b, g, i, s, f: (b, 0, g, 0, i)),
        ],
        out_specs=pl.BlockSpec((1, streams, tq, grp_rows), lambda b, g, i, s, f: (b, 0, i, g)),
        scratch_shapes=[
            pltpu.VMEM((streams, kaug, wide), BF16),
            pltpu.VMEM((streams, 2, 1, wide), F32),
            pltpu.VMEM((streams, 2, NSA_DH + 16, wide), F32),
            pltpu.VMEM((1, wide), F32),
            pltpu.VMEM((4, tq, wide), F32),
            pltpu.VMEM((streams, 2, tq, wide), F32),
            pltpu.VMEM((streams, 2, tq, wide), BF16),
            pltpu.VMEM((streams, 2, 1, wide), F32),
            pltpu.VMEM((streams, 2, T, kaug), BF16),
            pltpu.VMEM((streams, 2, NSA_DH + 16, T), BF16),
            pltpu.SMEM((streams, nq), I32),
        ],
    )
    out = pl.pallas_call(
        functools.partial(_sel_win_kernel, n_tiles=nq),
        grid_spec=grid_spec,
        out_shape=jax.ShapeDtypeStruct((nbg, streams, T, NSA_Q), BF16),
        compiler_params=_cp(("parallel", "parallel", "arbitrary")),
        name="nsa_sel_win",
    )(slopes, tile_flags, h_t5, h_big5, h_big5, h_t5, h_t5, epad, mb5, ocmp5, gates5)
    return out.reshape(B * T, NSA_Q)


def _mix_kernel(og_ref, on_ref, mg1_ref, mg2_ref, x_ref, wg_ref, wn_ref, wo_ref, lg_ref, lb_ref,
                x1_ref, x1b_ref):
    g1 = _dot(og_ref[...], wg_ref[...])
    g2 = _dot(on_ref[...], wn_ref[...])
    merged = (jax.nn.sigmoid(mg1_ref[...].astype(F32)) * g1
              + jax.nn.sigmoid(mg2_ref[...].astype(F32)) * g2)
    y = _dot(merged.astype(BF16), wo_ref[...])
    x1 = _layer_norm(DN_ALPHA * x_ref[...] + y, lg_ref[...], lb_ref[...])
    x1_ref[...] = x1
    x1b_ref[...] = x1.astype(BF16)


def _const_spec(shape):
    nd = len(shape)
    return pl.BlockSpec(shape, lambda *_: (0,) * nd, pipeline_mode=pl.Buffered(1))


def _mix(o_gla, o_nsa, h_big, x, wg, wn, wo, lg, lb, tm):
    n, d = x.shape
    return pl.pallas_call(
        _mix_kernel,
        grid=(n // tm,),
        in_specs=[
            pl.BlockSpec((tm, GLA_V), lambda i: (i, 0)),
            pl.BlockSpec((tm, NSA_Q), lambda i: (i, 0)),
            pl.BlockSpec((tm, d), lambda i: (i, 0)),
            pl.BlockSpec((tm, d), lambda i: (i, 1)),
            pl.BlockSpec((tm, d), lambda i: (i, 0)),
            _const_spec(wg.shape), _const_spec(wn.shape), _const_spec(wo.shape),
            _const_spec(lg.shape), _const_spec(lb.shape),
        ],
        out_specs=[pl.BlockSpec((tm, d), lambda i: (i, 0)), pl.BlockSpec((tm, d), lambda i: (i, 0))],
        out_shape=[jax.ShapeDtypeStruct((n, d), F32), jax.ShapeDtypeStruct((n, d), BF16)],
        compiler_params=_cp(("parallel",)),
        name="mix_ln",
    )(o_gla, o_nsa, h_big, h_big, x, wg, wn, wo, lg, lb)


def _xattn_kernel(x_ref, xb_ref, kv_ref, wq_ref, wo_ref, lg_ref, lb_ref, x2_ref):
    hd = XA_HEADS * XA_DH
    q = (_dot(xb_ref[...], wq_ref[...]) * (XA_DH ** -0.5)).astype(BF16)
    outs = []
    for h in range(XA_HEADS):
        kh = kv_ref[0, :, h * XA_DH:(h + 1) * XA_DH]
        vh = kv_ref[0, :, hd + h * XA_DH:hd + (h + 1) * XA_DH]
        s = _dot_nt(q[:, h * XA_DH:(h + 1) * XA_DH], kh)
        e = jnp.exp(s - jnp.max(s, axis=-1, keepdims=True))
        p = e / jnp.sum(e, axis=-1, keepdims=True)
        outs.append(_dot(p.astype(BF16), vh).astype(BF16))
    o = jnp.concatenate(outs, axis=-1)
    y = _dot(o, wo_ref[...])
    x2_ref[...] = _layer_norm(DN_ALPHA * x_ref[...] + y, lg_ref[...], lb_ref[...])


def _xattn(x1, x1b, kv, wq, wo, lg, lb, B, T, tm):
    n, d = x1.shape
    nt = T // tm
    return pl.pallas_call(
        _xattn_kernel,
        grid=(B, nt),
        in_specs=[
            pl.BlockSpec((tm, d), lambda b, i: (b * nt + i, 0)),
            pl.BlockSpec((tm, d), lambda b, i: (b * nt + i, 0)),
            pl.BlockSpec((1,) + kv.shape[1:], lambda b, i: (b, 0, 0)),
            _const_spec(wq.shape), _const_spec(wo.shape), _const_spec(lg.shape), _const_spec(lb.shape),
        ],
        out_specs=pl.BlockSpec((tm, d), lambda b, i: (b * nt + i, 0)),
        out_shape=jax.ShapeDtypeStruct((n, d), F32),
        compiler_params=_cp(("parallel", "parallel")),
        name="xattn_ln",
    )(x1, x1b, kv, wq, wo, lg, lb)


def _router_kernel(x_ref, wh_ref, wl_ref, rb_ref, e_ref, gate_ref, rank_ref, cnt_ref, carry_ref):
    i = pl.program_id(0)
    tr = x_ref.shape[0]
    E = N_EXPERTS

    @pl.when(i == 0)
    def _():
        carry_ref[...] = jnp.zeros_like(carry_ref)

    x = x_ref[...]
    x_hi = x.astype(BF16)
    x_lo = (x - x_hi.astype(F32)).astype(BF16)
    wh = wh_ref[...]
    logits = _dot_nt(wh, x_hi) + _dot_nt(wh, x_lo) + _dot_nt(wl_ref[...], x_hi)
    biased = logits + rb_ref[...]
    rows = [biased[e:e + 1, :] for e in range(E)]
    raw = [logits[e:e + 1, :] for e in range(E)]
    best_score = None
    best = None
    for gi in range(N_GROUPS):
        v = rows[gi * EXPERTS_PER_GROUP:(gi + 1) * EXPERTS_PER_GROUP]
        sc = None
        for a in range(EXPERTS_PER_GROUP):
            for b in range(a + 1, EXPERTS_PER_GROUP):
                pair = v[a] + v[b]
                sc = pair if sc is None else jnp.maximum(sc, pair)
        if best is None:
            best_score, best = sc, jnp.zeros((1, tr), I32)
        else:
            better = sc > best_score
            best_score = jnp.where(better, sc, best_score)
            best = jnp.where(better, gi, best)

    def pick(vals):
        out = vals[0:EXPERTS_PER_GROUP]
        for gi in range(1, N_GROUPS):
            out = [jnp.where(best == gi, vals[gi * EXPERTS_PER_GROUP + a], out[a]) for a in range(EXPERTS_PER_GROUP)]
        return out

    w = pick(rows)
    lraw = pick(raw)
    i1 = jnp.zeros((1, tr), I32)
    v1 = w[0]
    l1 = lraw[0]
    for a in range(1, EXPERTS_PER_GROUP):
        better = w[a] > v1
        v1 = jnp.where(better, w[a], v1)
        l1 = jnp.where(better, lraw[a], l1)
        i1 = jnp.where(better, a, i1)
    i2 = jnp.full((1, tr), -1, I32)
    v2 = jnp.full((1, tr), -jnp.inf, F32)
    l2 = jnp.zeros((1, tr), F32)
    for a in range(EXPERTS_PER_GROUP):
        better = (i1 != a) & ((w[a] > v2) | (i2 < 0))
        v2 = jnp.where(better, w[a], v2)
        l2 = jnp.where(better, lraw[a], l2)
        i2 = jnp.where(better, a, i2)
    e1 = best * EXPERTS_PER_GROUP + i1
    e2 = best * EXPERTS_PER_GROUP + i2
    mx = jnp.maximum(l1, l2)
    p1 = jnp.exp(l1 - mx)
    p2 = jnp.exp(l2 - mx)
    den = p1 + p2
    e_ref[0:1, :] = e1
    e_ref[1:2, :] = e2
    gate_ref[0:1, :] = p1 / den
    gate_ref[1:2, :] = p2 / den
    eidx = lax.broadcasted_iota(I32, (E, tr), 0)
    is1 = eidx == e1
    is2 = eidx == e2
    member = jnp.where(is1 | is2, 1.0, 0.0)
    uu = lax.broadcasted_iota(I32, (tr, tr), 0)
    tt = lax.broadcasted_iota(I32, (tr, tr), 1)
    tri = jnp.where(uu <= tt, 1.0, 0.0).astype(BF16)
    incl = _dot(member.astype(BF16), tri)
    excl = carry_ref[:, 0:1] + incl - member
    rank_ref[0:1, :] = jnp.sum(jnp.where(is1, excl, 0.0), axis=0, keepdims=True).astype(I32)
    rank_ref[1:2, :] = jnp.sum(jnp.where(is2, excl, 0.0), axis=0, keepdims=True).astype(I32)
    new_carry = carry_ref[...] + jnp.sum(member, axis=1, keepdims=True)
    carry_ref[...] = new_carry
    cnt_ref[...] = new_carry


def _router(x2, rw_hi, rw_lo, rb, tr):
    n, d = x2.shape
    E = N_EXPERTS
    return pl.pallas_call(
        _router_kernel,
        grid=(n // tr,),
        in_specs=[
            pl.BlockSpec((tr, d), lambda i: (i, 0)),
            pl.BlockSpec((E, d), lambda i: (0, 0)),
            pl.BlockSpec((E, d), lambda i: (0, 0)),
            pl.BlockSpec((E, 1), lambda i: (0, 0)),
        ],
        out_specs=[
            pl.BlockSpec((2, tr), lambda i: (0, i)),
            pl.BlockSpec((2, tr), lambda i: (0, i)),
            pl.BlockSpec((2, tr), lambda i: (0, i)),
            pl.BlockSpec((E, LANES), lambda i: (0, 0)),
        ],
        out_shape=[jax.ShapeDtypeStruct((2, n), I32), jax.ShapeDtypeStruct((2, n), F32),
                   jax.ShapeDtypeStruct((2, n), I32), jax.ShapeDtypeStruct((E, LANES), F32)],
        scratch_shapes=[pltpu.VMEM((E, LANES), F32)],
        compiler_params=_cp(("arbitrary",)),
        name="moe_router",
    )(x2, rw_hi, rw_lo, rb)


def _slot_kernel(ps_ref, e_ref, rank_ref, slot_ref):
    e = e_ref[...]
    start = jnp.zeros(e.shape, I32)
    for ex in range(N_EXPERTS):
        start = jnp.where(e == ex, ps_ref[ex], start)
    slot_ref[...] = start + rank_ref[...]


def _slots(pad_start, e, rank, ts):
    n = e.shape[1]
    grid_spec = pltpu.PrefetchScalarGridSpec(
        num_scalar_prefetch=1,
        grid=(n // ts,),
        in_specs=[pl.BlockSpec((TOP_K, ts), lambda i, s: (0, i)), pl.BlockSpec((TOP_K, ts), lambda i, s: (0, i))],
        out_specs=pl.BlockSpec((TOP_K, ts), lambda i, s: (0, i)),
    )
    return pl.pallas_call(
        _slot_kernel,
        grid_spec=grid_spec,
        out_shape=jax.ShapeDtypeStruct((TOP_K, n), I32),
        compiler_params=_cp(("parallel",)),
        name="moe_slots",
    )(pad_start, e, rank)


def _dispatch_kernel(pe_ref, s0_ref, s1_ref, x_hbm, buf_hbm, zero_ref, xs_ref, sem, load_sems, scat_sems):
    i = pl.program_id(0)
    n_steps = pl.num_programs(0)
    td = s0_ref.shape[0]
    slots = (s0_ref, s1_ref)
    n_slots = xs_ref.shape[0]

    def load(tile):
        slot = tile % n_slots
        start = pl.multiple_of(tile * td, td)
        return pltpu.make_async_copy(x_hbm.at[pl.ds(start, td), :], xs_ref.at[slot], load_sems.at[slot])

    def scatter_wait(tile):
        slot = tile % n_slots
        for _ in range(TOP_K):
            pltpu.make_async_copy(xs_ref.at[slot], buf_hbm.at[pl.ds(0, td), :], scat_sems.at[slot]).wait()

    @pl.when(i == 0)
    def _():
        load(0).start()
        pl.when(n_steps > 1)(lambda: load(1).start())
        zero_ref[...] = jnp.zeros_like(zero_ref)

        def zero_copy(ex):
            last = pl.multiple_of(jnp.maximum(pe_ref[ex] - MOE_BLOCK, 0), MOE_BLOCK)
            return pltpu.make_async_copy(zero_ref, buf_hbm.at[pl.ds(last, MOE_BLOCK), :], sem)

        def nonempty(ex):
            return pe_ref[ex] > (pe_ref[ex - 1] if ex > 0 else 0)

        n_blocks = buf_hbm.shape[0] // MOE_BLOCK
        first_unused = pe_ref[N_EXPERTS - 1] // MOE_BLOCK

        def tail_copy(k):
            row = pl.multiple_of((first_unused + k) * MOE_BLOCK, MOE_BLOCK)
            return pltpu.make_async_copy(zero_ref, buf_hbm.at[pl.ds(row, MOE_BLOCK), :], sem)

        for ex in range(N_EXPERTS):
            pl.when(nonempty(ex))(lambda ex=ex: zero_copy(ex).start())
            pl.when(first_unused + ex < n_blocks)(lambda ex=ex: tail_copy(ex).start())
        for ex in range(N_EXPERTS):
            pl.when(nonempty(ex))(lambda ex=ex: zero_copy(ex).wait())
            pl.when(first_unused + ex < n_blocks)(lambda ex=ex: tail_copy(ex).wait())

    cur = i % n_slots
    load(i).wait()

    def issue(t, carry):
        for kk in range(TOP_K):
            dest = slots[kk][t]
            pltpu.make_async_copy(xs_ref.at[cur, pl.ds(t, 1), :], buf_hbm.at[pl.ds(dest, 1), :],
                                  scat_sems.at[cur]).start()
        return carry

    lax.fori_loop(0, td, issue, 0, unroll=8)

    @pl.when(i > 0)
    def _():
        scatter_wait(i - 1)

    @pl.when(i + 2 < n_steps)
    def _():
        load(i + 2).start()

    @pl.when(i == n_steps - 1)
    def _():
        scatter_wait(i)


def _dispatch(pad_end, slot0, slot1, x2, n_rows, td):
    n, d = x2.shape
    grid_spec = pltpu.PrefetchScalarGridSpec(
        num_scalar_prefetch=1,
        grid=(n // td,),
        in_specs=[
            pl.BlockSpec((td,), lambda i, s: (i,), memory_space=pltpu.SMEM),
            pl.BlockSpec((td,), lambda i, s: (i,), memory_space=pltpu.SMEM),
            pl.BlockSpec(memory_space=pl.ANY),
        ],
        out_specs=pl.BlockSpec(memory_space=pl.ANY),
        scratch_shapes=[pltpu.VMEM((MOE_BLOCK, d), F32), pltpu.VMEM((3, td, d), F32), pltpu.SemaphoreType.DMA(()),
                        pltpu.SemaphoreType.DMA((3,)), pltpu.SemaphoreType.DMA((3,))],
    )
    return pl.pallas_call(
        _dispatch_kernel,
        grid_spec=grid_spec,
        out_shape=jax.ShapeDtypeStruct((n_rows, d), F32),
        compiler_params=_cp(("arbitrary",)),
        name="moe_dispatch",
    )(pad_end, slot0, slot1, x2)


def _expert_kernel(be_ref, nb_ref, x_ref, win_hbm, wdn_hbm, y_ref, xb_ref, wa_s, wu_s, wd_s, sa, su, sd, sems,
                   *, layer):
    b = pl.program_id(0)
    nf = D_FF // FF_TILE
    n_used = nb_ref[0]
    e = be_ref[b]
    e_prev = be_ref[jnp.maximum(b - 1, 0)]
    e_next = be_ref[jnp.minimum(b + 1, pl.num_programs(0) - 1)]
    active = b < n_used
    is_first = active & ((b == 0) | (e_prev != e))
    feeds_next = active & (b + 1 < n_used) & (e_next != e)

    def tile_copies(ex, f):
        lo = f * FF_TILE
        return (pltpu.make_async_copy(win_hbm.at[layer, ex, :, pl.ds(lo, FF_TILE)], sa, sems.at[0]),
                pltpu.make_async_copy(win_hbm.at[layer, ex, :, pl.ds(D_FF + lo, FF_TILE)], su, sems.at[1]),
                pltpu.make_async_copy(wdn_hbm.at[layer, ex, pl.ds(lo, FF_TILE), :], sd, sems.at[2]))

    def start(ex, f):
        for c in tile_copies(ex, f):
            c.start()

    def finish(ex, f):
        for c in tile_copies(ex, f):
            c.wait()
        wa_s[f] = sa[...].astype(BF16)
        wu_s[f] = su[...].astype(BF16)
        wd_s[f] = sd[...].astype(BF16)

    @pl.when(b == 0)
    def _():
        for f in range(nf - 1):
            start(e, f)
            finish(e, f)
        start(e, nf - 1)

    @pl.when(jnp.logical_not(active))
    def _():
        y_ref[...] = jnp.zeros_like(y_ref)

    @pl.when(active)
    def _():
        xb_ref[...] = x_ref[...].astype(BF16)
        for f in range(nf):
            xb = xb_ref[...]
            a = _dot(xb, wa_s[f])
            u = _dot(xb, wu_s[f])
            act = (a * jax.nn.sigmoid(a) * u).astype(BF16)
            y = _dot(act, wd_s[f])
            if f == 0:
                y_ref[...] = y
                pl.when(is_first)(lambda: finish(e, nf - 1))
            else:
                y_ref[...] += y

            @pl.when(feeds_next)
            def _(f=f):
                if f >= 1:
                    finish(e_next, f - 1)
                start(e_next, f)


def _experts(blk_expert, n_used, buf, w_in, w_down, layer):
    p, d = buf.shape
    nb = p // MOE_BLOCK
    nf = D_FF // FF_TILE
    grid_spec = pltpu.PrefetchScalarGridSpec(
        num_scalar_prefetch=2,
        grid=(nb,),
        in_specs=[
            pl.BlockSpec((MOE_BLOCK, d), lambda b, be, nu: (jnp.minimum(b, nu[0] - 1), 0)),
            pl.BlockSpec(memory_space=pl.ANY),
            pl.BlockSpec(memory_space=pl.ANY),
        ],
        out_specs=pl.BlockSpec((MOE_BLOCK, d), lambda b, be, nu: (b, 0)),
        scratch_shapes=[
            pltpu.VMEM((MOE_BLOCK, d), BF16),
            pltpu.VMEM((nf, d, FF_TILE), BF16),
            pltpu.VMEM((nf, d, FF_TILE), BF16),
            pltpu.VMEM((nf, FF_TILE, d), BF16),
            pltpu.VMEM((d, FF_TILE), F32),
            pltpu.VMEM((d, FF_TILE), F32),
            pltpu.VMEM((FF_TILE, d), F32),
            pltpu.SemaphoreType.DMA((3,)),
        ],
    )
    return pl.pallas_call(
        functools.partial(_expert_kernel, layer=layer),
        grid_spec=grid_spec,
        out_shape=jax.ShapeDtypeStruct((p, d), F32),
        compiler_params=_cp(("arbitrary",)),
        name="moe_experts",
    )(blk_expert, n_used, buf, w_in, w_down)


def _combine_kernel(s0_ref, s1_ref, n0_ref, n1_ref, y_hbm, x_ref, gate_ref, lg_ref, lb_ref, x3_ref, x3b_ref,
                    y0_ref, y1_ref, sems):
    i = pl.program_id(0)
    tc = x_ref.shape[0]
    bufs = (y0_ref, y1_ref)
    cur = i % 2

    def issue_tile(slot_refs, half):
        def issue(t, carry):
            for kk in range(TOP_K):
                src = slot_refs[kk][t]
                pltpu.make_async_copy(y_hbm.at[pl.ds(src, 1), :], bufs[kk].at[half, pl.ds(t, 1), :],
                                      sems.at[half]).start()
            return carry

        lax.fori_loop(0, tc, issue, 0, unroll=8)

    @pl.when(i == 0)
    def _():
        issue_tile((s0_ref, s1_ref), 0)

    @pl.when(i + 1 < pl.num_programs(0))
    def _():
        issue_tile((n0_ref, n1_ref), 1 - cur)

    for kk in range(TOP_K):
        pltpu.make_async_copy(y_hbm.at[pl.ds(0, tc), :], bufs[kk].at[cur], sems.at[cur]).wait()
    gate = gate_ref[...]
    z = DN_ALPHA * x_ref[...] + gate[:, 0:1] * y0_ref[cur] + gate[:, 1:2] * y1_ref[cur]
    x3 = _layer_norm(z, lg_ref[...], lb_ref[...])
    x3_ref[...] = x3
    x3b_ref[...] = x3.astype(BF16)


def _combine(slot0, slot1, y, x2, gate_nt, lg, lb, tc):
    n, d = x2.shape
    last = n // tc - 1
    return pl.pallas_call(
        _combine_kernel,
        grid=(n // tc,),
        in_specs=[
            pl.BlockSpec((tc,), lambda i: (i,), memory_space=pltpu.SMEM),
            pl.BlockSpec((tc,), lambda i: (i,), memory_space=pltpu.SMEM),
            pl.BlockSpec((tc,), lambda i: (jnp.minimum(i + 1, last),), memory_space=pltpu.SMEM),
            pl.BlockSpec((tc,), lambda i: (jnp.minimum(i + 1, last),), memory_space=pltpu.SMEM),
            pl.BlockSpec(memory_space=pl.ANY),
            pl.BlockSpec((tc, d), lambda i: (i, 0)),
            pl.BlockSpec((tc, 2), lambda i: (i, 0)),
            pl.BlockSpec((1, d), lambda i: (0, 0)),
            pl.BlockSpec((1, d), lambda i: (0, 0)),
        ],
        out_specs=[pl.BlockSpec((tc, d), lambda i: (i, 0)), pl.BlockSpec((tc, d), lambda i: (i, 0))],
        out_shape=[jax.ShapeDtypeStruct((n, d), F32), jax.ShapeDtypeStruct((n, d), BF16)],
        scratch_shapes=[pltpu.VMEM((2, tc, d), F32), pltpu.VMEM((2, tc, d), F32), pltpu.SemaphoreType.DMA((2,))],
        compiler_params=_cp(("arbitrary",)),
        name="moe_combine_ln",
    )(slot0, slot1, slot0, slot1, y, x2, gate_nt, lg, lb)


def _layer(x, xb, mem_b, p, moe_w, layer, consts, B, T):
    n, d = x.shape
    G, HPG, DH = NSA_GROUPS, NSA_HPG, NSA_DH
    slopes, ovt, epad, tile_ind = consts

    seqs = TILES.seqs if B % TILES.seqs == 0 else 1
    h_big = _matmul(xb, p["w_big"], BF16, TILES.proj_rows, TILES.proj_cols)
    h_small = _matmul(xb, p["w_small"], F32, TILES.proj_rows, SCOL_END)
    h_t = _matmul_t(xb, p["w_t"], B, T, TILES.proj_rows, TROW_END // 2)

    o_gla = _gla(h_big, h_small, p["wa_pad"], p["b_a"], p["norm_g"], B, T, seqs, TILES.gla_rows)

    kcmp, kcmp_t = _compress(h_small, p["cmp_w1bd"], p["cmp_w2bd"], p["cmp_w2bdt"], p["cmp_pe_pair"], B, T)
    tq_sel = TILES.sel_q
    ocmp_t, mb, in_tile = _cmp_select(slopes, h_t, kcmp, kcmp_t, ovt, tile_ind, B, T, TILES.cmp_q, seqs)
    nq = T // tq_sel
    tile_flags = (in_tile.reshape(B, G, nq, nq, tq_sel).max(axis=-1) > 0).astype(I32)
    tile_flags = tile_flags.transpose(0, 1, 3, 2).reshape(-1)
    gates_t = h_small[:, GLA_GATE_RANK:GLA_GATE_RANK + 3 * NSA_HEADS].reshape(B, T, G, 3 * HPG)
    gates_t = jnp.pad(gates_t.transpose(0, 2, 3, 1), ((0, 0), (0, 0), (0, GATE_ROWS - 3 * HPG), (0, 0)))
    o_nsa = _sel_win(slopes, tile_flags, h_t, h_big, epad, mb, ocmp_t, gates_t, B, T, tq_sel, seqs)

    x1, x1b = _mix(o_gla, o_nsa, h_big, x, p["w_bg"], p["w_bn"], p["w_out"], p["ln_mix_g"], p["ln_mix_b"],
                   TILES.mix_rows)

    kvm = _matmul(mem_b, p["xa_wkv"], BF16, TILES.kv_rows, TILES.kv_cols).reshape(B, MEM_LEN, 2 * XA_HEADS * XA_DH)
    x2 = _xattn(x1, x1b, kvm, p["xa_wq"], p["xa_wo"], p["ln_xa_g"], p["ln_xa_b"], B, T, TILES.xattn_rows)

    e, gate, rank, cnt = _router(x2, p["rw_hi"], p["rw_lo"], p["rb"], TILES.router_rows)
    counts = cnt[:, 0].astype(I32)
    padded = (counts + MOE_BLOCK - 1) // MOE_BLOCK * MOE_BLOCK
    pad_end = jnp.cumsum(padded)
    pad_start = (pad_end - padded).astype(I32)
    nb = (n * TOP_K) // MOE_BLOCK + N_EXPERTS
    n_used = (pad_end[-1] // MOE_BLOCK).astype(I32).reshape(1)
    blk_start = jnp.arange(nb, dtype=I32) * MOE_BLOCK
    blk_expert = jnp.minimum(jnp.sum(blk_start[:, None] >= pad_end[None, :], axis=1), N_EXPERTS - 1).astype(I32)
    blk_expert = jnp.where(jnp.arange(nb) < n_used[0], blk_expert, blk_expert[jnp.maximum(n_used[0] - 1, 0)])
    slot = _slots(pad_start, e, rank, TILES.slot_cols)
    buf = _dispatch(pad_end.astype(I32), slot[0], slot[1], x2, nb * MOE_BLOCK, TILES.dispatch_rows)
    y = _experts(blk_expert, n_used, buf, moe_w[0], moe_w[1], layer)
    x3, x3b = _combine(slot[0], slot[1], y, x2, gate.T, p["ln_ffn_g"], p["ln_ffn_b"],
                       TILES.combine_rows)
    return x3, x3b


def _prep_layer(l, w_in, gla_w_a2, gla_b_a, gla_norm_g, nsa_cmp_pe, nsa_cmp_w1, nsa_cmp_w2, w_branch_gla,
                w_branch_nsa, w_out, ln_mix_g, ln_mix_b, xa_wq, xa_wkv, xa_wo, ln_xa_g, ln_xa_b, router_w,
                router_b, moe_w_in, moe_w_down, ln_ffn_g, ln_ffn_b):
    d = w_in.shape[1]
    w = w_in[l]
    o_gq, o_gk, o_gv, o_gr = 0, GLA_QK, 2 * GLA_QK, 2 * GLA_QK + GLA_V
    o_ga = o_gr + GLA_V
    o_nq = o_ga + GLA_GATE_RANK
    o_nkv = o_nq + NSA_Q
    o_ng = o_nkv + 6 * NSA_KV
    o_mg = o_ng + 3 * NSA_HEADS
    G, DH = NSA_GROUPS, NSA_DH

    def kv_cols(kind):
        return w[:, o_nkv + kind * NSA_KV:o_nkv + (kind + 1) * NSA_KV]

    def slabs(wk):
        return jnp.pad(wk.reshape(d, G, DH), ((0, 0), (0, 0), (0, LANES - DH))).reshape(d, G * LANES)

    w_big = jnp.concatenate([w[:, o_mg:o_mg + 2 * d], w[:, o_gq:o_ga], slabs(kv_cols(2)), slabs(kv_cols(4))],
                            axis=1).astype(BF16)
    w_small = jnp.concatenate([w[:, o_ga:o_nq], w[:, o_ng:o_mg],
                               jnp.zeros((d, LANES - GLA_GATE_RANK - 3 * NSA_HEADS), F32),
                               kv_cols(0), kv_cols(1)], axis=1).astype(BF16)
    w_t = jnp.concatenate([w[:, o_nq:o_nkv], kv_cols(3), kv_cols(5)], axis=1).T.astype(BF16)
    w1 = nsa_cmp_w1[l].reshape(2, CMP_LEN, DH, CMP_HIDDEN)
    z1 = jnp.zeros_like(w1)
    w1bd = jnp.concatenate([jnp.concatenate([w1, z1], axis=3), jnp.concatenate([z1, w1], axis=3)], axis=2)
    w2 = nsa_cmp_w2[l]
    z2 = jnp.zeros_like(w2)
    w2bd = jnp.concatenate([jnp.concatenate([w2, z2], axis=2), jnp.concatenate([z2, w2], axis=2)], axis=1)
    pe = nsa_cmp_pe[l]
    pe_pair = jnp.broadcast_to(jnp.concatenate([pe, pe], axis=-1)[:, :, None, :], (2, CMP_LEN, 16, 2 * DH))
    wa_pad = jnp.concatenate([gla_w_a2[l], jnp.zeros((LANES - GLA_GATE_RANK, GLA_QK), F32)], axis=0).astype(BF16)
    rw_t = router_w.T
    rw_hi = rw_t.astype(BF16)
    rw_lo = (rw_t - rw_hi.astype(F32)).astype(BF16)
    return dict(
        w_big=w_big, w_small=w_small, w_t=w_t, wa_pad=wa_pad,
        b_a=gla_b_a[l].reshape(1, -1), norm_g=gla_norm_g[l].reshape(1, -1),
        cmp_w1bd=w1bd.astype(BF16), cmp_w2bd=w2bd.astype(BF16), cmp_w2bdt=w2bd.transpose(0, 2, 1).astype(BF16),
        cmp_pe_pair=pe_pair.astype(BF16),
        w_bg=w_branch_gla[l].astype(BF16), w_bn=w_branch_nsa[l].astype(BF16), w_out=w_out[l].astype(BF16),
        ln_mix_g=ln_mix_g[l].reshape(1, -1), ln_mix_b=ln_mix_b[l].reshape(1, -1),
        xa_wq=xa_wq[l].astype(BF16), xa_wkv=xa_wkv[l].astype(BF16), xa_wo=xa_wo[l].astype(BF16),
        ln_xa_g=ln_xa_g[l].reshape(1, -1), ln_xa_b=ln_xa_b[l].reshape(1, -1),
        rw_hi=rw_hi, rw_lo=rw_lo, rb=router_b.reshape(-1, 1),
        ln_ffn_g=ln_ffn_g[l].reshape(1, -1), ln_ffn_b=ln_ffn_b[l].reshape(1, -1),
    )


def kernel(x, mem, w_in, gla_w_a2, gla_b_a, gla_norm_g, nsa_cmp_pe, nsa_cmp_w1, nsa_cmp_w2, w_branch_gla, w_branch_nsa, w_out, ln_mix_g, ln_mix_b, xa_wq, xa_wkv, xa_wo, ln_xa_g, ln_xa_b, router_w, router_b, moe_w_in, moe_w_down, ln_ffn_g, ln_ffn_b):
    B, T, d = x.shape
    n = B * T
    assert d == 2048 and mem.shape[1] == MEM_LEN
    assert T % max(TILES.cmp_q, TILES.gla_rows, TILES.sel_q, TILES.xattn_rows) == 0 and WINDOW % TILES.sel_q == 0
    assert n % max(TILES.proj_rows, TILES.slot_cols) == 0
    params = (w_in, gla_w_a2, gla_b_a, gla_norm_g, nsa_cmp_pe, nsa_cmp_w1, nsa_cmp_w2, w_branch_gla, w_branch_nsa,
              w_out, ln_mix_g, ln_mix_b, xa_wq, xa_wkv, xa_wo, ln_xa_g, ln_xa_b, router_w, router_b, moe_w_in,
              moe_w_down, ln_ffn_g, ln_ffn_b)
    slopes = (2.0 ** (-8.0 * jnp.arange(1, NSA_HEADS + 1, dtype=F32) / NSA_HEADS)).astype(F32)
    nc, ns = T // CMP_STRIDE, T // SEL_LEN
    cs = np.arange(nc) * CMP_STRIDE
    ss = np.arange(ns) * SEL_LEN
    ovt = ((cs[None, :] < ss[:, None] + SEL_LEN) & (cs[None, :] + CMP_LEN > ss[:, None])
           & (cs[None, :] + CMP_LEN <= T)).astype(np.float32)
    assert NSA_DH + ns <= LANES
    epad = np.zeros((T, LANES), np.float32)
    epad[np.arange(T), NSA_DH + np.arange(T) // SEL_LEN] = 1.0
    tile_ind = (np.arange(ns)[None, :] // (TILES.sel_q // SEL_LEN) == np.arange(T // TILES.sel_q)[:, None]).astype(np.float32)
    consts = (slopes, jnp.asarray(ovt, BF16), jnp.asarray(epad, BF16), jnp.asarray(tile_ind, BF16))

    xf = x.reshape(n, d)
    xb = xf.astype(BF16)
    mem_b = mem.reshape(B * MEM_LEN, d).astype(BF16)
    moe_w = (moe_w_in, moe_w_down)
    for l in range(DEPTH):
        p = _prep_layer(l, *params)
        xf, xb = _layer(xf, xb, mem_b, p, moe_w, l, consts, B, T)
    return xf.reshape(B, T, d)
```

```python
import functools
from typing import NamedTuple

import jax
import jax.numpy as jnp
import numpy as np
from jax import lax
from jax.experimental import pallas as pl
from jax.experimental.pallas import tpu as pltpu

F32 = jnp.float32
BF16 = jnp.bfloat16
I32 = jnp.int32

DEPTH = 2
MEM_LEN = 256
GLA_HEADS = 4
GLA_DK = 128
GLA_DV = 256
GLA_GATE_RANK = 16
GLA_TAU = 16.0
GLA_CHUNK = 64
NSA_HEADS = 16
NSA_GROUPS = 4
NSA_HPG = NSA_HEADS // NSA_GROUPS
NSA_DH = 64
CMP_LEN = 32
CMP_STRIDE = 16
CMP_HIDDEN = 256
SEL_LEN = 64
SEL_TOPN = 8
WINDOW = 512
XA_HEADS = 4
XA_DH = 128
N_EXPERTS = 16
N_GROUPS = 4
EXPERTS_PER_GROUP = N_EXPERTS // N_GROUPS
TOP_K = 2
D_FF = 1536
DN_ALPHA = float((2 * DEPTH) ** 0.25)
LN_EPS = 1e-5
NEG = -1e30
LOG2E = 1.4426950408889634
FORCE_BONUS = 1e6

GLA_QK = GLA_HEADS * GLA_DK
GLA_V = GLA_HEADS * GLA_DV
NSA_Q = NSA_HEADS * NSA_DH
NSA_KV = NSA_GROUPS * NSA_DH

LANES = 128
VMEM_LIMIT = 56 * 1024 * 1024

COL_MG = 0
COL_GQ = 2 * 2048
COL_GK = COL_GQ + GLA_QK
COL_GV = COL_GK + GLA_QK
COL_GR = COL_GV + GLA_V
COL_KS = COL_GR + GLA_V
COL_KW = COL_KS + NSA_GROUPS * LANES
COL_END = COL_KW + NSA_GROUPS * LANES
SCOL_CK = LANES
SCOL_CV = SCOL_CK + NSA_KV
SCOL_END = SCOL_CV + NSA_KV
TROW_Q = 0
TROW_VS = NSA_Q
TROW_VW = TROW_VS + NSA_KV
TROW_END = TROW_VW + NSA_KV

MOE_BLOCK = 512
FF_TILE = 512
GATE_ROWS = 16


class _Tiles(NamedTuple):
    proj_rows: int = 1024
    proj_cols: int = 2048
    seqs: int = 4
    gla_rows: int = 512
    cmp_q: int = 512
    sel_q: int = 256
    mix_rows: int = 512
    xattn_rows: int = 512
    kv_rows: int = 512
    kv_cols: int = 512
    router_rows: int = 512
    slot_cols: int = 2048
    dispatch_rows: int = 512
    combine_rows: int = 256


TILES = _Tiles()


def _cp(sem):
    return pltpu.CompilerParams(dimension_semantics=sem, vmem_limit_bytes=VMEM_LIMIT)


def _dot(a, b):
    return jnp.dot(a, b, preferred_element_type=F32)


def _dot_nt(a, b):
    return lax.dot_general(a, b, (((1,), (1,)), ((), ())), preferred_element_type=F32)


def _dot_tn(a, b):
    return lax.dot_general(a, b, (((0,), (0,)), ((), ())), preferred_element_type=F32)


def _layer_norm(z, g, b):
    mu = jnp.mean(z, axis=-1, keepdims=True)
    zc = z - mu
    var = jnp.mean(zc * zc, axis=-1, keepdims=True)
    return zc * lax.rsqrt(var + LN_EPS) * g + b


def _mm_kernel(a_ref, b_ref, o_ref):
    o_ref[...] = _dot(a_ref[...], b_ref[...]).astype(o_ref.dtype)


def _matmul(a, b, out_dtype, tm, tn):
    m, k = a.shape
    n = b.shape[1]
    return pl.pallas_call(
        _mm_kernel,
        grid=(m // tm, n // tn),
        in_specs=[pl.BlockSpec((tm, k), lambda i, j: (i, 0)),
                  pl.BlockSpec((k, tn), lambda i, j: (0, j))],
        out_specs=pl.BlockSpec((tm, tn), lambda i, j: (i, j)),
        out_shape=jax.ShapeDtypeStruct((m, n), out_dtype),
        compiler_params=_cp(("parallel", "parallel")),
        name="matmul",
    )(a, b)


def _mm_nt_kernel(wt_ref, x_ref, o_ref):
    o_ref[0] = _dot_nt(wt_ref[...], x_ref[...]).astype(o_ref.dtype)


def _matmul_t(x, wt, B, T, tm, tr):
    n, k = x.shape
    r = wt.shape[0]
    nt = T // tm
    return pl.pallas_call(
        _mm_nt_kernel,
        grid=(n // tm, r // tr),
        in_specs=[pl.BlockSpec((tr, k), lambda i, j: (j, 0)),
                  pl.BlockSpec((tm, k), lambda i, j: (i, 0))],
        out_specs=pl.BlockSpec((1, tr, tm), lambda i, j: (i // nt, j, i % nt)),
        out_shape=jax.ShapeDtypeStruct((B, r, T), BF16),
        compiler_params=_cp(("parallel", "parallel")),
        name="matmul_t",
    )(wt, x)


def _gla_kernel(q_ref, k_ref, v_ref, r_ref, sm_ref, wa_ref, ba_ref, ng_ref, o_ref, st_ref):
    C = GLA_CHUNK
    seqs = q_ref.shape[1]
    n_chunks = q_ref.shape[2] // C

    @pl.when(pl.program_id(1) == 0)
    def _():
        st_ref[...] = jnp.zeros_like(st_ref)

    rowi = lax.broadcasted_iota(I32, (C, GLA_DK), 0)
    tt = lax.broadcasted_iota(I32, (C, C), 0)
    ss = lax.broadcasted_iota(I32, (C, C), 1)
    levels = (1, 2, 4, 8, 16, 32)
    pair_masks = [((tt // (2 * L)) == (ss // (2 * L))) & ((tt & L) != 0) & ((ss & L) == 0) for L in levels]
    diag_mask = tt == ss
    scale = GLA_DK ** -0.5

    def head_chunk(sq, rows, h, z):
        qk_cols = slice(h * GLA_DK, (h + 1) * GLA_DK)
        v_cols = slice(h * GLA_DV, (h + 1) * GLA_DV)
        state = sq * GLA_HEADS + h
        q = q_ref[0, sq, rows, qk_cols].astype(F32) * scale
        k = k_ref[0, sq, rows, qk_cols].astype(F32)
        v = v_ref[0, sq, rows, v_cols]
        g = (jnp.minimum(z, 0.0) - jnp.log1p(jnp.exp(-jnp.abs(z)))) * (1.0 / GLA_TAU)
        incl = g
        tot = g
        att = jnp.where(diag_mask, _dot_nt(q.astype(BF16), k.astype(BF16)), 0.0)
        for L, pm in zip(levels, pair_masks):
            ql = (q * jnp.exp(incl)).astype(BF16)
            kl = (k * jnp.exp(tot - incl)).astype(BF16)
            att = jnp.where(pm, _dot_nt(ql, kl), att)
            upper = (rowi & L) != 0
            from_lower = pltpu.roll(tot, L, 0)
            from_upper = pltpu.roll(tot, C - L, 0)
            incl = incl + jnp.where(upper, from_lower, 0.0)
            tot = tot + jnp.where(upper, from_lower, from_upper)
        qd = (q * jnp.exp(incl)).astype(BF16)
        kd = (k * jnp.exp(tot - incl)).astype(BF16)
        st = st_ref[state]
        o = _dot_nt(qd, st.astype(BF16)) + _dot(att.astype(BF16), v)
        st_ref[state] = st * jnp.exp(tot[0:1, :]) + _dot_tn(v, kd)
        mu = jnp.mean(o, axis=-1, keepdims=True)
        oc = o - mu
        var = jnp.mean(oc * oc, axis=-1, keepdims=True)
        on = oc * lax.rsqrt(var + LN_EPS) * ng_ref[:, v_cols]
        r = r_ref[0, sq, rows, v_cols].astype(F32)
        o_ref[0, sq, rows, v_cols] = (on * (r * jax.nn.sigmoid(r))).astype(o_ref.dtype)

    def chunk(c, carry):
        rows = pl.ds(pl.multiple_of(c * C, C), C)
        for sq in range(seqs):
            z = _dot(sm_ref[0, sq, rows, :].astype(BF16), wa_ref[...]) + ba_ref[...]
            for h in range(GLA_HEADS):
                head_chunk(sq, rows, h, z[:, h * GLA_DK:(h + 1) * GLA_DK])
        return carry

    lax.fori_loop(0, n_chunks, chunk, 0)


def _gla(h_big, h_small, wa_pad, b_a, norm_g, B, T, seqs, tb):
    groups = B // seqs
    hb = h_big.reshape(groups, seqs, T, h_big.shape[1])
    hs = h_small.reshape(groups, seqs, T, h_small.shape[1])
    out = pl.pallas_call(
        _gla_kernel,
        grid=(groups, T // tb),
        in_specs=[
            pl.BlockSpec((1, seqs, tb, GLA_QK), lambda b, j: (b, 0, j, COL_GQ // GLA_QK)),
            pl.BlockSpec((1, seqs, tb, GLA_QK), lambda b, j: (b, 0, j, COL_GK // GLA_QK)),
            pl.BlockSpec((1, seqs, tb, GLA_V), lambda b, j: (b, 0, j, COL_GV // GLA_V)),
            pl.BlockSpec((1, seqs, tb, GLA_V), lambda b, j: (b, 0, j, COL_GR // GLA_V)),
            pl.BlockSpec((1, seqs, tb, LANES), lambda b, j: (b, 0, j, 0)),
            pl.BlockSpec((LANES, GLA_QK), lambda b, j: (0, 0)),
            pl.BlockSpec((1, GLA_QK), lambda b, j: (0, 0)),
            pl.BlockSpec((1, GLA_V), lambda b, j: (0, 0)),
        ],
        out_specs=pl.BlockSpec((1, seqs, tb, GLA_V), lambda b, j: (b, 0, j, 0)),
        out_shape=jax.ShapeDtypeStruct((groups, seqs, T, GLA_V), BF16),
        scratch_shapes=[pltpu.VMEM((seqs * GLA_HEADS, GLA_DV, GLA_DK), F32)],
        compiler_params=_cp(("parallel", "arbitrary")),
        name="gla",
    )(hb, hb, hb, hb, hs, wa_pad, b_a, norm_g)
    return out.reshape(B * T, GLA_V)


def _compress_kernel(x_ref, w1_ref, w2_ref, w2t_ref, pe_ref, o_ref, ot_ref):
    nc = x_ref.shape[0] // CMP_STRIDE
    hid_w = w1_ref.shape[3]
    a = jnp.zeros((nc, hid_w), F32)
    bm = jnp.zeros((nc, hid_w), F32)
    c = jnp.zeros((pe_ref.shape[2], hid_w), F32)
    for l in range(CMP_STRIDE):
        xl = x_ref[pl.ds(l, nc, stride=CMP_STRIDE), :].astype(BF16)
        a = a + _dot(xl, w1_ref[0, l])
        bm = bm + _dot(xl, w1_ref[0, CMP_STRIDE + l])
    for l in range(CMP_LEN):
        c = c + _dot(pe_ref[0, l], w1_ref[0, l])
    hid = a + pltpu.roll(bm, nc - 1, 0) + c[0:1, :]
    act = jax.nn.gelu(hid).astype(BF16)
    o_ref[0, 0, 0] = _dot(act, w2_ref[0]).astype(o_ref.dtype)
    ot_ref[0, 0, 0] = _dot_nt(w2t_ref[0], act).astype(ot_ref.dtype)


def _compress(h_small, w1bd, w2bd, w2bdt, pe_pair, B, T):
    nc = T // CMP_STRIDE
    pairs = NSA_GROUPS // 2
    return pl.pallas_call(
        _compress_kernel,
        grid=(B, 2, pairs),
        in_specs=[
            pl.BlockSpec((T, LANES), lambda b, s, j: (b, SCOL_CK // LANES + s * pairs + j)),
            pl.BlockSpec((1,) + w1bd.shape[1:], lambda b, s, j: (s, 0, 0, 0)),
            pl.BlockSpec((1,) + w2bd.shape[1:], lambda b, s, j: (s, 0, 0)),
            pl.BlockSpec((1,) + w2bdt.shape[1:], lambda b, s, j: (s, 0, 0)),
            pl.BlockSpec((1,) + pe_pair.shape[1:], lambda b, s, j: (s, 0, 0, 0)),
        ],
        out_specs=[
            pl.BlockSpec((1, 1, 1, nc, LANES), lambda b, s, j: (b, s, j, 0, 0)),
            pl.BlockSpec((1, 1, 1, LANES, nc), lambda b, s, j: (b, s, j, 0, 0)),
        ],
        out_shape=[jax.ShapeDtypeStruct((B, 2, pairs, nc, LANES), BF16),
                   jax.ShapeDtypeStruct((B, 2, pairs, LANES, nc), BF16)],
        compiler_params=_cp(("parallel", "parallel", "parallel")),
        name="nsa_compress",
    )(h_small, w1bd, w2bd, w2bdt, pe_pair)


def _cmp_select_kernel(slopes_ref, qt_ref, kc_ref, vct_ref, ovt_ref, ind_ref, ocmp_ref, mb_ref, kt_ref, qpad_ref):
    g = pl.program_id(1)
    i = pl.program_id(2)
    streams = qt_ref.shape[1]
    tq = qt_ref.shape[3]
    nc = kc_ref.shape[4]
    ns = mb_ref.shape[3]
    dh = NSA_DH
    t0 = i * tq
    wide = NSA_HPG * tq
    tpos = t0 + (lax.broadcasted_iota(I32, (nc, wide), 1) & (tq - 1))
    nidx = lax.broadcasted_iota(I32, (nc, wide), 0)
    mask_c = (nidx * CMP_STRIDE + (CMP_LEN - 1)) <= tpos
    absd = jnp.abs(tpos.astype(F32) - (nidx.astype(F32) * CMP_STRIDE + 0.5 * (CMP_LEN - 1)))
    srow = jnp.concatenate([jnp.full((1, tq), slopes_ref[g * NSA_HPG + hh], F32) for hh in range(NSA_HPG)], axis=1)
    lower = g % 2 == 0
    j = lax.broadcasted_iota(I32, (ns, tq), 0)
    tp = t0 + lax.broadcasted_iota(I32, (ns, tq), 1)
    cur = tp // SEL_LEN
    forced = (j == 0) | (j == cur) | (j == cur - 1)
    valid = j * SEL_LEN <= tp
    for u in range(streams):
        kc = kc_ref[0, u, 0, 0]
        vct = jnp.where(lower, vct_ref[0, u, 0, 0, 0:dh, :], vct_ref[0, u, 0, 0, dh:2 * dh, :])
        for hh in range(NSA_HPG):
            q = qt_ref[0, u, hh * dh:(hh + 1) * dh, :] * jnp.asarray(dh ** -0.5, BF16)
            zero = jnp.zeros_like(q)
            qpad_ref[u, 0:dh, hh * tq:(hh + 1) * tq] = jnp.where(lower, q, zero)
            qpad_ref[u, dh:2 * dh, hh * tq:(hh + 1) * tq] = jnp.where(lower, zero, q)
        s = _dot(kc, qpad_ref[u]) - srow * absd
        s = jnp.where(mask_c, s, NEG)
        e = jnp.exp(s - jnp.max(s, axis=0, keepdims=True))
        p = jnp.where(mask_c, e * (1.0 / jnp.sum(e, axis=0, keepdims=True)), 0.0)
        o = _dot(vct, p.astype(BF16))
        psum = jnp.zeros((nc, tq), F32)
        for hh in range(NSA_HPG):
            ocmp_ref[0, u, hh * dh:(hh + 1) * dh, :] = o[:, hh * tq:(hh + 1) * tq].astype(ocmp_ref.dtype)
            psum = psum + p[:, hh * tq:(hh + 1) * tq]
        p_hi = psum.astype(BF16)
        p_lo = (psum - p_hi.astype(F32)).astype(BF16)
        imp = _dot(ovt_ref[...], p_hi) + _dot(ovt_ref[...], p_lo)
        score = jnp.where(valid, imp + jnp.where(forced, FORCE_BONUS, 0.0), NEG)
        rank = jnp.zeros((ns, tq), F32)
        for jp in range(ns):
            row = score[jp:jp + 1, :]
            beats = (row > score) | ((row == score) & (j > jp))
            rank = rank + jnp.where(beats, 1.0, 0.0)
        keep = valid & (rank < float(min(SEL_TOPN, ns)))
        mb_ref[0, u, 0] = jnp.where(keep, 0.0, NEG).astype(mb_ref.dtype)
        kt_ref[0, u, 0] = _dot(ind_ref[...], jnp.where(keep, 1.0, 0.0).astype(BF16))


def _cmp_select(slopes, h_t, kcmp, kcmp_t, ovt, tile_ind, B, T, tq, streams):
    nc = T // CMP_STRIDE
    ns = T // SEL_LEN
    nkt = tile_ind.shape[0]
    G = NSA_GROUPS
    grp_rows = NSA_HPG * NSA_DH
    nbg = B // streams
    h_t4 = h_t.reshape(nbg, streams, h_t.shape[1], T)
    kcmp6 = kcmp.reshape((nbg, streams) + kcmp.shape[1:])
    kcmp_t6 = kcmp_t.reshape((nbg, streams) + kcmp_t.shape[1:])
    grid_spec = pltpu.PrefetchScalarGridSpec(
        num_scalar_prefetch=1,
        grid=(nbg, G, T // tq),
        in_specs=[
            pl.BlockSpec((1, streams, grp_rows, tq), lambda b, g, i, s: (b, 0, TROW_Q // grp_rows + g, i)),
            pl.BlockSpec((1, streams, 1, 1, nc, LANES), lambda b, g, i, s: (b, 0, 0, g // 2, 0, 0)),
            pl.BlockSpec((1, streams, 1, 1, LANES, nc), lambda b, g, i, s: (b, 0, 1, g // 2, 0, 0)),
            pl.BlockSpec((ns, nc), lambda b, g, i, s: (0, 0)),
            pl.BlockSpec((nkt, ns), lambda b, g, i, s: (0, 0)),
        ],
        out_specs=[
            pl.BlockSpec((1, streams, grp_rows, tq), lambda b, g, i, s: (b, 0, g, i)),
            pl.BlockSpec((1, streams, 1, ns, tq), lambda b, g, i, s: (b, 0, g, 0, i)),
            pl.BlockSpec((1, streams, 1, nkt, tq), lambda b, g, i, s: (b, 0, g, 0, i)),
        ],
        scratch_shapes=[pltpu.VMEM((streams, LANES, NSA_HPG * tq), BF16)],
    )
    ocmp, mb, in_tile = pl.pallas_call(
        _cmp_select_kernel,
        grid_spec=grid_spec,
        out_shape=[jax.ShapeDtypeStruct((nbg, streams, NSA_Q, T), BF16),
                   jax.ShapeDtypeStruct((nbg, streams, G, ns, T), BF16),
                   jax.ShapeDtypeStruct((nbg, streams, G, nkt, T), F32)],
        compiler_params=_cp(("parallel", "parallel", "parallel")),
        name="nsa_cmp_select",
    )(slopes, h_t4, kcmp6, kcmp_t6, ovt, tile_ind)
    return ocmp.reshape(B, NSA_Q, T), mb.reshape(B, G, ns, T), in_tile.reshape(B, G, nkt, T)


def _sel_win_kernel(slopes_ref, flags_ref, qt_ref, ks_ref, kw_ref, vs_ref, vw_ref, epad_ref, mb_ref, ocmp_ref, gt_ref,
                    o_ref, qaug_ref, m_ref, acc_ref, srow_ref, bias_ref, s_ref, p_ref, alpha_ref,
                    kall_ref, vall_ref, tiles_ref, *, n_tiles):
    g = pl.program_id(1)
    i = pl.program_id(2)
    streams = qt_ref.shape[1]
    tq = qt_ref.shape[3]
    tk = tq
    ns = mb_ref.shape[3]
    dh = NSA_DH
    wide = NSA_HPG * tq
    BIG = -NEG

    @pl.when(i == 0)
    def _():
        srow = jnp.concatenate([jnp.full((1, tq), slopes_ref[g * NSA_HPG + hh] * LOG2E, F32)
                                for hh in range(NSA_HPG)], axis=1)
        srow_ref[...] = srow
        lane = lax.broadcasted_iota(I32, (tk, wide), 1) & (tq - 1)
        dist0 = (lane - lax.broadcasted_iota(I32, (tk, wide), 0)).astype(F32)
        sd0 = srow * dist0
        bias_ref[0] = sd0
        bias_ref[1] = sd0 + jnp.where(dist0 >= 0.0, 0.0, BIG)
        bias_ref[2] = sd0 + jnp.where(dist0 < 0.0, 0.0, BIG)
        bias_ref[3] = jnp.full((tk, wide), BIG, F32)
        extra = jnp.where(lax.broadcasted_iota(I32, (vall_ref.shape[2] - dh, vall_ref.shape[3]), 0) == 0, 1.0, 0.0)
        for u in range(streams):
            kall_ref[u, 0] = ks_ref[0, u] + epad_ref[...]
            kall_ref[u, 1] = kw_ref[0, u]
            vall_ref[u, 0, 0:dh, :] = vs_ref[0, u]
            vall_ref[u, 1, 0:dh, :] = vw_ref[0, u]
            vall_ref[u, 0, dh:, :] = extra.astype(BF16)
            vall_ref[u, 1, dh:, :] = extra.astype(BF16)

    for u in range(streams):
        for hh in range(NSA_HPG):
            cols = slice(hh * tq, (hh + 1) * tq)
            q = qt_ref[0, u, hh * dh:(hh + 1) * dh, :].astype(F32) * (dh ** -0.5 * LOG2E)
            qaug_ref[u, 0:dh, cols] = q.astype(BF16)
            qaug_ref[u, dh:dh + ns, cols] = mb_ref[0, u, 0]
            qaug_ref[u, dh + ns:, cols] = jnp.zeros((qaug_ref.shape[1] - dh - ns, tq), BF16)

    m_ref[...] = jnp.full(m_ref.shape, NEG, F32)
    acc_ref[...] = jnp.zeros(acc_ref.shape, F32)

    n_back = WINDOW // tk
    n_win = jnp.minimum(i, n_back) + 1
    n_sel = []
    for u in range(streams):
        flag_base = (((pl.program_id(0) * streams + u) * NSA_GROUPS + g) * n_tiles + i) * n_tiles
        cnt = jnp.int32(0)
        for kb_static in range(n_tiles - 1):
            active = (kb_static < i) & (flags_ref[flag_base + kb_static] != 0)
            tiles_ref[u, cnt] = kb_static
            cnt = cnt + active.astype(I32)
        tiles_ref[u, cnt] = i
        n_sel.append(cnt + 1)
    n_steps = n_sel[0] + n_win
    for u in range(1, streams):
        n_steps = jnp.maximum(n_steps, n_sel[u] + n_win)

    def describe(u, n):
        n = jnp.maximum(n, 0)
        is_win = n >= n_sel[u]
        kb_sel = tiles_ref[u, jnp.minimum(n, n_sel[u] - 1)]
        kb = jnp.clip(jnp.where(is_win, i - n_win + 1 + (n - n_sel[u]), kb_sel), 0, i)
        mode = jnp.where(kb == i, 1, jnp.where(is_win & (kb == i - n_back), 2, 0))
        mode = jnp.where(n >= n_sel[u] + n_win, 3, mode)
        return is_win.astype(I32), kb, mode

    def scores(u, n, slot):
        br, kb, _ = describe(u, n)
        s0 = pl.multiple_of(kb * tk, tk)
        s_ref[u, slot] = _dot(kall_ref[u, br, pl.ds(s0, tk), :], qaug_ref[u])

    def softmax(u, n, slot):
        br, kb, mode = describe(u, n)
        crow = srow_ref[...] * ((i - kb) * tk).astype(F32)
        s = s_ref[u, slot] - bias_ref[mode]
        m_old = m_ref[u, br]
        m_new = jnp.maximum(m_old, jnp.max(s, axis=0, keepdims=True) - crow)
        alpha = jnp.exp2(m_old - m_new)
        p = jnp.exp2(s - (m_new + crow))
        m_ref[u, br] = m_new
        alpha_ref[u, slot] = alpha
        p_ref[u, slot] = p.astype(BF16)

    def weighted_values(u, n, slot):
        br, kb, _ = describe(u, n)
        s0 = pl.multiple_of(kb * tk, tk)
        acc_ref[u, br] = (alpha_ref[u, slot] * acc_ref[u, br]
                          + _dot(vall_ref[u, br, :, pl.ds(s0, tk)], p_ref[u, slot]))

    def each(fn, n, slot):
        for u in range(streams):
            fn(u, n, slot)

    for u in range(streams):
        p_ref[u, 1] = jnp.zeros(p_ref.shape[2:], BF16)
        alpha_ref[u, 1] = jnp.ones(alpha_ref.shape[2:], F32)
    each(scores, 0, 0)

    def pair(j, carry):
        n = 2 * j
        each(scores, n + 1, 1)
        each(softmax, n, 0)
        each(weighted_values, n - 1, 1)
        each(scores, n + 2, 0)
        each(softmax, n + 1, 1)
        each(weighted_values, n, 0)
        return carry

    n_pairs = n_steps // 2
    lax.fori_loop(0, n_pairs, pair, 0)
    each(weighted_values, 2 * n_pairs - 1, 1)

    @pl.when(n_steps % 2 == 1)
    def _():
        each(softmax, n_steps - 1, 0)
        each(weighted_values, n_steps - 1, 0)

    for u in range(streams):
        def gate_row(branch, u=u):
            rows = [gt_ref[0, u, 0, 3 * hh + branch:3 * hh + branch + 1, :] for hh in range(NSA_HPG)]
            return jax.nn.sigmoid(jnp.concatenate(rows, axis=1))

        o = (acc_ref[u, 0, 0:dh, :] * (gate_row(1) / acc_ref[u, 0, dh:dh + 1, :])
             + acc_ref[u, 1, 0:dh, :] * (gate_row(2) / acc_ref[u, 1, dh:dh + 1, :]))
        ocmp = jnp.concatenate([ocmp_ref[0, u, hh * dh:(hh + 1) * dh, :] for hh in range(NSA_HPG)],
                               axis=1).astype(F32)
        o = o + gate_row(0) * ocmp
        o_heads = jnp.concatenate([o[:, hh * tq:(hh + 1) * tq] for hh in range(NSA_HPG)], axis=0)
        o_ref[0, u] = o_heads.T.astype(o_ref.dtype)


def _sel_win(slopes, tile_flags, h_t, h_big, epad, mb, ocmp_t, gates_t, B, T, tq, streams):
    ns = T // SEL_LEN
    kaug = LANES
    G = NSA_GROUPS
    grp_rows = NSA_HPG * NSA_DH
    nq = T // tq
    nbg = B // streams
    wide = NSA_HPG * tq
    h_t5 = h_t.reshape(nbg, streams, h_t.shape[1], T)
    h_big5 = h_big.reshape(nbg, streams, T, h_big.shape[1])
    mb5 = mb.reshape(nbg, streams, G, ns, T)
    ocmp5 = ocmp_t.reshape(nbg, streams, NSA_Q, T)
    gates5 = gates_t.reshape(nbg, streams, G, GATE_ROWS, T)
    grid_spec = pltpu.PrefetchScalarGridSpec(
        num_scalar_prefetch=2,
        grid=(nbg, G, nq),
        in_specs=[
            pl.BlockSpec((1, streams, grp_rows, tq), lambda b, g, i, s, f: (b, 0, TROW_Q // grp_rows + g, i)),
            pl.BlockSpec((1, streams, T, LANES), lambda b, g, i, s, f: (b, 0, 0, COL_KS // LANES + g)),
            pl.BlockSpec((1, streams, T, LANES), lambda b, g, i, s, f: (b, 0, 0, COL_KW // LANES + g)),
            pl.BlockSpec((1, streams, NSA_DH, T), lambda b, g, i, s, f: (b, 0, TROW_VS // NSA_DH + g, 0)),
            pl.BlockSpec((1, streams, NSA_DH, T), lambda b, g, i, s, f: (b, 0, TROW_VW // NSA_DH + g, 0)),
            pl.BlockSpec((T, LANES), lambda b, g, i, s, f: (0, 0)),
            pl.BlockSpec((1, streams, 1, ns, tq), lambda b, g, i, s, f: (b, 0, g, 0, i)),
            pl.BlockSpec((1, streams, grp_rows, tq), lambda b, g, i, s, f: (b, 0, g, i)),
            pl.BlockSpec((1, streams, 1, GATE_ROWS, tq), lambda b, g, i, s, f: (b, 0, g, 0, i)),
        ],
        out_specs=pl.BlockSpec((1, streams, tq, grp_rows), lambda b, g, i, s, f: (b, 0, i, g)),
        scratch_shapes=[
            pltpu.VMEM((streams, kaug, wide), BF16),
            pltpu.VMEM((streams, 2, 1, wide), F32),
            pltpu.VMEM((streams, 2, NSA_DH + 16, wide), F32),
            pltpu.VMEM((1, wide), F32),
            pltpu.VMEM((4, tq, wide), F32),
            pltpu.VMEM((streams, 2, tq, wide), F32),
            pltpu.VMEM((streams, 2, tq, wide), BF16),
            pltpu.VMEM((streams, 2, 1, wide), F32),
            pltpu.VMEM((streams, 2, T, kaug), BF16),
            pltpu.VMEM((streams, 2, NSA_DH + 16, T), BF16),
            pltpu.SMEM((streams, nq), I32),
        ],
    )
    out = pl.pallas_call(
        functools.partial(_sel_win_kernel, n_tiles=nq),
        grid_spec=grid_spec,
        out_shape=jax.ShapeDtypeStruct((nbg, streams, T, NSA_Q), BF16),
        compiler_params=_cp(("parallel", "parallel", "arbitrary")),
        name="nsa_sel_win",
    )(slopes, tile_flags, h_t5, h_big5, h_big5, h_t5, h_t5, epad, mb5, ocmp5, gates5)
    return out.reshape(B * T, NSA_Q)


def _mix_kernel(og_ref, on_ref, mg1_ref, mg2_ref, x_ref, wg_ref, wn_ref, wo_ref, lg_ref, lb_ref,
                x1_ref, x1b_ref):
    g1 = _dot(og_ref[...], wg_ref[...])
    g2 = _dot(on_ref[...], wn_ref[...])
    merged = (jax.nn.sigmoid(mg1_ref[...].astype(F32)) * g1
              + jax.nn.sigmoid(mg2_ref[...].astype(F32)) * g2)
    y = _dot(merged.astype(BF16), wo_ref[...])
    x1 = _layer_norm(DN_ALPHA * x_ref[...] + y, lg_ref[...], lb_ref[...])
    x1_ref[...] = x1
    x1b_ref[...] = x1.astype(BF16)


def _const_spec(shape):
    nd = len(shape)
    return pl.BlockSpec(shape, lambda *_: (0,) * nd, pipeline_mode=pl.Buffered(1))


def _mix(o_gla, o_nsa, h_big, x, wg, wn, wo, lg, lb, tm):
    n, d = x.shape
    return pl.pallas_call(
        _mix_kernel,
        grid=(n // tm,),
        in_specs=[
            pl.BlockSpec((tm, GLA_V), lambda i: (i, 0)),
            pl.BlockSpec((tm, NSA_Q), lambda i: (i, 0)),
            pl.BlockSpec((tm, d), lambda i: (i, 0)),
            pl.BlockSpec((tm, d), lambda i: (i, 1)),
            pl.BlockSpec((tm, d), lambda i: (i, 0)),
            _const_spec(wg.shape), _const_spec(wn.shape), _const_spec(wo.shape),
            _const_spec(lg.shape), _const_spec(lb.shape),
        ],
        out_specs=[pl.BlockSpec((tm, d), lambda i: (i, 0)), pl.BlockSpec((tm, d), lambda i: (i, 0))],
        out_shape=[jax.ShapeDtypeStruct((n, d), F32), jax.ShapeDtypeStruct((n, d), BF16)],
        compiler_params=_cp(("parallel",)),
        name="mix_ln",
    )(o_gla, o_nsa, h_big, h_big, x, wg, wn, wo, lg, lb)


def _xattn_kernel(x_ref, xb_ref, kv_ref, wq_ref, wo_ref, lg_ref, lb_ref, x2_ref):
    hd = XA_HEADS * XA_DH
    q = (_dot(xb_ref[...], wq_ref[...]) * (XA_DH ** -0.5)).astype(BF16)
    outs = []
    for h in range(XA_HEADS):
        kh = kv_ref[0, :, h * XA_DH:(h + 1) * XA_DH]
        vh = kv_ref[0, :, hd + h * XA_DH:hd + (h + 1) * XA_DH]
        s = _dot_nt(q[:, h * XA_DH:(h + 1) * XA_DH], kh)
        e = jnp.exp(s - jnp.max(s, axis=-1, keepdims=True))
        p = e / jnp.sum(e, axis=-1, keepdims=True)
        outs.append(_dot(p.astype(BF16), vh).astype(BF16))
    o = jnp.concatenate(outs, axis=-1)
    y = _dot(o, wo_ref[...])
    x2_ref[...] = _layer_norm(DN_ALPHA * x_ref[...] + y, lg_ref[...], lb_ref[...])


def _xattn(x1, x1b, kv, wq, wo, lg, lb, B, T, tm):
    n, d = x1.shape
    nt = T // tm
    return pl.pallas_call(
        _xattn_kernel,
        grid=(B, nt),
        in_specs=[
            pl.BlockSpec((tm, d), lambda b, i: (b * nt + i, 0)),
            pl.BlockSpec((tm, d), lambda b, i: (b * nt + i, 0)),
            pl.BlockSpec((1,) + kv.shape[1:], lambda b, i: (b, 0, 0)),
            _const_spec(wq.shape), _const_spec(wo.shape), _const_spec(lg.shape), _const_spec(lb.shape),
        ],
        out_specs=pl.BlockSpec((tm, d), lambda b, i: (b * nt + i, 0)),
        out_shape=jax.ShapeDtypeStruct((n, d), F32),
        compiler_params=_cp(("parallel", "parallel")),
        name="xattn_ln",
    )(x1, x1b, kv, wq, wo, lg, lb)


def _router_kernel(x_ref, wh_ref, wl_ref, rb_ref, e_ref, gate_ref, rank_ref, cnt_ref, carry_ref):
    i = pl.program_id(0)
    tr = x_ref.shape[0]
    E = N_EXPERTS

    @pl.when(i == 0)
    def _():
        carry_ref[...] = jnp.zeros_like(carry_ref)

    x = x_ref[...]
    x_hi = x.astype(BF16)
    x_lo = (x - x_hi.astype(F32)).astype(BF16)
    wh = wh_ref[...]
    logits = _dot_nt(wh, x_hi) + _dot_nt(wh, x_lo) + _dot_nt(wl_ref[...], x_hi)
    biased = logits + rb_ref[...]
    rows = [biased[e:e + 1, :] for e in range(E)]
    raw = [logits[e:e + 1, :] for e in range(E)]
    best_score = None
    best = None
    for gi in range(N_GROUPS):
        v = rows[gi * EXPERTS_PER_GROUP:(gi + 1) * EXPERTS_PER_GROUP]
        sc = None
        for a in range(EXPERTS_PER_GROUP):
            for b in range(a + 1, EXPERTS_PER_GROUP):
                pair = v[a] + v[b]
                sc = pair if sc is None else jnp.maximum(sc, pair)
        if best is None:
            best_score, best = sc, jnp.zeros((1, tr), I32)
        else:
            better = sc > best_score
            best_score = jnp.where(better, sc, best_score)
            best = jnp.where(better, gi, best)

    def pick(vals):
        out = vals[0:EXPERTS_PER_GROUP]
        for gi in range(1, N_GROUPS):
            out = [jnp.where(best == gi, vals[gi * EXPERTS_PER_GROUP + a], out[a]) for a in range(EXPERTS_PER_GROUP)]
        return out

    w = pick(rows)
    lraw = pick(raw)
    i1 = jnp.zeros((1, tr), I32)
    v1 = w[0]
    l1 = lraw[0]
    for a in range(1, EXPERTS_PER_GROUP):
        better = w[a] > v1
        v1 = jnp.where(better, w[a], v1)
        l1 = jnp.where(better, lraw[a], l1)
        i1 = jnp.where(better, a, i1)
    i2 = jnp.full((1, tr), -1, I32)
    v2 = jnp.full((1, tr), -jnp.inf, F32)
    l2 = jnp.zeros((1, tr), F32)
    for a in range(EXPERTS_PER_GROUP):
        better = (i1 != a) & ((w[a] > v2) | (i2 < 0))
        v2 = jnp.where(better, w[a], v2)
        l2 = jnp.where(better, lraw[a], l2)
        i2 = jnp.where(better, a, i2)
    e1 = best * EXPERTS_PER_GROUP + i1
    e2 = best * EXPERTS_PER_GROUP + i2
    mx = jnp.maximum(l1, l2)
    p1 = jnp.exp(l1 - mx)
    p2 = jnp.exp(l2 - mx)
    den = p1 + p2
    e_ref[0:1, :] = e1
    e_ref[1:2, :] = e2
    gate_ref[0:1, :] = p1 / den
    gate_ref[1:2, :] = p2 / den
    eidx = lax.broadcasted_iota(I32, (E, tr), 0)
    is1 = eidx == e1
    is2 = eidx == e2
    member = jnp.where(is1 | is2, 1.0, 0.0)
    uu = lax.broadcasted_iota(I32, (tr, tr), 0)
    tt = lax.broadcasted_iota(I32, (tr, tr), 1)
    tri = jnp.where(uu <= tt, 1.0, 0.0).astype(BF16)
    incl = _dot(member.astype(BF16), tri)
    excl = carry_ref[:, 0:1] + incl - member
    rank_ref[0:1, :] = jnp.sum(jnp.where(is1, excl, 0.0), axis=0, keepdims=True).astype(I32)
    rank_ref[1:2, :] = jnp.sum(jnp.where(is2, excl, 0.0), axis=0, keepdims=True).astype(I32)
    new_carry = carry_ref[...] + jnp.sum(member, axis=1, keepdims=True)
    carry_ref[...] = new_carry
    cnt_ref[...] = new_carry


def _router(x2, rw_hi, rw_lo, rb, tr):
    n, d = x2.shape
    E = N_EXPERTS
    return pl.pallas_call(
        _router_kernel,
        grid=(n // tr,),
        in_specs=[
            pl.BlockSpec((tr, d), lambda i: (i, 0)),
            pl.BlockSpec((E, d), lambda i: (0, 0)),
            pl.BlockSpec((E, d), lambda i: (0, 0)),
            pl.BlockSpec((E, 1), lambda i: (0, 0)),
        ],
        out_specs=[
            pl.BlockSpec((2, tr), lambda i: (0, i)),
            pl.BlockSpec((2, tr), lambda i: (0, i)),
            pl.BlockSpec((2, tr), lambda i: (0, i)),
            pl.BlockSpec((E, LANES), lambda i: (0, 0)),
        ],
        out_shape=[jax.ShapeDtypeStruct((2, n), I32), jax.ShapeDtypeStruct((2, n), F32),
                   jax.ShapeDtypeStruct((2, n), I32), jax.ShapeDtypeStruct((E, LANES), F32)],
        scratch_shapes=[pltpu.VMEM((E, LANES), F32)],
        compiler_params=_cp(("arbitrary",)),
        name="moe_router",
    )(x2, rw_hi, rw_lo, rb)


def _slot_kernel(ps_ref, e_ref, rank_ref, slot_ref):
    e = e_ref[...]
    start = jnp.zeros(e.shape, I32)
    for ex in range(N_EXPERTS):
        start = jnp.where(e == ex, ps_ref[ex], start)
    slot_ref[...] = start + rank_ref[...]


def _slots(pad_start, e, rank, ts):
    n = e.shape[1]
    grid_spec = pltpu.PrefetchScalarGridSpec(
        num_scalar_prefetch=1,
        grid=(n // ts,),
        in_specs=[pl.BlockSpec((TOP_K, ts), lambda i, s: (0, i)), pl.BlockSpec((TOP_K, ts), lambda i, s: (0, i))],
        out_specs=pl.BlockSpec((TOP_K, ts), lambda i, s: (0, i)),
    )
    return pl.pallas_call(
        _slot_kernel,
        grid_spec=grid_spec,
        out_shape=jax.ShapeDtypeStruct((TOP_K, n), I32),
        compiler_params=_cp(("parallel",)),
        name="moe_slots",
    )(pad_start, e, rank)


def _dispatch_kernel(pe_ref, s0_ref, s1_ref, x_ref, buf_hbm, zero_ref, sem):
    td = s0_ref.shape[0]
    slots = (s0_ref, s1_ref)

    @pl.when(pl.program_id(0) == 0)
    def _():
        zero_ref[...] = jnp.zeros_like(zero_ref)

        def zero_copy(ex):
            last = pl.multiple_of(jnp.maximum(pe_ref[ex] - MOE_BLOCK, 0), MOE_BLOCK)
            return pltpu.make_async_copy(zero_ref, buf_hbm.at[pl.ds(last, MOE_BLOCK), :], sem)

        def nonempty(ex):
            return pe_ref[ex] > (pe_ref[ex - 1] if ex > 0 else 0)

        n_blocks = buf_hbm.shape[0] // MOE_BLOCK
        first_unused = pe_ref[N_EXPERTS - 1] // MOE_BLOCK

        def tail_copy(k):
            row = pl.multiple_of((first_unused + k) * MOE_BLOCK, MOE_BLOCK)
            return pltpu.make_async_copy(zero_ref, buf_hbm.at[pl.ds(row, MOE_BLOCK), :], sem)

        for ex in range(N_EXPERTS):
            pl.when(nonempty(ex))(lambda ex=ex: zero_copy(ex).start())
            pl.when(first_unused + ex < n_blocks)(lambda ex=ex: tail_copy(ex).start())
        for ex in range(N_EXPERTS):
            pl.when(nonempty(ex))(lambda ex=ex: zero_copy(ex).wait())
            pl.when(first_unused + ex < n_blocks)(lambda ex=ex: tail_copy(ex).wait())

    def issue(t, carry):
        for kk in range(TOP_K):
            dest = slots[kk][t]
            pltpu.make_async_copy(x_ref.at[pl.ds(t, 1), :], buf_hbm.at[pl.ds(dest, 1), :], sem).start()
        return carry

    lax.fori_loop(0, td, issue, 0, unroll=8)
    for kk in range(TOP_K):
        pltpu.make_async_copy(x_ref, buf_hbm.at[pl.ds(0, td), :], sem).wait()


def _dispatch(pad_end, slot0, slot1, x2, n_rows, td):
    n, d = x2.shape
    grid_spec = pltpu.PrefetchScalarGridSpec(
        num_scalar_prefetch=1,
        grid=(n // td,),
        in_specs=[
            pl.BlockSpec((td,), lambda i, s: (i,), memory_space=pltpu.SMEM),
            pl.BlockSpec((td,), lambda i, s: (i,), memory_space=pltpu.SMEM),
            pl.BlockSpec((td, d), lambda i, s: (i, 0)),
        ],
        out_specs=pl.BlockSpec(memory_space=pl.ANY),
        scratch_shapes=[pltpu.VMEM((MOE_BLOCK, d), F32), pltpu.SemaphoreType.DMA(())],
    )
    return pl.pallas_call(
        _dispatch_kernel,
        grid_spec=grid_spec,
        out_shape=jax.ShapeDtypeStruct((n_rows, d), F32),
        compiler_params=_cp(("arbitrary",)),
        name="moe_dispatch",
    )(pad_end, slot0, slot1, x2)


def _expert_kernel(be_ref, nb_ref, x_ref, win_hbm, wdn_hbm, y_ref, xb_ref, wa_s, wu_s, wd_s, sa, su, sd, sems,
                   *, layer):
    b = pl.program_id(0)
    nf = D_FF // FF_TILE
    n_used = nb_ref[0]
    e = be_ref[b]
    e_prev = be_ref[jnp.maximum(b - 1, 0)]
    e_next = be_ref[jnp.minimum(b + 1, pl.num_programs(0) - 1)]
    active = b < n_used
    is_first = active & ((b == 0) | (e_prev != e))
    feeds_next = active & (b + 1 < n_used) & (e_next != e)

    def tile_copies(ex, f):
        lo = f * FF_TILE
        return (pltpu.make_async_copy(win_hbm.at[layer, ex, :, pl.ds(lo, FF_TILE)], sa, sems.at[0]),
                pltpu.make_async_copy(win_hbm.at[layer, ex, :, pl.ds(D_FF + lo, FF_TILE)], su, sems.at[1]),
                pltpu.make_async_copy(wdn_hbm.at[layer, ex, pl.ds(lo, FF_TILE), :], sd, sems.at[2]))

    def start(ex, f):
        for c in tile_copies(ex, f):
            c.start()

    def finish(ex, f):
        for c in tile_copies(ex, f):
            c.wait()
        wa_s[f] = sa[...].astype(BF16)
        wu_s[f] = su[...].astype(BF16)
        wd_s[f] = sd[...].astype(BF16)

    @pl.when(b == 0)
    def _():
        for f in range(nf - 1):
            start(e, f)
            finish(e, f)
        start(e, nf - 1)

    @pl.when(jnp.logical_not(active))
    def _():
        y_ref[...] = jnp.zeros_like(y_ref)

    @pl.when(active)
    def _():
        xb_ref[...] = x_ref[...].astype(BF16)
        for f in range(nf):
            xb = xb_ref[...]
            a = _dot(xb, wa_s[f])
            u = _dot(xb, wu_s[f])
            act = (a * jax.nn.sigmoid(a) * u).astype(BF16)
            y = _dot(act, wd_s[f])
            if f == 0:
                y_ref[...] = y
                pl.when(is_first)(lambda: finish(e, nf - 1))
            else:
                y_ref[...] += y

            @pl.when(feeds_next)
            def _(f=f):
                if f >= 1:
                    finish(e_next, f - 1)
                start(e_next, f)


def _experts(blk_expert, n_used, buf, w_in, w_down, layer):
    p, d = buf.shape
    nb = p // MOE_BLOCK
    nf = D_FF // FF_TILE
    grid_spec = pltpu.PrefetchScalarGridSpec(
        num_scalar_prefetch=2,
        grid=(nb,),
        in_specs=[
            pl.BlockSpec((MOE_BLOCK, d), lambda b, be, nu: (jnp.minimum(b, nu[0] - 1), 0)),
            pl.BlockSpec(memory_space=pl.ANY),
            pl.BlockSpec(memory_space=pl.ANY),
        ],
        out_specs=pl.BlockSpec((MOE_BLOCK, d), lambda b, be, nu: (b, 0)),
        scratch_shapes=[
            pltpu.VMEM((MOE_BLOCK, d), BF16),
            pltpu.VMEM((nf, d, FF_TILE), BF16),
            pltpu.VMEM((nf, d, FF_TILE), BF16),
            pltpu.VMEM((nf, FF_TILE, d), BF16),
            pltpu.VMEM((d, FF_TILE), F32),
            pltpu.VMEM((d, FF_TILE), F32),
            pltpu.VMEM((FF_TILE, d), F32),
            pltpu.SemaphoreType.DMA((3,)),
        ],
    )
    return pl.pallas_call(
        functools.partial(_expert_kernel, layer=layer),
        grid_spec=grid_spec,
        out_shape=jax.ShapeDtypeStruct((p, d), F32),
        compiler_params=_cp(("arbitrary",)),
        name="moe_experts",
    )(blk_expert, n_used, buf, w_in, w_down)


def _combine_kernel(s0_ref, s1_ref, n0_ref, n1_ref, y_hbm, x_ref, gate_ref, lg_ref, lb_ref, x3_ref, x3b_ref,
                    y0_ref, y1_ref, sems):
    i = pl.program_id(0)
    tc = x_ref.shape[0]
    bufs = (y0_ref, y1_ref)
    cur = i % 2

    def issue_tile(slot_refs, half):
        def issue(t, carry):
            for kk in range(TOP_K):
                src = slot_refs[kk][t]
                pltpu.make_async_copy(y_hbm.at[pl.ds(src, 1), :], bufs[kk].at[half, pl.ds(t, 1), :],
                                      sems.at[half]).start()
            return carry

        lax.fori_loop(0, tc, issue, 0, unroll=8)

    @pl.when(i == 0)
    def _():
        issue_tile((s0_ref, s1_ref), 0)

    @pl.when(i + 1 < pl.num_programs(0))
    def _():
        issue_tile((n0_ref, n1_ref), 1 - cur)

    for kk in range(TOP_K):
        pltpu.make_async_copy(y_hbm.at[pl.ds(0, tc), :], bufs[kk].at[cur], sems.at[cur]).wait()
    gate = gate_ref[...]
    z = DN_ALPHA * x_ref[...] + gate[:, 0:1] * y0_ref[cur] + gate[:, 1:2] * y1_ref[cur]
    x3 = _layer_norm(z, lg_ref[...], lb_ref[...])
    x3_ref[...] = x3
    x3b_ref[...] = x3.astype(BF16)


def _combine(slot0, slot1, y, x2, gate_nt, lg, lb, tc):
    n, d = x2.shape
    last = n // tc - 1
    return pl.pallas_call(
        _combine_kernel,
        grid=(n // tc,),
        in_specs=[
            pl.BlockSpec((tc,), lambda i: (i,), memory_space=pltpu.SMEM),
            pl.BlockSpec((tc,), lambda i: (i,), memory_space=pltpu.SMEM),
            pl.BlockSpec((tc,), lambda i: (jnp.minimum(i + 1, last),), memory_space=pltpu.SMEM),
            pl.BlockSpec((tc,), lambda i: (jnp.minimum(i + 1, last),), memory_space=pltpu.SMEM),
            pl.BlockSpec(memory_space=pl.ANY),
            pl.BlockSpec((tc, d), lambda i: (i, 0)),
            pl.BlockSpec((tc, 2), lambda i: (i, 0)),
            pl.BlockSpec((1, d), lambda i: (0, 0)),
            pl.BlockSpec((1, d), lambda i: (0, 0)),
        ],
        out_specs=[pl.BlockSpec((tc, d), lambda i: (i, 0)), pl.BlockSpec((tc, d), lambda i: (i, 0))],
        out_shape=[jax.ShapeDtypeStruct((n, d), F32), jax.ShapeDtypeStruct((n, d), BF16)],
        scratch_shapes=[pltpu.VMEM((2, tc, d), F32), pltpu.VMEM((2, tc, d), F32), pltpu.SemaphoreType.DMA((2,))],
        compiler_params=_cp(("arbitrary",)),
        name="moe_combine_ln",
    )(slot0, slot1, slot0, slot1, y, x2, gate_nt, lg, lb)


def _layer(x, xb, mem_b, p, moe_w, layer, consts, B, T):
    n, d = x.shape
    G, HPG, DH = NSA_GROUPS, NSA_HPG, NSA_DH
    slopes, ovt, epad, tile_ind = consts

    seqs = TILES.seqs if B % TILES.seqs == 0 else 1
    h_big = _matmul(xb, p["w_big"], BF16, TILES.proj_rows, TILES.proj_cols)
    h_small = _matmul(xb, p["w_small"], F32, TILES.proj_rows, SCOL_END)
    h_t = _matmul_t(xb, p["w_t"], B, T, TILES.proj_rows, TROW_END // 2)

    o_gla = _gla(h_big, h_small, p["wa_pad"], p["b_a"], p["norm_g"], B, T, seqs, TILES.gla_rows)

    kcmp, kcmp_t = _compress(h_small, p["cmp_w1bd"], p["cmp_w2bd"], p["cmp_w2bdt"], p["cmp_pe_pair"], B, T)
    tq_sel = TILES.sel_q
    ocmp_t, mb, in_tile = _cmp_select(slopes, h_t, kcmp, kcmp_t, ovt, tile_ind, B, T, TILES.cmp_q, seqs)
    nq = T // tq_sel
    tile_flags = (in_tile.reshape(B, G, nq, nq, tq_sel).max(axis=-1) > 0).astype(I32)
    tile_flags = tile_flags.transpose(0, 1, 3, 2).reshape(-1)
    gates_t = h_small[:, GLA_GATE_RANK:GLA_GATE_RANK + 3 * NSA_HEADS].reshape(B, T, G, 3 * HPG)
    gates_t = jnp.pad(gates_t.transpose(0, 2, 3, 1), ((0, 0), (0, 0), (0, GATE_ROWS - 3 * HPG), (0, 0)))
    o_nsa = _sel_win(slopes, tile_flags, h_t, h_big, epad, mb, ocmp_t, gates_t, B, T, tq_sel, seqs)

    x1, x1b = _mix(o_gla, o_nsa, h_big, x, p["w_bg"], p["w_bn"], p["w_out"], p["ln_mix_g"], p["ln_mix_b"],
                   TILES.mix_rows)

    kvm = _matmul(mem_b, p["xa_wkv"], BF16, TILES.kv_rows, TILES.kv_cols).reshape(B, MEM_LEN, 2 * XA_HEADS * XA_DH)
    x2 = _xattn(x1, x1b, kvm, p["xa_wq"], p["xa_wo"], p["ln_xa_g"], p["ln_xa_b"], B, T, TILES.xattn_rows)

    e, gate, rank, cnt = _router(x2, p["rw_hi"], p["rw_lo"], p["rb"], TILES.router_rows)
    counts = cnt[:, 0].astype(I32)
    padded = (counts + MOE_BLOCK - 1) // MOE_BLOCK * MOE_BLOCK
    pad_end = jnp.cumsum(padded)
    pad_start = (pad_end - padded).astype(I32)
    nb = (n * TOP_K) // MOE_BLOCK + N_EXPERTS
    n_used = (pad_end[-1] // MOE_BLOCK).astype(I32).reshape(1)
    blk_start = jnp.arange(nb, dtype=I32) * MOE_BLOCK
    blk_expert = jnp.minimum(jnp.sum(blk_start[:, None] >= pad_end[None, :], axis=1), N_EXPERTS - 1).astype(I32)
    blk_expert = jnp.where(jnp.arange(nb) < n_used[0], blk_expert, blk_expert[jnp.maximum(n_used[0] - 1, 0)])
    slot = _slots(pad_start, e, rank, TILES.slot_cols)
    buf = _dispatch(pad_end.astype(I32), slot[0], slot[1], x2, nb * MOE_BLOCK, TILES.dispatch_rows)
    y = _experts(blk_expert, n_used, buf, moe_w[0], moe_w[1], layer)
    x3, x3b = _combine(slot[0], slot[1], y, x2, gate.T, p["ln_ffn_g"], p["ln_ffn_b"],
                       TILES.combine_rows)
    return x3, x3b


def _prep_layer(l, w_in, gla_w_a2, gla_b_a, gla_norm_g, nsa_cmp_pe, nsa_cmp_w1, nsa_cmp_w2, w_branch_gla,
                w_branch_nsa, w_out, ln_mix_g, ln_mix_b, xa_wq, xa_wkv, xa_wo, ln_xa_g, ln_xa_b, router_w,
                router_b, moe_w_in, moe_w_down, ln_ffn_g, ln_ffn_b):
    d = w_in.shape[1]
    w = w_in[l]
    o_gq, o_gk, o_gv, o_gr = 0, GLA_QK, 2 * GLA_QK, 2 * GLA_QK + GLA_V
    o_ga = o_gr + GLA_V
    o_nq = o_ga + GLA_GATE_RANK
    o_nkv = o_nq + NSA_Q
    o_ng = o_nkv + 6 * NSA_KV
    o_mg = o_ng + 3 * NSA_HEADS
    G, DH = NSA_GROUPS, NSA_DH

    def kv_cols(kind):
        return w[:, o_nkv + kind * NSA_KV:o_nkv + (kind + 1) * NSA_KV]

    def slabs(wk):
        return jnp.pad(wk.reshape(d, G, DH), ((0, 0), (0, 0), (0, LANES - DH))).reshape(d, G * LANES)

    w_big = jnp.concatenate([w[:, o_mg:o_mg + 2 * d], w[:, o_gq:o_ga], slabs(kv_cols(2)), slabs(kv_cols(4))],
                            axis=1).astype(BF16)
    w_small = jnp.concatenate([w[:, o_ga:o_nq], w[:, o_ng:o_mg],
                               jnp.zeros((d, LANES - GLA_GATE_RANK - 3 * NSA_HEADS), F32),
                               kv_cols(0), kv_cols(1)], axis=1).astype(BF16)
    w_t = jnp.concatenate([w[:, o_nq:o_nkv], kv_cols(3), kv_cols(5)], axis=1).T.astype(BF16)
    w1 = nsa_cmp_w1[l].reshape(2, CMP_LEN, DH, CMP_HIDDEN)
    z1 = jnp.zeros_like(w1)
    w1bd = jnp.concatenate([jnp.concatenate([w1, z1], axis=3), jnp.concatenate([z1, w1], axis=3)], axis=2)
    w2 = nsa_cmp_w2[l]
    z2 = jnp.zeros_like(w2)
    w2bd = jnp.concatenate([jnp.concatenate([w2, z2], axis=2), jnp.concatenate([z2, w2], axis=2)], axis=1)
    pe = nsa_cmp_pe[l]
    pe_pair = jnp.broadcast_to(jnp.concatenate([pe, pe], axis=-1)[:, :, None, :], (2, CMP_LEN, 16, 2 * DH))
    wa_pad = jnp.concatenate([gla_w_a2[l], jnp.zeros((LANES - GLA_GATE_RANK, GLA_QK), F32)], axis=0).astype(BF16)
    rw_t = router_w.T
    rw_hi = rw_t.astype(BF16)
    rw_lo = (rw_t - rw_hi.astype(F32)).astype(BF16)
    return dict(
        w_big=w_big, w_small=w_small, w_t=w_t, wa_pad=wa_pad,
        b_a=gla_b_a[l].reshape(1, -1), norm_g=gla_norm_g[l].reshape(1, -1),
        cmp_w1bd=w1bd.astype(BF16), cmp_w2bd=w2bd.astype(BF16), cmp_w2bdt=w2bd.transpose(0, 2, 1).astype(BF16),
        cmp_pe_pair=pe_pair.astype(BF16),
        w_bg=w_branch_gla[l].astype(BF16), w_bn=w_branch_nsa[l].astype(BF16), w_out=w_out[l].astype(BF16),
        ln_mix_g=ln_mix_g[l].reshape(1, -1), ln_mix_b=ln_mix_b[l].reshape(1, -1),
        xa_wq=xa_wq[l].astype(BF16), xa_wkv=xa_wkv[l].astype(BF16), xa_wo=xa_wo[l].astype(BF16),
        ln_xa_g=ln_xa_g[l].reshape(1, -1), ln_xa_b=ln_xa_b[l].reshape(1, -1),
        rw_hi=rw_hi, rw_lo=rw_lo, rb=router_b.reshape(-1, 1),
        ln_ffn_g=ln_ffn_g[l].reshape(1, -1), ln_ffn_b=ln_ffn_b[l].reshape(1, -1),
    )


def kernel(x, mem, w_in, gla_w_a2, gla_b_a, gla_norm_g, nsa_cmp_pe, nsa_cmp_w1, nsa_cmp_w2, w_branch_gla, w_branch_nsa, w_out, ln_mix_g, ln_mix_b, xa_wq, xa_wkv, xa_wo, ln_xa_g, ln_xa_b, router_w, router_b, moe_w_in, moe_w_down, ln_ffn_g, ln_ffn_b):
    B, T, d = x.shape
    n = B * T
    assert d == 2048 and mem.shape[1] == MEM_LEN
    assert T % max(TILES.cmp_q, TILES.gla_rows, TILES.sel_q, TILES.xattn_rows) == 0 and WINDOW % TILES.sel_q == 0
    assert n % max(TILES.proj_rows, TILES.slot_cols) == 0
    params = (w_in, gla_w_a2, gla_b_a, gla_norm_g, nsa_cmp_pe, nsa_cmp_w1, nsa_cmp_w2, w_branch_gla, w_branch_nsa,
              w_out, ln_mix_g, ln_mix_b, xa_wq, xa_wkv, xa_wo, ln_xa_g, ln_xa_b, router_w, router_b, moe_w_in,
              moe_w_down, ln_ffn_g, ln_ffn_b)
    slopes = (2.0 ** (-8.0 * jnp.arange(1, NSA_HEADS + 1, dtype=F32) / NSA_HEADS)).astype(F32)
    nc, ns = T // CMP_STRIDE, T // SEL_LEN
    cs = np.arange(nc) * CMP_STRIDE
    ss = np.arange(ns) * SEL_LEN
    ovt = ((cs[None, :] < ss[:, None] + SEL_LEN) & (cs[None, :] + CMP_LEN > ss[:, None])
           & (cs[None, :] + CMP_LEN <= T)).astype(np.float32)
    assert NSA_DH + ns <= LANES
    epad = np.zeros((T, LANES), np.float32)
    epad[np.arange(T), NSA_DH + np.arange(T) // SEL_LEN] = 1.0
    tile_ind = (np.arange(ns)[None, :] // (TILES.sel_q // SEL_LEN) == np.arange(T // TILES.sel_q)[:, None]).astype(np.float32)
    consts = (slopes, jnp.asarray(ovt, BF16), jnp.asarray(epad, BF16), jnp.asarray(tile_ind, BF16))

    xf = x.reshape(n, d)
    xb = xf.astype(BF16)
    mem_b = mem.reshape(B * MEM_LEN, d).astype(BF16)
    moe_w = (moe_w_in, moe_w_down)
    for l in range(DEPTH):
        p = _prep_layer(l, *params)
        xf, xb = _layer(xf, xb, mem_b, p, moe_w, l, consts, B, T)
    return xf.reshape(B, T, d)
```
